```python
import math
import jax
import jax.numpy as jnp
from jax import lax
import numpy as np

D_MODEL = 1024
BATCH = 8
SEQ = 4096
DEPTH = 4

N_MIXERS = 3
Q_BLOCK = 128
NORM_EPS = 1e-6
FFN_HIDDEN = -(-(8 * D_MODEL) // (3 * 256)) * 256

DN_HEAD_DIM = 128
DN_HEADS = D_MODEL // DN_HEAD_DIM
DN_WIDTH = DN_HEADS * DN_HEAD_DIM
DN_CONV = 4
DN_CHUNK = 64

SB_HEAD_DIM = 128
SB_HEADS = D_MODEL // SB_HEAD_DIM

MLA_HEADS = D_MODEL // 128
MLA_NOPE = 128
MLA_ROPE = 64
MLA_QK = MLA_NOPE + MLA_ROPE
MLA_V = 128
MLA_Q_RANK = 256
MLA_KV_RANK = 128
ROPE_THETA = 10000.0

kernel_name = 'hybrid_deltanet_stickbreak_mla_trunk'


def _rmsnorm(x, g):
    xf = x.astype(jnp.float32)
    y = xf * lax.rsqrt(jnp.mean(xf * xf, axis=-1, keepdims=True) + NORM_EPS)
    return (y * g.astype(jnp.float32)).astype(x.dtype)


def _l2norm(x):
    return x * lax.rsqrt(jnp.sum(x * x, axis=-1, keepdims=True) + NORM_EPS)


def _heads_first(x):
    return x.transpose(0, 2, 1, 3)


def _swiglu(h, w_gate_up, w_down):
    gate, up = jnp.split(h @ w_gate_up, 2, axis=-1)
    return (jax.nn.silu(gate) * up) @ w_down


def _causal_dwconv(u, w):
    k_width, t = w.shape[0], u.shape[1]
    up = jnp.pad(u, ((0, 0), (k_width - 1, 0), (0, 0)))
    y = up[:, 0:t] * w[0]
    for j in range(1, k_width):
        y = y + up[:, j:j + t] * w[j]
    return y


def _to_chunks(x):
    b, t, h = x.shape[:3]
    return jnp.moveaxis(x.reshape(b, t // DN_CHUNK, DN_CHUNK, h, *x.shape[3:]), 3, 1)


def _chunk_gated_delta_rule(q, k, v, g, beta):
    b, t, h, dk = q.shape
    dv = v.shape[-1]
    qc, kc, vc = _to_chunks(q), _to_chunks(k), _to_chunks(v)
    gc = jnp.cumsum(_to_chunks(g), axis=-1)
    bc = _to_chunks(beta)[..., None]
    idx = jnp.arange(DN_CHUNK)
    causal = idx[:, None] >= idx[None, :]
    strict = idx[:, None] > idx[None, :]
    diff = gc[..., :, None] - gc[..., None, :]
    decay = jnp.where(causal, jnp.exp(jnp.where(causal, diff, 0.0)), 0.0)
    k_beta = kc * bc
    lower = jnp.where(strict, jnp.einsum('bhncd,bhnsd->bhncs', k_beta, kc) * decay, 0.0)
    tmat = lower + jnp.eye(DN_CHUNK, dtype=lower.dtype)
    u = lax.linalg.triangular_solve(tmat, vc * bc, left_side=True, lower=True, unit_diagonal=True)
    w = lax.linalg.triangular_solve(tmat, k_beta * jnp.exp(gc)[..., None], left_side=True, lower=True, unit_diagonal=True)
    attn = jnp.einsum('bhncd,bhnsd->bhncs', qc, kc) * decay
    q_dec = qc * jnp.exp(gc)[..., None]
    k_dec = kc * jnp.exp(gc[..., -1:] - gc)[..., None]
    chunk_decay = jnp.exp(gc[..., -1])[..., None, None]

    def step(state, inp):
        u_i, w_i, q_i, k_i, a_i, d_i = inp
        v_new = u_i - jnp.einsum('bhcd,bhde->bhce', w_i, state)
        o_i = jnp.einsum('bhcd,bhde->bhce', q_i, state) + jnp.einsum('bhcs,bhse->bhce', a_i, v_new)
        state = state * d_i + jnp.einsum('bhcd,bhce->bhde', k_i, v_new)
        return state, o_i

    xs = tuple(jnp.moveaxis(a, 2, 0) for a in (u, w, q_dec, k_dec, attn, chunk_decay))
    state0 = jnp.zeros((b, h, dk, dv), q.dtype)
    _, o = lax.scan(step, state0, xs)
    return jnp.moveaxis(o, 0, 2).reshape(b, h, t, dv).transpose(0, 2, 1, 3)


def _gated_deltanet_mixer(h, w_in, conv_w, a_log, dt_bias, out_norm, w_out):
    b, t, _ = h.shape
    proj = h @ w_in
    qkv = jax.nn.silu(_causal_dwconv(proj[..., :3 * DN_WIDTH], conv_w))
    z = proj[..., 3 * DN_WIDTH:4 * DN_WIDTH].reshape(b, t, DN_HEADS, DN_HEAD_DIM).astype(jnp.float32)
    a = proj[..., 4 * DN_WIDTH:4 * DN_WIDTH + DN_HEADS].astype(jnp.float32)
    bl = proj[..., 4 * DN_WIDTH + DN_HEADS:].astype(jnp.float32)
    q, k, v = (y.reshape(b, t, DN_HEADS, DN_HEAD_DIM).astype(jnp.float32) for y in jnp.split(qkv, 3, axis=-1))
    q = _l2norm(q) * (DN_HEAD_DIM ** -0.5)
    k = _l2norm(k)
    g = -jnp.exp(a_log.astype(jnp.float32)) * jax.nn.softplus(a + dt_bias.astype(jnp.float32))
    beta = jax.nn.sigmoid(bl)
    o = _chunk_gated_delta_rule(q, k, v, g, beta)
    o = _rmsnorm(o, out_norm) * jax.nn.silu(z)
    return o.reshape(b, t, DN_WIDTH).astype(h.dtype) @ w_out


def _stick_breaking_weights(z, tpos, spos):
    past = spos < tpos
    log_stay = jnp.where(past, jax.nn.log_sigmoid(-z), 0.0)
    log_after = lax.cumsum(log_stay, axis=z.ndim - 1, reverse=True) - log_stay
    return jnp.where(past, jnp.exp(jax.nn.log_sigmoid(z) + log_after), 0.0)


def _softmax_weights(z, tpos, spos):
    return jax.nn.softmax(jnp.where(spos <= tpos, z, -jnp.inf), axis=-1)


def _causal_block_sweep(q, k, v, weight_fn):
    t = q.shape[2]
    local = jnp.arange(Q_BLOCK)
    outs = []
    for blk in range(t // Q_BLOCK):
        start, end = blk * Q_BLOCK, (blk + 1) * Q_BLOCK
        z = jnp.einsum('bhqd,bhkd->bhqk', q[:, :, start:end], k[:, :, :end])
        tpos = (start + local)[:, None]
        spos = jnp.arange(end)[None, :]
        wts = weight_fn(z, tpos, spos)
        outs.append(jnp.einsum('bhqk,bhkd->bhqd', wts, v[:, :, :end]))
    return jnp.concatenate(outs, axis=2)


def _stick_breaking_mixer(h, w_qkv, q_norm, k_norm, w_out):
    b, t, _ = h.shape
    q, k, v = (y.reshape(b, t, SB_HEADS, SB_HEAD_DIM) for y in jnp.split(h @ w_qkv, 3, axis=-1))
    q = _rmsnorm(q, q_norm).astype(jnp.float32) * (SB_HEAD_DIM ** -0.5)
    k = _rmsnorm(k, k_norm).astype(jnp.float32)
    o = _causal_block_sweep(_heads_first(q), _heads_first(k), _heads_first(v.astype(jnp.float32)), _stick_breaking_weights)
    return _heads_first(o).reshape(b, t, SB_HEADS * SB_HEAD_DIM).astype(h.dtype) @ w_out


def _rope_tables(t):
    inv_freq = ROPE_THETA ** (-jnp.arange(0, MLA_ROPE, 2, dtype=jnp.float32) / MLA_ROPE)
    ang = jnp.arange(t, dtype=jnp.float32)[:, None] * inv_freq[None, :]
    return jnp.cos(ang), jnp.sin(ang)


def _apply_rope(x, cos, sin):
    x1, x2 = jnp.split(x.astype(jnp.float32), 2, axis=-1)
    c, s = cos[None, :, None, :], sin[None, :, None, :]
    return jnp.concatenate([x1 * c - x2 * s, x2 * c + x1 * s], axis=-1)


def _mla_mixer(h, w_down, q_a_norm, kv_a_norm, w_uq, w_ukv, q_nope_norm, q_rope_norm, k_nope_norm, k_rope_norm, w_out):
    b, t, _ = h.shape
    c_q, c_kv, k_rope = jnp.split(h @ w_down, [MLA_Q_RANK, MLA_Q_RANK + MLA_KV_RANK], axis=-1)
    q = (_rmsnorm(c_q, q_a_norm) @ w_uq).reshape(b, t, MLA_HEADS, MLA_QK)
    kv = (_rmsnorm(c_kv, kv_a_norm) @ w_ukv).reshape(b, t, MLA_HEADS, MLA_NOPE + MLA_V)
    v = kv[..., MLA_NOPE:]
    cos, sin = _rope_tables(t)
    q_nope = _rmsnorm(q[..., :MLA_NOPE], q_nope_norm).astype(jnp.float32)
    q_rot = _apply_rope(_rmsnorm(q[..., MLA_NOPE:], q_rope_norm), cos, sin)
    q = jnp.concatenate([q_nope, q_rot], axis=-1) * (MLA_QK ** -0.5)
    k_nope = _rmsnorm(kv[..., :MLA_NOPE], k_nope_norm).astype(jnp.float32)
    k_rot = _apply_rope(_rmsnorm(k_rope, k_rope_norm)[:, :, None, :], cos, sin)
    k = jnp.concatenate([k_nope, jnp.broadcast_to(k_rot, (b, t, MLA_HEADS, MLA_ROPE))], axis=-1)
    o = _causal_block_sweep(_heads_first(q), _heads_first(k), _heads_first(v.astype(jnp.float32)), _softmax_weights)
    return _heads_first(o).reshape(b, t, MLA_HEADS * MLA_V).astype(h.dtype) @ w_out


def _dense(key, fan_in, fan_out):
    return jax.random.normal(key, (fan_in, fan_out), jnp.float32) * (fan_in ** -0.5)


def _gain(key, n):
    return 1.0 + 0.02 * jax.random.normal(key, (n,), jnp.float32)


def _deltanet_params(key, p):
    ks = jax.random.split(key, 6)
    dt = jnp.exp(jax.random.uniform(ks[3], (DN_HEADS,), jnp.float32, math.log(1e-3), math.log(1e-1)))
    return {
        p + 'dn_w_in': _dense(ks[0], D_MODEL, 4 * DN_WIDTH + 2 * DN_HEADS),
        p + 'dn_conv_w': jax.random.normal(ks[1], (DN_CONV, 3 * DN_WIDTH), jnp.float32) * (DN_CONV ** -0.5),
        p + 'dn_a_log': jnp.log(jax.random.uniform(ks[2], (DN_HEADS,), jnp.float32, 1.0, 16.0)),
        p + 'dn_dt_bias': dt + jnp.log(-jnp.expm1(-dt)),
        p + 'dn_out_norm': _gain(ks[4], DN_HEAD_DIM),
        p + 'dn_w_out': _dense(ks[5], DN_WIDTH, D_MODEL),
    }


def _stick_breaking_params(key, p):
    ks = jax.random.split(key, 4)
    return {
        p + 'sb_w_qkv': _dense(ks[0], D_MODEL, 3 * SB_HEADS * SB_HEAD_DIM),
        p + 'sb_q_norm': _gain(ks[1], SB_HEAD_DIM),
        p + 'sb_k_norm': _gain(ks[2], SB_HEAD_DIM),
        p + 'sb_w_out': _dense(ks[3], SB_HEADS * SB_HEAD_DIM, D_MODEL),
    }


def _mla_params(key, p):
    ks = jax.random.split(key, 10)
    return {
        p + 'mla_w_down': _dense(ks[0], D_MODEL, MLA_Q_RANK + MLA_KV_RANK + MLA_ROPE),
        p + 'mla_q_a_norm': _gain(ks[1], MLA_Q_RANK),
        p + 'mla_kv_a_norm': _gain(ks[2], MLA_KV_RANK),
        p + 'mla_w_uq': _dense(ks[3], MLA_Q_RANK, MLA_HEADS * MLA_QK),
        p + 'mla_w_ukv': _dense(ks[4], MLA_KV_RANK, MLA_HEADS * (MLA_NOPE + MLA_V)),
        p + 'mla_q_nope_norm': _gain(ks[5], MLA_NOPE),
        p + 'mla_q_rope_norm': _gain(ks[6], MLA_ROPE),
        p + 'mla_k_nope_norm': _gain(ks[7], MLA_NOPE),
        p + 'mla_k_rope_norm': _gain(ks[8], MLA_ROPE),
        p + 'mla_w_out': _dense(ks[9], MLA_HEADS * MLA_V, D_MODEL),
    }


def _fwd_setup_inputs(seed: int = 0) -> dict:
    key = jax.random.key(seed)
    k_x, k_layers = jax.random.split(key)
    inputs = {'x': jax.random.normal(k_x, (BATCH, SEQ, D_MODEL), jnp.float32)}
    builders = (_deltanet_params, _stick_breaking_params, _mla_params)
    layer_keys = jax.random.split(k_layers, DEPTH)
    for i in range(DEPTH):
        ks = jax.random.split(layer_keys[i], 5)
        p = 'l' + str(i) + '_'
        inputs[p + 'mix_norm'] = _gain(ks[0], D_MODEL)
        inputs.update(builders[i % N_MIXERS](ks[1], p))
        inputs[p + 'ffn_norm'] = _gain(ks[2], D_MODEL)
        inputs[p + 'ffn_w_gate_up'] = _dense(ks[3], D_MODEL, 2 * FFN_HIDDEN)
        inputs[p + 'ffn_w_down'] = _dense(ks[4], FFN_HIDDEN, D_MODEL)
    return inputs


def _fwd_reference(x,
              l0_mix_norm, l0_dn_w_in, l0_dn_conv_w, l0_dn_a_log, l0_dn_dt_bias, l0_dn_out_norm, l0_dn_w_out,
              l0_ffn_norm, l0_ffn_w_gate_up, l0_ffn_w_down,
              l1_mix_norm, l1_sb_w_qkv, l1_sb_q_norm, l1_sb_k_norm, l1_sb_w_out,
              l1_ffn_norm, l1_ffn_w_gate_up, l1_ffn_w_down,
              l2_mix_norm, l2_mla_w_down, l2_mla_q_a_norm, l2_mla_kv_a_norm, l2_mla_w_uq, l2_mla_w_ukv,
              l2_mla_q_nope_norm, l2_mla_q_rope_norm, l2_mla_k_nope_norm, l2_mla_k_rope_norm, l2_mla_w_out,
              l2_ffn_norm, l2_ffn_w_gate_up, l2_ffn_w_down,
              l3_mix_norm, l3_dn_w_in, l3_dn_conv_w, l3_dn_a_log, l3_dn_dt_bias, l3_dn_out_norm, l3_dn_w_out,
              l3_ffn_norm, l3_ffn_w_gate_up, l3_ffn_w_down):
    mixers = (_gated_deltanet_mixer, _stick_breaking_mixer, _mla_mixer)
    layers = (
        (l0_mix_norm, (l0_dn_w_in, l0_dn_conv_w, l0_dn_a_log, l0_dn_dt_bias, l0_dn_out_norm, l0_dn_w_out),
         l0_ffn_norm, l0_ffn_w_gate_up, l0_ffn_w_down),
        (l1_mix_norm, (l1_sb_w_qkv, l1_sb_q_norm, l1_sb_k_norm, l1_sb_w_out),
         l1_ffn_norm, l1_ffn_w_gate_up, l1_ffn_w_down),
        (l2_mix_norm, (l2_mla_w_down, l2_mla_q_a_norm, l2_mla_kv_a_norm, l2_mla_w_uq, l2_mla_w_ukv,
                       l2_mla_q_nope_norm, l2_mla_q_rope_norm, l2_mla_k_nope_norm, l2_mla_k_rope_norm, l2_mla_w_out),
         l2_ffn_norm, l2_ffn_w_gate_up, l2_ffn_w_down),
        (l3_mix_norm, (l3_dn_w_in, l3_dn_conv_w, l3_dn_a_log, l3_dn_dt_bias, l3_dn_out_norm, l3_dn_w_out),
         l3_ffn_norm, l3_ffn_w_gate_up, l3_ffn_w_down),
    )
    for i in range(DEPTH):
        mix_norm, mix_params, ffn_norm, w_gate_up, w_down = layers[i]
        x = x + mixers[i % N_MIXERS](_rmsnorm(x, mix_norm), *mix_params)
        x = x + _swiglu(_rmsnorm(x, ffn_norm), w_gate_up, w_down)
    return x


import jax as _jax
import jax.numpy as _jnp

TWIN_FORMAT = 'train_step'
FWD_PARAMS = ['x', 'l0_mix_norm', 'l0_dn_w_in', 'l0_dn_conv_w', 'l0_dn_a_log', 'l0_dn_dt_bias', 'l0_dn_out_norm', 'l0_dn_w_out', 'l0_ffn_norm', 'l0_ffn_w_gate_up', 'l0_ffn_w_down', 'l1_mix_norm', 'l1_sb_w_qkv', 'l1_sb_q_norm', 'l1_sb_k_norm', 'l1_sb_w_out', 'l1_ffn_norm', 'l1_ffn_w_gate_up', 'l1_ffn_w_down', 'l2_mix_norm', 'l2_mla_w_down', 'l2_mla_q_a_norm', 'l2_mla_kv_a_norm', 'l2_mla_w_uq', 'l2_mla_w_ukv', 'l2_mla_q_nope_norm', 'l2_mla_q_rope_norm', 'l2_mla_k_nope_norm', 'l2_mla_k_rope_norm', 'l2_mla_w_out', 'l2_ffn_norm', 'l2_ffn_w_gate_up', 'l2_ffn_w_down', 'l3_mix_norm', 'l3_dn_w_in', 'l3_dn_conv_w', 'l3_dn_a_log', 'l3_dn_dt_bias', 'l3_dn_out_norm', 'l3_dn_w_out', 'l3_ffn_norm', 'l3_ffn_w_gate_up', 'l3_ffn_w_down']
TWIN_WEIGHTS = ['l0_mix_norm', 'l0_dn_w_in', 'l0_dn_conv_w', 'l0_dn_a_log', 'l0_dn_dt_bias', 'l0_dn_out_norm', 'l0_dn_w_out', 'l0_ffn_norm', 'l0_ffn_w_gate_up', 'l0_ffn_w_down', 'l1_mix_norm', 'l1_sb_w_qkv', 'l1_sb_q_norm', 'l1_sb_k_norm', 'l1_sb_w_out', 'l1_ffn_norm', 'l1_ffn_w_gate_up', 'l1_ffn_w_down', 'l2_mix_norm', 'l2_mla_w_down', 'l2_mla_q_a_norm', 'l2_mla_kv_a_norm', 'l2_mla_w_uq', 'l2_mla_w_ukv', 'l2_mla_q_nope_norm', 'l2_mla_q_rope_norm', 'l2_mla_k_nope_norm', 'l2_mla_k_rope_norm', 'l2_mla_w_out', 'l2_ffn_norm', 'l2_ffn_w_gate_up', 'l2_ffn_w_down', 'l3_mix_norm', 'l3_dn_w_in', 'l3_dn_conv_w', 'l3_dn_a_log', 'l3_dn_dt_bias', 'l3_dn_out_norm', 'l3_dn_w_out', 'l3_ffn_norm', 'l3_ffn_w_gate_up', 'l3_ffn_w_down']
TWIN_DIFF_INPUT = 'x'
TWIN_INPUTS = ['x', 'l0_mix_norm', 'l0_dn_w_in', 'l0_dn_conv_w', 'l0_dn_a_log', 'l0_dn_dt_bias', 'l0_dn_out_norm', 'l0_dn_w_out', 'l0_ffn_norm', 'l0_ffn_w_gate_up', 'l0_ffn_w_down', 'l1_mix_norm', 'l1_sb_w_qkv', 'l1_sb_q_norm', 'l1_sb_k_norm', 'l1_sb_w_out', 'l1_ffn_norm', 'l1_ffn_w_gate_up', 'l1_ffn_w_down', 'l2_mix_norm', 'l2_mla_w_down', 'l2_mla_q_a_norm', 'l2_mla_kv_a_norm', 'l2_mla_w_uq', 'l2_mla_w_ukv', 'l2_mla_q_nope_norm', 'l2_mla_q_rope_norm', 'l2_mla_k_nope_norm', 'l2_mla_k_rope_norm', 'l2_mla_w_out', 'l2_ffn_norm', 'l2_ffn_w_gate_up', 'l2_ffn_w_down', 'l3_mix_norm', 'l3_dn_w_in', 'l3_dn_conv_w', 'l3_dn_a_log', 'l3_dn_dt_bias', 'l3_dn_out_norm', 'l3_dn_w_out', 'l3_ffn_norm', 'l3_ffn_w_gate_up', 'l3_ffn_w_down', 'loss_target', 'm_l0_mix_norm', 'm_l0_dn_w_in', 'm_l0_dn_conv_w', 'm_l0_dn_a_log', 'm_l0_dn_dt_bias', 'm_l0_dn_out_norm', 'm_l0_dn_w_out', 'm_l0_ffn_norm', 'm_l0_ffn_w_gate_up', 'm_l0_ffn_w_down', 'm_l1_mix_norm', 'm_l1_sb_w_qkv', 'm_l1_sb_q_norm', 'm_l1_sb_k_norm', 'm_l1_sb_w_out', 'm_l1_ffn_norm', 'm_l1_ffn_w_gate_up', 'm_l1_ffn_w_down', 'm_l2_mix_norm', 'm_l2_mla_w_down', 'm_l2_mla_q_a_norm', 'm_l2_mla_kv_a_norm', 'm_l2_mla_w_uq', 'm_l2_mla_w_ukv', 'm_l2_mla_q_nope_norm', 'm_l2_mla_q_rope_norm', 'm_l2_mla_k_nope_norm', 'm_l2_mla_k_rope_norm', 'm_l2_mla_w_out', 'm_l2_ffn_norm', 'm_l2_ffn_w_gate_up', 'm_l2_ffn_w_down', 'm_l3_mix_norm', 'm_l3_dn_w_in', 'm_l3_dn_conv_w', 'm_l3_dn_a_log', 'm_l3_dn_dt_bias', 'm_l3_dn_out_norm', 'm_l3_dn_w_out', 'm_l3_ffn_norm', 'm_l3_ffn_w_gate_up', 'm_l3_ffn_w_down', 'v_l0_mix_norm', 'v_l0_dn_w_in', 'v_l0_dn_conv_w', 'v_l0_dn_a_log', 'v_l0_dn_dt_bias', 'v_l0_dn_out_norm', 'v_l0_dn_w_out', 'v_l0_ffn_norm', 'v_l0_ffn_w_gate_up', 'v_l0_ffn_w_down', 'v_l1_mix_norm', 'v_l1_sb_w_qkv', 'v_l1_sb_q_norm', 'v_l1_sb_k_norm', 'v_l1_sb_w_out', 'v_l1_ffn_norm', 'v_l1_ffn_w_gate_up', 'v_l1_ffn_w_down', 'v_l2_mix_norm', 'v_l2_mla_w_down', 'v_l2_mla_q_a_norm', 'v_l2_mla_kv_a_norm', 'v_l2_mla_w_uq', 'v_l2_mla_w_ukv', 'v_l2_mla_q_nope_norm', 'v_l2_mla_q_rope_norm', 'v_l2_mla_k_nope_norm', 'v_l2_mla_k_rope_norm', 'v_l2_mla_w_out', 'v_l2_ffn_norm', 'v_l2_ffn_w_gate_up', 'v_l2_ffn_w_down', 'v_l3_mix_norm', 'v_l3_dn_w_in', 'v_l3_dn_conv_w', 'v_l3_dn_a_log', 'v_l3_dn_dt_bias', 'v_l3_dn_out_norm', 'v_l3_dn_w_out', 'v_l3_ffn_norm', 'v_l3_ffn_w_gate_up', 'v_l3_ffn_w_down']
TWIN_OUTPUTS = ['loss', 'grad_x', 'grad_l0_mix_norm', 'grad_l0_dn_w_in', 'grad_l0_dn_conv_w', 'grad_l0_dn_a_log', 'grad_l0_dn_dt_bias', 'grad_l0_dn_out_norm', 'grad_l0_dn_w_out', 'grad_l0_ffn_norm', 'grad_l0_ffn_w_gate_up', 'grad_l0_ffn_w_down', 'grad_l1_mix_norm', 'grad_l1_sb_w_qkv', 'grad_l1_sb_q_norm', 'grad_l1_sb_k_norm', 'grad_l1_sb_w_out', 'grad_l1_ffn_norm', 'grad_l1_ffn_w_gate_up', 'grad_l1_ffn_w_down', 'grad_l2_mix_norm', 'grad_l2_mla_w_down', 'grad_l2_mla_q_a_norm', 'grad_l2_mla_kv_a_norm', 'grad_l2_mla_w_uq', 'grad_l2_mla_w_ukv', 'grad_l2_mla_q_nope_norm', 'grad_l2_mla_q_rope_norm', 'grad_l2_mla_k_nope_norm', 'grad_l2_mla_k_rope_norm', 'grad_l2_mla_w_out', 'grad_l2_ffn_norm', 'grad_l2_ffn_w_gate_up', 'grad_l2_ffn_w_down', 'grad_l3_mix_norm', 'grad_l3_dn_w_in', 'grad_l3_dn_conv_w', 'grad_l3_dn_a_log', 'grad_l3_dn_dt_bias', 'grad_l3_dn_out_norm', 'grad_l3_dn_w_out', 'grad_l3_ffn_norm', 'grad_l3_ffn_w_gate_up', 'grad_l3_ffn_w_down', 'delta_l0_mix_norm', 'delta_l0_dn_w_in', 'delta_l0_dn_conv_w', 'delta_l0_dn_a_log', 'delta_l0_dn_dt_bias', 'delta_l0_dn_out_norm', 'delta_l0_dn_w_out', 'delta_l0_ffn_norm', 'delta_l0_ffn_w_gate_up', 'delta_l0_ffn_w_down', 'delta_l1_mix_norm', 'delta_l1_sb_w_qkv', 'delta_l1_sb_q_norm', 'delta_l1_sb_k_norm', 'delta_l1_sb_w_out', 'delta_l1_ffn_norm', 'delta_l1_ffn_w_gate_up', 'delta_l1_ffn_w_down', 'delta_l2_mix_norm', 'delta_l2_mla_w_down', 'delta_l2_mla_q_a_norm', 'delta_l2_mla_kv_a_norm', 'delta_l2_mla_w_uq', 'delta_l2_mla_w_ukv', 'delta_l2_mla_q_nope_norm', 'delta_l2_mla_q_rope_norm', 'delta_l2_mla_k_nope_norm', 'delta_l2_mla_k_rope_norm', 'delta_l2_mla_w_out', 'delta_l2_ffn_norm', 'delta_l2_ffn_w_gate_up', 'delta_l2_ffn_w_down', 'delta_l3_mix_norm', 'delta_l3_dn_w_in', 'delta_l3_dn_conv_w', 'delta_l3_dn_a_log', 'delta_l3_dn_dt_bias', 'delta_l3_dn_out_norm', 'delta_l3_dn_w_out', 'delta_l3_ffn_norm', 'delta_l3_ffn_w_gate_up', 'delta_l3_ffn_w_down', 'new_m_l0_mix_norm', 'new_m_l0_dn_w_in', 'new_m_l0_dn_conv_w', 'new_m_l0_dn_a_log', 'new_m_l0_dn_dt_bias', 'new_m_l0_dn_out_norm', 'new_m_l0_dn_w_out', 'new_m_l0_ffn_norm', 'new_m_l0_ffn_w_gate_up', 'new_m_l0_ffn_w_down', 'new_m_l1_mix_norm', 'new_m_l1_sb_w_qkv', 'new_m_l1_sb_q_norm', 'new_m_l1_sb_k_norm', 'new_m_l1_sb_w_out', 'new_m_l1_ffn_norm', 'new_m_l1_ffn_w_gate_up', 'new_m_l1_ffn_w_down', 'new_m_l2_mix_norm', 'new_m_l2_mla_w_down', 'new_m_l2_mla_q_a_norm', 'new_m_l2_mla_kv_a_norm', 'new_m_l2_mla_w_uq', 'new_m_l2_mla_w_ukv', 'new_m_l2_mla_q_nope_norm', 'new_m_l2_mla_q_rope_norm', 'new_m_l2_mla_k_nope_norm', 'new_m_l2_mla_k_rope_norm', 'new_m_l2_mla_w_out', 'new_m_l2_ffn_norm', 'new_m_l2_ffn_w_gate_up', 'new_m_l2_ffn_w_down', 'new_m_l3_mix_norm', 'new_m_l3_dn_w_in', 'new_m_l3_dn_conv_w', 'new_m_l3_dn_a_log', 'new_m_l3_dn_dt_bias', 'new_m_l3_dn_out_norm', 'new_m_l3_dn_w_out', 'new_m_l3_ffn_norm', 'new_m_l3_ffn_w_gate_up', 'new_m_l3_ffn_w_down', 'new_v_l0_mix_norm', 'new_v_l0_dn_w_in', 'new_v_l0_dn_conv_w', 'new_v_l0_dn_a_log', 'new_v_l0_dn_dt_bias', 'new_v_l0_dn_out_norm', 'new_v_l0_dn_w_out', 'new_v_l0_ffn_norm', 'new_v_l0_ffn_w_gate_up', 'new_v_l0_ffn_w_down', 'new_v_l1_mix_norm', 'new_v_l1_sb_w_qkv', 'new_v_l1_sb_q_norm', 'new_v_l1_sb_k_norm', 'new_v_l1_sb_w_out', 'new_v_l1_ffn_norm', 'new_v_l1_ffn_w_gate_up', 'new_v_l1_ffn_w_down', 'new_v_l2_mix_norm', 'new_v_l2_mla_w_down', 'new_v_l2_mla_q_a_norm', 'new_v_l2_mla_kv_a_norm', 'new_v_l2_mla_w_uq', 'new_v_l2_mla_w_ukv', 'new_v_l2_mla_q_nope_norm', 'new_v_l2_mla_q_rope_norm', 'new_v_l2_mla_k_nope_norm', 'new_v_l2_mla_k_rope_norm', 'new_v_l2_mla_w_out', 'new_v_l2_ffn_norm', 'new_v_l2_ffn_w_gate_up', 'new_v_l2_ffn_w_down', 'new_v_l3_mix_norm', 'new_v_l3_dn_w_in', 'new_v_l3_dn_conv_w', 'new_v_l3_dn_a_log', 'new_v_l3_dn_dt_bias', 'new_v_l3_dn_out_norm', 'new_v_l3_dn_w_out', 'new_v_l3_ffn_norm', 'new_v_l3_ffn_w_gate_up', 'new_v_l3_ffn_w_down']
TWIN_LEAF_KINDS = {'loss': 'loss', 'grad_x': 'grad_x', 'grad_l0_mix_norm': 'grad_w', 'grad_l0_dn_w_in': 'grad_w', 'grad_l0_dn_conv_w': 'grad_w', 'grad_l0_dn_a_log': 'grad_w', 'grad_l0_dn_dt_bias': 'grad_w', 'grad_l0_dn_out_norm': 'grad_w', 'grad_l0_dn_w_out': 'grad_w', 'grad_l0_ffn_norm': 'grad_w', 'grad_l0_ffn_w_gate_up': 'grad_w', 'grad_l0_ffn_w_down': 'grad_w', 'grad_l1_mix_norm': 'grad_w', 'grad_l1_sb_w_qkv': 'grad_w', 'grad_l1_sb_q_norm': 'grad_w', 'grad_l1_sb_k_norm': 'grad_w', 'grad_l1_sb_w_out': 'grad_w', 'grad_l1_ffn_norm': 'grad_w', 'grad_l1_ffn_w_gate_up': 'grad_w', 'grad_l1_ffn_w_down': 'grad_w', 'grad_l2_mix_norm': 'grad_w', 'grad_l2_mla_w_down': 'grad_w', 'grad_l2_mla_q_a_norm': 'grad_w', 'grad_l2_mla_kv_a_norm': 'grad_w', 'grad_l2_mla_w_uq': 'grad_w', 'grad_l2_mla_w_ukv': 'grad_w', 'grad_l2_mla_q_nope_norm': 'grad_w', 'grad_l2_mla_q_rope_norm': 'grad_w', 'grad_l2_mla_k_nope_norm': 'grad_w', 'grad_l2_mla_k_rope_norm': 'grad_w', 'grad_l2_mla_w_out': 'grad_w', 'grad_l2_ffn_norm': 'grad_w', 'grad_l2_ffn_w_gate_up': 'grad_w', 'grad_l2_ffn_w_down': 'grad_w', 'grad_l3_mix_norm': 'grad_w', 'grad_l3_dn_w_in': 'grad_w', 'grad_l3_dn_conv_w': 'grad_w', 'grad_l3_dn_a_log': 'grad_w', 'grad_l3_dn_dt_bias': 'grad_w', 'grad_l3_dn_out_norm': 'grad_w', 'grad_l3_dn_w_out': 'grad_w', 'grad_l3_ffn_norm': 'grad_w', 'grad_l3_ffn_w_gate_up': 'grad_w', 'grad_l3_ffn_w_down': 'grad_w', 'delta_l0_mix_norm': 'delta_w', 'delta_l0_dn_w_in': 'delta_w', 'delta_l0_dn_conv_w': 'delta_w', 'delta_l0_dn_a_log': 'delta_w', 'delta_l0_dn_dt_bias': 'delta_w', 'delta_l0_dn_out_norm': 'delta_w', 'delta_l0_dn_w_out': 'delta_w', 'delta_l0_ffn_norm': 'delta_w', 'delta_l0_ffn_w_gate_up': 'delta_w', 'delta_l0_ffn_w_down': 'delta_w', 'delta_l1_mix_norm': 'delta_w', 'delta_l1_sb_w_qkv': 'delta_w', 'delta_l1_sb_q_norm': 'delta_w', 'delta_l1_sb_k_norm': 'delta_w', 'delta_l1_sb_w_out': 'delta_w', 'delta_l1_ffn_norm': 'delta_w', 'delta_l1_ffn_w_gate_up': 'delta_w', 'delta_l1_ffn_w_down': 'delta_w', 'delta_l2_mix_norm': 'delta_w', 'delta_l2_mla_w_down': 'delta_w', 'delta_l2_mla_q_a_norm': 'delta_w', 'delta_l2_mla_kv_a_norm': 'delta_w', 'delta_l2_mla_w_uq': 'delta_w', 'delta_l2_mla_w_ukv': 'delta_w', 'delta_l2_mla_q_nope_norm': 'delta_w', 'delta_l2_mla_q_rope_norm': 'delta_w', 'delta_l2_mla_k_nope_norm': 'delta_w', 'delta_l2_mla_k_rope_norm': 'delta_w', 'delta_l2_mla_w_out': 'delta_w', 'delta_l2_ffn_norm': 'delta_w', 'delta_l2_ffn_w_gate_up': 'delta_w', 'delta_l2_ffn_w_down': 'delta_w', 'delta_l3_mix_norm': 'delta_w', 'delta_l3_dn_w_in': 'delta_w', 'delta_l3_dn_conv_w': 'delta_w', 'delta_l3_dn_a_log': 'delta_w', 'delta_l3_dn_dt_bias': 'delta_w', 'delta_l3_dn_out_norm': 'delta_w', 'delta_l3_dn_w_out': 'delta_w', 'delta_l3_ffn_norm': 'delta_w', 'delta_l3_ffn_w_gate_up': 'delta_w', 'delta_l3_ffn_w_down': 'delta_w', 'new_m_l0_mix_norm': 'new_m', 'new_m_l0_dn_w_in': 'new_m', 'new_m_l0_dn_conv_w': 'new_m', 'new_m_l0_dn_a_log': 'new_m', 'new_m_l0_dn_dt_bias': 'new_m', 'new_m_l0_dn_out_norm': 'new_m', 'new_m_l0_dn_w_out': 'new_m', 'new_m_l0_ffn_norm': 'new_m', 'new_m_l0_ffn_w_gate_up': 'new_m', 'new_m_l0_ffn_w_down': 'new_m', 'new_m_l1_mix_norm': 'new_m', 'new_m_l1_sb_w_qkv': 'new_m', 'new_m_l1_sb_q_norm': 'new_m', 'new_m_l1_sb_k_norm': 'new_m', 'new_m_l1_sb_w_out': 'new_m', 'new_m_l1_ffn_norm': 'new_m', 'new_m_l1_ffn_w_gate_up': 'new_m', 'new_m_l1_ffn_w_down': 'new_m', 'new_m_l2_mix_norm': 'new_m', 'new_m_l2_mla_w_down': 'new_m', 'new_m_l2_mla_q_a_norm': 'new_m', 'new_m_l2_mla_kv_a_norm': 'new_m', 'new_m_l2_mla_w_uq': 'new_m', 'new_m_l2_mla_w_ukv': 'new_m', 'new_m_l2_mla_q_nope_norm': 'new_m', 'new_m_l2_mla_q_rope_norm': 'new_m', 'new_m_l2_mla_k_nope_norm': 'new_m', 'new_m_l2_mla_k_rope_norm': 'new_m', 'new_m_l2_mla_w_out': 'new_m', 'new_m_l2_ffn_norm': 'new_m', 'new_m_l2_ffn_w_gate_up': 'new_m', 'new_m_l2_ffn_w_down': 'new_m', 'new_m_l3_mix_norm': 'new_m', 'new_m_l3_dn_w_in': 'new_m', 'new_m_l3_dn_conv_w': 'new_m', 'new_m_l3_dn_a_log': 'new_m', 'new_m_l3_dn_dt_bias': 'new_m', 'new_m_l3_dn_out_norm': 'new_m', 'new_m_l3_dn_w_out': 'new_m', 'new_m_l3_ffn_norm': 'new_m', 'new_m_l3_ffn_w_gate_up': 'new_m', 'new_m_l3_ffn_w_down': 'new_m', 'new_v_l0_mix_norm': 'new_v', 'new_v_l0_dn_w_in': 'new_v', 'new_v_l0_dn_conv_w': 'new_v', 'new_v_l0_dn_a_log': 'new_v', 'new_v_l0_dn_dt_bias': 'new_v', 'new_v_l0_dn_out_norm': 'new_v', 'new_v_l0_dn_w_out': 'new_v', 'new_v_l0_ffn_norm': 'new_v', 'new_v_l0_ffn_w_gate_up': 'new_v', 'new_v_l0_ffn_w_down': 'new_v', 'new_v_l1_mix_norm': 'new_v', 'new_v_l1_sb_w_qkv': 'new_v', 'new_v_l1_sb_q_norm': 'new_v', 'new_v_l1_sb_k_norm': 'new_v', 'new_v_l1_sb_w_out': 'new_v', 'new_v_l1_ffn_norm': 'new_v', 'new_v_l1_ffn_w_gate_up': 'new_v', 'new_v_l1_ffn_w_down': 'new_v', 'new_v_l2_mix_norm': 'new_v', 'new_v_l2_mla_w_down': 'new_v', 'new_v_l2_mla_q_a_norm': 'new_v', 'new_v_l2_mla_kv_a_norm': 'new_v', 'new_v_l2_mla_w_uq': 'new_v', 'new_v_l2_mla_w_ukv': 'new_v', 'new_v_l2_mla_q_nope_norm': 'new_v', 'new_v_l2_mla_q_rope_norm': 'new_v', 'new_v_l2_mla_k_nope_norm': 'new_v', 'new_v_l2_mla_k_rope_norm': 'new_v', 'new_v_l2_mla_w_out': 'new_v', 'new_v_l2_ffn_norm': 'new_v', 'new_v_l2_ffn_w_gate_up': 'new_v', 'new_v_l2_ffn_w_down': 'new_v', 'new_v_l3_mix_norm': 'new_v', 'new_v_l3_dn_w_in': 'new_v', 'new_v_l3_dn_conv_w': 'new_v', 'new_v_l3_dn_a_log': 'new_v', 'new_v_l3_dn_dt_bias': 'new_v', 'new_v_l3_dn_out_norm': 'new_v', 'new_v_l3_dn_w_out': 'new_v', 'new_v_l3_ffn_norm': 'new_v', 'new_v_l3_ffn_w_gate_up': 'new_v', 'new_v_l3_ffn_w_down': 'new_v'}


def _forward(args):
    return _fwd_reference(*[args[k] for k in FWD_PARAMS])


def _output_shape():
    def fwd():
        inp = _fwd_setup_inputs(0)
        return _fwd_reference(*[inp[k] for k in FWD_PARAMS])
    out = _jax.eval_shape(fwd)
    return out.shape, out.dtype

N_MICROBATCH = 1
ADAM_LR = 0.001
ADAM_B1 = 0.9
ADAM_B2 = 0.999
ADAM_EPS = 1e-08
ADAM_WD = 0.01
ADAM_STEP = 10
PER_EXAMPLE_BATCH_AXIS = {'x': 0, 'loss_target': 0}
SHARED_INPUTS = []
_WEIGHT_DTYPES = {'l0_mix_norm': _jnp.float32, 'l0_dn_w_in': _jnp.float32, 'l0_dn_conv_w': _jnp.float32, 'l0_dn_a_log': _jnp.float32, 'l0_dn_dt_bias': _jnp.float32, 'l0_dn_out_norm': _jnp.float32, 'l0_dn_w_out': _jnp.float32, 'l0_ffn_norm': _jnp.float32, 'l0_ffn_w_gate_up': _jnp.float32, 'l0_ffn_w_down': _jnp.float32, 'l1_mix_norm': _jnp.float32, 'l1_sb_w_qkv': _jnp.float32, 'l1_sb_q_norm': _jnp.float32, 'l1_sb_k_norm': _jnp.float32, 'l1_sb_w_out': _jnp.float32, 'l1_ffn_norm': _jnp.float32, 'l1_ffn_w_gate_up': _jnp.float32, 'l1_ffn_w_down': _jnp.float32, 'l2_mix_norm': _jnp.float32, 'l2_mla_w_down': _jnp.float32, 'l2_mla_q_a_norm': _jnp.float32, 'l2_mla_kv_a_norm': _jnp.float32, 'l2_mla_w_uq': _jnp.float32, 'l2_mla_w_ukv': _jnp.float32, 'l2_mla_q_nope_norm': _jnp.float32, 'l2_mla_q_rope_norm': _jnp.float32, 'l2_mla_k_nope_norm': _jnp.float32, 'l2_mla_k_rope_norm': _jnp.float32, 'l2_mla_w_out': _jnp.float32, 'l2_ffn_norm': _jnp.float32, 'l2_ffn_w_gate_up': _jnp.float32, 'l2_ffn_w_down': _jnp.float32, 'l3_mix_norm': _jnp.float32, 'l3_dn_w_in': _jnp.float32, 'l3_dn_conv_w': _jnp.float32, 'l3_dn_a_log': _jnp.float32, 'l3_dn_dt_bias': _jnp.float32, 'l3_dn_out_norm': _jnp.float32, 'l3_dn_w_out': _jnp.float32, 'l3_ffn_norm': _jnp.float32, 'l3_ffn_w_gate_up': _jnp.float32, 'l3_ffn_w_down': _jnp.float32}
MOMENT_SCALE = {'l0_mix_norm': 1.405619e+01, 'l0_dn_w_in': 6.607940e-01, 'l0_dn_conv_w': 1.177468e+00, 'l0_dn_a_log': 4.166327e+01, 'l0_dn_dt_bias': 3.983746e+01, 'l0_dn_out_norm': 9.493708e+01, 'l0_dn_w_out': 2.747607e+00, 'l0_ffn_norm': 2.457922e+01, 'l0_ffn_w_gate_up': 4.705783e-01, 'l0_ffn_w_down': 7.706498e-01, 'l1_mix_norm': 1.373478e+01, 'l1_sb_w_qkv': 9.678704e-01, 'l1_sb_q_norm': 1.563548e+01, 'l1_sb_k_norm': 1.566737e+01, 'l1_sb_w_out': 1.613665e+00, 'l1_ffn_norm': 2.457994e+01, 'l1_ffn_w_gate_up': 3.914061e-01, 'l1_ffn_w_down': 6.156264e-01, 'l2_mix_norm': 1.596444e+00, 'l2_mla_w_down': 2.136464e+00, 'l2_mla_q_a_norm': 2.696547e-01, 'l2_mla_kv_a_norm': 5.085783e+00, 'l2_mla_w_uq': 1.130588e-01, 'l2_mla_w_ukv': 7.811232e-01, 'l2_mla_q_nope_norm': 1.630522e+00, 'l2_mla_q_rope_norm': 1.136293e+00, 'l2_mla_k_nope_norm': 1.634728e+00, 'l2_mla_k_rope_norm': 1.129266e+00, 'l2_mla_w_out': 1.150153e+00, 'l2_ffn_norm': 2.453291e+01, 'l2_ffn_w_gate_up': 3.842608e-01, 'l2_ffn_w_down': 5.511669e-01, 'l3_mix_norm': 1.357846e+01, 'l3_dn_w_in': 8.366729e-01, 'l3_dn_conv_w': 8.910643e-01, 'l3_dn_a_log': 2.860323e+01, 'l3_dn_dt_bias': 2.759915e+01, 'l3_dn_out_norm': 9.456297e+01, 'l3_dn_w_out': 1.346016e+00, 'l3_ffn_norm': 2.491266e+01, 'l3_ffn_w_gate_up': 3.796805e-01, 'l3_ffn_w_down': 5.013836e-01}


def _to_microbatches(a, axis):
    t = _jnp.moveaxis(a, axis, 0)
    t = t.reshape((N_MICROBATCH, t.shape[0] // N_MICROBATCH) + t.shape[1:])
    return _jnp.moveaxis(t, 1, axis + 1)


def setup_inputs(seed: int = 0) -> dict:
    inp = _fwd_setup_inputs(seed)
    key = _jax.random.fold_in(_jax.random.key(seed), 7919)
    shape, _ = _output_shape()
    out = dict(inp)
    out["loss_target"] = _jax.random.normal(_jax.random.fold_in(key, 0), shape, _jnp.float32)
    for i, name in enumerate(TWIN_WEIGHTS):
        w = inp[name].astype(_jnp.float32)
        if MOMENT_SCALE is None:
            s = _jnp.sqrt(_jnp.mean(_jnp.square(w)) + 1e-30)
        else:
            s = MOMENT_SCALE[name]
        km, kv = _jax.random.split(_jax.random.fold_in(key, i + 1))
        out[name] = w
        out["m_" + name] = s * _jax.random.normal(km, w.shape, _jnp.float32)
        out["v_" + name] = (s * s) * _jax.random.uniform(kv, w.shape, _jnp.float32, 0.5, 1.5)
    if N_MICROBATCH > 1:
        for name, axis in PER_EXAMPLE_BATCH_AXIS.items():
            out[name] = _to_microbatches(out[name], axis)
    return {'x': out['x'], 'l0_mix_norm': out['l0_mix_norm'], 'l0_dn_w_in': out['l0_dn_w_in'], 'l0_dn_conv_w': out['l0_dn_conv_w'], 'l0_dn_a_log': out['l0_dn_a_log'], 'l0_dn_dt_bias': out['l0_dn_dt_bias'], 'l0_dn_out_norm': out['l0_dn_out_norm'], 'l0_dn_w_out': out['l0_dn_w_out'], 'l0_ffn_norm': out['l0_ffn_norm'], 'l0_ffn_w_gate_up': out['l0_ffn_w_gate_up'], 'l0_ffn_w_down': out['l0_ffn_w_down'], 'l1_mix_norm': out['l1_mix_norm'], 'l1_sb_w_qkv': out['l1_sb_w_qkv'], 'l1_sb_q_norm': out['l1_sb_q_norm'], 'l1_sb_k_norm': out['l1_sb_k_norm'], 'l1_sb_w_out': out['l1_sb_w_out'], 'l1_ffn_norm': out['l1_ffn_norm'], 'l1_ffn_w_gate_up': out['l1_ffn_w_gate_up'], 'l1_ffn_w_down': out['l1_ffn_w_down'], 'l2_mix_norm': out['l2_mix_norm'], 'l2_mla_w_down': out['l2_mla_w_down'], 'l2_mla_q_a_norm': out['l2_mla_q_a_norm'], 'l2_mla_kv_a_norm': out['l2_mla_kv_a_norm'], 'l2_mla_w_uq': out['l2_mla_w_uq'], 'l2_mla_w_ukv': out['l2_mla_w_ukv'], 'l2_mla_q_nope_norm': out['l2_mla_q_nope_norm'], 'l2_mla_q_rope_norm': out['l2_mla_q_rope_norm'], 'l2_mla_k_nope_norm': out['l2_mla_k_nope_norm'], 'l2_mla_k_rope_norm': out['l2_mla_k_rope_norm'], 'l2_mla_w_out': out['l2_mla_w_out'], 'l2_ffn_norm': out['l2_ffn_norm'], 'l2_ffn_w_gate_up': out['l2_ffn_w_gate_up'], 'l2_ffn_w_down': out['l2_ffn_w_down'], 'l3_mix_norm': out['l3_mix_norm'], 'l3_dn_w_in': out['l3_dn_w_in'], 'l3_dn_conv_w': out['l3_dn_conv_w'], 'l3_dn_a_log': out['l3_dn_a_log'], 'l3_dn_dt_bias': out['l3_dn_dt_bias'], 'l3_dn_out_norm': out['l3_dn_out_norm'], 'l3_dn_w_out': out['l3_dn_w_out'], 'l3_ffn_norm': out['l3_ffn_norm'], 'l3_ffn_w_gate_up': out['l3_ffn_w_gate_up'], 'l3_ffn_w_down': out['l3_ffn_w_down'], 'loss_target': out['loss_target'], 'm_l0_mix_norm': out['m_l0_mix_norm'], 'm_l0_dn_w_in': out['m_l0_dn_w_in'], 'm_l0_dn_conv_w': out['m_l0_dn_conv_w'], 'm_l0_dn_a_log': out['m_l0_dn_a_log'], 'm_l0_dn_dt_bias': out['m_l0_dn_dt_bias'], 'm_l0_dn_out_norm': out['m_l0_dn_out_norm'], 'm_l0_dn_w_out': out['m_l0_dn_w_out'], 'm_l0_ffn_norm': out['m_l0_ffn_norm'], 'm_l0_ffn_w_gate_up': out['m_l0_ffn_w_gate_up'], 'm_l0_ffn_w_down': out['m_l0_ffn_w_down'], 'm_l1_mix_norm': out['m_l1_mix_norm'], 'm_l1_sb_w_qkv': out['m_l1_sb_w_qkv'], 'm_l1_sb_q_norm': out['m_l1_sb_q_norm'], 'm_l1_sb_k_norm': out['m_l1_sb_k_norm'], 'm_l1_sb_w_out': out['m_l1_sb_w_out'], 'm_l1_ffn_norm': out['m_l1_ffn_norm'], 'm_l1_ffn_w_gate_up': out['m_l1_ffn_w_gate_up'], 'm_l1_ffn_w_down': out['m_l1_ffn_w_down'], 'm_l2_mix_norm': out['m_l2_mix_norm'], 'm_l2_mla_w_down': out['m_l2_mla_w_down'], 'm_l2_mla_q_a_norm': out['m_l2_mla_q_a_norm'], 'm_l2_mla_kv_a_norm': out['m_l2_mla_kv_a_norm'], 'm_l2_mla_w_uq': out['m_l2_mla_w_uq'], 'm_l2_mla_w_ukv': out['m_l2_mla_w_ukv'], 'm_l2_mla_q_nope_norm': out['m_l2_mla_q_nope_norm'], 'm_l2_mla_q_rope_norm': out['m_l2_mla_q_rope_norm'], 'm_l2_mla_k_nope_norm': out['m_l2_mla_k_nope_norm'], 'm_l2_mla_k_rope_norm': out['m_l2_mla_k_rope_norm'], 'm_l2_mla_w_out': out['m_l2_mla_w_out'], 'm_l2_ffn_norm': out['m_l2_ffn_norm'], 'm_l2_ffn_w_gate_up': out['m_l2_ffn_w_gate_up'], 'm_l2_ffn_w_down': out['m_l2_ffn_w_down'], 'm_l3_mix_norm': out['m_l3_mix_norm'], 'm_l3_dn_w_in': out['m_l3_dn_w_in'], 'm_l3_dn_conv_w': out['m_l3_dn_conv_w'], 'm_l3_dn_a_log': out['m_l3_dn_a_log'], 'm_l3_dn_dt_bias': out['m_l3_dn_dt_bias'], 'm_l3_dn_out_norm': out['m_l3_dn_out_norm'], 'm_l3_dn_w_out': out['m_l3_dn_w_out'], 'm_l3_ffn_norm': out['m_l3_ffn_norm'], 'm_l3_ffn_w_gate_up': out['m_l3_ffn_w_gate_up'], 'm_l3_ffn_w_down': out['m_l3_ffn_w_down'], 'v_l0_mix_norm': out['v_l0_mix_norm'], 'v_l0_dn_w_in': out['v_l0_dn_w_in'], 'v_l0_dn_conv_w': out['v_l0_dn_conv_w'], 'v_l0_dn_a_log': out['v_l0_dn_a_log'], 'v_l0_dn_dt_bias': out['v_l0_dn_dt_bias'], 'v_l0_dn_out_norm': out['v_l0_dn_out_norm'], 'v_l0_dn_w_out': out['v_l0_dn_w_out'], 'v_l0_ffn_norm': out['v_l0_ffn_norm'], 'v_l0_ffn_w_gate_up': out['v_l0_ffn_w_gate_up'], 'v_l0_ffn_w_down': out['v_l0_ffn_w_down'], 'v_l1_mix_norm': out['v_l1_mix_norm'], 'v_l1_sb_w_qkv': out['v_l1_sb_w_qkv'], 'v_l1_sb_q_norm': out['v_l1_sb_q_norm'], 'v_l1_sb_k_norm': out['v_l1_sb_k_norm'], 'v_l1_sb_w_out': out['v_l1_sb_w_out'], 'v_l1_ffn_norm': out['v_l1_ffn_norm'], 'v_l1_ffn_w_gate_up': out['v_l1_ffn_w_gate_up'], 'v_l1_ffn_w_down': out['v_l1_ffn_w_down'], 'v_l2_mix_norm': out['v_l2_mix_norm'], 'v_l2_mla_w_down': out['v_l2_mla_w_down'], 'v_l2_mla_q_a_norm': out['v_l2_mla_q_a_norm'], 'v_l2_mla_kv_a_norm': out['v_l2_mla_kv_a_norm'], 'v_l2_mla_w_uq': out['v_l2_mla_w_uq'], 'v_l2_mla_w_ukv': out['v_l2_mla_w_ukv'], 'v_l2_mla_q_nope_norm': out['v_l2_mla_q_nope_norm'], 'v_l2_mla_q_rope_norm': out['v_l2_mla_q_rope_norm'], 'v_l2_mla_k_nope_norm': out['v_l2_mla_k_nope_norm'], 'v_l2_mla_k_rope_norm': out['v_l2_mla_k_rope_norm'], 'v_l2_mla_w_out': out['v_l2_mla_w_out'], 'v_l2_ffn_norm': out['v_l2_ffn_norm'], 'v_l2_ffn_w_gate_up': out['v_l2_ffn_w_gate_up'], 'v_l2_ffn_w_down': out['v_l2_ffn_w_down'], 'v_l3_mix_norm': out['v_l3_mix_norm'], 'v_l3_dn_w_in': out['v_l3_dn_w_in'], 'v_l3_dn_conv_w': out['v_l3_dn_conv_w'], 'v_l3_dn_a_log': out['v_l3_dn_a_log'], 'v_l3_dn_dt_bias': out['v_l3_dn_dt_bias'], 'v_l3_dn_out_norm': out['v_l3_dn_out_norm'], 'v_l3_dn_w_out': out['v_l3_dn_w_out'], 'v_l3_ffn_norm': out['v_l3_ffn_norm'], 'v_l3_ffn_w_gate_up': out['v_l3_ffn_w_gate_up'], 'v_l3_ffn_w_down': out['v_l3_ffn_w_down']}


def _loss(weights, diff, rest, loss_target):
    with _jax.named_scope("forward"):
        args = {**rest, TWIN_DIFF_INPUT: diff, **{k: w.astype(_WEIGHT_DTYPES[k]) for k, w in weights.items()}}
        y = _forward(args)
    with _jax.named_scope("loss_head"):
        err = _jnp.square(y.astype(_jnp.float32) - loss_target)
        return 0.5 * _jnp.sum(_jnp.mean(err, axis=-1)) if err.ndim else 0.5 * err


def _adamw(w, g, m, v):
    m = ADAM_B1 * m + (1.0 - ADAM_B1) * g
    v = ADAM_B2 * v + (1.0 - ADAM_B2) * _jnp.square(g)
    m_hat = m / (1.0 - ADAM_B1 ** ADAM_STEP)
    v_hat = v / (1.0 - ADAM_B2 ** ADAM_STEP)
    delta = -ADAM_LR * (m_hat / (_jnp.sqrt(v_hat) + ADAM_EPS) + ADAM_WD * w)
    return delta, m, v


def reference(x, l0_mix_norm, l0_dn_w_in, l0_dn_conv_w, l0_dn_a_log, l0_dn_dt_bias, l0_dn_out_norm, l0_dn_w_out, l0_ffn_norm, l0_ffn_w_gate_up, l0_ffn_w_down, l1_mix_norm, l1_sb_w_qkv, l1_sb_q_norm, l1_sb_k_norm, l1_sb_w_out, l1_ffn_norm, l1_ffn_w_gate_up, l1_ffn_w_down, l2_mix_norm, l2_mla_w_down, l2_mla_q_a_norm, l2_mla_kv_a_norm, l2_mla_w_uq, l2_mla_w_ukv, l2_mla_q_nope_norm, l2_mla_q_rope_norm, l2_mla_k_nope_norm, l2_mla_k_rope_norm, l2_mla_w_out, l2_ffn_norm, l2_ffn_w_gate_up, l2_ffn_w_down, l3_mix_norm, l3_dn_w_in, l3_dn_conv_w, l3_dn_a_log, l3_dn_dt_bias, l3_dn_out_norm, l3_dn_w_out, l3_ffn_norm, l3_ffn_w_gate_up, l3_ffn_w_down, loss_target, m_l0_mix_norm, m_l0_dn_w_in, m_l0_dn_conv_w, m_l0_dn_a_log, m_l0_dn_dt_bias, m_l0_dn_out_norm, m_l0_dn_w_out, m_l0_ffn_norm, m_l0_ffn_w_gate_up, m_l0_ffn_w_down, m_l1_mix_norm, m_l1_sb_w_qkv, m_l1_sb_q_norm, m_l1_sb_k_norm, m_l1_sb_w_out, m_l1_ffn_norm, m_l1_ffn_w_gate_up, m_l1_ffn_w_down, m_l2_mix_norm, m_l2_mla_w_down, m_l2_mla_q_a_norm, m_l2_mla_kv_a_norm, m_l2_mla_w_uq, m_l2_mla_w_ukv, m_l2_mla_q_nope_norm, m_l2_mla_q_rope_norm, m_l2_mla_k_nope_norm, m_l2_mla_k_rope_norm, m_l2_mla_w_out, m_l2_ffn_norm, m_l2_ffn_w_gate_up, m_l2_ffn_w_down, m_l3_mix_norm, m_l3_dn_w_in, m_l3_dn_conv_w, m_l3_dn_a_log, m_l3_dn_dt_bias, m_l3_dn_out_norm, m_l3_dn_w_out, m_l3_ffn_norm, m_l3_ffn_w_gate_up, m_l3_ffn_w_down, v_l0_mix_norm, v_l0_dn_w_in, v_l0_dn_conv_w, v_l0_dn_a_log, v_l0_dn_dt_bias, v_l0_dn_out_norm, v_l0_dn_w_out, v_l0_ffn_norm, v_l0_ffn_w_gate_up, v_l0_ffn_w_down, v_l1_mix_norm, v_l1_sb_w_qkv, v_l1_sb_q_norm, v_l1_sb_k_norm, v_l1_sb_w_out, v_l1_ffn_norm, v_l1_ffn_w_gate_up, v_l1_ffn_w_down, v_l2_mix_norm, v_l2_mla_w_down, v_l2_mla_q_a_norm, v_l2_mla_kv_a_norm, v_l2_mla_w_uq, v_l2_mla_w_ukv, v_l2_mla_q_nope_norm, v_l2_mla_q_rope_norm, v_l2_mla_k_nope_norm, v_l2_mla_k_rope_norm, v_l2_mla_w_out, v_l2_ffn_norm, v_l2_ffn_w_gate_up, v_l2_ffn_w_down, v_l3_mix_norm, v_l3_dn_w_in, v_l3_dn_conv_w, v_l3_dn_a_log, v_l3_dn_dt_bias, v_l3_dn_out_norm, v_l3_dn_w_out, v_l3_ffn_norm, v_l3_ffn_w_gate_up, v_l3_ffn_w_down):
    given = dict(x=x, l0_mix_norm=l0_mix_norm, l0_dn_w_in=l0_dn_w_in, l0_dn_conv_w=l0_dn_conv_w, l0_dn_a_log=l0_dn_a_log, l0_dn_dt_bias=l0_dn_dt_bias, l0_dn_out_norm=l0_dn_out_norm, l0_dn_w_out=l0_dn_w_out, l0_ffn_norm=l0_ffn_norm, l0_ffn_w_gate_up=l0_ffn_w_gate_up, l0_ffn_w_down=l0_ffn_w_down, l1_mix_norm=l1_mix_norm, l1_sb_w_qkv=l1_sb_w_qkv, l1_sb_q_norm=l1_sb_q_norm, l1_sb_k_norm=l1_sb_k_norm, l1_sb_w_out=l1_sb_w_out, l1_ffn_norm=l1_ffn_norm, l1_ffn_w_gate_up=l1_ffn_w_gate_up, l1_ffn_w_down=l1_ffn_w_down, l2_mix_norm=l2_mix_norm, l2_mla_w_down=l2_mla_w_down, l2_mla_q_a_norm=l2_mla_q_a_norm, l2_mla_kv_a_norm=l2_mla_kv_a_norm, l2_mla_w_uq=l2_mla_w_uq, l2_mla_w_ukv=l2_mla_w_ukv, l2_mla_q_nope_norm=l2_mla_q_nope_norm, l2_mla_q_rope_norm=l2_mla_q_rope_norm, l2_mla_k_nope_norm=l2_mla_k_nope_norm, l2_mla_k_rope_norm=l2_mla_k_rope_norm, l2_mla_w_out=l2_mla_w_out, l2_ffn_norm=l2_ffn_norm, l2_ffn_w_gate_up=l2_ffn_w_gate_up, l2_ffn_w_down=l2_ffn_w_down, l3_mix_norm=l3_mix_norm, l3_dn_w_in=l3_dn_w_in, l3_dn_conv_w=l3_dn_conv_w, l3_dn_a_log=l3_dn_a_log, l3_dn_dt_bias=l3_dn_dt_bias, l3_dn_out_norm=l3_dn_out_norm, l3_dn_w_out=l3_dn_w_out, l3_ffn_norm=l3_ffn_norm, l3_ffn_w_gate_up=l3_ffn_w_gate_up, l3_ffn_w_down=l3_ffn_w_down, loss_target=loss_target, m_l0_mix_norm=m_l0_mix_norm, m_l0_dn_w_in=m_l0_dn_w_in, m_l0_dn_conv_w=m_l0_dn_conv_w, m_l0_dn_a_log=m_l0_dn_a_log, m_l0_dn_dt_bias=m_l0_dn_dt_bias, m_l0_dn_out_norm=m_l0_dn_out_norm, m_l0_dn_w_out=m_l0_dn_w_out, m_l0_ffn_norm=m_l0_ffn_norm, m_l0_ffn_w_gate_up=m_l0_ffn_w_gate_up, m_l0_ffn_w_down=m_l0_ffn_w_down, m_l1_mix_norm=m_l1_mix_norm, m_l1_sb_w_qkv=m_l1_sb_w_qkv, m_l1_sb_q_norm=m_l1_sb_q_norm, m_l1_sb_k_norm=m_l1_sb_k_norm, m_l1_sb_w_out=m_l1_sb_w_out, m_l1_ffn_norm=m_l1_ffn_norm, m_l1_ffn_w_gate_up=m_l1_ffn_w_gate_up, m_l1_ffn_w_down=m_l1_ffn_w_down, m_l2_mix_norm=m_l2_mix_norm, m_l2_mla_w_down=m_l2_mla_w_down, m_l2_mla_q_a_norm=m_l2_mla_q_a_norm, m_l2_mla_kv_a_norm=m_l2_mla_kv_a_norm, m_l2_mla_w_uq=m_l2_mla_w_uq, m_l2_mla_w_ukv=m_l2_mla_w_ukv, m_l2_mla_q_nope_norm=m_l2_mla_q_nope_norm, m_l2_mla_q_rope_norm=m_l2_mla_q_rope_norm, m_l2_mla_k_nope_norm=m_l2_mla_k_nope_norm, m_l2_mla_k_rope_norm=m_l2_mla_k_rope_norm, m_l2_mla_w_out=m_l2_mla_w_out, m_l2_ffn_norm=m_l2_ffn_norm, m_l2_ffn_w_gate_up=m_l2_ffn_w_gate_up, m_l2_ffn_w_down=m_l2_ffn_w_down, m_l3_mix_norm=m_l3_mix_norm, m_l3_dn_w_in=m_l3_dn_w_in, m_l3_dn_conv_w=m_l3_dn_conv_w, m_l3_dn_a_log=m_l3_dn_a_log, m_l3_dn_dt_bias=m_l3_dn_dt_bias, m_l3_dn_out_norm=m_l3_dn_out_norm, m_l3_dn_w_out=m_l3_dn_w_out, m_l3_ffn_norm=m_l3_ffn_norm, m_l3_ffn_w_gate_up=m_l3_ffn_w_gate_up, m_l3_ffn_w_down=m_l3_ffn_w_down, v_l0_mix_norm=v_l0_mix_norm, v_l0_dn_w_in=v_l0_dn_w_in, v_l0_dn_conv_w=v_l0_dn_conv_w, v_l0_dn_a_log=v_l0_dn_a_log, v_l0_dn_dt_bias=v_l0_dn_dt_bias, v_l0_dn_out_norm=v_l0_dn_out_norm, v_l0_dn_w_out=v_l0_dn_w_out, v_l0_ffn_norm=v_l0_ffn_norm, v_l0_ffn_w_gate_up=v_l0_ffn_w_gate_up, v_l0_ffn_w_down=v_l0_ffn_w_down, v_l1_mix_norm=v_l1_mix_norm, v_l1_sb_w_qkv=v_l1_sb_w_qkv, v_l1_sb_q_norm=v_l1_sb_q_norm, v_l1_sb_k_norm=v_l1_sb_k_norm, v_l1_sb_w_out=v_l1_sb_w_out, v_l1_ffn_norm=v_l1_ffn_norm, v_l1_ffn_w_gate_up=v_l1_ffn_w_gate_up, v_l1_ffn_w_down=v_l1_ffn_w_down, v_l2_mix_norm=v_l2_mix_norm, v_l2_mla_w_down=v_l2_mla_w_down, v_l2_mla_q_a_norm=v_l2_mla_q_a_norm, v_l2_mla_kv_a_norm=v_l2_mla_kv_a_norm, v_l2_mla_w_uq=v_l2_mla_w_uq, v_l2_mla_w_ukv=v_l2_mla_w_ukv, v_l2_mla_q_nope_norm=v_l2_mla_q_nope_norm, v_l2_mla_q_rope_norm=v_l2_mla_q_rope_norm, v_l2_mla_k_nope_norm=v_l2_mla_k_nope_norm, v_l2_mla_k_rope_norm=v_l2_mla_k_rope_norm, v_l2_mla_w_out=v_l2_mla_w_out, v_l2_ffn_norm=v_l2_ffn_norm, v_l2_ffn_w_gate_up=v_l2_ffn_w_gate_up, v_l2_ffn_w_down=v_l2_ffn_w_down, v_l3_mix_norm=v_l3_mix_norm, v_l3_dn_w_in=v_l3_dn_w_in, v_l3_dn_conv_w=v_l3_dn_conv_w, v_l3_dn_a_log=v_l3_dn_a_log, v_l3_dn_dt_bias=v_l3_dn_dt_bias, v_l3_dn_out_norm=v_l3_dn_out_norm, v_l3_dn_w_out=v_l3_dn_w_out, v_l3_ffn_norm=v_l3_ffn_norm, v_l3_ffn_w_gate_up=v_l3_ffn_w_gate_up, v_l3_ffn_w_down=v_l3_ffn_w_down)
    weights = {n: given[n] for n in TWIN_WEIGHTS}
    shared = {n: given[n] for n in SHARED_INPUTS}
    per_example = {n: given[n] for n in ['x']}
    grad_fn = _jax.value_and_grad(_loss, argnums=(0, 1))

    def one_microbatch(ex, loss_target):
        ex = dict(ex)
        diff = ex.pop(TWIN_DIFF_INPUT)
        return grad_fn(weights, diff, {**shared, **ex}, loss_target)

    if N_MICROBATCH == 1:
        loss, (grad_w, grad_x) = one_microbatch(per_example, given["loss_target"])
    else:
        def body(carry, xs):
            loss_sum, grad_sum = carry
            l_k, (gw_k, gx_k) = one_microbatch(xs[0], xs[1])
            with _jax.named_scope("update"):
                return (loss_sum + l_k, _jax.tree.map(_jnp.add, grad_sum, gw_k)), gx_k

        init = (_jnp.zeros((), _jnp.float32), _jax.tree.map(_jnp.zeros_like, weights))
        (loss, grad_w), grad_x = _jax.lax.scan(body, init, (per_example, given["loss_target"]))
    with _jax.named_scope("update"):
        delta_w, new_m, new_v = {}, {}, {}
        for n in TWIN_WEIGHTS:
            delta_w[n], new_m[n], new_v[n] = _adamw(weights[n], grad_w[n], given["m_" + n], given["v_" + n])
    return (loss, grad_x, *[grad_w[n] for n in TWIN_WEIGHTS], *[delta_w[n] for n in TWIN_WEIGHTS],
            *[new_m[n] for n in TWIN_WEIGHTS], *[new_v[n] for n in TWIN_WEIGHTS])
```

```python
import functools
import math

import jax
import jax.numpy as jnp
from jax import lax
from jax.experimental import pallas as pl
from jax.experimental.pallas import tpu as pltpu

F32, BF16 = jnp.float32, jnp.bfloat16
MESH = pl.DeviceIdType.MESH

D_MODEL = 1024
N_HEADS = 8
HEAD = 128
FFN_HIDDEN = 2816
DN_CHUNK = 64
NORM_EPS = 1e-6
MLA_ROPE = 64
MLA_QK = 192
ROPE_THETA = 10000.0
ADAM_LR, ADAM_B1, ADAM_B2, ADAM_EPS, ADAM_WD, ADAM_STEP = 0.001, 0.9, 0.999, 1e-08, 0.01, 10
N_CHIPS = 4
LANES = 128
VMEM_LIMIT = 56 * 2 ** 20


def _params(n_grid):
    return pltpu.CompilerParams(dimension_semantics=("arbitrary",) * n_grid, vmem_limit_bytes=VMEM_LIMIT)


_MIXERS = (
    ("dn_w_in", "dn_conv_w", "dn_a_log", "dn_dt_bias", "dn_out_norm", "dn_w_out"),
    ("sb_w_qkv", "sb_q_norm", "sb_k_norm", "sb_w_out"),
    ("mla_w_down", "mla_q_a_norm", "mla_kv_a_norm", "mla_w_uq", "mla_w_ukv", "mla_q_nope_norm",
     "mla_q_rope_norm", "mla_k_nope_norm", "mla_k_rope_norm", "mla_w_out"),
)
DEPTH = 4


def _layer_names(i):
    p = "l%d_" % i
    return [p + "mix_norm"] + [p + n for n in _MIXERS[i % 3]] + [p + "ffn_norm", p + "ffn_w_gate_up", p + "ffn_w_down"]


WEIGHTS = [n for i in range(DEPTH) for n in _layer_names(i)]
_ROW_SHARDED = ("w_out", "ffn_w_down", "mla_w_down")
_COL_SHARDED = ("dn_w_in", "sb_w_qkv", "mla_w_uq", "mla_w_ukv", "ffn_w_gate_up")


def _shard_axis(name):
    if name.endswith(_ROW_SHARDED):
        return 0
    if name.endswith(_COL_SHARDED):
        return 1
    return None


BIG = [n for n in WEIGHTS if _shard_axis(n) is not None]
SMALL = [n for n in WEIGHTS if _shard_axis(n) is None]


_DN = {"nn": (((1,), (0,)), ((), ())), "nt": (((1,), (1,)), ((), ())), "tn": (((0,), (0,)), ((), ()))}


def _dg(a, b, kind):
    return lax.dot_general(a.astype(BF16), b.astype(BF16), _DN[kind], preferred_element_type=F32)


@functools.partial(jax.custom_vjp, nondiff_argnums=(2,))
def bdot(a, b, kind):
    return _dg(a, b, kind)


def _bdot_fwd(a, b, kind):
    return _dg(a, b, kind), (a, b)


def _bdot_bwd(kind, res, ct):
    a, b = res
    if kind == "nn":
        return _dg(ct, b, "nt"), _dg(a, ct, "tn")
    if kind == "nt":
        return _dg(ct, b, "nn"), _dg(ct, a, "tn")
    return _dg(b, ct, "nt"), _dg(a, ct, "nn")


bdot.defvjp(_bdot_fwd, _bdot_bwd)


def _split3(a):
    hi = a.astype(BF16)
    r1 = a - hi.astype(F32)
    mid = r1.astype(BF16)
    lo = (r1 - mid.astype(F32)).astype(BF16)
    return hi, mid, lo


def _xdot(a, b, kind, exact):
    if exact == 0:
        return sum(lax.dot_general(a, p, _DN[kind], preferred_element_type=F32) for p in _split3(b))
    return sum(lax.dot_general(p, b, _DN[kind], preferred_element_type=F32) for p in _split3(a))


def _tri(n, rel):
    r = lax.broadcasted_iota(jnp.int32, (n, n), 0)
    c = lax.broadcasted_iota(jnp.int32, (n, n), 1)
    return {"le": c <= r, "lt": c < r, "ge": c >= r, "gt": c > r}[rel]


@jax.custom_vjp
def cumsum_rows(g):
    return _xdot(_tri(g.shape[0], "le").astype(BF16), g, "nn", 0)


def _cumsum_fwd(g):
    return cumsum_rows(g), None


def _cumsum_bwd(_, ct):
    return (_xdot(_tri(ct.shape[0], "le").astype(BF16), ct, "tn", 0),)


cumsum_rows.defvjp(_cumsum_fwd, _cumsum_bwd)


def _hdot(a, b):
    return jnp.dot(a, b, precision=lax.Precision.HIGHEST, preferred_element_type=F32)


def _unit_lower_inverse(lower):
    n = lower.shape[0]
    eye = (lax.broadcasted_iota(jnp.int32, (n, n), 0) == lax.broadcasted_iota(jnp.int32, (n, n), 1)).astype(F32)
    m = -lower
    p = eye + m
    for _ in range(int(math.log2(n)) - 1):
        m = _hdot(m, m)
        p = p + _hdot(p, m)
    return p


def _rms(x, g, n=None):
    n = x.shape[-1] if n is None else n
    return x * lax.rsqrt(jnp.sum(x * x, axis=-1, keepdims=True) * (1.0 / n) + NORM_EPS) * g


def _l2(x):
    return x * lax.rsqrt(jnp.sum(x * x, axis=-1, keepdims=True) + NORM_EPS)


def _silu(x):
    return x * jax.nn.sigmoid(x)


def _logsig(z):
    return jnp.minimum(z, 0.0) - jnp.log1p(jnp.exp(-jnp.abs(z)))


@jax.custom_vjp
def _rope(x, cos, sin_lo, sin_hi):
    return x * cos + pltpu.roll(x, 96, 1) * sin_lo + pltpu.roll(x, 32, 1) * sin_hi


def _rope_fwd(x, cos, sin_lo, sin_hi):
    return _rope(x, cos, sin_lo, sin_hi), (cos, sin_lo, sin_hi)


def _rope_bwd(res, ct):
    cos, sin_lo, sin_hi = res
    dx = ct * cos + pltpu.roll(ct * sin_lo, 32, 1) + pltpu.roll(ct * sin_hi, 96, 1)
    return dx, jnp.zeros_like(cos), jnp.zeros_like(sin_lo), jnp.zeros_like(sin_hi)


_rope.defvjp(_rope_fwd, _rope_bwd)


def _tile(n, prefs=(512, 384, 256, 128)):
    for t in prefs:
        if n % t == 0:
            return t
    return n


def _mm(name, a, b, kind, out_dtype=F32, add=None):
    if kind == "tn":
        (kdim, m), n = a.shape, b.shape[1]
    else:
        (m, kdim), n = a.shape, (b.shape[0] if kind == "nt" else b.shape[1])
    tm, tn, tk = _tile(m), _tile(n), _tile(kdim)
    nk = kdim // tk
    a_spec = pl.BlockSpec((tk, tm), lambda i, j, k: (k, i)) if kind == "tn" else pl.BlockSpec((tm, tk), lambda i, j, k: (i, k))
    b_spec = pl.BlockSpec((tn, tk), lambda i, j, k: (j, k)) if kind == "nt" else pl.BlockSpec((tk, tn), lambda i, j, k: (k, j))
    o_spec = pl.BlockSpec((tm, tn), lambda i, j, k: (i, j))
    has_add = add is not None

    def body(*refs):
        a_ref, b_ref = refs[0], refs[1]
        o_ref, acc = refs[-2], refs[-1]
        k = pl.program_id(2)

        @pl.when(k == 0)
        def _():
            acc[...] = jnp.zeros_like(acc)

        acc[...] += _dg(a_ref[...], b_ref[...], kind)

        @pl.when(k == nk - 1)
        def _():
            r = acc[...]
            if has_add:
                r = r + refs[2][...]
            o_ref[...] = r.astype(o_ref.dtype)

    return pl.pallas_call(
        body, grid=(m // tm, n // tn, nk),
        in_specs=[a_spec, b_spec] + ([o_spec] if has_add else []),
        out_specs=o_spec, out_shape=jax.ShapeDtypeStruct((m, n), out_dtype),
        scratch_shapes=[pltpu.VMEM((tm, tn), F32)], name=name, compiler_params=_params(3),
    )(*([a, b] + ([add] if has_add else [])))


class _V:
    def __init__(self, arr, w=None, base=0, diff=True):
        self.arr, self.base, self.diff = arr, base, diff
        self.w = arr.shape[1] if w is None else w

    def spec(self, tm):
        return pl.BlockSpec((tm, self.w), lambda i, b=self.base: (i, b))


def _as_views(ins):
    return [v if isinstance(v, _V) else _V(v) for v in ins]


def _tup(r):
    return tuple(r) if isinstance(r, (tuple, list)) else (r,)


def _ew(name, fn, ins, smalls, outs, tm=256):
    ins = _as_views(ins)
    t = ins[0].arr.shape[0]
    tm = min(tm, t)
    n_in = len(ins) + len(smalls)

    def body(*refs):
        res = _tup(fn(*[r[...] for r in refs[:n_in]]))
        for r, o in zip(refs[n_in:], res):
            r[...] = o.astype(r.dtype)

    return pl.pallas_call(
        body, grid=(t // tm,),
        in_specs=[v.spec(tm) for v in ins] + [pl.BlockSpec(s.shape, lambda i: (0, 0)) for s in smalls],
        out_specs=[pl.BlockSpec((tm, w), lambda i: (i, 0)) for w, _ in outs],
        out_shape=[jax.ShapeDtypeStruct((t, w), dt) for w, dt in outs],
        name=name, compiler_params=_params(1),
    )(*[v.arr for v in ins], *smalls)


def _ew_bwd(name, fn, ins, smalls, cts, tm=256, add=None):
    ins = _as_views(ins)
    t = ins[0].arr.shape[0]
    tm = min(tm, t)
    n_in, n_sm = len(ins), len(smalls)
    diff = [k for k, v in enumerate(ins) if v.diff]
    ct_arrs = [c for c in cts if c is not None]
    has_add = add is not None

    def body(*refs):
        vals = [r[...] for r in refs[:n_in]]
        svals = [r[...] for r in refs[n_in:n_in + n_sm]]
        p = n_in + n_sm
        ct_refs = list(refs[p:p + len(ct_arrs)])
        p += len(ct_arrs)
        add_ref = refs[p] if has_add else None
        p += int(has_add)
        din_refs = refs[p:p + len(diff)]
        dsm_refs = refs[p + len(diff):]

        def f(dv, sv):
            full = list(vals)
            for k, d in zip(diff, dv):
                full[k] = d
            return _tup(fn(*full, *sv))

        res, vjp = jax.vjp(f, [vals[k] for k in diff], svals)
        ctv = tuple(jnp.zeros_like(o) if c is None else ct_refs.pop(0)[...].astype(o.dtype) for c, o in zip(cts, res))
        dv, dsv = vjp(ctv)
        for n, (r, d) in enumerate(zip(din_refs, dv)):
            if n == 0 and has_add:
                d = d + add_ref[...]
            r[...] = d.astype(r.dtype)

        @pl.when(pl.program_id(0) == 0)
        def _():
            for r in dsm_refs:
                r[...] = jnp.zeros_like(r)

        for r, d in zip(dsm_refs, dsv):
            r[...] += d

    row = lambda w: pl.BlockSpec((tm, w), lambda i: (i, 0))
    small_specs = [pl.BlockSpec(s.shape, lambda i: (0, 0)) for s in smalls]
    out = pl.pallas_call(
        body, grid=(t // tm,),
        in_specs=[v.spec(tm) for v in ins] + small_specs + [row(c.shape[1]) for c in ct_arrs]
        + ([row(add.shape[1])] if has_add else []),
        out_specs=[row(ins[k].w) for k in diff] + small_specs,
        out_shape=[jax.ShapeDtypeStruct((t, ins[k].w), F32) for k in diff]
        + [jax.ShapeDtypeStruct(s.shape, F32) for s in smalls],
        name=name, compiler_params=_params(1),
    )(*[v.arr for v in ins], *smalls, *ct_arrs, *([add] if has_add else []))
    return out[:len(diff)], out[len(diff):]


BQ = 128


def _attn_specs(qs, ks, v, t):
    q_specs = [pl.BlockSpec((BQ, HEAD), lambda h, i: (i, h)) for _ in qs]
    k_specs = [pl.BlockSpec((t, HEAD), (lambda h, i: (0, 0)) if sh else (lambda h, i: (0, h))) for _, sh in ks]
    v_spec = pl.BlockSpec((t, HEAD), lambda h, i, b=v[1], s=v[2]: (0, b + h * s))
    return q_specs, k_specs, v_spec


def _attn_fwd(name, mode, qs, ks, v):
    t = qs[0].shape[0]
    nq, n = t // BQ, len(qs)
    q_specs, k_specs, v_spec = _attn_specs(qs, ks, v, t)

    def body(*refs):
        q_refs, k_refs, v_ref = refs[:n], refs[n:2 * n], refs[2 * n]
        o_ref, st_ref = refs[2 * n + 1], refs[2 * n + 2]
        i = pl.program_id(1)
        q = [r[...] for r in q_refs]
        row = i * BQ + lax.broadcasted_iota(jnp.int32, (BQ, BQ), 0)
        col0 = lax.broadcasted_iota(jnp.int32, (BQ, BQ), 1)

        def logits(j):
            off = pl.multiple_of(j * BQ, BQ)
            z = sum(_dg(qp, kr[pl.ds(off, BQ), :], "nt") for qp, kr in zip(q, k_refs))
            return z, v_ref[pl.ds(off, BQ), :], j * BQ + col0

        if mode == "sb":
            after = _tri(BQ, "lt").astype(BF16)

            def step(s, carry):
                acc, run = carry
                z, vj, col = logits(i - s)
                past = col < row
                lsz = _logsig(z)
                stay = jnp.where(past, lsz - z, 0.0)
                later = run + _xdot(stay, after, "nn", 1)
                a = jnp.where(past, jnp.exp(lsz + later), 0.0)
                return acc + _dg(a, vj, "nn"), run + jnp.sum(stay, axis=1, keepdims=True)

            acc, run = lax.fori_loop(0, i + 1, step, (jnp.zeros((BQ, HEAD), F32), jnp.zeros((BQ, 1), F32)))
            o_ref[...] = acc
            st_ref[...] = jnp.broadcast_to(run, (BQ, HEAD))
        else:
            def step(j, carry):
                m, l, acc = carry
                z, vj, col = logits(j)
                z = jnp.where(col <= row, z, -1e30)
                m2 = jnp.maximum(m, jnp.max(z, axis=1, keepdims=True))
                p = jnp.exp(z - m2)
                alpha = jnp.exp(m - m2)
                return m2, alpha * l + jnp.sum(p, axis=1, keepdims=True), alpha * acc + _dg(p, vj, "nn")

            m, l, acc = lax.fori_loop(0, i + 1, step, (jnp.full((BQ, 1), -1e30, F32), jnp.zeros((BQ, 1), F32),
                                                       jnp.zeros((BQ, HEAD), F32)))
            o_ref[...] = acc / l
            st_ref[...] = jnp.broadcast_to(m + jnp.log(l), (BQ, HEAD))

    blk = pl.BlockSpec((BQ, HEAD), lambda h, i: (i, h))
    return pl.pallas_call(
        body, grid=(N_HEADS, nq), in_specs=q_specs + k_specs + [v_spec], out_specs=[blk, blk],
        out_shape=[jax.ShapeDtypeStruct((t, N_HEADS * HEAD), F32)] * 2, name=name, compiler_params=_params(2),
    )(*qs, *[k for k, _ in ks], v[0])


def _attn_bwd(name, mode, qs, ks, v, o, stat, do):
    t = qs[0].shape[0]
    nq, n = t // BQ, len(qs)
    q_specs, k_specs, v_spec = _attn_specs(qs, ks, v, t)
    shared = [sh for _, sh in ks]

    def body(*refs):
        q_refs, k_refs, v_ref = refs[:n], refs[n:2 * n], refs[2 * n]
        o_ref, st_ref, do_ref = refs[2 * n + 1:2 * n + 4]
        dq_refs = refs[2 * n + 4:3 * n + 4]
        dk_refs = refs[3 * n + 4:4 * n + 4]
        dv_ref = refs[4 * n + 4]
        h, i = pl.program_id(0), pl.program_id(1)

        @pl.when(i == 0)
        def _():
            dv_ref[...] = jnp.zeros_like(dv_ref)
            for r, sh in zip(dk_refs, shared):
                if not sh:
                    r[...] = jnp.zeros_like(r)

        for r, sh in zip(dk_refs, shared):
            if sh:
                @pl.when((i == 0) & (h == 0))
                def _(r=r):
                    r[...] = jnp.zeros_like(r)

        q = [r[...] for r in q_refs]
        do_t = do_ref[...]
        st = st_ref[:, :1]
        row = i * BQ + lax.broadcasted_iota(jnp.int32, (BQ, BQ), 0)
        col0 = lax.broadcasted_iota(jnp.int32, (BQ, BQ), 1)
        if mode == "sb":
            upto = _tri(BQ, "ge").astype(BF16)
            before = _tri(BQ, "gt").astype(BF16)
        else:
            dsum = jnp.sum(do_t * o_ref[...], axis=1, keepdims=True)

        def step(j, carry):
            off = pl.multiple_of(j * BQ, BQ)
            kj = [kr[pl.ds(off, BQ), :] for kr in k_refs]
            vj = v_ref[pl.ds(off, BQ), :]
            z = sum(_dg(qp, kp, "nt") for qp, kp in zip(q, kj))
            col = j * BQ + col0
            da = _dg(do_t, vj, "nt")
            if mode == "sb":
                dq, pre, gpre = carry[0], carry[1], carry[2]
                past = col < row
                lsz = _logsig(z)
                stay = jnp.where(past, lsz - z, 0.0)
                later = st - (pre + _xdot(stay, upto, "nn", 1))
                a = jnp.where(past, jnp.exp(lsz + later), 0.0)
                g = a * da
                gbefore = gpre + _xdot(g, before, "nn", 1)
                sig = jnp.exp(lsz)
                dz = jnp.where(past, g * (1.0 - sig) - sig * gbefore, 0.0)
                tail = (pre + jnp.sum(stay, axis=1, keepdims=True), gpre + jnp.sum(g, axis=1, keepdims=True))
            else:
                dq = carry[0]
                a = jnp.where(col <= row, jnp.exp(z - st), 0.0)
                dz = a * (da - dsum)
                tail = ()
            dq = tuple(d + _dg(dz, kp, "nn") for d, kp in zip(dq, kj))
            for r, qp in zip(dk_refs, q):
                r[pl.ds(off, BQ), :] += _dg(dz, qp, "tn")
            dv_ref[pl.ds(off, BQ), :] += _dg(a, do_t, "tn")
            return (dq,) + tail

        zero = jnp.zeros((BQ, 1), F32)
        init = (tuple(jnp.zeros((BQ, HEAD), F32) for _ in q),) + ((zero, zero) if mode == "sb" else ())
        dq = lax.fori_loop(0, i + 1, step, init)[0]
        for r, d in zip(dq_refs, dq):
            r[...] = d

    blk = pl.BlockSpec((BQ, HEAD), lambda h, i: (i, h))
    per_head = pl.BlockSpec((t, HEAD), lambda h, i: (0, h))
    dk_specs = [pl.BlockSpec((t, HEAD), lambda h, i: (0, 0)) if sh else per_head for sh in shared]
    wide = jax.ShapeDtypeStruct((t, N_HEADS * HEAD), F32)
    out = pl.pallas_call(
        body, grid=(N_HEADS, nq), in_specs=q_specs + k_specs + [v_spec, blk, blk, blk],
        out_specs=[blk] * n + dk_specs + [per_head],
        out_shape=[wide] * n + [jax.ShapeDtypeStruct((t, HEAD), F32) if sh else wide for sh in shared] + [wide],
        name=name, compiler_params=_params(2),
    )(*qs, *[k for k, _ in ks], v[0], o, stat, do)
    return out[:n], out[n:2 * n], out[2 * n]


def _dn_chunk(q, k, v, g, beta, state):
    c = q.shape[0]
    gc = cumsum_rows(g)
    gcc = gc[:, :c]
    diff = gcc - gcc.T
    causal, strict = _tri(c, "le"), _tri(c, "lt")
    decay = jnp.where(causal, jnp.exp(jnp.where(causal, diff, 0.0)), 0.0)
    kb = k * beta
    lower = jnp.where(strict, bdot(kb, k, "nt") * decay, 0.0)
    tinv = _unit_lower_inverse(lower)
    eg = jnp.exp(gc)
    u = _hdot(tinv, v * beta)
    w = _hdot(tinv, kb * eg)
    attn = bdot(q, k, "nt") * decay
    glast = gc[c - 1:c, :]
    v_new = u - bdot(w, state, "nn")
    o = bdot(q * eg, state, "nn") + bdot(attn, v_new, "nn")
    new_state = state * jnp.exp(glast) + bdot(k * jnp.exp(glast - gc), v_new, "tn")
    return o, new_state


def _dn_fwd(name, q, k, v, g, beta):
    t = q.shape[0]
    nc = t // DN_CHUNK
    blk = pl.BlockSpec((DN_CHUNK, HEAD), lambda h, n: (n, h))
    st_spec = pl.BlockSpec((None, None, HEAD, HEAD), lambda h, n: (h, n, 0, 0))

    def body(q_ref, k_ref, v_ref, g_ref, b_ref, o_ref, st_ref, state):
        @pl.when(pl.program_id(1) == 0)
        def _():
            state[...] = jnp.zeros_like(state)

        s_in = state[...]
        st_ref[...] = s_in
        o, s_out = _dn_chunk(q_ref[...], k_ref[...], v_ref[...], g_ref[...], b_ref[...], s_in)
        o_ref[...] = o
        state[...] = s_out

    return pl.pallas_call(
        body, grid=(N_HEADS, nc), in_specs=[blk] * 5, out_specs=[blk, st_spec],
        out_shape=[jax.ShapeDtypeStruct((t, N_HEADS * HEAD), F32), jax.ShapeDtypeStruct((N_HEADS, nc, HEAD, HEAD), F32)],
        scratch_shapes=[pltpu.VMEM((HEAD, HEAD), F32)], name=name, compiler_params=_params(2),
    )(q, k, v, g, beta)


def _dn_bwd(name, q, k, v, g, beta, states, do):
    t = q.shape[0]
    nc = t // DN_CHUNK
    blk = pl.BlockSpec((DN_CHUNK, HEAD), lambda h, n: (nc - 1 - n, h))
    st_spec = pl.BlockSpec((None, None, HEAD, HEAD), lambda h, n: (h, nc - 1 - n, 0, 0))

    def body(q_ref, k_ref, v_ref, g_ref, b_ref, st_ref, do_ref, dq_ref, dk_ref, dv_ref, dg_ref, db_ref, dstate):
        @pl.when(pl.program_id(1) == 0)
        def _():
            dstate[...] = jnp.zeros_like(dstate)

        _, vjp = jax.vjp(_dn_chunk, q_ref[...], k_ref[...], v_ref[...], g_ref[...], b_ref[...], st_ref[...])
        dq, dk, dv, dg, db, ds = vjp((do_ref[...], dstate[...]))
        dq_ref[...], dk_ref[...], dv_ref[...], dg_ref[...], db_ref[...] = dq, dk, dv, dg, db
        dstate[...] = ds

    wide = jax.ShapeDtypeStruct((t, N_HEADS * HEAD), F32)
    return pl.pallas_call(
        body, grid=(N_HEADS, nc), in_specs=[blk] * 5 + [st_spec, blk], out_specs=[blk] * 5, out_shape=[wide] * 5,
        scratch_shapes=[pltpu.VMEM((HEAD, HEAD), F32)], name=name, compiler_params=_params(2),
    )(q, k, v, g, beta, states, do)


CONV_W = 1024
HALO = 8


def _shift_down(cur, prev, s):
    sh = pltpu.roll(cur, s, 0)
    ph = pltpu.roll(prev, s, 0)
    r = lax.broadcasted_iota(jnp.int32, (HALO, cur.shape[1]), 0)
    return jnp.concatenate([jnp.where(r < s, ph, sh[:HALO]), sh[HALO:]], axis=0)


def _shift_up(cur, nxt, s):
    tm = cur.shape[0]
    sh = pltpu.roll(cur, tm - s, 0)
    nh = pltpu.roll(nxt, HALO - s, 0)
    r = lax.broadcasted_iota(jnp.int32, (HALO, cur.shape[1]), 0)
    return jnp.concatenate([sh[:tm - HALO], jnp.where(r >= HALO - s, nh, sh[tm - HALO:])], axis=0)


def _conv_fwd(name, proj, w, tm=256):
    t = proj.shape[0]
    tm = min(tm, t)
    width = w.shape[1]
    per = tm // HALO

    def body(cur_ref, prev_ref, w_ref, y_ref):
        cur = cur_ref[...]
        prev = jnp.where(pl.program_id(0) > 0, prev_ref[...], 0.0)
        y = cur * w_ref[3:4, :]
        for s in (1, 2, 3):
            y = y + _shift_down(cur, prev, s) * w_ref[3 - s:4 - s, :]
        y_ref[...] = y

    return pl.pallas_call(
        body, grid=(t // tm, width // CONV_W),
        in_specs=[pl.BlockSpec((tm, CONV_W), lambda i, c: (i, c)),
                  pl.BlockSpec((HALO, CONV_W), lambda i, c: (jnp.maximum(i * per - 1, 0), c)),
                  pl.BlockSpec((HALO, CONV_W), lambda i, c: (0, c))],
        out_specs=pl.BlockSpec((tm, CONV_W), lambda i, c: (i, c)),
        out_shape=jax.ShapeDtypeStruct((t, width), F32), name=name, compiler_params=_params(2),
    )(proj, proj, w)


def _conv_bwd(name, proj, w, dy, tm=256):
    t = proj.shape[0]
    tm = min(tm, t)
    width = w.shape[1]
    per, nt = tm // HALO, t // tm

    def body(cur_ref, prev_ref, w_ref, dy_ref, nxt_ref, du_ref, dw_ref):
        i = pl.program_id(1)
        cur, dy_t = cur_ref[...], dy_ref[...]
        prev = jnp.where(i > 0, prev_ref[...], 0.0)
        nxt = jnp.where(i < nt - 1, nxt_ref[...], 0.0)
        du = dy_t * w_ref[3:4, :]
        rows = [jnp.sum(dy_t * cur, axis=0, keepdims=True)]
        for s in (1, 2, 3):
            du = du + _shift_up(dy_t, nxt, s) * w_ref[3 - s:4 - s, :]
            rows.insert(0, jnp.sum(dy_t * _shift_down(cur, prev, s), axis=0, keepdims=True))
        du_ref[...] = du

        @pl.when(i == 0)
        def _():
            dw_ref[...] = jnp.zeros_like(dw_ref)

        dw_ref[...] += jnp.concatenate(rows + [jnp.zeros((HALO - 4, CONV_W), F32)], axis=0)

    return pl.pallas_call(
        body, grid=(width // CONV_W, nt),
        in_specs=[pl.BlockSpec((tm, CONV_W), lambda c, i: (i, c)),
                  pl.BlockSpec((HALO, CONV_W), lambda c, i: (jnp.maximum(i * per - 1, 0), c)),
                  pl.BlockSpec((HALO, CONV_W), lambda c, i: (0, c)),
                  pl.BlockSpec((tm, CONV_W), lambda c, i: (i, c)),
                  pl.BlockSpec((HALO, CONV_W), lambda c, i: (jnp.minimum((i + 1) * per, t // HALO - 1), c))],
        out_specs=[pl.BlockSpec((tm, CONV_W), lambda c, i: (i, c)), pl.BlockSpec((HALO, CONV_W), lambda c, i: (0, c))],
        out_shape=[jax.ShapeDtypeStruct((t, width), F32), jax.ShapeDtypeStruct((HALO, width), F32)],
        name=name, compiler_params=_params(2),
    )(proj, proj, w, dy, dy)


def _norm_fn(x, g):
    return _rms(x, g)


def _swiglu_fn(gu):
    return _silu(gu[:, :FFN_HIDDEN]) * gu[:, FFN_HIDDEN:]


def _heads(x):
    return [x[:, h * HEAD:(h + 1) * HEAD] for h in range(x.shape[1] // HEAD)]


def _dn_pre_fn(c, ab, a_log, dt_bias):
    w = N_HEADS * HEAD
    q = [_l2(_silu(x)) * (HEAD ** -0.5) for x in _heads(c[:, :w])]
    k = [_l2(_silu(x)) for x in _heads(c[:, w:2 * w])]
    v = _silu(c[:, 2 * w:])
    g, beta = [], []
    for h in range(N_HEADS):
        gh = -jnp.exp(a_log[:, h:h + 1]) * jax.nn.softplus(ab[:, h:h + 1] + dt_bias[:, h:h + 1])
        bh = jax.nn.sigmoid(ab[:, N_HEADS + h:N_HEADS + h + 1])
        g.append(jnp.broadcast_to(gh, (c.shape[0], HEAD)))
        beta.append(jnp.broadcast_to(bh, (c.shape[0], HEAD)))
    cat = lambda xs: jnp.concatenate(xs, axis=1)
    return cat(q), cat(k), v, cat(g), cat(beta)


def _dn_post_fn(o, z, out_norm):
    return jnp.concatenate([_rms(oh, out_norm) * _silu(zh) for oh, zh in zip(_heads(o), _heads(z))], axis=1)


def _sb_pre_fn(qkv, q_norm, k_norm):
    w = N_HEADS * HEAD
    q = [_rms(x, q_norm) * (HEAD ** -0.5) for x in _heads(qkv[:, :w])]
    k = [_rms(x, k_norm) for x in _heads(qkv[:, w:2 * w])]
    return jnp.concatenate(q, axis=1), jnp.concatenate(k, axis=1), qkv[:, 2 * w:]


def _mla_a_fn(down, cos, sin_lo, sin_hi, q_a_norm, kv_a_norm, k_rope_norm):
    cq = _rms(down[:, :256], q_a_norm)
    ckv = _rms(down[:, 256:384], kv_a_norm)
    kr = _rope(_rms(down[:, 384:], k_rope_norm, MLA_ROPE), cos, sin_lo, sin_hi)
    return cq, ckv, kr


def _mla_b_fn(qf, kvf, cos, sin_lo, sin_hi, q_nope_norm, q_rope_norm, k_nope_norm):
    scale = MLA_QK ** -0.5
    qn, qr, kn, v = [], [], [], []
    for h in range(N_HEADS):
        a = 2 * h * HEAD
        qn.append(_rms(qf[:, a:a + HEAD], q_nope_norm) * scale)
        qr.append(_rope(_rms(qf[:, a + HEAD:a + 2 * HEAD], q_rope_norm, MLA_ROPE), cos, sin_lo, sin_hi) * scale)
        kn.append(_rms(kvf[:, a:a + HEAD], k_nope_norm))
        v.append(kvf[:, a + HEAD:a + 2 * HEAD])
    cat = lambda xs: jnp.concatenate(xs, axis=1)
    return cat(qn), cat(qr), cat(kn), cat(v)


def _rope_tables(t):
    inv_freq = ROPE_THETA ** (-jnp.arange(0, MLA_ROPE, 2, dtype=F32) / MLA_ROPE)
    ang = jnp.arange(t, dtype=F32)[:, None] * inv_freq[None, :]
    cos, sin, zero = jnp.cos(ang), jnp.sin(ang), jnp.zeros((t, MLA_ROPE // 2), F32)
    cat = lambda xs: jnp.concatenate(xs, axis=1)
    return cat([cos, cos, zero, zero]), cat([-sin, zero, zero, zero]), cat([zero, sin, zero, zero])


def _row(v, width=None):
    width = v.shape[0] if width is None else width
    return jnp.pad(v.astype(F32), (0, width - v.shape[0])).reshape(1, width)


def _loss_kernel(y, target):
    t, d = y.shape
    tm = min(256, t)

    def body(y_ref, t_ref, part_ref, dy_ref):
        e = y_ref[...] - t_ref[...]
        dy_ref[...] = e * (1.0 / d)

        @pl.when(pl.program_id(0) == 0)
        def _():
            part_ref[...] = jnp.zeros_like(part_ref)

        part_ref[...] += jnp.sum(e * e, axis=0, keepdims=True)

    blk = pl.BlockSpec((tm, d), lambda i: (i, 0))
    one = pl.BlockSpec((1, d), lambda i: (0, 0))
    return pl.pallas_call(body, grid=(t // tm,), in_specs=[blk, blk], out_specs=[one, blk],
                          out_shape=[jax.ShapeDtypeStruct((1, d), F32), jax.ShapeDtypeStruct((t, d), F32)],
                          name="loss", compiler_params=_params(1))(y, target)


def _ffn_fwd(p, x, w, sm):
    h, = _ew(p + "ffn_norm", _norm_fn, [x], [sm["ffn_norm"]], [(D_MODEL, BF16)])
    gu = _mm(p + "ffn_gu", h, w["ffn_w_gate_up"], "nn")
    act, = _ew(p + "ffn_act", _swiglu_fn, [gu], [], [(FFN_HIDDEN, BF16)], tm=128)
    y = _mm(p + "ffn_down", act, w["ffn_w_down"], "nn", add=x)
    return y, (x, h, gu, act)


def _ffn_bwd(p, saved, dy, w, sm, grads):
    x, h, gu, act = saved
    dact = _mm(p + "ffn_down_dx", dy, w["ffn_w_down"], "nt")
    grads["ffn_w_down"] = _mm(p + "ffn_down_dw", act, dy, "tn")
    (dgu,), _ = _ew_bwd(p + "ffn_act_bwd", _swiglu_fn, [gu], [], [dact], tm=128)
    dh = _mm(p + "ffn_gu_dx", dgu, w["ffn_w_gate_up"], "nt")
    grads["ffn_w_gate_up"] = _mm(p + "ffn_gu_dw", h, dgu, "tn")
    (dx,), (dg,) = _ew_bwd(p + "ffn_norm_bwd", _norm_fn, [x], [sm["ffn_norm"]], [dh], add=dy)
    grads["ffn_norm"] = dg
    return dx


def _dn_layer_fwd(p, x, w, sm):
    h, = _ew(p + "mix_norm", _norm_fn, [x], [sm["mix_norm"]], [(D_MODEL, BF16)])
    proj = _mm(p + "dn_in", h, w["dn_w_in"], "nn")
    conv = _conv_fwd(p + "dn_conv", proj, sm["dn_conv_w"])
    ab = _V(proj, LANES, 4 * N_HEADS)
    wide = N_HEADS * HEAD
    q, k, v, g, beta = _ew(p + "dn_pre", _dn_pre_fn, [conv, ab], [sm["dn_a_log"], sm["dn_dt_bias"]], [(wide, F32)] * 5)
    o, states = _dn_fwd(p + "dn_core", q, k, v, g, beta)
    z = _V(proj, wide, 3)
    on, = _ew(p + "dn_post", _dn_post_fn, [o, z], [sm["dn_out_norm"]], [(wide, BF16)])
    y = _mm(p + "dn_out", on, w["dn_w_out"], "nn", add=x)
    return y, (x, h, proj, conv, q, k, v, g, beta, o, states, on)


def _dn_layer_bwd(p, saved, dy, w, sm, grads):
    x, h, proj, conv, q, k, v, g, beta, o, states, on = saved
    wide = N_HEADS * HEAD
    don = _mm(p + "dn_out_dx", dy, w["dn_w_out"], "nt")
    grads["dn_w_out"] = _mm(p + "dn_out_dw", on, dy, "tn")
    (do, dz), (d_out_norm,) = _ew_bwd(p + "dn_post_bwd", _dn_post_fn, [o, _V(proj, wide, 3)], [sm["dn_out_norm"]], [don])
    grads["dn_out_norm"] = d_out_norm
    dq, dk, dv, dg, db = _dn_bwd(p + "dn_core_bwd", q, k, v, g, beta, states, do)
    (dconv, dab), (d_a_log, d_dt) = _ew_bwd(p + "dn_pre_bwd", _dn_pre_fn, [conv, _V(proj, LANES, 4 * N_HEADS)],
                                            [sm["dn_a_log"], sm["dn_dt_bias"]], [dq, dk, dv, dg, db])
    grads["dn_a_log"], grads["dn_dt_bias"] = d_a_log, d_dt
    dqkv, dconv_w = _conv_bwd(p + "dn_conv_bwd", proj, sm["dn_conv_w"], dconv)
    grads["dn_conv_w"] = dconv_w
    dproj = jnp.concatenate([dqkv, dz, dab], axis=1)
    dh = _mm(p + "dn_in_dx", dproj, w["dn_w_in"], "nt")
    grads["dn_w_in"] = _mm(p + "dn_in_dw", h, dproj, "tn")
    (dx,), (dgain,) = _ew_bwd(p + "mix_norm_bwd", _norm_fn, [x], [sm["mix_norm"]], [dh], add=dy)
    grads["mix_norm"] = dgain
    return dx


def _sb_layer_fwd(p, x, w, sm):
    h, = _ew(p + "mix_norm", _norm_fn, [x], [sm["mix_norm"]], [(D_MODEL, BF16)])
    qkv = _mm(p + "sb_qkv", h, w["sb_w_qkv"], "nn")
    wide = N_HEADS * HEAD
    q, k, v = _ew(p + "sb_pre", _sb_pre_fn, [qkv], [sm["sb_q_norm"], sm["sb_k_norm"]], [(wide, BF16)] * 3)
    o, stat = _attn_fwd(p + "sb_core", "sb", [q], [(k, False)], (v, 0, 1))
    y = _mm(p + "sb_out", o, w["sb_w_out"], "nn", add=x)
    return y, (x, h, qkv, q, k, v, o, stat)


def _sb_layer_bwd(p, saved, dy, w, sm, grads):
    x, h, qkv, q, k, v, o, stat = saved
    do = _mm(p + "sb_out_dx", dy, w["sb_w_out"], "nt")
    grads["sb_w_out"] = _mm(p + "sb_out_dw", o, dy, "tn")
    (dq,), (dk,), dv = _attn_bwd(p + "sb_core_bwd", "sb", [q], [(k, False)], (v, 0, 1), o, stat, do)
    (dqkv,), (dqn, dkn) = _ew_bwd(p + "sb_pre_bwd", _sb_pre_fn, [qkv], [sm["sb_q_norm"], sm["sb_k_norm"]], [dq, dk, dv])
    grads["sb_q_norm"], grads["sb_k_norm"] = dqn, dkn
    dh = _mm(p + "sb_qkv_dx", dqkv, w["sb_w_qkv"], "nt")
    grads["sb_w_qkv"] = _mm(p + "sb_qkv_dw", h, dqkv, "tn")
    (dx,), (dgain,) = _ew_bwd(p + "mix_norm_bwd", _norm_fn, [x], [sm["mix_norm"]], [dh], add=dy)
    grads["mix_norm"] = dgain
    return dx


def _mla_layer_fwd(p, x, w, sm):
    t = x.shape[0]
    tabs = [_V(a, diff=False) for a in _rope_tables(t)]
    h, = _ew(p + "mix_norm", _norm_fn, [x], [sm["mix_norm"]], [(D_MODEL, BF16)])
    down = _mm(p + "mla_down", h, w["mla_w_down"], "nn")
    sm_a = [sm["mla_q_a_norm"], sm["mla_kv_a_norm"], sm["mla_k_rope_norm"]]
    cq, ckv, kr = _ew(p + "mla_a", _mla_a_fn, [down] + tabs, sm_a, [(256, BF16), (128, BF16), (128, BF16)])
    qf = _mm(p + "mla_uq", cq, w["mla_w_uq"], "nn")
    kvf = _mm(p + "mla_ukv", ckv, w["mla_w_ukv"], "nn")
    sm_b = [sm["mla_q_nope_norm"], sm["mla_q_rope_norm"], sm["mla_k_nope_norm"]]
    wide = N_HEADS * HEAD
    qn, qr, kn, v = _ew(p + "mla_b", _mla_b_fn, [qf, kvf] + tabs, sm_b, [(wide, BF16)] * 4)
    o, stat = _attn_fwd(p + "mla_core", "softmax", [qn, qr], [(kn, False), (kr, True)], (v, 0, 1))
    y = _mm(p + "mla_out", o, w["mla_w_out"], "nn", add=x)
    return y, (x, h, down, cq, ckv, kr, qf, kvf, qn, qr, kn, v, o, stat)


def _mla_layer_bwd(p, saved, dy, w, sm, grads):
    x, h, down, cq, ckv, kr, qf, kvf, qn, qr, kn, v, o, stat = saved
    tabs = [_V(a, diff=False) for a in _rope_tables(x.shape[0])]
    do = _mm(p + "mla_out_dx", dy, w["mla_w_out"], "nt")
    grads["mla_w_out"] = _mm(p + "mla_out_dw", o, dy, "tn")
    (dqn, dqr), (dkn, dkr), dv = _attn_bwd(p + "mla_core_bwd", "softmax", [qn, qr], [(kn, False), (kr, True)],
                                           (v, 0, 1), o, stat, do)
    sm_b = [sm["mla_q_nope_norm"], sm["mla_q_rope_norm"], sm["mla_k_nope_norm"]]
    (dqf, dkvf), dsm_b = _ew_bwd(p + "mla_b_bwd", _mla_b_fn, [qf, kvf] + tabs, sm_b, [dqn, dqr, dkn, dv])
    grads["mla_q_nope_norm"], grads["mla_q_rope_norm"], grads["mla_k_nope_norm"] = dsm_b
    dcq = _mm(p + "mla_uq_dx", dqf, w["mla_w_uq"], "nt")
    grads["mla_w_uq"] = _mm(p + "mla_uq_dw", cq, dqf, "tn")
    dckv = _mm(p + "mla_ukv_dx", dkvf, w["mla_w_ukv"], "nt")
    grads["mla_w_ukv"] = _mm(p + "mla_ukv_dw", ckv, dkvf, "tn")
    sm_a = [sm["mla_q_a_norm"], sm["mla_kv_a_norm"], sm["mla_k_rope_norm"]]
    (ddown,), dsm_a = _ew_bwd(p + "mla_a_bwd", _mla_a_fn, [down] + tabs, sm_a, [dcq, dckv, dkr])
    grads["mla_q_a_norm"], grads["mla_kv_a_norm"], grads["mla_k_rope_norm"] = dsm_a
    dh = _mm(p + "mla_down_dx", ddown, w["mla_w_down"], "nt")
    grads["mla_w_down"] = _mm(p + "mla_down_dw", h, ddown, "tn")
    (dx,), (dgain,) = _ew_bwd(p + "mix_norm_bwd", _norm_fn, [x], [sm["mix_norm"]], [dh], add=dy)
    grads["mix_norm"] = dgain
    return dx


_MIX_FWD = (_dn_layer_fwd, _sb_layer_fwd, _mla_layer_fwd)
_MIX_BWD = (_dn_layer_bwd, _sb_layer_bwd, _mla_layer_bwd)


def _pad_cols(a, n):
    return jnp.pad(a, ((0, 0), (0, n - a.shape[1])))


def _prep_big(name, a):
    if name.endswith("dn_w_in"):
        return _pad_cols(a, 4 * N_HEADS * HEAD + LANES)
    if name.endswith("mla_w_down"):
        return _pad_cols(a, 512)
    if name.endswith("mla_w_uq"):
        a3 = a.reshape(a.shape[0], N_HEADS, MLA_QK)
        return jnp.pad(a3, ((0, 0), (0, 0), (0, 2 * HEAD - MLA_QK))).reshape(a.shape[0], N_HEADS * 2 * HEAD)
    return a


def _unprep_big(name, g):
    if name.endswith("dn_w_in"):
        return g[:, :4 * N_HEADS * HEAD + 2 * N_HEADS]
    if name.endswith("mla_w_down"):
        return g[:, :448]
    if name.endswith("mla_w_uq"):
        return g.reshape(g.shape[0], N_HEADS, 2 * HEAD)[:, :, :MLA_QK].reshape(g.shape[0], N_HEADS * MLA_QK)
    return g


def _prep_small(name, a):
    if name.endswith("dn_conv_w"):
        return jnp.pad(a.astype(F32), ((0, HALO - a.shape[0]), (0, 0)))
    if name.endswith(("dn_a_log", "dn_dt_bias", "mla_q_rope_norm", "mla_k_rope_norm")):
        return _row(a, LANES)
    return _row(a)


def _unprep_small(name, g, like):
    if name.endswith("dn_conv_w"):
        return g[:like.shape[0]]
    return g.reshape(-1)[:like.shape[0]]


def local_step(x, target, big, small):
    layers = []
    for i in range(DEPTH):
        p = "l%d_" % i
        w = {n: _prep_big(n, big[p + n]) for n in _MIXERS[i % 3] + ("ffn_w_gate_up", "ffn_w_down") if p + n in big}
        sm = {n: _prep_small(n, small[p + n]) for n in _MIXERS[i % 3] + ("mix_norm", "ffn_norm") if p + n in small}
        layers.append((p, w, sm))
    saved = []
    for i, (p, w, sm) in enumerate(layers):
        x, s_mix = _MIX_FWD[i % 3](p, x, w, sm)
        x, s_ffn = _ffn_fwd(p, x, w, sm)
        saved.append((s_mix, s_ffn))
    part, dx = _loss_kernel(x, target)
    grads = {}
    for i in reversed(range(DEPTH)):
        p, w, sm = layers[i]
        g = {}
        dx = _ffn_bwd(p, saved[i][1], dx, w, sm, g)
        dx = _MIX_BWD[i % 3](p, saved[i][0], dx, w, sm, g)
        for n, val in g.items():
            grads[p + n] = _unprep_big(p + n, val) if p + n in big else _unprep_small(p + n, val, small[p + n])
    return part, dx, grads


ROW = 1024
BIG_ROWS = 1024


def _pack(arrs, rows_multiple):
    flat = jnp.concatenate([a.reshape(-1) for a in arrs])
    rows = -(-flat.shape[0] // (ROW * rows_multiple)) * rows_multiple
    return jnp.pad(flat, (0, rows * ROW - flat.shape[0])).reshape(rows, ROW)


def _unpack(buf, shapes):
    flat, out, off = buf.reshape(-1), [], 0
    for s in shapes:
        n = math.prod(s)
        out.append(flat[off:off + n].reshape(s))
        off += n
    return out


def _me():
    return lax.axis_index("x"), lax.axis_index("y"), lax.axis_index("c")


def _other_chips(x, y):
    return [(1 - x, y), (x, 1 - y), (1 - x, 1 - y)]


HBM = pl.BlockSpec(memory_space=pl.ANY)


def _gather_shards(packed):
    rows = packed.shape[0]
    half = rows // 2

    def body(x_ref, out_ref, send_sems, recv_sems, local_sem):
        x, y, c = _me()
        sibling, chips = (x, y, 1 - c), _other_chips(x, y)

        def part(px, py, pc):
            return out_ref.at[2 * px + py, pl.ds(pl.multiple_of(pc * half, 16), half), :]

        def copy(k, block, to, src=None):
            return pltpu.make_async_remote_copy(
                src_ref=part(*block) if src is None else src, dst_ref=part(*block),
                send_sem=send_sems.at[k], recv_sem=recv_sems.at[k], device_id=to, device_id_type=MESH)

        mine = pltpu.make_async_copy(x_ref, out_ref.at[2 * x + y], local_sem)
        mine.start()
        my_half = x_ref.at[pl.ds(pl.multiple_of(c * half, 16), half), :]
        first = [copy(j, (x, y, c), (*chip, c), src=my_half) for j, chip in enumerate(chips)]
        for cp in first:
            cp.start()
        passed = [copy(3 + j, (*chip, c), sibling) for j, chip in enumerate(chips)]
        for j, chip in enumerate(chips):
            copy(j, (*chip, c), (x, y, c)).wait_recv()
            passed[j].start()
        for j, chip in enumerate(chips):
            copy(3 + j, (*chip, 1 - c), (x, y, c)).wait_recv()
        for cp in first + passed:
            cp.wait_send()
        mine.wait()

    return pl.pallas_call(
        body, out_shape=jax.ShapeDtypeStruct((N_CHIPS,) + packed.shape, packed.dtype), in_specs=[HBM], out_specs=HBM,
        scratch_shapes=[pltpu.SemaphoreType.DMA((6,)), pltpu.SemaphoreType.DMA((6,)), pltpu.SemaphoreType.DMA],
        name="gather_weights",
    )(packed)


def _swap_halves(g):
    n, rows, _ = g.shape
    half = rows // 2

    def body(g_ref, mine_ref, theirs_ref, send_sem, recv_sem, local_sem):
        x, y, c = _me()
        keep = g_ref.at[:, pl.ds(pl.multiple_of(c * half, 8), half), :]
        give = g_ref.at[:, pl.ds(pl.multiple_of((1 - c) * half, 8), half), :]
        local = pltpu.make_async_copy(keep, mine_ref, local_sem)
        local.start()
        swap = pltpu.make_async_remote_copy(src_ref=give, dst_ref=theirs_ref, send_sem=send_sem, recv_sem=recv_sem,
                                            device_id=(x, y, 1 - c), device_id_type=MESH)
        swap.start()
        swap.wait()
        local.wait()

    shape = jax.ShapeDtypeStruct((n, half, ROW), g.dtype)
    return pl.pallas_call(
        body, out_shape=[shape, shape], in_specs=[HBM], out_specs=[HBM, HBM],
        scratch_shapes=[pltpu.SemaphoreType.DMA, pltpu.SemaphoreType.DMA, pltpu.SemaphoreType.DMA],
        name="grad_swap_halves",
    )(g)


def _scatter_chunks(s16, s32):
    _, half, _ = s16.shape

    def body(s16_ref, s32_ref, own_ref, got_ref, send_sems, recv_sems, local_sem):
        x, y, c = _me()
        local = pltpu.make_async_copy(s32_ref.at[2 * x + y], own_ref, local_sem)
        local.start()
        sends = []
        for j, (px, py) in enumerate(_other_chips(x, y)):
            cp = pltpu.make_async_remote_copy(src_ref=s16_ref.at[2 * px + py], dst_ref=got_ref.at[j],
                                              send_sem=send_sems.at[j], recv_sem=recv_sems.at[j],
                                              device_id=(px, py, c), device_id_type=MESH)
            cp.start()
            sends.append(cp)
        for cp in sends:
            cp.wait()
        local.wait()

    return pl.pallas_call(
        body, out_shape=[jax.ShapeDtypeStruct((half, ROW), F32), jax.ShapeDtypeStruct((3, half, ROW), BF16)],
        in_specs=[HBM, HBM], out_specs=[HBM, HBM],
        scratch_shapes=[pltpu.SemaphoreType.DMA((3,)), pltpu.SemaphoreType.DMA((3,)), pltpu.SemaphoreType.DMA],
        name="grad_scatter",
    )(s16, s32)


def _join_halves(f):
    half = f.shape[0]

    def body(f_ref, out_ref, send_sem, recv_sem, local_sem):
        x, y, c = _me()
        at = lambda pc: out_ref.at[pl.ds(pl.multiple_of(pc * half, 8), half), :]
        local = pltpu.make_async_copy(f_ref, at(c), local_sem)
        local.start()
        push = pltpu.make_async_remote_copy(src_ref=f_ref, dst_ref=at(c), send_sem=send_sem, recv_sem=recv_sem,
                                            device_id=(x, y, 1 - c), device_id_type=MESH)
        push.start()
        push.wait()
        local.wait()

    return pl.pallas_call(
        body, out_shape=jax.ShapeDtypeStruct((2 * half, ROW), F32), in_specs=[HBM], out_specs=HBM,
        scratch_shapes=[pltpu.SemaphoreType.DMA, pltpu.SemaphoreType.DMA, pltpu.SemaphoreType.DMA],
        name="grad_join_halves",
    )(f)


def _all_reduce_small(name, v):
    rows = v.shape[0]

    def body(v_ref, out_ref, slots, send_sems, recv_sems):
        x, y, c = _me()
        me = 4 * x + 2 * y + c
        slots[me] = v_ref[...]
        sends = []
        for r in range(1, 8):
            to = (x ^ (r >> 2), y ^ ((r >> 1) & 1), c ^ (r & 1))
            cp = pltpu.make_async_remote_copy(src_ref=v_ref, dst_ref=slots.at[me], send_sem=send_sems.at[r - 1],
                                              recv_sem=recv_sems.at[r - 1], device_id=to, device_id_type=MESH)
            cp.start()
            sends.append(cp)
        for cp in sends:
            cp.wait()
        total = slots[0]
        for d in range(1, 8):
            total = total + slots[d]
        out_ref[...] = total

    vmem = pl.BlockSpec(memory_space=pltpu.VMEM)
    return pl.pallas_call(
        body, out_shape=jax.ShapeDtypeStruct(v.shape, F32), in_specs=[vmem], out_specs=vmem,
        scratch_shapes=[pltpu.VMEM((8, rows, ROW), F32), pltpu.SemaphoreType.DMA((7,)), pltpu.SemaphoreType.DMA((7,))],
        name=name,
    )(v)


def _add_fn(a, b):
    s = a + b
    return s, s


def _add4_fn(own, g0, g1, g2):
    return ((own + g0.astype(F32)) + g1.astype(F32)) + g2.astype(F32)


def _adam_fn(w, g, m, v):
    m2 = ADAM_B1 * m + (1.0 - ADAM_B1) * g
    v2 = ADAM_B2 * v + (1.0 - ADAM_B2) * (g * g)
    m_hat = m2 / (1.0 - ADAM_B1 ** ADAM_STEP)
    v_hat = v2 / (1.0 - ADAM_B2 ** ADAM_STEP)
    return -ADAM_LR * (m_hat / (jnp.sqrt(v_hat) + ADAM_EPS) + ADAM_WD * w), m2, v2


def _adam(name, w, g, m, v):
    return _ew(name, _adam_fn, [w, g, m, v], [], [(ROW, F32)] * 3, tm=min(512, w.shape[0]))


def _full_shape(name, shard_shape):
    ax = _shard_axis(name)
    return tuple(n * N_CHIPS if k == ax else n for k, n in enumerate(shard_shape))


def _step(a):
    x_i, y_i, _ = _me()
    chip = 2 * x_i + y_i
    big_shapes = [a[n].shape for n in BIG]

    packed = _pack([a[n].astype(BF16) for n in BIG], BIG_ROWS)
    gathered = _gather_shards(packed)
    pieces = [_unpack(gathered[j], big_shapes) for j in range(N_CHIPS)]
    big = {n: jnp.concatenate([pieces[j][k] for j in range(N_CHIPS)], axis=_shard_axis(n)) for k, n in enumerate(BIG)}

    small = {n: a[n] for n in SMALL}
    convs = [n for n in SMALL if n.endswith("dn_conv_w")]
    placed = []
    for n in convs:
        full = jnp.zeros(_full_shape_conv(a[n].shape), F32)
        placed.append(lax.dynamic_update_slice(full, a[n], (0, chip * a[n].shape[1])))
    conv_sum = _all_reduce_small("gather_conv", _pack(placed, 8))
    for n, full in zip(convs, _unpack(conv_sum, [p.shape for p in placed])):
        small[n] = full * 0.5

    part, dx, grads = local_step(a["x"][0], a["loss_target"][0], big, small)
    loss = lax.psum(0.5 * jnp.sum(part) / D_MODEL, ("x", "y", "c"))

    chunks = []
    for j in range(N_CHIPS):
        per = []
        for n, shp in zip(BIG, big_shapes):
            ax = _shard_axis(n)
            per.append(lax.slice_in_dim(grads[n], j * shp[ax], (j + 1) * shp[ax], axis=ax))
        chunks.append(_pack(per, BIG_ROWS))
    g_all = jnp.stack(chunks)
    mine, theirs = _swap_halves(g_all)
    half = mine.shape[1]
    s32, s16 = _ew("grad_pair_sum", _add_fn, [mine.reshape(-1, ROW), theirs.reshape(-1, ROW)], [],
                   [(ROW, F32), (ROW, BF16)], tm=_tile(half))
    own, got = _scatter_chunks(s16.reshape(N_CHIPS, half, ROW), s32.reshape(N_CHIPS, half, ROW))
    reduced_half, = _ew("grad_chip_sum", _add4_fn, [own, got[0], got[1], got[2]], [], [(ROW, F32)], tm=_tile(half))
    g_big = _join_halves(reduced_half)

    g_small_full = _all_reduce_small("reduce_small", _pack([grads[n] for n in SMALL], 8))
    g_small = dict(zip(SMALL, _unpack(g_small_full, [grads[n].shape for n in SMALL])))
    for n in convs:
        g_small[n] = lax.dynamic_slice_in_dim(g_small[n], chip * a[n].shape[1], a[n].shape[1], axis=1)

    pk = lambda prefix, names, mult: _pack([a[prefix + n] for n in names], mult)
    d_big, m_big, v_big = _adam("adam_big", pk("", BIG, BIG_ROWS), g_big, pk("m_", BIG, BIG_ROWS), pk("v_", BIG, BIG_ROWS))
    gs_packed = _pack([g_small[n] for n in SMALL], 8)
    d_sm, m_sm, v_sm = _adam("adam_small", pk("", SMALL, 8), gs_packed, pk("m_", SMALL, 8), pk("v_", SMALL, 8))

    small_shapes = [a[n].shape for n in SMALL]
    outs = {}
    for key, bbuf, sbuf in (("grad_", g_big, gs_packed), ("delta_", d_big, d_sm), ("new_m_", m_big, m_sm), ("new_v_", v_big, v_sm)):
        outs.update({key + n: val for n, val in zip(BIG, _unpack(bbuf, big_shapes))})
        outs.update({key + n: val for n, val in zip(SMALL, _unpack(sbuf, small_shapes))})
    result = [loss, dx[None]]
    for key in ("grad_", "delta_", "new_m_", "new_v_"):
        result += [outs[key + n] for n in WEIGHTS]
    return tuple(result)


def _full_shape_conv(shard_shape):
    return (shard_shape[0], shard_shape[1] * N_CHIPS)


def kernel(x, l0_mix_norm, l0_dn_w_in, l0_dn_conv_w, l0_dn_a_log, l0_dn_dt_bias, l0_dn_out_norm, l0_dn_w_out, l0_ffn_norm, l0_ffn_w_gate_up, l0_ffn_w_down, l1_mix_norm, l1_sb_w_qkv, l1_sb_q_norm, l1_sb_k_norm, l1_sb_w_out, l1_ffn_norm, l1_ffn_w_gate_up, l1_ffn_w_down, l2_mix_norm, l2_mla_w_down, l2_mla_q_a_norm, l2_mla_kv_a_norm, l2_mla_w_uq, l2_mla_w_ukv, l2_mla_q_nope_norm, l2_mla_q_rope_norm, l2_mla_k_nope_norm, l2_mla_k_rope_norm, l2_mla_w_out, l2_ffn_norm, l2_ffn_w_gate_up, l2_ffn_w_down, l3_mix_norm, l3_dn_w_in, l3_dn_conv_w, l3_dn_a_log, l3_dn_dt_bias, l3_dn_out_norm, l3_dn_w_out, l3_ffn_norm, l3_ffn_w_gate_up, l3_ffn_w_down, loss_target, m_l0_mix_norm, m_l0_dn_w_in, m_l0_dn_conv_w, m_l0_dn_a_log, m_l0_dn_dt_bias, m_l0_dn_out_norm, m_l0_dn_w_out, m_l0_ffn_norm, m_l0_ffn_w_gate_up, m_l0_ffn_w_down, m_l1_mix_norm, m_l1_sb_w_qkv, m_l1_sb_q_norm, m_l1_sb_k_norm, m_l1_sb_w_out, m_l1_ffn_norm, m_l1_ffn_w_gate_up, m_l1_ffn_w_down, m_l2_mix_norm, m_l2_mla_w_down, m_l2_mla_q_a_norm, m_l2_mla_kv_a_norm, m_l2_mla_w_uq, m_l2_mla_w_ukv, m_l2_mla_q_nope_norm, m_l2_mla_q_rope_norm, m_l2_mla_k_nope_norm, m_l2_mla_k_rope_norm, m_l2_mla_w_out, m_l2_ffn_norm, m_l2_ffn_w_gate_up, m_l2_ffn_w_down, m_l3_mix_norm, m_l3_dn_w_in, m_l3_dn_conv_w, m_l3_dn_a_log, m_l3_dn_dt_bias, m_l3_dn_out_norm, m_l3_dn_w_out, m_l3_ffn_norm, m_l3_ffn_w_gate_up, m_l3_ffn_w_down, v_l0_mix_norm, v_l0_dn_w_in, v_l0_dn_conv_w, v_l0_dn_a_log, v_l0_dn_dt_bias, v_l0_dn_out_norm, v_l0_dn_w_out, v_l0_ffn_norm, v_l0_ffn_w_gate_up, v_l0_ffn_w_down, v_l1_mix_norm, v_l1_sb_w_qkv, v_l1_sb_q_norm, v_l1_sb_k_norm, v_l1_sb_w_out, v_l1_ffn_norm, v_l1_ffn_w_gate_up, v_l1_ffn_w_down, v_l2_mix_norm, v_l2_mla_w_down, v_l2_mla_q_a_norm, v_l2_mla_kv_a_norm, v_l2_mla_w_uq, v_l2_mla_w_ukv, v_l2_mla_q_nope_norm, v_l2_mla_q_rope_norm, v_l2_mla_k_nope_norm, v_l2_mla_k_rope_norm, v_l2_mla_w_out, v_l2_ffn_norm, v_l2_ffn_w_gate_up, v_l2_ffn_w_down, v_l3_mix_norm, v_l3_dn_w_in, v_l3_dn_conv_w, v_l3_dn_a_log, v_l3_dn_dt_bias, v_l3_dn_out_norm, v_l3_dn_w_out, v_l3_ffn_norm, v_l3_ffn_w_gate_up, v_l3_ffn_w_down):
    return _step(dict(locals()))
```

```python
import functools
import math

import jax
import jax.numpy as jnp
from jax import lax
from jax.experimental import pallas as pl
from jax.experimental.pallas import tpu as pltpu

F32, BF16 = jnp.float32, jnp.bfloat16
MESH = pl.DeviceIdType.MESH

D_MODEL = 1024
N_HEADS = 8
HEAD = 128
FFN_HIDDEN = 2816
DN_CHUNK = 64
NORM_EPS = 1e-6
MLA_ROPE = 64
MLA_QK = 192
ROPE_THETA = 10000.0
ADAM_LR, ADAM_B1, ADAM_B2, ADAM_EPS, ADAM_WD, ADAM_STEP = 0.001, 0.9, 0.999, 1e-08, 0.01, 10
N_CHIPS = 4
LANES = 128
VMEM_LIMIT = 56 * 2 ** 20


def _params(n_grid):
    return pltpu.CompilerParams(dimension_semantics=("arbitrary",) * n_grid, vmem_limit_bytes=VMEM_LIMIT)


_MIXERS = (
    ("dn_w_in", "dn_conv_w", "dn_a_log", "dn_dt_bias", "dn_out_norm", "dn_w_out"),
    ("sb_w_qkv", "sb_q_norm", "sb_k_norm", "sb_w_out"),
    ("mla_w_down", "mla_q_a_norm", "mla_kv_a_norm", "mla_w_uq", "mla_w_ukv", "mla_q_nope_norm",
     "mla_q_rope_norm", "mla_k_nope_norm", "mla_k_rope_norm", "mla_w_out"),
)
DEPTH = 4


def _layer_names(i):
    p = "l%d_" % i
    return [p + "mix_norm"] + [p + n for n in _MIXERS[i % 3]] + [p + "ffn_norm", p + "ffn_w_gate_up", p + "ffn_w_down"]


WEIGHTS = [n for i in range(DEPTH) for n in _layer_names(i)]
_ROW_SHARDED = ("w_out", "ffn_w_down", "mla_w_down")
_COL_SHARDED = ("dn_w_in", "sb_w_qkv", "mla_w_uq", "mla_w_ukv", "ffn_w_gate_up")


def _shard_axis(name):
    if name.endswith(_ROW_SHARDED):
        return 0
    if name.endswith(_COL_SHARDED):
        return 1
    return None


BIG = [n for n in WEIGHTS if _shard_axis(n) is not None]
SMALL = [n for n in WEIGHTS if _shard_axis(n) is None]


_DN = {"nn": (((1,), (0,)), ((), ())), "nt": (((1,), (1,)), ((), ())), "tn": (((0,), (0,)), ((), ()))}


def _dg(a, b, kind):
    return lax.dot_general(a.astype(BF16), b.astype(BF16), _DN[kind], preferred_element_type=F32)


@functools.partial(jax.custom_vjp, nondiff_argnums=(2,))
def bdot(a, b, kind):
    return _dg(a, b, kind)


def _bdot_fwd(a, b, kind):
    return _dg(a, b, kind), (a, b)


def _bdot_bwd(kind, res, ct):
    a, b = res
    if kind == "nn":
        return _dg(ct, b, "nt"), _dg(a, ct, "tn")
    if kind == "nt":
        return _dg(ct, b, "nn"), _dg(ct, a, "tn")
    return _dg(b, ct, "nt"), _dg(a, ct, "nn")


bdot.defvjp(_bdot_fwd, _bdot_bwd)


def _split(a, terms):
    out = []
    for _ in range(terms):
        hi = a.astype(BF16)
        out.append(hi)
        a = a - hi.astype(F32)
    return out


def _xdot(a, b, kind, exact, terms=3):
    if exact == 0:
        return sum(lax.dot_general(a, p, _DN[kind], preferred_element_type=F32) for p in _split(b, terms))
    return sum(lax.dot_general(p, b, _DN[kind], preferred_element_type=F32) for p in _split(a, terms))


def _tri(n, rel):
    r = lax.broadcasted_iota(jnp.int32, (n, n), 0)
    c = lax.broadcasted_iota(jnp.int32, (n, n), 1)
    return {"le": c <= r, "lt": c < r, "ge": c >= r, "gt": c > r}[rel]


@jax.custom_vjp
def cumsum_rows(g):
    return _xdot(_tri(g.shape[0], "le").astype(BF16), g, "nn", 0)


def _cumsum_fwd(g):
    return cumsum_rows(g), None


def _cumsum_bwd(_, ct):
    return (_xdot(_tri(ct.shape[0], "le").astype(BF16), ct, "tn", 0),)


cumsum_rows.defvjp(_cumsum_fwd, _cumsum_bwd)


def _hdot(a, b):
    return jnp.dot(a, b, precision=lax.Precision.HIGHEST, preferred_element_type=F32)


def _unit_lower_inverse(lower):
    n = lower.shape[0]
    eye = (lax.broadcasted_iota(jnp.int32, (n, n), 0) == lax.broadcasted_iota(jnp.int32, (n, n), 1)).astype(F32)
    m = -lower
    p = eye + m
    for _ in range(int(math.log2(n)) - 1):
        m = _hdot(m, m)
        p = p + _hdot(p, m)
    return p


def _rms(x, g, n=None):
    n = x.shape[-1] if n is None else n
    return x * lax.rsqrt(jnp.sum(x * x, axis=-1, keepdims=True) * (1.0 / n) + NORM_EPS) * g


def _l2(x):
    return x * lax.rsqrt(jnp.sum(x * x, axis=-1, keepdims=True) + NORM_EPS)


def _silu(x):
    return x * jax.nn.sigmoid(x)


def _logsig(z):
    return jnp.minimum(z, 0.0) - jnp.log1p(jnp.exp(-jnp.abs(z)))


@jax.custom_vjp
def _rope(x, cos, sin_lo, sin_hi):
    return x * cos + pltpu.roll(x, 96, 1) * sin_lo + pltpu.roll(x, 32, 1) * sin_hi


def _rope_fwd(x, cos, sin_lo, sin_hi):
    return _rope(x, cos, sin_lo, sin_hi), (cos, sin_lo, sin_hi)


def _rope_bwd(res, ct):
    cos, sin_lo, sin_hi = res
    dx = ct * cos + pltpu.roll(ct * sin_lo, 32, 1) + pltpu.roll(ct * sin_hi, 96, 1)
    return dx, jnp.zeros_like(cos), jnp.zeros_like(sin_lo), jnp.zeros_like(sin_hi)


_rope.defvjp(_rope_fwd, _rope_bwd)


def _tile(n, prefs=(512, 384, 256, 128)):
    for t in prefs:
        if n % t == 0:
            return t
    return n


MM_OUT_TILES = (1024, 1408, 512, 384, 256, 128)


def _mm(name, a, b, kind, out_dtype=F32, add=None):
    if kind == "tn":
        (kdim, m), n = a.shape, b.shape[1]
    else:
        (m, kdim), n = a.shape, (b.shape[0] if kind == "nt" else b.shape[1])
    tm, tn, tk = _tile(m, MM_OUT_TILES), _tile(n, MM_OUT_TILES), _tile(kdim)
    nk = kdim // tk
    a_spec = pl.BlockSpec((tk, tm), lambda i, j, k: (k, i)) if kind == "tn" else pl.BlockSpec((tm, tk), lambda i, j, k: (i, k))
    b_spec = pl.BlockSpec((tn, tk), lambda i, j, k: (j, k)) if kind == "nt" else pl.BlockSpec((tk, tn), lambda i, j, k: (k, j))
    o_spec = pl.BlockSpec((tm, tn), lambda i, j, k: (i, j))
    has_add = add is not None

    def body(*refs):
        a_ref, b_ref = refs[0], refs[1]
        o_ref, acc = refs[-2], refs[-1]
        k = pl.program_id(2)

        @pl.when(k == 0)
        def _():
            acc[...] = jnp.zeros_like(acc)

        acc[...] += _dg(a_ref[...], b_ref[...], kind)

        @pl.when(k == nk - 1)
        def _():
            r = acc[...]
            if has_add:
                r = r + refs[2][...]
            o_ref[...] = r.astype(o_ref.dtype)

    return pl.pallas_call(
        body, grid=(m // tm, n // tn, nk),
        in_specs=[a_spec, b_spec] + ([o_spec] if has_add else []),
        out_specs=o_spec, out_shape=jax.ShapeDtypeStruct((m, n), out_dtype),
        scratch_shapes=[pltpu.VMEM((tm, tn), F32)], name=name, compiler_params=_params(3),
    )(*([a, b] + ([add] if has_add else [])))


class _V:
    def __init__(self, arr, w=None, base=0, diff=True):
        self.arr, self.base, self.diff = arr, base, diff
        self.w = arr.shape[1] if w is None else w

    def spec(self, tm):
        return pl.BlockSpec((tm, self.w), lambda i, b=self.base: (i, b))


def _as_views(ins):
    return [v if isinstance(v, _V) else _V(v) for v in ins]


def _tup(r):
    return tuple(r) if isinstance(r, (tuple, list)) else (r,)


def _ew(name, fn, ins, smalls, outs, tm=256):
    ins = _as_views(ins)
    t = ins[0].arr.shape[0]
    tm = min(tm, t)
    n_in = len(ins) + len(smalls)

    def body(*refs):
        res = _tup(fn(*[r[...] for r in refs[:n_in]]))
        for r, o in zip(refs[n_in:], res):
            r[...] = o.astype(r.dtype)

    return pl.pallas_call(
        body, grid=(t // tm,),
        in_specs=[v.spec(tm) for v in ins] + [pl.BlockSpec(s.shape, lambda i: (0, 0)) for s in smalls],
        out_specs=[pl.BlockSpec((tm, w), lambda i: (i, 0)) for w, _ in outs],
        out_shape=[jax.ShapeDtypeStruct((t, w), dt) for w, dt in outs],
        name=name, compiler_params=_params(1),
    )(*[v.arr for v in ins], *smalls)


def _ew_bwd(name, fn, ins, smalls, cts, tm=256, add=None):
    ins = _as_views(ins)
    t = ins[0].arr.shape[0]
    tm = min(tm, t)
    n_in, n_sm = len(ins), len(smalls)
    diff = [k for k, v in enumerate(ins) if v.diff]
    ct_arrs = [c for c in cts if c is not None]
    has_add = add is not None

    def body(*refs):
        vals = [r[...] for r in refs[:n_in]]
        svals = [r[...] for r in refs[n_in:n_in + n_sm]]
        p = n_in + n_sm
        ct_refs = list(refs[p:p + len(ct_arrs)])
        p += len(ct_arrs)
        add_ref = refs[p] if has_add else None
        p += int(has_add)
        din_refs = refs[p:p + len(diff)]
        dsm_refs = refs[p + len(diff):]

        def f(dv, sv):
            full = list(vals)
            for k, d in zip(diff, dv):
                full[k] = d
            return _tup(fn(*full, *sv))

        res, vjp = jax.vjp(f, [vals[k] for k in diff], svals)
        ctv = tuple(jnp.zeros_like(o) if c is None else ct_refs.pop(0)[...].astype(o.dtype) for c, o in zip(cts, res))
        dv, dsv = vjp(ctv)
        for n, (r, d) in enumerate(zip(din_refs, dv)):
            if n == 0 and has_add:
                d = d + add_ref[...]
            r[...] = d.astype(r.dtype)

        @pl.when(pl.program_id(0) == 0)
        def _():
            for r in dsm_refs:
                r[...] = jnp.zeros_like(r)

        for r, d in zip(dsm_refs, dsv):
            r[...] += d

    row = lambda w: pl.BlockSpec((tm, w), lambda i: (i, 0))
    small_specs = [pl.BlockSpec(s.shape, lambda i: (0, 0)) for s in smalls]
    out = pl.pallas_call(
        body, grid=(t // tm,),
        in_specs=[v.spec(tm) for v in ins] + small_specs + [row(c.shape[1]) for c in ct_arrs]
        + ([row(add.shape[1])] if has_add else []),
        out_specs=[row(ins[k].w) for k in diff] + small_specs,
        out_shape=[jax.ShapeDtypeStruct((t, ins[k].w), F32) for k in diff]
        + [jax.ShapeDtypeStruct(s.shape, F32) for s in smalls],
        name=name, compiler_params=_params(1),
    )(*[v.arr for v in ins], *smalls, *ct_arrs, *([add] if has_add else []))
    return out[:len(diff)], out[len(diff):]


BQ = 256
SUM_TERMS = 2


def _cat(parts):
    return parts[0] if len(parts) == 1 else jnp.concatenate(parts, axis=1)


def _attn_specs(qs, ks, v, t):
    q_specs = [pl.BlockSpec((BQ, HEAD), lambda h, i: (i, h)) for _ in qs]
    k_specs = [pl.BlockSpec((t, HEAD), (lambda h, i: (0, 0)) if sh else (lambda h, i: (0, h))) for _, sh in ks]
    v_spec = pl.BlockSpec((t, HEAD), lambda h, i, b=v[1], s=v[2]: (0, b + h * s))
    return q_specs, k_specs, v_spec


def _attn_fwd(name, mode, qs, ks, v):
    t = qs[0].shape[0]
    nq, n = t // BQ, len(qs)
    q_specs, k_specs, v_spec = _attn_specs(qs, ks, v, t)

    def body(*refs):
        q_refs, k_refs, v_ref = refs[:n], refs[n:2 * n], refs[2 * n]
        o_ref, st_ref = refs[2 * n + 1], refs[2 * n + 2]
        i = pl.program_id(1)
        q = _cat([r[...] for r in q_refs])
        row = i * BQ + lax.broadcasted_iota(jnp.int32, (BQ, BQ), 0)
        col0 = lax.broadcasted_iota(jnp.int32, (BQ, BQ), 1)

        def logits(j):
            off = pl.multiple_of(j * BQ, BQ)
            z = _dg(q, _cat([kr[pl.ds(off, BQ), :] for kr in k_refs]), "nt")
            return z, v_ref[pl.ds(off, BQ), :], j * BQ + col0

        if mode == "sb":
            after = _tri(BQ, "lt").astype(BF16)

            def step(s, carry):
                acc, run = carry
                z, vj, col = logits(i - s)
                past = col < row
                lsz = _logsig(z)
                stay = jnp.where(past, lsz - z, 0.0)
                later = run + _xdot(stay, after, "nn", 1, SUM_TERMS)
                a = jnp.where(past, jnp.exp(lsz + later), 0.0)
                return acc + _dg(a, vj, "nn"), run + jnp.sum(stay, axis=1, keepdims=True)

            acc, run = lax.fori_loop(0, i + 1, step, (jnp.zeros((BQ, HEAD), F32), jnp.zeros((BQ, 1), F32)))
            o_ref[...] = acc
            st_ref[...] = jnp.broadcast_to(run, (BQ, HEAD))
        else:
            def step(j, carry):
                m, l, acc = carry
                z, vj, col = logits(j)
                z = jnp.where(col <= row, z, -1e30)
                m2 = jnp.maximum(m, jnp.max(z, axis=1, keepdims=True))
                p = jnp.exp(z - m2)
                alpha = jnp.exp(m - m2)
                return m2, alpha * l + jnp.sum(p, axis=1, keepdims=True), alpha * acc + _dg(p, vj, "nn")

            m, l, acc = lax.fori_loop(0, i + 1, step, (jnp.full((BQ, 1), -1e30, F32), jnp.zeros((BQ, 1), F32),
                                                       jnp.zeros((BQ, HEAD), F32)))
            o_ref[...] = acc / l
            st_ref[...] = jnp.broadcast_to(m + jnp.log(l), (BQ, HEAD))

    blk = pl.BlockSpec((BQ, HEAD), lambda h, i: (i, h))
    return pl.pallas_call(
        body, grid=(N_HEADS, nq), in_specs=q_specs + k_specs + [v_spec], out_specs=[blk, blk],
        out_shape=[jax.ShapeDtypeStruct((t, N_HEADS * HEAD), F32)] * 2, name=name, compiler_params=_params(2),
    )(*qs, *[k for k, _ in ks], v[0])


def _attn_bwd(name, mode, qs, ks, v, o, stat, do):
    t = qs[0].shape[0]
    nq, n = t // BQ, len(qs)
    q_specs, k_specs, v_spec = _attn_specs(qs, ks, v, t)
    shared = [sh for _, sh in ks]

    def body(*refs):
        q_refs, k_refs, v_ref = refs[:n], refs[n:2 * n], refs[2 * n]
        o_ref, st_ref, do_ref = refs[2 * n + 1:2 * n + 4]
        dq_refs = refs[2 * n + 4:3 * n + 4]
        dk_refs = refs[3 * n + 4:4 * n + 4]
        dv_ref = refs[4 * n + 4]
        h, i = pl.program_id(0), pl.program_id(1)

        @pl.when(i == 0)
        def _():
            dv_ref[...] = jnp.zeros_like(dv_ref)
            for r, sh in zip(dk_refs, shared):
                if not sh:
                    r[...] = jnp.zeros_like(r)

        for r, sh in zip(dk_refs, shared):
            if sh:
                @pl.when((i == 0) & (h == 0))
                def _(r=r):
                    r[...] = jnp.zeros_like(r)

        q = _cat([r[...] for r in q_refs])
        do_t = do_ref[...]
        st = st_ref[:, :1]
        row = i * BQ + lax.broadcasted_iota(jnp.int32, (BQ, BQ), 0)
        col0 = lax.broadcasted_iota(jnp.int32, (BQ, BQ), 1)
        if mode == "sb":
            upto = _tri(BQ, "ge").astype(BF16)
            before = _tri(BQ, "gt").astype(BF16)
        else:
            dsum = jnp.sum(do_t * o_ref[...], axis=1, keepdims=True)

        def step(j, carry):
            off = pl.multiple_of(j * BQ, BQ)
            kj = _cat([kr[pl.ds(off, BQ), :] for kr in k_refs])
            vj = v_ref[pl.ds(off, BQ), :]
            z = _dg(q, kj, "nt")
            col = j * BQ + col0
            da = _dg(do_t, vj, "nt")
            if mode == "sb":
                dq, pre, gpre = carry[0], carry[1], carry[2]
                past = col < row
                lsz = _logsig(z)
                stay = jnp.where(past, lsz - z, 0.0)
                later = st - (pre + _xdot(stay, upto, "nn", 1, SUM_TERMS))
                a = jnp.where(past, jnp.exp(lsz + later), 0.0)
                g = a * da
                gbefore = gpre + _xdot(g, before, "nn", 1, SUM_TERMS)
                sig = jnp.exp(lsz)
                dz = jnp.where(past, g * (1.0 - sig) - sig * gbefore, 0.0)
                tail = (pre + jnp.sum(stay, axis=1, keepdims=True), gpre + jnp.sum(g, axis=1, keepdims=True))
            else:
                dq = carry[0]
                a = jnp.where(col <= row, jnp.exp(z - st), 0.0)
                dz = a * (da - dsum)
                tail = ()
            dq = dq + _dg(dz, kj, "nn")
            dk = _dg(dz, q, "tn")
            for p, r in enumerate(dk_refs):
                r[pl.ds(off, BQ), :] += dk[:, p * HEAD:(p + 1) * HEAD]
            dv_ref[pl.ds(off, BQ), :] += _dg(a, do_t, "tn")
            return (dq,) + tail

        zero = jnp.zeros((BQ, 1), F32)
        init = (jnp.zeros((BQ, n * HEAD), F32),) + ((zero, zero) if mode == "sb" else ())
        dq = lax.fori_loop(0, i + 1, step, init)[0]
        for p, r in enumerate(dq_refs):
            r[...] = dq[:, p * HEAD:(p + 1) * HEAD]

    blk = pl.BlockSpec((BQ, HEAD), lambda h, i: (i, h))
    per_head = pl.BlockSpec((t, HEAD), lambda h, i: (0, h))
    dk_specs = [pl.BlockSpec((t, HEAD), lambda h, i: (0, 0)) if sh else per_head for sh in shared]
    wide = jax.ShapeDtypeStruct((t, N_HEADS * HEAD), F32)
    out = pl.pallas_call(
        body, grid=(N_HEADS, nq), in_specs=q_specs + k_specs + [v_spec, blk, blk, blk],
        out_specs=[blk] * n + dk_specs + [per_head],
        out_shape=[wide] * n + [jax.ShapeDtypeStruct((t, HEAD), F32) if sh else wide for sh in shared] + [wide],
        name=name, compiler_params=_params(2),
    )(*qs, *[k for k, _ in ks], v[0], o, stat, do)
    return out[:n], out[n:2 * n], out[2 * n]


def _dn_chunk(q, k, v, g, beta, state):
    c = q.shape[0]
    gc = cumsum_rows(g)
    gcc = gc[:, :c]
    diff = gcc - gcc.T
    causal, strict = _tri(c, "le"), _tri(c, "lt")
    decay = jnp.where(causal, jnp.exp(jnp.where(causal, diff, 0.0)), 0.0)
    kb = k * beta
    lower = jnp.where(strict, bdot(kb, k, "nt") * decay, 0.0)
    tinv = _unit_lower_inverse(lower)
    eg = jnp.exp(gc)
    u = _hdot(tinv, v * beta)
    w = _hdot(tinv, kb * eg)
    attn = bdot(q, k, "nt") * decay
    glast = gc[c - 1:c, :]
    v_new = u - bdot(w, state, "nn")
    o = bdot(q * eg, state, "nn") + bdot(attn, v_new, "nn")
    new_state = state * jnp.exp(glast) + bdot(k * jnp.exp(glast - gc), v_new, "tn")
    return o, new_state


def _dn_fwd(name, q, k, v, g, beta):
    t = q.shape[0]
    nc = t // DN_CHUNK
    wide = N_HEADS * HEAD
    blk = pl.BlockSpec((DN_CHUNK, wide), lambda n: (n, 0))
    st_spec = pl.BlockSpec((N_HEADS, None, HEAD, HEAD), lambda n: (0, n, 0, 0))

    def body(q_ref, k_ref, v_ref, g_ref, b_ref, o_ref, st_ref, state):
        @pl.when(pl.program_id(0) == 0)
        def _():
            state[...] = jnp.zeros_like(state)

        for h in range(N_HEADS):
            cols = slice(h * HEAD, (h + 1) * HEAD)
            s_in = state[h]
            st_ref[h] = s_in
            o, s_out = _dn_chunk(q_ref[:, cols], k_ref[:, cols], v_ref[:, cols], g_ref[:, cols], b_ref[:, cols], s_in)
            o_ref[:, cols] = o
            state[h] = s_out

    return pl.pallas_call(
        body, grid=(nc,), in_specs=[blk] * 5, out_specs=[blk, st_spec],
        out_shape=[jax.ShapeDtypeStruct((t, wide), F32), jax.ShapeDtypeStruct((N_HEADS, nc, HEAD, HEAD), F32)],
        scratch_shapes=[pltpu.VMEM((N_HEADS, HEAD, HEAD), F32)], name=name, compiler_params=_params(1),
    )(q, k, v, g, beta)


def _dn_bwd(name, q, k, v, g, beta, states, do):
    t = q.shape[0]
    nc = t // DN_CHUNK
    wide = N_HEADS * HEAD
    blk = pl.BlockSpec((DN_CHUNK, wide), lambda n: (nc - 1 - n, 0))
    st_spec = pl.BlockSpec((N_HEADS, None, HEAD, HEAD), lambda n: (0, nc - 1 - n, 0, 0))

    def body(q_ref, k_ref, v_ref, g_ref, b_ref, st_ref, do_ref, dq_ref, dk_ref, dv_ref, dg_ref, db_ref, dstate):
        @pl.when(pl.program_id(0) == 0)
        def _():
            dstate[...] = jnp.zeros_like(dstate)

        for h in range(N_HEADS):
            cols = slice(h * HEAD, (h + 1) * HEAD)
            _, vjp = jax.vjp(_dn_chunk, q_ref[:, cols], k_ref[:, cols], v_ref[:, cols], g_ref[:, cols], b_ref[:, cols], st_ref[h])
            dq, dk, dv, dg, db, ds = vjp((do_ref[:, cols], dstate[h]))
            dq_ref[:, cols], dk_ref[:, cols], dv_ref[:, cols], dg_ref[:, cols], db_ref[:, cols] = dq, dk, dv, dg, db
            dstate[h] = ds

    shape = jax.ShapeDtypeStruct((t, wide), F32)
    return pl.pallas_call(
        body, grid=(nc,), in_specs=[blk] * 5 + [st_spec, blk], out_specs=[blk] * 5, out_shape=[shape] * 5,
        scratch_shapes=[pltpu.VMEM((N_HEADS, HEAD, HEAD), F32)], name=name, compiler_params=_params(1),
    )(q, k, v, g, beta, states, do)


CONV_W = 1024
HALO = 8


def _shift_down(cur, prev, s):
    sh = pltpu.roll(cur, s, 0)
    ph = pltpu.roll(prev, s, 0)
    r = lax.broadcasted_iota(jnp.int32, (HALO, cur.shape[1]), 0)
    return jnp.concatenate([jnp.where(r < s, ph, sh[:HALO]), sh[HALO:]], axis=0)


def _shift_up(cur, nxt, s):
    tm = cur.shape[0]
    sh = pltpu.roll(cur, tm - s, 0)
    nh = pltpu.roll(nxt, HALO - s, 0)
    r = lax.broadcasted_iota(jnp.int32, (HALO, cur.shape[1]), 0)
    return jnp.concatenate([sh[:tm - HALO], jnp.where(r >= HALO - s, nh, sh[tm - HALO:])], axis=0)


def _conv_fwd(name, proj, w, tm=256):
    t = proj.shape[0]
    tm = min(tm, t)
    width = w.shape[1]
    per = tm // HALO

    def body(cur_ref, prev_ref, w_ref, y_ref):
        cur = cur_ref[...]
        prev = jnp.where(pl.program_id(0) > 0, prev_ref[...], 0.0)
        y = cur * w_ref[3:4, :]
        for s in (1, 2, 3):
            y = y + _shift_down(cur, prev, s) * w_ref[3 - s:4 - s, :]
        y_ref[...] = y

    return pl.pallas_call(
        body, grid=(t // tm, width // CONV_W),
        in_specs=[pl.BlockSpec((tm, CONV_W), lambda i, c: (i, c)),
                  pl.BlockSpec((HALO, CONV_W), lambda i, c: (jnp.maximum(i * per - 1, 0), c)),
                  pl.BlockSpec((HALO, CONV_W), lambda i, c: (0, c))],
        out_specs=pl.BlockSpec((tm, CONV_W), lambda i, c: (i, c)),
        out_shape=jax.ShapeDtypeStruct((t, width), F32), name=name, compiler_params=_params(2),
    )(proj, proj, w)


def _conv_bwd(name, proj, w, dy, tm=256):
    t = proj.shape[0]
    tm = min(tm, t)
    width = w.shape[1]
    per, nt = tm // HALO, t // tm

    def body(cur_ref, prev_ref, w_ref, dy_ref, nxt_ref, du_ref, dw_ref):
        i = pl.program_id(1)
        cur, dy_t = cur_ref[...], dy_ref[...]
        prev = jnp.where(i > 0, prev_ref[...], 0.0)
        nxt = jnp.where(i < nt - 1, nxt_ref[...], 0.0)
        du = dy_t * w_ref[3:4, :]
        rows = [jnp.sum(dy_t * cur, axis=0, keepdims=True)]
        for s in (1, 2, 3):
            du = du + _shift_up(dy_t, nxt, s) * w_ref[3 - s:4 - s, :]
            rows.insert(0, jnp.sum(dy_t * _shift_down(cur, prev, s), axis=0, keepdims=True))
        du_ref[...] = du

        @pl.when(i == 0)
        def _():
            dw_ref[...] = jnp.zeros_like(dw_ref)

        dw_ref[...] += jnp.concatenate(rows + [jnp.zeros((HALO - 4, CONV_W), F32)], axis=0)

    return pl.pallas_call(
        body, grid=(width // CONV_W, nt),
        in_specs=[pl.BlockSpec((tm, CONV_W), lambda c, i: (i, c)),
                  pl.BlockSpec((HALO, CONV_W), lambda c, i: (jnp.maximum(i * per - 1, 0), c)),
                  pl.BlockSpec((HALO, CONV_W), lambda c, i: (0, c)),
                  pl.BlockSpec((tm, CONV_W), lambda c, i: (i, c)),
                  pl.BlockSpec((HALO, CONV_W), lambda c, i: (jnp.minimum((i + 1) * per, t // HALO - 1), c))],
        out_specs=[pl.BlockSpec((tm, CONV_W), lambda c, i: (i, c)), pl.BlockSpec((HALO, CONV_W), lambda c, i: (0, c))],
        out_shape=[jax.ShapeDtypeStruct((t, width), F32), jax.ShapeDtypeStruct((HALO, width), F32)],
        name=name, compiler_params=_params(2),
    )(proj, proj, w, dy, dy)


def _norm_fn(x, g):
    return _rms(x, g)


def _swiglu_fn(gu):
    return _silu(gu[:, :FFN_HIDDEN]) * gu[:, FFN_HIDDEN:]


def _heads(x):
    return [x[:, h * HEAD:(h + 1) * HEAD] for h in range(x.shape[1] // HEAD)]


def _dn_pre_fn(c, ab, a_log, dt_bias):
    w = N_HEADS * HEAD
    q = [_l2(_silu(x)) * (HEAD ** -0.5) for x in _heads(c[:, :w])]
    k = [_l2(_silu(x)) for x in _heads(c[:, w:2 * w])]
    v = _silu(c[:, 2 * w:])
    g, beta = [], []
    for h in range(N_HEADS):
        gh = -jnp.exp(a_log[:, h:h + 1]) * jax.nn.softplus(ab[:, h:h + 1] + dt_bias[:, h:h + 1])
        bh = jax.nn.sigmoid(ab[:, N_HEADS + h:N_HEADS + h + 1])
        g.append(jnp.broadcast_to(gh, (c.shape[0], HEAD)))
        beta.append(jnp.broadcast_to(bh, (c.shape[0], HEAD)))
    cat = lambda xs: jnp.concatenate(xs, axis=1)
    return cat(q), cat(k), v, cat(g), cat(beta)


def _dn_post_fn(o, z, out_norm):
    return jnp.concatenate([_rms(oh, out_norm) * _silu(zh) for oh, zh in zip(_heads(o), _heads(z))], axis=1)


def _sb_pre_fn(qkv, q_norm, k_norm):
    w = N_HEADS * HEAD
    q = [_rms(x, q_norm) * (HEAD ** -0.5) for x in _heads(qkv[:, :w])]
    k = [_rms(x, k_norm) for x in _heads(qkv[:, w:2 * w])]
    return jnp.concatenate(q, axis=1), jnp.concatenate(k, axis=1), qkv[:, 2 * w:]


def _mla_a_fn(down, cos, sin_lo, sin_hi, q_a_norm, kv_a_norm, k_rope_norm):
    cq = _rms(down[:, :256], q_a_norm)
    ckv = _rms(down[:, 256:384], kv_a_norm)
    kr = _rope(_rms(down[:, 384:], k_rope_norm, MLA_ROPE), cos, sin_lo, sin_hi)
    return cq, ckv, kr


def _mla_b_fn(qf, kvf, cos, sin_lo, sin_hi, q_nope_norm, q_rope_norm, k_nope_norm):
    scale = MLA_QK ** -0.5
    qn, qr, kn, v = [], [], [], []
    for h in range(N_HEADS):
        a = 2 * h * HEAD
        qn.append(_rms(qf[:, a:a + HEAD], q_nope_norm) * scale)
        qr.append(_rope(_rms(qf[:, a + HEAD:a + 2 * HEAD], q_rope_norm, MLA_ROPE), cos, sin_lo, sin_hi) * scale)
        kn.append(_rms(kvf[:, a:a + HEAD], k_nope_norm))
        v.append(kvf[:, a + HEAD:a + 2 * HEAD])
    cat = lambda xs: jnp.concatenate(xs, axis=1)
    return cat(qn), cat(qr), cat(kn), cat(v)


def _rope_tables(t):
    inv_freq = ROPE_THETA ** (-jnp.arange(0, MLA_ROPE, 2, dtype=F32) / MLA_ROPE)
    ang = jnp.arange(t, dtype=F32)[:, None] * inv_freq[None, :]
    cos, sin, zero = jnp.cos(ang), jnp.sin(ang), jnp.zeros((t, MLA_ROPE // 2), F32)
    cat = lambda xs: jnp.concatenate(xs, axis=1)
    return cat([cos, cos, zero, zero]), cat([-sin, zero, zero, zero]), cat([zero, sin, zero, zero])


def _row(v, width=None):
    width = v.shape[0] if width is None else width
    return jnp.pad(v.astype(F32), (0, width - v.shape[0])).reshape(1, width)


def _loss_kernel(y, target):
    t, d = y.shape
    tm = min(256, t)

    def body(y_ref, t_ref, part_ref, dy_ref):
        e = y_ref[...] - t_ref[...]
        dy_ref[...] = e * (1.0 / d)

        @pl.when(pl.program_id(0) == 0)
        def _():
            part_ref[...] = jnp.zeros_like(part_ref)

        part_ref[...] += jnp.sum(e * e, axis=0, keepdims=True)

    blk = pl.BlockSpec((tm, d), lambda i: (i, 0))
    one = pl.BlockSpec((1, d), lambda i: (0, 0))
    return pl.pallas_call(body, grid=(t // tm,), in_specs=[blk, blk], out_specs=[one, blk],
                          out_shape=[jax.ShapeDtypeStruct((1, d), F32), jax.ShapeDtypeStruct((t, d), F32)],
                          name="loss", compiler_params=_params(1))(y, target)


def _ffn_fwd(p, x, w, sm):
    h, = _ew(p + "ffn_norm", _norm_fn, [x], [sm["ffn_norm"]], [(D_MODEL, BF16)])
    gu = _mm(p + "ffn_gu", h, w["ffn_w_gate_up"], "nn")
    act, = _ew(p + "ffn_act", _swiglu_fn, [gu], [], [(FFN_HIDDEN, BF16)], tm=128)
    y = _mm(p + "ffn_down", act, w["ffn_w_down"], "nn", add=x)
    return y, (x, h, gu, act)


def _ffn_bwd(p, saved, dy, w, sm, grads):
    x, h, gu, act = saved
    dact = _mm(p + "ffn_down_dx", dy, w["ffn_w_down"], "nt")
    grads["ffn_w_down"] = _mm(p + "ffn_down_dw", act, dy, "tn")
    (dgu,), _ = _ew_bwd(p + "ffn_act_bwd", _swiglu_fn, [gu], [], [dact], tm=128)
    dh = _mm(p + "ffn_gu_dx", dgu, w["ffn_w_gate_up"], "nt")
    grads["ffn_w_gate_up"] = _mm(p + "ffn_gu_dw", h, dgu, "tn")
    (dx,), (dg,) = _ew_bwd(p + "ffn_norm_bwd", _norm_fn, [x], [sm["ffn_norm"]], [dh], add=dy)
    grads["ffn_norm"] = dg
    return dx


def _dn_layer_fwd(p, x, w, sm):
    h, = _ew(p + "mix_norm", _norm_fn, [x], [sm["mix_norm"]], [(D_MODEL, BF16)])
    proj = _mm(p + "dn_in", h, w["dn_w_in"], "nn")
    conv = _conv_fwd(p + "dn_conv", proj, sm["dn_conv_w"])
    ab = _V(proj, LANES, 4 * N_HEADS)
    wide = N_HEADS * HEAD
    q, k, v, g, beta = _ew(p + "dn_pre", _dn_pre_fn, [conv, ab], [sm["dn_a_log"], sm["dn_dt_bias"]], [(wide, F32)] * 5)
    o, states = _dn_fwd(p + "dn_core", q, k, v, g, beta)
    z = _V(proj, wide, 3)
    on, = _ew(p + "dn_post", _dn_post_fn, [o, z], [sm["dn_out_norm"]], [(wide, BF16)])
    y = _mm(p + "dn_out", on, w["dn_w_out"], "nn", add=x)
    return y, (x, h, proj, conv, q, k, v, g, beta, o, states, on)


def _dn_layer_bwd(p, saved, dy, w, sm, grads):
    x, h, proj, conv, q, k, v, g, beta, o, states, on = saved
    wide = N_HEADS * HEAD
    don = _mm(p + "dn_out_dx", dy, w["dn_w_out"], "nt")
    grads["dn_w_out"] = _mm(p + "dn_out_dw", on, dy, "tn")
    (do, dz), (d_out_norm,) = _ew_bwd(p + "dn_post_bwd", _dn_post_fn, [o, _V(proj, wide, 3)], [sm["dn_out_norm"]], [don])
    grads["dn_out_norm"] = d_out_norm
    dq, dk, dv, dg, db = _dn_bwd(p + "dn_core_bwd", q, k, v, g, beta, states, do)
    (dconv, dab), (d_a_log, d_dt) = _ew_bwd(p + "dn_pre_bwd", _dn_pre_fn, [conv, _V(proj, LANES, 4 * N_HEADS)],
                                            [sm["dn_a_log"], sm["dn_dt_bias"]], [dq, dk, dv, dg, db])
    grads["dn_a_log"], grads["dn_dt_bias"] = d_a_log, d_dt
    dqkv, dconv_w = _conv_bwd(p + "dn_conv_bwd", proj, sm["dn_conv_w"], dconv)
    grads["dn_conv_w"] = dconv_w
    dproj = jnp.concatenate([dqkv, dz, dab], axis=1)
    dh = _mm(p + "dn_in_dx", dproj, w["dn_w_in"], "nt")
    grads["dn_w_in"] = _mm(p + "dn_in_dw", h, dproj, "tn")
    (dx,), (dgain,) = _ew_bwd(p + "mix_norm_bwd", _norm_fn, [x], [sm["mix_norm"]], [dh], add=dy)
    grads["mix_norm"] = dgain
    return dx


def _sb_layer_fwd(p, x, w, sm):
    h, = _ew(p + "mix_norm", _norm_fn, [x], [sm["mix_norm"]], [(D_MODEL, BF16)])
    qkv = _mm(p + "sb_qkv", h, w["sb_w_qkv"], "nn")
    wide = N_HEADS * HEAD
    q, k, v = _ew(p + "sb_pre", _sb_pre_fn, [qkv], [sm["sb_q_norm"], sm["sb_k_norm"]], [(wide, BF16)] * 3)
    o, stat = _attn_fwd(p + "sb_core", "sb", [q], [(k, False)], (v, 0, 1))
    y = _mm(p + "sb_out", o, w["sb_w_out"], "nn", add=x)
    return y, (x, h, qkv, q, k, v, o, stat)


def _sb_layer_bwd(p, saved, dy, w, sm, grads):
    x, h, qkv, q, k, v, o, stat = saved
    do = _mm(p + "sb_out_dx", dy, w["sb_w_out"], "nt")
    grads["sb_w_out"] = _mm(p + "sb_out_dw", o, dy, "tn")
    (dq,), (dk,), dv = _attn_bwd(p + "sb_core_bwd", "sb", [q], [(k, False)], (v, 0, 1), o, stat, do)
    (dqkv,), (dqn, dkn) = _ew_bwd(p + "sb_pre_bwd", _sb_pre_fn, [qkv], [sm["sb_q_norm"], sm["sb_k_norm"]], [dq, dk, dv])
    grads["sb_q_norm"], grads["sb_k_norm"] = dqn, dkn
    dh = _mm(p + "sb_qkv_dx", dqkv, w["sb_w_qkv"], "nt")
    grads["sb_w_qkv"] = _mm(p + "sb_qkv_dw", h, dqkv, "tn")
    (dx,), (dgain,) = _ew_bwd(p + "mix_norm_bwd", _norm_fn, [x], [sm["mix_norm"]], [dh], add=dy)
    grads["mix_norm"] = dgain
    return dx


def _mla_layer_fwd(p, x, w, sm):
    t = x.shape[0]
    tabs = [_V(a, diff=False) for a in _rope_tables(t)]
    h, = _ew(p + "mix_norm", _norm_fn, [x], [sm["mix_norm"]], [(D_MODEL, BF16)])
    down = _mm(p + "mla_down", h, w["mla_w_down"], "nn")
    sm_a = [sm["mla_q_a_norm"], sm["mla_kv_a_norm"], sm["mla_k_rope_norm"]]
    cq, ckv, kr = _ew(p + "mla_a", _mla_a_fn, [down] + tabs, sm_a, [(256, BF16), (128, BF16), (128, BF16)])
    qf = _mm(p + "mla_uq", cq, w["mla_w_uq"], "nn")
    kvf = _mm(p + "mla_ukv", ckv, w["mla_w_ukv"], "nn")
    sm_b = [sm["mla_q_nope_norm"], sm["mla_q_rope_norm"], sm["mla_k_nope_norm"]]
    wide = N_HEADS * HEAD
    qn, qr, kn, v = _ew(p + "mla_b", _mla_b_fn, [qf, kvf] + tabs, sm_b, [(wide, BF16)] * 4)
    o, stat = _attn_fwd(p + "mla_core", "softmax", [qn, qr], [(kn, False), (kr, True)], (v, 0, 1))
    y = _mm(p + "mla_out", o, w["mla_w_out"], "nn", add=x)
    return y, (x, h, down, cq, ckv, kr, qf, kvf, qn, qr, kn, v, o, stat)


def _mla_layer_bwd(p, saved, dy, w, sm, grads):
    x, h, down, cq, ckv, kr, qf, kvf, qn, qr, kn, v, o, stat = saved
    tabs = [_V(a, diff=False) for a in _rope_tables(x.shape[0])]
    do = _mm(p + "mla_out_dx", dy, w["mla_w_out"], "nt")
    grads["mla_w_out"] = _mm(p + "mla_out_dw", o, dy, "tn")
    (dqn, dqr), (dkn, dkr), dv = _attn_bwd(p + "mla_core_bwd", "softmax", [qn, qr], [(kn, False), (kr, True)],
                                           (v, 0, 1), o, stat, do)
    sm_b = [sm["mla_q_nope_norm"], sm["mla_q_rope_norm"], sm["mla_k_nope_norm"]]
    (dqf, dkvf), dsm_b = _ew_bwd(p + "mla_b_bwd", _mla_b_fn, [qf, kvf] + tabs, sm_b, [dqn, dqr, dkn, dv])
    grads["mla_q_nope_norm"], grads["mla_q_rope_norm"], grads["mla_k_nope_norm"] = dsm_b
    dcq = _mm(p + "mla_uq_dx", dqf, w["mla_w_uq"], "nt")
    grads["mla_w_uq"] = _mm(p + "mla_uq_dw", cq, dqf, "tn")
    dckv = _mm(p + "mla_ukv_dx", dkvf, w["mla_w_ukv"], "nt")
    grads["mla_w_ukv"] = _mm(p + "mla_ukv_dw", ckv, dkvf, "tn")
    sm_a = [sm["mla_q_a_norm"], sm["mla_kv_a_norm"], sm["mla_k_rope_norm"]]
    (ddown,), dsm_a = _ew_bwd(p + "mla_a_bwd", _mla_a_fn, [down] + tabs, sm_a, [dcq, dckv, dkr])
    grads["mla_q_a_norm"], grads["mla_kv_a_norm"], grads["mla_k_rope_norm"] = dsm_a
    dh = _mm(p + "mla_down_dx", ddown, w["mla_w_down"], "nt")
    grads["mla_w_down"] = _mm(p + "mla_down_dw", h, ddown, "tn")
    (dx,), (dgain,) = _ew_bwd(p + "mix_norm_bwd", _norm_fn, [x], [sm["mix_norm"]], [dh], add=dy)
    grads["mix_norm"] = dgain
    return dx


_MIX_FWD = (_dn_layer_fwd, _sb_layer_fwd, _mla_layer_fwd)
_MIX_BWD = (_dn_layer_bwd, _sb_layer_bwd, _mla_layer_bwd)


def _pad_cols(a, n):
    return jnp.pad(a, ((0, 0), (0, n - a.shape[1])))


def _prep_big(name, a):
    if name.endswith("dn_w_in"):
        return _pad_cols(a, 4 * N_HEADS * HEAD + LANES)
    if name.endswith("mla_w_down"):
        return _pad_cols(a, 512)
    if name.endswith("mla_w_uq"):
        a3 = a.reshape(a.shape[0], N_HEADS, MLA_QK)
        return jnp.pad(a3, ((0, 0), (0, 0), (0, 2 * HEAD - MLA_QK))).reshape(a.shape[0], N_HEADS * 2 * HEAD)
    return a


def _unprep_big(name, g):
    if name.endswith("dn_w_in"):
        return g[:, :4 * N_HEADS * HEAD + 2 * N_HEADS]
    if name.endswith("mla_w_down"):
        return g[:, :448]
    if name.endswith("mla_w_uq"):
        return g.reshape(g.shape[0], N_HEADS, 2 * HEAD)[:, :, :MLA_QK].reshape(g.shape[0], N_HEADS * MLA_QK)
    return g


def _prep_small(name, a):
    if name.endswith("dn_conv_w"):
        return jnp.pad(a.astype(F32), ((0, HALO - a.shape[0]), (0, 0)))
    if name.endswith(("dn_a_log", "dn_dt_bias", "mla_q_rope_norm", "mla_k_rope_norm")):
        return _row(a, LANES)
    return _row(a)


def _unprep_small(name, g, like):
    if name.endswith("dn_conv_w"):
        return g[:like.shape[0]]
    return g.reshape(-1)[:like.shape[0]]


def local_step(x, target, big, small):
    layers = []
    for i in range(DEPTH):
        p = "l%d_" % i
        w = {n: _prep_big(n, big[p + n]) for n in _MIXERS[i % 3] + ("ffn_w_gate_up", "ffn_w_down") if p + n in big}
        sm = {n: _prep_small(n, small[p + n]) for n in _MIXERS[i % 3] + ("mix_norm", "ffn_norm") if p + n in small}
        layers.append((p, w, sm))
    saved = []
    for i, (p, w, sm) in enumerate(layers):
        x, s_mix = _MIX_FWD[i % 3](p, x, w, sm)
        x, s_ffn = _ffn_fwd(p, x, w, sm)
        saved.append((s_mix, s_ffn))
    part, dx = _loss_kernel(x, target)
    grads = {}
    for i in reversed(range(DEPTH)):
        p, w, sm = layers[i]
        g = {}
        dx = _ffn_bwd(p, saved[i][1], dx, w, sm, g)
        dx = _MIX_BWD[i % 3](p, saved[i][0], dx, w, sm, g)
        for n, val in g.items():
            grads[p + n] = _unprep_big(p + n, val) if p + n in big else _unprep_small(p + n, val, small[p + n])
    return part, dx, grads


ROW = 1024
BIG_ROWS = 1024


def _pack(arrs, rows_multiple):
    flat = jnp.concatenate([a.reshape(-1) for a in arrs])
    rows = -(-flat.shape[0] // (ROW * rows_multiple)) * rows_multiple
    return jnp.pad(flat, (0, rows * ROW - flat.shape[0])).reshape(rows, ROW)


def _unpack(buf, shapes):
    flat, out, off = buf.reshape(-1), [], 0
    for s in shapes:
        n = math.prod(s)
        out.append(flat[off:off + n].reshape(s))
        off += n
    return out


def _me():
    return lax.axis_index("x"), lax.axis_index("y"), lax.axis_index("c")


def _other_chips(x, y):
    return [(1 - x, y), (x, 1 - y), (1 - x, 1 - y)]


HBM = pl.BlockSpec(memory_space=pl.ANY)


def _gather_shards(packed):
    rows = packed.shape[0]
    half = rows // 2

    def body(x_ref, out_ref, send_sems, recv_sems, local_sem):
        x, y, c = _me()
        sibling, chips = (x, y, 1 - c), _other_chips(x, y)

        def part(px, py, pc):
            return out_ref.at[2 * px + py, pl.ds(pl.multiple_of(pc * half, 16), half), :]

        def copy(k, block, to, src=None):
            return pltpu.make_async_remote_copy(
                src_ref=part(*block) if src is None else src, dst_ref=part(*block),
                send_sem=send_sems.at[k], recv_sem=recv_sems.at[k], device_id=to, device_id_type=MESH)

        mine = pltpu.make_async_copy(x_ref, out_ref.at[2 * x + y], local_sem)
        mine.start()
        my_half = x_ref.at[pl.ds(pl.multiple_of(c * half, 16), half), :]
        first = [copy(j, (x, y, c), (*chip, c), src=my_half) for j, chip in enumerate(chips)]
        for cp in first:
            cp.start()
        passed = [copy(3 + j, (*chip, c), sibling) for j, chip in enumerate(chips)]
        for j, chip in enumerate(chips):
            copy(j, (*chip, c), (x, y, c)).wait_recv()
            passed[j].start()
        for j, chip in enumerate(chips):
            copy(3 + j, (*chip, 1 - c), (x, y, c)).wait_recv()
        for cp in first + passed:
            cp.wait_send()
        mine.wait()

    return pl.pallas_call(
        body, out_shape=jax.ShapeDtypeStruct((N_CHIPS,) + packed.shape, packed.dtype), in_specs=[HBM], out_specs=HBM,
        scratch_shapes=[pltpu.SemaphoreType.DMA((6,)), pltpu.SemaphoreType.DMA((6,)), pltpu.SemaphoreType.DMA],
        name="gather_weights",
    )(packed)


D2D_STREAMS = 16


def _swap_halves(g):
    n, rows, _ = g.shape
    half = rows // 2
    per = D2D_STREAMS // n
    piece = half // per

    def body(g_ref, theirs_ref, send_sems, recv_sems):
        x, y, c = _me()
        give = (1 - c) * half
        copies = []
        for j in range(n):
            for p in range(per):
                k = j * per + p
                cp = pltpu.make_async_remote_copy(
                    src_ref=g_ref.at[j, pl.ds(pl.multiple_of(give + p * piece, 8), piece), :],
                    dst_ref=theirs_ref.at[j, pl.ds(p * piece, piece), :],
                    send_sem=send_sems.at[k], recv_sem=recv_sems.at[k], device_id=(x, y, 1 - c), device_id_type=MESH)
                cp.start()
                copies.append(cp)
        for cp in copies:
            cp.wait()

    return pl.pallas_call(
        body, out_shape=jax.ShapeDtypeStruct((n, half, ROW), g.dtype), in_specs=[HBM], out_specs=HBM,
        scratch_shapes=[pltpu.SemaphoreType.DMA((D2D_STREAMS,)), pltpu.SemaphoreType.DMA((D2D_STREAMS,))],
        name="grad_swap_halves",
    )(g)


def _pair_sum(g, theirs, c):
    n, half, _ = theirs.shape
    tm = _tile(half)
    nb = half // tm

    def body(c_ref, g_ref, t_ref, s32_ref, s16_ref):
        s = g_ref[...] + t_ref[...]
        s32_ref[...] = s
        s16_ref[...] = s.astype(BF16)

    blk = pl.BlockSpec((None, tm, ROW), lambda j, i, c_ref: (j, i, 0))
    return pl.pallas_call(
        body,
        grid_spec=pltpu.PrefetchScalarGridSpec(
            num_scalar_prefetch=1, grid=(n, nb),
            in_specs=[pl.BlockSpec((None, tm, ROW), lambda j, i, c_ref: (j, c_ref[0] * nb + i, 0)), blk],
            out_specs=[blk, blk]),
        out_shape=[jax.ShapeDtypeStruct(theirs.shape, F32), jax.ShapeDtypeStruct(theirs.shape, BF16)],
        name="grad_pair_sum", compiler_params=_params(2),
    )(c.reshape(1).astype(jnp.int32), g, theirs)


def _scatter_chunks(s16):
    _, half, _ = s16.shape

    def body(s16_ref, got_ref, send_sems, recv_sems):
        x, y, c = _me()
        sends = []
        for j, (px, py) in enumerate(_other_chips(x, y)):
            cp = pltpu.make_async_remote_copy(src_ref=s16_ref.at[2 * px + py], dst_ref=got_ref.at[j],
                                              send_sem=send_sems.at[j], recv_sem=recv_sems.at[j],
                                              device_id=(px, py, c), device_id_type=MESH)
            cp.start()
            sends.append(cp)
        for cp in sends:
            cp.wait()

    return pl.pallas_call(
        body, out_shape=jax.ShapeDtypeStruct((3, half, ROW), BF16), in_specs=[HBM], out_specs=HBM,
        scratch_shapes=[pltpu.SemaphoreType.DMA((3,)), pltpu.SemaphoreType.DMA((3,))],
        name="grad_scatter",
    )(s16)


def _chip_sum(s32, got, chip):
    _, half, _ = s32.shape
    tm = _tile(half)

    def body(chip_ref, own_ref, g0_ref, g1_ref, g2_ref, o_ref):
        o_ref[...] = ((own_ref[...] + g0_ref[...].astype(F32)) + g1_ref[...].astype(F32)) + g2_ref[...].astype(F32)

    got_spec = lambda k: pl.BlockSpec((None, tm, ROW), lambda i, chip_ref, k=k: (k, i, 0))
    return pl.pallas_call(
        body,
        grid_spec=pltpu.PrefetchScalarGridSpec(
            num_scalar_prefetch=1, grid=(half // tm,),
            in_specs=[pl.BlockSpec((None, tm, ROW), lambda i, chip_ref: (chip_ref[0], i, 0)), got_spec(0), got_spec(1), got_spec(2)],
            out_specs=pl.BlockSpec((tm, ROW), lambda i, chip_ref: (i, 0))),
        out_shape=jax.ShapeDtypeStruct((half, ROW), F32), name="grad_chip_sum", compiler_params=_params(1),
    )(chip.reshape(1).astype(jnp.int32), s32, got, got, got)


def _join_halves(f):
    half = f.shape[0]
    per = D2D_STREAMS // 2
    piece = half // per

    def body(f_ref, out_ref, send_sems, recv_sems, local_sems):
        x, y, c = _me()
        copies = []
        for p in range(per):
            src = f_ref.at[pl.ds(p * piece, piece), :]
            dst = out_ref.at[pl.ds(pl.multiple_of(c * half + p * piece, 8), piece), :]
            local = pltpu.make_async_copy(src, dst, local_sems.at[p])
            push = pltpu.make_async_remote_copy(src_ref=src, dst_ref=dst, send_sem=send_sems.at[p], recv_sem=recv_sems.at[p],
                                                device_id=(x, y, 1 - c), device_id_type=MESH)
            local.start()
            push.start()
            copies += [local, push]
        for cp in copies:
            cp.wait()

    return pl.pallas_call(
        body, out_shape=jax.ShapeDtypeStruct((2 * half, ROW), F32), in_specs=[HBM], out_specs=HBM,
        scratch_shapes=[pltpu.SemaphoreType.DMA((per,)), pltpu.SemaphoreType.DMA((per,)), pltpu.SemaphoreType.DMA((per,))],
        name="grad_join_halves",
    )(f)


def _all_reduce_small(name, v):
    rows = v.shape[0]

    def body(v_ref, out_ref, slots, send_sems, recv_sems):
        x, y, c = _me()
        me = 4 * x + 2 * y + c
        slots[me] = v_ref[...]
        sends = []
        for r in range(1, 8):
            to = (x ^ (r >> 2), y ^ ((r >> 1) & 1), c ^ (r & 1))
            cp = pltpu.make_async_remote_copy(src_ref=v_ref, dst_ref=slots.at[me], send_sem=send_sems.at[r - 1],
                                              recv_sem=recv_sems.at[r - 1], device_id=to, device_id_type=MESH)
            cp.start()
            sends.append(cp)
        for cp in sends:
            cp.wait()
        total = slots[0]
        for d in range(1, 8):
            total = total + slots[d]
        out_ref[...] = total

    vmem = pl.BlockSpec(memory_space=pltpu.VMEM)
    return pl.pallas_call(
        body, out_shape=jax.ShapeDtypeStruct(v.shape, F32), in_specs=[vmem], out_specs=vmem,
        scratch_shapes=[pltpu.VMEM((8, rows, ROW), F32), pltpu.SemaphoreType.DMA((7,)), pltpu.SemaphoreType.DMA((7,))],
        name=name,
    )(v)


def _adam_fn(w, g, m, v):
    m2 = ADAM_B1 * m + (1.0 - ADAM_B1) * g
    v2 = ADAM_B2 * v + (1.0 - ADAM_B2) * (g * g)
    m_hat = m2 / (1.0 - ADAM_B1 ** ADAM_STEP)
    v_hat = v2 / (1.0 - ADAM_B2 ** ADAM_STEP)
    return -ADAM_LR * (m_hat / (jnp.sqrt(v_hat) + ADAM_EPS) + ADAM_WD * w), m2, v2


def _adam(name, w, g, m, v):
    return _ew(name, _adam_fn, [w, g, m, v], [], [(ROW, F32)] * 3, tm=min(512, w.shape[0]))


def _full_shape(name, shard_shape):
    ax = _shard_axis(name)
    return tuple(n * N_CHIPS if k == ax else n for k, n in enumerate(shard_shape))


def _step(a):
    x_i, y_i, c_i = _me()
    chip = 2 * x_i + y_i
    big_shapes = [a[n].shape for n in BIG]

    packed = _pack([a[n].astype(BF16) for n in BIG], BIG_ROWS)
    gathered = _gather_shards(packed)
    pieces = [_unpack(gathered[j], big_shapes) for j in range(N_CHIPS)]
    big = {n: jnp.concatenate([pieces[j][k] for j in range(N_CHIPS)], axis=_shard_axis(n)) for k, n in enumerate(BIG)}

    small = {n: a[n] for n in SMALL}
    convs = [n for n in SMALL if n.endswith("dn_conv_w")]
    placed = []
    for n in convs:
        full = jnp.zeros(_full_shape_conv(a[n].shape), F32)
        placed.append(lax.dynamic_update_slice(full, a[n], (0, chip * a[n].shape[1])))
    conv_sum = _all_reduce_small("gather_conv", _pack(placed, 8))
    for n, full in zip(convs, _unpack(conv_sum, [p.shape for p in placed])):
        small[n] = full * 0.5

    part, dx, grads = local_step(a["x"][0], a["loss_target"][0], big, small)
    loss = lax.psum(0.5 * jnp.sum(part) / D_MODEL, ("x", "y", "c"))

    chunks = []
    for j in range(N_CHIPS):
        per = []
        for n, shp in zip(BIG, big_shapes):
            ax = _shard_axis(n)
            per.append(lax.slice_in_dim(grads[n], j * shp[ax], (j + 1) * shp[ax], axis=ax))
        chunks.append(_pack(per, BIG_ROWS))
    g_all = jnp.stack(chunks)
    s32, s16 = _pair_sum(g_all, _swap_halves(g_all), c_i)
    g_big = _join_halves(_chip_sum(s32, _scatter_chunks(s16), chip))

    g_small_full = _all_reduce_small("reduce_small", _pack([grads[n] for n in SMALL], 8))
    g_small = dict(zip(SMALL, _unpack(g_small_full, [grads[n].shape for n in SMALL])))
    for n in convs:
        g_small[n] = lax.dynamic_slice_in_dim(g_small[n], chip * a[n].shape[1], a[n].shape[1], axis=1)

    pk = lambda prefix, names, mult: _pack([a[prefix + n] for n in names], mult)
    d_big, m_big, v_big = _adam("adam_big", pk("", BIG, BIG_ROWS), g_big, pk("m_", BIG, BIG_ROWS), pk("v_", BIG, BIG_ROWS))
    gs_packed = _pack([g_small[n] for n in SMALL], 8)
    d_sm, m_sm, v_sm = _adam("adam_small", pk("", SMALL, 8), gs_packed, pk("m_", SMALL, 8), pk("v_", SMALL, 8))

    small_shapes = [a[n].shape for n in SMALL]
    outs = {}
    for key, bbuf, sbuf in (("grad_", g_big, gs_packed), ("delta_", d_big, d_sm), ("new_m_", m_big, m_sm), ("new_v_", v_big, v_sm)):
        outs.update({key + n: val for n, val in zip(BIG, _unpack(bbuf, big_shapes))})
        outs.update({key + n: val for n, val in zip(SMALL, _unpack(sbuf, small_shapes))})
    result = [loss, dx[None]]
    for key in ("grad_", "delta_", "new_m_", "new_v_"):
        result += [outs[key + n] for n in WEIGHTS]
    return tuple(result)


def _full_shape_conv(shard_shape):
    return (shard_shape[0], shard_shape[1] * N_CHIPS)


def kernel(x, l0_mix_norm, l0_dn_w_in, l0_dn_conv_w, l0_dn_a_log, l0_dn_dt_bias, l0_dn_out_norm, l0_dn_w_out, l0_ffn_norm, l0_ffn_w_gate_up, l0_ffn_w_down, l1_mix_norm, l1_sb_w_qkv, l1_sb_q_norm, l1_sb_k_norm, l1_sb_w_out, l1_ffn_norm, l1_ffn_w_gate_up, l1_ffn_w_down, l2_mix_norm, l2_mla_w_down, l2_mla_q_a_norm, l2_mla_kv_a_norm, l2_mla_w_uq, l2_mla_w_ukv, l2_mla_q_nope_norm, l2_mla_q_rope_norm, l2_mla_k_nope_norm, l2_mla_k_rope_norm, l2_mla_w_out, l2_ffn_norm, l2_ffn_w_gate_up, l2_ffn_w_down, l3_mix_norm, l3_dn_w_in, l3_dn_conv_w, l3_dn_a_log, l3_dn_dt_bias, l3_dn_out_norm, l3_dn_w_out, l3_ffn_norm, l3_ffn_w_gate_up, l3_ffn_w_down, loss_target, m_l0_mix_norm, m_l0_dn_w_in, m_l0_dn_conv_w, m_l0_dn_a_log, m_l0_dn_dt_bias, m_l0_dn_out_norm, m_l0_dn_w_out, m_l0_ffn_norm, m_l0_ffn_w_gate_up, m_l0_ffn_w_down, m_l1_mix_norm, m_l1_sb_w_qkv, m_l1_sb_q_norm, m_l1_sb_k_norm, m_l1_sb_w_out, m_l1_ffn_norm, m_l1_ffn_w_gate_up, m_l1_ffn_w_down, m_l2_mix_norm, m_l2_mla_w_down, m_l2_mla_q_a_norm, m_l2_mla_kv_a_norm, m_l2_mla_w_uq, m_l2_mla_w_ukv, m_l2_mla_q_nope_norm, m_l2_mla_q_rope_norm, m_l2_mla_k_nope_norm, m_l2_mla_k_rope_norm, m_l2_mla_w_out, m_l2_ffn_norm, m_l2_ffn_w_gate_up, m_l2_ffn_w_down, m_l3_mix_norm, m_l3_dn_w_in, m_l3_dn_conv_w, m_l3_dn_a_log, m_l3_dn_dt_bias, m_l3_dn_out_norm, m_l3_dn_w_out, m_l3_ffn_norm, m_l3_ffn_w_gate_up, m_l3_ffn_w_down, v_l0_mix_norm, v_l0_dn_w_in, v_l0_dn_conv_w, v_l0_dn_a_log, v_l0_dn_dt_bias, v_l0_dn_out_norm, v_l0_dn_w_out, v_l0_ffn_norm, v_l0_ffn_w_gate_up, v_l0_ffn_w_down, v_l1_mix_norm, v_l1_sb_w_qkv, v_l1_sb_q_norm, v_l1_sb_k_norm, v_l1_sb_w_out, v_l1_ffn_norm, v_l1_ffn_w_gate_up, v_l1_ffn_w_down, v_l2_mix_norm, v_l2_mla_w_down, v_l2_mla_q_a_norm, v_l2_mla_kv_a_norm, v_l2_mla_w_uq, v_l2_mla_w_ukv, v_l2_mla_q_nope_norm, v_l2_mla_q_rope_norm, v_l2_mla_k_nope_norm, v_l2_mla_k_rope_norm, v_l2_mla_w_out, v_l2_ffn_norm, v_l2_ffn_w_gate_up, v_l2_ffn_w_down, v_l3_mix_norm, v_l3_dn_w_in, v_l3_dn_conv_w, v_l3_dn_a_log, v_l3_dn_dt_bias, v_l3_dn_out_norm, v_l3_dn_w_out, v_l3_ffn_norm, v_l3_ffn_w_gate_up, v_l3_ffn_w_down):
    return _step(dict(locals()))
```

```python
import functools
import math

import jax
import jax.numpy as jnp
from jax import lax
from jax.experimental import pallas as pl
from jax.experimental.pallas import tpu as pltpu

F32, BF16 = jnp.float32, jnp.bfloat16
MESH = pl.DeviceIdType.MESH

D_MODEL = 1024
N_HEADS = 8
HEAD = 128
FFN_HIDDEN = 2816
DN_CHUNK = 64
NORM_EPS = 1e-6
MLA_ROPE = 64
MLA_QK = 192
ROPE_THETA = 10000.0
ADAM_LR, ADAM_B1, ADAM_B2, ADAM_EPS, ADAM_WD, ADAM_STEP = 0.001, 0.9, 0.999, 1e-08, 0.01, 10
N_CHIPS = 4
LANES = 128
VMEM_LIMIT = 56 * 2 ** 20


def _params(n_grid):
    return pltpu.CompilerParams(dimension_semantics=("arbitrary",) * n_grid, vmem_limit_bytes=VMEM_LIMIT)


_MIXERS = (
    ("dn_w_in", "dn_conv_w", "dn_a_log", "dn_dt_bias", "dn_out_norm", "dn_w_out"),
    ("sb_w_qkv", "sb_q_norm", "sb_k_norm", "sb_w_out"),
    ("mla_w_down", "mla_q_a_norm", "mla_kv_a_norm", "mla_w_uq", "mla_w_ukv", "mla_q_nope_norm",
     "mla_q_rope_norm", "mla_k_nope_norm", "mla_k_rope_norm", "mla_w_out"),
)
DEPTH = 4


def _layer_names(i):
    p = "l%d_" % i
    return [p + "mix_norm"] + [p + n for n in _MIXERS[i % 3]] + [p + "ffn_norm", p + "ffn_w_gate_up", p + "ffn_w_down"]


WEIGHTS = [n for i in range(DEPTH) for n in _layer_names(i)]
_ROW_SHARDED = ("w_out", "ffn_w_down", "mla_w_down")
_COL_SHARDED = ("dn_w_in", "sb_w_qkv", "mla_w_uq", "mla_w_ukv", "ffn_w_gate_up")


def _shard_axis(name):
    if name.endswith(_ROW_SHARDED):
        return 0
    if name.endswith(_COL_SHARDED):
        return 1
    return None


BIG = [n for n in WEIGHTS if _shard_axis(n) is not None]
SMALL = [n for n in WEIGHTS if _shard_axis(n) is None]


_DN = {"nn": (((1,), (0,)), ((), ())), "nt": (((1,), (1,)), ((), ())), "tn": (((0,), (0,)), ((), ()))}
_DN_BATCHED = {"nn": (((2,), (1,)), ((0,), (0,))), "nt": (((2,), (2,)), ((0,), (0,))), "tn": (((1,), (1,)), ((0,), (0,)))}


def _dims(a, kind):
    return _DN_BATCHED[kind] if a.ndim == 3 else _DN[kind]


def _dg(a, b, kind):
    return lax.dot_general(a.astype(BF16), b.astype(BF16), _dims(a, kind), preferred_element_type=F32)


@functools.partial(jax.custom_vjp, nondiff_argnums=(2,))
def bdot(a, b, kind):
    return _dg(a, b, kind)


def _bdot_fwd(a, b, kind):
    return _dg(a, b, kind), (a, b)


def _bdot_bwd(kind, res, ct):
    a, b = res
    if kind == "nn":
        return _dg(ct, b, "nt"), _dg(a, ct, "tn")
    if kind == "nt":
        return _dg(ct, b, "nn"), _dg(ct, a, "tn")
    return _dg(b, ct, "nt"), _dg(a, ct, "nn")


bdot.defvjp(_bdot_fwd, _bdot_bwd)


def _split(a, terms):
    out = []
    for _ in range(terms):
        hi = a.astype(BF16)
        out.append(hi)
        a = a - hi.astype(F32)
    return out


def _xdot(a, b, kind, exact, terms=3):
    if exact == 0:
        return sum(lax.dot_general(a, p, _dims(a, kind), preferred_element_type=F32) for p in _split(b, terms))
    return sum(lax.dot_general(p, b, _dims(a, kind), preferred_element_type=F32) for p in _split(a, terms))


def _tri(n, rel):
    r = lax.broadcasted_iota(jnp.int32, (n, n), 0)
    c = lax.broadcasted_iota(jnp.int32, (n, n), 1)
    return {"le": c <= r, "lt": c < r, "ge": c >= r, "gt": c > r}[rel]


def _running(g):
    n = g.shape[-2]
    return jnp.broadcast_to(_tri(n, "le").astype(BF16), g.shape[:-2] + (n, n))


@jax.custom_vjp
def cumsum_rows(g):
    return _xdot(_running(g), g, "nn", 0)


def _cumsum_fwd(g):
    return cumsum_rows(g), None


def _cumsum_bwd(_, ct):
    return (_xdot(_running(ct), ct, "tn", 0),)


cumsum_rows.defvjp(_cumsum_fwd, _cumsum_bwd)


def _dot3(a, b, kind):
    (ah, al), (bh, bl) = _split(a, 2), _split(b, 2)
    dot = lambda p, q: lax.dot_general(p, q, _dims(a, kind), preferred_element_type=F32)
    return dot(ah, bh) + (dot(ah, bl) + dot(al, bh))


@functools.partial(jax.custom_vjp, nondiff_argnums=(2,))
def _hdot3(a, b, kind):
    return _dot3(a, b, kind)


def _hdot3_fwd(a, b, kind):
    return _dot3(a, b, kind), (a, b)


def _hdot3_bwd(kind, res, ct):
    a, b = res
    if kind == "nn":
        return _dot3(ct, b, "nt"), _dot3(a, ct, "tn")
    if kind == "nt":
        return _dot3(ct, b, "nn"), _dot3(ct, a, "tn")
    return _dot3(b, ct, "nt"), _dot3(a, ct, "nn")


_hdot3.defvjp(_hdot3_fwd, _hdot3_bwd)


def _hdot(a, b):
    return _hdot3(a, b, "nn")


def _unit_lower_inverse(lower):
    n = lower.shape[-1]
    eye = (lax.broadcasted_iota(jnp.int32, (n, n), 0) == lax.broadcasted_iota(jnp.int32, (n, n), 1)).astype(F32)
    m = -lower
    p = eye + m
    for _ in range(int(math.log2(n)) - 1):
        m = _hdot(m, m)
        p = p + _hdot(p, m)
    return p


def _rms(x, g, n=None):
    n = x.shape[-1] if n is None else n
    return x * lax.rsqrt(jnp.sum(x * x, axis=-1, keepdims=True) * (1.0 / n) + NORM_EPS) * g


def _l2(x):
    return x * lax.rsqrt(jnp.sum(x * x, axis=-1, keepdims=True) + NORM_EPS)


def _silu(x):
    return x * jax.nn.sigmoid(x)


def _logsig(z):
    return jnp.minimum(z, 0.0) - jnp.log1p(jnp.exp(-jnp.abs(z)))


@jax.custom_vjp
def _rope(x, cos, sin_lo, sin_hi):
    return x * cos + pltpu.roll(x, 96, 1) * sin_lo + pltpu.roll(x, 32, 1) * sin_hi


def _rope_fwd(x, cos, sin_lo, sin_hi):
    return _rope(x, cos, sin_lo, sin_hi), (cos, sin_lo, sin_hi)


def _rope_bwd(res, ct):
    cos, sin_lo, sin_hi = res
    dx = ct * cos + pltpu.roll(ct * sin_lo, 32, 1) + pltpu.roll(ct * sin_hi, 96, 1)
    return dx, jnp.zeros_like(cos), jnp.zeros_like(sin_lo), jnp.zeros_like(sin_hi)


_rope.defvjp(_rope_fwd, _rope_bwd)


def _tile(n, prefs=(512, 384, 256, 128)):
    for t in prefs:
        if n % t == 0:
            return t
    return n


MM_OUT_TILES = (1024, 1408, 512, 384, 256, 128)


def _mm(name, a, b, kind, out_dtype=F32, add=None):
    if kind == "tn":
        (kdim, m), n = a.shape, b.shape[1]
    else:
        (m, kdim), n = a.shape, (b.shape[0] if kind == "nt" else b.shape[1])
    tm, tn, tk = _tile(m, MM_OUT_TILES), _tile(n, MM_OUT_TILES), _tile(kdim)
    nk = kdim // tk
    a_spec = pl.BlockSpec((tk, tm), lambda i, j, k: (k, i)) if kind == "tn" else pl.BlockSpec((tm, tk), lambda i, j, k: (i, k))
    b_spec = pl.BlockSpec((tn, tk), lambda i, j, k: (j, k)) if kind == "nt" else pl.BlockSpec((tk, tn), lambda i, j, k: (k, j))
    o_spec = pl.BlockSpec((tm, tn), lambda i, j, k: (i, j))
    has_add = add is not None

    def body(*refs):
        a_ref, b_ref = refs[0], refs[1]
        o_ref, acc = refs[-2], refs[-1]
        k = pl.program_id(2)

        @pl.when(k == 0)
        def _():
            acc[...] = jnp.zeros_like(acc)

        acc[...] += _dg(a_ref[...], b_ref[...], kind)

        @pl.when(k == nk - 1)
        def _():
            r = acc[...]
            if has_add:
                r = r + refs[2][...]
            o_ref[...] = r.astype(o_ref.dtype)

    return pl.pallas_call(
        body, grid=(m // tm, n // tn, nk),
        in_specs=[a_spec, b_spec] + ([o_spec] if has_add else []),
        out_specs=o_spec, out_shape=jax.ShapeDtypeStruct((m, n), out_dtype),
        scratch_shapes=[pltpu.VMEM((tm, tn), F32)], name=name, compiler_params=_params(3),
    )(*([a, b] + ([add] if has_add else [])))


class _V:
    def __init__(self, arr, w=None, base=0, diff=True):
        self.arr, self.base, self.diff = arr, base, diff
        self.w = arr.shape[1] if w is None else w

    def spec(self, tm):
        return pl.BlockSpec((tm, self.w), lambda i, b=self.base: (i, b))


def _as_views(ins):
    return [v if isinstance(v, _V) else _V(v) for v in ins]


def _tup(r):
    return tuple(r) if isinstance(r, (tuple, list)) else (r,)


def _ew(name, fn, ins, smalls, outs, tm=256):
    ins = _as_views(ins)
    t = ins[0].arr.shape[0]
    tm = min(tm, t)
    n_in = len(ins) + len(smalls)

    def body(*refs):
        res = _tup(fn(*[r[...] for r in refs[:n_in]]))
        for r, o in zip(refs[n_in:], res):
            r[...] = o.astype(r.dtype)

    return pl.pallas_call(
        body, grid=(t // tm,),
        in_specs=[v.spec(tm) for v in ins] + [pl.BlockSpec(s.shape, lambda i: (0, 0)) for s in smalls],
        out_specs=[pl.BlockSpec((tm, w), lambda i: (i, 0)) for w, _ in outs],
        out_shape=[jax.ShapeDtypeStruct((t, w), dt) for w, dt in outs],
        name=name, compiler_params=_params(1),
    )(*[v.arr for v in ins], *smalls)


def _ew_bwd(name, fn, ins, smalls, cts, tm=256, add=None):
    ins = _as_views(ins)
    t = ins[0].arr.shape[0]
    tm = min(tm, t)
    n_in, n_sm = len(ins), len(smalls)
    diff = [k for k, v in enumerate(ins) if v.diff]
    ct_arrs = [c for c in cts if c is not None]
    has_add = add is not None

    def body(*refs):
        vals = [r[...] for r in refs[:n_in]]
        svals = [r[...] for r in refs[n_in:n_in + n_sm]]
        p = n_in + n_sm
        ct_refs = list(refs[p:p + len(ct_arrs)])
        p += len(ct_arrs)
        add_ref = refs[p] if has_add else None
        p += int(has_add)
        din_refs = refs[p:p + len(diff)]
        dsm_refs = refs[p + len(diff):]

        def f(dv, sv):
            full = list(vals)
            for k, d in zip(diff, dv):
                full[k] = d
            return _tup(fn(*full, *sv))

        res, vjp = jax.vjp(f, [vals[k] for k in diff], svals)
        ctv = tuple(jnp.zeros_like(o) if c is None else ct_refs.pop(0)[...].astype(o.dtype) for c, o in zip(cts, res))
        dv, dsv = vjp(ctv)
        for n, (r, d) in enumerate(zip(din_refs, dv)):
            if n == 0 and has_add:
                d = d + add_ref[...]
            r[...] = d.astype(r.dtype)

        @pl.when(pl.program_id(0) == 0)
        def _():
            for r in dsm_refs:
                r[...] = jnp.zeros_like(r)

        for r, d in zip(dsm_refs, dsv):
            r[...] += d

    row = lambda w: pl.BlockSpec((tm, w), lambda i: (i, 0))
    small_specs = [pl.BlockSpec(s.shape, lambda i: (0, 0)) for s in smalls]
    out = pl.pallas_call(
        body, grid=(t // tm,),
        in_specs=[v.spec(tm) for v in ins] + small_specs + [row(c.shape[1]) for c in ct_arrs]
        + ([row(add.shape[1])] if has_add else []),
        out_specs=[row(ins[k].w) for k in diff] + small_specs,
        out_shape=[jax.ShapeDtypeStruct((t, ins[k].w), F32) for k in diff]
        + [jax.ShapeDtypeStruct(s.shape, F32) for s in smalls],
        name=name, compiler_params=_params(1),
    )(*[v.arr for v in ins], *smalls, *ct_arrs, *([add] if has_add else []))
    return out[:len(diff)], out[len(diff):]


BQ = 256
SUM_TERMS = 2


def _cat(parts):
    return parts[0] if len(parts) == 1 else jnp.concatenate(parts, axis=1)


def _attn_specs(qs, ks, v, t):
    q_specs = [pl.BlockSpec((BQ, HEAD), lambda h, i: (i, h)) for _ in qs]
    k_specs = [pl.BlockSpec((t, HEAD), (lambda h, i: (0, 0)) if sh else (lambda h, i: (0, h))) for _, sh in ks]
    v_spec = pl.BlockSpec((t, HEAD), lambda h, i, b=v[1], s=v[2]: (0, b + h * s))
    return q_specs, k_specs, v_spec


def _attn_fwd(name, mode, qs, ks, v):
    t = qs[0].shape[0]
    nq, n = t // BQ, len(qs)
    q_specs, k_specs, v_spec = _attn_specs(qs, ks, v, t)

    def body(*refs):
        q_refs, k_refs, v_ref = refs[:n], refs[n:2 * n], refs[2 * n]
        o_ref, st_ref = refs[2 * n + 1], refs[2 * n + 2]
        i = pl.program_id(1)
        q = _cat([r[...] for r in q_refs])
        row = i * BQ + lax.broadcasted_iota(jnp.int32, (BQ, BQ), 0)
        col0 = lax.broadcasted_iota(jnp.int32, (BQ, BQ), 1)

        def logits(j):
            off = pl.multiple_of(j * BQ, BQ)
            z = _dg(q, _cat([kr[pl.ds(off, BQ), :] for kr in k_refs]), "nt")
            return z, v_ref[pl.ds(off, BQ), :], j * BQ + col0

        if mode == "sb":
            after = _tri(BQ, "lt").astype(BF16)

            def step(s, carry):
                acc, run = carry
                z, vj, col = logits(i - s)
                past = col < row
                lsz = _logsig(z)
                stay = jnp.where(past, lsz - z, 0.0)
                later = run + _xdot(stay, after, "nn", 1, SUM_TERMS)
                a = jnp.where(past, jnp.exp(lsz + later), 0.0)
                return acc + _dg(a, vj, "nn"), run + jnp.sum(stay, axis=1, keepdims=True)

            acc, run = lax.fori_loop(0, i + 1, step, (jnp.zeros((BQ, HEAD), F32), jnp.zeros((BQ, 1), F32)))
            o_ref[...] = acc
            st_ref[...] = jnp.broadcast_to(run, (BQ, HEAD))
        else:
            def step(j, carry):
                m, l, acc = carry
                z, vj, col = logits(j)
                z = jnp.where(col <= row, z, -1e30)
                m2 = jnp.maximum(m, jnp.max(z, axis=1, keepdims=True))
                p = jnp.exp(z - m2)
                alpha = jnp.exp(m - m2)
                return m2, alpha * l + jnp.sum(p, axis=1, keepdims=True), alpha * acc + _dg(p, vj, "nn")

            m, l, acc = lax.fori_loop(0, i + 1, step, (jnp.full((BQ, 1), -1e30, F32), jnp.zeros((BQ, 1), F32),
                                                       jnp.zeros((BQ, HEAD), F32)))
            o_ref[...] = acc / l
            st_ref[...] = jnp.broadcast_to(m + jnp.log(l), (BQ, HEAD))

    blk = pl.BlockSpec((BQ, HEAD), lambda h, i: (i, h))
    return pl.pallas_call(
        body, grid=(N_HEADS, nq), in_specs=q_specs + k_specs + [v_spec], out_specs=[blk, blk],
        out_shape=[jax.ShapeDtypeStruct((t, N_HEADS * HEAD), F32)] * 2, name=name, compiler_params=_params(2),
    )(*qs, *[k for k, _ in ks], v[0])


def _attn_bwd(name, mode, qs, ks, v, o, stat, do):
    t = qs[0].shape[0]
    nq, n = t // BQ, len(qs)
    q_specs, k_specs, v_spec = _attn_specs(qs, ks, v, t)
    shared = [sh for _, sh in ks]

    def body(*refs):
        q_refs, k_refs, v_ref = refs[:n], refs[n:2 * n], refs[2 * n]
        o_ref, st_ref, do_ref = refs[2 * n + 1:2 * n + 4]
        dq_refs = refs[2 * n + 4:3 * n + 4]
        dk_refs = refs[3 * n + 4:4 * n + 4]
        dv_ref = refs[4 * n + 4]
        h, i = pl.program_id(0), pl.program_id(1)

        @pl.when(i == 0)
        def _():
            dv_ref[...] = jnp.zeros_like(dv_ref)
            for r, sh in zip(dk_refs, shared):
                if not sh:
                    r[...] = jnp.zeros_like(r)

        for r, sh in zip(dk_refs, shared):
            if sh:
                @pl.when((i == 0) & (h == 0))
                def _(r=r):
                    r[...] = jnp.zeros_like(r)

        q = _cat([r[...] for r in q_refs])
        do_t = do_ref[...]
        st = st_ref[:, :1]
        row = i * BQ + lax.broadcasted_iota(jnp.int32, (BQ, BQ), 0)
        col0 = lax.broadcasted_iota(jnp.int32, (BQ, BQ), 1)
        if mode == "sb":
            upto = _tri(BQ, "ge").astype(BF16)
            before = _tri(BQ, "gt").astype(BF16)
        else:
            dsum = jnp.sum(do_t * o_ref[...], axis=1, keepdims=True)

        def step(j, carry):
            off = pl.multiple_of(j * BQ, BQ)
            kj = _cat([kr[pl.ds(off, BQ), :] for kr in k_refs])
            vj = v_ref[pl.ds(off, BQ), :]
            z = _dg(q, kj, "nt")
            col = j * BQ + col0
            da = _dg(do_t, vj, "nt")
            if mode == "sb":
                dq, pre, gpre = carry[0], carry[1], carry[2]
                past = col < row
                lsz = _logsig(z)
                stay = jnp.where(past, lsz - z, 0.0)
                later = st - (pre + _xdot(stay, upto, "nn", 1, SUM_TERMS))
                a = jnp.where(past, jnp.exp(lsz + later), 0.0)
                g = a * da
                gbefore = gpre + _xdot(g, before, "nn", 1, SUM_TERMS)
                sig = jnp.exp(lsz)
                dz = jnp.where(past, g * (1.0 - sig) - sig * gbefore, 0.0)
                tail = (pre + jnp.sum(stay, axis=1, keepdims=True), gpre + jnp.sum(g, axis=1, keepdims=True))
            else:
                dq = carry[0]
                a = jnp.where(col <= row, jnp.exp(z - st), 0.0)
                dz = a * (da - dsum)
                tail = ()
            dq = dq + _dg(dz, kj, "nn")
            dk = _dg(dz, q, "tn")
            for p, r in enumerate(dk_refs):
                r[pl.ds(off, BQ), :] += dk[:, p * HEAD:(p + 1) * HEAD]
            dv_ref[pl.ds(off, BQ), :] += _dg(a, do_t, "tn")
            return (dq,) + tail

        zero = jnp.zeros((BQ, 1), F32)
        init = (jnp.zeros((BQ, n * HEAD), F32),) + ((zero, zero) if mode == "sb" else ())
        dq = lax.fori_loop(0, i + 1, step, init)[0]
        for p, r in enumerate(dq_refs):
            r[...] = dq[:, p * HEAD:(p + 1) * HEAD]

    blk = pl.BlockSpec((BQ, HEAD), lambda h, i: (i, h))
    per_head = pl.BlockSpec((t, HEAD), lambda h, i: (0, h))
    dk_specs = [pl.BlockSpec((t, HEAD), lambda h, i: (0, 0)) if sh else per_head for sh in shared]
    wide = jax.ShapeDtypeStruct((t, N_HEADS * HEAD), F32)
    out = pl.pallas_call(
        body, grid=(N_HEADS, nq), in_specs=q_specs + k_specs + [v_spec, blk, blk, blk],
        out_specs=[blk] * n + dk_specs + [per_head],
        out_shape=[wide] * n + [jax.ShapeDtypeStruct((t, HEAD), F32) if sh else wide for sh in shared] + [wide],
        name=name, compiler_params=_params(2),
    )(*qs, *[k for k, _ in ks], v[0], o, stat, do)
    return out[:n], out[n:2 * n], out[2 * n]


def _dn_chunk(q, k, v, g, beta, state):
    c = q.shape[-2]
    gc = cumsum_rows(g)
    gcc = gc[..., :c]
    diff = gcc - jnp.swapaxes(gcc, -1, -2)
    causal, strict = _tri(c, "le"), _tri(c, "lt")
    decay = jnp.where(causal, jnp.exp(jnp.where(causal, diff, 0.0)), 0.0)
    kb = k * beta
    lower = jnp.where(strict, bdot(kb, k, "nt") * decay, 0.0)
    tinv = _unit_lower_inverse(lower)
    eg = jnp.exp(gc)
    u = _hdot(tinv, v * beta)
    w = _hdot(tinv, kb * eg)
    attn = bdot(q, k, "nt") * decay
    glast = gc[..., c - 1:c, :]
    v_new = u - bdot(w, state, "nn")
    o = bdot(q * eg, state, "nn") + bdot(attn, v_new, "nn")
    new_state = state * jnp.exp(glast) + bdot(k * jnp.exp(glast - gc), v_new, "tn")
    return o, new_state


def _stack_heads(ref):
    return jnp.stack([ref[:, h * HEAD:(h + 1) * HEAD] for h in range(N_HEADS)])


def _store_heads(ref, val):
    for h in range(N_HEADS):
        ref[:, h * HEAD:(h + 1) * HEAD] = val[h]


def _dn_fwd(name, q, k, v, g, beta):
    t = q.shape[0]
    nc = t // DN_CHUNK
    wide = N_HEADS * HEAD
    blk = pl.BlockSpec((DN_CHUNK, wide), lambda n: (n, 0))
    st_spec = pl.BlockSpec((N_HEADS, None, HEAD, HEAD), lambda n: (0, n, 0, 0))

    def body(q_ref, k_ref, v_ref, g_ref, b_ref, o_ref, st_ref, state):
        @pl.when(pl.program_id(0) == 0)
        def _():
            state[...] = jnp.zeros_like(state)

        s_in = state[...]
        st_ref[...] = s_in
        o, s_out = _dn_chunk(*[_stack_heads(r) for r in (q_ref, k_ref, v_ref, g_ref, b_ref)], s_in)
        _store_heads(o_ref, o)
        state[...] = s_out

    return pl.pallas_call(
        body, grid=(nc,), in_specs=[blk] * 5, out_specs=[blk, st_spec],
        out_shape=[jax.ShapeDtypeStruct((t, wide), F32), jax.ShapeDtypeStruct((N_HEADS, nc, HEAD, HEAD), F32)],
        scratch_shapes=[pltpu.VMEM((N_HEADS, HEAD, HEAD), F32)], name=name, compiler_params=_params(1),
    )(q, k, v, g, beta)


def _dn_bwd(name, q, k, v, g, beta, states, do):
    t = q.shape[0]
    nc = t // DN_CHUNK
    wide = N_HEADS * HEAD
    blk = pl.BlockSpec((DN_CHUNK, wide), lambda n: (nc - 1 - n, 0))
    st_spec = pl.BlockSpec((N_HEADS, None, HEAD, HEAD), lambda n: (0, nc - 1 - n, 0, 0))

    def body(q_ref, k_ref, v_ref, g_ref, b_ref, st_ref, do_ref, dq_ref, dk_ref, dv_ref, dg_ref, db_ref, dstate):
        @pl.when(pl.program_id(0) == 0)
        def _():
            dstate[...] = jnp.zeros_like(dstate)

        _, vjp = jax.vjp(_dn_chunk, *[_stack_heads(r) for r in (q_ref, k_ref, v_ref, g_ref, b_ref)], st_ref[...])
        cts = vjp((_stack_heads(do_ref), dstate[...]))
        for r, d in zip((dq_ref, dk_ref, dv_ref, dg_ref, db_ref), cts[:5]):
            _store_heads(r, d)
        dstate[...] = cts[5]

    shape = jax.ShapeDtypeStruct((t, wide), F32)
    return pl.pallas_call(
        body, grid=(nc,), in_specs=[blk] * 5 + [st_spec, blk], out_specs=[blk] * 5, out_shape=[shape] * 5,
        scratch_shapes=[pltpu.VMEM((N_HEADS, HEAD, HEAD), F32)], name=name, compiler_params=_params(1),
    )(q, k, v, g, beta, states, do)


CONV_W = 1024
HALO = 8


def _shift_down(cur, prev, s):
    sh = pltpu.roll(cur, s, 0)
    ph = pltpu.roll(prev, s, 0)
    r = lax.broadcasted_iota(jnp.int32, (HALO, cur.shape[1]), 0)
    return jnp.concatenate([jnp.where(r < s, ph, sh[:HALO]), sh[HALO:]], axis=0)


def _shift_up(cur, nxt, s):
    tm = cur.shape[0]
    sh = pltpu.roll(cur, tm - s, 0)
    nh = pltpu.roll(nxt, HALO - s, 0)
    r = lax.broadcasted_iota(jnp.int32, (HALO, cur.shape[1]), 0)
    return jnp.concatenate([sh[:tm - HALO], jnp.where(r >= HALO - s, nh, sh[tm - HALO:])], axis=0)


def _conv_fwd(name, proj, w, tm=256):
    t = proj.shape[0]
    tm = min(tm, t)
    width = w.shape[1]
    per = tm // HALO

    def body(cur_ref, prev_ref, w_ref, y_ref):
        cur = cur_ref[...]
        prev = jnp.where(pl.program_id(0) > 0, prev_ref[...], 0.0)
        y = cur * w_ref[3:4, :]
        for s in (1, 2, 3):
            y = y + _shift_down(cur, prev, s) * w_ref[3 - s:4 - s, :]
        y_ref[...] = y

    return pl.pallas_call(
        body, grid=(t // tm, width // CONV_W),
        in_specs=[pl.BlockSpec((tm, CONV_W), lambda i, c: (i, c)),
                  pl.BlockSpec((HALO, CONV_W), lambda i, c: (jnp.maximum(i * per - 1, 0), c)),
                  pl.BlockSpec((HALO, CONV_W), lambda i, c: (0, c))],
        out_specs=pl.BlockSpec((tm, CONV_W), lambda i, c: (i, c)),
        out_shape=jax.ShapeDtypeStruct((t, width), F32), name=name, compiler_params=_params(2),
    )(proj, proj, w)


def _conv_bwd(name, proj, w, dy, tm=256):
    t = proj.shape[0]
    tm = min(tm, t)
    width = w.shape[1]
    per, nt = tm // HALO, t // tm

    def body(cur_ref, prev_ref, w_ref, dy_ref, nxt_ref, du_ref, dw_ref):
        i = pl.program_id(1)
        cur, dy_t = cur_ref[...], dy_ref[...]
        prev = jnp.where(i > 0, prev_ref[...], 0.0)
        nxt = jnp.where(i < nt - 1, nxt_ref[...], 0.0)
        du = dy_t * w_ref[3:4, :]
        rows = [jnp.sum(dy_t * cur, axis=0, keepdims=True)]
        for s in (1, 2, 3):
            du = du + _shift_up(dy_t, nxt, s) * w_ref[3 - s:4 - s, :]
            rows.insert(0, jnp.sum(dy_t * _shift_down(cur, prev, s), axis=0, keepdims=True))
        du_ref[...] = du

        @pl.when(i == 0)
        def _():
            dw_ref[...] = jnp.zeros_like(dw_ref)

        dw_ref[...] += jnp.concatenate(rows + [jnp.zeros((HALO - 4, CONV_W), F32)], axis=0)

    return pl.pallas_call(
        body, grid=(width // CONV_W, nt),
        in_specs=[pl.BlockSpec((tm, CONV_W), lambda c, i: (i, c)),
                  pl.BlockSpec((HALO, CONV_W), lambda c, i: (jnp.maximum(i * per - 1, 0), c)),
                  pl.BlockSpec((HALO, CONV_W), lambda c, i: (0, c)),
                  pl.BlockSpec((tm, CONV_W), lambda c, i: (i, c)),
                  pl.BlockSpec((HALO, CONV_W), lambda c, i: (jnp.minimum((i + 1) * per, t // HALO - 1), c))],
        out_specs=[pl.BlockSpec((tm, CONV_W), lambda c, i: (i, c)), pl.BlockSpec((HALO, CONV_W), lambda c, i: (0, c))],
        out_shape=[jax.ShapeDtypeStruct((t, width), F32), jax.ShapeDtypeStruct((HALO, width), F32)],
        name=name, compiler_params=_params(2),
    )(proj, proj, w, dy, dy)


def _norm_fn(x, g):
    return _rms(x, g)


def _swiglu_fn(gu):
    return _silu(gu[:, :FFN_HIDDEN]) * gu[:, FFN_HIDDEN:]


def _heads(x):
    return [x[:, h * HEAD:(h + 1) * HEAD] for h in range(x.shape[1] // HEAD)]


def _dn_pre_fn(c, ab, a_log, dt_bias):
    w = N_HEADS * HEAD
    q = [_l2(_silu(x)) * (HEAD ** -0.5) for x in _heads(c[:, :w])]
    k = [_l2(_silu(x)) for x in _heads(c[:, w:2 * w])]
    v = _silu(c[:, 2 * w:])
    g, beta = [], []
    for h in range(N_HEADS):
        gh = -jnp.exp(a_log[:, h:h + 1]) * jax.nn.softplus(ab[:, h:h + 1] + dt_bias[:, h:h + 1])
        bh = jax.nn.sigmoid(ab[:, N_HEADS + h:N_HEADS + h + 1])
        g.append(jnp.broadcast_to(gh, (c.shape[0], HEAD)))
        beta.append(jnp.broadcast_to(bh, (c.shape[0], HEAD)))
    cat = lambda xs: jnp.concatenate(xs, axis=1)
    return cat(q), cat(k), v, cat(g), cat(beta)


def _dn_post_fn(o, z, out_norm):
    return jnp.concatenate([_rms(oh, out_norm) * _silu(zh) for oh, zh in zip(_heads(o), _heads(z))], axis=1)


def _sb_pre_fn(qkv, q_norm, k_norm):
    w = N_HEADS * HEAD
    q = [_rms(x, q_norm) * (HEAD ** -0.5) for x in _heads(qkv[:, :w])]
    k = [_rms(x, k_norm) for x in _heads(qkv[:, w:2 * w])]
    return jnp.concatenate(q, axis=1), jnp.concatenate(k, axis=1), qkv[:, 2 * w:]


def _mla_a_fn(down, cos, sin_lo, sin_hi, q_a_norm, kv_a_norm, k_rope_norm):
    cq = _rms(down[:, :256], q_a_norm)
    ckv = _rms(down[:, 256:384], kv_a_norm)
    kr = _rope(_rms(down[:, 384:], k_rope_norm, MLA_ROPE), cos, sin_lo, sin_hi)
    return cq, ckv, kr


def _mla_b_fn(qf, kvf, cos, sin_lo, sin_hi, q_nope_norm, q_rope_norm, k_nope_norm):
    scale = MLA_QK ** -0.5
    qn, qr, kn, v = [], [], [], []
    for h in range(N_HEADS):
        a = 2 * h * HEAD
        qn.append(_rms(qf[:, a:a + HEAD], q_nope_norm) * scale)
        qr.append(_rope(_rms(qf[:, a + HEAD:a + 2 * HEAD], q_rope_norm, MLA_ROPE), cos, sin_lo, sin_hi) * scale)
        kn.append(_rms(kvf[:, a:a + HEAD], k_nope_norm))
        v.append(kvf[:, a + HEAD:a + 2 * HEAD])
    cat = lambda xs: jnp.concatenate(xs, axis=1)
    return cat(qn), cat(qr), cat(kn), cat(v)


def _rope_tables(t):
    inv_freq = ROPE_THETA ** (-jnp.arange(0, MLA_ROPE, 2, dtype=F32) / MLA_ROPE)
    ang = jnp.arange(t, dtype=F32)[:, None] * inv_freq[None, :]
    cos, sin, zero = jnp.cos(ang), jnp.sin(ang), jnp.zeros((t, MLA_ROPE // 2), F32)
    cat = lambda xs: jnp.concatenate(xs, axis=1)
    return cat([cos, cos, zero, zero]), cat([-sin, zero, zero, zero]), cat([zero, sin, zero, zero])


def _row(v, width=None):
    width = v.shape[0] if width is None else width
    return jnp.pad(v.astype(F32), (0, width - v.shape[0])).reshape(1, width)


def _loss_kernel(y, target):
    t, d = y.shape
    tm = min(256, t)

    def body(y_ref, t_ref, part_ref, dy_ref):
        e = y_ref[...] - t_ref[...]
        dy_ref[...] = e * (1.0 / d)

        @pl.when(pl.program_id(0) == 0)
        def _():
            part_ref[...] = jnp.zeros_like(part_ref)

        part_ref[...] += jnp.sum(e * e, axis=0, keepdims=True)

    blk = pl.BlockSpec((tm, d), lambda i: (i, 0))
    one = pl.BlockSpec((1, d), lambda i: (0, 0))
    return pl.pallas_call(body, grid=(t // tm,), in_specs=[blk, blk], out_specs=[one, blk],
                          out_shape=[jax.ShapeDtypeStruct((1, d), F32), jax.ShapeDtypeStruct((t, d), F32)],
                          name="loss", compiler_params=_params(1))(y, target)


def _ffn_fwd(p, x, w, sm):
    h, = _ew(p + "ffn_norm", _norm_fn, [x], [sm["ffn_norm"]], [(D_MODEL, BF16)])
    gu = _mm(p + "ffn_gu", h, w["ffn_w_gate_up"], "nn")
    act, = _ew(p + "ffn_act", _swiglu_fn, [gu], [], [(FFN_HIDDEN, BF16)], tm=128)
    y = _mm(p + "ffn_down", act, w["ffn_w_down"], "nn", add=x)
    return y, (x, h, gu, act)


def _ffn_bwd(p, saved, dy, w, sm, grads):
    x, h, gu, act = saved
    dact = _mm(p + "ffn_down_dx", dy, w["ffn_w_down"], "nt")
    grads["ffn_w_down"] = _mm(p + "ffn_down_dw", act, dy, "tn")
    (dgu,), _ = _ew_bwd(p + "ffn_act_bwd", _swiglu_fn, [gu], [], [dact], tm=128)
    dh = _mm(p + "ffn_gu_dx", dgu, w["ffn_w_gate_up"], "nt")
    grads["ffn_w_gate_up"] = _mm(p + "ffn_gu_dw", h, dgu, "tn")
    (dx,), (dg,) = _ew_bwd(p + "ffn_norm_bwd", _norm_fn, [x], [sm["ffn_norm"]], [dh], add=dy)
    grads["ffn_norm"] = dg
    return dx


def _dn_layer_fwd(p, x, w, sm):
    h, = _ew(p + "mix_norm", _norm_fn, [x], [sm["mix_norm"]], [(D_MODEL, BF16)])
    proj = _mm(p + "dn_in", h, w["dn_w_in"], "nn")
    conv = _conv_fwd(p + "dn_conv", proj, sm["dn_conv_w"])
    ab = _V(proj, LANES, 4 * N_HEADS)
    wide = N_HEADS * HEAD
    q, k, v, g, beta = _ew(p + "dn_pre", _dn_pre_fn, [conv, ab], [sm["dn_a_log"], sm["dn_dt_bias"]], [(wide, F32)] * 5)
    o, states = _dn_fwd(p + "dn_core", q, k, v, g, beta)
    z = _V(proj, wide, 3)
    on, = _ew(p + "dn_post", _dn_post_fn, [o, z], [sm["dn_out_norm"]], [(wide, BF16)])
    y = _mm(p + "dn_out", on, w["dn_w_out"], "nn", add=x)
    return y, (x, h, proj, conv, q, k, v, g, beta, o, states, on)


def _dn_layer_bwd(p, saved, dy, w, sm, grads):
    x, h, proj, conv, q, k, v, g, beta, o, states, on = saved
    wide = N_HEADS * HEAD
    don = _mm(p + "dn_out_dx", dy, w["dn_w_out"], "nt")
    grads["dn_w_out"] = _mm(p + "dn_out_dw", on, dy, "tn")
    (do, dz), (d_out_norm,) = _ew_bwd(p + "dn_post_bwd", _dn_post_fn, [o, _V(proj, wide, 3)], [sm["dn_out_norm"]], [don])
    grads["dn_out_norm"] = d_out_norm
    dq, dk, dv, dg, db = _dn_bwd(p + "dn_core_bwd", q, k, v, g, beta, states, do)
    (dconv, dab), (d_a_log, d_dt) = _ew_bwd(p + "dn_pre_bwd", _dn_pre_fn, [conv, _V(proj, LANES, 4 * N_HEADS)],
                                            [sm["dn_a_log"], sm["dn_dt_bias"]], [dq, dk, dv, dg, db])
    grads["dn_a_log"], grads["dn_dt_bias"] = d_a_log, d_dt
    dqkv, dconv_w = _conv_bwd(p + "dn_conv_bwd", proj, sm["dn_conv_w"], dconv)
    grads["dn_conv_w"] = dconv_w
    dproj = jnp.concatenate([dqkv, dz, dab], axis=1)
    dh = _mm(p + "dn_in_dx", dproj, w["dn_w_in"], "nt")
    grads["dn_w_in"] = _mm(p + "dn_in_dw", h, dproj, "tn")
    (dx,), (dgain,) = _ew_bwd(p + "mix_norm_bwd", _norm_fn, [x], [sm["mix_norm"]], [dh], add=dy)
    grads["mix_norm"] = dgain
    return dx


def _sb_layer_fwd(p, x, w, sm):
    h, = _ew(p + "mix_norm", _norm_fn, [x], [sm["mix_norm"]], [(D_MODEL, BF16)])
    qkv = _mm(p + "sb_qkv", h, w["sb_w_qkv"], "nn")
    wide = N_HEADS * HEAD
    q, k, v = _ew(p + "sb_pre", _sb_pre_fn, [qkv], [sm["sb_q_norm"], sm["sb_k_norm"]], [(wide, BF16)] * 3)
    o, stat = _attn_fwd(p + "sb_core", "sb", [q], [(k, False)], (v, 0, 1))
    y = _mm(p + "sb_out", o, w["sb_w_out"], "nn", add=x)
    return y, (x, h, qkv, q, k, v, o, stat)


def _sb_layer_bwd(p, saved, dy, w, sm, grads):
    x, h, qkv, q, k, v, o, stat = saved
    do = _mm(p + "sb_out_dx", dy, w["sb_w_out"], "nt")
    grads["sb_w_out"] = _mm(p + "sb_out_dw", o, dy, "tn")
    (dq,), (dk,), dv = _attn_bwd(p + "sb_core_bwd", "sb", [q], [(k, False)], (v, 0, 1), o, stat, do)
    (dqkv,), (dqn, dkn) = _ew_bwd(p + "sb_pre_bwd", _sb_pre_fn, [qkv], [sm["sb_q_norm"], sm["sb_k_norm"]], [dq, dk, dv])
    grads["sb_q_norm"], grads["sb_k_norm"] = dqn, dkn
    dh = _mm(p + "sb_qkv_dx", dqkv, w["sb_w_qkv"], "nt")
    grads["sb_w_qkv"] = _mm(p + "sb_qkv_dw", h, dqkv, "tn")
    (dx,), (dgain,) = _ew_bwd(p + "mix_norm_bwd", _norm_fn, [x], [sm["mix_norm"]], [dh], add=dy)
    grads["mix_norm"] = dgain
    return dx


def _mla_layer_fwd(p, x, w, sm):
    t = x.shape[0]
    tabs = [_V(a, diff=False) for a in _rope_tables(t)]
    h, = _ew(p + "mix_norm", _norm_fn, [x], [sm["mix_norm"]], [(D_MODEL, BF16)])
    down = _mm(p + "mla_down", h, w["mla_w_down"], "nn")
    sm_a = [sm["mla_q_a_norm"], sm["mla_kv_a_norm"], sm["mla_k_rope_norm"]]
    cq, ckv, kr = _ew(p + "mla_a", _mla_a_fn, [down] + tabs, sm_a, [(256, BF16), (128, BF16), (128, BF16)])
    qf = _mm(p + "mla_uq", cq, w["mla_w_uq"], "nn")
    kvf = _mm(p + "mla_ukv", ckv, w["mla_w_ukv"], "nn")
    sm_b = [sm["mla_q_nope_norm"], sm["mla_q_rope_norm"], sm["mla_k_nope_norm"]]
    wide = N_HEADS * HEAD
    qn, qr, kn, v = _ew(p + "mla_b", _mla_b_fn, [qf, kvf] + tabs, sm_b, [(wide, BF16)] * 4)
    o, stat = _attn_fwd(p + "mla_core", "softmax", [qn, qr], [(kn, False), (kr, True)], (v, 0, 1))
    y = _mm(p + "mla_out", o, w["mla_w_out"], "nn", add=x)
    return y, (x, h, down, cq, ckv, kr, qf, kvf, qn, qr, kn, v, o, stat)


def _mla_layer_bwd(p, saved, dy, w, sm, grads):
    x, h, down, cq, ckv, kr, qf, kvf, qn, qr, kn, v, o, stat = saved
    tabs = [_V(a, diff=False) for a in _rope_tables(x.shape[0])]
    do = _mm(p + "mla_out_dx", dy, w["mla_w_out"], "nt")
    grads["mla_w_out"] = _mm(p + "mla_out_dw", o, dy, "tn")
    (dqn, dqr), (dkn, dkr), dv = _attn_bwd(p + "mla_core_bwd", "softmax", [qn, qr], [(kn, False), (kr, True)],
                                           (v, 0, 1), o, stat, do)
    sm_b = [sm["mla_q_nope_norm"], sm["mla_q_rope_norm"], sm["mla_k_nope_norm"]]
    (dqf, dkvf), dsm_b = _ew_bwd(p + "mla_b_bwd", _mla_b_fn, [qf, kvf] + tabs, sm_b, [dqn, dqr, dkn, dv])
    grads["mla_q_nope_norm"], grads["mla_q_rope_norm"], grads["mla_k_nope_norm"] = dsm_b
    dcq = _mm(p + "mla_uq_dx", dqf, w["mla_w_uq"], "nt")
    grads["mla_w_uq"] = _mm(p + "mla_uq_dw", cq, dqf, "tn")
    dckv = _mm(p + "mla_ukv_dx", dkvf, w["mla_w_ukv"], "nt")
    grads["mla_w_ukv"] = _mm(p + "mla_ukv_dw", ckv, dkvf, "tn")
    sm_a = [sm["mla_q_a_norm"], sm["mla_kv_a_norm"], sm["mla_k_rope_norm"]]
    (ddown,), dsm_a = _ew_bwd(p + "mla_a_bwd", _mla_a_fn, [down] + tabs, sm_a, [dcq, dckv, dkr])
    grads["mla_q_a_norm"], grads["mla_kv_a_norm"], grads["mla_k_rope_norm"] = dsm_a
    dh = _mm(p + "mla_down_dx", ddown, w["mla_w_down"], "nt")
    grads["mla_w_down"] = _mm(p + "mla_down_dw", h, ddown, "tn")
    (dx,), (dgain,) = _ew_bwd(p + "mix_norm_bwd", _norm_fn, [x], [sm["mix_norm"]], [dh], add=dy)
    grads["mix_norm"] = dgain
    return dx


_MIX_FWD = (_dn_layer_fwd, _sb_layer_fwd, _mla_layer_fwd)
_MIX_BWD = (_dn_layer_bwd, _sb_layer_bwd, _mla_layer_bwd)


def _pad_cols(a, n):
    return jnp.pad(a, ((0, 0), (0, n - a.shape[1])))


def _prep_big(name, a):
    if name.endswith("dn_w_in"):
        return _pad_cols(a, 4 * N_HEADS * HEAD + LANES)
    if name.endswith("mla_w_down"):
        return _pad_cols(a, 512)
    if name.endswith("mla_w_uq"):
        a3 = a.reshape(a.shape[0], N_HEADS, MLA_QK)
        return jnp.pad(a3, ((0, 0), (0, 0), (0, 2 * HEAD - MLA_QK))).reshape(a.shape[0], N_HEADS * 2 * HEAD)
    return a


def _unprep_big(name, g):
    if name.endswith("dn_w_in"):
        return g[:, :4 * N_HEADS * HEAD + 2 * N_HEADS]
    if name.endswith("mla_w_down"):
        return g[:, :448]
    if name.endswith("mla_w_uq"):
        return g.reshape(g.shape[0], N_HEADS, 2 * HEAD)[:, :, :MLA_QK].reshape(g.shape[0], N_HEADS * MLA_QK)
    return g


def _prep_small(name, a):
    if name.endswith("dn_conv_w"):
        return jnp.pad(a.astype(F32), ((0, HALO - a.shape[0]), (0, 0)))
    if name.endswith(("dn_a_log", "dn_dt_bias", "mla_q_rope_norm", "mla_k_rope_norm")):
        return _row(a, LANES)
    return _row(a)


def _unprep_small(name, g, like):
    if name.endswith("dn_conv_w"):
        return g[:like.shape[0]]
    return g.reshape(-1)[:like.shape[0]]


def local_step(x, target, big, small):
    layers = []
    for i in range(DEPTH):
        p = "l%d_" % i
        w = {n: _prep_big(n, big[p + n]) for n in _MIXERS[i % 3] + ("ffn_w_gate_up", "ffn_w_down") if p + n in big}
        sm = {n: _prep_small(n, small[p + n]) for n in _MIXERS[i % 3] + ("mix_norm", "ffn_norm") if p + n in small}
        layers.append((p, w, sm))
    saved = []
    for i, (p, w, sm) in enumerate(layers):
        x, s_mix = _MIX_FWD[i % 3](p, x, w, sm)
        x, s_ffn = _ffn_fwd(p, x, w, sm)
        saved.append((s_mix, s_ffn))
    part, dx = _loss_kernel(x, target)
    grads = {}
    for i in reversed(range(DEPTH)):
        p, w, sm = layers[i]
        g = {}
        dx = _ffn_bwd(p, saved[i][1], dx, w, sm, g)
        dx = _MIX_BWD[i % 3](p, saved[i][0], dx, w, sm, g)
        for n, val in g.items():
            grads[p + n] = _unprep_big(p + n, val) if p + n in big else _unprep_small(p + n, val, small[p + n])
    return part, dx, grads


ROW = 1024
BIG_ROWS = 1024


def _pack(arrs, rows_multiple):
    flat = jnp.concatenate([a.reshape(-1) for a in arrs])
    rows = -(-flat.shape[0] // (ROW * rows_multiple)) * rows_multiple
    return jnp.pad(flat, (0, rows * ROW - flat.shape[0])).reshape(rows, ROW)


def _unpack(buf, shapes):
    flat, out, off = buf.reshape(-1), [], 0
    for s in shapes:
        n = math.prod(s)
        out.append(flat[off:off + n].reshape(s))
        off += n
    return out


def _me():
    return lax.axis_index("x"), lax.axis_index("y"), lax.axis_index("c")


def _other_chips(x, y):
    return [(1 - x, y), (x, 1 - y), (1 - x, 1 - y)]


HBM = pl.BlockSpec(memory_space=pl.ANY)


OWN_STREAMS = 4


def _gather_shards(packed):
    rows = packed.shape[0]
    half = rows // 2

    def body(x_ref, out_ref, send_sems, recv_sems):
        x, y, c = _me()
        sibling, chips = (x, y, 1 - c), _other_chips(x, y)

        def part(px, py, pc):
            return out_ref.at[2 * px + py, pl.ds(pl.multiple_of(pc * half, 16), half), :]

        def copy(k, block, to, src=None):
            return pltpu.make_async_remote_copy(
                src_ref=part(*block) if src is None else src, dst_ref=part(*block),
                send_sem=send_sems.at[k], recv_sem=recv_sems.at[k], device_id=to, device_id_type=MESH)

        my_half = x_ref.at[pl.ds(pl.multiple_of(c * half, 16), half), :]
        first = [copy(j, (x, y, c), (*chip, c), src=my_half) for j, chip in enumerate(chips)]
        piece = rows // OWN_STREAMS
        for p in range(OWN_STREAMS):
            rows_p = pl.ds(p * piece, piece)
            first.append(pltpu.make_async_remote_copy(
                src_ref=x_ref.at[rows_p, :], dst_ref=out_ref.at[2 * x + y, rows_p, :], send_sem=send_sems.at[6 + p],
                recv_sem=recv_sems.at[6 + p], device_id=sibling, device_id_type=MESH))
        for cp in first:
            cp.start()
        passed = [copy(3 + j, (*chip, c), sibling) for j, chip in enumerate(chips)]
        for j, chip in enumerate(chips):
            copy(j, (*chip, c), (x, y, c)).wait_recv()
            passed[j].start()
        for j, chip in enumerate(chips):
            copy(3 + j, (*chip, 1 - c), (x, y, c)).wait_recv()
        for cp in first[3:]:
            cp.wait_recv()
        for cp in first + passed:
            cp.wait_send()

    n_sems = 6 + OWN_STREAMS
    return pl.pallas_call(
        body, out_shape=jax.ShapeDtypeStruct((N_CHIPS,) + packed.shape, packed.dtype), in_specs=[HBM], out_specs=HBM,
        scratch_shapes=[pltpu.SemaphoreType.DMA((n_sems,)), pltpu.SemaphoreType.DMA((n_sems,))],
        name="gather_weights",
    )(packed)


D2D_STREAMS = 16


def _swap_halves(g):
    n, rows, _ = g.shape
    half = rows // 2
    per = D2D_STREAMS // n
    piece = half // per

    def body(g_ref, theirs_ref, send_sems, recv_sems):
        x, y, c = _me()
        give = (1 - c) * half
        copies = []
        for j in range(n):
            for p in range(per):
                k = j * per + p
                cp = pltpu.make_async_remote_copy(
                    src_ref=g_ref.at[j, pl.ds(pl.multiple_of(give + p * piece, 8), piece), :],
                    dst_ref=theirs_ref.at[j, pl.ds(p * piece, piece), :],
                    send_sem=send_sems.at[k], recv_sem=recv_sems.at[k], device_id=(x, y, 1 - c), device_id_type=MESH)
                cp.start()
                copies.append(cp)
        for cp in copies:
            cp.wait()

    return pl.pallas_call(
        body, out_shape=jax.ShapeDtypeStruct((n, half, ROW), g.dtype), in_specs=[HBM], out_specs=HBM,
        scratch_shapes=[pltpu.SemaphoreType.DMA((D2D_STREAMS,)), pltpu.SemaphoreType.DMA((D2D_STREAMS,))],
        name="grad_swap_halves",
    )(g)


def _pair_sum(g, theirs, c):
    n, half, _ = theirs.shape
    tm = _tile(half)
    nb = half // tm

    def body(c_ref, g_ref, t_ref, s32_ref, s16_ref):
        s = g_ref[...] + t_ref[...]
        s32_ref[...] = s
        s16_ref[...] = s.astype(BF16)

    blk = pl.BlockSpec((None, tm, ROW), lambda j, i, c_ref: (j, i, 0))
    return pl.pallas_call(
        body,
        grid_spec=pltpu.PrefetchScalarGridSpec(
            num_scalar_prefetch=1, grid=(n, nb),
            in_specs=[pl.BlockSpec((None, tm, ROW), lambda j, i, c_ref: (j, c_ref[0] * nb + i, 0)), blk],
            out_specs=[blk, blk]),
        out_shape=[jax.ShapeDtypeStruct(theirs.shape, F32), jax.ShapeDtypeStruct(theirs.shape, BF16)],
        name="grad_pair_sum", compiler_params=_params(2),
    )(c.reshape(1).astype(jnp.int32), g, theirs)


def _scatter_chunks(s16):
    _, half, _ = s16.shape

    def body(s16_ref, got_ref, send_sems, recv_sems):
        x, y, c = _me()
        sends = []
        for j, (px, py) in enumerate(_other_chips(x, y)):
            cp = pltpu.make_async_remote_copy(src_ref=s16_ref.at[2 * px + py], dst_ref=got_ref.at[j],
                                              send_sem=send_sems.at[j], recv_sem=recv_sems.at[j],
                                              device_id=(px, py, c), device_id_type=MESH)
            cp.start()
            sends.append(cp)
        for cp in sends:
            cp.wait()

    return pl.pallas_call(
        body, out_shape=jax.ShapeDtypeStruct((3, half, ROW), BF16), in_specs=[HBM], out_specs=HBM,
        scratch_shapes=[pltpu.SemaphoreType.DMA((3,)), pltpu.SemaphoreType.DMA((3,))],
        name="grad_scatter",
    )(s16)


def _chip_sum(s32, got, chip, c):
    _, half, _ = s32.shape
    tm = _tile(half)
    nb = half // tm

    def body(where_ref, own_ref, g0_ref, g1_ref, g2_ref, o_ref):
        o_ref[...] = ((own_ref[...] + g0_ref[...].astype(F32)) + g1_ref[...].astype(F32)) + g2_ref[...].astype(F32)

    got_spec = lambda k: pl.BlockSpec((None, tm, ROW), lambda i, where_ref, k=k: (k, i, 0))
    return pl.pallas_call(
        body,
        grid_spec=pltpu.PrefetchScalarGridSpec(
            num_scalar_prefetch=1, grid=(nb,),
            in_specs=[pl.BlockSpec((None, tm, ROW), lambda i, where_ref: (where_ref[0], i, 0)), got_spec(0), got_spec(1), got_spec(2)],
            out_specs=pl.BlockSpec((tm, ROW), lambda i, where_ref: (where_ref[1] * nb + i, 0))),
        out_shape=jax.ShapeDtypeStruct((2 * half, ROW), F32), name="grad_chip_sum", compiler_params=_params(1),
    )(jnp.stack([chip, c]).astype(jnp.int32), s32, got, got, got)


def _join_halves(f):
    half = f.shape[0] // 2
    piece = half // D2D_STREAMS

    def body(f_ref, out_ref, send_sems, recv_sems):
        x, y, c = _me()
        copies = []
        for p in range(D2D_STREAMS):
            rows = out_ref.at[pl.ds(pl.multiple_of(c * half + p * piece, 8), piece), :]
            cp = pltpu.make_async_remote_copy(src_ref=rows, dst_ref=rows, send_sem=send_sems.at[p], recv_sem=recv_sems.at[p],
                                              device_id=(x, y, 1 - c), device_id_type=MESH)
            cp.start()
            copies.append(cp)
        for cp in copies:
            cp.wait()

    return pl.pallas_call(
        body, out_shape=jax.ShapeDtypeStruct(f.shape, F32), in_specs=[HBM], out_specs=HBM, input_output_aliases={0: 0},
        scratch_shapes=[pltpu.SemaphoreType.DMA((D2D_STREAMS,)), pltpu.SemaphoreType.DMA((D2D_STREAMS,))],
        name="grad_join_halves",
    )(f)


def _all_reduce_small(name, v):
    rows = v.shape[0]

    def body(v_ref, out_ref, slots, send_sems, recv_sems):
        x, y, c = _me()
        me = 4 * x + 2 * y + c
        slots[me] = v_ref[...]
        sends = []
        for r in range(1, 8):
            to = (x ^ (r >> 2), y ^ ((r >> 1) & 1), c ^ (r & 1))
            cp = pltpu.make_async_remote_copy(src_ref=v_ref, dst_ref=slots.at[me], send_sem=send_sems.at[r - 1],
                                              recv_sem=recv_sems.at[r - 1], device_id=to, device_id_type=MESH)
            cp.start()
            sends.append(cp)
        for cp in sends:
            cp.wait()
        total = slots[0]
        for d in range(1, 8):
            total = total + slots[d]
        out_ref[...] = total

    vmem = pl.BlockSpec(memory_space=pltpu.VMEM)
    return pl.pallas_call(
        body, out_shape=jax.ShapeDtypeStruct(v.shape, F32), in_specs=[vmem], out_specs=vmem,
        scratch_shapes=[pltpu.VMEM((8, rows, ROW), F32), pltpu.SemaphoreType.DMA((7,)), pltpu.SemaphoreType.DMA((7,))],
        name=name,
    )(v)


def _adam_fn(w, g, m, v):
    m2 = ADAM_B1 * m + (1.0 - ADAM_B1) * g
    v2 = ADAM_B2 * v + (1.0 - ADAM_B2) * (g * g)
    m_hat = m2 / (1.0 - ADAM_B1 ** ADAM_STEP)
    v_hat = v2 / (1.0 - ADAM_B2 ** ADAM_STEP)
    return -ADAM_LR * (m_hat / (jnp.sqrt(v_hat) + ADAM_EPS) + ADAM_WD * w), m2, v2


def _full_shape(name, shard_shape):
    ax = _shard_axis(name)
    return tuple(n * N_CHIPS if k == ax else n for k, n in enumerate(shard_shape))


def _chip_major(name, full):
    if _shard_axis(name) == 0:
        return full.reshape(N_CHIPS, -1)
    n = full.shape[1] // N_CHIPS
    return jnp.stack([full[:, j * n:(j + 1) * n].reshape(-1) for j in range(N_CHIPS)])


def _from_chip_major(name, rows, shard_shape):
    k, n = shard_shape
    if _shard_axis(name) == 0:
        return rows.reshape(N_CHIPS * k, n)
    return jnp.concatenate([rows[j].reshape(k, n) for j in range(N_CHIPS)], axis=1)


def _row_tile(rows, cap=512):
    return max(t for t in range(8, min(rows, cap) + 1, 8) if rows % t == 0)


def _step(a):
    x_i, y_i, c_i = _me()
    chip = 2 * x_i + y_i
    big_shapes = [a[n].shape for n in BIG]
    sizes = [math.prod(s) for s in big_shapes]
    offsets = [sum(sizes[:k]) for k in range(len(sizes))]

    packed = _pack([a[n].astype(BF16) for n in BIG], BIG_ROWS)
    flat = _gather_shards(packed).reshape(N_CHIPS, -1)
    big = {n: _from_chip_major(n, flat[:, off:off + size], shp) for n, off, size, shp in zip(BIG, offsets, sizes, big_shapes)}

    small = {n: a[n] for n in SMALL}
    convs = [n for n in SMALL if n.endswith("dn_conv_w")]
    placed = []
    for n in convs:
        full = jnp.zeros(_full_shape_conv(a[n].shape), F32)
        placed.append(lax.dynamic_update_slice(full, a[n], (0, chip * a[n].shape[1])))
    conv_sum = _all_reduce_small("gather_conv", _pack(placed, 8))
    for n, full in zip(convs, _unpack(conv_sum, [p.shape for p in placed])):
        small[n] = full * 0.5

    part, dx, grads = local_step(a["x"][0], a["loss_target"][0], big, small)
    loss = lax.psum(0.5 * jnp.sum(part) / D_MODEL, ("x", "y", "c"))

    g_flat = jnp.concatenate([_chip_major(n, grads[n]) for n in BIG], axis=1)
    rows = packed.shape[0]
    g_all = jnp.pad(g_flat, ((0, 0), (0, rows * ROW - g_flat.shape[1]))).reshape(N_CHIPS, rows, ROW)
    s32, s16 = _pair_sum(g_all, _swap_halves(g_all), c_i)
    g_big = _join_halves(_chip_sum(s32, _scatter_chunks(s16), chip, c_i))
    g_big = dict(zip(BIG, _unpack(g_big, big_shapes)))

    g_small_full = _all_reduce_small("reduce_small", _pack([grads[n] for n in SMALL], 8))
    g_small = dict(zip(SMALL, _unpack(g_small_full, [grads[n].shape for n in SMALL])))
    for n in convs:
        g_small[n] = lax.dynamic_slice_in_dim(g_small[n], chip * a[n].shape[1], a[n].shape[1], axis=1)

    outs = {}
    for n in BIG:
        d, m2, v2 = _ew("adam_" + n, _adam_fn, [a[n], g_big[n], a["m_" + n], a["v_" + n]], [],
                        [(a[n].shape[1], F32)] * 3, tm=_row_tile(a[n].shape[0]))
        outs.update({"grad_" + n: g_big[n], "delta_" + n: d, "new_m_" + n: m2, "new_v_" + n: v2})
    pk = lambda prefix: _pack([a[prefix + n] for n in SMALL], 8)
    gs_packed = _pack([g_small[n] for n in SMALL], 8)
    small_bufs = (gs_packed,) + tuple(_ew("adam_small", _adam_fn, [pk(""), gs_packed, pk("m_"), pk("v_")], [], [(ROW, F32)] * 3,
                                          tm=gs_packed.shape[0]))
    small_shapes = [a[n].shape for n in SMALL]
    for key, buf in zip(("grad_", "delta_", "new_m_", "new_v_"), small_bufs):
        outs.update({key + n: val for n, val in zip(SMALL, _unpack(buf, small_shapes))})
    result = [loss, dx[None]]
    for key in ("grad_", "delta_", "new_m_", "new_v_"):
        result += [outs[key + n] for n in WEIGHTS]
    return tuple(result)


def _full_shape_conv(shard_shape):
    return (shard_shape[0], shard_shape[1] * N_CHIPS)


def kernel(x, l0_mix_norm, l0_dn_w_in, l0_dn_conv_w, l0_dn_a_log, l0_dn_dt_bias, l0_dn_out_norm, l0_dn_w_out, l0_ffn_norm, l0_ffn_w_gate_up, l0_ffn_w_down, l1_mix_norm, l1_sb_w_qkv, l1_sb_q_norm, l1_sb_k_norm, l1_sb_w_out, l1_ffn_norm, l1_ffn_w_gate_up, l1_ffn_w_down, l2_mix_norm, l2_mla_w_down, l2_mla_q_a_norm, l2_mla_kv_a_norm, l2_mla_w_uq, l2_mla_w_ukv, l2_mla_q_nope_norm, l2_mla_q_rope_norm, l2_mla_k_nope_norm, l2_mla_k_rope_norm, l2_mla_w_out, l2_ffn_norm, l2_ffn_w_gate_up, l2_ffn_w_down, l3_mix_norm, l3_dn_w_in, l3_dn_conv_w, l3_dn_a_log, l3_dn_dt_bias, l3_dn_out_norm, l3_dn_w_out, l3_ffn_norm, l3_ffn_w_gate_up, l3_ffn_w_down, loss_target, m_l0_mix_norm, m_l0_dn_w_in, m_l0_dn_conv_w, m_l0_dn_a_log, m_l0_dn_dt_bias, m_l0_dn_out_norm, m_l0_dn_w_out, m_l0_ffn_norm, m_l0_ffn_w_gate_up, m_l0_ffn_w_down, m_l1_mix_norm, m_l1_sb_w_qkv, m_l1_sb_q_norm, m_l1_sb_k_norm, m_l1_sb_w_out, m_l1_ffn_norm, m_l1_ffn_w_gate_up, m_l1_ffn_w_down, m_l2_mix_norm, m_l2_mla_w_down, m_l2_mla_q_a_norm, m_l2_mla_kv_a_norm, m_l2_mla_w_uq, m_l2_mla_w_ukv, m_l2_mla_q_nope_norm, m_l2_mla_q_rope_norm, m_l2_mla_k_nope_norm, m_l2_mla_k_rope_norm, m_l2_mla_w_out, m_l2_ffn_norm, m_l2_ffn_w_gate_up, m_l2_ffn_w_down, m_l3_mix_norm, m_l3_dn_w_in, m_l3_dn_conv_w, m_l3_dn_a_log, m_l3_dn_dt_bias, m_l3_dn_out_norm, m_l3_dn_w_out, m_l3_ffn_norm, m_l3_ffn_w_gate_up, m_l3_ffn_w_down, v_l0_mix_norm, v_l0_dn_w_in, v_l0_dn_conv_w, v_l0_dn_a_log, v_l0_dn_dt_bias, v_l0_dn_out_norm, v_l0_dn_w_out, v_l0_ffn_norm, v_l0_ffn_w_gate_up, v_l0_ffn_w_down, v_l1_mix_norm, v_l1_sb_w_qkv, v_l1_sb_q_norm, v_l1_sb_k_norm, v_l1_sb_w_out, v_l1_ffn_norm, v_l1_ffn_w_gate_up, v_l1_ffn_w_down, v_l2_mix_norm, v_l2_mla_w_down, v_l2_mla_q_a_norm, v_l2_mla_kv_a_norm, v_l2_mla_w_uq, v_l2_mla_w_ukv, v_l2_mla_q_nope_norm, v_l2_mla_q_rope_norm, v_l2_mla_k_nope_norm, v_l2_mla_k_rope_norm, v_l2_mla_w_out, v_l2_ffn_norm, v_l2_ffn_w_gate_up, v_l2_ffn_w_down, v_l3_mix_norm, v_l3_dn_w_in, v_l3_dn_conv_w, v_l3_dn_a_log, v_l3_dn_dt_bias, v_l3_dn_out_norm, v_l3_dn_w_out, v_l3_ffn_norm, v_l3_ffn_w_gate_up, v_l3_ffn_w_down):
    return _step(dict(locals()))
```

```python
import functools
import math

import jax
import jax.numpy as jnp
from jax import lax
from jax.experimental import pallas as pl
from jax.experimental.pallas import tpu as pltpu

F32, BF16 = jnp.float32, jnp.bfloat16
MESH = pl.DeviceIdType.MESH

D_MODEL = 1024
N_HEADS = 8
HEAD = 128
FFN_HIDDEN = 2816
DN_CHUNK = 64
NORM_EPS = 1e-6
MLA_ROPE = 64
MLA_QK = 192
ROPE_THETA = 10000.0
ADAM_LR, ADAM_B1, ADAM_B2, ADAM_EPS, ADAM_WD, ADAM_STEP = 0.001, 0.9, 0.999, 1e-08, 0.01, 10
N_CHIPS = 4
LANES = 128
VMEM_LIMIT = 56 * 2 ** 20


def _params(n_grid):
    return pltpu.CompilerParams(dimension_semantics=("arbitrary",) * n_grid, vmem_limit_bytes=VMEM_LIMIT)


_MIXERS = (
    ("dn_w_in", "dn_conv_w", "dn_a_log", "dn_dt_bias", "dn_out_norm", "dn_w_out"),
    ("sb_w_qkv", "sb_q_norm", "sb_k_norm", "sb_w_out"),
    ("mla_w_down", "mla_q_a_norm", "mla_kv_a_norm", "mla_w_uq", "mla_w_ukv", "mla_q_nope_norm",
     "mla_q_rope_norm", "mla_k_nope_norm", "mla_k_rope_norm", "mla_w_out"),
)
DEPTH = 4


def _layer_names(i):
    p = "l%d_" % i
    return [p + "mix_norm"] + [p + n for n in _MIXERS[i % 3]] + [p + "ffn_norm", p + "ffn_w_gate_up", p + "ffn_w_down"]


WEIGHTS = [n for i in range(DEPTH) for n in _layer_names(i)]
_ROW_SHARDED = ("w_out", "ffn_w_down", "mla_w_down")
_COL_SHARDED = ("dn_w_in", "sb_w_qkv", "mla_w_uq", "mla_w_ukv", "ffn_w_gate_up")


def _shard_axis(name):
    if name.endswith(_ROW_SHARDED):
        return 0
    if name.endswith(_COL_SHARDED):
        return 1
    return None


BIG = [n for n in WEIGHTS if _shard_axis(n) is not None]
SMALL = [n for n in WEIGHTS if _shard_axis(n) is None]


_DN = {"nn": (((1,), (0,)), ((), ())), "nt": (((1,), (1,)), ((), ())), "tn": (((0,), (0,)), ((), ()))}
_DN_BATCHED = {"nn": (((2,), (1,)), ((0,), (0,))), "nt": (((2,), (2,)), ((0,), (0,))), "tn": (((1,), (1,)), ((0,), (0,)))}


def _dims(a, kind):
    return _DN_BATCHED[kind] if a.ndim == 3 else _DN[kind]


def _dg(a, b, kind):
    return lax.dot_general(a.astype(BF16), b.astype(BF16), _dims(a, kind), preferred_element_type=F32)


@functools.partial(jax.custom_vjp, nondiff_argnums=(2,))
def bdot(a, b, kind):
    return _dg(a, b, kind)


def _bdot_fwd(a, b, kind):
    return _dg(a, b, kind), (a, b)


def _bdot_bwd(kind, res, ct):
    a, b = res
    if kind == "nn":
        return _dg(ct, b, "nt"), _dg(a, ct, "tn")
    if kind == "nt":
        return _dg(ct, b, "nn"), _dg(ct, a, "tn")
    return _dg(b, ct, "nt"), _dg(a, ct, "nn")


bdot.defvjp(_bdot_fwd, _bdot_bwd)


def _split(a, terms):
    out = []
    for _ in range(terms):
        hi = a.astype(BF16)
        out.append(hi)
        a = a - hi.astype(F32)
    return out


def _xdot(a, b, kind, exact, terms=3):
    if exact == 0:
        return sum(lax.dot_general(a, p, _dims(a, kind), preferred_element_type=F32) for p in _split(b, terms))
    return sum(lax.dot_general(p, b, _dims(a, kind), preferred_element_type=F32) for p in _split(a, terms))


def _tri(n, rel):
    r = lax.broadcasted_iota(jnp.int32, (n, n), 0)
    c = lax.broadcasted_iota(jnp.int32, (n, n), 1)
    return {"le": c <= r, "lt": c < r, "ge": c >= r, "gt": c > r}[rel]


def _running(g):
    n = g.shape[-2]
    return jnp.broadcast_to(_tri(n, "le").astype(BF16), g.shape[:-2] + (n, n))


@jax.custom_vjp
def cumsum_rows(g):
    return _xdot(_running(g), g, "nn", 0)


def _cumsum_fwd(g):
    return cumsum_rows(g), None


def _cumsum_bwd(_, ct):
    return (_xdot(_running(ct), ct, "tn", 0),)


cumsum_rows.defvjp(_cumsum_fwd, _cumsum_bwd)


def _dot3(a, b, kind):
    (ah, al), (bh, bl) = _split(a, 2), _split(b, 2)
    dot = lambda p, q: lax.dot_general(p, q, _dims(a, kind), preferred_element_type=F32)
    return dot(ah, bh) + (dot(ah, bl) + dot(al, bh))


@functools.partial(jax.custom_vjp, nondiff_argnums=(2,))
def _hdot3(a, b, kind):
    return _dot3(a, b, kind)


def _hdot3_fwd(a, b, kind):
    return _dot3(a, b, kind), (a, b)


def _hdot3_bwd(kind, res, ct):
    a, b = res
    if kind == "nn":
        return _dot3(ct, b, "nt"), _dot3(a, ct, "tn")
    if kind == "nt":
        return _dot3(ct, b, "nn"), _dot3(ct, a, "tn")
    return _dot3(b, ct, "nt"), _dot3(a, ct, "nn")


_hdot3.defvjp(_hdot3_fwd, _hdot3_bwd)


def _hdot(a, b):
    return _hdot3(a, b, "nn")


def _unit_lower_inverse(lower):
    n = lower.shape[-1]
    eye = (lax.broadcasted_iota(jnp.int32, (n, n), 0) == lax.broadcasted_iota(jnp.int32, (n, n), 1)).astype(F32)
    m = -lower
    p = eye + m
    for _ in range(int(math.log2(n)) - 1):
        m = _hdot(m, m)
        p = p + _hdot(p, m)
    return p


def _rms(x, g, n=None):
    n = x.shape[-1] if n is None else n
    return x * lax.rsqrt(jnp.sum(x * x, axis=-1, keepdims=True) * (1.0 / n) + NORM_EPS) * g


def _l2(x):
    return x * lax.rsqrt(jnp.sum(x * x, axis=-1, keepdims=True) + NORM_EPS)


def _silu(x):
    return x * jax.nn.sigmoid(x)


def _logsig(z):
    return jnp.minimum(z, 0.0) - jnp.log1p(jnp.exp(-jnp.abs(z)))


@jax.custom_vjp
def _rope(x, cos, sin_lo, sin_hi):
    return x * cos + pltpu.roll(x, 96, 1) * sin_lo + pltpu.roll(x, 32, 1) * sin_hi


def _rope_fwd(x, cos, sin_lo, sin_hi):
    return _rope(x, cos, sin_lo, sin_hi), (cos, sin_lo, sin_hi)


def _rope_bwd(res, ct):
    cos, sin_lo, sin_hi = res
    dx = ct * cos + pltpu.roll(ct * sin_lo, 32, 1) + pltpu.roll(ct * sin_hi, 96, 1)
    return dx, jnp.zeros_like(cos), jnp.zeros_like(sin_lo), jnp.zeros_like(sin_hi)


_rope.defvjp(_rope_fwd, _rope_bwd)


def _tile(n, prefs=(512, 384, 256, 128)):
    for t in prefs:
        if n % t == 0:
            return t
    return n


MM_OUT_TILES = (1024, 1408, 512, 384, 256, 128)


def _mm(name, a, b, kind, out_dtype=F32, add=None):
    if kind == "tn":
        (kdim, m), n = a.shape, b.shape[1]
    else:
        (m, kdim), n = a.shape, (b.shape[0] if kind == "nt" else b.shape[1])
    tm, tn, tk = _tile(m, MM_OUT_TILES), _tile(n, MM_OUT_TILES), _tile(kdim)
    nk = kdim // tk
    a_spec = pl.BlockSpec((tk, tm), lambda i, j, k: (k, i)) if kind == "tn" else pl.BlockSpec((tm, tk), lambda i, j, k: (i, k))
    b_spec = pl.BlockSpec((tn, tk), lambda i, j, k: (j, k)) if kind == "nt" else pl.BlockSpec((tk, tn), lambda i, j, k: (k, j))
    o_spec = pl.BlockSpec((tm, tn), lambda i, j, k: (i, j))
    has_add = add is not None

    def body(*refs):
        a_ref, b_ref = refs[0], refs[1]
        o_ref, acc = refs[-2], refs[-1]
        k = pl.program_id(2)

        @pl.when(k == 0)
        def _():
            acc[...] = jnp.zeros_like(acc)

        acc[...] += _dg(a_ref[...], b_ref[...], kind)

        @pl.when(k == nk - 1)
        def _():
            r = acc[...]
            if has_add:
                r = r + refs[2][...]
            o_ref[...] = r.astype(o_ref.dtype)

    return pl.pallas_call(
        body, grid=(m // tm, n // tn, nk),
        in_specs=[a_spec, b_spec] + ([o_spec] if has_add else []),
        out_specs=o_spec, out_shape=jax.ShapeDtypeStruct((m, n), out_dtype),
        scratch_shapes=[pltpu.VMEM((tm, tn), F32)], name=name, compiler_params=_params(3),
    )(*([a, b] + ([add] if has_add else [])))


class _V:
    def __init__(self, arr, w=None, base=0, diff=True):
        self.arr, self.base, self.diff = arr, base, diff
        self.w = arr.shape[1] if w is None else w

    def spec(self, tm):
        return pl.BlockSpec((tm, self.w), lambda i, b=self.base: (i, b))


def _as_views(ins):
    return [v if isinstance(v, _V) else _V(v) for v in ins]


def _tup(r):
    return tuple(r) if isinstance(r, (tuple, list)) else (r,)


def _ew(name, fn, ins, smalls, outs, tm=256):
    ins = _as_views(ins)
    t = ins[0].arr.shape[0]
    tm = min(tm, t)
    n_in = len(ins) + len(smalls)

    def body(*refs):
        res = _tup(fn(*[r[...] for r in refs[:n_in]]))
        for r, o in zip(refs[n_in:], res):
            r[...] = o.astype(r.dtype)

    return pl.pallas_call(
        body, grid=(t // tm,),
        in_specs=[v.spec(tm) for v in ins] + [pl.BlockSpec(s.shape, lambda i: (0, 0)) for s in smalls],
        out_specs=[pl.BlockSpec((tm, w), lambda i: (i, 0)) for w, _ in outs],
        out_shape=[jax.ShapeDtypeStruct((t, w), dt) for w, dt in outs],
        name=name, compiler_params=_params(1),
    )(*[v.arr for v in ins], *smalls)


def _ew_bwd(name, fn, ins, smalls, cts, tm=256, add=None):
    ins = _as_views(ins)
    t = ins[0].arr.shape[0]
    tm = min(tm, t)
    n_in, n_sm = len(ins), len(smalls)
    diff = [k for k, v in enumerate(ins) if v.diff]
    ct_arrs = [c for c in cts if c is not None]
    has_add = add is not None

    def body(*refs):
        vals = [r[...] for r in refs[:n_in]]
        svals = [r[...] for r in refs[n_in:n_in + n_sm]]
        p = n_in + n_sm
        ct_refs = list(refs[p:p + len(ct_arrs)])
        p += len(ct_arrs)
        add_ref = refs[p] if has_add else None
        p += int(has_add)
        din_refs = refs[p:p + len(diff)]
        dsm_refs = refs[p + len(diff):]

        def f(dv, sv):
            full = list(vals)
            for k, d in zip(diff, dv):
                full[k] = d
            return _tup(fn(*full, *sv))

        res, vjp = jax.vjp(f, [vals[k] for k in diff], svals)
        ctv = tuple(jnp.zeros_like(o) if c is None else ct_refs.pop(0)[...].astype(o.dtype) for c, o in zip(cts, res))
        dv, dsv = vjp(ctv)
        for n, (r, d) in enumerate(zip(din_refs, dv)):
            if n == 0 and has_add:
                d = d + add_ref[...]
            r[...] = d.astype(r.dtype)

        @pl.when(pl.program_id(0) == 0)
        def _():
            for r in dsm_refs:
                r[...] = jnp.zeros_like(r)

        for r, d in zip(dsm_refs, dsv):
            r[...] += d

    row = lambda w: pl.BlockSpec((tm, w), lambda i: (i, 0))
    small_specs = [pl.BlockSpec(s.shape, lambda i: (0, 0)) for s in smalls]
    out = pl.pallas_call(
        body, grid=(t // tm,),
        in_specs=[v.spec(tm) for v in ins] + small_specs + [row(c.shape[1]) for c in ct_arrs]
        + ([row(add.shape[1])] if has_add else []),
        out_specs=[row(ins[k].w) for k in diff] + small_specs,
        out_shape=[jax.ShapeDtypeStruct((t, ins[k].w), F32) for k in diff]
        + [jax.ShapeDtypeStruct(s.shape, F32) for s in smalls],
        name=name, compiler_params=_params(1),
    )(*[v.arr for v in ins], *smalls, *ct_arrs, *([add] if has_add else []))
    return out[:len(diff)], out[len(diff):]


BQ = 256
HPB = 2
SUM_TERMS = 2


def _cat(parts):
    return parts[0] if len(parts) == 1 else jnp.concatenate(parts, axis=1)


def _head_view(ref, hh):
    return ref.at[:, hh * HEAD:(hh + 1) * HEAD]


def _attn_specs(qs, ks, t):
    q_specs = [pl.BlockSpec((BQ, HPB * HEAD), lambda h, i: (i, h)) for _ in qs]
    per_head = pl.BlockSpec((t, HPB * HEAD), lambda h, i: (0, h))
    k_specs = [pl.BlockSpec((t, HEAD), lambda h, i: (0, 0)) if sh else per_head for _, sh in ks]
    return q_specs, k_specs, per_head


def _causal_sweep(i, pair, init, diagonal_first):
    if diagonal_first:
        carry = pair(i, init, True)
        return lax.fori_loop(0, i, lambda s, c: pair(i - 1 - s, c, False), carry)
    carry = lax.fori_loop(0, i, lambda j, c: pair(j, c, False), init)
    return pair(i, carry, True)


def _attn_fwd(name, mode, qs, ks, v):
    t = qs[0].shape[0]
    nq, n = t // BQ, len(qs)
    q_specs, k_specs, per_head = _attn_specs(qs, ks, t)
    shared = [sh for _, sh in ks]

    def body(*refs):
        q_refs, k_refs, v_ref = refs[:n], refs[n:2 * n], refs[2 * n]
        o_ref, st_ref = refs[2 * n + 1], refs[2 * n + 2]
        i = pl.program_id(1)
        row = lax.broadcasted_iota(jnp.int32, (BQ, BQ), 0)
        col = lax.broadcasted_iota(jnp.int32, (BQ, BQ), 1)
        after = _tri(BQ, "lt").astype(BF16)

        def head(hh):
            q = _cat([_head_view(r, hh)[...] for r in q_refs])
            k_h = [kr if sh else _head_view(kr, hh) for kr, sh in zip(k_refs, shared)]
            v_h = _head_view(v_ref, hh)

            def pair(j, carry, masked):
                off = pl.multiple_of(j * BQ, BQ)
                z = _dg(q, _cat([kr[pl.ds(off, BQ), :] for kr in k_h]), "nt")
                vj = v_h[pl.ds(off, BQ), :]
                if mode == "sb":
                    acc, run = carry
                    lsz = _logsig(z)
                    stay = lsz - z
                    if masked:
                        stay = jnp.where(col < row, stay, 0.0)
                    a = jnp.exp(lsz + (run + _xdot(stay, after, "nn", 1, SUM_TERMS)))
                    if masked:
                        a = jnp.where(col < row, a, 0.0)
                    return acc + _dg(a, vj, "nn"), run + jnp.sum(stay, axis=1, keepdims=True)
                m, l, acc = carry
                if masked:
                    z = jnp.where(col <= row, z, -1e30)
                m2 = jnp.maximum(m, jnp.max(z, axis=1, keepdims=True))
                p = jnp.exp(z - m2)
                alpha = jnp.exp(m - m2)
                return m2, alpha * l + jnp.sum(p, axis=1, keepdims=True), alpha * acc + _dg(p, vj, "nn")

            def finish(carry):
                if mode == "sb":
                    acc, run = carry
                    _head_view(o_ref, hh)[...] = acc
                    _head_view(st_ref, hh)[...] = jnp.broadcast_to(run, (BQ, HEAD))
                else:
                    m, l, acc = carry
                    _head_view(o_ref, hh)[...] = acc / l
                    _head_view(st_ref, hh)[...] = jnp.broadcast_to(m + jnp.log(l), (BQ, HEAD))

            zero = jnp.zeros((BQ, 1), F32)
            acc0 = jnp.zeros((BQ, HEAD), F32)
            init = (acc0, zero) if mode == "sb" else (jnp.full((BQ, 1), -1e30, F32), zero, acc0)
            return pair, init, finish

        heads = [head(hh) for hh in range(HPB)]
        both = lambda j, carry, masked: tuple(h[0](j, c, masked) for h, c in zip(heads, carry))
        final = _causal_sweep(i, both, tuple(h[1] for h in heads), diagonal_first=(mode == "sb"))
        for h, c in zip(heads, final):
            h[2](c)

    blk = pl.BlockSpec((BQ, HPB * HEAD), lambda h, i: (i, h))
    return pl.pallas_call(
        body, grid=(N_HEADS // HPB, nq), in_specs=q_specs + k_specs + [per_head], out_specs=[blk, blk],
        out_shape=[jax.ShapeDtypeStruct((t, N_HEADS * HEAD), F32)] * 2, name=name, compiler_params=_params(2),
    )(*qs, *[k for k, _ in ks], v)


def _attn_bwd(name, mode, qs, ks, v, o, stat, do):
    t = qs[0].shape[0]
    nq, n = t // BQ, len(qs)
    q_specs, k_specs, per_head = _attn_specs(qs, ks, t)
    shared = [sh for _, sh in ks]

    def body(*refs):
        q_refs, k_refs, v_ref = refs[:n], refs[n:2 * n], refs[2 * n]
        o_ref, st_ref, do_ref = refs[2 * n + 1:2 * n + 4]
        dq_refs = refs[2 * n + 4:3 * n + 4]
        dk_refs = refs[3 * n + 4:4 * n + 4]
        dv_ref = refs[4 * n + 4]
        g, i = pl.program_id(0), pl.program_id(1)

        @pl.when(i == 0)
        def _():
            dv_ref[...] = jnp.zeros_like(dv_ref)
            for r, sh in zip(dk_refs, shared):
                if not sh:
                    r[...] = jnp.zeros_like(r)

        for r, sh in zip(dk_refs, shared):
            if sh:
                @pl.when((i == 0) & (g == 0))
                def _(r=r):
                    r[...] = jnp.zeros_like(r)

        row = lax.broadcasted_iota(jnp.int32, (BQ, BQ), 0)
        col = lax.broadcasted_iota(jnp.int32, (BQ, BQ), 1)
        upto = _tri(BQ, "ge").astype(BF16)
        before = _tri(BQ, "gt").astype(BF16)

        def head(hh):
            q = _cat([_head_view(r, hh)[...] for r in q_refs])
            k_h = [kr if sh else _head_view(kr, hh) for kr, sh in zip(k_refs, shared)]
            dk_h = [r if sh else _head_view(r, hh) for r, sh in zip(dk_refs, shared)]
            v_h, dv_h = _head_view(v_ref, hh), _head_view(dv_ref, hh)
            do_t = _head_view(do_ref, hh)[...]
            st = _head_view(st_ref, hh)[:, :1]
            if mode == "softmax":
                dsum = jnp.sum(do_t * _head_view(o_ref, hh)[...], axis=1, keepdims=True)

            def pair(j, carry, masked):
                off = pl.multiple_of(j * BQ, BQ)
                kj = _cat([kr[pl.ds(off, BQ), :] for kr in k_h])
                z = _dg(q, kj, "nt")
                da = _dg(do_t, v_h[pl.ds(off, BQ), :], "nt")
                if mode == "sb":
                    dq, pre, gpre = carry
                    lsz = _logsig(z)
                    stay = lsz - z
                    if masked:
                        stay = jnp.where(col < row, stay, 0.0)
                    a = jnp.exp(lsz + (st - (pre + _xdot(stay, upto, "nn", 1, SUM_TERMS))))
                    if masked:
                        a = jnp.where(col < row, a, 0.0)
                    gr = a * da
                    sig = jnp.exp(lsz)
                    dz = gr * (1.0 - sig) - sig * (gpre + _xdot(gr, before, "nn", 1, SUM_TERMS))
                    if masked:
                        dz = jnp.where(col < row, dz, 0.0)
                    tail = (pre + jnp.sum(stay, axis=1, keepdims=True), gpre + jnp.sum(gr, axis=1, keepdims=True))
                else:
                    dq = carry[0]
                    a = jnp.exp(z - st)
                    if masked:
                        a = jnp.where(col <= row, a, 0.0)
                    dz = a * (da - dsum)
                    tail = ()
                dk = _dg(dz, q, "tn")
                for p, r in enumerate(dk_h):
                    r[pl.ds(off, BQ), :] += dk[:, p * HEAD:(p + 1) * HEAD]
                dv_h[pl.ds(off, BQ), :] += _dg(a, do_t, "tn")
                return (dq + _dg(dz, kj, "nn"),) + tail

            def finish(carry):
                for p, r in enumerate(dq_refs):
                    _head_view(r, hh)[...] = carry[0][:, p * HEAD:(p + 1) * HEAD]

            zero = jnp.zeros((BQ, 1), F32)
            init = (jnp.zeros((BQ, n * HEAD), F32),) + ((zero, zero) if mode == "sb" else ())
            return pair, init, finish

        heads = [head(hh) for hh in range(HPB)]
        both = lambda j, carry, masked: tuple(h[0](j, c, masked) for h, c in zip(heads, carry))
        final = _causal_sweep(i, both, tuple(h[1] for h in heads), diagonal_first=False)
        for h, c in zip(heads, final):
            h[2](c)

    blk = pl.BlockSpec((BQ, HPB * HEAD), lambda h, i: (i, h))
    dk_specs = [pl.BlockSpec((t, HEAD), lambda h, i: (0, 0)) if sh else per_head for sh in shared]
    wide = jax.ShapeDtypeStruct((t, N_HEADS * HEAD), F32)
    out = pl.pallas_call(
        body, grid=(N_HEADS // HPB, nq), in_specs=q_specs + k_specs + [per_head, blk, blk, blk],
        out_specs=[blk] * n + dk_specs + [per_head],
        out_shape=[wide] * n + [jax.ShapeDtypeStruct((t, HEAD), F32) if sh else wide for sh in shared] + [wide],
        name=name, compiler_params=_params(2),
    )(*qs, *[k for k, _ in ks], v, o, stat, do)
    return out[:n], out[n:2 * n], out[2 * n]


def _dn_chunk(q, k, v, g, beta, state):
    c = q.shape[-2]
    gc = cumsum_rows(g)
    gcc = gc[..., :c]
    diff = gcc - jnp.swapaxes(gcc, -1, -2)
    causal, strict = _tri(c, "le"), _tri(c, "lt")
    decay = jnp.where(causal, jnp.exp(jnp.where(causal, diff, 0.0)), 0.0)
    kb = k * beta
    lower = jnp.where(strict, bdot(kb, k, "nt") * decay, 0.0)
    tinv = _unit_lower_inverse(lower)
    eg = jnp.exp(gc)
    u = _hdot(tinv, v * beta)
    w = _hdot(tinv, kb * eg)
    attn = bdot(q, k, "nt") * decay
    glast = gc[..., c - 1:c, :]
    v_new = u - bdot(w, state, "nn")
    o = bdot(q * eg, state, "nn") + bdot(attn, v_new, "nn")
    new_state = state * jnp.exp(glast) + bdot(k * jnp.exp(glast - gc), v_new, "tn")
    return o, new_state


def _stack_heads(ref):
    return jnp.stack([ref[:, h * HEAD:(h + 1) * HEAD] for h in range(N_HEADS)])


def _store_heads(ref, val):
    for h in range(N_HEADS):
        ref[:, h * HEAD:(h + 1) * HEAD] = val[h]


def _dn_fwd(name, q, k, v, g, beta):
    t = q.shape[0]
    nc = t // DN_CHUNK
    wide = N_HEADS * HEAD
    blk = pl.BlockSpec((DN_CHUNK, wide), lambda n: (n, 0))
    st_spec = pl.BlockSpec((N_HEADS, None, HEAD, HEAD), lambda n: (0, n, 0, 0))

    def body(q_ref, k_ref, v_ref, g_ref, b_ref, o_ref, st_ref, state):
        @pl.when(pl.program_id(0) == 0)
        def _():
            state[...] = jnp.zeros_like(state)

        s_in = state[...]
        st_ref[...] = s_in
        o, s_out = _dn_chunk(*[_stack_heads(r) for r in (q_ref, k_ref, v_ref, g_ref, b_ref)], s_in)
        _store_heads(o_ref, o)
        state[...] = s_out

    return pl.pallas_call(
        body, grid=(nc,), in_specs=[blk] * 5, out_specs=[blk, st_spec],
        out_shape=[jax.ShapeDtypeStruct((t, wide), F32), jax.ShapeDtypeStruct((N_HEADS, nc, HEAD, HEAD), F32)],
        scratch_shapes=[pltpu.VMEM((N_HEADS, HEAD, HEAD), F32)], name=name, compiler_params=_params(1),
    )(q, k, v, g, beta)


def _dn_bwd(name, q, k, v, g, beta, states, do):
    t = q.shape[0]
    nc = t // DN_CHUNK
    wide = N_HEADS * HEAD
    blk = pl.BlockSpec((DN_CHUNK, wide), lambda n: (nc - 1 - n, 0))
    st_spec = pl.BlockSpec((N_HEADS, None, HEAD, HEAD), lambda n: (0, nc - 1 - n, 0, 0))

    def body(q_ref, k_ref, v_ref, g_ref, b_ref, st_ref, do_ref, dq_ref, dk_ref, dv_ref, dg_ref, db_ref, dstate):
        @pl.when(pl.program_id(0) == 0)
        def _():
            dstate[...] = jnp.zeros_like(dstate)

        _, vjp = jax.vjp(_dn_chunk, *[_stack_heads(r) for r in (q_ref, k_ref, v_ref, g_ref, b_ref)], st_ref[...])
        cts = vjp((_stack_heads(do_ref), dstate[...]))
        for r, d in zip((dq_ref, dk_ref, dv_ref, dg_ref, db_ref), cts[:5]):
            _store_heads(r, d)
        dstate[...] = cts[5]

    shape = jax.ShapeDtypeStruct((t, wide), F32)
    return pl.pallas_call(
        body, grid=(nc,), in_specs=[blk] * 5 + [st_spec, blk], out_specs=[blk] * 5, out_shape=[shape] * 5,
        scratch_shapes=[pltpu.VMEM((N_HEADS, HEAD, HEAD), F32)], name=name, compiler_params=_params(1),
    )(q, k, v, g, beta, states, do)


CONV_W = 1024
HALO = 8


def _shift_down(cur, prev, s):
    sh = pltpu.roll(cur, s, 0)
    ph = pltpu.roll(prev, s, 0)
    r = lax.broadcasted_iota(jnp.int32, (HALO, cur.shape[1]), 0)
    return jnp.concatenate([jnp.where(r < s, ph, sh[:HALO]), sh[HALO:]], axis=0)


def _shift_up(cur, nxt, s):
    tm = cur.shape[0]
    sh = pltpu.roll(cur, tm - s, 0)
    nh = pltpu.roll(nxt, HALO - s, 0)
    r = lax.broadcasted_iota(jnp.int32, (HALO, cur.shape[1]), 0)
    return jnp.concatenate([sh[:tm - HALO], jnp.where(r >= HALO - s, nh, sh[tm - HALO:])], axis=0)


def _conv_fwd(name, proj, w, tm=256):
    t = proj.shape[0]
    tm = min(tm, t)
    width = w.shape[1]
    per = tm // HALO

    def body(cur_ref, prev_ref, w_ref, y_ref):
        cur = cur_ref[...]
        prev = jnp.where(pl.program_id(0) > 0, prev_ref[...], 0.0)
        y = cur * w_ref[3:4, :]
        for s in (1, 2, 3):
            y = y + _shift_down(cur, prev, s) * w_ref[3 - s:4 - s, :]
        y_ref[...] = y

    return pl.pallas_call(
        body, grid=(t // tm, width // CONV_W),
        in_specs=[pl.BlockSpec((tm, CONV_W), lambda i, c: (i, c)),
                  pl.BlockSpec((HALO, CONV_W), lambda i, c: (jnp.maximum(i * per - 1, 0), c)),
                  pl.BlockSpec((HALO, CONV_W), lambda i, c: (0, c))],
        out_specs=pl.BlockSpec((tm, CONV_W), lambda i, c: (i, c)),
        out_shape=jax.ShapeDtypeStruct((t, width), F32), name=name, compiler_params=_params(2),
    )(proj, proj, w)


def _conv_bwd(name, proj, w, dy, tm=256):
    t = proj.shape[0]
    tm = min(tm, t)
    width = w.shape[1]
    per, nt = tm // HALO, t // tm

    def body(cur_ref, prev_ref, w_ref, dy_ref, nxt_ref, du_ref, dw_ref):
        i = pl.program_id(1)
        cur, dy_t = cur_ref[...], dy_ref[...]
        prev = jnp.where(i > 0, prev_ref[...], 0.0)
        nxt = jnp.where(i < nt - 1, nxt_ref[...], 0.0)
        du = dy_t * w_ref[3:4, :]
        rows = [jnp.sum(dy_t * cur, axis=0, keepdims=True)]
        for s in (1, 2, 3):
            du = du + _shift_up(dy_t, nxt, s) * w_ref[3 - s:4 - s, :]
            rows.insert(0, jnp.sum(dy_t * _shift_down(cur, prev, s), axis=0, keepdims=True))
        du_ref[...] = du

        @pl.when(i == 0)
        def _():
            dw_ref[...] = jnp.zeros_like(dw_ref)

        dw_ref[...] += jnp.concatenate(rows + [jnp.zeros((HALO - 4, CONV_W), F32)], axis=0)

    return pl.pallas_call(
        body, grid=(width // CONV_W, nt),
        in_specs=[pl.BlockSpec((tm, CONV_W), lambda c, i: (i, c)),
                  pl.BlockSpec((HALO, CONV_W), lambda c, i: (jnp.maximum(i * per - 1, 0), c)),
                  pl.BlockSpec((HALO, CONV_W), lambda c, i: (0, c)),
                  pl.BlockSpec((tm, CONV_W), lambda c, i: (i, c)),
                  pl.BlockSpec((HALO, CONV_W), lambda c, i: (jnp.minimum((i + 1) * per, t // HALO - 1), c))],
        out_specs=[pl.BlockSpec((tm, CONV_W), lambda c, i: (i, c)), pl.BlockSpec((HALO, CONV_W), lambda c, i: (0, c))],
        out_shape=[jax.ShapeDtypeStruct((t, width), F32), jax.ShapeDtypeStruct((HALO, width), F32)],
        name=name, compiler_params=_params(2),
    )(proj, proj, w, dy, dy)


def _norm_fn(x, g):
    return _rms(x, g)


def _swiglu_fn(gu):
    return _silu(gu[:, :FFN_HIDDEN]) * gu[:, FFN_HIDDEN:]


def _heads(x):
    return [x[:, h * HEAD:(h + 1) * HEAD] for h in range(x.shape[1] // HEAD)]


def _dn_pre_fn(c, ab, a_log, dt_bias):
    w = N_HEADS * HEAD
    q = [_l2(_silu(x)) * (HEAD ** -0.5) for x in _heads(c[:, :w])]
    k = [_l2(_silu(x)) for x in _heads(c[:, w:2 * w])]
    v = _silu(c[:, 2 * w:])
    g, beta = [], []
    for h in range(N_HEADS):
        gh = -jnp.exp(a_log[:, h:h + 1]) * jax.nn.softplus(ab[:, h:h + 1] + dt_bias[:, h:h + 1])
        bh = jax.nn.sigmoid(ab[:, N_HEADS + h:N_HEADS + h + 1])
        g.append(jnp.broadcast_to(gh, (c.shape[0], HEAD)))
        beta.append(jnp.broadcast_to(bh, (c.shape[0], HEAD)))
    cat = lambda xs: jnp.concatenate(xs, axis=1)
    return cat(q), cat(k), v, cat(g), cat(beta)


def _dn_post_fn(o, z, out_norm):
    return jnp.concatenate([_rms(oh, out_norm) * _silu(zh) for oh, zh in zip(_heads(o), _heads(z))], axis=1)


def _sb_pre_fn(qkv, q_norm, k_norm):
    w = N_HEADS * HEAD
    q = [_rms(x, q_norm) * (HEAD ** -0.5) for x in _heads(qkv[:, :w])]
    k = [_rms(x, k_norm) for x in _heads(qkv[:, w:2 * w])]
    return jnp.concatenate(q, axis=1), jnp.concatenate(k, axis=1), qkv[:, 2 * w:]


def _mla_a_fn(down, cos, sin_lo, sin_hi, q_a_norm, kv_a_norm, k_rope_norm):
    cq = _rms(down[:, :256], q_a_norm)
    ckv = _rms(down[:, 256:384], kv_a_norm)
    kr = _rope(_rms(down[:, 384:], k_rope_norm, MLA_ROPE), cos, sin_lo, sin_hi)
    return cq, ckv, kr


def _mla_b_fn(qf, kvf, cos, sin_lo, sin_hi, q_nope_norm, q_rope_norm, k_nope_norm):
    scale = MLA_QK ** -0.5
    qn, qr, kn, v = [], [], [], []
    for h in range(N_HEADS):
        a = 2 * h * HEAD
        qn.append(_rms(qf[:, a:a + HEAD], q_nope_norm) * scale)
        qr.append(_rope(_rms(qf[:, a + HEAD:a + 2 * HEAD], q_rope_norm, MLA_ROPE), cos, sin_lo, sin_hi) * scale)
        kn.append(_rms(kvf[:, a:a + HEAD], k_nope_norm))
        v.append(kvf[:, a + HEAD:a + 2 * HEAD])
    cat = lambda xs: jnp.concatenate(xs, axis=1)
    return cat(qn), cat(qr), cat(kn), cat(v)


def _rope_tables(t):
    inv_freq = ROPE_THETA ** (-jnp.arange(0, MLA_ROPE, 2, dtype=F32) / MLA_ROPE)
    ang = jnp.arange(t, dtype=F32)[:, None] * inv_freq[None, :]
    cos, sin, zero = jnp.cos(ang), jnp.sin(ang), jnp.zeros((t, MLA_ROPE // 2), F32)
    cat = lambda xs: jnp.concatenate(xs, axis=1)
    return cat([cos, cos, zero, zero]), cat([-sin, zero, zero, zero]), cat([zero, sin, zero, zero])


def _row(v, width=None):
    width = v.shape[0] if width is None else width
    return jnp.pad(v.astype(F32), (0, width - v.shape[0])).reshape(1, width)


def _loss_kernel(y, target):
    t, d = y.shape
    tm = min(256, t)

    def body(y_ref, t_ref, part_ref, dy_ref):
        e = y_ref[...] - t_ref[...]
        dy_ref[...] = e * (1.0 / d)

        @pl.when(pl.program_id(0) == 0)
        def _():
            part_ref[...] = jnp.zeros_like(part_ref)

        part_ref[...] += jnp.sum(e * e, axis=0, keepdims=True)

    blk = pl.BlockSpec((tm, d), lambda i: (i, 0))
    one = pl.BlockSpec((1, d), lambda i: (0, 0))
    return pl.pallas_call(body, grid=(t // tm,), in_specs=[blk, blk], out_specs=[one, blk],
                          out_shape=[jax.ShapeDtypeStruct((1, d), F32), jax.ShapeDtypeStruct((t, d), F32)],
                          name="loss", compiler_params=_params(1))(y, target)


def _ffn_fwd(p, x, w, sm):
    h, = _ew(p + "ffn_norm", _norm_fn, [x], [sm["ffn_norm"]], [(D_MODEL, BF16)])
    gu = _mm(p + "ffn_gu", h, w["ffn_w_gate_up"], "nn")
    act, = _ew(p + "ffn_act", _swiglu_fn, [gu], [], [(FFN_HIDDEN, BF16)], tm=128)
    y = _mm(p + "ffn_down", act, w["ffn_w_down"], "nn", add=x)
    return y, (x, h, gu, act)


def _ffn_bwd(p, saved, dy, w, sm, grads):
    x, h, gu, act = saved
    dact = _mm(p + "ffn_down_dx", dy, w["ffn_w_down"], "nt")
    grads["ffn_w_down"] = _mm(p + "ffn_down_dw", act, dy, "tn")
    (dgu,), _ = _ew_bwd(p + "ffn_act_bwd", _swiglu_fn, [gu], [], [dact], tm=128)
    dh = _mm(p + "ffn_gu_dx", dgu, w["ffn_w_gate_up"], "nt")
    grads["ffn_w_gate_up"] = _mm(p + "ffn_gu_dw", h, dgu, "tn")
    (dx,), (dg,) = _ew_bwd(p + "ffn_norm_bwd", _norm_fn, [x], [sm["ffn_norm"]], [dh], add=dy)
    grads["ffn_norm"] = dg
    return dx


def _dn_layer_fwd(p, x, w, sm):
    h, = _ew(p + "mix_norm", _norm_fn, [x], [sm["mix_norm"]], [(D_MODEL, BF16)])
    proj = _mm(p + "dn_in", h, w["dn_w_in"], "nn")
    conv = _conv_fwd(p + "dn_conv", proj, sm["dn_conv_w"])
    ab = _V(proj, LANES, 4 * N_HEADS)
    wide = N_HEADS * HEAD
    q, k, v, g, beta = _ew(p + "dn_pre", _dn_pre_fn, [conv, ab], [sm["dn_a_log"], sm["dn_dt_bias"]], [(wide, F32)] * 5)
    o, states = _dn_fwd(p + "dn_core", q, k, v, g, beta)
    z = _V(proj, wide, 3)
    on, = _ew(p + "dn_post", _dn_post_fn, [o, z], [sm["dn_out_norm"]], [(wide, BF16)])
    y = _mm(p + "dn_out", on, w["dn_w_out"], "nn", add=x)
    return y, (x, h, proj, conv, q, k, v, g, beta, o, states, on)


def _dn_layer_bwd(p, saved, dy, w, sm, grads):
    x, h, proj, conv, q, k, v, g, beta, o, states, on = saved
    wide = N_HEADS * HEAD
    don = _mm(p + "dn_out_dx", dy, w["dn_w_out"], "nt")
    grads["dn_w_out"] = _mm(p + "dn_out_dw", on, dy, "tn")
    (do, dz), (d_out_norm,) = _ew_bwd(p + "dn_post_bwd", _dn_post_fn, [o, _V(proj, wide, 3)], [sm["dn_out_norm"]], [don])
    grads["dn_out_norm"] = d_out_norm
    dq, dk, dv, dg, db = _dn_bwd(p + "dn_core_bwd", q, k, v, g, beta, states, do)
    (dconv, dab), (d_a_log, d_dt) = _ew_bwd(p + "dn_pre_bwd", _dn_pre_fn, [conv, _V(proj, LANES, 4 * N_HEADS)],
                                            [sm["dn_a_log"], sm["dn_dt_bias"]], [dq, dk, dv, dg, db])
    grads["dn_a_log"], grads["dn_dt_bias"] = d_a_log, d_dt
    dqkv, dconv_w = _conv_bwd(p + "dn_conv_bwd", proj, sm["dn_conv_w"], dconv)
    grads["dn_conv_w"] = dconv_w
    dproj = jnp.concatenate([dqkv, dz, dab], axis=1)
    dh = _mm(p + "dn_in_dx", dproj, w["dn_w_in"], "nt")
    grads["dn_w_in"] = _mm(p + "dn_in_dw", h, dproj, "tn")
    (dx,), (dgain,) = _ew_bwd(p + "mix_norm_bwd", _norm_fn, [x], [sm["mix_norm"]], [dh], add=dy)
    grads["mix_norm"] = dgain
    return dx


def _sb_layer_fwd(p, x, w, sm):
    h, = _ew(p + "mix_norm", _norm_fn, [x], [sm["mix_norm"]], [(D_MODEL, BF16)])
    qkv = _mm(p + "sb_qkv", h, w["sb_w_qkv"], "nn")
    wide = N_HEADS * HEAD
    q, k, v = _ew(p + "sb_pre", _sb_pre_fn, [qkv], [sm["sb_q_norm"], sm["sb_k_norm"]], [(wide, BF16)] * 3)
    o, stat = _attn_fwd(p + "sb_core", "sb", [q], [(k, False)], v)
    y = _mm(p + "sb_out", o, w["sb_w_out"], "nn", add=x)
    return y, (x, h, qkv, q, k, v, o, stat)


def _sb_layer_bwd(p, saved, dy, w, sm, grads):
    x, h, qkv, q, k, v, o, stat = saved
    do = _mm(p + "sb_out_dx", dy, w["sb_w_out"], "nt")
    grads["sb_w_out"] = _mm(p + "sb_out_dw", o, dy, "tn")
    (dq,), (dk,), dv = _attn_bwd(p + "sb_core_bwd", "sb", [q], [(k, False)], v, o, stat, do)
    (dqkv,), (dqn, dkn) = _ew_bwd(p + "sb_pre_bwd", _sb_pre_fn, [qkv], [sm["sb_q_norm"], sm["sb_k_norm"]], [dq, dk, dv])
    grads["sb_q_norm"], grads["sb_k_norm"] = dqn, dkn
    dh = _mm(p + "sb_qkv_dx", dqkv, w["sb_w_qkv"], "nt")
    grads["sb_w_qkv"] = _mm(p + "sb_qkv_dw", h, dqkv, "tn")
    (dx,), (dgain,) = _ew_bwd(p + "mix_norm_bwd", _norm_fn, [x], [sm["mix_norm"]], [dh], add=dy)
    grads["mix_norm"] = dgain
    return dx


def _mla_layer_fwd(p, x, w, sm):
    t = x.shape[0]
    tabs = [_V(a, diff=False) for a in _rope_tables(t)]
    h, = _ew(p + "mix_norm", _norm_fn, [x], [sm["mix_norm"]], [(D_MODEL, BF16)])
    down = _mm(p + "mla_down", h, w["mla_w_down"], "nn")
    sm_a = [sm["mla_q_a_norm"], sm["mla_kv_a_norm"], sm["mla_k_rope_norm"]]
    cq, ckv, kr = _ew(p + "mla_a", _mla_a_fn, [down] + tabs, sm_a, [(256, BF16), (128, BF16), (128, BF16)])
    qf = _mm(p + "mla_uq", cq, w["mla_w_uq"], "nn")
    kvf = _mm(p + "mla_ukv", ckv, w["mla_w_ukv"], "nn")
    sm_b = [sm["mla_q_nope_norm"], sm["mla_q_rope_norm"], sm["mla_k_nope_norm"]]
    wide = N_HEADS * HEAD
    qn, qr, kn, v = _ew(p + "mla_b", _mla_b_fn, [qf, kvf] + tabs, sm_b, [(wide, BF16)] * 4)
    o, stat = _attn_fwd(p + "mla_core", "softmax", [qn, qr], [(kn, False), (kr, True)], v)
    y = _mm(p + "mla_out", o, w["mla_w_out"], "nn", add=x)
    return y, (x, h, down, cq, ckv, kr, qf, kvf, qn, qr, kn, v, o, stat)


def _mla_layer_bwd(p, saved, dy, w, sm, grads):
    x, h, down, cq, ckv, kr, qf, kvf, qn, qr, kn, v, o, stat = saved
    tabs = [_V(a, diff=False) for a in _rope_tables(x.shape[0])]
    do = _mm(p + "mla_out_dx", dy, w["mla_w_out"], "nt")
    grads["mla_w_out"] = _mm(p + "mla_out_dw", o, dy, "tn")
    (dqn, dqr), (dkn, dkr), dv = _attn_bwd(p + "mla_core_bwd", "softmax", [qn, qr], [(kn, False), (kr, True)],
                                           v, o, stat, do)
    sm_b = [sm["mla_q_nope_norm"], sm["mla_q_rope_norm"], sm["mla_k_nope_norm"]]
    (dqf, dkvf), dsm_b = _ew_bwd(p + "mla_b_bwd", _mla_b_fn, [qf, kvf] + tabs, sm_b, [dqn, dqr, dkn, dv])
    grads["mla_q_nope_norm"], grads["mla_q_rope_norm"], grads["mla_k_nope_norm"] = dsm_b
    dcq = _mm(p + "mla_uq_dx", dqf, w["mla_w_uq"], "nt")
    grads["mla_w_uq"] = _mm(p + "mla_uq_dw", cq, dqf, "tn")
    dckv = _mm(p + "mla_ukv_dx", dkvf, w["mla_w_ukv"], "nt")
    grads["mla_w_ukv"] = _mm(p + "mla_ukv_dw", ckv, dkvf, "tn")
    sm_a = [sm["mla_q_a_norm"], sm["mla_kv_a_norm"], sm["mla_k_rope_norm"]]
    (ddown,), dsm_a = _ew_bwd(p + "mla_a_bwd", _mla_a_fn, [down] + tabs, sm_a, [dcq, dckv, dkr])
    grads["mla_q_a_norm"], grads["mla_kv_a_norm"], grads["mla_k_rope_norm"] = dsm_a
    dh = _mm(p + "mla_down_dx", ddown, w["mla_w_down"], "nt")
    grads["mla_w_down"] = _mm(p + "mla_down_dw", h, ddown, "tn")
    (dx,), (dgain,) = _ew_bwd(p + "mix_norm_bwd", _norm_fn, [x], [sm["mix_norm"]], [dh], add=dy)
    grads["mix_norm"] = dgain
    return dx


_MIX_FWD = (_dn_layer_fwd, _sb_layer_fwd, _mla_layer_fwd)
_MIX_BWD = (_dn_layer_bwd, _sb_layer_bwd, _mla_layer_bwd)


def _pad_cols(a, n):
    return jnp.pad(a, ((0, 0), (0, n - a.shape[1])))


def _prep_big(name, a):
    if name.endswith("dn_w_in"):
        return _pad_cols(a, 4 * N_HEADS * HEAD + LANES)
    if name.endswith("mla_w_down"):
        return _pad_cols(a, 512)
    if name.endswith("mla_w_uq"):
        a3 = a.reshape(a.shape[0], N_HEADS, MLA_QK)
        return jnp.pad(a3, ((0, 0), (0, 0), (0, 2 * HEAD - MLA_QK))).reshape(a.shape[0], N_HEADS * 2 * HEAD)
    return a


def _unprep_big(name, g):
    if name.endswith("dn_w_in"):
        return g[:, :4 * N_HEADS * HEAD + 2 * N_HEADS]
    if name.endswith("mla_w_down"):
        return g[:, :448]
    if name.endswith("mla_w_uq"):
        return g.reshape(g.shape[0], N_HEADS, 2 * HEAD)[:, :, :MLA_QK].reshape(g.shape[0], N_HEADS * MLA_QK)
    return g


def _prep_small(name, a):
    if name.endswith("dn_conv_w"):
        return jnp.pad(a.astype(F32), ((0, HALO - a.shape[0]), (0, 0)))
    if name.endswith(("dn_a_log", "dn_dt_bias", "mla_q_rope_norm", "mla_k_rope_norm")):
        return _row(a, LANES)
    return _row(a)


def _unprep_small(name, g, like):
    if name.endswith("dn_conv_w"):
        return g[:like.shape[0]]
    return g.reshape(-1)[:like.shape[0]]


def local_step(x, target, big, small):
    layers = []
    for i in range(DEPTH):
        p = "l%d_" % i
        w = {n: _prep_big(n, big[p + n]) for n in _MIXERS[i % 3] + ("ffn_w_gate_up", "ffn_w_down") if p + n in big}
        sm = {n: _prep_small(n, small[p + n]) for n in _MIXERS[i % 3] + ("mix_norm", "ffn_norm") if p + n in small}
        layers.append((p, w, sm))
    saved = []
    for i, (p, w, sm) in enumerate(layers):
        x, s_mix = _MIX_FWD[i % 3](p, x, w, sm)
        x, s_ffn = _ffn_fwd(p, x, w, sm)
        saved.append((s_mix, s_ffn))
    part, dx = _loss_kernel(x, target)
    grads = {}
    for i in reversed(range(DEPTH)):
        p, w, sm = layers[i]
        g = {}
        dx = _ffn_bwd(p, saved[i][1], dx, w, sm, g)
        dx = _MIX_BWD[i % 3](p, saved[i][0], dx, w, sm, g)
        for n, val in g.items():
            grads[p + n] = _unprep_big(p + n, val) if p + n in big else _unprep_small(p + n, val, small[p + n])
    return part, dx, grads


ROW = 1024
BIG_ROWS = 1024


def _pack(arrs, rows_multiple):
    flat = jnp.concatenate([a.reshape(-1) for a in arrs])
    rows = -(-flat.shape[0] // (ROW * rows_multiple)) * rows_multiple
    return jnp.pad(flat, (0, rows * ROW - flat.shape[0])).reshape(rows, ROW)


def _unpack(buf, shapes):
    flat, out, off = buf.reshape(-1), [], 0
    for s in shapes:
        n = math.prod(s)
        out.append(flat[off:off + n].reshape(s))
        off += n
    return out


def _me():
    return lax.axis_index("x"), lax.axis_index("y"), lax.axis_index("c")


def _other_chips(x, y):
    return [(1 - x, y), (x, 1 - y), (1 - x, 1 - y)]


HBM = pl.BlockSpec(memory_space=pl.ANY)


OWN_STREAMS = 4


def _gather_shards(packed):
    rows = packed.shape[0]
    half = rows // 2

    def body(x_ref, out_ref, send_sems, recv_sems):
        x, y, c = _me()
        sibling, chips = (x, y, 1 - c), _other_chips(x, y)

        def part(px, py, pc):
            return out_ref.at[2 * px + py, pl.ds(pl.multiple_of(pc * half, 16), half), :]

        def copy(k, block, to, src=None):
            return pltpu.make_async_remote_copy(
                src_ref=part(*block) if src is None else src, dst_ref=part(*block),
                send_sem=send_sems.at[k], recv_sem=recv_sems.at[k], device_id=to, device_id_type=MESH)

        my_half = x_ref.at[pl.ds(pl.multiple_of(c * half, 16), half), :]
        first = [copy(j, (x, y, c), (*chip, c), src=my_half) for j, chip in enumerate(chips)]
        piece = rows // OWN_STREAMS
        for p in range(OWN_STREAMS):
            rows_p = pl.ds(p * piece, piece)
            first.append(pltpu.make_async_remote_copy(
                src_ref=x_ref.at[rows_p, :], dst_ref=out_ref.at[2 * x + y, rows_p, :], send_sem=send_sems.at[6 + p],
                recv_sem=recv_sems.at[6 + p], device_id=sibling, device_id_type=MESH))
        for cp in first:
            cp.start()
        passed = [copy(3 + j, (*chip, c), sibling) for j, chip in enumerate(chips)]
        for j, chip in enumerate(chips):
            copy(j, (*chip, c), (x, y, c)).wait_recv()
            passed[j].start()
        for j, chip in enumerate(chips):
            copy(3 + j, (*chip, 1 - c), (x, y, c)).wait_recv()
        for cp in first[3:]:
            cp.wait_recv()
        for cp in first + passed:
            cp.wait_send()

    n_sems = 6 + OWN_STREAMS
    return pl.pallas_call(
        body, out_shape=jax.ShapeDtypeStruct((N_CHIPS,) + packed.shape, packed.dtype), in_specs=[HBM], out_specs=HBM,
        scratch_shapes=[pltpu.SemaphoreType.DMA((n_sems,)), pltpu.SemaphoreType.DMA((n_sems,))],
        name="gather_weights",
    )(packed)


D2D_STREAMS = 16


def _swap_halves(g):
    n, rows, _ = g.shape
    half = rows // 2
    per = D2D_STREAMS // n
    piece = half // per

    def body(g_ref, theirs_ref, send_sems, recv_sems):
        x, y, c = _me()
        give = (1 - c) * half
        copies = []
        for j in range(n):
            for p in range(per):
                k = j * per + p
                cp = pltpu.make_async_remote_copy(
                    src_ref=g_ref.at[j, pl.ds(pl.multiple_of(give + p * piece, 8), piece), :],
                    dst_ref=theirs_ref.at[j, pl.ds(p * piece, piece), :],
                    send_sem=send_sems.at[k], recv_sem=recv_sems.at[k], device_id=(x, y, 1 - c), device_id_type=MESH)
                cp.start()
                copies.append(cp)
        for cp in copies:
            cp.wait()

    return pl.pallas_call(
        body, out_shape=jax.ShapeDtypeStruct((n, half, ROW), g.dtype), in_specs=[HBM], out_specs=HBM,
        scratch_shapes=[pltpu.SemaphoreType.DMA((D2D_STREAMS,)), pltpu.SemaphoreType.DMA((D2D_STREAMS,))],
        name="grad_swap_halves",
    )(g)


def _pair_sum(g, theirs, c):
    n, half, _ = theirs.shape
    tm = _tile(half)
    nb = half // tm

    def body(c_ref, g_ref, t_ref, s32_ref, s16_ref):
        s = g_ref[...] + t_ref[...]
        s32_ref[...] = s
        s16_ref[...] = s.astype(BF16)

    blk = pl.BlockSpec((None, tm, ROW), lambda j, i, c_ref: (j, i, 0))
    return pl.pallas_call(
        body,
        grid_spec=pltpu.PrefetchScalarGridSpec(
            num_scalar_prefetch=1, grid=(n, nb),
            in_specs=[pl.BlockSpec((None, tm, ROW), lambda j, i, c_ref: (j, c_ref[0] * nb + i, 0)), blk],
            out_specs=[blk, blk]),
        out_shape=[jax.ShapeDtypeStruct(theirs.shape, F32), jax.ShapeDtypeStruct(theirs.shape, BF16)],
        name="grad_pair_sum", compiler_params=_params(2),
    )(c.reshape(1).astype(jnp.int32), g, theirs)


def _scatter_chunks(s16):
    _, half, _ = s16.shape

    def body(s16_ref, got_ref, send_sems, recv_sems):
        x, y, c = _me()
        sends = []
        for j, (px, py) in enumerate(_other_chips(x, y)):
            cp = pltpu.make_async_remote_copy(src_ref=s16_ref.at[2 * px + py], dst_ref=got_ref.at[j],
                                              send_sem=send_sems.at[j], recv_sem=recv_sems.at[j],
                                              device_id=(px, py, c), device_id_type=MESH)
            cp.start()
            sends.append(cp)
        for cp in sends:
            cp.wait()

    return pl.pallas_call(
        body, out_shape=jax.ShapeDtypeStruct((3, half, ROW), BF16), in_specs=[HBM], out_specs=HBM,
        scratch_shapes=[pltpu.SemaphoreType.DMA((3,)), pltpu.SemaphoreType.DMA((3,))],
        name="grad_scatter",
    )(s16)


def _chip_sum(s32, got, chip, c):
    _, half, _ = s32.shape
    tm = _tile(half)
    nb = half // tm

    def body(where_ref, own_ref, g0_ref, g1_ref, g2_ref, o_ref):
        o_ref[...] = ((own_ref[...] + g0_ref[...].astype(F32)) + g1_ref[...].astype(F32)) + g2_ref[...].astype(F32)

    got_spec = lambda k: pl.BlockSpec((None, tm, ROW), lambda i, where_ref, k=k: (k, i, 0))
    return pl.pallas_call(
        body,
        grid_spec=pltpu.PrefetchScalarGridSpec(
            num_scalar_prefetch=1, grid=(nb,),
            in_specs=[pl.BlockSpec((None, tm, ROW), lambda i, where_ref: (where_ref[0], i, 0)), got_spec(0), got_spec(1), got_spec(2)],
            out_specs=pl.BlockSpec((tm, ROW), lambda i, where_ref: (where_ref[1] * nb + i, 0))),
        out_shape=jax.ShapeDtypeStruct((2 * half, ROW), F32), name="grad_chip_sum", compiler_params=_params(1),
    )(jnp.stack([chip, c]).astype(jnp.int32), s32, got, got, got)


def _join_halves(f):
    half = f.shape[0] // 2
    piece = half // D2D_STREAMS

    def body(f_ref, out_ref, send_sems, recv_sems):
        x, y, c = _me()
        copies = []
        for p in range(D2D_STREAMS):
            rows = out_ref.at[pl.ds(pl.multiple_of(c * half + p * piece, 8), piece), :]
            cp = pltpu.make_async_remote_copy(src_ref=rows, dst_ref=rows, send_sem=send_sems.at[p], recv_sem=recv_sems.at[p],
                                              device_id=(x, y, 1 - c), device_id_type=MESH)
            cp.start()
            copies.append(cp)
        for cp in copies:
            cp.wait()

    return pl.pallas_call(
        body, out_shape=jax.ShapeDtypeStruct(f.shape, F32), in_specs=[HBM], out_specs=HBM, input_output_aliases={0: 0},
        scratch_shapes=[pltpu.SemaphoreType.DMA((D2D_STREAMS,)), pltpu.SemaphoreType.DMA((D2D_STREAMS,))],
        name="grad_join_halves",
    )(f)


def _all_reduce_small(name, v):
    rows = v.shape[0]

    def body(v_ref, out_ref, slots, send_sems, recv_sems):
        x, y, c = _me()
        me = 4 * x + 2 * y + c
        slots[me] = v_ref[...]
        sends = []
        for r in range(1, 8):
            to = (x ^ (r >> 2), y ^ ((r >> 1) & 1), c ^ (r & 1))
            cp = pltpu.make_async_remote_copy(src_ref=v_ref, dst_ref=slots.at[me], send_sem=send_sems.at[r - 1],
                                              recv_sem=recv_sems.at[r - 1], device_id=to, device_id_type=MESH)
            cp.start()
            sends.append(cp)
        for cp in sends:
            cp.wait()
        total = slots[0]
        for d in range(1, 8):
            total = total + slots[d]
        out_ref[...] = total

    vmem = pl.BlockSpec(memory_space=pltpu.VMEM)
    return pl.pallas_call(
        body, out_shape=jax.ShapeDtypeStruct(v.shape, F32), in_specs=[vmem], out_specs=vmem,
        scratch_shapes=[pltpu.VMEM((8, rows, ROW), F32), pltpu.SemaphoreType.DMA((7,)), pltpu.SemaphoreType.DMA((7,))],
        name=name,
    )(v)


def _adam_fn(w, g, m, v):
    m2 = ADAM_B1 * m + (1.0 - ADAM_B1) * g
    v2 = ADAM_B2 * v + (1.0 - ADAM_B2) * (g * g)
    m_hat = m2 / (1.0 - ADAM_B1 ** ADAM_STEP)
    v_hat = v2 / (1.0 - ADAM_B2 ** ADAM_STEP)
    return -ADAM_LR * (m_hat / (jnp.sqrt(v_hat) + ADAM_EPS) + ADAM_WD * w), m2, v2


def _full_shape(name, shard_shape):
    ax = _shard_axis(name)
    return tuple(n * N_CHIPS if k == ax else n for k, n in enumerate(shard_shape))


def _chip_major(name, full):
    if _shard_axis(name) == 0:
        return full.reshape(N_CHIPS, -1)
    n = full.shape[1] // N_CHIPS
    return jnp.stack([full[:, j * n:(j + 1) * n].reshape(-1) for j in range(N_CHIPS)])


def _from_chip_major(name, rows, shard_shape):
    k, n = shard_shape
    if _shard_axis(name) == 0:
        return rows.reshape(N_CHIPS * k, n)
    return jnp.concatenate([rows[j].reshape(k, n) for j in range(N_CHIPS)], axis=1)


def _row_tile(rows, cap=512):
    return max(t for t in range(8, min(rows, cap) + 1, 8) if rows % t == 0)


def _step(a):
    x_i, y_i, c_i = _me()
    chip = 2 * x_i + y_i
    big_shapes = [a[n].shape for n in BIG]
    sizes = [math.prod(s) for s in big_shapes]
    offsets = [sum(sizes[:k]) for k in range(len(sizes))]

    packed = _pack([a[n].astype(BF16) for n in BIG], BIG_ROWS)
    flat = _gather_shards(packed).reshape(N_CHIPS, -1)
    big = {n: _from_chip_major(n, flat[:, off:off + size], shp) for n, off, size, shp in zip(BIG, offsets, sizes, big_shapes)}

    small = {n: a[n] for n in SMALL}
    convs = [n for n in SMALL if n.endswith("dn_conv_w")]
    placed = []
    for n in convs:
        full = jnp.zeros(_full_shape_conv(a[n].shape), F32)
        placed.append(lax.dynamic_update_slice(full, a[n], (0, chip * a[n].shape[1])))
    conv_sum = _all_reduce_small("gather_conv", _pack(placed, 8))
    for n, full in zip(convs, _unpack(conv_sum, [p.shape for p in placed])):
        small[n] = full * 0.5

    part, dx, grads = local_step(a["x"][0], a["loss_target"][0], big, small)
    loss = lax.psum(0.5 * jnp.sum(part) / D_MODEL, ("x", "y", "c"))

    g_flat = jnp.concatenate([_chip_major(n, grads[n]) for n in BIG], axis=1)
    rows = packed.shape[0]
    g_all = jnp.pad(g_flat, ((0, 0), (0, rows * ROW - g_flat.shape[1]))).reshape(N_CHIPS, rows, ROW)
    s32, s16 = _pair_sum(g_all, _swap_halves(g_all), c_i)
    g_big = _join_halves(_chip_sum(s32, _scatter_chunks(s16), chip, c_i))
    g_big = dict(zip(BIG, _unpack(g_big, big_shapes)))

    g_small_full = _all_reduce_small("reduce_small", _pack([grads[n] for n in SMALL], 8))
    g_small = dict(zip(SMALL, _unpack(g_small_full, [grads[n].shape for n in SMALL])))
    for n in convs:
        g_small[n] = lax.dynamic_slice_in_dim(g_small[n], chip * a[n].shape[1], a[n].shape[1], axis=1)

    outs = {}
    for n in BIG:
        d, m2, v2 = _ew("adam_" + n, _adam_fn, [a[n], g_big[n], a["m_" + n], a["v_" + n]], [],
                        [(a[n].shape[1], F32)] * 3, tm=_row_tile(a[n].shape[0]))
        outs.update({"grad_" + n: g_big[n], "delta_" + n: d, "new_m_" + n: m2, "new_v_" + n: v2})
    pk = lambda prefix: _pack([a[prefix + n] for n in SMALL], 8)
    gs_packed = _pack([g_small[n] for n in SMALL], 8)
    small_bufs = (gs_packed,) + tuple(_ew("adam_small", _adam_fn, [pk(""), gs_packed, pk("m_"), pk("v_")], [], [(ROW, F32)] * 3,
                                          tm=gs_packed.shape[0]))
    small_shapes = [a[n].shape for n in SMALL]
    for key, buf in zip(("grad_", "delta_", "new_m_", "new_v_"), small_bufs):
        outs.update({key + n: val for n, val in zip(SMALL, _unpack(buf, small_shapes))})
    result = [loss, dx[None]]
    for key in ("grad_", "delta_", "new_m_", "new_v_"):
        result += [outs[key + n] for n in WEIGHTS]
    return tuple(result)


def _full_shape_conv(shard_shape):
    return (shard_shape[0], shard_shape[1] * N_CHIPS)


def kernel(x, l0_mix_norm, l0_dn_w_in, l0_dn_conv_w, l0_dn_a_log, l0_dn_dt_bias, l0_dn_out_norm, l0_dn_w_out, l0_ffn_norm, l0_ffn_w_gate_up, l0_ffn_w_down, l1_mix_norm, l1_sb_w_qkv, l1_sb_q_norm, l1_sb_k_norm, l1_sb_w_out, l1_ffn_norm, l1_ffn_w_gate_up, l1_ffn_w_down, l2_mix_norm, l2_mla_w_down, l2_mla_q_a_norm, l2_mla_kv_a_norm, l2_mla_w_uq, l2_mla_w_ukv, l2_mla_q_nope_norm, l2_mla_q_rope_norm, l2_mla_k_nope_norm, l2_mla_k_rope_norm, l2_mla_w_out, l2_ffn_norm, l2_ffn_w_gate_up, l2_ffn_w_down, l3_mix_norm, l3_dn_w_in, l3_dn_conv_w, l3_dn_a_log, l3_dn_dt_bias, l3_dn_out_norm, l3_dn_w_out, l3_ffn_norm, l3_ffn_w_gate_up, l3_ffn_w_down, loss_target, m_l0_mix_norm, m_l0_dn_w_in, m_l0_dn_conv_w, m_l0_dn_a_log, m_l0_dn_dt_bias, m_l0_dn_out_norm, m_l0_dn_w_out, m_l0_ffn_norm, m_l0_ffn_w_gate_up, m_l0_ffn_w_down, m_l1_mix_norm, m_l1_sb_w_qkv, m_l1_sb_q_norm, m_l1_sb_k_norm, m_l1_sb_w_out, m_l1_ffn_norm, m_l1_ffn_w_gate_up, m_l1_ffn_w_down, m_l2_mix_norm, m_l2_mla_w_down, m_l2_mla_q_a_norm, m_l2_mla_kv_a_norm, m_l2_mla_w_uq, m_l2_mla_w_ukv, m_l2_mla_q_nope_norm, m_l2_mla_q_rope_norm, m_l2_mla_k_nope_norm, m_l2_mla_k_rope_norm, m_l2_mla_w_out, m_l2_ffn_norm, m_l2_ffn_w_gate_up, m_l2_ffn_w_down, m_l3_mix_norm, m_l3_dn_w_in, m_l3_dn_conv_w, m_l3_dn_a_log, m_l3_dn_dt_bias, m_l3_dn_out_norm, m_l3_dn_w_out, m_l3_ffn_norm, m_l3_ffn_w_gate_up, m_l3_ffn_w_down, v_l0_mix_norm, v_l0_dn_w_in, v_l0_dn_conv_w, v_l0_dn_a_log, v_l0_dn_dt_bias, v_l0_dn_out_norm, v_l0_dn_w_out, v_l0_ffn_norm, v_l0_ffn_w_gate_up, v_l0_ffn_w_down, v_l1_mix_norm, v_l1_sb_w_qkv, v_l1_sb_q_norm, v_l1_sb_k_norm, v_l1_sb_w_out, v_l1_ffn_norm, v_l1_ffn_w_gate_up, v_l1_ffn_w_down, v_l2_mix_norm, v_l2_mla_w_down, v_l2_mla_q_a_norm, v_l2_mla_kv_a_norm, v_l2_mla_w_uq, v_l2_mla_w_ukv, v_l2_mla_q_nope_norm, v_l2_mla_q_rope_norm, v_l2_mla_k_nope_norm, v_l2_mla_k_rope_norm, v_l2_mla_w_out, v_l2_ffn_norm, v_l2_ffn_w_gate_up, v_l2_ffn_w_down, v_l3_mix_norm, v_l3_dn_w_in, v_l3_dn_conv_w, v_l3_dn_a_log, v_l3_dn_dt_bias, v_l3_dn_out_norm, v_l3_dn_w_out, v_l3_ffn_norm, v_l3_ffn_w_gate_up, v_l3_ffn_w_down):
    return _step(dict(locals()))
```

```python
import functools
import math

import jax
import jax.numpy as jnp
from jax import lax
from jax.experimental import pallas as pl
from jax.experimental.pallas import tpu as pltpu

F32, BF16 = jnp.float32, jnp.bfloat16
MESH = pl.DeviceIdType.MESH

D_MODEL = 1024
N_HEADS = 8
HEAD = 128
FFN_HIDDEN = 2816
DN_CHUNK = 64
NORM_EPS = 1e-6
MLA_ROPE = 64
MLA_QK = 192
ROPE_THETA = 10000.0
ADAM_LR, ADAM_B1, ADAM_B2, ADAM_EPS, ADAM_WD, ADAM_STEP = 0.001, 0.9, 0.999, 1e-08, 0.01, 10
N_CHIPS = 4
LANES = 128
VMEM_LIMIT = 56 * 2 ** 20


def _params(n_grid):
    return pltpu.CompilerParams(dimension_semantics=("arbitrary",) * n_grid, vmem_limit_bytes=VMEM_LIMIT)


_MIXERS = (
    ("dn_w_in", "dn_conv_w", "dn_a_log", "dn_dt_bias", "dn_out_norm", "dn_w_out"),
    ("sb_w_qkv", "sb_q_norm", "sb_k_norm", "sb_w_out"),
    ("mla_w_down", "mla_q_a_norm", "mla_kv_a_norm", "mla_w_uq", "mla_w_ukv", "mla_q_nope_norm",
     "mla_q_rope_norm", "mla_k_nope_norm", "mla_k_rope_norm", "mla_w_out"),
)
DEPTH = 4


def _layer_names(i):
    p = "l%d_" % i
    return [p + "mix_norm"] + [p + n for n in _MIXERS[i % 3]] + [p + "ffn_norm", p + "ffn_w_gate_up", p + "ffn_w_down"]


WEIGHTS = [n for i in range(DEPTH) for n in _layer_names(i)]
_ROW_SHARDED = ("w_out", "ffn_w_down", "mla_w_down")
_COL_SHARDED = ("dn_w_in", "sb_w_qkv", "mla_w_uq", "mla_w_ukv", "ffn_w_gate_up")


def _shard_axis(name):
    if name.endswith(_ROW_SHARDED):
        return 0
    if name.endswith(_COL_SHARDED):
        return 1
    return None


BIG = [n for n in WEIGHTS if _shard_axis(n) is not None]
SMALL = [n for n in WEIGHTS if _shard_axis(n) is None]


_DN = {"nn": (((1,), (0,)), ((), ())), "nt": (((1,), (1,)), ((), ())), "tn": (((0,), (0,)), ((), ()))}
_DN_BATCHED = {"nn": (((2,), (1,)), ((0,), (0,))), "nt": (((2,), (2,)), ((0,), (0,))), "tn": (((1,), (1,)), ((0,), (0,)))}


def _dims(a, kind):
    return _DN_BATCHED[kind] if a.ndim == 3 else _DN[kind]


def _dg(a, b, kind):
    return lax.dot_general(a.astype(BF16), b.astype(BF16), _dims(a, kind), preferred_element_type=F32)


@functools.partial(jax.custom_vjp, nondiff_argnums=(2,))
def bdot(a, b, kind):
    return _dg(a, b, kind)


def _bdot_fwd(a, b, kind):
    return _dg(a, b, kind), (a, b)


def _bdot_bwd(kind, res, ct):
    a, b = res
    if kind == "nn":
        return _dg(ct, b, "nt"), _dg(a, ct, "tn")
    if kind == "nt":
        return _dg(ct, b, "nn"), _dg(ct, a, "tn")
    return _dg(b, ct, "nt"), _dg(a, ct, "nn")


bdot.defvjp(_bdot_fwd, _bdot_bwd)


def _split(a, terms):
    out = []
    for _ in range(terms):
        hi = a.astype(BF16)
        out.append(hi)
        a = a - hi.astype(F32)
    return out


def _xdot(a, b, kind, exact, terms=3):
    if exact == 0:
        return sum(lax.dot_general(a, p, _dims(a, kind), preferred_element_type=F32) for p in _split(b, terms))
    return sum(lax.dot_general(p, b, _dims(a, kind), preferred_element_type=F32) for p in _split(a, terms))


def _tri(n, rel):
    r = lax.broadcasted_iota(jnp.int32, (n, n), 0)
    c = lax.broadcasted_iota(jnp.int32, (n, n), 1)
    return {"le": c <= r, "lt": c < r, "ge": c >= r, "gt": c > r}[rel]


def _running(g):
    n = g.shape[-2]
    return jnp.broadcast_to(_tri(n, "le").astype(BF16), g.shape[:-2] + (n, n))


@jax.custom_vjp
def cumsum_rows(g):
    return _xdot(_running(g), g, "nn", 0)


def _cumsum_fwd(g):
    return cumsum_rows(g), None


def _cumsum_bwd(_, ct):
    return (_xdot(_running(ct), ct, "tn", 0),)


cumsum_rows.defvjp(_cumsum_fwd, _cumsum_bwd)


def _dot3(a, b, kind):
    (ah, al), (bh, bl) = _split(a, 2), _split(b, 2)
    dot = lambda p, q: lax.dot_general(p, q, _dims(a, kind), preferred_element_type=F32)
    return dot(ah, bh) + (dot(ah, bl) + dot(al, bh))


@functools.partial(jax.custom_vjp, nondiff_argnums=(2,))
def _hdot3(a, b, kind):
    return _dot3(a, b, kind)


def _hdot3_fwd(a, b, kind):
    return _dot3(a, b, kind), (a, b)


def _hdot3_bwd(kind, res, ct):
    a, b = res
    if kind == "nn":
        return _dot3(ct, b, "nt"), _dot3(a, ct, "tn")
    if kind == "nt":
        return _dot3(ct, b, "nn"), _dot3(ct, a, "tn")
    return _dot3(b, ct, "nt"), _dot3(a, ct, "nn")


_hdot3.defvjp(_hdot3_fwd, _hdot3_bwd)


def _hdot(a, b):
    return _hdot3(a, b, "nn")


def _unit_lower_inverse(lower):
    n = lower.shape[-1]
    eye = (lax.broadcasted_iota(jnp.int32, (n, n), 0) == lax.broadcasted_iota(jnp.int32, (n, n), 1)).astype(F32)
    m = -lower
    p = eye + m
    for _ in range(int(math.log2(n)) - 1):
        m = _hdot(m, m)
        p = p + _hdot(p, m)
    return p


def _rms(x, g, n=None):
    n = x.shape[-1] if n is None else n
    return x * lax.rsqrt(jnp.sum(x * x, axis=-1, keepdims=True) * (1.0 / n) + NORM_EPS) * g


def _l2(x):
    return x * lax.rsqrt(jnp.sum(x * x, axis=-1, keepdims=True) + NORM_EPS)


def _silu(x):
    return x * jax.nn.sigmoid(x)


def _logsig(z):
    return jnp.minimum(z, 0.0) - jnp.log1p(jnp.exp(-jnp.abs(z)))


@jax.custom_vjp
def _rope(x, cos, sin_lo, sin_hi):
    return x * cos + pltpu.roll(x, 96, 1) * sin_lo + pltpu.roll(x, 32, 1) * sin_hi


def _rope_fwd(x, cos, sin_lo, sin_hi):
    return _rope(x, cos, sin_lo, sin_hi), (cos, sin_lo, sin_hi)


def _rope_bwd(res, ct):
    cos, sin_lo, sin_hi = res
    dx = ct * cos + pltpu.roll(ct * sin_lo, 32, 1) + pltpu.roll(ct * sin_hi, 96, 1)
    return dx, jnp.zeros_like(cos), jnp.zeros_like(sin_lo), jnp.zeros_like(sin_hi)


_rope.defvjp(_rope_fwd, _rope_bwd)


def _tile(n, prefs=(512, 384, 256, 128)):
    for t in prefs:
        if n % t == 0:
            return t
    return n


MM_OUT_TILES = (1024, 1408, 512, 384, 256, 128)


def _mm(name, a, b, kind, out_dtype=F32, add=None):
    if kind == "tn":
        (kdim, m), n = a.shape, b.shape[1]
    else:
        (m, kdim), n = a.shape, (b.shape[0] if kind == "nt" else b.shape[1])
    tm, tn, tk = _tile(m, MM_OUT_TILES), _tile(n, MM_OUT_TILES), _tile(kdim)
    nk = kdim // tk
    a_spec = pl.BlockSpec((tk, tm), lambda i, j, k: (k, i)) if kind == "tn" else pl.BlockSpec((tm, tk), lambda i, j, k: (i, k))
    b_spec = pl.BlockSpec((tn, tk), lambda i, j, k: (j, k)) if kind == "nt" else pl.BlockSpec((tk, tn), lambda i, j, k: (k, j))
    o_spec = pl.BlockSpec((tm, tn), lambda i, j, k: (i, j))
    has_add = add is not None

    def body(*refs):
        a_ref, b_ref = refs[0], refs[1]
        o_ref, acc = refs[-2], refs[-1]
        k = pl.program_id(2)

        @pl.when(k == 0)
        def _():
            acc[...] = jnp.zeros_like(acc)

        acc[...] += _dg(a_ref[...], b_ref[...], kind)

        @pl.when(k == nk - 1)
        def _():
            r = acc[...]
            if has_add:
                r = r + refs[2][...]
            o_ref[...] = r.astype(o_ref.dtype)

    return pl.pallas_call(
        body, grid=(m // tm, n // tn, nk),
        in_specs=[a_spec, b_spec] + ([o_spec] if has_add else []),
        out_specs=o_spec, out_shape=jax.ShapeDtypeStruct((m, n), out_dtype),
        scratch_shapes=[pltpu.VMEM((tm, tn), F32)], name=name, compiler_params=_params(3),
    )(*([a, b] + ([add] if has_add else [])))


class _V:
    def __init__(self, arr, w=None, base=0, diff=True):
        self.arr, self.base, self.diff = arr, base, diff
        self.w = arr.shape[1] if w is None else w

    def spec(self, tm):
        return pl.BlockSpec((tm, self.w), lambda i, b=self.base: (i, b))


def _as_views(ins):
    return [v if isinstance(v, _V) else _V(v) for v in ins]


def _tup(r):
    return tuple(r) if isinstance(r, (tuple, list)) else (r,)


def _ew(name, fn, ins, smalls, outs, tm=256):
    ins = _as_views(ins)
    t = ins[0].arr.shape[0]
    tm = min(tm, t)
    n_in = len(ins) + len(smalls)

    def body(*refs):
        res = _tup(fn(*[r[...] for r in refs[:n_in]]))
        for r, o in zip(refs[n_in:], res):
            r[...] = o.astype(r.dtype)

    return pl.pallas_call(
        body, grid=(t // tm,),
        in_specs=[v.spec(tm) for v in ins] + [pl.BlockSpec(s.shape, lambda i: (0, 0)) for s in smalls],
        out_specs=[pl.BlockSpec((tm, w), lambda i: (i, 0)) for w, _ in outs],
        out_shape=[jax.ShapeDtypeStruct((t, w), dt) for w, dt in outs],
        name=name, compiler_params=_params(1),
    )(*[v.arr for v in ins], *smalls)


def _ew_bwd(name, fn, ins, smalls, cts, tm=256, add=None, ct_dtypes=None):
    ins = _as_views(ins)
    t = ins[0].arr.shape[0]
    tm = min(tm, t)
    n_in, n_sm = len(ins), len(smalls)
    diff = [k for k, v in enumerate(ins) if v.diff]
    ct_dtypes = [F32] * len(diff) if ct_dtypes is None else ct_dtypes
    ct_arrs = [c for c in cts if c is not None]
    has_add = add is not None

    def body(*refs):
        vals = [r[...] for r in refs[:n_in]]
        svals = [r[...] for r in refs[n_in:n_in + n_sm]]
        p = n_in + n_sm
        ct_refs = list(refs[p:p + len(ct_arrs)])
        p += len(ct_arrs)
        add_ref = refs[p] if has_add else None
        p += int(has_add)
        din_refs = refs[p:p + len(diff)]
        dsm_refs = refs[p + len(diff):]

        def f(dv, sv):
            full = list(vals)
            for k, d in zip(diff, dv):
                full[k] = d
            return _tup(fn(*full, *sv))

        res, vjp = jax.vjp(f, [vals[k] for k in diff], svals)
        ctv = tuple(jnp.zeros_like(o) if c is None else ct_refs.pop(0)[...].astype(o.dtype) for c, o in zip(cts, res))
        dv, dsv = vjp(ctv)
        for n, (r, d) in enumerate(zip(din_refs, dv)):
            if n == 0 and has_add:
                d = d + add_ref[...]
            r[...] = d.astype(r.dtype)

        @pl.when(pl.program_id(0) == 0)
        def _():
            for r in dsm_refs:
                r[...] = jnp.zeros_like(r)

        for r, d in zip(dsm_refs, dsv):
            r[...] += d

    row = lambda w: pl.BlockSpec((tm, w), lambda i: (i, 0))
    small_specs = [pl.BlockSpec(s.shape, lambda i: (0, 0)) for s in smalls]
    out = pl.pallas_call(
        body, grid=(t // tm,),
        in_specs=[v.spec(tm) for v in ins] + small_specs + [row(c.shape[1]) for c in ct_arrs]
        + ([row(add.shape[1])] if has_add else []),
        out_specs=[row(ins[k].w) for k in diff] + small_specs,
        out_shape=[jax.ShapeDtypeStruct((t, ins[k].w), dt) for k, dt in zip(diff, ct_dtypes)]
        + [jax.ShapeDtypeStruct(s.shape, F32) for s in smalls],
        name=name, compiler_params=_params(1),
    )(*[v.arr for v in ins], *smalls, *ct_arrs, *([add] if has_add else []))
    return out[:len(diff)], out[len(diff):]


BQ = 256
HPB = 2
SUM_TERMS = 2


def _cat(parts):
    return parts[0] if len(parts) == 1 else jnp.concatenate(parts, axis=1)


def _head_view(ref, hh):
    return ref.at[:, hh * HEAD:(hh + 1) * HEAD]


def _attn_specs(qs, ks, t):
    q_specs = [pl.BlockSpec((BQ, HPB * HEAD), lambda h, i: (i, h)) for _ in qs]
    per_head = pl.BlockSpec((t, HPB * HEAD), lambda h, i: (0, h))
    k_specs = [pl.BlockSpec((t, HEAD), lambda h, i: (0, 0)) if sh else per_head for _, sh in ks]
    return q_specs, k_specs, per_head


def _causal_sweep(i, pair, init, diagonal_first):
    if diagonal_first:
        carry = pair(i, init, True)
        return lax.fori_loop(0, i, lambda s, c: pair(i - 1 - s, c, False), carry)
    carry = lax.fori_loop(0, i, lambda j, c: pair(j, c, False), init)
    return pair(i, carry, True)


def _attn_fwd(name, mode, qs, ks, v):
    t = qs[0].shape[0]
    nq, n = t // BQ, len(qs)
    q_specs, k_specs, per_head = _attn_specs(qs, ks, t)
    shared = [sh for _, sh in ks]

    def body(*refs):
        q_refs, k_refs, v_ref = refs[:n], refs[n:2 * n], refs[2 * n]
        o_ref, st_ref = refs[2 * n + 1], refs[2 * n + 2]
        i = pl.program_id(1)
        row = lax.broadcasted_iota(jnp.int32, (BQ, BQ), 0)
        col = lax.broadcasted_iota(jnp.int32, (BQ, BQ), 1)
        after = _tri(BQ, "lt").astype(BF16)

        def head(hh):
            q = _cat([_head_view(r, hh)[...] for r in q_refs])
            k_h = [kr if sh else _head_view(kr, hh) for kr, sh in zip(k_refs, shared)]
            v_h = _head_view(v_ref, hh)

            def pair(j, carry, masked):
                off = pl.multiple_of(j * BQ, BQ)
                z = _dg(q, _cat([kr[pl.ds(off, BQ), :] for kr in k_h]), "nt")
                vj = v_h[pl.ds(off, BQ), :]
                if mode == "sb":
                    acc, run = carry
                    lsz = _logsig(z)
                    stay = lsz - z
                    if masked:
                        stay = jnp.where(col < row, stay, 0.0)
                    a = jnp.exp(lsz + (run + _xdot(stay, after, "nn", 1, SUM_TERMS)))
                    if masked:
                        a = jnp.where(col < row, a, 0.0)
                    return acc + _dg(a, vj, "nn"), run + jnp.sum(stay, axis=1, keepdims=True)
                m, l, acc = carry
                if masked:
                    z = jnp.where(col <= row, z, -1e30)
                m2 = jnp.maximum(m, jnp.max(z, axis=1, keepdims=True))
                p = jnp.exp(z - m2)
                alpha = jnp.exp(m - m2)
                return m2, alpha * l + jnp.sum(p, axis=1, keepdims=True), alpha * acc + _dg(p, vj, "nn")

            def finish(carry):
                if mode == "sb":
                    acc, run = carry
                    _head_view(o_ref, hh)[...] = acc
                    _head_view(st_ref, hh)[...] = jnp.broadcast_to(run, (BQ, HEAD))
                else:
                    m, l, acc = carry
                    _head_view(o_ref, hh)[...] = acc / l
                    _head_view(st_ref, hh)[...] = jnp.broadcast_to(m + jnp.log(l), (BQ, HEAD))

            zero = jnp.zeros((BQ, 1), F32)
            acc0 = jnp.zeros((BQ, HEAD), F32)
            init = (acc0, zero) if mode == "sb" else (jnp.full((BQ, 1), -1e30, F32), zero, acc0)
            return pair, init, finish

        heads = [head(hh) for hh in range(HPB)]
        both = lambda j, carry, masked: tuple(h[0](j, c, masked) for h, c in zip(heads, carry))
        final = _causal_sweep(i, both, tuple(h[1] for h in heads), diagonal_first=(mode == "sb"))
        for h, c in zip(heads, final):
            h[2](c)

    blk = pl.BlockSpec((BQ, HPB * HEAD), lambda h, i: (i, h))
    return pl.pallas_call(
        body, grid=(N_HEADS // HPB, nq), in_specs=q_specs + k_specs + [per_head], out_specs=[blk, blk],
        out_shape=[jax.ShapeDtypeStruct((t, N_HEADS * HEAD), F32)] * 2, name=name, compiler_params=_params(2),
    )(*qs, *[k for k, _ in ks], v)


def _attn_bwd(name, mode, qs, ks, v, o, stat, do):
    t = qs[0].shape[0]
    nq, n = t // BQ, len(qs)
    q_specs, k_specs, per_head = _attn_specs(qs, ks, t)
    shared = [sh for _, sh in ks]

    def body(*refs):
        q_refs, k_refs, v_ref = refs[:n], refs[n:2 * n], refs[2 * n]
        o_ref, st_ref, do_ref = refs[2 * n + 1:2 * n + 4]
        dq_refs = refs[2 * n + 4:3 * n + 4]
        dk_refs = refs[3 * n + 4:4 * n + 4]
        dv_ref = refs[4 * n + 4]
        g, i = pl.program_id(0), pl.program_id(1)

        @pl.when(i == 0)
        def _():
            dv_ref[...] = jnp.zeros_like(dv_ref)
            for r, sh in zip(dk_refs, shared):
                if not sh:
                    r[...] = jnp.zeros_like(r)

        for r, sh in zip(dk_refs, shared):
            if sh:
                @pl.when((i == 0) & (g == 0))
                def _(r=r):
                    r[...] = jnp.zeros_like(r)

        row = lax.broadcasted_iota(jnp.int32, (BQ, BQ), 0)
        col = lax.broadcasted_iota(jnp.int32, (BQ, BQ), 1)
        upto = _tri(BQ, "ge").astype(BF16)
        before = _tri(BQ, "gt").astype(BF16)

        def head(hh):
            q = _cat([_head_view(r, hh)[...] for r in q_refs])
            k_h = [kr if sh else _head_view(kr, hh) for kr, sh in zip(k_refs, shared)]
            dk_h = [r if sh else _head_view(r, hh) for r, sh in zip(dk_refs, shared)]
            v_h, dv_h = _head_view(v_ref, hh), _head_view(dv_ref, hh)
            do_t = _head_view(do_ref, hh)[...]
            st = _head_view(st_ref, hh)[:, :1]
            if mode == "softmax":
                dsum = jnp.sum(do_t * _head_view(o_ref, hh)[...], axis=1, keepdims=True)

            def pair(j, carry, masked):
                off = pl.multiple_of(j * BQ, BQ)
                kj = _cat([kr[pl.ds(off, BQ), :] for kr in k_h])
                z = _dg(q, kj, "nt")
                da = _dg(do_t, v_h[pl.ds(off, BQ), :], "nt")
                if mode == "sb":
                    dq, pre, gpre = carry
                    lsz = _logsig(z)
                    stay = lsz - z
                    if masked:
                        stay = jnp.where(col < row, stay, 0.0)
                    a = jnp.exp(lsz + (st - (pre + _xdot(stay, upto, "nn", 1, SUM_TERMS))))
                    if masked:
                        a = jnp.where(col < row, a, 0.0)
                    gr = a * da
                    sig = jnp.exp(lsz)
                    dz = gr * (1.0 - sig) - sig * (gpre + _xdot(gr, before, "nn", 1, SUM_TERMS))
                    if masked:
                        dz = jnp.where(col < row, dz, 0.0)
                    tail = (pre + jnp.sum(stay, axis=1, keepdims=True), gpre + jnp.sum(gr, axis=1, keepdims=True))
                else:
                    dq = carry[0]
                    a = jnp.exp(z - st)
                    if masked:
                        a = jnp.where(col <= row, a, 0.0)
                    dz = a * (da - dsum)
                    tail = ()
                dk = _dg(dz, q, "tn")
                for p, r in enumerate(dk_h):
                    r[pl.ds(off, BQ), :] += dk[:, p * HEAD:(p + 1) * HEAD]
                dv_h[pl.ds(off, BQ), :] += _dg(a, do_t, "tn")
                return (dq + _dg(dz, kj, "nn"),) + tail

            def finish(carry):
                for p, r in enumerate(dq_refs):
                    _head_view(r, hh)[...] = carry[0][:, p * HEAD:(p + 1) * HEAD]

            zero = jnp.zeros((BQ, 1), F32)
            init = (jnp.zeros((BQ, n * HEAD), F32),) + ((zero, zero) if mode == "sb" else ())
            return pair, init, finish

        heads = [head(hh) for hh in range(HPB)]
        both = lambda j, carry, masked: tuple(h[0](j, c, masked) for h, c in zip(heads, carry))
        final = _causal_sweep(i, both, tuple(h[1] for h in heads), diagonal_first=False)
        for h, c in zip(heads, final):
            h[2](c)

    blk = pl.BlockSpec((BQ, HPB * HEAD), lambda h, i: (i, h))
    dk_specs = [pl.BlockSpec((t, HEAD), lambda h, i: (0, 0)) if sh else per_head for sh in shared]
    wide = jax.ShapeDtypeStruct((t, N_HEADS * HEAD), F32)
    out = pl.pallas_call(
        body, grid=(N_HEADS // HPB, nq), in_specs=q_specs + k_specs + [per_head, blk, blk, blk],
        out_specs=[blk] * n + dk_specs + [per_head],
        out_shape=[wide] * n + [jax.ShapeDtypeStruct((t, HEAD), F32) if sh else wide for sh in shared] + [wide],
        name=name, compiler_params=_params(2),
    )(*qs, *[k for k, _ in ks], v, o, stat, do)
    return out[:n], out[n:2 * n], out[2 * n]


def _dn_chunk(q, k, v, g, beta, state):
    c = q.shape[-2]
    gc = cumsum_rows(g)
    gcc = gc[..., :c]
    diff = gcc - jnp.swapaxes(gcc, -1, -2)
    causal, strict = _tri(c, "le"), _tri(c, "lt")
    decay = jnp.where(causal, jnp.exp(jnp.where(causal, diff, 0.0)), 0.0)
    kb = k * beta
    lower = jnp.where(strict, bdot(kb, k, "nt") * decay, 0.0)
    tinv = _unit_lower_inverse(lower)
    eg = jnp.exp(gc)
    u = _hdot(tinv, v * beta)
    w = _hdot(tinv, kb * eg)
    attn = bdot(q, k, "nt") * decay
    glast = gc[..., c - 1:c, :]
    v_new = u - bdot(w, state, "nn")
    o = bdot(q * eg, state, "nn") + bdot(attn, v_new, "nn")
    new_state = state * jnp.exp(glast) + bdot(k * jnp.exp(glast - gc), v_new, "tn")
    return o, new_state


def _stack_heads(ref):
    return jnp.stack([ref[:, h * HEAD:(h + 1) * HEAD] for h in range(N_HEADS)])


def _store_heads(ref, val):
    for h in range(N_HEADS):
        ref[:, h * HEAD:(h + 1) * HEAD] = val[h]


def _dn_fwd(name, q, k, v, g, beta):
    t = q.shape[0]
    nc = t // DN_CHUNK
    wide = N_HEADS * HEAD
    blk = pl.BlockSpec((DN_CHUNK, wide), lambda n: (n, 0))
    st_spec = pl.BlockSpec((N_HEADS, None, HEAD, HEAD), lambda n: (0, n, 0, 0))

    def body(q_ref, k_ref, v_ref, g_ref, b_ref, o_ref, st_ref, state):
        @pl.when(pl.program_id(0) == 0)
        def _():
            state[...] = jnp.zeros_like(state)

        s_in = state[...]
        st_ref[...] = s_in
        o, s_out = _dn_chunk(*[_stack_heads(r) for r in (q_ref, k_ref, v_ref, g_ref, b_ref)], s_in)
        _store_heads(o_ref, o)
        state[...] = s_out

    return pl.pallas_call(
        body, grid=(nc,), in_specs=[blk] * 5, out_specs=[blk, st_spec],
        out_shape=[jax.ShapeDtypeStruct((t, wide), F32), jax.ShapeDtypeStruct((N_HEADS, nc, HEAD, HEAD), F32)],
        scratch_shapes=[pltpu.VMEM((N_HEADS, HEAD, HEAD), F32)], name=name, compiler_params=_params(1),
    )(q, k, v, g, beta)


def _dn_bwd(name, q, k, v, g, beta, states, do):
    t = q.shape[0]
    nc = t // DN_CHUNK
    wide = N_HEADS * HEAD
    blk = pl.BlockSpec((DN_CHUNK, wide), lambda n: (nc - 1 - n, 0))
    st_spec = pl.BlockSpec((N_HEADS, None, HEAD, HEAD), lambda n: (0, nc - 1 - n, 0, 0))

    def body(q_ref, k_ref, v_ref, g_ref, b_ref, st_ref, do_ref, dq_ref, dk_ref, dv_ref, dg_ref, db_ref, dstate):
        @pl.when(pl.program_id(0) == 0)
        def _():
            dstate[...] = jnp.zeros_like(dstate)

        _, vjp = jax.vjp(_dn_chunk, *[_stack_heads(r) for r in (q_ref, k_ref, v_ref, g_ref, b_ref)], st_ref[...])
        cts = vjp((_stack_heads(do_ref), dstate[...]))
        for r, d in zip((dq_ref, dk_ref, dv_ref, dg_ref, db_ref), cts[:5]):
            _store_heads(r, d)
        dstate[...] = cts[5]

    shape = jax.ShapeDtypeStruct((t, wide), F32)
    return pl.pallas_call(
        body, grid=(nc,), in_specs=[blk] * 5 + [st_spec, blk], out_specs=[blk] * 5, out_shape=[shape] * 5,
        scratch_shapes=[pltpu.VMEM((N_HEADS, HEAD, HEAD), F32)], name=name, compiler_params=_params(1),
    )(q, k, v, g, beta, states, do)


CONV_W = 1024
HALO = 8


def _shift_down(cur, prev, s):
    sh = pltpu.roll(cur, s, 0)
    ph = pltpu.roll(prev, s, 0)
    r = lax.broadcasted_iota(jnp.int32, (HALO, cur.shape[1]), 0)
    return jnp.concatenate([jnp.where(r < s, ph, sh[:HALO]), sh[HALO:]], axis=0)


def _shift_up(cur, nxt, s):
    tm = cur.shape[0]
    sh = pltpu.roll(cur, tm - s, 0)
    nh = pltpu.roll(nxt, HALO - s, 0)
    r = lax.broadcasted_iota(jnp.int32, (HALO, cur.shape[1]), 0)
    return jnp.concatenate([sh[:tm - HALO], jnp.where(r >= HALO - s, nh, sh[tm - HALO:])], axis=0)


def _conv_fwd(name, proj, w, tm=256):
    t = proj.shape[0]
    tm = min(tm, t)
    width = w.shape[1]
    per = tm // HALO

    def body(cur_ref, prev_ref, w_ref, y_ref):
        cur = cur_ref[...]
        prev = jnp.where(pl.program_id(0) > 0, prev_ref[...], 0.0)
        y = cur * w_ref[3:4, :]
        for s in (1, 2, 3):
            y = y + _shift_down(cur, prev, s) * w_ref[3 - s:4 - s, :]
        y_ref[...] = y

    return pl.pallas_call(
        body, grid=(t // tm, width // CONV_W),
        in_specs=[pl.BlockSpec((tm, CONV_W), lambda i, c: (i, c)),
                  pl.BlockSpec((HALO, CONV_W), lambda i, c: (jnp.maximum(i * per - 1, 0), c)),
                  pl.BlockSpec((HALO, CONV_W), lambda i, c: (0, c))],
        out_specs=pl.BlockSpec((tm, CONV_W), lambda i, c: (i, c)),
        out_shape=jax.ShapeDtypeStruct((t, width), F32), name=name, compiler_params=_params(2),
    )(proj, proj, w)


def _conv_bwd(name, proj, w, dy, tm=256):
    t = proj.shape[0]
    tm = min(tm, t)
    width = w.shape[1]
    per, nt = tm // HALO, t // tm

    def body(cur_ref, prev_ref, w_ref, dy_ref, nxt_ref, du_ref, dw_ref):
        i = pl.program_id(1)
        cur, dy_t = cur_ref[...], dy_ref[...]
        prev = jnp.where(i > 0, prev_ref[...], 0.0)
        nxt = jnp.where(i < nt - 1, nxt_ref[...], 0.0)
        du = dy_t * w_ref[3:4, :]
        rows = [jnp.sum(dy_t * cur, axis=0, keepdims=True)]
        for s in (1, 2, 3):
            du = du + _shift_up(dy_t, nxt, s) * w_ref[3 - s:4 - s, :]
            rows.insert(0, jnp.sum(dy_t * _shift_down(cur, prev, s), axis=0, keepdims=True))
        du_ref[...] = du.astype(du_ref.dtype)

        @pl.when(i == 0)
        def _():
            dw_ref[...] = jnp.zeros_like(dw_ref)

        dw_ref[...] += jnp.concatenate(rows + [jnp.zeros((HALO - 4, CONV_W), F32)], axis=0)

    return pl.pallas_call(
        body, grid=(width // CONV_W, nt),
        in_specs=[pl.BlockSpec((tm, CONV_W), lambda c, i: (i, c)),
                  pl.BlockSpec((HALO, CONV_W), lambda c, i: (jnp.maximum(i * per - 1, 0), c)),
                  pl.BlockSpec((HALO, CONV_W), lambda c, i: (0, c)),
                  pl.BlockSpec((tm, CONV_W), lambda c, i: (i, c)),
                  pl.BlockSpec((HALO, CONV_W), lambda c, i: (jnp.minimum((i + 1) * per, t // HALO - 1), c))],
        out_specs=[pl.BlockSpec((tm, CONV_W), lambda c, i: (i, c)), pl.BlockSpec((HALO, CONV_W), lambda c, i: (0, c))],
        out_shape=[jax.ShapeDtypeStruct((t, width), BF16), jax.ShapeDtypeStruct((HALO, width), F32)],
        name=name, compiler_params=_params(2),
    )(proj, proj, w, dy, dy)


def _norm_fn(x, g):
    return _rms(x, g)


def _swiglu_fn(gu):
    return _silu(gu[:, :FFN_HIDDEN]) * gu[:, FFN_HIDDEN:]


def _heads(x):
    return [x[:, h * HEAD:(h + 1) * HEAD] for h in range(x.shape[1] // HEAD)]


def _dn_pre_fn(c, ab, a_log, dt_bias):
    w = N_HEADS * HEAD
    q = [_l2(_silu(x)) * (HEAD ** -0.5) for x in _heads(c[:, :w])]
    k = [_l2(_silu(x)) for x in _heads(c[:, w:2 * w])]
    v = _silu(c[:, 2 * w:])
    g, beta = [], []
    for h in range(N_HEADS):
        gh = -jnp.exp(a_log[:, h:h + 1]) * jax.nn.softplus(ab[:, h:h + 1] + dt_bias[:, h:h + 1])
        bh = jax.nn.sigmoid(ab[:, N_HEADS + h:N_HEADS + h + 1])
        g.append(jnp.broadcast_to(gh, (c.shape[0], HEAD)))
        beta.append(jnp.broadcast_to(bh, (c.shape[0], HEAD)))
    cat = lambda xs: jnp.concatenate(xs, axis=1)
    return cat(q), cat(k), v, cat(g), cat(beta)


def _dn_post_fn(o, z, out_norm):
    return jnp.concatenate([_rms(oh, out_norm) * _silu(zh) for oh, zh in zip(_heads(o), _heads(z))], axis=1)


def _sb_pre_fn(qkv, q_norm, k_norm):
    w = N_HEADS * HEAD
    q = [_rms(x, q_norm) * (HEAD ** -0.5) for x in _heads(qkv[:, :w])]
    k = [_rms(x, k_norm) for x in _heads(qkv[:, w:2 * w])]
    return jnp.concatenate(q, axis=1), jnp.concatenate(k, axis=1), qkv[:, 2 * w:]


def _mla_a_fn(down, cos, sin_lo, sin_hi, q_a_norm, kv_a_norm, k_rope_norm):
    cq = _rms(down[:, :256], q_a_norm)
    ckv = _rms(down[:, 256:384], kv_a_norm)
    kr = _rope(_rms(down[:, 384:], k_rope_norm, MLA_ROPE), cos, sin_lo, sin_hi)
    return cq, ckv, kr


def _mla_b_fn(qf, kvf, cos, sin_lo, sin_hi, q_nope_norm, q_rope_norm, k_nope_norm):
    scale = MLA_QK ** -0.5
    qn, qr, kn, v = [], [], [], []
    for h in range(N_HEADS):
        a = 2 * h * HEAD
        qn.append(_rms(qf[:, a:a + HEAD], q_nope_norm) * scale)
        qr.append(_rope(_rms(qf[:, a + HEAD:a + 2 * HEAD], q_rope_norm, MLA_ROPE), cos, sin_lo, sin_hi) * scale)
        kn.append(_rms(kvf[:, a:a + HEAD], k_nope_norm))
        v.append(kvf[:, a + HEAD:a + 2 * HEAD])
    cat = lambda xs: jnp.concatenate(xs, axis=1)
    return cat(qn), cat(qr), cat(kn), cat(v)


def _rope_tables(t):
    inv_freq = ROPE_THETA ** (-jnp.arange(0, MLA_ROPE, 2, dtype=F32) / MLA_ROPE)
    ang = jnp.arange(t, dtype=F32)[:, None] * inv_freq[None, :]
    cos, sin, zero = jnp.cos(ang), jnp.sin(ang), jnp.zeros((t, MLA_ROPE // 2), F32)
    cat = lambda xs: jnp.concatenate(xs, axis=1)
    return cat([cos, cos, zero, zero]), cat([-sin, zero, zero, zero]), cat([zero, sin, zero, zero])


def _row(v, width=None):
    width = v.shape[0] if width is None else width
    return jnp.pad(v.astype(F32), (0, width - v.shape[0])).reshape(1, width)


def _loss_kernel(y, target):
    t, d = y.shape
    tm = min(256, t)

    def body(y_ref, t_ref, part_ref, dy_ref):
        e = y_ref[...] - t_ref[...]
        dy_ref[...] = e * (1.0 / d)

        @pl.when(pl.program_id(0) == 0)
        def _():
            part_ref[...] = jnp.zeros_like(part_ref)

        part_ref[...] += jnp.sum(e * e, axis=0, keepdims=True)

    blk = pl.BlockSpec((tm, d), lambda i: (i, 0))
    one = pl.BlockSpec((1, d), lambda i: (0, 0))
    return pl.pallas_call(body, grid=(t // tm,), in_specs=[blk, blk], out_specs=[one, blk],
                          out_shape=[jax.ShapeDtypeStruct((1, d), F32), jax.ShapeDtypeStruct((t, d), F32)],
                          name="loss", compiler_params=_params(1))(y, target)


def _ffn_fwd(p, x, w, sm):
    h, = _ew(p + "ffn_norm", _norm_fn, [x], [sm["ffn_norm"]], [(D_MODEL, BF16)])
    gu = _mm(p + "ffn_gu", h, w["ffn_w_gate_up"], "nn")
    act, = _ew(p + "ffn_act", _swiglu_fn, [gu], [], [(FFN_HIDDEN, BF16)], tm=128)
    y = _mm(p + "ffn_down", act, w["ffn_w_down"], "nn", add=x)
    return y, (x, h, gu, act)


def _ffn_bwd(p, saved, dy, w, sm, grads):
    x, h, gu, act = saved
    dact = _mm(p + "ffn_down_dx", dy, w["ffn_w_down"], "nt")
    grads["ffn_w_down"] = _mm(p + "ffn_down_dw", act, dy, "tn")
    (dgu,), _ = _ew_bwd(p + "ffn_act_bwd", _swiglu_fn, [gu], [], [dact], tm=128, ct_dtypes=[BF16])
    dh = _mm(p + "ffn_gu_dx", dgu, w["ffn_w_gate_up"], "nt")
    grads["ffn_w_gate_up"] = _mm(p + "ffn_gu_dw", h, dgu, "tn")
    (dx,), (dg,) = _ew_bwd(p + "ffn_norm_bwd", _norm_fn, [x], [sm["ffn_norm"]], [dh], add=dy)
    grads["ffn_norm"] = dg
    return dx


def _dn_layer_fwd(p, x, w, sm):
    h, = _ew(p + "mix_norm", _norm_fn, [x], [sm["mix_norm"]], [(D_MODEL, BF16)])
    proj = _mm(p + "dn_in", h, w["dn_w_in"], "nn")
    conv = _conv_fwd(p + "dn_conv", proj, sm["dn_conv_w"])
    ab = _V(proj, LANES, 4 * N_HEADS)
    wide = N_HEADS * HEAD
    q, k, v, g, beta = _ew(p + "dn_pre", _dn_pre_fn, [conv, ab], [sm["dn_a_log"], sm["dn_dt_bias"]], [(wide, F32)] * 5)
    o, states = _dn_fwd(p + "dn_core", q, k, v, g, beta)
    z = _V(proj, wide, 3)
    on, = _ew(p + "dn_post", _dn_post_fn, [o, z], [sm["dn_out_norm"]], [(wide, BF16)])
    y = _mm(p + "dn_out", on, w["dn_w_out"], "nn", add=x)
    return y, (x, h, proj, conv, q, k, v, g, beta, o, states, on)


def _dn_layer_bwd(p, saved, dy, w, sm, grads):
    x, h, proj, conv, q, k, v, g, beta, o, states, on = saved
    wide = N_HEADS * HEAD
    don = _mm(p + "dn_out_dx", dy, w["dn_w_out"], "nt")
    grads["dn_w_out"] = _mm(p + "dn_out_dw", on, dy, "tn")
    (do, dz), (d_out_norm,) = _ew_bwd(p + "dn_post_bwd", _dn_post_fn, [o, _V(proj, wide, 3)], [sm["dn_out_norm"]], [don],
                                          ct_dtypes=[F32, BF16])
    grads["dn_out_norm"] = d_out_norm
    dq, dk, dv, dg, db = _dn_bwd(p + "dn_core_bwd", q, k, v, g, beta, states, do)
    (dconv, dab), (d_a_log, d_dt) = _ew_bwd(p + "dn_pre_bwd", _dn_pre_fn, [conv, _V(proj, LANES, 4 * N_HEADS)],
                                            [sm["dn_a_log"], sm["dn_dt_bias"]], [dq, dk, dv, dg, db], ct_dtypes=[F32, BF16])
    grads["dn_a_log"], grads["dn_dt_bias"] = d_a_log, d_dt
    dqkv, dconv_w = _conv_bwd(p + "dn_conv_bwd", proj, sm["dn_conv_w"], dconv)
    grads["dn_conv_w"] = dconv_w
    dproj = jnp.concatenate([dqkv, dz, dab], axis=1)
    dh = _mm(p + "dn_in_dx", dproj, w["dn_w_in"], "nt")
    grads["dn_w_in"] = _mm(p + "dn_in_dw", h, dproj, "tn")
    (dx,), (dgain,) = _ew_bwd(p + "mix_norm_bwd", _norm_fn, [x], [sm["mix_norm"]], [dh], add=dy)
    grads["mix_norm"] = dgain
    return dx


def _sb_layer_fwd(p, x, w, sm):
    h, = _ew(p + "mix_norm", _norm_fn, [x], [sm["mix_norm"]], [(D_MODEL, BF16)])
    qkv = _mm(p + "sb_qkv", h, w["sb_w_qkv"], "nn")
    wide = N_HEADS * HEAD
    q, k, v = _ew(p + "sb_pre", _sb_pre_fn, [qkv], [sm["sb_q_norm"], sm["sb_k_norm"]], [(wide, BF16)] * 3)
    o, stat = _attn_fwd(p + "sb_core", "sb", [q], [(k, False)], v)
    y = _mm(p + "sb_out", o, w["sb_w_out"], "nn", add=x)
    return y, (x, h, qkv, q, k, v, o, stat)


def _sb_layer_bwd(p, saved, dy, w, sm, grads):
    x, h, qkv, q, k, v, o, stat = saved
    do = _mm(p + "sb_out_dx", dy, w["sb_w_out"], "nt")
    grads["sb_w_out"] = _mm(p + "sb_out_dw", o, dy, "tn")
    (dq,), (dk,), dv = _attn_bwd(p + "sb_core_bwd", "sb", [q], [(k, False)], v, o, stat, do)
    (dqkv,), (dqn, dkn) = _ew_bwd(p + "sb_pre_bwd", _sb_pre_fn, [qkv], [sm["sb_q_norm"], sm["sb_k_norm"]], [dq, dk, dv],
                                  ct_dtypes=[BF16])
    grads["sb_q_norm"], grads["sb_k_norm"] = dqn, dkn
    dh = _mm(p + "sb_qkv_dx", dqkv, w["sb_w_qkv"], "nt")
    grads["sb_w_qkv"] = _mm(p + "sb_qkv_dw", h, dqkv, "tn")
    (dx,), (dgain,) = _ew_bwd(p + "mix_norm_bwd", _norm_fn, [x], [sm["mix_norm"]], [dh], add=dy)
    grads["mix_norm"] = dgain
    return dx


def _mla_layer_fwd(p, x, w, sm):
    t = x.shape[0]
    tabs = [_V(a, diff=False) for a in _rope_tables(t)]
    h, = _ew(p + "mix_norm", _norm_fn, [x], [sm["mix_norm"]], [(D_MODEL, BF16)])
    down = _mm(p + "mla_down", h, w["mla_w_down"], "nn")
    sm_a = [sm["mla_q_a_norm"], sm["mla_kv_a_norm"], sm["mla_k_rope_norm"]]
    cq, ckv, kr = _ew(p + "mla_a", _mla_a_fn, [down] + tabs, sm_a, [(256, BF16), (128, BF16), (128, BF16)])
    qf = _mm(p + "mla_uq", cq, w["mla_w_uq"], "nn")
    kvf = _mm(p + "mla_ukv", ckv, w["mla_w_ukv"], "nn")
    sm_b = [sm["mla_q_nope_norm"], sm["mla_q_rope_norm"], sm["mla_k_nope_norm"]]
    wide = N_HEADS * HEAD
    qn, qr, kn, v = _ew(p + "mla_b", _mla_b_fn, [qf, kvf] + tabs, sm_b, [(wide, BF16)] * 4)
    o, stat = _attn_fwd(p + "mla_core", "softmax", [qn, qr], [(kn, False), (kr, True)], v)
    y = _mm(p + "mla_out", o, w["mla_w_out"], "nn", add=x)
    return y, (x, h, down, cq, ckv, kr, qf, kvf, qn, qr, kn, v, o, stat)


def _mla_layer_bwd(p, saved, dy, w, sm, grads):
    x, h, down, cq, ckv, kr, qf, kvf, qn, qr, kn, v, o, stat = saved
    tabs = [_V(a, diff=False) for a in _rope_tables(x.shape[0])]
    do = _mm(p + "mla_out_dx", dy, w["mla_w_out"], "nt")
    grads["mla_w_out"] = _mm(p + "mla_out_dw", o, dy, "tn")
    (dqn, dqr), (dkn, dkr), dv = _attn_bwd(p + "mla_core_bwd", "softmax", [qn, qr], [(kn, False), (kr, True)],
                                           v, o, stat, do)
    sm_b = [sm["mla_q_nope_norm"], sm["mla_q_rope_norm"], sm["mla_k_nope_norm"]]
    (dqf, dkvf), dsm_b = _ew_bwd(p + "mla_b_bwd", _mla_b_fn, [qf, kvf] + tabs, sm_b, [dqn, dqr, dkn, dv],
                                 ct_dtypes=[BF16, BF16])
    grads["mla_q_nope_norm"], grads["mla_q_rope_norm"], grads["mla_k_nope_norm"] = dsm_b
    dcq = _mm(p + "mla_uq_dx", dqf, w["mla_w_uq"], "nt")
    grads["mla_w_uq"] = _mm(p + "mla_uq_dw", cq, dqf, "tn")
    dckv = _mm(p + "mla_ukv_dx", dkvf, w["mla_w_ukv"], "nt")
    grads["mla_w_ukv"] = _mm(p + "mla_ukv_dw", ckv, dkvf, "tn")
    sm_a = [sm["mla_q_a_norm"], sm["mla_kv_a_norm"], sm["mla_k_rope_norm"]]
    (ddown,), dsm_a = _ew_bwd(p + "mla_a_bwd", _mla_a_fn, [down] + tabs, sm_a, [dcq, dckv, dkr], ct_dtypes=[BF16])
    grads["mla_q_a_norm"], grads["mla_kv_a_norm"], grads["mla_k_rope_norm"] = dsm_a
    dh = _mm(p + "mla_down_dx", ddown, w["mla_w_down"], "nt")
    grads["mla_w_down"] = _mm(p + "mla_down_dw", h, ddown, "tn")
    (dx,), (dgain,) = _ew_bwd(p + "mix_norm_bwd", _norm_fn, [x], [sm["mix_norm"]], [dh], add=dy)
    grads["mix_norm"] = dgain
    return dx


_MIX_FWD = (_dn_layer_fwd, _sb_layer_fwd, _mla_layer_fwd)
_MIX_BWD = (_dn_layer_bwd, _sb_layer_bwd, _mla_layer_bwd)


def _pad_cols(a, n):
    return jnp.pad(a, ((0, 0), (0, n - a.shape[1])))


def _prep_big(name, a):
    if name.endswith("dn_w_in"):
        return _pad_cols(a, 4 * N_HEADS * HEAD + LANES)
    if name.endswith("mla_w_down"):
        return _pad_cols(a, 512)
    if name.endswith("mla_w_uq"):
        a3 = a.reshape(a.shape[0], N_HEADS, MLA_QK)
        return jnp.pad(a3, ((0, 0), (0, 0), (0, 2 * HEAD - MLA_QK))).reshape(a.shape[0], N_HEADS * 2 * HEAD)
    return a


def _unprep_big(name, g):
    if name.endswith("dn_w_in"):
        return g[:, :4 * N_HEADS * HEAD + 2 * N_HEADS]
    if name.endswith("mla_w_down"):
        return g[:, :448]
    if name.endswith("mla_w_uq"):
        return g.reshape(g.shape[0], N_HEADS, 2 * HEAD)[:, :, :MLA_QK].reshape(g.shape[0], N_HEADS * MLA_QK)
    return g


def _prep_small(name, a):
    if name.endswith("dn_conv_w"):
        return jnp.pad(a.astype(F32), ((0, HALO - a.shape[0]), (0, 0)))
    if name.endswith(("dn_a_log", "dn_dt_bias", "mla_q_rope_norm", "mla_k_rope_norm")):
        return _row(a, LANES)
    return _row(a)


def _unprep_small(name, g, like):
    if name.endswith("dn_conv_w"):
        return g[:like.shape[0]]
    return g.reshape(-1)[:like.shape[0]]


def local_step(x, target, big, small):
    layers = []
    for i in range(DEPTH):
        p = "l%d_" % i
        w = {n: _prep_big(n, big[p + n]) for n in _MIXERS[i % 3] + ("ffn_w_gate_up", "ffn_w_down") if p + n in big}
        sm = {n: _prep_small(n, small[p + n]) for n in _MIXERS[i % 3] + ("mix_norm", "ffn_norm") if p + n in small}
        layers.append((p, w, sm))
    saved = []
    for i, (p, w, sm) in enumerate(layers):
        x, s_mix = _MIX_FWD[i % 3](p, x, w, sm)
        x, s_ffn = _ffn_fwd(p, x, w, sm)
        saved.append((s_mix, s_ffn))
    part, dx = _loss_kernel(x, target)
    grads = {}
    for i in reversed(range(DEPTH)):
        p, w, sm = layers[i]
        g = {}
        dx = _ffn_bwd(p, saved[i][1], dx, w, sm, g)
        dx = _MIX_BWD[i % 3](p, saved[i][0], dx, w, sm, g)
        for n, val in g.items():
            grads[p + n] = _unprep_big(p + n, val) if p + n in big else _unprep_small(p + n, val, small[p + n])
    return part, dx, grads


ROW = 1024
BIG_ROWS = 1024


PACK_ALIGN = 16


def _packed_rows(shape):
    return -(-math.prod(shape) // (ROW * PACK_ALIGN)) * PACK_ALIGN


def _as_rows(a, lead=()):
    rows = _packed_rows(a.shape[len(lead):])
    flat = a.reshape(lead + (-1,))
    return jnp.pad(flat, ((0, 0),) * len(lead) + ((0, rows * ROW - flat.shape[-1]),)).reshape(lead + (rows, ROW))


def _pack(arrs, rows_multiple, lead=()):
    blocks = [_as_rows(a, lead) for a in arrs]
    used = sum(b.shape[-2] for b in blocks)
    fill = -(-used // rows_multiple) * rows_multiple - used
    if fill:
        blocks.append(jnp.zeros(lead + (fill, ROW), blocks[0].dtype))
    return jnp.concatenate(blocks, axis=len(lead))


def _unpack(buf, shapes, lead=()):
    out, r0 = [], 0
    for s in shapes:
        rows, n = _packed_rows(s), math.prod(s)
        block = lax.slice_in_dim(buf, r0, r0 + rows, axis=len(lead))
        out.append(block.reshape(lead + (-1,))[..., :n].reshape(lead + tuple(s)))
        r0 += rows
    return out


def _me():
    return lax.axis_index("x"), lax.axis_index("y"), lax.axis_index("c")


def _other_chips(x, y):
    return [(1 - x, y), (x, 1 - y), (1 - x, 1 - y)]


HBM = pl.BlockSpec(memory_space=pl.ANY)


OWN_STREAMS = 4


def _gather_shards(packed):
    rows = packed.shape[0]
    half = rows // 2

    def body(x_ref, out_ref, send_sems, recv_sems):
        x, y, c = _me()
        sibling, chips = (x, y, 1 - c), _other_chips(x, y)

        def part(px, py, pc):
            return out_ref.at[2 * px + py, pl.ds(pl.multiple_of(pc * half, 16), half), :]

        def copy(k, block, to, src=None):
            return pltpu.make_async_remote_copy(
                src_ref=part(*block) if src is None else src, dst_ref=part(*block),
                send_sem=send_sems.at[k], recv_sem=recv_sems.at[k], device_id=to, device_id_type=MESH)

        my_half = x_ref.at[pl.ds(pl.multiple_of(c * half, 16), half), :]
        first = [copy(j, (x, y, c), (*chip, c), src=my_half) for j, chip in enumerate(chips)]
        piece = rows // OWN_STREAMS
        for p in range(OWN_STREAMS):
            rows_p = pl.ds(p * piece, piece)
            first.append(pltpu.make_async_remote_copy(
                src_ref=x_ref.at[rows_p, :], dst_ref=out_ref.at[2 * x + y, rows_p, :], send_sem=send_sems.at[6 + p],
                recv_sem=recv_sems.at[6 + p], device_id=sibling, device_id_type=MESH))
        for cp in first:
            cp.start()
        passed = [copy(3 + j, (*chip, c), sibling) for j, chip in enumerate(chips)]
        for j, chip in enumerate(chips):
            copy(j, (*chip, c), (x, y, c)).wait_recv()
            passed[j].start()
        for j, chip in enumerate(chips):
            copy(3 + j, (*chip, 1 - c), (x, y, c)).wait_recv()
        for cp in first[3:]:
            cp.wait_recv()
        for cp in first + passed:
            cp.wait_send()

    n_sems = 6 + OWN_STREAMS
    return pl.pallas_call(
        body, out_shape=jax.ShapeDtypeStruct((N_CHIPS,) + packed.shape, packed.dtype), in_specs=[HBM], out_specs=HBM,
        scratch_shapes=[pltpu.SemaphoreType.DMA((n_sems,)), pltpu.SemaphoreType.DMA((n_sems,))],
        name="gather_weights",
    )(packed)


D2D_STREAMS = 16


def _swap_halves(g):
    n, rows, _ = g.shape
    half = rows // 2
    per = D2D_STREAMS // n
    piece = half // per

    def body(g_ref, theirs_ref, send_sems, recv_sems):
        x, y, c = _me()
        give = (1 - c) * half
        copies = []
        for j in range(n):
            for p in range(per):
                k = j * per + p
                cp = pltpu.make_async_remote_copy(
                    src_ref=g_ref.at[j, pl.ds(pl.multiple_of(give + p * piece, 8), piece), :],
                    dst_ref=theirs_ref.at[j, pl.ds(p * piece, piece), :],
                    send_sem=send_sems.at[k], recv_sem=recv_sems.at[k], device_id=(x, y, 1 - c), device_id_type=MESH)
                cp.start()
                copies.append(cp)
        for cp in copies:
            cp.wait()

    return pl.pallas_call(
        body, out_shape=jax.ShapeDtypeStruct((n, half, ROW), g.dtype), in_specs=[HBM], out_specs=HBM,
        scratch_shapes=[pltpu.SemaphoreType.DMA((D2D_STREAMS,)), pltpu.SemaphoreType.DMA((D2D_STREAMS,))],
        name="grad_swap_halves",
    )(g)


def _pair_sum(g, theirs, c):
    n, half, _ = theirs.shape
    tm = _tile(half)
    nb = half // tm

    def body(c_ref, g_ref, t_ref, s32_ref, s16_ref):
        s = g_ref[...] + t_ref[...]
        s32_ref[...] = s
        s16_ref[...] = s.astype(BF16)

    blk = pl.BlockSpec((None, tm, ROW), lambda j, i, c_ref: (j, i, 0))
    return pl.pallas_call(
        body,
        grid_spec=pltpu.PrefetchScalarGridSpec(
            num_scalar_prefetch=1, grid=(n, nb),
            in_specs=[pl.BlockSpec((None, tm, ROW), lambda j, i, c_ref: (j, c_ref[0] * nb + i, 0)), blk],
            out_specs=[blk, blk]),
        out_shape=[jax.ShapeDtypeStruct(theirs.shape, F32), jax.ShapeDtypeStruct(theirs.shape, BF16)],
        name="grad_pair_sum", compiler_params=_params(2),
    )(c.reshape(1).astype(jnp.int32), g, theirs)


def _scatter_chunks(s16):
    _, half, _ = s16.shape

    def body(s16_ref, got_ref, send_sems, recv_sems):
        x, y, c = _me()
        sends = []
        for j, (px, py) in enumerate(_other_chips(x, y)):
            cp = pltpu.make_async_remote_copy(src_ref=s16_ref.at[2 * px + py], dst_ref=got_ref.at[j],
                                              send_sem=send_sems.at[j], recv_sem=recv_sems.at[j],
                                              device_id=(px, py, c), device_id_type=MESH)
            cp.start()
            sends.append(cp)
        for cp in sends:
            cp.wait()

    return pl.pallas_call(
        body, out_shape=jax.ShapeDtypeStruct((3, half, ROW), BF16), in_specs=[HBM], out_specs=HBM,
        scratch_shapes=[pltpu.SemaphoreType.DMA((3,)), pltpu.SemaphoreType.DMA((3,))],
        name="grad_scatter",
    )(s16)


def _chip_sum(s32, got, chip, c):
    _, half, _ = s32.shape
    tm = _tile(half)
    nb = half // tm

    def body(where_ref, own_ref, g0_ref, g1_ref, g2_ref, o_ref):
        o_ref[...] = ((own_ref[...] + g0_ref[...].astype(F32)) + g1_ref[...].astype(F32)) + g2_ref[...].astype(F32)

    got_spec = lambda k: pl.BlockSpec((None, tm, ROW), lambda i, where_ref, k=k: (k, i, 0))
    return pl.pallas_call(
        body,
        grid_spec=pltpu.PrefetchScalarGridSpec(
            num_scalar_prefetch=1, grid=(nb,),
            in_specs=[pl.BlockSpec((None, tm, ROW), lambda i, where_ref: (where_ref[0], i, 0)), got_spec(0), got_spec(1), got_spec(2)],
            out_specs=pl.BlockSpec((tm, ROW), lambda i, where_ref: (where_ref[1] * nb + i, 0))),
        out_shape=jax.ShapeDtypeStruct((2 * half, ROW), F32), name="grad_chip_sum", compiler_params=_params(1),
    )(jnp.stack([chip, c]).astype(jnp.int32), s32, got, got, got)


def _join_halves(f):
    half = f.shape[0] // 2
    piece = half // D2D_STREAMS

    def body(f_ref, out_ref, send_sems, recv_sems):
        x, y, c = _me()
        copies = []
        for p in range(D2D_STREAMS):
            rows = out_ref.at[pl.ds(pl.multiple_of(c * half + p * piece, 8), piece), :]
            cp = pltpu.make_async_remote_copy(src_ref=rows, dst_ref=rows, send_sem=send_sems.at[p], recv_sem=recv_sems.at[p],
                                              device_id=(x, y, 1 - c), device_id_type=MESH)
            cp.start()
            copies.append(cp)
        for cp in copies:
            cp.wait()

    return pl.pallas_call(
        body, out_shape=jax.ShapeDtypeStruct(f.shape, F32), in_specs=[HBM], out_specs=HBM, input_output_aliases={0: 0},
        scratch_shapes=[pltpu.SemaphoreType.DMA((D2D_STREAMS,)), pltpu.SemaphoreType.DMA((D2D_STREAMS,))],
        name="grad_join_halves",
    )(f)


def _all_reduce_small(name, v):
    rows = v.shape[0]

    def body(v_ref, out_ref, slots, send_sems, recv_sems):
        x, y, c = _me()
        me = 4 * x + 2 * y + c
        slots[me] = v_ref[...]
        sends = []
        for r in range(1, 8):
            to = (x ^ (r >> 2), y ^ ((r >> 1) & 1), c ^ (r & 1))
            cp = pltpu.make_async_remote_copy(src_ref=v_ref, dst_ref=slots.at[me], send_sem=send_sems.at[r - 1],
                                              recv_sem=recv_sems.at[r - 1], device_id=to, device_id_type=MESH)
            cp.start()
            sends.append(cp)
        for cp in sends:
            cp.wait()
        total = slots[0]
        for d in range(1, 8):
            total = total + slots[d]
        out_ref[...] = total

    vmem = pl.BlockSpec(memory_space=pltpu.VMEM)
    return pl.pallas_call(
        body, out_shape=jax.ShapeDtypeStruct(v.shape, F32), in_specs=[vmem], out_specs=vmem,
        scratch_shapes=[pltpu.VMEM((8, rows, ROW), F32), pltpu.SemaphoreType.DMA((7,)), pltpu.SemaphoreType.DMA((7,))],
        name=name,
    )(v)


def _adam_fn(w, g, m, v):
    m2 = ADAM_B1 * m + (1.0 - ADAM_B1) * g
    v2 = ADAM_B2 * v + (1.0 - ADAM_B2) * (g * g)
    m_hat = m2 / (1.0 - ADAM_B1 ** ADAM_STEP)
    v_hat = v2 / (1.0 - ADAM_B2 ** ADAM_STEP)
    return -ADAM_LR * (m_hat / (jnp.sqrt(v_hat) + ADAM_EPS) + ADAM_WD * w), m2, v2


def _full_shape(name, shard_shape):
    ax = _shard_axis(name)
    return tuple(n * N_CHIPS if k == ax else n for k, n in enumerate(shard_shape))


def _chip_major(name, full):
    if _shard_axis(name) == 0:
        return full.reshape(N_CHIPS, -1, full.shape[1])
    n = full.shape[1] // N_CHIPS
    return jnp.stack([full[:, j * n:(j + 1) * n] for j in range(N_CHIPS)])


def _from_chip_major(name, shards):
    if _shard_axis(name) == 0:
        return shards.reshape(-1, shards.shape[2])
    return jnp.concatenate([shards[j] for j in range(N_CHIPS)], axis=1)


def _row_tile(rows, cap=512):
    return max(t for t in range(8, min(rows, cap) + 1, 8) if rows % t == 0)


def _step(a):
    x_i, y_i, c_i = _me()
    chip = 2 * x_i + y_i
    big_shapes = [a[n].shape for n in BIG]

    packed = _pack([a[n].astype(BF16) for n in BIG], BIG_ROWS)
    shards = _unpack(_gather_shards(packed), big_shapes, lead=(N_CHIPS,))
    big = {n: _from_chip_major(n, sh) for n, sh in zip(BIG, shards)}

    small = {n: a[n] for n in SMALL}
    convs = [n for n in SMALL if n.endswith("dn_conv_w")]
    placed = []
    for n in convs:
        full = jnp.zeros(_full_shape_conv(a[n].shape), F32)
        placed.append(lax.dynamic_update_slice(full, a[n], (0, chip * a[n].shape[1])))
    conv_sum = _all_reduce_small("gather_conv", _pack(placed, 8))
    for n, full in zip(convs, _unpack(conv_sum, [p.shape for p in placed])):
        small[n] = full * 0.5

    part, dx, grads = local_step(a["x"][0], a["loss_target"][0], big, small)
    loss = lax.psum(0.5 * jnp.sum(part) / D_MODEL, ("x", "y", "c"))

    g_all = _pack([_chip_major(n, grads[n]) for n in BIG], BIG_ROWS, lead=(N_CHIPS,))
    s32, s16 = _pair_sum(g_all, _swap_halves(g_all), c_i)
    g_big = _join_halves(_chip_sum(s32, _scatter_chunks(s16), chip, c_i))
    g_big = dict(zip(BIG, _unpack(g_big, big_shapes)))

    g_small_full = _all_reduce_small("reduce_small", _pack([grads[n] for n in SMALL], 8))
    g_small = dict(zip(SMALL, _unpack(g_small_full, [grads[n].shape for n in SMALL])))
    for n in convs:
        g_small[n] = lax.dynamic_slice_in_dim(g_small[n], chip * a[n].shape[1], a[n].shape[1], axis=1)

    outs = {}
    for n in BIG:
        d, m2, v2 = _ew("adam_" + n, _adam_fn, [a[n], g_big[n], a["m_" + n], a["v_" + n]], [],
                        [(a[n].shape[1], F32)] * 3, tm=_row_tile(a[n].shape[0]))
        outs.update({"grad_" + n: g_big[n], "delta_" + n: d, "new_m_" + n: m2, "new_v_" + n: v2})
    pk = lambda prefix: _pack([a[prefix + n] for n in SMALL], 8)
    gs_packed = _pack([g_small[n] for n in SMALL], 8)
    small_bufs = (gs_packed,) + tuple(_ew("adam_small", _adam_fn, [pk(""), gs_packed, pk("m_"), pk("v_")], [], [(ROW, F32)] * 3,
                                          tm=gs_packed.shape[0]))
    small_shapes = [a[n].shape for n in SMALL]
    for key, buf in zip(("grad_", "delta_", "new_m_", "new_v_"), small_bufs):
        outs.update({key + n: val for n, val in zip(SMALL, _unpack(buf, small_shapes))})
    result = [loss, dx[None]]
    for key in ("grad_", "delta_", "new_m_", "new_v_"):
        result += [outs[key + n] for n in WEIGHTS]
    return tuple(result)


def _full_shape_conv(shard_shape):
    return (shard_shape[0], shard_shape[1] * N_CHIPS)


def kernel(x, l0_mix_norm, l0_dn_w_in, l0_dn_conv_w, l0_dn_a_log, l0_dn_dt_bias, l0_dn_out_norm, l0_dn_w_out, l0_ffn_norm, l0_ffn_w_gate_up, l0_ffn_w_down, l1_mix_norm, l1_sb_w_qkv, l1_sb_q_norm, l1_sb_k_norm, l1_sb_w_out, l1_ffn_norm, l1_ffn_w_gate_up, l1_ffn_w_down, l2_mix_norm, l2_mla_w_down, l2_mla_q_a_norm, l2_mla_kv_a_norm, l2_mla_w_uq, l2_mla_w_ukv, l2_mla_q_nope_norm, l2_mla_q_rope_norm, l2_mla_k_nope_norm, l2_mla_k_rope_norm, l2_mla_w_out, l2_ffn_norm, l2_ffn_w_gate_up, l2_ffn_w_down, l3_mix_norm, l3_dn_w_in, l3_dn_conv_w, l3_dn_a_log, l3_dn_dt_bias, l3_dn_out_norm, l3_dn_w_out, l3_ffn_norm, l3_ffn_w_gate_up, l3_ffn_w_down, loss_target, m_l0_mix_norm, m_l0_dn_w_in, m_l0_dn_conv_w, m_l0_dn_a_log, m_l0_dn_dt_bias, m_l0_dn_out_norm, m_l0_dn_w_out, m_l0_ffn_norm, m_l0_ffn_w_gate_up, m_l0_ffn_w_down, m_l1_mix_norm, m_l1_sb_w_qkv, m_l1_sb_q_norm, m_l1_sb_k_norm, m_l1_sb_w_out, m_l1_ffn_norm, m_l1_ffn_w_gate_up, m_l1_ffn_w_down, m_l2_mix_norm, m_l2_mla_w_down, m_l2_mla_q_a_norm, m_l2_mla_kv_a_norm, m_l2_mla_w_uq, m_l2_mla_w_ukv, m_l2_mla_q_nope_norm, m_l2_mla_q_rope_norm, m_l2_mla_k_nope_norm, m_l2_mla_k_rope_norm, m_l2_mla_w_out, m_l2_ffn_norm, m_l2_ffn_w_gate_up, m_l2_ffn_w_down, m_l3_mix_norm, m_l3_dn_w_in, m_l3_dn_conv_w, m_l3_dn_a_log, m_l3_dn_dt_bias, m_l3_dn_out_norm, m_l3_dn_w_out, m_l3_ffn_norm, m_l3_ffn_w_gate_up, m_l3_ffn_w_down, v_l0_mix_norm, v_l0_dn_w_in, v_l0_dn_conv_w, v_l0_dn_a_log, v_l0_dn_dt_bias, v_l0_dn_out_norm, v_l0_dn_w_out, v_l0_ffn_norm, v_l0_ffn_w_gate_up, v_l0_ffn_w_down, v_l1_mix_norm, v_l1_sb_w_qkv, v_l1_sb_q_norm, v_l1_sb_k_norm, v_l1_sb_w_out, v_l1_ffn_norm, v_l1_ffn_w_gate_up, v_l1_ffn_w_down, v_l2_mix_norm, v_l2_mla_w_down, v_l2_mla_q_a_norm, v_l2_mla_kv_a_norm, v_l2_mla_w_uq, v_l2_mla_w_ukv, v_l2_mla_q_nope_norm, v_l2_mla_q_rope_norm, v_l2_mla_k_nope_norm, v_l2_mla_k_rope_norm, v_l2_mla_w_out, v_l2_ffn_norm, v_l2_ffn_w_gate_up, v_l2_ffn_w_down, v_l3_mix_norm, v_l3_dn_w_in, v_l3_dn_conv_w, v_l3_dn_a_log, v_l3_dn_dt_bias, v_l3_dn_out_norm, v_l3_dn_w_out, v_l3_ffn_norm, v_l3_ffn_w_gate_up, v_l3_ffn_w_down):
    return _step(dict(locals()))
```

```python
import functools
import math

import jax
import jax.numpy as jnp
from jax import lax
from jax.experimental import pallas as pl
from jax.experimental.pallas import tpu as pltpu

F32, BF16 = jnp.float32, jnp.bfloat16
MESH = pl.DeviceIdType.MESH

D_MODEL = 1024
N_HEADS = 8
HEAD = 128
FFN_HIDDEN = 2816
DN_CHUNK = 64
NORM_EPS = 1e-6
MLA_ROPE = 64
MLA_QK = 192
ROPE_THETA = 10000.0
ADAM_LR, ADAM_B1, ADAM_B2, ADAM_EPS, ADAM_WD, ADAM_STEP = 0.001, 0.9, 0.999, 1e-08, 0.01, 10
N_CHIPS = 4
LANES = 128
VMEM_LIMIT = 56 * 2 ** 20


def _params(n_grid):
    return pltpu.CompilerParams(dimension_semantics=("arbitrary",) * n_grid, vmem_limit_bytes=VMEM_LIMIT)


_MIXERS = (
    ("dn_w_in", "dn_conv_w", "dn_a_log", "dn_dt_bias", "dn_out_norm", "dn_w_out"),
    ("sb_w_qkv", "sb_q_norm", "sb_k_norm", "sb_w_out"),
    ("mla_w_down", "mla_q_a_norm", "mla_kv_a_norm", "mla_w_uq", "mla_w_ukv", "mla_q_nope_norm",
     "mla_q_rope_norm", "mla_k_nope_norm", "mla_k_rope_norm", "mla_w_out"),
)
DEPTH = 4


def _layer_names(i):
    p = "l%d_" % i
    return [p + "mix_norm"] + [p + n for n in _MIXERS[i % 3]] + [p + "ffn_norm", p + "ffn_w_gate_up", p + "ffn_w_down"]


WEIGHTS = [n for i in range(DEPTH) for n in _layer_names(i)]
_ROW_SHARDED = ("w_out", "ffn_w_down", "mla_w_down")
_COL_SHARDED = ("dn_w_in", "sb_w_qkv", "mla_w_uq", "mla_w_ukv", "ffn_w_gate_up")


def _shard_axis(name):
    if name.endswith(_ROW_SHARDED):
        return 0
    if name.endswith(_COL_SHARDED):
        return 1
    return None


BIG = [n for n in WEIGHTS if _shard_axis(n) is not None]
SMALL = [n for n in WEIGHTS if _shard_axis(n) is None]


_DN = {"nn": (((1,), (0,)), ((), ())), "nt": (((1,), (1,)), ((), ())), "tn": (((0,), (0,)), ((), ()))}
_DN_BATCHED = {"nn": (((2,), (1,)), ((0,), (0,))), "nt": (((2,), (2,)), ((0,), (0,))), "tn": (((1,), (1,)), ((0,), (0,)))}


def _dims(a, kind):
    return _DN_BATCHED[kind] if a.ndim == 3 else _DN[kind]


def _dg(a, b, kind):
    return lax.dot_general(a.astype(BF16), b.astype(BF16), _dims(a, kind), preferred_element_type=F32)


@functools.partial(jax.custom_vjp, nondiff_argnums=(2,))
def bdot(a, b, kind):
    return _dg(a, b, kind)


def _bdot_fwd(a, b, kind):
    return _dg(a, b, kind), (a, b)


def _bdot_bwd(kind, res, ct):
    a, b = res
    if kind == "nn":
        return _dg(ct, b, "nt"), _dg(a, ct, "tn")
    if kind == "nt":
        return _dg(ct, b, "nn"), _dg(ct, a, "tn")
    return _dg(b, ct, "nt"), _dg(a, ct, "nn")


bdot.defvjp(_bdot_fwd, _bdot_bwd)


def _split(a, terms):
    out = []
    for _ in range(terms):
        hi = a.astype(BF16)
        out.append(hi)
        a = a - hi.astype(F32)
    return out


def _xdot(a, b, kind, exact, terms=3):
    if exact == 0:
        return sum(lax.dot_general(a, p, _dims(a, kind), preferred_element_type=F32) for p in _split(b, terms))
    return sum(lax.dot_general(p, b, _dims(a, kind), preferred_element_type=F32) for p in _split(a, terms))


def _tri(n, rel):
    r = lax.broadcasted_iota(jnp.int32, (n, n), 0)
    c = lax.broadcasted_iota(jnp.int32, (n, n), 1)
    return {"le": c <= r, "lt": c < r, "ge": c >= r, "gt": c > r}[rel]


def _running(g):
    n = g.shape[-2]
    return jnp.broadcast_to(_tri(n, "le").astype(BF16), g.shape[:-2] + (n, n))


@jax.custom_vjp
def cumsum_rows(g):
    return _xdot(_running(g), g, "nn", 0)


def _cumsum_fwd(g):
    return cumsum_rows(g), None


def _cumsum_bwd(_, ct):
    return (_xdot(_running(ct), ct, "tn", 0),)


cumsum_rows.defvjp(_cumsum_fwd, _cumsum_bwd)


def _dot3(a, b, kind):
    (ah, al), (bh, bl) = _split(a, 2), _split(b, 2)
    dot = lambda p, q: lax.dot_general(p, q, _dims(a, kind), preferred_element_type=F32)
    return dot(ah, bh) + (dot(ah, bl) + dot(al, bh))


@functools.partial(jax.custom_vjp, nondiff_argnums=(2,))
def _hdot3(a, b, kind):
    return _dot3(a, b, kind)


def _hdot3_fwd(a, b, kind):
    return _dot3(a, b, kind), (a, b)


def _hdot3_bwd(kind, res, ct):
    a, b = res
    if kind == "nn":
        return _dot3(ct, b, "nt"), _dot3(a, ct, "tn")
    if kind == "nt":
        return _dot3(ct, b, "nn"), _dot3(ct, a, "tn")
    return _dot3(b, ct, "nt"), _dot3(a, ct, "nn")


_hdot3.defvjp(_hdot3_fwd, _hdot3_bwd)


def _hdot(a, b):
    return _hdot3(a, b, "nn")


def _unit_lower_inverse(lower):
    n = lower.shape[-1]
    eye = (lax.broadcasted_iota(jnp.int32, (n, n), 0) == lax.broadcasted_iota(jnp.int32, (n, n), 1)).astype(F32)
    m = -lower
    p = eye + m
    for _ in range(int(math.log2(n)) - 1):
        m = _hdot(m, m)
        p = p + _hdot(p, m)
    return p


def _rms(x, g, n=None):
    n = x.shape[-1] if n is None else n
    return x * lax.rsqrt(jnp.sum(x * x, axis=-1, keepdims=True) * (1.0 / n) + NORM_EPS) * g


def _l2(x):
    return x * lax.rsqrt(jnp.sum(x * x, axis=-1, keepdims=True) + NORM_EPS)


def _silu(x):
    return x * jax.nn.sigmoid(x)


def _logsig(z):
    return jnp.minimum(z, 0.0) - jnp.log1p(jnp.exp(-jnp.abs(z)))


@jax.custom_vjp
def _rope(x, cos, sin_lo, sin_hi):
    return x * cos + pltpu.roll(x, 96, 1) * sin_lo + pltpu.roll(x, 32, 1) * sin_hi


def _rope_fwd(x, cos, sin_lo, sin_hi):
    return _rope(x, cos, sin_lo, sin_hi), (cos, sin_lo, sin_hi)


def _rope_bwd(res, ct):
    cos, sin_lo, sin_hi = res
    dx = ct * cos + pltpu.roll(ct * sin_lo, 32, 1) + pltpu.roll(ct * sin_hi, 96, 1)
    return dx, jnp.zeros_like(cos), jnp.zeros_like(sin_lo), jnp.zeros_like(sin_hi)


_rope.defvjp(_rope_fwd, _rope_bwd)


def _tile(n, prefs=(512, 384, 256, 128)):
    for t in prefs:
        if n % t == 0:
            return t
    return n


MM_OUT_TILES = (1024, 1408, 512, 384, 256, 128)
MM_K_TILES = (1024, 512, 384, 256, 128)


def _mm(name, a, b, kind, out_dtype=F32, add=None):
    if kind == "tn":
        (kdim, m), n = a.shape, b.shape[1]
    else:
        (m, kdim), n = a.shape, (b.shape[0] if kind == "nt" else b.shape[1])
    tm, tn, tk = _tile(m, MM_OUT_TILES), _tile(n, MM_OUT_TILES), _tile(kdim, MM_K_TILES)
    nk = kdim // tk
    a_spec = pl.BlockSpec((tk, tm), lambda i, j, k: (k, i)) if kind == "tn" else pl.BlockSpec((tm, tk), lambda i, j, k: (i, k))
    b_spec = pl.BlockSpec((tn, tk), lambda i, j, k: (j, k)) if kind == "nt" else pl.BlockSpec((tk, tn), lambda i, j, k: (k, j))
    o_spec = pl.BlockSpec((tm, tn), lambda i, j, k: (i, j))
    has_add = add is not None

    def body(*refs):
        a_ref, b_ref = refs[0], refs[1]
        o_ref, acc = refs[-2], refs[-1]
        k = pl.program_id(2)

        @pl.when(k == 0)
        def _():
            acc[...] = jnp.zeros_like(acc)

        acc[...] += _dg(a_ref[...], b_ref[...], kind)

        @pl.when(k == nk - 1)
        def _():
            r = acc[...]
            if has_add:
                r = r + refs[2][...]
            o_ref[...] = r.astype(o_ref.dtype)

    return pl.pallas_call(
        body, grid=(m // tm, n // tn, nk),
        in_specs=[a_spec, b_spec] + ([o_spec] if has_add else []),
        out_specs=o_spec, out_shape=jax.ShapeDtypeStruct((m, n), out_dtype),
        scratch_shapes=[pltpu.VMEM((tm, tn), F32)], name=name, compiler_params=_params(3),
    )(*([a, b] + ([add] if has_add else [])))


class _V:
    def __init__(self, arr, w=None, base=0, diff=True):
        self.arr, self.base, self.diff = arr, base, diff
        self.w = arr.shape[1] if w is None else w

    def spec(self, tm):
        return pl.BlockSpec((tm, self.w), lambda i, b=self.base: (i, b))


def _as_views(ins):
    return [v if isinstance(v, _V) else _V(v) for v in ins]


def _tup(r):
    return tuple(r) if isinstance(r, (tuple, list)) else (r,)


def _ew(name, fn, ins, smalls, outs, tm=256):
    ins = _as_views(ins)
    t = ins[0].arr.shape[0]
    tm = min(tm, t)
    n_in = len(ins) + len(smalls)

    def body(*refs):
        res = _tup(fn(*[r[...] for r in refs[:n_in]]))
        for r, o in zip(refs[n_in:], res):
            r[...] = o.astype(r.dtype)

    return pl.pallas_call(
        body, grid=(t // tm,),
        in_specs=[v.spec(tm) for v in ins] + [pl.BlockSpec(s.shape, lambda i: (0, 0)) for s in smalls],
        out_specs=[pl.BlockSpec((tm, w), lambda i: (i, 0)) for w, _ in outs],
        out_shape=[jax.ShapeDtypeStruct((t, w), dt) for w, dt in outs],
        name=name, compiler_params=_params(1),
    )(*[v.arr for v in ins], *smalls)


def _ew_bwd(name, fn, ins, smalls, cts, tm=256, add=None, ct_dtypes=None):
    ins = _as_views(ins)
    t = ins[0].arr.shape[0]
    tm = min(tm, t)
    n_in, n_sm = len(ins), len(smalls)
    diff = [k for k, v in enumerate(ins) if v.diff]
    ct_dtypes = [F32] * len(diff) if ct_dtypes is None else ct_dtypes
    ct_arrs = [c for c in cts if c is not None]
    has_add = add is not None

    def body(*refs):
        vals = [r[...] for r in refs[:n_in]]
        svals = [r[...] for r in refs[n_in:n_in + n_sm]]
        p = n_in + n_sm
        ct_refs = list(refs[p:p + len(ct_arrs)])
        p += len(ct_arrs)
        add_ref = refs[p] if has_add else None
        p += int(has_add)
        din_refs = refs[p:p + len(diff)]
        dsm_refs = refs[p + len(diff):]

        def f(dv, sv):
            full = list(vals)
            for k, d in zip(diff, dv):
                full[k] = d
            return _tup(fn(*full, *sv))

        res, vjp = jax.vjp(f, [vals[k] for k in diff], svals)
        ctv = tuple(jnp.zeros_like(o) if c is None else ct_refs.pop(0)[...].astype(o.dtype) for c, o in zip(cts, res))
        dv, dsv = vjp(ctv)
        for n, (r, d) in enumerate(zip(din_refs, dv)):
            if n == 0 and has_add:
                d = d + add_ref[...]
            r[...] = d.astype(r.dtype)

        @pl.when(pl.program_id(0) == 0)
        def _():
            for r in dsm_refs:
                r[...] = jnp.zeros_like(r)

        for r, d in zip(dsm_refs, dsv):
            r[...] += d

    row = lambda w: pl.BlockSpec((tm, w), lambda i: (i, 0))
    small_specs = [pl.BlockSpec(s.shape, lambda i: (0, 0)) for s in smalls]
    out = pl.pallas_call(
        body, grid=(t // tm,),
        in_specs=[v.spec(tm) for v in ins] + small_specs + [row(c.shape[1]) for c in ct_arrs]
        + ([row(add.shape[1])] if has_add else []),
        out_specs=[row(ins[k].w) for k in diff] + small_specs,
        out_shape=[jax.ShapeDtypeStruct((t, ins[k].w), dt) for k, dt in zip(diff, ct_dtypes)]
        + [jax.ShapeDtypeStruct(s.shape, F32) for s in smalls],
        name=name, compiler_params=_params(1),
    )(*[v.arr for v in ins], *smalls, *ct_arrs, *([add] if has_add else []))
    return out[:len(diff)], out[len(diff):]


BQ = 256
HPB = 2
SUM_TERMS = 2


def _cat(parts):
    return parts[0] if len(parts) == 1 else jnp.concatenate(parts, axis=1)


def _head_view(ref, hh):
    return ref.at[:, hh * HEAD:(hh + 1) * HEAD]


def _attn_specs(qs, ks, t):
    q_specs = [pl.BlockSpec((BQ, HPB * HEAD), lambda h, i: (i, h)) for _ in qs]
    per_head = pl.BlockSpec((t, HPB * HEAD), lambda h, i: (0, h))
    k_specs = [pl.BlockSpec((t, HEAD), lambda h, i: (0, 0)) if sh else per_head for _, sh in ks]
    return q_specs, k_specs, per_head


def _causal_sweep(i, pair, init, diagonal_first):
    order = (lambda s: i - 1 - s) if diagonal_first else (lambda s: s)
    two = lambda s, c: pair(order(2 * s + 1), pair(order(2 * s), c, False), False)
    one = lambda s, c: pair(order(s), c, False)
    carry = pair(i, init, True) if diagonal_first else init
    carry = lax.fori_loop(0, lax.shift_right_logical(i, 1), two, carry)
    carry = lax.fori_loop(i - (i & 1), i, one, carry)
    return carry if diagonal_first else pair(i, carry, True)


def _attn_fwd(name, mode, qs, ks, v):
    t = qs[0].shape[0]
    nq, n = t // BQ, len(qs)
    q_specs, k_specs, per_head = _attn_specs(qs, ks, t)
    shared = [sh for _, sh in ks]

    def body(*refs):
        q_refs, k_refs, v_ref = refs[:n], refs[n:2 * n], refs[2 * n]
        o_ref, st_ref = refs[2 * n + 1], refs[2 * n + 2]
        i = pl.program_id(1)
        row = lax.broadcasted_iota(jnp.int32, (BQ, BQ), 0)
        col = lax.broadcasted_iota(jnp.int32, (BQ, BQ), 1)
        after = _tri(BQ, "lt").astype(BF16)

        def head(hh):
            q = _cat([_head_view(r, hh)[...] for r in q_refs])
            k_h = [kr if sh else _head_view(kr, hh) for kr, sh in zip(k_refs, shared)]
            v_h = _head_view(v_ref, hh)

            def pair(j, carry, masked):
                off = pl.multiple_of(j * BQ, BQ)
                z = _dg(q, _cat([kr[pl.ds(off, BQ), :] for kr in k_h]), "nt")
                vj = v_h[pl.ds(off, BQ), :]
                if mode == "sb":
                    acc, run = carry
                    lsz = _logsig(z)
                    stay = lsz - z
                    if masked:
                        stay = jnp.where(col < row, stay, 0.0)
                    a = jnp.exp(lsz + (run + _xdot(stay, after, "nn", 1, SUM_TERMS)))
                    if masked:
                        a = jnp.where(col < row, a, 0.0)
                    return acc + _dg(a, vj, "nn"), run + jnp.sum(stay, axis=1, keepdims=True)
                m, l, acc = carry
                if masked:
                    z = jnp.where(col <= row, z, -1e30)
                m2 = jnp.maximum(m, jnp.max(z, axis=1, keepdims=True))
                p = jnp.exp(z - m2)
                alpha = jnp.exp(m - m2)
                return m2, alpha * l + jnp.sum(p, axis=1, keepdims=True), alpha * acc + _dg(p, vj, "nn")

            def finish(carry):
                if mode == "sb":
                    acc, run = carry
                    _head_view(o_ref, hh)[...] = acc
                    _head_view(st_ref, hh)[...] = jnp.broadcast_to(run, (BQ, HEAD))
                else:
                    m, l, acc = carry
                    _head_view(o_ref, hh)[...] = acc / l
                    _head_view(st_ref, hh)[...] = jnp.broadcast_to(m + jnp.log(l), (BQ, HEAD))

            zero = jnp.zeros((BQ, 1), F32)
            acc0 = jnp.zeros((BQ, HEAD), F32)
            init = (acc0, zero) if mode == "sb" else (jnp.full((BQ, 1), -1e30, F32), zero, acc0)
            return pair, init, finish

        heads = [head(hh) for hh in range(HPB)]
        both = lambda j, carry, masked: tuple(h[0](j, c, masked) for h, c in zip(heads, carry))
        final = _causal_sweep(i, both, tuple(h[1] for h in heads), diagonal_first=(mode == "sb"))
        for h, c in zip(heads, final):
            h[2](c)

    blk = pl.BlockSpec((BQ, HPB * HEAD), lambda h, i: (i, h))
    return pl.pallas_call(
        body, grid=(N_HEADS // HPB, nq), in_specs=q_specs + k_specs + [per_head], out_specs=[blk, blk],
        out_shape=[jax.ShapeDtypeStruct((t, N_HEADS * HEAD), F32)] * 2, name=name, compiler_params=_params(2),
    )(*qs, *[k for k, _ in ks], v)


def _attn_bwd(name, mode, qs, ks, v, o, stat, do):
    t = qs[0].shape[0]
    nq, n = t // BQ, len(qs)
    q_specs, k_specs, per_head = _attn_specs(qs, ks, t)
    shared = [sh for _, sh in ks]

    def body(*refs):
        q_refs, k_refs, v_ref = refs[:n], refs[n:2 * n], refs[2 * n]
        o_ref, st_ref, do_ref = refs[2 * n + 1:2 * n + 4]
        dq_refs = refs[2 * n + 4:3 * n + 4]
        dk_refs = refs[3 * n + 4:4 * n + 4]
        dv_ref = refs[4 * n + 4]
        g, i = pl.program_id(0), pl.program_id(1)

        @pl.when(i == 0)
        def _():
            dv_ref[...] = jnp.zeros_like(dv_ref)
            for r, sh in zip(dk_refs, shared):
                if not sh:
                    r[...] = jnp.zeros_like(r)

        for r, sh in zip(dk_refs, shared):
            if sh:
                @pl.when((i == 0) & (g == 0))
                def _(r=r):
                    r[...] = jnp.zeros_like(r)

        row = lax.broadcasted_iota(jnp.int32, (BQ, BQ), 0)
        col = lax.broadcasted_iota(jnp.int32, (BQ, BQ), 1)
        upto = _tri(BQ, "ge").astype(BF16)
        before = _tri(BQ, "gt").astype(BF16)

        def head(hh):
            q = _cat([_head_view(r, hh)[...] for r in q_refs])
            k_h = [kr if sh else _head_view(kr, hh) for kr, sh in zip(k_refs, shared)]
            dk_h = [r if sh else _head_view(r, hh) for r, sh in zip(dk_refs, shared)]
            v_h, dv_h = _head_view(v_ref, hh), _head_view(dv_ref, hh)
            do_t = _head_view(do_ref, hh)[...]
            st = _head_view(st_ref, hh)[:, :1]
            if mode == "softmax":
                dsum = jnp.sum(do_t * _head_view(o_ref, hh)[...], axis=1, keepdims=True)

            def pair(j, carry, masked):
                off = pl.multiple_of(j * BQ, BQ)
                kj = _cat([kr[pl.ds(off, BQ), :] for kr in k_h])
                z = _dg(q, kj, "nt")
                da = _dg(do_t, v_h[pl.ds(off, BQ), :], "nt")
                if mode == "sb":
                    dq, pre, gpre = carry
                    lsz = _logsig(z)
                    stay = lsz - z
                    if masked:
                        stay = jnp.where(col < row, stay, 0.0)
                    a = jnp.exp(lsz + (st - (pre + _xdot(stay, upto, "nn", 1, SUM_TERMS))))
                    if masked:
                        a = jnp.where(col < row, a, 0.0)
                    gr = a * da
                    sig = jnp.exp(lsz)
                    dz = gr * (1.0 - sig) - sig * (gpre + _xdot(gr, before, "nn", 1, SUM_TERMS))
                    if masked:
                        dz = jnp.where(col < row, dz, 0.0)
                    tail = (pre + jnp.sum(stay, axis=1, keepdims=True), gpre + jnp.sum(gr, axis=1, keepdims=True))
                else:
                    dq = carry[0]
                    a = jnp.exp(z - st)
                    if masked:
                        a = jnp.where(col <= row, a, 0.0)
                    dz = a * (da - dsum)
                    tail = ()
                dk = _dg(dz, q, "tn")
                for p, r in enumerate(dk_h):
                    r[pl.ds(off, BQ), :] += dk[:, p * HEAD:(p + 1) * HEAD]
                dv_h[pl.ds(off, BQ), :] += _dg(a, do_t, "tn")
                return (dq + _dg(dz, kj, "nn"),) + tail

            def finish(carry):
                for p, r in enumerate(dq_refs):
                    _head_view(r, hh)[...] = carry[0][:, p * HEAD:(p + 1) * HEAD]

            zero = jnp.zeros((BQ, 1), F32)
            init = (jnp.zeros((BQ, n * HEAD), F32),) + ((zero, zero) if mode == "sb" else ())
            return pair, init, finish

        heads = [head(hh) for hh in range(HPB)]
        both = lambda j, carry, masked: tuple(h[0](j, c, masked) for h, c in zip(heads, carry))
        final = _causal_sweep(i, both, tuple(h[1] for h in heads), diagonal_first=False)
        for h, c in zip(heads, final):
            h[2](c)

    blk = pl.BlockSpec((BQ, HPB * HEAD), lambda h, i: (i, h))
    dk_specs = [pl.BlockSpec((t, HEAD), lambda h, i: (0, 0)) if sh else per_head for sh in shared]
    wide = jax.ShapeDtypeStruct((t, N_HEADS * HEAD), F32)
    out = pl.pallas_call(
        body, grid=(N_HEADS // HPB, nq), in_specs=q_specs + k_specs + [per_head, blk, blk, blk],
        out_specs=[blk] * n + dk_specs + [per_head],
        out_shape=[wide] * n + [jax.ShapeDtypeStruct((t, HEAD), F32) if sh else wide for sh in shared] + [wide],
        name=name, compiler_params=_params(2),
    )(*qs, *[k for k, _ in ks], v, o, stat, do)
    return out[:n], out[n:2 * n], out[2 * n]


def _dn_chunk(q, k, v, g, beta, state):
    c = q.shape[-2]
    gc = cumsum_rows(g)
    gcc = gc[..., :c]
    diff = gcc - jnp.swapaxes(gcc, -1, -2)
    causal, strict = _tri(c, "le"), _tri(c, "lt")
    decay = jnp.where(causal, jnp.exp(jnp.where(causal, diff, 0.0)), 0.0)
    kb = k * beta
    lower = jnp.where(strict, bdot(kb, k, "nt") * decay, 0.0)
    tinv = _unit_lower_inverse(lower)
    eg = jnp.exp(gc)
    u = _hdot(tinv, v * beta)
    w = _hdot(tinv, kb * eg)
    attn = bdot(q, k, "nt") * decay
    glast = gc[..., c - 1:c, :]
    v_new = u - bdot(w, state, "nn")
    o = bdot(q * eg, state, "nn") + bdot(attn, v_new, "nn")
    new_state = state * jnp.exp(glast) + bdot(k * jnp.exp(glast - gc), v_new, "tn")
    return o, new_state


def _stack_heads(ref):
    return jnp.stack([ref[:, h * HEAD:(h + 1) * HEAD] for h in range(N_HEADS)])


def _store_heads(ref, val):
    for h in range(N_HEADS):
        ref[:, h * HEAD:(h + 1) * HEAD] = val[h]


def _dn_fwd(name, q, k, v, g, beta):
    t = q.shape[0]
    nc = t // DN_CHUNK
    wide = N_HEADS * HEAD
    blk = pl.BlockSpec((DN_CHUNK, wide), lambda n: (n, 0))
    st_spec = pl.BlockSpec((N_HEADS, None, HEAD, HEAD), lambda n: (0, n, 0, 0))

    def body(q_ref, k_ref, v_ref, g_ref, b_ref, o_ref, st_ref, state):
        @pl.when(pl.program_id(0) == 0)
        def _():
            state[...] = jnp.zeros_like(state)

        s_in = state[...]
        st_ref[...] = s_in
        o, s_out = _dn_chunk(*[_stack_heads(r) for r in (q_ref, k_ref, v_ref, g_ref, b_ref)], s_in)
        _store_heads(o_ref, o)
        state[...] = s_out

    return pl.pallas_call(
        body, grid=(nc,), in_specs=[blk] * 5, out_specs=[blk, st_spec],
        out_shape=[jax.ShapeDtypeStruct((t, wide), F32), jax.ShapeDtypeStruct((N_HEADS, nc, HEAD, HEAD), F32)],
        scratch_shapes=[pltpu.VMEM((N_HEADS, HEAD, HEAD), F32)], name=name, compiler_params=_params(1),
    )(q, k, v, g, beta)


def _dn_bwd(name, q, k, v, g, beta, states, do):
    t = q.shape[0]
    nc = t // DN_CHUNK
    wide = N_HEADS * HEAD
    blk = pl.BlockSpec((DN_CHUNK, wide), lambda n: (nc - 1 - n, 0))
    st_spec = pl.BlockSpec((N_HEADS, None, HEAD, HEAD), lambda n: (0, nc - 1 - n, 0, 0))

    def body(q_ref, k_ref, v_ref, g_ref, b_ref, st_ref, do_ref, dq_ref, dk_ref, dv_ref, dg_ref, db_ref, dstate):
        @pl.when(pl.program_id(0) == 0)
        def _():
            dstate[...] = jnp.zeros_like(dstate)

        _, vjp = jax.vjp(_dn_chunk, *[_stack_heads(r) for r in (q_ref, k_ref, v_ref, g_ref, b_ref)], st_ref[...])
        cts = vjp((_stack_heads(do_ref), dstate[...]))
        for r, d in zip((dq_ref, dk_ref, dv_ref, dg_ref, db_ref), cts[:5]):
            _store_heads(r, d)
        dstate[...] = cts[5]

    shape = jax.ShapeDtypeStruct((t, wide), F32)
    return pl.pallas_call(
        body, grid=(nc,), in_specs=[blk] * 5 + [st_spec, blk], out_specs=[blk] * 5, out_shape=[shape] * 5,
        scratch_shapes=[pltpu.VMEM((N_HEADS, HEAD, HEAD), F32)], name=name, compiler_params=_params(1),
    )(q, k, v, g, beta, states, do)


CONV_W = 1024
HALO = 8


def _shift_down(cur, prev, s):
    sh = pltpu.roll(cur, s, 0)
    ph = pltpu.roll(prev, s, 0)
    r = lax.broadcasted_iota(jnp.int32, (HALO, cur.shape[1]), 0)
    return jnp.concatenate([jnp.where(r < s, ph, sh[:HALO]), sh[HALO:]], axis=0)


def _shift_up(cur, nxt, s):
    tm = cur.shape[0]
    sh = pltpu.roll(cur, tm - s, 0)
    nh = pltpu.roll(nxt, HALO - s, 0)
    r = lax.broadcasted_iota(jnp.int32, (HALO, cur.shape[1]), 0)
    return jnp.concatenate([sh[:tm - HALO], jnp.where(r >= HALO - s, nh, sh[tm - HALO:])], axis=0)


def _conv_fwd(name, proj, w, tm=256):
    t = proj.shape[0]
    tm = min(tm, t)
    width = w.shape[1]
    per = tm // HALO

    def body(cur_ref, prev_ref, w_ref, y_ref):
        cur = cur_ref[...]
        prev = jnp.where(pl.program_id(0) > 0, prev_ref[...], 0.0)
        y = cur * w_ref[3:4, :]
        for s in (1, 2, 3):
            y = y + _shift_down(cur, prev, s) * w_ref[3 - s:4 - s, :]
        y_ref[...] = y

    return pl.pallas_call(
        body, grid=(t // tm, width // CONV_W),
        in_specs=[pl.BlockSpec((tm, CONV_W), lambda i, c: (i, c)),
                  pl.BlockSpec((HALO, CONV_W), lambda i, c: (jnp.maximum(i * per - 1, 0), c)),
                  pl.BlockSpec((HALO, CONV_W), lambda i, c: (0, c))],
        out_specs=pl.BlockSpec((tm, CONV_W), lambda i, c: (i, c)),
        out_shape=jax.ShapeDtypeStruct((t, width), F32), name=name, compiler_params=_params(2),
    )(proj, proj, w)


def _conv_bwd(name, proj, w, dy, tm=256):
    t = proj.shape[0]
    tm = min(tm, t)
    width = w.shape[1]
    per, nt = tm // HALO, t // tm

    def body(cur_ref, prev_ref, w_ref, dy_ref, nxt_ref, du_ref, dw_ref):
        i = pl.program_id(1)
        cur, dy_t = cur_ref[...], dy_ref[...]
        prev = jnp.where(i > 0, prev_ref[...], 0.0)
        nxt = jnp.where(i < nt - 1, nxt_ref[...], 0.0)
        du = dy_t * w_ref[3:4, :]
        rows = [jnp.sum(dy_t * cur, axis=0, keepdims=True)]
        for s in (1, 2, 3):
            du = du + _shift_up(dy_t, nxt, s) * w_ref[3 - s:4 - s, :]
            rows.insert(0, jnp.sum(dy_t * _shift_down(cur, prev, s), axis=0, keepdims=True))
        du_ref[...] = du.astype(du_ref.dtype)

        @pl.when(i == 0)
        def _():
            dw_ref[...] = jnp.zeros_like(dw_ref)

        dw_ref[...] += jnp.concatenate(rows + [jnp.zeros((HALO - 4, CONV_W), F32)], axis=0)

    return pl.pallas_call(
        body, grid=(width // CONV_W, nt),
        in_specs=[pl.BlockSpec((tm, CONV_W), lambda c, i: (i, c)),
                  pl.BlockSpec((HALO, CONV_W), lambda c, i: (jnp.maximum(i * per - 1, 0), c)),
                  pl.BlockSpec((HALO, CONV_W), lambda c, i: (0, c)),
                  pl.BlockSpec((tm, CONV_W), lambda c, i: (i, c)),
                  pl.BlockSpec((HALO, CONV_W), lambda c, i: (jnp.minimum((i + 1) * per, t // HALO - 1), c))],
        out_specs=[pl.BlockSpec((tm, CONV_W), lambda c, i: (i, c)), pl.BlockSpec((HALO, CONV_W), lambda c, i: (0, c))],
        out_shape=[jax.ShapeDtypeStruct((t, width), BF16), jax.ShapeDtypeStruct((HALO, width), F32)],
        name=name, compiler_params=_params(2),
    )(proj, proj, w, dy, dy)


def _norm_fn(x, g):
    return _rms(x, g)


def _swiglu_fn(gu):
    return _silu(gu[:, :FFN_HIDDEN]) * gu[:, FFN_HIDDEN:]


def _heads(x):
    return [x[:, h * HEAD:(h + 1) * HEAD] for h in range(x.shape[1] // HEAD)]


def _dn_pre_fn(c, ab, a_log, dt_bias):
    w = N_HEADS * HEAD
    q = [_l2(_silu(x)) * (HEAD ** -0.5) for x in _heads(c[:, :w])]
    k = [_l2(_silu(x)) for x in _heads(c[:, w:2 * w])]
    v = _silu(c[:, 2 * w:])
    g, beta = [], []
    for h in range(N_HEADS):
        gh = -jnp.exp(a_log[:, h:h + 1]) * jax.nn.softplus(ab[:, h:h + 1] + dt_bias[:, h:h + 1])
        bh = jax.nn.sigmoid(ab[:, N_HEADS + h:N_HEADS + h + 1])
        g.append(jnp.broadcast_to(gh, (c.shape[0], HEAD)))
        beta.append(jnp.broadcast_to(bh, (c.shape[0], HEAD)))
    cat = lambda xs: jnp.concatenate(xs, axis=1)
    return cat(q), cat(k), v, cat(g), cat(beta)


def _dn_post_fn(o, z, out_norm):
    return jnp.concatenate([_rms(oh, out_norm) * _silu(zh) for oh, zh in zip(_heads(o), _heads(z))], axis=1)


def _sb_pre_fn(qkv, q_norm, k_norm):
    w = N_HEADS * HEAD
    q = [_rms(x, q_norm) * (HEAD ** -0.5) for x in _heads(qkv[:, :w])]
    k = [_rms(x, k_norm) for x in _heads(qkv[:, w:2 * w])]
    return jnp.concatenate(q, axis=1), jnp.concatenate(k, axis=1), qkv[:, 2 * w:]


def _mla_a_fn(down, cos, sin_lo, sin_hi, q_a_norm, kv_a_norm, k_rope_norm):
    cq = _rms(down[:, :256], q_a_norm)
    ckv = _rms(down[:, 256:384], kv_a_norm)
    kr = _rope(_rms(down[:, 384:], k_rope_norm, MLA_ROPE), cos, sin_lo, sin_hi)
    return cq, ckv, kr


def _mla_b_fn(qf, kvf, cos, sin_lo, sin_hi, q_nope_norm, q_rope_norm, k_nope_norm):
    scale = MLA_QK ** -0.5
    qn, qr, kn, v = [], [], [], []
    for h in range(N_HEADS):
        a = 2 * h * HEAD
        qn.append(_rms(qf[:, a:a + HEAD], q_nope_norm) * scale)
        qr.append(_rope(_rms(qf[:, a + HEAD:a + 2 * HEAD], q_rope_norm, MLA_ROPE), cos, sin_lo, sin_hi) * scale)
        kn.append(_rms(kvf[:, a:a + HEAD], k_nope_norm))
        v.append(kvf[:, a + HEAD:a + 2 * HEAD])
    cat = lambda xs: jnp.concatenate(xs, axis=1)
    return cat(qn), cat(qr), cat(kn), cat(v)


def _rope_tables(t):
    inv_freq = ROPE_THETA ** (-jnp.arange(0, MLA_ROPE, 2, dtype=F32) / MLA_ROPE)
    ang = jnp.arange(t, dtype=F32)[:, None] * inv_freq[None, :]
    cos, sin, zero = jnp.cos(ang), jnp.sin(ang), jnp.zeros((t, MLA_ROPE // 2), F32)
    cat = lambda xs: jnp.concatenate(xs, axis=1)
    return cat([cos, cos, zero, zero]), cat([-sin, zero, zero, zero]), cat([zero, sin, zero, zero])


def _row(v, width=None):
    width = v.shape[0] if width is None else width
    return jnp.pad(v.astype(F32), (0, width - v.shape[0])).reshape(1, width)


def _loss_kernel(y, target):
    t, d = y.shape
    tm = min(256, t)

    def body(y_ref, t_ref, part_ref, dy_ref):
        e = y_ref[...] - t_ref[...]
        dy_ref[...] = e * (1.0 / d)

        @pl.when(pl.program_id(0) == 0)
        def _():
            part_ref[...] = jnp.zeros_like(part_ref)

        part_ref[...] += jnp.sum(e * e, axis=0, keepdims=True)

    blk = pl.BlockSpec((tm, d), lambda i: (i, 0))
    one = pl.BlockSpec((1, d), lambda i: (0, 0))
    return pl.pallas_call(body, grid=(t // tm,), in_specs=[blk, blk], out_specs=[one, blk],
                          out_shape=[jax.ShapeDtypeStruct((1, d), F32), jax.ShapeDtypeStruct((t, d), F32)],
                          name="loss", compiler_params=_params(1))(y, target)


def _ffn_fwd(p, x, w, sm):
    h, = _ew(p + "ffn_norm", _norm_fn, [x], [sm["ffn_norm"]], [(D_MODEL, BF16)])
    gu = _mm(p + "ffn_gu", h, w["ffn_w_gate_up"], "nn")
    act, = _ew(p + "ffn_act", _swiglu_fn, [gu], [], [(FFN_HIDDEN, BF16)], tm=128)
    y = _mm(p + "ffn_down", act, w["ffn_w_down"], "nn", add=x)
    return y, (x, h, gu, act)


def _ffn_bwd(p, saved, dy, w, sm, grads):
    x, h, gu, act = saved
    dact = _mm(p + "ffn_down_dx", dy, w["ffn_w_down"], "nt")
    grads["ffn_w_down"] = _mm(p + "ffn_down_dw", act, dy, "tn")
    (dgu,), _ = _ew_bwd(p + "ffn_act_bwd", _swiglu_fn, [gu], [], [dact], tm=128, ct_dtypes=[BF16])
    dh = _mm(p + "ffn_gu_dx", dgu, w["ffn_w_gate_up"], "nt")
    grads["ffn_w_gate_up"] = _mm(p + "ffn_gu_dw", h, dgu, "tn")
    (dx,), (dg,) = _ew_bwd(p + "ffn_norm_bwd", _norm_fn, [x], [sm["ffn_norm"]], [dh], add=dy)
    grads["ffn_norm"] = dg
    return dx


def _dn_layer_fwd(p, x, w, sm):
    h, = _ew(p + "mix_norm", _norm_fn, [x], [sm["mix_norm"]], [(D_MODEL, BF16)])
    proj = _mm(p + "dn_in", h, w["dn_w_in"], "nn")
    conv = _conv_fwd(p + "dn_conv", proj, sm["dn_conv_w"])
    ab = _V(proj, LANES, 4 * N_HEADS)
    wide = N_HEADS * HEAD
    q, k, v, g, beta = _ew(p + "dn_pre", _dn_pre_fn, [conv, ab], [sm["dn_a_log"], sm["dn_dt_bias"]], [(wide, F32)] * 5)
    o, states = _dn_fwd(p + "dn_core", q, k, v, g, beta)
    z = _V(proj, wide, 3)
    on, = _ew(p + "dn_post", _dn_post_fn, [o, z], [sm["dn_out_norm"]], [(wide, BF16)])
    y = _mm(p + "dn_out", on, w["dn_w_out"], "nn", add=x)
    return y, (x, h, proj, conv, q, k, v, g, beta, o, states, on)


def _dn_layer_bwd(p, saved, dy, w, sm, grads):
    x, h, proj, conv, q, k, v, g, beta, o, states, on = saved
    wide = N_HEADS * HEAD
    don = _mm(p + "dn_out_dx", dy, w["dn_w_out"], "nt")
    grads["dn_w_out"] = _mm(p + "dn_out_dw", on, dy, "tn")
    (do, dz), (d_out_norm,) = _ew_bwd(p + "dn_post_bwd", _dn_post_fn, [o, _V(proj, wide, 3)], [sm["dn_out_norm"]], [don],
                                          ct_dtypes=[F32, BF16])
    grads["dn_out_norm"] = d_out_norm
    dq, dk, dv, dg, db = _dn_bwd(p + "dn_core_bwd", q, k, v, g, beta, states, do)
    (dconv, dab), (d_a_log, d_dt) = _ew_bwd(p + "dn_pre_bwd", _dn_pre_fn, [conv, _V(proj, LANES, 4 * N_HEADS)],
                                            [sm["dn_a_log"], sm["dn_dt_bias"]], [dq, dk, dv, dg, db], ct_dtypes=[F32, BF16])
    grads["dn_a_log"], grads["dn_dt_bias"] = d_a_log, d_dt
    dqkv, dconv_w = _conv_bwd(p + "dn_conv_bwd", proj, sm["dn_conv_w"], dconv)
    grads["dn_conv_w"] = dconv_w
    dproj = jnp.concatenate([dqkv, dz, dab], axis=1)
    dh = _mm(p + "dn_in_dx", dproj, w["dn_w_in"], "nt")
    grads["dn_w_in"] = _mm(p + "dn_in_dw", h, dproj, "tn")
    (dx,), (dgain,) = _ew_bwd(p + "mix_norm_bwd", _norm_fn, [x], [sm["mix_norm"]], [dh], add=dy)
    grads["mix_norm"] = dgain
    return dx


def _sb_layer_fwd(p, x, w, sm):
    h, = _ew(p + "mix_norm", _norm_fn, [x], [sm["mix_norm"]], [(D_MODEL, BF16)])
    qkv = _mm(p + "sb_qkv", h, w["sb_w_qkv"], "nn")
    wide = N_HEADS * HEAD
    q, k, v = _ew(p + "sb_pre", _sb_pre_fn, [qkv], [sm["sb_q_norm"], sm["sb_k_norm"]], [(wide, BF16)] * 3)
    o, stat = _attn_fwd(p + "sb_core", "sb", [q], [(k, False)], v)
    y = _mm(p + "sb_out", o, w["sb_w_out"], "nn", add=x)
    return y, (x, h, qkv, q, k, v, o, stat)


def _sb_layer_bwd(p, saved, dy, w, sm, grads):
    x, h, qkv, q, k, v, o, stat = saved
    do = _mm(p + "sb_out_dx", dy, w["sb_w_out"], "nt")
    grads["sb_w_out"] = _mm(p + "sb_out_dw", o, dy, "tn")
    (dq,), (dk,), dv = _attn_bwd(p + "sb_core_bwd", "sb", [q], [(k, False)], v, o, stat, do)
    (dqkv,), (dqn, dkn) = _ew_bwd(p + "sb_pre_bwd", _sb_pre_fn, [qkv], [sm["sb_q_norm"], sm["sb_k_norm"]], [dq, dk, dv],
                                  ct_dtypes=[BF16])
    grads["sb_q_norm"], grads["sb_k_norm"] = dqn, dkn
    dh = _mm(p + "sb_qkv_dx", dqkv, w["sb_w_qkv"], "nt")
    grads["sb_w_qkv"] = _mm(p + "sb_qkv_dw", h, dqkv, "tn")
    (dx,), (dgain,) = _ew_bwd(p + "mix_norm_bwd", _norm_fn, [x], [sm["mix_norm"]], [dh], add=dy)
    grads["mix_norm"] = dgain
    return dx


def _mla_layer_fwd(p, x, w, sm):
    t = x.shape[0]
    tabs = [_V(a, diff=False) for a in _rope_tables(t)]
    h, = _ew(p + "mix_norm", _norm_fn, [x], [sm["mix_norm"]], [(D_MODEL, BF16)])
    down = _mm(p + "mla_down", h, w["mla_w_down"], "nn")
    sm_a = [sm["mla_q_a_norm"], sm["mla_kv_a_norm"], sm["mla_k_rope_norm"]]
    cq, ckv, kr = _ew(p + "mla_a", _mla_a_fn, [down] + tabs, sm_a, [(256, BF16), (128, BF16), (128, BF16)])
    qf = _mm(p + "mla_uq", cq, w["mla_w_uq"], "nn")
    kvf = _mm(p + "mla_ukv", ckv, w["mla_w_ukv"], "nn")
    sm_b = [sm["mla_q_nope_norm"], sm["mla_q_rope_norm"], sm["mla_k_nope_norm"]]
    wide = N_HEADS * HEAD
    qn, qr, kn, v = _ew(p + "mla_b", _mla_b_fn, [qf, kvf] + tabs, sm_b, [(wide, BF16)] * 4)
    o, stat = _attn_fwd(p + "mla_core", "softmax", [qn, qr], [(kn, False), (kr, True)], v)
    y = _mm(p + "mla_out", o, w["mla_w_out"], "nn", add=x)
    return y, (x, h, down, cq, ckv, kr, qf, kvf, qn, qr, kn, v, o, stat)


def _mla_layer_bwd(p, saved, dy, w, sm, grads):
    x, h, down, cq, ckv, kr, qf, kvf, qn, qr, kn, v, o, stat = saved
    tabs = [_V(a, diff=False) for a in _rope_tables(x.shape[0])]
    do = _mm(p + "mla_out_dx", dy, w["mla_w_out"], "nt")
    grads["mla_w_out"] = _mm(p + "mla_out_dw", o, dy, "tn")
    (dqn, dqr), (dkn, dkr), dv = _attn_bwd(p + "mla_core_bwd", "softmax", [qn, qr], [(kn, False), (kr, True)],
                                           v, o, stat, do)
    sm_b = [sm["mla_q_nope_norm"], sm["mla_q_rope_norm"], sm["mla_k_nope_norm"]]
    (dqf, dkvf), dsm_b = _ew_bwd(p + "mla_b_bwd", _mla_b_fn, [qf, kvf] + tabs, sm_b, [dqn, dqr, dkn, dv],
                                 ct_dtypes=[BF16, BF16])
    grads["mla_q_nope_norm"], grads["mla_q_rope_norm"], grads["mla_k_nope_norm"] = dsm_b
    dcq = _mm(p + "mla_uq_dx", dqf, w["mla_w_uq"], "nt")
    grads["mla_w_uq"] = _mm(p + "mla_uq_dw", cq, dqf, "tn")
    dckv = _mm(p + "mla_ukv_dx", dkvf, w["mla_w_ukv"], "nt")
    grads["mla_w_ukv"] = _mm(p + "mla_ukv_dw", ckv, dkvf, "tn")
    sm_a = [sm["mla_q_a_norm"], sm["mla_kv_a_norm"], sm["mla_k_rope_norm"]]
    (ddown,), dsm_a = _ew_bwd(p + "mla_a_bwd", _mla_a_fn, [down] + tabs, sm_a, [dcq, dckv, dkr], ct_dtypes=[BF16])
    grads["mla_q_a_norm"], grads["mla_kv_a_norm"], grads["mla_k_rope_norm"] = dsm_a
    dh = _mm(p + "mla_down_dx", ddown, w["mla_w_down"], "nt")
    grads["mla_w_down"] = _mm(p + "mla_down_dw", h, ddown, "tn")
    (dx,), (dgain,) = _ew_bwd(p + "mix_norm_bwd", _norm_fn, [x], [sm["mix_norm"]], [dh], add=dy)
    grads["mix_norm"] = dgain
    return dx


_MIX_FWD = (_dn_layer_fwd, _sb_layer_fwd, _mla_layer_fwd)
_MIX_BWD = (_dn_layer_bwd, _sb_layer_bwd, _mla_layer_bwd)


def _pad_cols(a, n):
    return jnp.pad(a, ((0, 0), (0, n - a.shape[1])))


def _prep_big(name, a):
    if name.endswith("dn_w_in"):
        return _pad_cols(a, 4 * N_HEADS * HEAD + LANES)
    if name.endswith("mla_w_down"):
        return _pad_cols(a, 512)
    if name.endswith("mla_w_uq"):
        a3 = a.reshape(a.shape[0], N_HEADS, MLA_QK)
        return jnp.pad(a3, ((0, 0), (0, 0), (0, 2 * HEAD - MLA_QK))).reshape(a.shape[0], N_HEADS * 2 * HEAD)
    return a


def _unprep_big(name, g):
    if name.endswith("dn_w_in"):
        return g[:, :4 * N_HEADS * HEAD + 2 * N_HEADS]
    if name.endswith("mla_w_down"):
        return g[:, :448]
    if name.endswith("mla_w_uq"):
        return g.reshape(g.shape[0], N_HEADS, 2 * HEAD)[:, :, :MLA_QK].reshape(g.shape[0], N_HEADS * MLA_QK)
    return g


def _prep_small(name, a):
    if name.endswith("dn_conv_w"):
        return jnp.pad(a.astype(F32), ((0, HALO - a.shape[0]), (0, 0)))
    if name.endswith(("dn_a_log", "dn_dt_bias", "mla_q_rope_norm", "mla_k_rope_norm")):
        return _row(a, LANES)
    return _row(a)


def _unprep_small(name, g, like):
    if name.endswith("dn_conv_w"):
        return g[:like.shape[0]]
    return g.reshape(-1)[:like.shape[0]]


def local_step(x, target, big, small):
    layers = []
    for i in range(DEPTH):
        p = "l%d_" % i
        w = {n: _prep_big(n, big[p + n]) for n in _MIXERS[i % 3] + ("ffn_w_gate_up", "ffn_w_down") if p + n in big}
        sm = {n: _prep_small(n, small[p + n]) for n in _MIXERS[i % 3] + ("mix_norm", "ffn_norm") if p + n in small}
        layers.append((p, w, sm))
    saved = []
    for i, (p, w, sm) in enumerate(layers):
        x, s_mix = _MIX_FWD[i % 3](p, x, w, sm)
        x, s_ffn = _ffn_fwd(p, x, w, sm)
        saved.append((s_mix, s_ffn))
    part, dx = _loss_kernel(x, target)
    grads = {}
    for i in reversed(range(DEPTH)):
        p, w, sm = layers[i]
        g = {}
        dx = _ffn_bwd(p, saved[i][1], dx, w, sm, g)
        dx = _MIX_BWD[i % 3](p, saved[i][0], dx, w, sm, g)
        for n, val in g.items():
            grads[p + n] = _unprep_big(p + n, val) if p + n in big else _unprep_small(p + n, val, small[p + n])
    return part, dx, grads


ROW = 1024
BIG_ROWS = 1024


PACK_ALIGN = 16


def _packed_rows(shape):
    return -(-math.prod(shape) // (ROW * PACK_ALIGN)) * PACK_ALIGN


def _as_rows(a, lead=()):
    rows = _packed_rows(a.shape[len(lead):])
    flat = a.reshape(lead + (-1,))
    return jnp.pad(flat, ((0, 0),) * len(lead) + ((0, rows * ROW - flat.shape[-1]),)).reshape(lead + (rows, ROW))


def _pack(arrs, rows_multiple, lead=()):
    blocks = [_as_rows(a, lead) for a in arrs]
    used = sum(b.shape[-2] for b in blocks)
    fill = -(-used // rows_multiple) * rows_multiple - used
    if fill:
        blocks.append(jnp.zeros(lead + (fill, ROW), blocks[0].dtype))
    return jnp.concatenate(blocks, axis=len(lead))


def _unpack(buf, shapes, lead=()):
    out, r0 = [], 0
    for s in shapes:
        rows, n = _packed_rows(s), math.prod(s)
        block = lax.slice_in_dim(buf, r0, r0 + rows, axis=len(lead))
        out.append(block.reshape(lead + (-1,))[..., :n].reshape(lead + tuple(s)))
        r0 += rows
    return out


def _me():
    return lax.axis_index("x"), lax.axis_index("y"), lax.axis_index("c")


def _other_chips(x, y):
    return [(1 - x, y), (x, 1 - y), (1 - x, 1 - y)]


HBM = pl.BlockSpec(memory_space=pl.ANY)


OWN_STREAMS = 4


def _gather_shards(packed):
    rows = packed.shape[0]
    half = rows // 2

    def body(x_ref, out_ref, send_sems, recv_sems):
        x, y, c = _me()
        sibling, chips = (x, y, 1 - c), _other_chips(x, y)

        def part(px, py, pc):
            return out_ref.at[2 * px + py, pl.ds(pl.multiple_of(pc * half, 16), half), :]

        def copy(k, block, to, src=None):
            return pltpu.make_async_remote_copy(
                src_ref=part(*block) if src is None else src, dst_ref=part(*block),
                send_sem=send_sems.at[k], recv_sem=recv_sems.at[k], device_id=to, device_id_type=MESH)

        my_half = x_ref.at[pl.ds(pl.multiple_of(c * half, 16), half), :]
        first = [copy(j, (x, y, c), (*chip, c), src=my_half) for j, chip in enumerate(chips)]
        piece = rows // OWN_STREAMS
        for p in range(OWN_STREAMS):
            rows_p = pl.ds(p * piece, piece)
            first.append(pltpu.make_async_remote_copy(
                src_ref=x_ref.at[rows_p, :], dst_ref=out_ref.at[2 * x + y, rows_p, :], send_sem=send_sems.at[6 + p],
                recv_sem=recv_sems.at[6 + p], device_id=sibling, device_id_type=MESH))
        for cp in first:
            cp.start()
        passed = [copy(3 + j, (*chip, c), sibling) for j, chip in enumerate(chips)]
        for j, chip in enumerate(chips):
            copy(j, (*chip, c), (x, y, c)).wait_recv()
            passed[j].start()
        for j, chip in enumerate(chips):
            copy(3 + j, (*chip, 1 - c), (x, y, c)).wait_recv()
        for cp in first[3:]:
            cp.wait_recv()
        for cp in first + passed:
            cp.wait_send()

    n_sems = 6 + OWN_STREAMS
    return pl.pallas_call(
        body, out_shape=jax.ShapeDtypeStruct((N_CHIPS,) + packed.shape, packed.dtype), in_specs=[HBM], out_specs=HBM,
        scratch_shapes=[pltpu.SemaphoreType.DMA((n_sems,)), pltpu.SemaphoreType.DMA((n_sems,))],
        name="gather_weights",
    )(packed)


D2D_STREAMS = 16


def _swap_halves(g):
    n, rows, _ = g.shape
    half = rows // 2
    per = D2D_STREAMS // n
    piece = half // per

    def body(g_ref, theirs_ref, send_sems, recv_sems):
        x, y, c = _me()
        give = (1 - c) * half
        copies = []
        for j in range(n):
            for p in range(per):
                k = j * per + p
                cp = pltpu.make_async_remote_copy(
                    src_ref=g_ref.at[j, pl.ds(pl.multiple_of(give + p * piece, 8), piece), :],
                    dst_ref=theirs_ref.at[j, pl.ds(p * piece, piece), :],
                    send_sem=send_sems.at[k], recv_sem=recv_sems.at[k], device_id=(x, y, 1 - c), device_id_type=MESH)
                cp.start()
                copies.append(cp)
        for cp in copies:
            cp.wait()

    return pl.pallas_call(
        body, out_shape=jax.ShapeDtypeStruct((n, half, ROW), g.dtype), in_specs=[HBM], out_specs=HBM,
        scratch_shapes=[pltpu.SemaphoreType.DMA((D2D_STREAMS,)), pltpu.SemaphoreType.DMA((D2D_STREAMS,))],
        name="grad_swap_halves",
    )(g)


def _pair_sum(g, theirs, c):
    n, half, _ = theirs.shape
    tm = _tile(half)
    nb = half // tm

    def body(c_ref, g_ref, t_ref, s32_ref, s16_ref):
        s = g_ref[...] + t_ref[...]
        s32_ref[...] = s
        s16_ref[...] = s.astype(BF16)

    blk = pl.BlockSpec((None, tm, ROW), lambda j, i, c_ref: (j, i, 0))
    return pl.pallas_call(
        body,
        grid_spec=pltpu.PrefetchScalarGridSpec(
            num_scalar_prefetch=1, grid=(n, nb),
            in_specs=[pl.BlockSpec((None, tm, ROW), lambda j, i, c_ref: (j, c_ref[0] * nb + i, 0)), blk],
            out_specs=[blk, blk]),
        out_shape=[jax.ShapeDtypeStruct(theirs.shape, F32), jax.ShapeDtypeStruct(theirs.shape, BF16)],
        name="grad_pair_sum", compiler_params=_params(2),
    )(c.reshape(1).astype(jnp.int32), g, theirs)


def _scatter_chunks(s16):
    _, half, _ = s16.shape

    def body(s16_ref, got_ref, send_sems, recv_sems):
        x, y, c = _me()
        sends = []
        for j, (px, py) in enumerate(_other_chips(x, y)):
            cp = pltpu.make_async_remote_copy(src_ref=s16_ref.at[2 * px + py], dst_ref=got_ref.at[j],
                                              send_sem=send_sems.at[j], recv_sem=recv_sems.at[j],
                                              device_id=(px, py, c), device_id_type=MESH)
            cp.start()
            sends.append(cp)
        for cp in sends:
            cp.wait()

    return pl.pallas_call(
        body, out_shape=jax.ShapeDtypeStruct((3, half, ROW), BF16), in_specs=[HBM], out_specs=HBM,
        scratch_shapes=[pltpu.SemaphoreType.DMA((3,)), pltpu.SemaphoreType.DMA((3,))],
        name="grad_scatter",
    )(s16)


def _chip_sum(s32, got, chip, c):
    _, half, _ = s32.shape
    tm = _tile(half)
    nb = half // tm

    def body(where_ref, own_ref, g0_ref, g1_ref, g2_ref, o_ref):
        o_ref[...] = ((own_ref[...] + g0_ref[...].astype(F32)) + g1_ref[...].astype(F32)) + g2_ref[...].astype(F32)

    got_spec = lambda k: pl.BlockSpec((None, tm, ROW), lambda i, where_ref, k=k: (k, i, 0))
    return pl.pallas_call(
        body,
        grid_spec=pltpu.PrefetchScalarGridSpec(
            num_scalar_prefetch=1, grid=(nb,),
            in_specs=[pl.BlockSpec((None, tm, ROW), lambda i, where_ref: (where_ref[0], i, 0)), got_spec(0), got_spec(1), got_spec(2)],
            out_specs=pl.BlockSpec((tm, ROW), lambda i, where_ref: (where_ref[1] * nb + i, 0))),
        out_shape=jax.ShapeDtypeStruct((2 * half, ROW), F32), name="grad_chip_sum", compiler_params=_params(1),
    )(jnp.stack([chip, c]).astype(jnp.int32), s32, got, got, got)


def _join_halves(f):
    half = f.shape[0] // 2
    piece = half // D2D_STREAMS

    def body(f_ref, out_ref, send_sems, recv_sems):
        x, y, c = _me()
        copies = []
        for p in range(D2D_STREAMS):
            rows = out_ref.at[pl.ds(pl.multiple_of(c * half + p * piece, 8), piece), :]
            cp = pltpu.make_async_remote_copy(src_ref=rows, dst_ref=rows, send_sem=send_sems.at[p], recv_sem=recv_sems.at[p],
                                              device_id=(x, y, 1 - c), device_id_type=MESH)
            cp.start()
            copies.append(cp)
        for cp in copies:
            cp.wait()

    return pl.pallas_call(
        body, out_shape=jax.ShapeDtypeStruct(f.shape, F32), in_specs=[HBM], out_specs=HBM, input_output_aliases={0: 0},
        scratch_shapes=[pltpu.SemaphoreType.DMA((D2D_STREAMS,)), pltpu.SemaphoreType.DMA((D2D_STREAMS,))],
        name="grad_join_halves",
    )(f)


def _all_reduce_small(name, v):
    rows = v.shape[0]

    def body(v_ref, out_ref, slots, send_sems, recv_sems):
        x, y, c = _me()
        me = 4 * x + 2 * y + c
        slots[me] = v_ref[...]
        sends = []
        for r in range(1, 8):
            to = (x ^ (r >> 2), y ^ ((r >> 1) & 1), c ^ (r & 1))
            cp = pltpu.make_async_remote_copy(src_ref=v_ref, dst_ref=slots.at[me], send_sem=send_sems.at[r - 1],
                                              recv_sem=recv_sems.at[r - 1], device_id=to, device_id_type=MESH)
            cp.start()
            sends.append(cp)
        for cp in sends:
            cp.wait()
        total = slots[0]
        for d in range(1, 8):
            total = total + slots[d]
        out_ref[...] = total

    vmem = pl.BlockSpec(memory_space=pltpu.VMEM)
    return pl.pallas_call(
        body, out_shape=jax.ShapeDtypeStruct(v.shape, F32), in_specs=[vmem], out_specs=vmem,
        scratch_shapes=[pltpu.VMEM((8, rows, ROW), F32), pltpu.SemaphoreType.DMA((7,)), pltpu.SemaphoreType.DMA((7,))],
        name=name,
    )(v)


def _adam_fn(w, g, m, v):
    m2 = ADAM_B1 * m + (1.0 - ADAM_B1) * g
    v2 = ADAM_B2 * v + (1.0 - ADAM_B2) * (g * g)
    m_hat = m2 / (1.0 - ADAM_B1 ** ADAM_STEP)
    v_hat = v2 / (1.0 - ADAM_B2 ** ADAM_STEP)
    return -ADAM_LR * (m_hat / (jnp.sqrt(v_hat) + ADAM_EPS) + ADAM_WD * w), m2, v2


def _full_shape(name, shard_shape):
    ax = _shard_axis(name)
    return tuple(n * N_CHIPS if k == ax else n for k, n in enumerate(shard_shape))


def _chip_major(name, full):
    if _shard_axis(name) == 0:
        return full.reshape(N_CHIPS, -1, full.shape[1])
    n = full.shape[1] // N_CHIPS
    return jnp.stack([full[:, j * n:(j + 1) * n] for j in range(N_CHIPS)])


def _from_chip_major(name, shards):
    if _shard_axis(name) == 0:
        return shards.reshape(-1, shards.shape[2])
    return jnp.concatenate([shards[j] for j in range(N_CHIPS)], axis=1)


def _row_tile(rows, cap=512):
    return max(t for t in range(8, min(rows, cap) + 1, 8) if rows % t == 0)


def _step(a):
    x_i, y_i, c_i = _me()
    chip = 2 * x_i + y_i
    big_shapes = [a[n].shape for n in BIG]

    packed = _pack([a[n].astype(BF16) for n in BIG], BIG_ROWS)
    shards = _unpack(_gather_shards(packed), big_shapes, lead=(N_CHIPS,))
    big = {n: _from_chip_major(n, sh) for n, sh in zip(BIG, shards)}

    small = {n: a[n] for n in SMALL}
    convs = [n for n in SMALL if n.endswith("dn_conv_w")]
    placed = []
    for n in convs:
        full = jnp.zeros(_full_shape_conv(a[n].shape), F32)
        placed.append(lax.dynamic_update_slice(full, a[n], (0, chip * a[n].shape[1])))
    conv_sum = _all_reduce_small("gather_conv", _pack(placed, 8))
    for n, full in zip(convs, _unpack(conv_sum, [p.shape for p in placed])):
        small[n] = full * 0.5

    part, dx, grads = local_step(a["x"][0], a["loss_target"][0], big, small)
    loss = lax.psum(0.5 * jnp.sum(part) / D_MODEL, ("x", "y", "c"))

    g_all = _pack([_chip_major(n, grads[n]) for n in BIG], BIG_ROWS, lead=(N_CHIPS,))
    s32, s16 = _pair_sum(g_all, _swap_halves(g_all), c_i)
    g_big = _join_halves(_chip_sum(s32, _scatter_chunks(s16), chip, c_i))
    g_big = dict(zip(BIG, _unpack(g_big, big_shapes)))

    g_small_full = _all_reduce_small("reduce_small", _pack([grads[n] for n in SMALL], 8))
    g_small = dict(zip(SMALL, _unpack(g_small_full, [grads[n].shape for n in SMALL])))
    for n in convs:
        g_small[n] = lax.dynamic_slice_in_dim(g_small[n], chip * a[n].shape[1], a[n].shape[1], axis=1)

    outs = {}
    for n in BIG:
        d, m2, v2 = _ew("adam_" + n, _adam_fn, [a[n], g_big[n], a["m_" + n], a["v_" + n]], [],
                        [(a[n].shape[1], F32)] * 3, tm=_row_tile(a[n].shape[0]))
        outs.update({"grad_" + n: g_big[n], "delta_" + n: d, "new_m_" + n: m2, "new_v_" + n: v2})
    pk = lambda prefix: _pack([a[prefix + n] for n in SMALL], 8)
    gs_packed = _pack([g_small[n] for n in SMALL], 8)
    small_bufs = (gs_packed,) + tuple(_ew("adam_small", _adam_fn, [pk(""), gs_packed, pk("m_"), pk("v_")], [], [(ROW, F32)] * 3,
                                          tm=gs_packed.shape[0]))
    small_shapes = [a[n].shape for n in SMALL]
    for key, buf in zip(("grad_", "delta_", "new_m_", "new_v_"), small_bufs):
        outs.update({key + n: val for n, val in zip(SMALL, _unpack(buf, small_shapes))})
    result = [loss, dx[None]]
    for key in ("grad_", "delta_", "new_m_", "new_v_"):
        result += [outs[key + n] for n in WEIGHTS]
    return tuple(result)


def _full_shape_conv(shard_shape):
    return (shard_shape[0], shard_shape[1] * N_CHIPS)


def kernel(x, l0_mix_norm, l0_dn_w_in, l0_dn_conv_w, l0_dn_a_log, l0_dn_dt_bias, l0_dn_out_norm, l0_dn_w_out, l0_ffn_norm, l0_ffn_w_gate_up, l0_ffn_w_down, l1_mix_norm, l1_sb_w_qkv, l1_sb_q_norm, l1_sb_k_norm, l1_sb_w_out, l1_ffn_norm, l1_ffn_w_gate_up, l1_ffn_w_down, l2_mix_norm, l2_mla_w_down, l2_mla_q_a_norm, l2_mla_kv_a_norm, l2_mla_w_uq, l2_mla_w_ukv, l2_mla_q_nope_norm, l2_mla_q_rope_norm, l2_mla_k_nope_norm, l2_mla_k_rope_norm, l2_mla_w_out, l2_ffn_norm, l2_ffn_w_gate_up, l2_ffn_w_down, l3_mix_norm, l3_dn_w_in, l3_dn_conv_w, l3_dn_a_log, l3_dn_dt_bias, l3_dn_out_norm, l3_dn_w_out, l3_ffn_norm, l3_ffn_w_gate_up, l3_ffn_w_down, loss_target, m_l0_mix_norm, m_l0_dn_w_in, m_l0_dn_conv_w, m_l0_dn_a_log, m_l0_dn_dt_bias, m_l0_dn_out_norm, m_l0_dn_w_out, m_l0_ffn_norm, m_l0_ffn_w_gate_up, m_l0_ffn_w_down, m_l1_mix_norm, m_l1_sb_w_qkv, m_l1_sb_q_norm, m_l1_sb_k_norm, m_l1_sb_w_out, m_l1_ffn_norm, m_l1_ffn_w_gate_up, m_l1_ffn_w_down, m_l2_mix_norm, m_l2_mla_w_down, m_l2_mla_q_a_norm, m_l2_mla_kv_a_norm, m_l2_mla_w_uq, m_l2_mla_w_ukv, m_l2_mla_q_nope_norm, m_l2_mla_q_rope_norm, m_l2_mla_k_nope_norm, m_l2_mla_k_rope_norm, m_l2_mla_w_out, m_l2_ffn_norm, m_l2_ffn_w_gate_up, m_l2_ffn_w_down, m_l3_mix_norm, m_l3_dn_w_in, m_l3_dn_conv_w, m_l3_dn_a_log, m_l3_dn_dt_bias, m_l3_dn_out_norm, m_l3_dn_w_out, m_l3_ffn_norm, m_l3_ffn_w_gate_up, m_l3_ffn_w_down, v_l0_mix_norm, v_l0_dn_w_in, v_l0_dn_conv_w, v_l0_dn_a_log, v_l0_dn_dt_bias, v_l0_dn_out_norm, v_l0_dn_w_out, v_l0_ffn_norm, v_l0_ffn_w_gate_up, v_l0_ffn_w_down, v_l1_mix_norm, v_l1_sb_w_qkv, v_l1_sb_q_norm, v_l1_sb_k_norm, v_l1_sb_w_out, v_l1_ffn_norm, v_l1_ffn_w_gate_up, v_l1_ffn_w_down, v_l2_mix_norm, v_l2_mla_w_down, v_l2_mla_q_a_norm, v_l2_mla_kv_a_norm, v_l2_mla_w_uq, v_l2_mla_w_ukv, v_l2_mla_q_nope_norm, v_l2_mla_q_rope_norm, v_l2_mla_k_nope_norm, v_l2_mla_k_rope_norm, v_l2_mla_w_out, v_l2_ffn_norm, v_l2_ffn_w_gate_up, v_l2_ffn_w_down, v_l3_mix_norm, v_l3_dn_w_in, v_l3_dn_conv_w, v_l3_dn_a_log, v_l3_dn_dt_bias, v_l3_dn_out_norm, v_l3_dn_w_out, v_l3_ffn_norm, v_l3_ffn_w_gate_up, v_l3_ffn_w_down):
    return _step(dict(locals()))
```

```python
import functools
import math

import jax
import jax.numpy as jnp
from jax import lax
from jax.experimental import pallas as pl
from jax.experimental.pallas import tpu as pltpu

F32, BF16 = jnp.float32, jnp.bfloat16
MESH = pl.DeviceIdType.MESH

D_MODEL = 1024
N_HEADS = 8
HEAD = 128
FFN_HIDDEN = 2816
DN_CHUNK = 64
NORM_EPS = 1e-6
MLA_ROPE = 64
MLA_QK = 192
ROPE_THETA = 10000.0
ADAM_LR, ADAM_B1, ADAM_B2, ADAM_EPS, ADAM_WD, ADAM_STEP = 0.001, 0.9, 0.999, 1e-08, 0.01, 10
N_CHIPS = 4
LANES = 128
VMEM_LIMIT = 56 * 2 ** 20


def _params(n_grid):
    return pltpu.CompilerParams(dimension_semantics=("arbitrary",) * n_grid, vmem_limit_bytes=VMEM_LIMIT)


_MIXERS = (
    ("dn_w_in", "dn_conv_w", "dn_a_log", "dn_dt_bias", "dn_out_norm", "dn_w_out"),
    ("sb_w_qkv", "sb_q_norm", "sb_k_norm", "sb_w_out"),
    ("mla_w_down", "mla_q_a_norm", "mla_kv_a_norm", "mla_w_uq", "mla_w_ukv", "mla_q_nope_norm",
     "mla_q_rope_norm", "mla_k_nope_norm", "mla_k_rope_norm", "mla_w_out"),
)
DEPTH = 4


def _layer_names(i):
    p = "l%d_" % i
    return [p + "mix_norm"] + [p + n for n in _MIXERS[i % 3]] + [p + "ffn_norm", p + "ffn_w_gate_up", p + "ffn_w_down"]


WEIGHTS = [n for i in range(DEPTH) for n in _layer_names(i)]
_ROW_SHARDED = ("w_out", "ffn_w_down", "mla_w_down")
_COL_SHARDED = ("dn_w_in", "sb_w_qkv", "mla_w_uq", "mla_w_ukv", "ffn_w_gate_up")


def _shard_axis(name):
    if name.endswith(_ROW_SHARDED):
        return 0
    if name.endswith(_COL_SHARDED):
        return 1
    return None


BIG = [n for n in WEIGHTS if _shard_axis(n) is not None]
SMALL = [n for n in WEIGHTS if _shard_axis(n) is None]


_DN = {"nn": (((1,), (0,)), ((), ())), "nt": (((1,), (1,)), ((), ())), "tn": (((0,), (0,)), ((), ()))}
_DN_BATCHED = {"nn": (((2,), (1,)), ((0,), (0,))), "nt": (((2,), (2,)), ((0,), (0,))), "tn": (((1,), (1,)), ((0,), (0,)))}


def _dims(a, kind):
    return _DN_BATCHED[kind] if a.ndim == 3 else _DN[kind]


def _dg(a, b, kind):
    return lax.dot_general(a.astype(BF16), b.astype(BF16), _dims(a, kind), preferred_element_type=F32)


@functools.partial(jax.custom_vjp, nondiff_argnums=(2,))
def bdot(a, b, kind):
    return _dg(a, b, kind)


def _bdot_fwd(a, b, kind):
    return _dg(a, b, kind), (a, b)


def _bdot_bwd(kind, res, ct):
    a, b = res
    if kind == "nn":
        return _dg(ct, b, "nt"), _dg(a, ct, "tn")
    if kind == "nt":
        return _dg(ct, b, "nn"), _dg(ct, a, "tn")
    return _dg(b, ct, "nt"), _dg(a, ct, "nn")


bdot.defvjp(_bdot_fwd, _bdot_bwd)


def _split(a, terms):
    out = []
    for _ in range(terms):
        hi = a.astype(BF16)
        out.append(hi)
        a = a - hi.astype(F32)
    return out


def _xdot(a, b, kind, exact, terms=3):
    if exact == 0:
        return sum(lax.dot_general(a, p, _dims(a, kind), preferred_element_type=F32) for p in _split(b, terms))
    return sum(lax.dot_general(p, b, _dims(a, kind), preferred_element_type=F32) for p in _split(a, terms))


def _tri(n, rel):
    r = lax.broadcasted_iota(jnp.int32, (n, n), 0)
    c = lax.broadcasted_iota(jnp.int32, (n, n), 1)
    return {"le": c <= r, "lt": c < r, "ge": c >= r, "gt": c > r}[rel]


def _running(g):
    n = g.shape[-2]
    return jnp.broadcast_to(_tri(n, "le").astype(BF16), g.shape[:-2] + (n, n))


@jax.custom_vjp
def cumsum_rows(g):
    return _xdot(_running(g), g, "nn", 0)


def _cumsum_fwd(g):
    return cumsum_rows(g), None


def _cumsum_bwd(_, ct):
    return (_xdot(_running(ct), ct, "tn", 0),)


cumsum_rows.defvjp(_cumsum_fwd, _cumsum_bwd)


def _dot3(a, b, kind):
    (ah, al), (bh, bl) = _split(a, 2), _split(b, 2)
    dot = lambda p, q: lax.dot_general(p, q, _dims(a, kind), preferred_element_type=F32)
    return dot(ah, bh) + (dot(ah, bl) + dot(al, bh))


@functools.partial(jax.custom_vjp, nondiff_argnums=(2,))
def _hdot3(a, b, kind):
    return _dot3(a, b, kind)


def _hdot3_fwd(a, b, kind):
    return _dot3(a, b, kind), (a, b)


def _hdot3_bwd(kind, res, ct):
    a, b = res
    if kind == "nn":
        return _dot3(ct, b, "nt"), _dot3(a, ct, "tn")
    if kind == "nt":
        return _dot3(ct, b, "nn"), _dot3(ct, a, "tn")
    return _dot3(b, ct, "nt"), _dot3(a, ct, "nn")


_hdot3.defvjp(_hdot3_fwd, _hdot3_bwd)


def _hdot(a, b):
    return _hdot3(a, b, "nn")


def _unit_lower_inverse(lower):
    n = lower.shape[-1]
    eye = (lax.broadcasted_iota(jnp.int32, (n, n), 0) == lax.broadcasted_iota(jnp.int32, (n, n), 1)).astype(F32)
    m = -lower
    p = eye + m
    for _ in range(int(math.log2(n)) - 1):
        m = _hdot(m, m)
        p = p + _hdot(p, m)
    return p


def _rms(x, g, n=None):
    n = x.shape[-1] if n is None else n
    return x * lax.rsqrt(jnp.sum(x * x, axis=-1, keepdims=True) * (1.0 / n) + NORM_EPS) * g


def _l2(x):
    return x * lax.rsqrt(jnp.sum(x * x, axis=-1, keepdims=True) + NORM_EPS)


def _silu(x):
    return x * jax.nn.sigmoid(x)


def _logsig(z):
    return jnp.minimum(z, 0.0) - jnp.log1p(jnp.exp(-jnp.abs(z)))


@jax.custom_vjp
def _rope(x, cos, sin_lo, sin_hi):
    return x * cos + pltpu.roll(x, 96, 1) * sin_lo + pltpu.roll(x, 32, 1) * sin_hi


def _rope_fwd(x, cos, sin_lo, sin_hi):
    return _rope(x, cos, sin_lo, sin_hi), (cos, sin_lo, sin_hi)


def _rope_bwd(res, ct):
    cos, sin_lo, sin_hi = res
    dx = ct * cos + pltpu.roll(ct * sin_lo, 32, 1) + pltpu.roll(ct * sin_hi, 96, 1)
    return dx, jnp.zeros_like(cos), jnp.zeros_like(sin_lo), jnp.zeros_like(sin_hi)


_rope.defvjp(_rope_fwd, _rope_bwd)


def _tile(n, prefs=(512, 384, 256, 128)):
    for t in prefs:
        if n % t == 0:
            return t
    return n


MM_OUT_TILES = (1024, 1408, 512, 384, 256, 128)
MM_K_TILES = (1024, 512, 384, 256, 128)


def _mm(name, a, b, kind, out_dtype=F32, add=None):
    if kind == "tn":
        (kdim, m), n = a.shape, b.shape[1]
    else:
        (m, kdim), n = a.shape, (b.shape[0] if kind == "nt" else b.shape[1])
    tm, tn, tk = _tile(m, MM_OUT_TILES), _tile(n, MM_OUT_TILES), _tile(kdim, MM_K_TILES)
    nk = kdim // tk
    a_spec = pl.BlockSpec((tk, tm), lambda i, j, k: (k, i)) if kind == "tn" else pl.BlockSpec((tm, tk), lambda i, j, k: (i, k))
    b_spec = pl.BlockSpec((tn, tk), lambda i, j, k: (j, k)) if kind == "nt" else pl.BlockSpec((tk, tn), lambda i, j, k: (k, j))
    o_spec = pl.BlockSpec((tm, tn), lambda i, j, k: (i, j))
    has_add = add is not None

    def body(*refs):
        a_ref, b_ref = refs[0], refs[1]
        o_ref, acc = refs[-2], refs[-1]
        k = pl.program_id(2)

        @pl.when(k == 0)
        def _():
            acc[...] = jnp.zeros_like(acc)

        acc[...] += _dg(a_ref[...], b_ref[...], kind)

        @pl.when(k == nk - 1)
        def _():
            r = acc[...]
            if has_add:
                r = r + refs[2][...]
            o_ref[...] = r.astype(o_ref.dtype)

    return pl.pallas_call(
        body, grid=(m // tm, n // tn, nk),
        in_specs=[a_spec, b_spec] + ([o_spec] if has_add else []),
        out_specs=o_spec, out_shape=jax.ShapeDtypeStruct((m, n), out_dtype),
        scratch_shapes=[pltpu.VMEM((tm, tn), F32)], name=name, compiler_params=_params(3),
    )(*([a, b] + ([add] if has_add else [])))


class _V:
    def __init__(self, arr, w=None, base=0, diff=True):
        self.arr, self.base, self.diff = arr, base, diff
        self.w = arr.shape[1] if w is None else w

    def spec(self, tm):
        return pl.BlockSpec((tm, self.w), lambda i, b=self.base: (i, b))


def _as_views(ins):
    return [v if isinstance(v, _V) else _V(v) for v in ins]


def _tup(r):
    return tuple(r) if isinstance(r, (tuple, list)) else (r,)


def _ew(name, fn, ins, smalls, outs, tm=256):
    ins = _as_views(ins)
    t = ins[0].arr.shape[0]
    tm = min(tm, t)
    n_in = len(ins) + len(smalls)

    def body(*refs):
        res = _tup(fn(*[r[...] for r in refs[:n_in]]))
        for r, o in zip(refs[n_in:], res):
            r[...] = o.astype(r.dtype)

    return pl.pallas_call(
        body, grid=(t // tm,),
        in_specs=[v.spec(tm) for v in ins] + [pl.BlockSpec(s.shape, lambda i: (0, 0)) for s in smalls],
        out_specs=[pl.BlockSpec((tm, w), lambda i: (i, 0)) for w, _ in outs],
        out_shape=[jax.ShapeDtypeStruct((t, w), dt) for w, dt in outs],
        name=name, compiler_params=_params(1),
    )(*[v.arr for v in ins], *smalls)


def _ew_bwd(name, fn, ins, smalls, cts, tm=256, add=None, ct_dtypes=None):
    ins = _as_views(ins)
    t = ins[0].arr.shape[0]
    tm = min(tm, t)
    n_in, n_sm = len(ins), len(smalls)
    diff = [k for k, v in enumerate(ins) if v.diff]
    ct_dtypes = [F32] * len(diff) if ct_dtypes is None else ct_dtypes
    ct_arrs = [c for c in cts if c is not None]
    has_add = add is not None

    def body(*refs):
        vals = [r[...] for r in refs[:n_in]]
        svals = [r[...] for r in refs[n_in:n_in + n_sm]]
        p = n_in + n_sm
        ct_refs = list(refs[p:p + len(ct_arrs)])
        p += len(ct_arrs)
        add_ref = refs[p] if has_add else None
        p += int(has_add)
        din_refs = refs[p:p + len(diff)]
        dsm_refs = refs[p + len(diff):]

        def f(dv, sv):
            full = list(vals)
            for k, d in zip(diff, dv):
                full[k] = d
            return _tup(fn(*full, *sv))

        res, vjp = jax.vjp(f, [vals[k] for k in diff], svals)
        ctv = tuple(jnp.zeros_like(o) if c is None else ct_refs.pop(0)[...].astype(o.dtype) for c, o in zip(cts, res))
        dv, dsv = vjp(ctv)
        for n, (r, d) in enumerate(zip(din_refs, dv)):
            if n == 0 and has_add:
                d = d + add_ref[...]
            r[...] = d.astype(r.dtype)

        @pl.when(pl.program_id(0) == 0)
        def _():
            for r in dsm_refs:
                r[...] = jnp.zeros_like(r)

        for r, d in zip(dsm_refs, dsv):
            r[...] += d

    row = lambda w: pl.BlockSpec((tm, w), lambda i: (i, 0))
    small_specs = [pl.BlockSpec(s.shape, lambda i: (0, 0)) for s in smalls]
    out = pl.pallas_call(
        body, grid=(t // tm,),
        in_specs=[v.spec(tm) for v in ins] + small_specs + [row(c.shape[1]) for c in ct_arrs]
        + ([row(add.shape[1])] if has_add else []),
        out_specs=[row(ins[k].w) for k in diff] + small_specs,
        out_shape=[jax.ShapeDtypeStruct((t, ins[k].w), dt) for k, dt in zip(diff, ct_dtypes)]
        + [jax.ShapeDtypeStruct(s.shape, F32) for s in smalls],
        name=name, compiler_params=_params(1),
    )(*[v.arr for v in ins], *smalls, *ct_arrs, *([add] if has_add else []))
    return out[:len(diff)], out[len(diff):]


BQ = 256
HPB = 2
SUM_TERMS = 2


def _cat(parts):
    return parts[0] if len(parts) == 1 else jnp.concatenate(parts, axis=1)


def _head_view(ref, hh):
    return ref.at[:, hh * HEAD:(hh + 1) * HEAD]


def _attn_specs(qs, ks, t):
    q_specs = [pl.BlockSpec((BQ, HPB * HEAD), lambda h, i: (i, h)) for _ in qs]
    per_head = pl.BlockSpec((t, HPB * HEAD), lambda h, i: (0, h))
    k_specs = [pl.BlockSpec((t, HEAD), lambda h, i: (0, 0)) if sh else per_head for _, sh in ks]
    return q_specs, k_specs, per_head


def _causal_sweep(i, pair, init, diagonal_first):
    order = (lambda s: i - 1 - s) if diagonal_first else (lambda s: s)
    two = lambda s, c: pair(order(2 * s + 1), pair(order(2 * s), c, False), False)
    one = lambda s, c: pair(order(s), c, False)
    carry = pair(i, init, True) if diagonal_first else init
    carry = lax.fori_loop(0, lax.shift_right_logical(i, 1), two, carry)
    carry = lax.fori_loop(i - (i & 1), i, one, carry)
    return carry if diagonal_first else pair(i, carry, True)


def _attn_fwd(name, mode, qs, ks, v):
    t = qs[0].shape[0]
    nq, n = t // BQ, len(qs)
    q_specs, k_specs, per_head = _attn_specs(qs, ks, t)
    shared = [sh for _, sh in ks]

    def body(*refs):
        q_refs, k_refs, v_ref = refs[:n], refs[n:2 * n], refs[2 * n]
        o_ref, st_ref = refs[2 * n + 1], refs[2 * n + 2]
        i = pl.program_id(1)
        row = lax.broadcasted_iota(jnp.int32, (BQ, BQ), 0)
        col = lax.broadcasted_iota(jnp.int32, (BQ, BQ), 1)
        after = _tri(BQ, "lt").astype(BF16)

        def head(hh):
            q = _cat([_head_view(r, hh)[...] for r in q_refs])
            k_h = [kr if sh else _head_view(kr, hh) for kr, sh in zip(k_refs, shared)]
            v_h = _head_view(v_ref, hh)

            def pair(j, carry, masked):
                off = pl.multiple_of(j * BQ, BQ)
                z = _dg(q, _cat([kr[pl.ds(off, BQ), :] for kr in k_h]), "nt")
                vj = v_h[pl.ds(off, BQ), :]
                if mode == "sb":
                    acc, run = carry
                    lsz = _logsig(z)
                    stay = lsz - z
                    if masked:
                        stay = jnp.where(col < row, stay, 0.0)
                    a = jnp.exp(lsz + (run + _xdot(stay, after, "nn", 1, SUM_TERMS)))
                    if masked:
                        a = jnp.where(col < row, a, 0.0)
                    return acc + _dg(a, vj, "nn"), run + jnp.sum(stay, axis=1, keepdims=True)
                m, l, acc = carry
                if masked:
                    z = jnp.where(col <= row, z, -1e30)
                m2 = jnp.maximum(m, jnp.max(z, axis=1, keepdims=True))
                p = jnp.exp(z - m2)
                alpha = jnp.exp(m - m2)
                return m2, alpha * l + jnp.sum(p, axis=1, keepdims=True), alpha * acc + _dg(p, vj, "nn")

            def finish(carry):
                if mode == "sb":
                    acc, run = carry
                    _head_view(o_ref, hh)[...] = acc
                    _head_view(st_ref, hh)[...] = jnp.broadcast_to(run, (BQ, HEAD))
                else:
                    m, l, acc = carry
                    _head_view(o_ref, hh)[...] = acc / l
                    _head_view(st_ref, hh)[...] = jnp.broadcast_to(m + jnp.log(l), (BQ, HEAD))

            zero = jnp.zeros((BQ, 1), F32)
            acc0 = jnp.zeros((BQ, HEAD), F32)
            init = (acc0, zero) if mode == "sb" else (jnp.full((BQ, 1), -1e30, F32), zero, acc0)
            return pair, init, finish

        heads = [head(hh) for hh in range(HPB)]
        both = lambda j, carry, masked: tuple(h[0](j, c, masked) for h, c in zip(heads, carry))
        final = _causal_sweep(i, both, tuple(h[1] for h in heads), diagonal_first=(mode == "sb"))
        for h, c in zip(heads, final):
            h[2](c)

    blk = pl.BlockSpec((BQ, HPB * HEAD), lambda h, i: (i, h))
    return pl.pallas_call(
        body, grid=(N_HEADS // HPB, nq), in_specs=q_specs + k_specs + [per_head], out_specs=[blk, blk],
        out_shape=[jax.ShapeDtypeStruct((t, N_HEADS * HEAD), F32)] * 2, name=name, compiler_params=_params(2),
    )(*qs, *[k for k, _ in ks], v)


def _attn_bwd(name, mode, qs, ks, v, o, stat, do):
    t = qs[0].shape[0]
    nq, n = t // BQ, len(qs)
    q_specs, k_specs, per_head = _attn_specs(qs, ks, t)
    shared = [sh for _, sh in ks]

    def body(*refs):
        q_refs, k_refs, v_ref = refs[:n], refs[n:2 * n], refs[2 * n]
        o_ref, st_ref, do_ref = refs[2 * n + 1:2 * n + 4]
        dq_refs = refs[2 * n + 4:3 * n + 4]
        dk_refs = refs[3 * n + 4:4 * n + 4]
        dv_ref = refs[4 * n + 4]
        g, i = pl.program_id(0), pl.program_id(1)

        @pl.when(i == 0)
        def _():
            dv_ref[...] = jnp.zeros_like(dv_ref)
            for r, sh in zip(dk_refs, shared):
                if not sh:
                    r[...] = jnp.zeros_like(r)

        for r, sh in zip(dk_refs, shared):
            if sh:
                @pl.when((i == 0) & (g == 0))
                def _(r=r):
                    r[...] = jnp.zeros_like(r)

        row = lax.broadcasted_iota(jnp.int32, (BQ, BQ), 0)
        col = lax.broadcasted_iota(jnp.int32, (BQ, BQ), 1)
        upto = _tri(BQ, "ge").astype(BF16)
        before = _tri(BQ, "gt").astype(BF16)

        def head(hh):
            q = _cat([_head_view(r, hh)[...] for r in q_refs])
            k_h = [kr if sh else _head_view(kr, hh) for kr, sh in zip(k_refs, shared)]
            dk_h = [r if sh else _head_view(r, hh) for r, sh in zip(dk_refs, shared)]
            v_h, dv_h = _head_view(v_ref, hh), _head_view(dv_ref, hh)
            do_t = _head_view(do_ref, hh)[...]
            st = _head_view(st_ref, hh)[:, :1]
            if mode == "softmax":
                dsum = jnp.sum(do_t * _head_view(o_ref, hh)[...], axis=1, keepdims=True)

            def pair(j, carry, masked):
                off = pl.multiple_of(j * BQ, BQ)
                kj = _cat([kr[pl.ds(off, BQ), :] for kr in k_h])
                z = _dg(q, kj, "nt")
                da = _dg(do_t, v_h[pl.ds(off, BQ), :], "nt")
                if mode == "sb":
                    dq, pre, gpre = carry
                    lsz = _logsig(z)
                    stay = lsz - z
                    if masked:
                        stay = jnp.where(col < row, stay, 0.0)
                    a = jnp.exp(lsz + (st - (pre + _xdot(stay, upto, "nn", 1, SUM_TERMS))))
                    if masked:
                        a = jnp.where(col < row, a, 0.0)
                    gr = a * da
                    sig = jnp.exp(lsz)
                    dz = gr * (1.0 - sig) - sig * (gpre + _xdot(gr, before, "nn", 1, SUM_TERMS))
                    if masked:
                        dz = jnp.where(col < row, dz, 0.0)
                    tail = (pre + jnp.sum(stay, axis=1, keepdims=True), gpre + jnp.sum(gr, axis=1, keepdims=True))
                else:
                    dq = carry[0]
                    a = jnp.exp(z - st)
                    if masked:
                        a = jnp.where(col <= row, a, 0.0)
                    dz = a * (da - dsum)
                    tail = ()
                dk = _dg(dz, q, "tn")
                for p, r in enumerate(dk_h):
                    r[pl.ds(off, BQ), :] += dk[:, p * HEAD:(p + 1) * HEAD]
                dv_h[pl.ds(off, BQ), :] += _dg(a, do_t, "tn")
                return (dq + _dg(dz, kj, "nn"),) + tail

            def finish(carry):
                for p, r in enumerate(dq_refs):
                    _head_view(r, hh)[...] = carry[0][:, p * HEAD:(p + 1) * HEAD]

            zero = jnp.zeros((BQ, 1), F32)
            init = (jnp.zeros((BQ, n * HEAD), F32),) + ((zero, zero) if mode == "sb" else ())
            return pair, init, finish

        heads = [head(hh) for hh in range(HPB)]
        both = lambda j, carry, masked: tuple(h[0](j, c, masked) for h, c in zip(heads, carry))
        final = _causal_sweep(i, both, tuple(h[1] for h in heads), diagonal_first=False)
        for h, c in zip(heads, final):
            h[2](c)

    blk = pl.BlockSpec((BQ, HPB * HEAD), lambda h, i: (i, h))
    dk_specs = [pl.BlockSpec((t, HEAD), lambda h, i: (0, 0)) if sh else per_head for sh in shared]
    wide = jax.ShapeDtypeStruct((t, N_HEADS * HEAD), F32)
    out = pl.pallas_call(
        body, grid=(N_HEADS // HPB, nq), in_specs=q_specs + k_specs + [per_head, blk, blk, blk],
        out_specs=[blk] * n + dk_specs + [per_head],
        out_shape=[wide] * n + [jax.ShapeDtypeStruct((t, HEAD), F32) if sh else wide for sh in shared] + [wide],
        name=name, compiler_params=_params(2),
    )(*qs, *[k for k, _ in ks], v, o, stat, do)
    return out[:n], out[n:2 * n], out[2 * n]


@jax.custom_vjp
def _given_inverse(lower, tinv):
    return tinv


def _given_inverse_fwd(lower, tinv):
    return tinv, tinv


def _given_inverse_bwd(tinv, ct):
    return -_dot3(_dot3(tinv, ct, "tn"), tinv, "nt"), jnp.zeros_like(tinv)


_given_inverse.defvjp(_given_inverse_fwd, _given_inverse_bwd)


def _dn_chunk(q, k, v, g, beta, state, tinv=None):
    c = q.shape[-2]
    gc = cumsum_rows(g)
    gcc = gc[..., :c]
    diff = gcc - jnp.swapaxes(gcc, -1, -2)
    causal, strict = _tri(c, "le"), _tri(c, "lt")
    decay = jnp.where(causal, jnp.exp(jnp.where(causal, diff, 0.0)), 0.0)
    kb = k * beta
    lower = jnp.where(strict, bdot(kb, k, "nt") * decay, 0.0)
    tinv = _unit_lower_inverse(lower) if tinv is None else _given_inverse(lower, tinv)
    eg = jnp.exp(gc)
    u = _hdot(tinv, v * beta)
    w = _hdot(tinv, kb * eg)
    attn = bdot(q, k, "nt") * decay
    glast = gc[..., c - 1:c, :]
    v_new = u - bdot(w, state, "nn")
    o = bdot(q * eg, state, "nn") + bdot(attn, v_new, "nn")
    new_state = state * jnp.exp(glast) + bdot(k * jnp.exp(glast - gc), v_new, "tn")
    return o, new_state, tinv


def _stack_heads(ref):
    return jnp.stack([ref[:, h * HEAD:(h + 1) * HEAD] for h in range(N_HEADS)])


def _store_heads(ref, val):
    for h in range(N_HEADS):
        ref[:, h * HEAD:(h + 1) * HEAD] = val[h]


def _dn_fwd(name, q, k, v, g, beta):
    t = q.shape[0]
    nc = t // DN_CHUNK
    wide = N_HEADS * HEAD
    blk = pl.BlockSpec((DN_CHUNK, wide), lambda n: (n, 0))
    st_spec = pl.BlockSpec((N_HEADS, None, HEAD, HEAD), lambda n: (0, n, 0, 0))
    inv_spec = pl.BlockSpec((N_HEADS, None, DN_CHUNK, DN_CHUNK), lambda n: (0, n, 0, 0))

    def body(q_ref, k_ref, v_ref, g_ref, b_ref, o_ref, st_ref, inv_ref, state):
        @pl.when(pl.program_id(0) == 0)
        def _():
            state[...] = jnp.zeros_like(state)

        s_in = state[...]
        st_ref[...] = s_in
        o, s_out, tinv = _dn_chunk(*[_stack_heads(r) for r in (q_ref, k_ref, v_ref, g_ref, b_ref)], s_in)
        _store_heads(o_ref, o)
        inv_ref[...] = tinv
        state[...] = s_out

    return pl.pallas_call(
        body, grid=(nc,), in_specs=[blk] * 5, out_specs=[blk, st_spec, inv_spec],
        out_shape=[jax.ShapeDtypeStruct((t, wide), F32), jax.ShapeDtypeStruct((N_HEADS, nc, HEAD, HEAD), F32),
                   jax.ShapeDtypeStruct((N_HEADS, nc, DN_CHUNK, DN_CHUNK), F32)],
        scratch_shapes=[pltpu.VMEM((N_HEADS, HEAD, HEAD), F32)], name=name, compiler_params=_params(1),
    )(q, k, v, g, beta)


def _dn_bwd(name, q, k, v, g, beta, states, inverses, do):
    t = q.shape[0]
    nc = t // DN_CHUNK
    wide = N_HEADS * HEAD
    blk = pl.BlockSpec((DN_CHUNK, wide), lambda n: (nc - 1 - n, 0))
    st_spec = pl.BlockSpec((N_HEADS, None, HEAD, HEAD), lambda n: (0, nc - 1 - n, 0, 0))
    inv_spec = pl.BlockSpec((N_HEADS, None, DN_CHUNK, DN_CHUNK), lambda n: (0, nc - 1 - n, 0, 0))

    def body(q_ref, k_ref, v_ref, g_ref, b_ref, st_ref, inv_ref, do_ref, dq_ref, dk_ref, dv_ref, dg_ref, db_ref, dstate):
        @pl.when(pl.program_id(0) == 0)
        def _():
            dstate[...] = jnp.zeros_like(dstate)

        tinv = inv_ref[...]
        chunk = lambda *args: _dn_chunk(*args, tinv=tinv)[:2]
        _, vjp = jax.vjp(chunk, *[_stack_heads(r) for r in (q_ref, k_ref, v_ref, g_ref, b_ref)], st_ref[...])
        cts = vjp((_stack_heads(do_ref), dstate[...]))
        for r, d in zip((dq_ref, dk_ref, dv_ref, dg_ref, db_ref), cts[:5]):
            _store_heads(r, d)
        dstate[...] = cts[5]

    shape = jax.ShapeDtypeStruct((t, wide), F32)
    return pl.pallas_call(
        body, grid=(nc,), in_specs=[blk] * 5 + [st_spec, inv_spec, blk], out_specs=[blk] * 5, out_shape=[shape] * 5,
        scratch_shapes=[pltpu.VMEM((N_HEADS, HEAD, HEAD), F32)], name=name, compiler_params=_params(1),
    )(q, k, v, g, beta, states, inverses, do)


CONV_W = 1024
HALO = 8


def _shift_down(cur, prev, s):
    sh = pltpu.roll(cur, s, 0)
    ph = pltpu.roll(prev, s, 0)
    r = lax.broadcasted_iota(jnp.int32, (HALO, cur.shape[1]), 0)
    return jnp.concatenate([jnp.where(r < s, ph, sh[:HALO]), sh[HALO:]], axis=0)


def _shift_up(cur, nxt, s):
    tm = cur.shape[0]
    sh = pltpu.roll(cur, tm - s, 0)
    nh = pltpu.roll(nxt, HALO - s, 0)
    r = lax.broadcasted_iota(jnp.int32, (HALO, cur.shape[1]), 0)
    return jnp.concatenate([sh[:tm - HALO], jnp.where(r >= HALO - s, nh, sh[tm - HALO:])], axis=0)


def _conv_fwd(name, proj, w, tm=256):
    t = proj.shape[0]
    tm = min(tm, t)
    width = w.shape[1]
    per = tm // HALO

    def body(cur_ref, prev_ref, w_ref, y_ref):
        cur = cur_ref[...]
        prev = jnp.where(pl.program_id(0) > 0, prev_ref[...], 0.0)
        y = cur * w_ref[3:4, :]
        for s in (1, 2, 3):
            y = y + _shift_down(cur, prev, s) * w_ref[3 - s:4 - s, :]
        y_ref[...] = y

    return pl.pallas_call(
        body, grid=(t // tm, width // CONV_W),
        in_specs=[pl.BlockSpec((tm, CONV_W), lambda i, c: (i, c)),
                  pl.BlockSpec((HALO, CONV_W), lambda i, c: (jnp.maximum(i * per - 1, 0), c)),
                  pl.BlockSpec((HALO, CONV_W), lambda i, c: (0, c))],
        out_specs=pl.BlockSpec((tm, CONV_W), lambda i, c: (i, c)),
        out_shape=jax.ShapeDtypeStruct((t, width), F32), name=name, compiler_params=_params(2),
    )(proj, proj, w)


def _conv_bwd(name, proj, w, dy, tm=256):
    t = proj.shape[0]
    tm = min(tm, t)
    width = w.shape[1]
    per, nt = tm // HALO, t // tm

    def body(cur_ref, prev_ref, w_ref, dy_ref, nxt_ref, du_ref, dw_ref):
        i = pl.program_id(1)
        cur, dy_t = cur_ref[...], dy_ref[...]
        prev = jnp.where(i > 0, prev_ref[...], 0.0)
        nxt = jnp.where(i < nt - 1, nxt_ref[...], 0.0)
        du = dy_t * w_ref[3:4, :]
        rows = [jnp.sum(dy_t * cur, axis=0, keepdims=True)]
        for s in (1, 2, 3):
            du = du + _shift_up(dy_t, nxt, s) * w_ref[3 - s:4 - s, :]
            rows.insert(0, jnp.sum(dy_t * _shift_down(cur, prev, s), axis=0, keepdims=True))
        du_ref[...] = du.astype(du_ref.dtype)

        @pl.when(i == 0)
        def _():
            dw_ref[...] = jnp.zeros_like(dw_ref)

        dw_ref[...] += jnp.concatenate(rows + [jnp.zeros((HALO - 4, CONV_W), F32)], axis=0)

    return pl.pallas_call(
        body, grid=(width // CONV_W, nt),
        in_specs=[pl.BlockSpec((tm, CONV_W), lambda c, i: (i, c)),
                  pl.BlockSpec((HALO, CONV_W), lambda c, i: (jnp.maximum(i * per - 1, 0), c)),
                  pl.BlockSpec((HALO, CONV_W), lambda c, i: (0, c)),
                  pl.BlockSpec((tm, CONV_W), lambda c, i: (i, c)),
                  pl.BlockSpec((HALO, CONV_W), lambda c, i: (jnp.minimum((i + 1) * per, t // HALO - 1), c))],
        out_specs=[pl.BlockSpec((tm, CONV_W), lambda c, i: (i, c)), pl.BlockSpec((HALO, CONV_W), lambda c, i: (0, c))],
        out_shape=[jax.ShapeDtypeStruct((t, width), BF16), jax.ShapeDtypeStruct((HALO, width), F32)],
        name=name, compiler_params=_params(2),
    )(proj, proj, w, dy, dy)


def _norm_fn(x, g):
    return _rms(x, g)


def _swiglu_fn(gu):
    return _silu(gu[:, :FFN_HIDDEN]) * gu[:, FFN_HIDDEN:]


def _heads(x):
    return [x[:, h * HEAD:(h + 1) * HEAD] for h in range(x.shape[1] // HEAD)]


def _dn_pre_fn(c, ab, a_log, dt_bias):
    w = N_HEADS * HEAD
    q = [_l2(_silu(x)) * (HEAD ** -0.5) for x in _heads(c[:, :w])]
    k = [_l2(_silu(x)) for x in _heads(c[:, w:2 * w])]
    v = _silu(c[:, 2 * w:])
    g, beta = [], []
    for h in range(N_HEADS):
        gh = -jnp.exp(a_log[:, h:h + 1]) * jax.nn.softplus(ab[:, h:h + 1] + dt_bias[:, h:h + 1])
        bh = jax.nn.sigmoid(ab[:, N_HEADS + h:N_HEADS + h + 1])
        g.append(jnp.broadcast_to(gh, (c.shape[0], HEAD)))
        beta.append(jnp.broadcast_to(bh, (c.shape[0], HEAD)))
    cat = lambda xs: jnp.concatenate(xs, axis=1)
    return cat(q), cat(k), v, cat(g), cat(beta)


def _dn_post_fn(o, z, out_norm):
    return jnp.concatenate([_rms(oh, out_norm) * _silu(zh) for oh, zh in zip(_heads(o), _heads(z))], axis=1)


def _sb_pre_fn(qkv, q_norm, k_norm):
    w = N_HEADS * HEAD
    q = [_rms(x, q_norm) * (HEAD ** -0.5) for x in _heads(qkv[:, :w])]
    k = [_rms(x, k_norm) for x in _heads(qkv[:, w:2 * w])]
    return jnp.concatenate(q, axis=1), jnp.concatenate(k, axis=1), qkv[:, 2 * w:]


def _mla_a_fn(down, cos, sin_lo, sin_hi, q_a_norm, kv_a_norm, k_rope_norm):
    cq = _rms(down[:, :256], q_a_norm)
    ckv = _rms(down[:, 256:384], kv_a_norm)
    kr = _rope(_rms(down[:, 384:], k_rope_norm, MLA_ROPE), cos, sin_lo, sin_hi)
    return cq, ckv, kr


def _mla_b_fn(qf, kvf, cos, sin_lo, sin_hi, q_nope_norm, q_rope_norm, k_nope_norm):
    scale = MLA_QK ** -0.5
    qn, qr, kn, v = [], [], [], []
    for h in range(N_HEADS):
        a = 2 * h * HEAD
        qn.append(_rms(qf[:, a:a + HEAD], q_nope_norm) * scale)
        qr.append(_rope(_rms(qf[:, a + HEAD:a + 2 * HEAD], q_rope_norm, MLA_ROPE), cos, sin_lo, sin_hi) * scale)
        kn.append(_rms(kvf[:, a:a + HEAD], k_nope_norm))
        v.append(kvf[:, a + HEAD:a + 2 * HEAD])
    cat = lambda xs: jnp.concatenate(xs, axis=1)
    return cat(qn), cat(qr), cat(kn), cat(v)


def _rope_tables(t):
    inv_freq = ROPE_THETA ** (-jnp.arange(0, MLA_ROPE, 2, dtype=F32) / MLA_ROPE)
    ang = jnp.arange(t, dtype=F32)[:, None] * inv_freq[None, :]
    cos, sin, zero = jnp.cos(ang), jnp.sin(ang), jnp.zeros((t, MLA_ROPE // 2), F32)
    cat = lambda xs: jnp.concatenate(xs, axis=1)
    return cat([cos, cos, zero, zero]), cat([-sin, zero, zero, zero]), cat([zero, sin, zero, zero])


def _row(v, width=None):
    width = v.shape[0] if width is None else width
    return jnp.pad(v.astype(F32), (0, width - v.shape[0])).reshape(1, width)


def _loss_kernel(y, target):
    t, d = y.shape
    tm = min(256, t)

    def body(y_ref, t_ref, part_ref, dy_ref):
        e = y_ref[...] - t_ref[...]
        dy_ref[...] = e * (1.0 / d)

        @pl.when(pl.program_id(0) == 0)
        def _():
            part_ref[...] = jnp.zeros_like(part_ref)

        part_ref[...] += jnp.sum(e * e, axis=0, keepdims=True)

    blk = pl.BlockSpec((tm, d), lambda i: (i, 0))
    one = pl.BlockSpec((1, d), lambda i: (0, 0))
    return pl.pallas_call(body, grid=(t // tm,), in_specs=[blk, blk], out_specs=[one, blk],
                          out_shape=[jax.ShapeDtypeStruct((1, d), F32), jax.ShapeDtypeStruct((t, d), F32)],
                          name="loss", compiler_params=_params(1))(y, target)


def _ffn_fwd(p, x, w, sm):
    h, = _ew(p + "ffn_norm", _norm_fn, [x], [sm["ffn_norm"]], [(D_MODEL, BF16)])
    gu = _mm(p + "ffn_gu", h, w["ffn_w_gate_up"], "nn")
    act, = _ew(p + "ffn_act", _swiglu_fn, [gu], [], [(FFN_HIDDEN, BF16)], tm=128)
    y = _mm(p + "ffn_down", act, w["ffn_w_down"], "nn", add=x)
    return y, (x, h, gu, act)


def _ffn_bwd(p, saved, dy, w, sm, grads):
    x, h, gu, act = saved
    dact = _mm(p + "ffn_down_dx", dy, w["ffn_w_down"], "nt")
    grads["ffn_w_down"] = _mm(p + "ffn_down_dw", act, dy, "tn")
    (dgu,), _ = _ew_bwd(p + "ffn_act_bwd", _swiglu_fn, [gu], [], [dact], tm=128, ct_dtypes=[BF16])
    dh = _mm(p + "ffn_gu_dx", dgu, w["ffn_w_gate_up"], "nt")
    grads["ffn_w_gate_up"] = _mm(p + "ffn_gu_dw", h, dgu, "tn")
    (dx,), (dg,) = _ew_bwd(p + "ffn_norm_bwd", _norm_fn, [x], [sm["ffn_norm"]], [dh], add=dy)
    grads["ffn_norm"] = dg
    return dx


def _dn_layer_fwd(p, x, w, sm):
    h, = _ew(p + "mix_norm", _norm_fn, [x], [sm["mix_norm"]], [(D_MODEL, BF16)])
    proj = _mm(p + "dn_in", h, w["dn_w_in"], "nn")
    conv = _conv_fwd(p + "dn_conv", proj, sm["dn_conv_w"])
    ab = _V(proj, LANES, 4 * N_HEADS)
    wide = N_HEADS * HEAD
    q, k, v, g, beta = _ew(p + "dn_pre", _dn_pre_fn, [conv, ab], [sm["dn_a_log"], sm["dn_dt_bias"]], [(wide, F32)] * 5)
    o, states, inverses = _dn_fwd(p + "dn_core", q, k, v, g, beta)
    z = _V(proj, wide, 3)
    on, = _ew(p + "dn_post", _dn_post_fn, [o, z], [sm["dn_out_norm"]], [(wide, BF16)])
    y = _mm(p + "dn_out", on, w["dn_w_out"], "nn", add=x)
    return y, (x, h, proj, conv, q, k, v, g, beta, o, states, inverses, on)


def _dn_layer_bwd(p, saved, dy, w, sm, grads):
    x, h, proj, conv, q, k, v, g, beta, o, states, inverses, on = saved
    wide = N_HEADS * HEAD
    don = _mm(p + "dn_out_dx", dy, w["dn_w_out"], "nt")
    grads["dn_w_out"] = _mm(p + "dn_out_dw", on, dy, "tn")
    (do, dz), (d_out_norm,) = _ew_bwd(p + "dn_post_bwd", _dn_post_fn, [o, _V(proj, wide, 3)], [sm["dn_out_norm"]], [don],
                                          ct_dtypes=[F32, BF16])
    grads["dn_out_norm"] = d_out_norm
    dq, dk, dv, dg, db = _dn_bwd(p + "dn_core_bwd", q, k, v, g, beta, states, inverses, do)
    (dconv, dab), (d_a_log, d_dt) = _ew_bwd(p + "dn_pre_bwd", _dn_pre_fn, [conv, _V(proj, LANES, 4 * N_HEADS)],
                                            [sm["dn_a_log"], sm["dn_dt_bias"]], [dq, dk, dv, dg, db], ct_dtypes=[F32, BF16])
    grads["dn_a_log"], grads["dn_dt_bias"] = d_a_log, d_dt
    dqkv, dconv_w = _conv_bwd(p + "dn_conv_bwd", proj, sm["dn_conv_w"], dconv)
    grads["dn_conv_w"] = dconv_w
    dproj = jnp.concatenate([dqkv, dz, dab], axis=1)
    dh = _mm(p + "dn_in_dx", dproj, w["dn_w_in"], "nt")
    grads["dn_w_in"] = _mm(p + "dn_in_dw", h, dproj, "tn")
    (dx,), (dgain,) = _ew_bwd(p + "mix_norm_bwd", _norm_fn, [x], [sm["mix_norm"]], [dh], add=dy)
    grads["mix_norm"] = dgain
    return dx


def _sb_layer_fwd(p, x, w, sm):
    h, = _ew(p + "mix_norm", _norm_fn, [x], [sm["mix_norm"]], [(D_MODEL, BF16)])
    qkv = _mm(p + "sb_qkv", h, w["sb_w_qkv"], "nn")
    wide = N_HEADS * HEAD
    q, k, v = _ew(p + "sb_pre", _sb_pre_fn, [qkv], [sm["sb_q_norm"], sm["sb_k_norm"]], [(wide, BF16)] * 3)
    o, stat = _attn_fwd(p + "sb_core", "sb", [q], [(k, False)], v)
    y = _mm(p + "sb_out", o, w["sb_w_out"], "nn", add=x)
    return y, (x, h, qkv, q, k, v, o, stat)


def _sb_layer_bwd(p, saved, dy, w, sm, grads):
    x, h, qkv, q, k, v, o, stat = saved
    do = _mm(p + "sb_out_dx", dy, w["sb_w_out"], "nt")
    grads["sb_w_out"] = _mm(p + "sb_out_dw", o, dy, "tn")
    (dq,), (dk,), dv = _attn_bwd(p + "sb_core_bwd", "sb", [q], [(k, False)], v, o, stat, do)
    (dqkv,), (dqn, dkn) = _ew_bwd(p + "sb_pre_bwd", _sb_pre_fn, [qkv], [sm["sb_q_norm"], sm["sb_k_norm"]], [dq, dk, dv],
                                  ct_dtypes=[BF16])
    grads["sb_q_norm"], grads["sb_k_norm"] = dqn, dkn
    dh = _mm(p + "sb_qkv_dx", dqkv, w["sb_w_qkv"], "nt")
    grads["sb_w_qkv"] = _mm(p + "sb_qkv_dw", h, dqkv, "tn")
    (dx,), (dgain,) = _ew_bwd(p + "mix_norm_bwd", _norm_fn, [x], [sm["mix_norm"]], [dh], add=dy)
    grads["mix_norm"] = dgain
    return dx


def _mla_layer_fwd(p, x, w, sm):
    t = x.shape[0]
    tabs = [_V(a, diff=False) for a in _rope_tables(t)]
    h, = _ew(p + "mix_norm", _norm_fn, [x], [sm["mix_norm"]], [(D_MODEL, BF16)])
    down = _mm(p + "mla_down", h, w["mla_w_down"], "nn")
    sm_a = [sm["mla_q_a_norm"], sm["mla_kv_a_norm"], sm["mla_k_rope_norm"]]
    cq, ckv, kr = _ew(p + "mla_a", _mla_a_fn, [down] + tabs, sm_a, [(256, BF16), (128, BF16), (128, BF16)])
    qf = _mm(p + "mla_uq", cq, w["mla_w_uq"], "nn")
    kvf = _mm(p + "mla_ukv", ckv, w["mla_w_ukv"], "nn")
    sm_b = [sm["mla_q_nope_norm"], sm["mla_q_rope_norm"], sm["mla_k_nope_norm"]]
    wide = N_HEADS * HEAD
    qn, qr, kn, v = _ew(p + "mla_b", _mla_b_fn, [qf, kvf] + tabs, sm_b, [(wide, BF16)] * 4)
    o, stat = _attn_fwd(p + "mla_core", "softmax", [qn, qr], [(kn, False), (kr, True)], v)
    y = _mm(p + "mla_out", o, w["mla_w_out"], "nn", add=x)
    return y, (x, h, down, cq, ckv, kr, qf, kvf, qn, qr, kn, v, o, stat)


def _mla_layer_bwd(p, saved, dy, w, sm, grads):
    x, h, down, cq, ckv, kr, qf, kvf, qn, qr, kn, v, o, stat = saved
    tabs = [_V(a, diff=False) for a in _rope_tables(x.shape[0])]
    do = _mm(p + "mla_out_dx", dy, w["mla_w_out"], "nt")
    grads["mla_w_out"] = _mm(p + "mla_out_dw", o, dy, "tn")
    (dqn, dqr), (dkn, dkr), dv = _attn_bwd(p + "mla_core_bwd", "softmax", [qn, qr], [(kn, False), (kr, True)],
                                           v, o, stat, do)
    sm_b = [sm["mla_q_nope_norm"], sm["mla_q_rope_norm"], sm["mla_k_nope_norm"]]
    (dqf, dkvf), dsm_b = _ew_bwd(p + "mla_b_bwd", _mla_b_fn, [qf, kvf] + tabs, sm_b, [dqn, dqr, dkn, dv],
                                 ct_dtypes=[BF16, BF16])
    grads["mla_q_nope_norm"], grads["mla_q_rope_norm"], grads["mla_k_nope_norm"] = dsm_b
    dcq = _mm(p + "mla_uq_dx", dqf, w["mla_w_uq"], "nt")
    grads["mla_w_uq"] = _mm(p + "mla_uq_dw", cq, dqf, "tn")
    dckv = _mm(p + "mla_ukv_dx", dkvf, w["mla_w_ukv"], "nt")
    grads["mla_w_ukv"] = _mm(p + "mla_ukv_dw", ckv, dkvf, "tn")
    sm_a = [sm["mla_q_a_norm"], sm["mla_kv_a_norm"], sm["mla_k_rope_norm"]]
    (ddown,), dsm_a = _ew_bwd(p + "mla_a_bwd", _mla_a_fn, [down] + tabs, sm_a, [dcq, dckv, dkr], ct_dtypes=[BF16])
    grads["mla_q_a_norm"], grads["mla_kv_a_norm"], grads["mla_k_rope_norm"] = dsm_a
    dh = _mm(p + "mla_down_dx", ddown, w["mla_w_down"], "nt")
    grads["mla_w_down"] = _mm(p + "mla_down_dw", h, ddown, "tn")
    (dx,), (dgain,) = _ew_bwd(p + "mix_norm_bwd", _norm_fn, [x], [sm["mix_norm"]], [dh], add=dy)
    grads["mix_norm"] = dgain
    return dx


_MIX_FWD = (_dn_layer_fwd, _sb_layer_fwd, _mla_layer_fwd)
_MIX_BWD = (_dn_layer_bwd, _sb_layer_bwd, _mla_layer_bwd)


def _pad_cols(a, n):
    return jnp.pad(a, ((0, 0), (0, n - a.shape[1])))


def _prep_big(name, a):
    if name.endswith("dn_w_in"):
        return _pad_cols(a, 4 * N_HEADS * HEAD + LANES)
    if name.endswith("mla_w_down"):
        return _pad_cols(a, 512)
    if name.endswith("mla_w_uq"):
        a3 = a.reshape(a.shape[0], N_HEADS, MLA_QK)
        return jnp.pad(a3, ((0, 0), (0, 0), (0, 2 * HEAD - MLA_QK))).reshape(a.shape[0], N_HEADS * 2 * HEAD)
    return a


def _unprep_big(name, g):
    if name.endswith("dn_w_in"):
        return g[:, :4 * N_HEADS * HEAD + 2 * N_HEADS]
    if name.endswith("mla_w_down"):
        return g[:, :448]
    if name.endswith("mla_w_uq"):
        return g.reshape(g.shape[0], N_HEADS, 2 * HEAD)[:, :, :MLA_QK].reshape(g.shape[0], N_HEADS * MLA_QK)
    return g


def _prep_small(name, a):
    if name.endswith("dn_conv_w"):
        return jnp.pad(a.astype(F32), ((0, HALO - a.shape[0]), (0, 0)))
    if name.endswith(("dn_a_log", "dn_dt_bias", "mla_q_rope_norm", "mla_k_rope_norm")):
        return _row(a, LANES)
    return _row(a)


def _unprep_small(name, g, like):
    if name.endswith("dn_conv_w"):
        return g[:like.shape[0]]
    return g.reshape(-1)[:like.shape[0]]


def local_step(x, target, big, small):
    layers = []
    for i in range(DEPTH):
        p = "l%d_" % i
        w = {n: _prep_big(n, big[p + n]) for n in _MIXERS[i % 3] + ("ffn_w_gate_up", "ffn_w_down") if p + n in big}
        sm = {n: _prep_small(n, small[p + n]) for n in _MIXERS[i % 3] + ("mix_norm", "ffn_norm") if p + n in small}
        layers.append((p, w, sm))
    saved = []
    for i, (p, w, sm) in enumerate(layers):
        x, s_mix = _MIX_FWD[i % 3](p, x, w, sm)
        x, s_ffn = _ffn_fwd(p, x, w, sm)
        saved.append((s_mix, s_ffn))
    part, dx = _loss_kernel(x, target)
    grads = {}
    for i in reversed(range(DEPTH)):
        p, w, sm = layers[i]
        g = {}
        dx = _ffn_bwd(p, saved[i][1], dx, w, sm, g)
        dx = _MIX_BWD[i % 3](p, saved[i][0], dx, w, sm, g)
        for n, val in g.items():
            grads[p + n] = _unprep_big(p + n, val) if p + n in big else _unprep_small(p + n, val, small[p + n])
    return part, dx, grads


ROW = 1024
BIG_ROWS = 1024


PACK_ALIGN = 16


def _packed_rows(shape):
    return -(-math.prod(shape) // (ROW * PACK_ALIGN)) * PACK_ALIGN


def _as_rows(a, lead=()):
    rows = _packed_rows(a.shape[len(lead):])
    flat = a.reshape(lead + (-1,))
    return jnp.pad(flat, ((0, 0),) * len(lead) + ((0, rows * ROW - flat.shape[-1]),)).reshape(lead + (rows, ROW))


def _pack(arrs, rows_multiple, lead=()):
    blocks = [_as_rows(a, lead) for a in arrs]
    used = sum(b.shape[-2] for b in blocks)
    fill = -(-used // rows_multiple) * rows_multiple - used
    if fill:
        blocks.append(jnp.zeros(lead + (fill, ROW), blocks[0].dtype))
    return jnp.concatenate(blocks, axis=len(lead))


def _unpack(buf, shapes, lead=()):
    out, r0 = [], 0
    for s in shapes:
        rows, n = _packed_rows(s), math.prod(s)
        block = lax.slice_in_dim(buf, r0, r0 + rows, axis=len(lead))
        out.append(block.reshape(lead + (-1,))[..., :n].reshape(lead + tuple(s)))
        r0 += rows
    return out


def _me():
    return lax.axis_index("x"), lax.axis_index("y"), lax.axis_index("c")


def _other_chips(x, y):
    return [(1 - x, y), (x, 1 - y), (1 - x, 1 - y)]


HBM = pl.BlockSpec(memory_space=pl.ANY)


OWN_STREAMS = 4


def _gather_shards(packed):
    rows = packed.shape[0]
    half = rows // 2

    def body(x_ref, out_ref, send_sems, recv_sems):
        x, y, c = _me()
        sibling, chips = (x, y, 1 - c), _other_chips(x, y)

        def part(px, py, pc):
            return out_ref.at[2 * px + py, pl.ds(pl.multiple_of(pc * half, 16), half), :]

        def copy(k, block, to, src=None):
            return pltpu.make_async_remote_copy(
                src_ref=part(*block) if src is None else src, dst_ref=part(*block),
                send_sem=send_sems.at[k], recv_sem=recv_sems.at[k], device_id=to, device_id_type=MESH)

        my_half = x_ref.at[pl.ds(pl.multiple_of(c * half, 16), half), :]
        first = [copy(j, (x, y, c), (*chip, c), src=my_half) for j, chip in enumerate(chips)]
        piece = rows // OWN_STREAMS
        for p in range(OWN_STREAMS):
            rows_p = pl.ds(p * piece, piece)
            first.append(pltpu.make_async_remote_copy(
                src_ref=x_ref.at[rows_p, :], dst_ref=out_ref.at[2 * x + y, rows_p, :], send_sem=send_sems.at[6 + p],
                recv_sem=recv_sems.at[6 + p], device_id=sibling, device_id_type=MESH))
        for cp in first:
            cp.start()
        passed = [copy(3 + j, (*chip, c), sibling) for j, chip in enumerate(chips)]
        for j, chip in enumerate(chips):
            copy(j, (*chip, c), (x, y, c)).wait_recv()
            passed[j].start()
        for j, chip in enumerate(chips):
            copy(3 + j, (*chip, 1 - c), (x, y, c)).wait_recv()
        for cp in first[3:]:
            cp.wait_recv()
        for cp in first + passed:
            cp.wait_send()

    n_sems = 6 + OWN_STREAMS
    return pl.pallas_call(
        body, out_shape=jax.ShapeDtypeStruct((N_CHIPS,) + packed.shape, packed.dtype), in_specs=[HBM], out_specs=HBM,
        scratch_shapes=[pltpu.SemaphoreType.DMA((n_sems,)), pltpu.SemaphoreType.DMA((n_sems,))],
        name="gather_weights",
    )(packed)


D2D_STREAMS = 16


def _swap_halves(g):
    n, rows, _ = g.shape
    half = rows // 2
    per = D2D_STREAMS // n
    piece = half // per

    def body(g_ref, theirs_ref, send_sems, recv_sems):
        x, y, c = _me()
        give = (1 - c) * half
        copies = []
        for j in range(n):
            for p in range(per):
                k = j * per + p
                cp = pltpu.make_async_remote_copy(
                    src_ref=g_ref.at[j, pl.ds(pl.multiple_of(give + p * piece, 8), piece), :],
                    dst_ref=theirs_ref.at[j, pl.ds(p * piece, piece), :],
                    send_sem=send_sems.at[k], recv_sem=recv_sems.at[k], device_id=(x, y, 1 - c), device_id_type=MESH)
                cp.start()
                copies.append(cp)
        for cp in copies:
            cp.wait()

    return pl.pallas_call(
        body, out_shape=jax.ShapeDtypeStruct((n, half, ROW), g.dtype), in_specs=[HBM], out_specs=HBM,
        scratch_shapes=[pltpu.SemaphoreType.DMA((D2D_STREAMS,)), pltpu.SemaphoreType.DMA((D2D_STREAMS,))],
        name="grad_swap_halves",
    )(g)


def _pair_sum(g, theirs, c):
    n, half, _ = theirs.shape
    tm = _tile(half)
    nb = half // tm

    def body(c_ref, g_ref, t_ref, s32_ref, s16_ref):
        s = g_ref[...] + t_ref[...]
        s32_ref[...] = s
        s16_ref[...] = s.astype(BF16)

    blk = pl.BlockSpec((None, tm, ROW), lambda j, i, c_ref: (j, i, 0))
    return pl.pallas_call(
        body,
        grid_spec=pltpu.PrefetchScalarGridSpec(
            num_scalar_prefetch=1, grid=(n, nb),
            in_specs=[pl.BlockSpec((None, tm, ROW), lambda j, i, c_ref: (j, c_ref[0] * nb + i, 0)), blk],
            out_specs=[blk, blk]),
        out_shape=[jax.ShapeDtypeStruct(theirs.shape, F32), jax.ShapeDtypeStruct(theirs.shape, BF16)],
        name="grad_pair_sum", compiler_params=_params(2),
    )(c.reshape(1).astype(jnp.int32), g, theirs)


def _scatter_chunks(s16):
    _, half, _ = s16.shape

    def body(s16_ref, got_ref, send_sems, recv_sems):
        x, y, c = _me()
        sends = []
        for j, (px, py) in enumerate(_other_chips(x, y)):
            cp = pltpu.make_async_remote_copy(src_ref=s16_ref.at[2 * px + py], dst_ref=got_ref.at[j],
                                              send_sem=send_sems.at[j], recv_sem=recv_sems.at[j],
                                              device_id=(px, py, c), device_id_type=MESH)
            cp.start()
            sends.append(cp)
        for cp in sends:
            cp.wait()

    return pl.pallas_call(
        body, out_shape=jax.ShapeDtypeStruct((3, half, ROW), BF16), in_specs=[HBM], out_specs=HBM,
        scratch_shapes=[pltpu.SemaphoreType.DMA((3,)), pltpu.SemaphoreType.DMA((3,))],
        name="grad_scatter",
    )(s16)


def _chip_sum(s32, got, chip, c):
    _, half, _ = s32.shape
    tm = _tile(half)
    nb = half // tm

    def body(where_ref, own_ref, g0_ref, g1_ref, g2_ref, o_ref):
        o_ref[...] = ((own_ref[...] + g0_ref[...].astype(F32)) + g1_ref[...].astype(F32)) + g2_ref[...].astype(F32)

    got_spec = lambda k: pl.BlockSpec((None, tm, ROW), lambda i, where_ref, k=k: (k, i, 0))
    return pl.pallas_call(
        body,
        grid_spec=pltpu.PrefetchScalarGridSpec(
            num_scalar_prefetch=1, grid=(nb,),
            in_specs=[pl.BlockSpec((None, tm, ROW), lambda i, where_ref: (where_ref[0], i, 0)), got_spec(0), got_spec(1), got_spec(2)],
            out_specs=pl.BlockSpec((tm, ROW), lambda i, where_ref: (where_ref[1] * nb + i, 0))),
        out_shape=jax.ShapeDtypeStruct((2 * half, ROW), F32), name="grad_chip_sum", compiler_params=_params(1),
    )(jnp.stack([chip, c]).astype(jnp.int32), s32, got, got, got)


def _join_halves(f):
    half = f.shape[0] // 2
    piece = half // D2D_STREAMS

    def body(f_ref, out_ref, send_sems, recv_sems):
        x, y, c = _me()
        copies = []
        for p in range(D2D_STREAMS):
            rows = out_ref.at[pl.ds(pl.multiple_of(c * half + p * piece, 8), piece), :]
            cp = pltpu.make_async_remote_copy(src_ref=rows, dst_ref=rows, send_sem=send_sems.at[p], recv_sem=recv_sems.at[p],
                                              device_id=(x, y, 1 - c), device_id_type=MESH)
            cp.start()
            copies.append(cp)
        for cp in copies:
            cp.wait()

    return pl.pallas_call(
        body, out_shape=jax.ShapeDtypeStruct(f.shape, F32), in_specs=[HBM], out_specs=HBM, input_output_aliases={0: 0},
        scratch_shapes=[pltpu.SemaphoreType.DMA((D2D_STREAMS,)), pltpu.SemaphoreType.DMA((D2D_STREAMS,))],
        name="grad_join_halves",
    )(f)


def _all_reduce_small(name, v):
    rows = v.shape[0]

    def body(v_ref, out_ref, slots, send_sems, recv_sems):
        x, y, c = _me()
        me = 4 * x + 2 * y + c
        slots[me] = v_ref[...]
        sends = []
        for r in range(1, 8):
            to = (x ^ (r >> 2), y ^ ((r >> 1) & 1), c ^ (r & 1))
            cp = pltpu.make_async_remote_copy(src_ref=v_ref, dst_ref=slots.at[me], send_sem=send_sems.at[r - 1],
                                              recv_sem=recv_sems.at[r - 1], device_id=to, device_id_type=MESH)
            cp.start()
            sends.append(cp)
        for cp in sends:
            cp.wait()
        total = slots[0]
        for d in range(1, 8):
            total = total + slots[d]
        out_ref[...] = total

    vmem = pl.BlockSpec(memory_space=pltpu.VMEM)
    return pl.pallas_call(
        body, out_shape=jax.ShapeDtypeStruct(v.shape, F32), in_specs=[vmem], out_specs=vmem,
        scratch_shapes=[pltpu.VMEM((8, rows, ROW), F32), pltpu.SemaphoreType.DMA((7,)), pltpu.SemaphoreType.DMA((7,))],
        name=name,
    )(v)


def _adam_fn(w, g, m, v):
    m2 = ADAM_B1 * m + (1.0 - ADAM_B1) * g
    v2 = ADAM_B2 * v + (1.0 - ADAM_B2) * (g * g)
    m_hat = m2 / (1.0 - ADAM_B1 ** ADAM_STEP)
    v_hat = v2 / (1.0 - ADAM_B2 ** ADAM_STEP)
    return -ADAM_LR * (m_hat / (jnp.sqrt(v_hat) + ADAM_EPS) + ADAM_WD * w), m2, v2


def _full_shape(name, shard_shape):
    ax = _shard_axis(name)
    return tuple(n * N_CHIPS if k == ax else n for k, n in enumerate(shard_shape))


def _chip_major(name, full):
    if _shard_axis(name) == 0:
        return full.reshape(N_CHIPS, -1, full.shape[1])
    n = full.shape[1] // N_CHIPS
    return jnp.stack([full[:, j * n:(j + 1) * n] for j in range(N_CHIPS)])


def _from_chip_major(name, shards):
    if _shard_axis(name) == 0:
        return shards.reshape(-1, shards.shape[2])
    return jnp.concatenate([shards[j] for j in range(N_CHIPS)], axis=1)


def _row_tile(rows, cap=512):
    return max(t for t in range(8, min(rows, cap) + 1, 8) if rows % t == 0)


def _step(a):
    x_i, y_i, c_i = _me()
    chip = 2 * x_i + y_i
    big_shapes = [a[n].shape for n in BIG]

    packed = _pack([a[n].astype(BF16) for n in BIG], BIG_ROWS)
    shards = _unpack(_gather_shards(packed), big_shapes, lead=(N_CHIPS,))
    big = {n: _from_chip_major(n, sh) for n, sh in zip(BIG, shards)}

    small = {n: a[n] for n in SMALL}
    convs = [n for n in SMALL if n.endswith("dn_conv_w")]
    placed = []
    for n in convs:
        full = jnp.zeros(_full_shape_conv(a[n].shape), F32)
        placed.append(lax.dynamic_update_slice(full, a[n], (0, chip * a[n].shape[1])))
    conv_sum = _all_reduce_small("gather_conv", _pack(placed, 8))
    for n, full in zip(convs, _unpack(conv_sum, [p.shape for p in placed])):
        small[n] = full * 0.5

    part, dx, grads = local_step(a["x"][0], a["loss_target"][0], big, small)
    loss = lax.psum(0.5 * jnp.sum(part) / D_MODEL, ("x", "y", "c"))

    g_all = _pack([_chip_major(n, grads[n]) for n in BIG], BIG_ROWS, lead=(N_CHIPS,))
    s32, s16 = _pair_sum(g_all, _swap_halves(g_all), c_i)
    g_big = _join_halves(_chip_sum(s32, _scatter_chunks(s16), chip, c_i))
    g_big = dict(zip(BIG, _unpack(g_big, big_shapes)))

    g_small_full = _all_reduce_small("reduce_small", _pack([grads[n] for n in SMALL], 8))
    g_small = dict(zip(SMALL, _unpack(g_small_full, [grads[n].shape for n in SMALL])))
    for n in convs:
        g_small[n] = lax.dynamic_slice_in_dim(g_small[n], chip * a[n].shape[1], a[n].shape[1], axis=1)

    outs = {}
    for n in BIG:
        d, m2, v2 = _ew("adam_" + n, _adam_fn, [a[n], g_big[n], a["m_" + n], a["v_" + n]], [],
                        [(a[n].shape[1], F32)] * 3, tm=_row_tile(a[n].shape[0]))
        outs.update({"grad_" + n: g_big[n], "delta_" + n: d, "new_m_" + n: m2, "new_v_" + n: v2})
    pk = lambda prefix: _pack([a[prefix + n] for n in SMALL], 8)
    gs_packed = _pack([g_small[n] for n in SMALL], 8)
    small_bufs = (gs_packed,) + tuple(_ew("adam_small", _adam_fn, [pk(""), gs_packed, pk("m_"), pk("v_")], [], [(ROW, F32)] * 3,
                                          tm=gs_packed.shape[0]))
    small_shapes = [a[n].shape for n in SMALL]
    for key, buf in zip(("grad_", "delta_", "new_m_", "new_v_"), small_bufs):
        outs.update({key + n: val for n, val in zip(SMALL, _unpack(buf, small_shapes))})
    result = [loss, dx[None]]
    for key in ("grad_", "delta_", "new_m_", "new_v_"):
        result += [outs[key + n] for n in WEIGHTS]
    return tuple(result)


def _full_shape_conv(shard_shape):
    return (shard_shape[0], shard_shape[1] * N_CHIPS)


def kernel(x, l0_mix_norm, l0_dn_w_in, l0_dn_conv_w, l0_dn_a_log, l0_dn_dt_bias, l0_dn_out_norm, l0_dn_w_out, l0_ffn_norm, l0_ffn_w_gate_up, l0_ffn_w_down, l1_mix_norm, l1_sb_w_qkv, l1_sb_q_norm, l1_sb_k_norm, l1_sb_w_out, l1_ffn_norm, l1_ffn_w_gate_up, l1_ffn_w_down, l2_mix_norm, l2_mla_w_down, l2_mla_q_a_norm, l2_mla_kv_a_norm, l2_mla_w_uq, l2_mla_w_ukv, l2_mla_q_nope_norm, l2_mla_q_rope_norm, l2_mla_k_nope_norm, l2_mla_k_rope_norm, l2_mla_w_out, l2_ffn_norm, l2_ffn_w_gate_up, l2_ffn_w_down, l3_mix_norm, l3_dn_w_in, l3_dn_conv_w, l3_dn_a_log, l3_dn_dt_bias, l3_dn_out_norm, l3_dn_w_out, l3_ffn_norm, l3_ffn_w_gate_up, l3_ffn_w_down, loss_target, m_l0_mix_norm, m_l0_dn_w_in, m_l0_dn_conv_w, m_l0_dn_a_log, m_l0_dn_dt_bias, m_l0_dn_out_norm, m_l0_dn_w_out, m_l0_ffn_norm, m_l0_ffn_w_gate_up, m_l0_ffn_w_down, m_l1_mix_norm, m_l1_sb_w_qkv, m_l1_sb_q_norm, m_l1_sb_k_norm, m_l1_sb_w_out, m_l1_ffn_norm, m_l1_ffn_w_gate_up, m_l1_ffn_w_down, m_l2_mix_norm, m_l2_mla_w_down, m_l2_mla_q_a_norm, m_l2_mla_kv_a_norm, m_l2_mla_w_uq, m_l2_mla_w_ukv, m_l2_mla_q_nope_norm, m_l2_mla_q_rope_norm, m_l2_mla_k_nope_norm, m_l2_mla_k_rope_norm, m_l2_mla_w_out, m_l2_ffn_norm, m_l2_ffn_w_gate_up, m_l2_ffn_w_down, m_l3_mix_norm, m_l3_dn_w_in, m_l3_dn_conv_w, m_l3_dn_a_log, m_l3_dn_dt_bias, m_l3_dn_out_norm, m_l3_dn_w_out, m_l3_ffn_norm, m_l3_ffn_w_gate_up, m_l3_ffn_w_down, v_l0_mix_norm, v_l0_dn_w_in, v_l0_dn_conv_w, v_l0_dn_a_log, v_l0_dn_dt_bias, v_l0_dn_out_norm, v_l0_dn_w_out, v_l0_ffn_norm, v_l0_ffn_w_gate_up, v_l0_ffn_w_down, v_l1_mix_norm, v_l1_sb_w_qkv, v_l1_sb_q_norm, v_l1_sb_k_norm, v_l1_sb_w_out, v_l1_ffn_norm, v_l1_ffn_w_gate_up, v_l1_ffn_w_down, v_l2_mix_norm, v_l2_mla_w_down, v_l2_mla_q_a_norm, v_l2_mla_kv_a_norm, v_l2_mla_w_uq, v_l2_mla_w_ukv, v_l2_mla_q_nope_norm, v_l2_mla_q_rope_norm, v_l2_mla_k_nope_norm, v_l2_mla_k_rope_norm, v_l2_mla_w_out, v_l2_ffn_norm, v_l2_ffn_w_gate_up, v_l2_ffn_w_down, v_l3_mix_norm, v_l3_dn_w_in, v_l3_dn_conv_w, v_l3_dn_a_log, v_l3_dn_dt_bias, v_l3_dn_out_norm, v_l3_dn_w_out, v_l3_ffn_norm, v_l3_ffn_w_gate_up, v_l3_ffn_w_down):
    return _step(dict(locals()))
```

```python
import functools
import math

import jax
import jax.numpy as jnp
from jax import lax
from jax.experimental import pallas as pl
from jax.experimental.pallas import tpu as pltpu

F32, BF16 = jnp.float32, jnp.bfloat16
MESH = pl.DeviceIdType.MESH

D_MODEL = 1024
N_HEADS = 8
HEAD = 128
FFN_HIDDEN = 2816
DN_CHUNK = 64
NORM_EPS = 1e-6
MLA_ROPE = 64
MLA_QK = 192
ROPE_THETA = 10000.0
ADAM_LR, ADAM_B1, ADAM_B2, ADAM_EPS, ADAM_WD, ADAM_STEP = 0.001, 0.9, 0.999, 1e-08, 0.01, 10
N_CHIPS = 4
LANES = 128
VMEM_LIMIT = 56 * 2 ** 20


def _params(n_grid):
    return pltpu.CompilerParams(dimension_semantics=("arbitrary",) * n_grid, vmem_limit_bytes=VMEM_LIMIT)


_MIXERS = (
    ("dn_w_in", "dn_conv_w", "dn_a_log", "dn_dt_bias", "dn_out_norm", "dn_w_out"),
    ("sb_w_qkv", "sb_q_norm", "sb_k_norm", "sb_w_out"),
    ("mla_w_down", "mla_q_a_norm", "mla_kv_a_norm", "mla_w_uq", "mla_w_ukv", "mla_q_nope_norm",
     "mla_q_rope_norm", "mla_k_nope_norm", "mla_k_rope_norm", "mla_w_out"),
)
DEPTH = 4


def _layer_names(i):
    p = "l%d_" % i
    return [p + "mix_norm"] + [p + n for n in _MIXERS[i % 3]] + [p + "ffn_norm", p + "ffn_w_gate_up", p + "ffn_w_down"]


WEIGHTS = [n for i in range(DEPTH) for n in _layer_names(i)]
_ROW_SHARDED = ("w_out", "ffn_w_down", "mla_w_down")
_COL_SHARDED = ("dn_w_in", "sb_w_qkv", "mla_w_uq", "mla_w_ukv", "ffn_w_gate_up")


def _shard_axis(name):
    if name.endswith(_ROW_SHARDED):
        return 0
    if name.endswith(_COL_SHARDED):
        return 1
    return None


BIG = [n for n in WEIGHTS if _shard_axis(n) is not None]
SMALL = [n for n in WEIGHTS if _shard_axis(n) is None]


_DN = {"nn": (((1,), (0,)), ((), ())), "nt": (((1,), (1,)), ((), ())), "tn": (((0,), (0,)), ((), ()))}
_DN_BATCHED = {"nn": (((2,), (1,)), ((0,), (0,))), "nt": (((2,), (2,)), ((0,), (0,))), "tn": (((1,), (1,)), ((0,), (0,)))}


def _dims(a, kind):
    return _DN_BATCHED[kind] if a.ndim == 3 else _DN[kind]


def _dg(a, b, kind):
    return lax.dot_general(a.astype(BF16), b.astype(BF16), _dims(a, kind), preferred_element_type=F32)


@functools.partial(jax.custom_vjp, nondiff_argnums=(2,))
def bdot(a, b, kind):
    return _dg(a, b, kind)


def _bdot_fwd(a, b, kind):
    return _dg(a, b, kind), (a, b)


def _bdot_bwd(kind, res, ct):
    a, b = res
    if kind == "nn":
        return _dg(ct, b, "nt"), _dg(a, ct, "tn")
    if kind == "nt":
        return _dg(ct, b, "nn"), _dg(ct, a, "tn")
    return _dg(b, ct, "nt"), _dg(a, ct, "nn")


bdot.defvjp(_bdot_fwd, _bdot_bwd)


def _split(a, terms):
    out = []
    for _ in range(terms):
        hi = a.astype(BF16)
        out.append(hi)
        a = a - hi.astype(F32)
    return out


def _xdot(a, b, kind, exact, terms=3):
    if exact == 0:
        return sum(lax.dot_general(a, p, _dims(a, kind), preferred_element_type=F32) for p in _split(b, terms))
    return sum(lax.dot_general(p, b, _dims(a, kind), preferred_element_type=F32) for p in _split(a, terms))


def _tri(n, rel):
    r = lax.broadcasted_iota(jnp.int32, (n, n), 0)
    c = lax.broadcasted_iota(jnp.int32, (n, n), 1)
    return {"le": c <= r, "lt": c < r, "ge": c >= r, "gt": c > r}[rel]


def _running(g):
    n = g.shape[-2]
    return jnp.broadcast_to(_tri(n, "le").astype(BF16), g.shape[:-2] + (n, n))


@jax.custom_vjp
def cumsum_rows(g):
    return _xdot(_running(g), g, "nn", 0)


def _cumsum_fwd(g):
    return cumsum_rows(g), None


def _cumsum_bwd(_, ct):
    return (_xdot(_running(ct), ct, "tn", 0),)


cumsum_rows.defvjp(_cumsum_fwd, _cumsum_bwd)


def _dot3(a, b, kind):
    (ah, al), (bh, bl) = _split(a, 2), _split(b, 2)
    dot = lambda p, q: lax.dot_general(p, q, _dims(a, kind), preferred_element_type=F32)
    return dot(ah, bh) + (dot(ah, bl) + dot(al, bh))


@functools.partial(jax.custom_vjp, nondiff_argnums=(2,))
def _hdot3(a, b, kind):
    return _dot3(a, b, kind)


def _hdot3_fwd(a, b, kind):
    return _dot3(a, b, kind), (a, b)


def _hdot3_bwd(kind, res, ct):
    a, b = res
    if kind == "nn":
        return _dot3(ct, b, "nt"), _dot3(a, ct, "tn")
    if kind == "nt":
        return _dot3(ct, b, "nn"), _dot3(ct, a, "tn")
    return _dot3(b, ct, "nt"), _dot3(a, ct, "nn")


_hdot3.defvjp(_hdot3_fwd, _hdot3_bwd)


def _hdot(a, b):
    return _hdot3(a, b, "nn")


def _unit_lower_inverse(lower):
    n = lower.shape[-1]
    eye = (lax.broadcasted_iota(jnp.int32, (n, n), 0) == lax.broadcasted_iota(jnp.int32, (n, n), 1)).astype(F32)
    m = -lower
    p = eye + m
    for _ in range(int(math.log2(n)) - 1):
        m = _hdot(m, m)
        p = p + _hdot(p, m)
    return p


def _rms(x, g, n=None):
    n = x.shape[-1] if n is None else n
    return x * lax.rsqrt(jnp.sum(x * x, axis=-1, keepdims=True) * (1.0 / n) + NORM_EPS) * g


def _l2(x):
    return x * lax.rsqrt(jnp.sum(x * x, axis=-1, keepdims=True) + NORM_EPS)


def _silu(x):
    return x * jax.nn.sigmoid(x)


def _logsig(z):
    return jnp.minimum(z, 0.0) - jnp.log1p(jnp.exp(-jnp.abs(z)))


@jax.custom_vjp
def _rope(x, cos, sin_lo, sin_hi):
    return x * cos + pltpu.roll(x, 96, 1) * sin_lo + pltpu.roll(x, 32, 1) * sin_hi


def _rope_fwd(x, cos, sin_lo, sin_hi):
    return _rope(x, cos, sin_lo, sin_hi), (cos, sin_lo, sin_hi)


def _rope_bwd(res, ct):
    cos, sin_lo, sin_hi = res
    dx = ct * cos + pltpu.roll(ct * sin_lo, 32, 1) + pltpu.roll(ct * sin_hi, 96, 1)
    return dx, jnp.zeros_like(cos), jnp.zeros_like(sin_lo), jnp.zeros_like(sin_hi)


_rope.defvjp(_rope_fwd, _rope_bwd)


def _tile(n, prefs=(512, 384, 256, 128)):
    for t in prefs:
        if n % t == 0:
            return t
    return n


MM_OUT_TILES = (1024, 1408, 512, 384, 256, 128)
MM_K_TILES = (1024, 512, 384, 256, 128)


def _mm(name, a, b, kind, out_dtype=F32, add=None):
    if kind == "tn":
        (kdim, m), n = a.shape, b.shape[1]
    else:
        (m, kdim), n = a.shape, (b.shape[0] if kind == "nt" else b.shape[1])
    tm, tn, tk = _tile(m, MM_OUT_TILES), _tile(n, MM_OUT_TILES), _tile(kdim, MM_K_TILES)
    nk = kdim // tk
    a_spec = pl.BlockSpec((tk, tm), lambda i, j, k: (k, i)) if kind == "tn" else pl.BlockSpec((tm, tk), lambda i, j, k: (i, k))
    b_spec = pl.BlockSpec((tn, tk), lambda i, j, k: (j, k)) if kind == "nt" else pl.BlockSpec((tk, tn), lambda i, j, k: (k, j))
    o_spec = pl.BlockSpec((tm, tn), lambda i, j, k: (i, j))
    has_add = add is not None

    def body(*refs):
        a_ref, b_ref = refs[0], refs[1]
        o_ref, acc = refs[-2], refs[-1]
        k = pl.program_id(2)

        @pl.when(k == 0)
        def _():
            acc[...] = jnp.zeros_like(acc)

        acc[...] += _dg(a_ref[...], b_ref[...], kind)

        @pl.when(k == nk - 1)
        def _():
            r = acc[...]
            if has_add:
                r = r + refs[2][...]
            o_ref[...] = r.astype(o_ref.dtype)

    return pl.pallas_call(
        body, grid=(m // tm, n // tn, nk),
        in_specs=[a_spec, b_spec] + ([o_spec] if has_add else []),
        out_specs=o_spec, out_shape=jax.ShapeDtypeStruct((m, n), out_dtype),
        scratch_shapes=[pltpu.VMEM((tm, tn), F32)], name=name, compiler_params=_params(3),
    )(*([a, b] + ([add] if has_add else [])))


class _V:
    def __init__(self, arr, w=None, base=0, diff=True):
        self.arr, self.base, self.diff = arr, base, diff
        self.w = arr.shape[1] if w is None else w

    def spec(self, tm):
        return pl.BlockSpec((tm, self.w), lambda i, b=self.base: (i, b))


def _as_views(ins):
    return [v if isinstance(v, _V) else _V(v) for v in ins]


def _tup(r):
    return tuple(r) if isinstance(r, (tuple, list)) else (r,)


def _ew(name, fn, ins, smalls, outs, tm=256):
    ins = _as_views(ins)
    t = ins[0].arr.shape[0]
    tm = min(tm, t)
    n_in = len(ins) + len(smalls)

    def body(*refs):
        res = _tup(fn(*[r[...] for r in refs[:n_in]]))
        for r, o in zip(refs[n_in:], res):
            r[...] = o.astype(r.dtype)

    return pl.pallas_call(
        body, grid=(t // tm,),
        in_specs=[v.spec(tm) for v in ins] + [pl.BlockSpec(s.shape, lambda i: (0, 0)) for s in smalls],
        out_specs=[pl.BlockSpec((tm, w), lambda i: (i, 0)) for w, _ in outs],
        out_shape=[jax.ShapeDtypeStruct((t, w), dt) for w, dt in outs],
        name=name, compiler_params=_params(1),
    )(*[v.arr for v in ins], *smalls)


def _ew_bwd(name, fn, ins, smalls, cts, tm=256, add=None, ct_dtypes=None):
    ins = _as_views(ins)
    t = ins[0].arr.shape[0]
    tm = min(tm, t)
    n_in, n_sm = len(ins), len(smalls)
    diff = [k for k, v in enumerate(ins) if v.diff]
    ct_dtypes = [F32] * len(diff) if ct_dtypes is None else ct_dtypes
    ct_arrs = [c for c in cts if c is not None]
    has_add = add is not None

    def body(*refs):
        vals = [r[...] for r in refs[:n_in]]
        svals = [r[...] for r in refs[n_in:n_in + n_sm]]
        p = n_in + n_sm
        ct_refs = list(refs[p:p + len(ct_arrs)])
        p += len(ct_arrs)
        add_ref = refs[p] if has_add else None
        p += int(has_add)
        din_refs = refs[p:p + len(diff)]
        dsm_refs = refs[p + len(diff):]

        def f(dv, sv):
            full = list(vals)
            for k, d in zip(diff, dv):
                full[k] = d
            return _tup(fn(*full, *sv))

        res, vjp = jax.vjp(f, [vals[k] for k in diff], svals)
        ctv = tuple(jnp.zeros_like(o) if c is None else ct_refs.pop(0)[...].astype(o.dtype) for c, o in zip(cts, res))
        dv, dsv = vjp(ctv)
        for n, (r, d) in enumerate(zip(din_refs, dv)):
            if n == 0 and has_add:
                d = d + add_ref[...]
            r[...] = d.astype(r.dtype)

        @pl.when(pl.program_id(0) == 0)
        def _():
            for r in dsm_refs:
                r[...] = jnp.zeros_like(r)

        for r, d in zip(dsm_refs, dsv):
            r[...] += d

    row = lambda w: pl.BlockSpec((tm, w), lambda i: (i, 0))
    small_specs = [pl.BlockSpec(s.shape, lambda i: (0, 0)) for s in smalls]
    out = pl.pallas_call(
        body, grid=(t // tm,),
        in_specs=[v.spec(tm) for v in ins] + small_specs + [row(c.shape[1]) for c in ct_arrs]
        + ([row(add.shape[1])] if has_add else []),
        out_specs=[row(ins[k].w) for k in diff] + small_specs,
        out_shape=[jax.ShapeDtypeStruct((t, ins[k].w), dt) for k, dt in zip(diff, ct_dtypes)]
        + [jax.ShapeDtypeStruct(s.shape, F32) for s in smalls],
        name=name, compiler_params=_params(1),
    )(*[v.arr for v in ins], *smalls, *ct_arrs, *([add] if has_add else []))
    return out[:len(diff)], out[len(diff):]


BQ = 256
HPB = 2
SUM_TERMS = 2


def _cat(parts):
    return parts[0] if len(parts) == 1 else jnp.concatenate(parts, axis=1)


def _head_view(ref, hh):
    return ref.at[:, hh * HEAD:(hh + 1) * HEAD]


def _attn_specs(qs, ks, t):
    q_specs = [pl.BlockSpec((BQ, HPB * HEAD), lambda h, i: (i, h)) for _ in qs]
    per_head = pl.BlockSpec((t, HPB * HEAD), lambda h, i: (0, h))
    k_specs = [pl.BlockSpec((t, HEAD), lambda h, i: (0, 0)) if sh else per_head for _, sh in ks]
    return q_specs, k_specs, per_head


def _causal_sweep(i, pair, init, diagonal_first):
    order = (lambda s: i - 1 - s) if diagonal_first else (lambda s: s)
    two = lambda s, c: pair(order(2 * s + 1), pair(order(2 * s), c, False), False)
    one = lambda s, c: pair(order(s), c, False)
    carry = pair(i, init, True) if diagonal_first else init
    carry = lax.fori_loop(0, lax.shift_right_logical(i, 1), two, carry)
    carry = lax.fori_loop(i - (i & 1), i, one, carry)
    return carry if diagonal_first else pair(i, carry, True)


def _attn_fwd(name, mode, qs, ks, v):
    t = qs[0].shape[0]
    nq, n = t // BQ, len(qs)
    q_specs, k_specs, per_head = _attn_specs(qs, ks, t)
    shared = [sh for _, sh in ks]

    def body(*refs):
        q_refs, k_refs, v_ref = refs[:n], refs[n:2 * n], refs[2 * n]
        o_ref, st_ref = refs[2 * n + 1], refs[2 * n + 2]
        i = pl.program_id(1)
        row = lax.broadcasted_iota(jnp.int32, (BQ, BQ), 0)
        col = lax.broadcasted_iota(jnp.int32, (BQ, BQ), 1)
        after = _tri(BQ, "lt").astype(BF16)

        def head(hh):
            q = _cat([_head_view(r, hh)[...] for r in q_refs])
            k_h = [kr if sh else _head_view(kr, hh) for kr, sh in zip(k_refs, shared)]
            v_h = _head_view(v_ref, hh)

            def pair(j, carry, masked):
                off = pl.multiple_of(j * BQ, BQ)
                z = _dg(q, _cat([kr[pl.ds(off, BQ), :] for kr in k_h]), "nt")
                vj = v_h[pl.ds(off, BQ), :]
                if mode == "sb":
                    acc, run = carry
                    lsz = _logsig(z)
                    stay = lsz - z
                    if masked:
                        stay = jnp.where(col < row, stay, 0.0)
                    a = jnp.exp(lsz + (run + _xdot(stay, after, "nn", 1, SUM_TERMS)))
                    if masked:
                        a = jnp.where(col < row, a, 0.0)
                    return acc + _dg(a, vj, "nn"), run + jnp.sum(stay, axis=1, keepdims=True)
                m, l, acc = carry
                if masked:
                    z = jnp.where(col <= row, z, -1e30)
                m2 = jnp.maximum(m, jnp.max(z, axis=1, keepdims=True))
                p = jnp.exp(z - m2)
                alpha = jnp.exp(m - m2)
                return m2, alpha * l + jnp.sum(p, axis=1, keepdims=True), alpha * acc + _dg(p, vj, "nn")

            def finish(carry):
                if mode == "sb":
                    acc, run = carry
                    _head_view(o_ref, hh)[...] = acc
                    _head_view(st_ref, hh)[...] = jnp.broadcast_to(run, (BQ, HEAD))
                else:
                    m, l, acc = carry
                    _head_view(o_ref, hh)[...] = acc / l
                    _head_view(st_ref, hh)[...] = jnp.broadcast_to(m + jnp.log(l), (BQ, HEAD))

            zero = jnp.zeros((BQ, 1), F32)
            acc0 = jnp.zeros((BQ, HEAD), F32)
            init = (acc0, zero) if mode == "sb" else (jnp.full((BQ, 1), -1e30, F32), zero, acc0)
            return pair, init, finish

        heads = [head(hh) for hh in range(HPB)]
        both = lambda j, carry, masked: tuple(h[0](j, c, masked) for h, c in zip(heads, carry))
        final = _causal_sweep(i, both, tuple(h[1] for h in heads), diagonal_first=(mode == "sb"))
        for h, c in zip(heads, final):
            h[2](c)

    blk = pl.BlockSpec((BQ, HPB * HEAD), lambda h, i: (i, h))
    return pl.pallas_call(
        body, grid=(N_HEADS // HPB, nq), in_specs=q_specs + k_specs + [per_head], out_specs=[blk, blk],
        out_shape=[jax.ShapeDtypeStruct((t, N_HEADS * HEAD), F32)] * 2, name=name, compiler_params=_params(2),
    )(*qs, *[k for k, _ in ks], v)


def _attn_bwd(name, mode, qs, ks, v, o, stat, do, send=None):
    t = qs[0].shape[0]
    nq, n = t // BQ, len(qs)
    q_specs, k_specs, per_head = _attn_specs(qs, ks, t)
    shared = [sh for _, sh in ks]

    def body(*refs):
        q_refs, k_refs, v_ref = refs[:n], refs[n:2 * n], refs[2 * n]
        o_ref, st_ref, do_ref = refs[2 * n + 1:2 * n + 4]
        n_in = 2 * n + 4 + int(send is not None)
        dq_refs = refs[n_in:n_in + n]
        dk_refs = refs[n_in + n:n_in + 2 * n]
        dv_ref = refs[n_in + 2 * n]
        g, i = pl.program_id(0), pl.program_id(1)
        if send is not None:
            send_ref, got_ref, send_sems, recv_sems = refs[n_in - 1], refs[n_in + 2 * n + 1], refs[-2], refs[-1]

            def exchange():
                x, y, c = _me()
                return [pltpu.make_async_remote_copy(src_ref=send_ref.at[2 * px + py], dst_ref=got_ref.at[j],
                                                     send_sem=send_sems.at[j], recv_sem=recv_sems.at[j],
                                                     device_id=(px, py, c), device_id_type=MESH)
                        for j, (px, py) in enumerate(_other_chips(x, y))]

            @pl.when((g == 0) & (i == 0))
            def _():
                for cp in exchange():
                    cp.start()

        @pl.when(i == 0)
        def _():
            dv_ref[...] = jnp.zeros_like(dv_ref)
            for r, sh in zip(dk_refs, shared):
                if not sh:
                    r[...] = jnp.zeros_like(r)

        for r, sh in zip(dk_refs, shared):
            if sh:
                @pl.when((i == 0) & (g == 0))
                def _(r=r):
                    r[...] = jnp.zeros_like(r)

        row = lax.broadcasted_iota(jnp.int32, (BQ, BQ), 0)
        col = lax.broadcasted_iota(jnp.int32, (BQ, BQ), 1)
        upto = _tri(BQ, "ge").astype(BF16)
        before = _tri(BQ, "gt").astype(BF16)

        def head(hh):
            q = _cat([_head_view(r, hh)[...] for r in q_refs])
            k_h = [kr if sh else _head_view(kr, hh) for kr, sh in zip(k_refs, shared)]
            dk_h = [r if sh else _head_view(r, hh) for r, sh in zip(dk_refs, shared)]
            v_h, dv_h = _head_view(v_ref, hh), _head_view(dv_ref, hh)
            do_t = _head_view(do_ref, hh)[...]
            st = _head_view(st_ref, hh)[:, :1]
            if mode == "softmax":
                dsum = jnp.sum(do_t * _head_view(o_ref, hh)[...], axis=1, keepdims=True)

            def pair(j, carry, masked):
                off = pl.multiple_of(j * BQ, BQ)
                kj = _cat([kr[pl.ds(off, BQ), :] for kr in k_h])
                z = _dg(q, kj, "nt")
                da = _dg(do_t, v_h[pl.ds(off, BQ), :], "nt")
                if mode == "sb":
                    dq, pre, gpre = carry
                    lsz = _logsig(z)
                    stay = lsz - z
                    if masked:
                        stay = jnp.where(col < row, stay, 0.0)
                    a = jnp.exp(lsz + (st - (pre + _xdot(stay, upto, "nn", 1, SUM_TERMS))))
                    if masked:
                        a = jnp.where(col < row, a, 0.0)
                    gr = a * da
                    sig = jnp.exp(lsz)
                    dz = gr * (1.0 - sig) - sig * (gpre + _xdot(gr, before, "nn", 1, SUM_TERMS))
                    if masked:
                        dz = jnp.where(col < row, dz, 0.0)
                    tail = (pre + jnp.sum(stay, axis=1, keepdims=True), gpre + jnp.sum(gr, axis=1, keepdims=True))
                else:
                    dq = carry[0]
                    a = jnp.exp(z - st)
                    if masked:
                        a = jnp.where(col <= row, a, 0.0)
                    dz = a * (da - dsum)
                    tail = ()
                dk = _dg(dz, q, "tn")
                for p, r in enumerate(dk_h):
                    r[pl.ds(off, BQ), :] += dk[:, p * HEAD:(p + 1) * HEAD]
                dv_h[pl.ds(off, BQ), :] += _dg(a, do_t, "tn")
                return (dq + _dg(dz, kj, "nn"),) + tail

            def finish(carry):
                for p, r in enumerate(dq_refs):
                    _head_view(r, hh)[...] = carry[0][:, p * HEAD:(p + 1) * HEAD]

            zero = jnp.zeros((BQ, 1), F32)
            init = (jnp.zeros((BQ, n * HEAD), F32),) + ((zero, zero) if mode == "sb" else ())
            return pair, init, finish

        heads = [head(hh) for hh in range(HPB)]
        both = lambda j, carry, masked: tuple(h[0](j, c, masked) for h, c in zip(heads, carry))
        final = _causal_sweep(i, both, tuple(h[1] for h in heads), diagonal_first=False)
        for h, c in zip(heads, final):
            h[2](c)

        if send is not None:
            @pl.when((g == N_HEADS // HPB - 1) & (i == nq - 1))
            def _():
                for cp in exchange():
                    cp.wait()

    blk = pl.BlockSpec((BQ, HPB * HEAD), lambda h, i: (i, h))
    dk_specs = [pl.BlockSpec((t, HEAD), lambda h, i: (0, 0)) if sh else per_head for sh in shared]
    wide = jax.ShapeDtypeStruct((t, N_HEADS * HEAD), F32)
    comm_in, comm_out, comm_shape, comm_scratch = [], [], [], []
    if send is not None:
        comm_in, comm_out = [send], [HBM]
        comm_shape = [jax.ShapeDtypeStruct((3,) + send.shape[1:], send.dtype)]
        comm_scratch = [pltpu.SemaphoreType.DMA((3,)), pltpu.SemaphoreType.DMA((3,))]
    out = pl.pallas_call(
        body, grid=(N_HEADS // HPB, nq), in_specs=q_specs + k_specs + [per_head, blk, blk, blk] + [HBM] * len(comm_in),
        out_specs=[blk] * n + dk_specs + [per_head] + comm_out,
        out_shape=[wide] * n + [jax.ShapeDtypeStruct((t, HEAD), F32) if sh else wide for sh in shared] + [wide] + comm_shape,
        scratch_shapes=comm_scratch, name=name, compiler_params=_params(2),
    )(*qs, *[k for k, _ in ks], v, o, stat, do, *comm_in)
    return (out[:n], out[n:2 * n], out[2 * n]) + tuple(out[2 * n + 1:])


@jax.custom_vjp
def _given_inverse(lower, tinv):
    return tinv


def _given_inverse_fwd(lower, tinv):
    return tinv, tinv


def _given_inverse_bwd(tinv, ct):
    return -_dot3(_dot3(tinv, ct, "tn"), tinv, "nt"), jnp.zeros_like(tinv)


_given_inverse.defvjp(_given_inverse_fwd, _given_inverse_bwd)


def _dn_chunk(q, k, v, g, beta, state, tinv=None):
    c = q.shape[-2]
    gc = cumsum_rows(g)
    gcc = gc[..., :c]
    diff = gcc - jnp.swapaxes(gcc, -1, -2)
    causal, strict = _tri(c, "le"), _tri(c, "lt")
    decay = jnp.where(causal, jnp.exp(jnp.where(causal, diff, 0.0)), 0.0)
    kb = k * beta
    lower = jnp.where(strict, bdot(kb, k, "nt") * decay, 0.0)
    tinv = _unit_lower_inverse(lower) if tinv is None else _given_inverse(lower, tinv)
    eg = jnp.exp(gc)
    u = _hdot(tinv, v * beta)
    w = _hdot(tinv, kb * eg)
    attn = bdot(q, k, "nt") * decay
    glast = gc[..., c - 1:c, :]
    v_new = u - bdot(w, state, "nn")
    o = bdot(q * eg, state, "nn") + bdot(attn, v_new, "nn")
    new_state = state * jnp.exp(glast) + bdot(k * jnp.exp(glast - gc), v_new, "tn")
    return o, new_state, tinv


def _stack_heads(ref):
    return jnp.stack([ref[:, h * HEAD:(h + 1) * HEAD] for h in range(N_HEADS)])


def _store_heads(ref, val):
    for h in range(N_HEADS):
        ref[:, h * HEAD:(h + 1) * HEAD] = val[h]


def _dn_fwd(name, q, k, v, g, beta):
    t = q.shape[0]
    nc = t // DN_CHUNK
    wide = N_HEADS * HEAD
    blk = pl.BlockSpec((DN_CHUNK, wide), lambda n: (n, 0))
    st_spec = pl.BlockSpec((N_HEADS, None, HEAD, HEAD), lambda n: (0, n, 0, 0))
    inv_spec = pl.BlockSpec((N_HEADS, None, DN_CHUNK, DN_CHUNK), lambda n: (0, n, 0, 0))

    def body(q_ref, k_ref, v_ref, g_ref, b_ref, o_ref, st_ref, inv_ref, state):
        @pl.when(pl.program_id(0) == 0)
        def _():
            state[...] = jnp.zeros_like(state)

        s_in = state[...]
        st_ref[...] = s_in
        o, s_out, tinv = _dn_chunk(*[_stack_heads(r) for r in (q_ref, k_ref, v_ref, g_ref, b_ref)], s_in)
        _store_heads(o_ref, o)
        inv_ref[...] = tinv
        state[...] = s_out

    return pl.pallas_call(
        body, grid=(nc,), in_specs=[blk] * 5, out_specs=[blk, st_spec, inv_spec],
        out_shape=[jax.ShapeDtypeStruct((t, wide), F32), jax.ShapeDtypeStruct((N_HEADS, nc, HEAD, HEAD), F32),
                   jax.ShapeDtypeStruct((N_HEADS, nc, DN_CHUNK, DN_CHUNK), F32)],
        scratch_shapes=[pltpu.VMEM((N_HEADS, HEAD, HEAD), F32)], name=name, compiler_params=_params(1),
    )(q, k, v, g, beta)


def _dn_bwd(name, q, k, v, g, beta, states, inverses, do):
    t = q.shape[0]
    nc = t // DN_CHUNK
    wide = N_HEADS * HEAD
    blk = pl.BlockSpec((DN_CHUNK, wide), lambda n: (nc - 1 - n, 0))
    st_spec = pl.BlockSpec((N_HEADS, None, HEAD, HEAD), lambda n: (0, nc - 1 - n, 0, 0))
    inv_spec = pl.BlockSpec((N_HEADS, None, DN_CHUNK, DN_CHUNK), lambda n: (0, nc - 1 - n, 0, 0))

    def body(q_ref, k_ref, v_ref, g_ref, b_ref, st_ref, inv_ref, do_ref, dq_ref, dk_ref, dv_ref, dg_ref, db_ref, dstate):
        @pl.when(pl.program_id(0) == 0)
        def _():
            dstate[...] = jnp.zeros_like(dstate)

        tinv = inv_ref[...]
        chunk = lambda *args: _dn_chunk(*args, tinv=tinv)[:2]
        _, vjp = jax.vjp(chunk, *[_stack_heads(r) for r in (q_ref, k_ref, v_ref, g_ref, b_ref)], st_ref[...])
        cts = vjp((_stack_heads(do_ref), dstate[...]))
        for r, d in zip((dq_ref, dk_ref, dv_ref, dg_ref, db_ref), cts[:5]):
            _store_heads(r, d)
        dstate[...] = cts[5]

    shape = jax.ShapeDtypeStruct((t, wide), F32)
    return pl.pallas_call(
        body, grid=(nc,), in_specs=[blk] * 5 + [st_spec, inv_spec, blk], out_specs=[blk] * 5, out_shape=[shape] * 5,
        scratch_shapes=[pltpu.VMEM((N_HEADS, HEAD, HEAD), F32)], name=name, compiler_params=_params(1),
    )(q, k, v, g, beta, states, inverses, do)


CONV_W = 1024
HALO = 8


def _shift_down(cur, prev, s):
    sh = pltpu.roll(cur, s, 0)
    ph = pltpu.roll(prev, s, 0)
    r = lax.broadcasted_iota(jnp.int32, (HALO, cur.shape[1]), 0)
    return jnp.concatenate([jnp.where(r < s, ph, sh[:HALO]), sh[HALO:]], axis=0)


def _shift_up(cur, nxt, s):
    tm = cur.shape[0]
    sh = pltpu.roll(cur, tm - s, 0)
    nh = pltpu.roll(nxt, HALO - s, 0)
    r = lax.broadcasted_iota(jnp.int32, (HALO, cur.shape[1]), 0)
    return jnp.concatenate([sh[:tm - HALO], jnp.where(r >= HALO - s, nh, sh[tm - HALO:])], axis=0)


def _conv_fwd(name, proj, w, tm=256):
    t = proj.shape[0]
    tm = min(tm, t)
    width = w.shape[1]
    per = tm // HALO

    def body(cur_ref, prev_ref, w_ref, y_ref):
        cur = cur_ref[...]
        prev = jnp.where(pl.program_id(0) > 0, prev_ref[...], 0.0)
        y = cur * w_ref[3:4, :]
        for s in (1, 2, 3):
            y = y + _shift_down(cur, prev, s) * w_ref[3 - s:4 - s, :]
        y_ref[...] = y

    return pl.pallas_call(
        body, grid=(t // tm, width // CONV_W),
        in_specs=[pl.BlockSpec((tm, CONV_W), lambda i, c: (i, c)),
                  pl.BlockSpec((HALO, CONV_W), lambda i, c: (jnp.maximum(i * per - 1, 0), c)),
                  pl.BlockSpec((HALO, CONV_W), lambda i, c: (0, c))],
        out_specs=pl.BlockSpec((tm, CONV_W), lambda i, c: (i, c)),
        out_shape=jax.ShapeDtypeStruct((t, width), F32), name=name, compiler_params=_params(2),
    )(proj, proj, w)


def _conv_bwd(name, proj, w, dy, tm=256):
    t = proj.shape[0]
    tm = min(tm, t)
    width = w.shape[1]
    per, nt = tm // HALO, t // tm

    def body(cur_ref, prev_ref, w_ref, dy_ref, nxt_ref, du_ref, dw_ref):
        i = pl.program_id(1)
        cur, dy_t = cur_ref[...], dy_ref[...]
        prev = jnp.where(i > 0, prev_ref[...], 0.0)
        nxt = jnp.where(i < nt - 1, nxt_ref[...], 0.0)
        du = dy_t * w_ref[3:4, :]
        rows = [jnp.sum(dy_t * cur, axis=0, keepdims=True)]
        for s in (1, 2, 3):
            du = du + _shift_up(dy_t, nxt, s) * w_ref[3 - s:4 - s, :]
            rows.insert(0, jnp.sum(dy_t * _shift_down(cur, prev, s), axis=0, keepdims=True))
        du_ref[...] = du.astype(du_ref.dtype)

        @pl.when(i == 0)
        def _():
            dw_ref[...] = jnp.zeros_like(dw_ref)

        dw_ref[...] += jnp.concatenate(rows + [jnp.zeros((HALO - 4, CONV_W), F32)], axis=0)

    return pl.pallas_call(
        body, grid=(width // CONV_W, nt),
        in_specs=[pl.BlockSpec((tm, CONV_W), lambda c, i: (i, c)),
                  pl.BlockSpec((HALO, CONV_W), lambda c, i: (jnp.maximum(i * per - 1, 0), c)),
                  pl.BlockSpec((HALO, CONV_W), lambda c, i: (0, c)),
                  pl.BlockSpec((tm, CONV_W), lambda c, i: (i, c)),
                  pl.BlockSpec((HALO, CONV_W), lambda c, i: (jnp.minimum((i + 1) * per, t // HALO - 1), c))],
        out_specs=[pl.BlockSpec((tm, CONV_W), lambda c, i: (i, c)), pl.BlockSpec((HALO, CONV_W), lambda c, i: (0, c))],
        out_shape=[jax.ShapeDtypeStruct((t, width), BF16), jax.ShapeDtypeStruct((HALO, width), F32)],
        name=name, compiler_params=_params(2),
    )(proj, proj, w, dy, dy)


def _norm_fn(x, g):
    return _rms(x, g)


def _swiglu_fn(gu):
    return _silu(gu[:, :FFN_HIDDEN]) * gu[:, FFN_HIDDEN:]


def _heads(x):
    return [x[:, h * HEAD:(h + 1) * HEAD] for h in range(x.shape[1] // HEAD)]


def _dn_pre_fn(c, ab, a_log, dt_bias):
    w = N_HEADS * HEAD
    q = [_l2(_silu(x)) * (HEAD ** -0.5) for x in _heads(c[:, :w])]
    k = [_l2(_silu(x)) for x in _heads(c[:, w:2 * w])]
    v = _silu(c[:, 2 * w:])
    g, beta = [], []
    for h in range(N_HEADS):
        gh = -jnp.exp(a_log[:, h:h + 1]) * jax.nn.softplus(ab[:, h:h + 1] + dt_bias[:, h:h + 1])
        bh = jax.nn.sigmoid(ab[:, N_HEADS + h:N_HEADS + h + 1])
        g.append(jnp.broadcast_to(gh, (c.shape[0], HEAD)))
        beta.append(jnp.broadcast_to(bh, (c.shape[0], HEAD)))
    cat = lambda xs: jnp.concatenate(xs, axis=1)
    return cat(q), cat(k), v, cat(g), cat(beta)


def _dn_post_fn(o, z, out_norm):
    return jnp.concatenate([_rms(oh, out_norm) * _silu(zh) for oh, zh in zip(_heads(o), _heads(z))], axis=1)


def _sb_pre_fn(qkv, q_norm, k_norm):
    w = N_HEADS * HEAD
    q = [_rms(x, q_norm) * (HEAD ** -0.5) for x in _heads(qkv[:, :w])]
    k = [_rms(x, k_norm) for x in _heads(qkv[:, w:2 * w])]
    return jnp.concatenate(q, axis=1), jnp.concatenate(k, axis=1), qkv[:, 2 * w:]


def _mla_a_fn(down, cos, sin_lo, sin_hi, q_a_norm, kv_a_norm, k_rope_norm):
    cq = _rms(down[:, :256], q_a_norm)
    ckv = _rms(down[:, 256:384], kv_a_norm)
    kr = _rope(_rms(down[:, 384:], k_rope_norm, MLA_ROPE), cos, sin_lo, sin_hi)
    return cq, ckv, kr


def _mla_b_fn(qf, kvf, cos, sin_lo, sin_hi, q_nope_norm, q_rope_norm, k_nope_norm):
    scale = MLA_QK ** -0.5
    qn, qr, kn, v = [], [], [], []
    for h in range(N_HEADS):
        a = 2 * h * HEAD
        qn.append(_rms(qf[:, a:a + HEAD], q_nope_norm) * scale)
        qr.append(_rope(_rms(qf[:, a + HEAD:a + 2 * HEAD], q_rope_norm, MLA_ROPE), cos, sin_lo, sin_hi) * scale)
        kn.append(_rms(kvf[:, a:a + HEAD], k_nope_norm))
        v.append(kvf[:, a + HEAD:a + 2 * HEAD])
    cat = lambda xs: jnp.concatenate(xs, axis=1)
    return cat(qn), cat(qr), cat(kn), cat(v)


def _rope_tables(t):
    inv_freq = ROPE_THETA ** (-jnp.arange(0, MLA_ROPE, 2, dtype=F32) / MLA_ROPE)
    ang = jnp.arange(t, dtype=F32)[:, None] * inv_freq[None, :]
    cos, sin, zero = jnp.cos(ang), jnp.sin(ang), jnp.zeros((t, MLA_ROPE // 2), F32)
    cat = lambda xs: jnp.concatenate(xs, axis=1)
    return cat([cos, cos, zero, zero]), cat([-sin, zero, zero, zero]), cat([zero, sin, zero, zero])


def _row(v, width=None):
    width = v.shape[0] if width is None else width
    return jnp.pad(v.astype(F32), (0, width - v.shape[0])).reshape(1, width)


def _loss_kernel(y, target):
    t, d = y.shape
    tm = min(256, t)

    def body(y_ref, t_ref, part_ref, dy_ref):
        e = y_ref[...] - t_ref[...]
        dy_ref[...] = e * (1.0 / d)

        @pl.when(pl.program_id(0) == 0)
        def _():
            part_ref[...] = jnp.zeros_like(part_ref)

        part_ref[...] += jnp.sum(e * e, axis=0, keepdims=True)

    blk = pl.BlockSpec((tm, d), lambda i: (i, 0))
    one = pl.BlockSpec((1, d), lambda i: (0, 0))
    return pl.pallas_call(body, grid=(t // tm,), in_specs=[blk, blk], out_specs=[one, blk],
                          out_shape=[jax.ShapeDtypeStruct((1, d), F32), jax.ShapeDtypeStruct((t, d), F32)],
                          name="loss", compiler_params=_params(1))(y, target)


def _ffn_fwd(p, x, w, sm):
    h, = _ew(p + "ffn_norm", _norm_fn, [x], [sm["ffn_norm"]], [(D_MODEL, BF16)])
    gu = _mm(p + "ffn_gu", h, w["ffn_w_gate_up"], "nn")
    act, = _ew(p + "ffn_act", _swiglu_fn, [gu], [], [(FFN_HIDDEN, BF16)], tm=128)
    y = _mm(p + "ffn_down", act, w["ffn_w_down"], "nn", add=x)
    return y, (x, h, gu, act)


def _ffn_bwd(p, saved, dy, w, sm, grads):
    x, h, gu, act = saved
    dact = _mm(p + "ffn_down_dx", dy, w["ffn_w_down"], "nt")
    grads["ffn_w_down"] = _mm(p + "ffn_down_dw", act, dy, "tn")
    (dgu,), _ = _ew_bwd(p + "ffn_act_bwd", _swiglu_fn, [gu], [], [dact], tm=128, ct_dtypes=[BF16])
    dh = _mm(p + "ffn_gu_dx", dgu, w["ffn_w_gate_up"], "nt")
    grads["ffn_w_gate_up"] = _mm(p + "ffn_gu_dw", h, dgu, "tn")
    (dx,), (dg,) = _ew_bwd(p + "ffn_norm_bwd", _norm_fn, [x], [sm["ffn_norm"]], [dh], add=dy)
    grads["ffn_norm"] = dg
    return dx


def _dn_layer_fwd(p, x, w, sm):
    h, = _ew(p + "mix_norm", _norm_fn, [x], [sm["mix_norm"]], [(D_MODEL, BF16)])
    proj = _mm(p + "dn_in", h, w["dn_w_in"], "nn")
    conv = _conv_fwd(p + "dn_conv", proj, sm["dn_conv_w"])
    ab = _V(proj, LANES, 4 * N_HEADS)
    wide = N_HEADS * HEAD
    q, k, v, g, beta = _ew(p + "dn_pre", _dn_pre_fn, [conv, ab], [sm["dn_a_log"], sm["dn_dt_bias"]], [(wide, F32)] * 5)
    o, states, inverses = _dn_fwd(p + "dn_core", q, k, v, g, beta)
    z = _V(proj, wide, 3)
    on, = _ew(p + "dn_post", _dn_post_fn, [o, z], [sm["dn_out_norm"]], [(wide, BF16)])
    y = _mm(p + "dn_out", on, w["dn_w_out"], "nn", add=x)
    return y, (x, h, proj, conv, q, k, v, g, beta, o, states, inverses, on)


def _dn_layer_bwd(p, saved, dy, w, sm, grads):
    x, h, proj, conv, q, k, v, g, beta, o, states, inverses, on = saved
    wide = N_HEADS * HEAD
    don = _mm(p + "dn_out_dx", dy, w["dn_w_out"], "nt")
    grads["dn_w_out"] = _mm(p + "dn_out_dw", on, dy, "tn")
    (do, dz), (d_out_norm,) = _ew_bwd(p + "dn_post_bwd", _dn_post_fn, [o, _V(proj, wide, 3)], [sm["dn_out_norm"]], [don],
                                          ct_dtypes=[F32, BF16])
    grads["dn_out_norm"] = d_out_norm
    dq, dk, dv, dg, db = _dn_bwd(p + "dn_core_bwd", q, k, v, g, beta, states, inverses, do)
    (dconv, dab), (d_a_log, d_dt) = _ew_bwd(p + "dn_pre_bwd", _dn_pre_fn, [conv, _V(proj, LANES, 4 * N_HEADS)],
                                            [sm["dn_a_log"], sm["dn_dt_bias"]], [dq, dk, dv, dg, db], ct_dtypes=[F32, BF16])
    grads["dn_a_log"], grads["dn_dt_bias"] = d_a_log, d_dt
    dqkv, dconv_w = _conv_bwd(p + "dn_conv_bwd", proj, sm["dn_conv_w"], dconv)
    grads["dn_conv_w"] = dconv_w
    dproj = jnp.concatenate([dqkv, dz, dab], axis=1)
    dh = _mm(p + "dn_in_dx", dproj, w["dn_w_in"], "nt")
    grads["dn_w_in"] = _mm(p + "dn_in_dw", h, dproj, "tn")
    (dx,), (dgain,) = _ew_bwd(p + "mix_norm_bwd", _norm_fn, [x], [sm["mix_norm"]], [dh], add=dy)
    grads["mix_norm"] = dgain
    return dx


def _sb_layer_fwd(p, x, w, sm):
    h, = _ew(p + "mix_norm", _norm_fn, [x], [sm["mix_norm"]], [(D_MODEL, BF16)])
    qkv = _mm(p + "sb_qkv", h, w["sb_w_qkv"], "nn")
    wide = N_HEADS * HEAD
    q, k, v = _ew(p + "sb_pre", _sb_pre_fn, [qkv], [sm["sb_q_norm"], sm["sb_k_norm"]], [(wide, BF16)] * 3)
    o, stat = _attn_fwd(p + "sb_core", "sb", [q], [(k, False)], v)
    y = _mm(p + "sb_out", o, w["sb_w_out"], "nn", add=x)
    return y, (x, h, qkv, q, k, v, o, stat)


def _sb_layer_bwd(p, saved, dy, w, sm, grads, exchange=None):
    x, h, qkv, q, k, v, o, stat = saved
    do = _mm(p + "sb_out_dx", dy, w["sb_w_out"], "nt")
    grads["sb_w_out"] = _mm(p + "sb_out_dw", o, dy, "tn")
    send = None if exchange is None else exchange.start(grads)
    (dq,), (dk,), dv, *arrived = _attn_bwd(p + "sb_core_bwd", "sb", [q], [(k, False)], v, o, stat, do, send=send)
    if exchange is not None:
        exchange.arrived = arrived[0]
    (dqkv,), (dqn, dkn) = _ew_bwd(p + "sb_pre_bwd", _sb_pre_fn, [qkv], [sm["sb_q_norm"], sm["sb_k_norm"]], [dq, dk, dv],
                                  ct_dtypes=[BF16])
    grads["sb_q_norm"], grads["sb_k_norm"] = dqn, dkn
    dh = _mm(p + "sb_qkv_dx", dqkv, w["sb_w_qkv"], "nt")
    grads["sb_w_qkv"] = _mm(p + "sb_qkv_dw", h, dqkv, "tn")
    (dx,), (dgain,) = _ew_bwd(p + "mix_norm_bwd", _norm_fn, [x], [sm["mix_norm"]], [dh], add=dy)
    grads["mix_norm"] = dgain
    return dx


def _mla_layer_fwd(p, x, w, sm):
    t = x.shape[0]
    tabs = [_V(a, diff=False) for a in _rope_tables(t)]
    h, = _ew(p + "mix_norm", _norm_fn, [x], [sm["mix_norm"]], [(D_MODEL, BF16)])
    down = _mm(p + "mla_down", h, w["mla_w_down"], "nn")
    sm_a = [sm["mla_q_a_norm"], sm["mla_kv_a_norm"], sm["mla_k_rope_norm"]]
    cq, ckv, kr = _ew(p + "mla_a", _mla_a_fn, [down] + tabs, sm_a, [(256, BF16), (128, BF16), (128, BF16)])
    qf = _mm(p + "mla_uq", cq, w["mla_w_uq"], "nn")
    kvf = _mm(p + "mla_ukv", ckv, w["mla_w_ukv"], "nn")
    sm_b = [sm["mla_q_nope_norm"], sm["mla_q_rope_norm"], sm["mla_k_nope_norm"]]
    wide = N_HEADS * HEAD
    qn, qr, kn, v = _ew(p + "mla_b", _mla_b_fn, [qf, kvf] + tabs, sm_b, [(wide, BF16)] * 4)
    o, stat = _attn_fwd(p + "mla_core", "softmax", [qn, qr], [(kn, False), (kr, True)], v)
    y = _mm(p + "mla_out", o, w["mla_w_out"], "nn", add=x)
    return y, (x, h, down, cq, ckv, kr, qf, kvf, qn, qr, kn, v, o, stat)


def _mla_layer_bwd(p, saved, dy, w, sm, grads):
    x, h, down, cq, ckv, kr, qf, kvf, qn, qr, kn, v, o, stat = saved
    tabs = [_V(a, diff=False) for a in _rope_tables(x.shape[0])]
    do = _mm(p + "mla_out_dx", dy, w["mla_w_out"], "nt")
    grads["mla_w_out"] = _mm(p + "mla_out_dw", o, dy, "tn")
    (dqn, dqr), (dkn, dkr), dv = _attn_bwd(p + "mla_core_bwd", "softmax", [qn, qr], [(kn, False), (kr, True)],
                                           v, o, stat, do)
    sm_b = [sm["mla_q_nope_norm"], sm["mla_q_rope_norm"], sm["mla_k_nope_norm"]]
    (dqf, dkvf), dsm_b = _ew_bwd(p + "mla_b_bwd", _mla_b_fn, [qf, kvf] + tabs, sm_b, [dqn, dqr, dkn, dv],
                                 ct_dtypes=[BF16, BF16])
    grads["mla_q_nope_norm"], grads["mla_q_rope_norm"], grads["mla_k_nope_norm"] = dsm_b
    dcq = _mm(p + "mla_uq_dx", dqf, w["mla_w_uq"], "nt")
    grads["mla_w_uq"] = _mm(p + "mla_uq_dw", cq, dqf, "tn")
    dckv = _mm(p + "mla_ukv_dx", dkvf, w["mla_w_ukv"], "nt")
    grads["mla_w_ukv"] = _mm(p + "mla_ukv_dw", ckv, dkvf, "tn")
    sm_a = [sm["mla_q_a_norm"], sm["mla_kv_a_norm"], sm["mla_k_rope_norm"]]
    (ddown,), dsm_a = _ew_bwd(p + "mla_a_bwd", _mla_a_fn, [down] + tabs, sm_a, [dcq, dckv, dkr], ct_dtypes=[BF16])
    grads["mla_q_a_norm"], grads["mla_kv_a_norm"], grads["mla_k_rope_norm"] = dsm_a
    dh = _mm(p + "mla_down_dx", ddown, w["mla_w_down"], "nt")
    grads["mla_w_down"] = _mm(p + "mla_down_dw", h, ddown, "tn")
    (dx,), (dgain,) = _ew_bwd(p + "mix_norm_bwd", _norm_fn, [x], [sm["mix_norm"]], [dh], add=dy)
    grads["mix_norm"] = dgain
    return dx


_MIX_FWD = (_dn_layer_fwd, _sb_layer_fwd, _mla_layer_fwd)
_MIX_BWD = (_dn_layer_bwd, _sb_layer_bwd, _mla_layer_bwd)


def _pad_cols(a, n):
    return jnp.pad(a, ((0, 0), (0, n - a.shape[1])))


def _prep_big(name, a):
    if name.endswith("dn_w_in"):
        return _pad_cols(a, 4 * N_HEADS * HEAD + LANES)
    if name.endswith("mla_w_down"):
        return _pad_cols(a, 512)
    if name.endswith("mla_w_uq"):
        a3 = a.reshape(a.shape[0], N_HEADS, MLA_QK)
        return jnp.pad(a3, ((0, 0), (0, 0), (0, 2 * HEAD - MLA_QK))).reshape(a.shape[0], N_HEADS * 2 * HEAD)
    return a


def _unprep_big(name, g):
    if name.endswith("dn_w_in"):
        return g[:, :4 * N_HEADS * HEAD + 2 * N_HEADS]
    if name.endswith("mla_w_down"):
        return g[:, :448]
    if name.endswith("mla_w_uq"):
        return g.reshape(g.shape[0], N_HEADS, 2 * HEAD)[:, :, :MLA_QK].reshape(g.shape[0], N_HEADS * MLA_QK)
    return g


def _prep_small(name, a):
    if name.endswith("dn_conv_w"):
        return jnp.pad(a.astype(F32), ((0, HALO - a.shape[0]), (0, 0)))
    if name.endswith(("dn_a_log", "dn_dt_bias", "mla_q_rope_norm", "mla_k_rope_norm")):
        return _row(a, LANES)
    return _row(a)


def _unprep_small(name, g, like):
    if name.endswith("dn_conv_w"):
        return g[:like.shape[0]]
    return g.reshape(-1)[:like.shape[0]]


EARLY_LAYER = 1


def local_step(x, target, big, small, exchange=None):
    layers = []
    for i in range(DEPTH):
        p = "l%d_" % i
        w = {n: _prep_big(n, big[p + n]) for n in _MIXERS[i % 3] + ("ffn_w_gate_up", "ffn_w_down") if p + n in big}
        sm = {n: _prep_small(n, small[p + n]) for n in _MIXERS[i % 3] + ("mix_norm", "ffn_norm") if p + n in small}
        layers.append((p, w, sm))
    saved = []
    for i, (p, w, sm) in enumerate(layers):
        x, s_mix = _MIX_FWD[i % 3](p, x, w, sm)
        x, s_ffn = _ffn_fwd(p, x, w, sm)
        saved.append((s_mix, s_ffn))
    part, dx = _loss_kernel(x, target)
    grads = {}
    for i in reversed(range(DEPTH)):
        p, w, sm = layers[i]
        g = {}
        dx = _ffn_bwd(p, saved[i][1], dx, w, sm, g)
        if exchange is not None and i == EARLY_LAYER:
            exchange.grads = grads
            dx = _MIX_BWD[i % 3](p, saved[i][0], dx, w, sm, g, exchange)
        else:
            dx = _MIX_BWD[i % 3](p, saved[i][0], dx, w, sm, g)
        for n, val in g.items():
            grads[p + n] = _unprep_big(p + n, val) if p + n in big else _unprep_small(p + n, val, small[p + n])
    return part, dx, grads


ROW = 1024
BIG_ROWS = 1024


PACK_ALIGN = 16


def _packed_rows(shape):
    return -(-math.prod(shape) // (ROW * PACK_ALIGN)) * PACK_ALIGN


def _as_rows(a, lead=()):
    rows = _packed_rows(a.shape[len(lead):])
    flat = a.reshape(lead + (-1,))
    return jnp.pad(flat, ((0, 0),) * len(lead) + ((0, rows * ROW - flat.shape[-1]),)).reshape(lead + (rows, ROW))


def _pack(arrs, rows_multiple, lead=()):
    blocks = [_as_rows(a, lead) for a in arrs]
    used = sum(b.shape[-2] for b in blocks)
    fill = -(-used // rows_multiple) * rows_multiple - used
    if fill:
        blocks.append(jnp.zeros(lead + (fill, ROW), blocks[0].dtype))
    return jnp.concatenate(blocks, axis=len(lead))


def _unpack(buf, shapes, lead=()):
    out, r0 = [], 0
    for s in shapes:
        rows, n = _packed_rows(s), math.prod(s)
        block = lax.slice_in_dim(buf, r0, r0 + rows, axis=len(lead))
        out.append(block.reshape(lead + (-1,))[..., :n].reshape(lead + tuple(s)))
        r0 += rows
    return out


def _me():
    return lax.axis_index("x"), lax.axis_index("y"), lax.axis_index("c")


def _other_chips(x, y):
    return [(1 - x, y), (x, 1 - y), (1 - x, 1 - y)]


HBM = pl.BlockSpec(memory_space=pl.ANY)


OWN_STREAMS = 4


def _gather_shards(packed):
    rows = packed.shape[0]
    half = rows // 2

    def body(x_ref, out_ref, send_sems, recv_sems):
        x, y, c = _me()
        sibling, chips = (x, y, 1 - c), _other_chips(x, y)

        def part(px, py, pc):
            return out_ref.at[2 * px + py, pl.ds(pl.multiple_of(pc * half, 16), half), :]

        def copy(k, block, to, src=None):
            return pltpu.make_async_remote_copy(
                src_ref=part(*block) if src is None else src, dst_ref=part(*block),
                send_sem=send_sems.at[k], recv_sem=recv_sems.at[k], device_id=to, device_id_type=MESH)

        my_half = x_ref.at[pl.ds(pl.multiple_of(c * half, 16), half), :]
        first = [copy(j, (x, y, c), (*chip, c), src=my_half) for j, chip in enumerate(chips)]
        piece = rows // OWN_STREAMS
        for p in range(OWN_STREAMS):
            rows_p = pl.ds(p * piece, piece)
            first.append(pltpu.make_async_remote_copy(
                src_ref=x_ref.at[rows_p, :], dst_ref=out_ref.at[2 * x + y, rows_p, :], send_sem=send_sems.at[6 + p],
                recv_sem=recv_sems.at[6 + p], device_id=sibling, device_id_type=MESH))
        for cp in first:
            cp.start()
        passed = [copy(3 + j, (*chip, c), sibling) for j, chip in enumerate(chips)]
        for j, chip in enumerate(chips):
            copy(j, (*chip, c), (x, y, c)).wait_recv()
            passed[j].start()
        for j, chip in enumerate(chips):
            copy(3 + j, (*chip, 1 - c), (x, y, c)).wait_recv()
        for cp in first[3:]:
            cp.wait_recv()
        for cp in first + passed:
            cp.wait_send()

    n_sems = 6 + OWN_STREAMS
    return pl.pallas_call(
        body, out_shape=jax.ShapeDtypeStruct((N_CHIPS,) + packed.shape, packed.dtype), in_specs=[HBM], out_specs=HBM,
        scratch_shapes=[pltpu.SemaphoreType.DMA((n_sems,)), pltpu.SemaphoreType.DMA((n_sems,))],
        name="gather_weights",
    )(packed)


D2D_STREAMS = 16


def _swap_halves(g, tag):
    n, rows, _ = g.shape
    half = rows // 2
    per = D2D_STREAMS // n
    piece = half // per

    def body(g_ref, theirs_ref, send_sems, recv_sems):
        x, y, c = _me()
        give = (1 - c) * half
        copies = []
        for j in range(n):
            for p in range(per):
                k = j * per + p
                cp = pltpu.make_async_remote_copy(
                    src_ref=g_ref.at[j, pl.ds(pl.multiple_of(give + p * piece, 8), piece), :],
                    dst_ref=theirs_ref.at[j, pl.ds(p * piece, piece), :],
                    send_sem=send_sems.at[k], recv_sem=recv_sems.at[k], device_id=(x, y, 1 - c), device_id_type=MESH)
                cp.start()
                copies.append(cp)
        for cp in copies:
            cp.wait()

    return pl.pallas_call(
        body, out_shape=jax.ShapeDtypeStruct((n, half, ROW), g.dtype), in_specs=[HBM], out_specs=HBM,
        scratch_shapes=[pltpu.SemaphoreType.DMA((D2D_STREAMS,)), pltpu.SemaphoreType.DMA((D2D_STREAMS,))],
        name="grad_swap_halves" + tag,
    )(g)


def _pair_sum(g, theirs, c, tag):
    n, half, _ = theirs.shape
    tm = _tile(half)
    nb = half // tm

    def body(c_ref, g_ref, t_ref, s32_ref, s16_ref):
        s = g_ref[...] + t_ref[...]
        s32_ref[...] = s
        s16_ref[...] = s.astype(BF16)

    blk = pl.BlockSpec((None, tm, ROW), lambda j, i, c_ref: (j, i, 0))
    return pl.pallas_call(
        body,
        grid_spec=pltpu.PrefetchScalarGridSpec(
            num_scalar_prefetch=1, grid=(n, nb),
            in_specs=[pl.BlockSpec((None, tm, ROW), lambda j, i, c_ref: (j, c_ref[0] * nb + i, 0)), blk],
            out_specs=[blk, blk]),
        out_shape=[jax.ShapeDtypeStruct(theirs.shape, F32), jax.ShapeDtypeStruct(theirs.shape, BF16)],
        name="grad_pair_sum" + tag, compiler_params=_params(2),
    )(c.reshape(1).astype(jnp.int32), g, theirs)


def _scatter_chunks(s16):
    _, half, _ = s16.shape

    def body(s16_ref, got_ref, send_sems, recv_sems):
        x, y, c = _me()
        sends = []
        for j, (px, py) in enumerate(_other_chips(x, y)):
            cp = pltpu.make_async_remote_copy(src_ref=s16_ref.at[2 * px + py], dst_ref=got_ref.at[j],
                                              send_sem=send_sems.at[j], recv_sem=recv_sems.at[j],
                                              device_id=(px, py, c), device_id_type=MESH)
            cp.start()
            sends.append(cp)
        for cp in sends:
            cp.wait()

    return pl.pallas_call(
        body, out_shape=jax.ShapeDtypeStruct((3, half, ROW), BF16), in_specs=[HBM], out_specs=HBM,
        scratch_shapes=[pltpu.SemaphoreType.DMA((3,)), pltpu.SemaphoreType.DMA((3,))],
        name="grad_scatter",
    )(s16)


def _chip_sum(s32, got, chip, c, tag):
    _, half, _ = s32.shape
    tm = _tile(half)
    nb = half // tm

    def body(where_ref, own_ref, g0_ref, g1_ref, g2_ref, o_ref):
        o_ref[...] = ((own_ref[...] + g0_ref[...].astype(F32)) + g1_ref[...].astype(F32)) + g2_ref[...].astype(F32)

    got_spec = lambda k: pl.BlockSpec((None, tm, ROW), lambda i, where_ref, k=k: (k, i, 0))
    return pl.pallas_call(
        body,
        grid_spec=pltpu.PrefetchScalarGridSpec(
            num_scalar_prefetch=1, grid=(nb,),
            in_specs=[pl.BlockSpec((None, tm, ROW), lambda i, where_ref: (where_ref[0], i, 0)), got_spec(0), got_spec(1), got_spec(2)],
            out_specs=pl.BlockSpec((tm, ROW), lambda i, where_ref: (where_ref[1] * nb + i, 0))),
        out_shape=jax.ShapeDtypeStruct((2 * half, ROW), F32), name="grad_chip_sum" + tag, compiler_params=_params(1),
    )(jnp.stack([chip, c]).astype(jnp.int32), s32, got, got, got)


def _join_halves(f, tag):
    half = f.shape[0] // 2
    piece = half // D2D_STREAMS

    def body(f_ref, out_ref, send_sems, recv_sems):
        x, y, c = _me()
        copies = []
        for p in range(D2D_STREAMS):
            rows = out_ref.at[pl.ds(pl.multiple_of(c * half + p * piece, 8), piece), :]
            cp = pltpu.make_async_remote_copy(src_ref=rows, dst_ref=rows, send_sem=send_sems.at[p], recv_sem=recv_sems.at[p],
                                              device_id=(x, y, 1 - c), device_id_type=MESH)
            cp.start()
            copies.append(cp)
        for cp in copies:
            cp.wait()

    return pl.pallas_call(
        body, out_shape=jax.ShapeDtypeStruct(f.shape, F32), in_specs=[HBM], out_specs=HBM, input_output_aliases={0: 0},
        scratch_shapes=[pltpu.SemaphoreType.DMA((D2D_STREAMS,)), pltpu.SemaphoreType.DMA((D2D_STREAMS,))],
        name="grad_join_halves" + tag,
    )(f)


def _all_reduce_small(name, v):
    rows = v.shape[0]

    def body(v_ref, out_ref, slots, send_sems, recv_sems):
        x, y, c = _me()
        me = 4 * x + 2 * y + c
        slots[me] = v_ref[...]
        sends = []
        for r in range(1, 8):
            to = (x ^ (r >> 2), y ^ ((r >> 1) & 1), c ^ (r & 1))
            cp = pltpu.make_async_remote_copy(src_ref=v_ref, dst_ref=slots.at[me], send_sem=send_sems.at[r - 1],
                                              recv_sem=recv_sems.at[r - 1], device_id=to, device_id_type=MESH)
            cp.start()
            sends.append(cp)
        for cp in sends:
            cp.wait()
        total = slots[0]
        for d in range(1, 8):
            total = total + slots[d]
        out_ref[...] = total

    vmem = pl.BlockSpec(memory_space=pltpu.VMEM)
    return pl.pallas_call(
        body, out_shape=jax.ShapeDtypeStruct(v.shape, F32), in_specs=[vmem], out_specs=vmem,
        scratch_shapes=[pltpu.VMEM((8, rows, ROW), F32), pltpu.SemaphoreType.DMA((7,)), pltpu.SemaphoreType.DMA((7,))],
        name=name,
    )(v)


def _adam_fn(w, g, m, v):
    m2 = ADAM_B1 * m + (1.0 - ADAM_B1) * g
    v2 = ADAM_B2 * v + (1.0 - ADAM_B2) * (g * g)
    m_hat = m2 / (1.0 - ADAM_B1 ** ADAM_STEP)
    v_hat = v2 / (1.0 - ADAM_B2 ** ADAM_STEP)
    return -ADAM_LR * (m_hat / (jnp.sqrt(v_hat) + ADAM_EPS) + ADAM_WD * w), m2, v2


def _full_shape(name, shard_shape):
    ax = _shard_axis(name)
    return tuple(n * N_CHIPS if k == ax else n for k, n in enumerate(shard_shape))


def _chip_major(name, full):
    if _shard_axis(name) == 0:
        return full.reshape(N_CHIPS, -1, full.shape[1])
    n = full.shape[1] // N_CHIPS
    return jnp.stack([full[:, j * n:(j + 1) * n] for j in range(N_CHIPS)])


def _from_chip_major(name, shards):
    if _shard_axis(name) == 0:
        return shards.reshape(-1, shards.shape[2])
    return jnp.concatenate([shards[j] for j in range(N_CHIPS)], axis=1)


def _row_tile(rows, cap=512):
    return max(t for t in range(8, min(rows, cap) + 1, 8) if rows % t == 0)


def _step(a):
    x_i, y_i, c_i = _me()
    chip = 2 * x_i + y_i
    big_shapes = [a[n].shape for n in BIG]

    packed = _pack([a[n].astype(BF16) for n in BIG], BIG_ROWS)
    shards = _unpack(_gather_shards(packed), big_shapes, lead=(N_CHIPS,))
    big = {n: _from_chip_major(n, sh) for n, sh in zip(BIG, shards)}

    small = {n: a[n] for n in SMALL}
    convs = [n for n in SMALL if n.endswith("dn_conv_w")]
    placed = []
    for n in convs:
        full = jnp.zeros(_full_shape_conv(a[n].shape), F32)
        placed.append(lax.dynamic_update_slice(full, a[n], (0, chip * a[n].shape[1])))
    conv_sum = _all_reduce_small("gather_conv", _pack(placed, 8))
    for n, full in zip(convs, _unpack(conv_sum, [p.shape for p in placed])):
        small[n] = full * 0.5

    def pair_sums(names, grads, tag):
        g_all = _pack([_chip_major(n, grads[n]) for n in names], BIG_ROWS, lead=(N_CHIPS,))
        return _pair_sum(g_all, _swap_halves(g_all, tag), c_i, tag)

    def reduced(names, s32, got, tag):
        whole = _join_halves(_chip_sum(s32, got, chip, c_i, tag), tag)
        return dict(zip(names, _unpack(whole, [a[n].shape for n in names])))

    early_prefix = "l%d_" % EARLY_LAYER
    early = [n for n in BIG if int(n[1]) > EARLY_LAYER or n in (early_prefix + "ffn_w_gate_up", early_prefix + "ffn_w_down",
                                                                early_prefix + "sb_w_out")]
    late = [n for n in BIG if n not in early]

    class Exchange:
        def start(self, layer_grads):
            have = dict(self.grads, **{early_prefix + k: val for k, val in layer_grads.items()})
            self.s32, s16 = pair_sums(early, have, "_early")
            return s16

    exchange = Exchange()
    part, dx, grads = local_step(a["x"][0], a["loss_target"][0], big, small, exchange)
    loss = lax.psum(0.5 * jnp.sum(part) / D_MODEL, ("x", "y", "c"))
    g_big = reduced(early, exchange.s32, exchange.arrived, "_early")
    s32, s16 = pair_sums(late, grads, "_late")
    g_big.update(reduced(late, s32, _scatter_chunks(s16), "_late"))

    g_small_full = _all_reduce_small("reduce_small", _pack([grads[n] for n in SMALL], 8))
    g_small = dict(zip(SMALL, _unpack(g_small_full, [grads[n].shape for n in SMALL])))
    for n in convs:
        g_small[n] = lax.dynamic_slice_in_dim(g_small[n], chip * a[n].shape[1], a[n].shape[1], axis=1)

    outs = {}
    for n in BIG:
        d, m2, v2 = _ew("adam_" + n, _adam_fn, [a[n], g_big[n], a["m_" + n], a["v_" + n]], [],
                        [(a[n].shape[1], F32)] * 3, tm=_row_tile(a[n].shape[0]))
        outs.update({"grad_" + n: g_big[n], "delta_" + n: d, "new_m_" + n: m2, "new_v_" + n: v2})
    pk = lambda prefix: _pack([a[prefix + n] for n in SMALL], 8)
    gs_packed = _pack([g_small[n] for n in SMALL], 8)
    small_bufs = (gs_packed,) + tuple(_ew("adam_small", _adam_fn, [pk(""), gs_packed, pk("m_"), pk("v_")], [], [(ROW, F32)] * 3,
                                          tm=gs_packed.shape[0]))
    small_shapes = [a[n].shape for n in SMALL]
    for key, buf in zip(("grad_", "delta_", "new_m_", "new_v_"), small_bufs):
        outs.update({key + n: val for n, val in zip(SMALL, _unpack(buf, small_shapes))})
    result = [loss, dx[None]]
    for key in ("grad_", "delta_", "new_m_", "new_v_"):
        result += [outs[key + n] for n in WEIGHTS]
    return tuple(result)


def _full_shape_conv(shard_shape):
    return (shard_shape[0], shard_shape[1] * N_CHIPS)


def kernel(x, l0_mix_norm, l0_dn_w_in, l0_dn_conv_w, l0_dn_a_log, l0_dn_dt_bias, l0_dn_out_norm, l0_dn_w_out, l0_ffn_norm, l0_ffn_w_gate_up, l0_ffn_w_down, l1_mix_norm, l1_sb_w_qkv, l1_sb_q_norm, l1_sb_k_norm, l1_sb_w_out, l1_ffn_norm, l1_ffn_w_gate_up, l1_ffn_w_down, l2_mix_norm, l2_mla_w_down, l2_mla_q_a_norm, l2_mla_kv_a_norm, l2_mla_w_uq, l2_mla_w_ukv, l2_mla_q_nope_norm, l2_mla_q_rope_norm, l2_mla_k_nope_norm, l2_mla_k_rope_norm, l2_mla_w_out, l2_ffn_norm, l2_ffn_w_gate_up, l2_ffn_w_down, l3_mix_norm, l3_dn_w_in, l3_dn_conv_w, l3_dn_a_log, l3_dn_dt_bias, l3_dn_out_norm, l3_dn_w_out, l3_ffn_norm, l3_ffn_w_gate_up, l3_ffn_w_down, loss_target, m_l0_mix_norm, m_l0_dn_w_in, m_l0_dn_conv_w, m_l0_dn_a_log, m_l0_dn_dt_bias, m_l0_dn_out_norm, m_l0_dn_w_out, m_l0_ffn_norm, m_l0_ffn_w_gate_up, m_l0_ffn_w_down, m_l1_mix_norm, m_l1_sb_w_qkv, m_l1_sb_q_norm, m_l1_sb_k_norm, m_l1_sb_w_out, m_l1_ffn_norm, m_l1_ffn_w_gate_up, m_l1_ffn_w_down, m_l2_mix_norm, m_l2_mla_w_down, m_l2_mla_q_a_norm, m_l2_mla_kv_a_norm, m_l2_mla_w_uq, m_l2_mla_w_ukv, m_l2_mla_q_nope_norm, m_l2_mla_q_rope_norm, m_l2_mla_k_nope_norm, m_l2_mla_k_rope_norm, m_l2_mla_w_out, m_l2_ffn_norm, m_l2_ffn_w_gate_up, m_l2_ffn_w_down, m_l3_mix_norm, m_l3_dn_w_in, m_l3_dn_conv_w, m_l3_dn_a_log, m_l3_dn_dt_bias, m_l3_dn_out_norm, m_l3_dn_w_out, m_l3_ffn_norm, m_l3_ffn_w_gate_up, m_l3_ffn_w_down, v_l0_mix_norm, v_l0_dn_w_in, v_l0_dn_conv_w, v_l0_dn_a_log, v_l0_dn_dt_bias, v_l0_dn_out_norm, v_l0_dn_w_out, v_l0_ffn_norm, v_l0_ffn_w_gate_up, v_l0_ffn_w_down, v_l1_mix_norm, v_l1_sb_w_qkv, v_l1_sb_q_norm, v_l1_sb_k_norm, v_l1_sb_w_out, v_l1_ffn_norm, v_l1_ffn_w_gate_up, v_l1_ffn_w_down, v_l2_mix_norm, v_l2_mla_w_down, v_l2_mla_q_a_norm, v_l2_mla_kv_a_norm, v_l2_mla_w_uq, v_l2_mla_w_ukv, v_l2_mla_q_nope_norm, v_l2_mla_q_rope_norm, v_l2_mla_k_nope_norm, v_l2_mla_k_rope_norm, v_l2_mla_w_out, v_l2_ffn_norm, v_l2_ffn_w_gate_up, v_l2_ffn_w_down, v_l3_mix_norm, v_l3_dn_w_in, v_l3_dn_conv_w, v_l3_dn_a_log, v_l3_dn_dt_bias, v_l3_dn_out_norm, v_l3_dn_w_out, v_l3_ffn_norm, v_l3_ffn_w_gate_up, v_l3_ffn_w_down):
    return _step(dict(locals()))
```

```python
import functools
import math

import jax
import jax.numpy as jnp
from jax import lax
from jax.experimental import pallas as pl
from jax.experimental.pallas import tpu as pltpu

F32, BF16 = jnp.float32, jnp.bfloat16
MESH = pl.DeviceIdType.MESH

D_MODEL = 1024
N_HEADS = 8
HEAD = 128
FFN_HIDDEN = 2816
DN_CHUNK = 64
NORM_EPS = 1e-6
MLA_ROPE = 64
MLA_QK = 192
ROPE_THETA = 10000.0
ADAM_LR, ADAM_B1, ADAM_B2, ADAM_EPS, ADAM_WD, ADAM_STEP = 0.001, 0.9, 0.999, 1e-08, 0.01, 10
N_CHIPS = 4
LANES = 128
VMEM_LIMIT = 56 * 2 ** 20


def _params(n_grid):
    return pltpu.CompilerParams(dimension_semantics=("arbitrary",) * n_grid, vmem_limit_bytes=VMEM_LIMIT)


_MIXERS = (
    ("dn_w_in", "dn_conv_w", "dn_a_log", "dn_dt_bias", "dn_out_norm", "dn_w_out"),
    ("sb_w_qkv", "sb_q_norm", "sb_k_norm", "sb_w_out"),
    ("mla_w_down", "mla_q_a_norm", "mla_kv_a_norm", "mla_w_uq", "mla_w_ukv", "mla_q_nope_norm",
     "mla_q_rope_norm", "mla_k_nope_norm", "mla_k_rope_norm", "mla_w_out"),
)
DEPTH = 4


def _layer_names(i):
    p = "l%d_" % i
    return [p + "mix_norm"] + [p + n for n in _MIXERS[i % 3]] + [p + "ffn_norm", p + "ffn_w_gate_up", p + "ffn_w_down"]


WEIGHTS = [n for i in range(DEPTH) for n in _layer_names(i)]
_ROW_SHARDED = ("w_out", "ffn_w_down", "mla_w_down")
_COL_SHARDED = ("dn_w_in", "sb_w_qkv", "mla_w_uq", "mla_w_ukv", "ffn_w_gate_up")


def _shard_axis(name):
    if name.endswith(_ROW_SHARDED):
        return 0
    if name.endswith(_COL_SHARDED):
        return 1
    return None


BIG = [n for n in WEIGHTS if _shard_axis(n) is not None]
SMALL = [n for n in WEIGHTS if _shard_axis(n) is None]


_DN = {"nn": (((1,), (0,)), ((), ())), "nt": (((1,), (1,)), ((), ())), "tn": (((0,), (0,)), ((), ()))}
_DN_BATCHED = {"nn": (((2,), (1,)), ((0,), (0,))), "nt": (((2,), (2,)), ((0,), (0,))), "tn": (((1,), (1,)), ((0,), (0,)))}


def _dims(a, kind):
    return _DN_BATCHED[kind] if a.ndim == 3 else _DN[kind]


def _dg(a, b, kind):
    return lax.dot_general(a.astype(BF16), b.astype(BF16), _dims(a, kind), preferred_element_type=F32)


@functools.partial(jax.custom_vjp, nondiff_argnums=(2,))
def bdot(a, b, kind):
    return _dg(a, b, kind)


def _bdot_fwd(a, b, kind):
    return _dg(a, b, kind), (a, b)


def _bdot_bwd(kind, res, ct):
    a, b = res
    if kind == "nn":
        return _dg(ct, b, "nt"), _dg(a, ct, "tn")
    if kind == "nt":
        return _dg(ct, b, "nn"), _dg(ct, a, "tn")
    return _dg(b, ct, "nt"), _dg(a, ct, "nn")


bdot.defvjp(_bdot_fwd, _bdot_bwd)


def _split(a, terms):
    out = []
    for _ in range(terms):
        hi = a.astype(BF16)
        out.append(hi)
        a = a - hi.astype(F32)
    return out


def _xdot(a, b, kind, exact, terms=3):
    if exact == 0:
        return sum(lax.dot_general(a, p, _dims(a, kind), preferred_element_type=F32) for p in _split(b, terms))
    return sum(lax.dot_general(p, b, _dims(a, kind), preferred_element_type=F32) for p in _split(a, terms))


def _tri(n, rel):
    r = lax.broadcasted_iota(jnp.int32, (n, n), 0)
    c = lax.broadcasted_iota(jnp.int32, (n, n), 1)
    return {"le": c <= r, "lt": c < r, "ge": c >= r, "gt": c > r}[rel]


def _running(g):
    n = g.shape[-2]
    return jnp.broadcast_to(_tri(n, "le").astype(BF16), g.shape[:-2] + (n, n))


@jax.custom_vjp
def cumsum_rows(g):
    return _xdot(_running(g), g, "nn", 0)


def _cumsum_fwd(g):
    return cumsum_rows(g), None


def _cumsum_bwd(_, ct):
    return (_xdot(_running(ct), ct, "tn", 0),)


cumsum_rows.defvjp(_cumsum_fwd, _cumsum_bwd)


def _dot3(a, b, kind):
    (ah, al), (bh, bl) = _split(a, 2), _split(b, 2)
    dot = lambda p, q: lax.dot_general(p, q, _dims(a, kind), preferred_element_type=F32)
    return dot(ah, bh) + (dot(ah, bl) + dot(al, bh))


@functools.partial(jax.custom_vjp, nondiff_argnums=(2,))
def _hdot3(a, b, kind):
    return _dot3(a, b, kind)


def _hdot3_fwd(a, b, kind):
    return _dot3(a, b, kind), (a, b)


def _hdot3_bwd(kind, res, ct):
    a, b = res
    if kind == "nn":
        return _dot3(ct, b, "nt"), _dot3(a, ct, "tn")
    if kind == "nt":
        return _dot3(ct, b, "nn"), _dot3(ct, a, "tn")
    return _dot3(b, ct, "nt"), _dot3(a, ct, "nn")


_hdot3.defvjp(_hdot3_fwd, _hdot3_bwd)


def _hdot(a, b):
    return _hdot3(a, b, "nn")


def _unit_lower_inverse(lower):
    n = lower.shape[-1]
    eye = (lax.broadcasted_iota(jnp.int32, (n, n), 0) == lax.broadcasted_iota(jnp.int32, (n, n), 1)).astype(F32)
    m = -lower
    p = eye + m
    for _ in range(int(math.log2(n)) - 1):
        m = _hdot(m, m)
        p = p + _hdot(p, m)
    return p


def _rms(x, g, n=None):
    n = x.shape[-1] if n is None else n
    return x * lax.rsqrt(jnp.sum(x * x, axis=-1, keepdims=True) * (1.0 / n) + NORM_EPS) * g


def _l2(x):
    return x * lax.rsqrt(jnp.sum(x * x, axis=-1, keepdims=True) + NORM_EPS)


def _silu(x):
    return x * jax.nn.sigmoid(x)


def _logsig(z):
    return jnp.minimum(z, 0.0) - jnp.log1p(jnp.exp(-jnp.abs(z)))


@jax.custom_vjp
def _rope(x, cos, sin_lo, sin_hi):
    return x * cos + pltpu.roll(x, 96, 1) * sin_lo + pltpu.roll(x, 32, 1) * sin_hi


def _rope_fwd(x, cos, sin_lo, sin_hi):
    return _rope(x, cos, sin_lo, sin_hi), (cos, sin_lo, sin_hi)


def _rope_bwd(res, ct):
    cos, sin_lo, sin_hi = res
    dx = ct * cos + pltpu.roll(ct * sin_lo, 32, 1) + pltpu.roll(ct * sin_hi, 96, 1)
    return dx, jnp.zeros_like(cos), jnp.zeros_like(sin_lo), jnp.zeros_like(sin_hi)


_rope.defvjp(_rope_fwd, _rope_bwd)


def _tile(n, prefs=(512, 384, 256, 128)):
    for t in prefs:
        if n % t == 0:
            return t
    return n


MM_OUT_TILES = (1024, 1408, 512, 384, 256, 128)
MM_K_TILES = (1024, 512, 384, 256, 128)


def _mm(name, a, b, kind, out_dtype=F32, add=None):
    if kind == "tn":
        (kdim, m), n = a.shape, b.shape[1]
    else:
        (m, kdim), n = a.shape, (b.shape[0] if kind == "nt" else b.shape[1])
    tm, tn, tk = _tile(m, MM_OUT_TILES), _tile(n, MM_OUT_TILES), _tile(kdim, MM_K_TILES)
    nk = kdim // tk
    a_spec = pl.BlockSpec((tk, tm), lambda i, j, k: (k, i)) if kind == "tn" else pl.BlockSpec((tm, tk), lambda i, j, k: (i, k))
    b_spec = pl.BlockSpec((tn, tk), lambda i, j, k: (j, k)) if kind == "nt" else pl.BlockSpec((tk, tn), lambda i, j, k: (k, j))
    o_spec = pl.BlockSpec((tm, tn), lambda i, j, k: (i, j))
    has_add = add is not None

    def body(*refs):
        a_ref, b_ref = refs[0], refs[1]
        o_ref, acc = refs[-2], refs[-1]
        k = pl.program_id(2)

        @pl.when(k == 0)
        def _():
            acc[...] = jnp.zeros_like(acc)

        acc[...] += _dg(a_ref[...], b_ref[...], kind)

        @pl.when(k == nk - 1)
        def _():
            r = acc[...]
            if has_add:
                r = r + refs[2][...]
            o_ref[...] = r.astype(o_ref.dtype)

    return pl.pallas_call(
        body, grid=(m // tm, n // tn, nk),
        in_specs=[a_spec, b_spec] + ([o_spec] if has_add else []),
        out_specs=o_spec, out_shape=jax.ShapeDtypeStruct((m, n), out_dtype),
        scratch_shapes=[pltpu.VMEM((tm, tn), F32)], name=name, compiler_params=_params(3),
    )(*([a, b] + ([add] if has_add else [])))


class _V:
    def __init__(self, arr, w=None, base=0, diff=True):
        self.arr, self.base, self.diff = arr, base, diff
        self.w = arr.shape[1] if w is None else w

    def spec(self, tm):
        return pl.BlockSpec((tm, self.w), lambda i, b=self.base: (i, b))


def _as_views(ins):
    return [v if isinstance(v, _V) else _V(v) for v in ins]


def _tup(r):
    return tuple(r) if isinstance(r, (tuple, list)) else (r,)


def _ew(name, fn, ins, smalls, outs, tm=256):
    ins = _as_views(ins)
    t = ins[0].arr.shape[0]
    tm = min(tm, t)
    n_in = len(ins) + len(smalls)

    def body(*refs):
        res = _tup(fn(*[r[...] for r in refs[:n_in]]))
        for r, o in zip(refs[n_in:], res):
            r[...] = o.astype(r.dtype)

    return pl.pallas_call(
        body, grid=(t // tm,),
        in_specs=[v.spec(tm) for v in ins] + [pl.BlockSpec(s.shape, lambda i: (0, 0)) for s in smalls],
        out_specs=[pl.BlockSpec((tm, w), lambda i: (i, 0)) for w, _ in outs],
        out_shape=[jax.ShapeDtypeStruct((t, w), dt) for w, dt in outs],
        name=name, compiler_params=_params(1),
    )(*[v.arr for v in ins], *smalls)


def _ew_bwd(name, fn, ins, smalls, cts, tm=256, add=None, ct_dtypes=None):
    ins = _as_views(ins)
    t = ins[0].arr.shape[0]
    tm = min(tm, t)
    n_in, n_sm = len(ins), len(smalls)
    diff = [k for k, v in enumerate(ins) if v.diff]
    ct_dtypes = [F32] * len(diff) if ct_dtypes is None else ct_dtypes
    ct_arrs = [c for c in cts if c is not None]
    has_add = add is not None

    def body(*refs):
        vals = [r[...] for r in refs[:n_in]]
        svals = [r[...] for r in refs[n_in:n_in + n_sm]]
        p = n_in + n_sm
        ct_refs = list(refs[p:p + len(ct_arrs)])
        p += len(ct_arrs)
        add_ref = refs[p] if has_add else None
        p += int(has_add)
        din_refs = refs[p:p + len(diff)]
        dsm_refs = refs[p + len(diff):]

        def f(dv, sv):
            full = list(vals)
            for k, d in zip(diff, dv):
                full[k] = d
            return _tup(fn(*full, *sv))

        res, vjp = jax.vjp(f, [vals[k] for k in diff], svals)
        ctv = tuple(jnp.zeros_like(o) if c is None else ct_refs.pop(0)[...].astype(o.dtype) for c, o in zip(cts, res))
        dv, dsv = vjp(ctv)
        for n, (r, d) in enumerate(zip(din_refs, dv)):
            if n == 0 and has_add:
                d = d + add_ref[...]
            r[...] = d.astype(r.dtype)

        @pl.when(pl.program_id(0) == 0)
        def _():
            for r in dsm_refs:
                r[...] = jnp.zeros_like(r)

        for r, d in zip(dsm_refs, dsv):
            r[...] += d

    row = lambda w: pl.BlockSpec((tm, w), lambda i: (i, 0))
    small_specs = [pl.BlockSpec(s.shape, lambda i: (0, 0)) for s in smalls]
    out = pl.pallas_call(
        body, grid=(t // tm,),
        in_specs=[v.spec(tm) for v in ins] + small_specs + [row(c.shape[1]) for c in ct_arrs]
        + ([row(add.shape[1])] if has_add else []),
        out_specs=[row(ins[k].w) for k in diff] + small_specs,
        out_shape=[jax.ShapeDtypeStruct((t, ins[k].w), dt) for k, dt in zip(diff, ct_dtypes)]
        + [jax.ShapeDtypeStruct(s.shape, F32) for s in smalls],
        name=name, compiler_params=_params(1),
    )(*[v.arr for v in ins], *smalls, *ct_arrs, *([add] if has_add else []))
    return out[:len(diff)], out[len(diff):]


BQ = 256
HPB = 2
SUM_TERMS = 2


def _cat(parts):
    return parts[0] if len(parts) == 1 else jnp.concatenate(parts, axis=1)


def _head_view(ref, hh):
    return ref.at[:, hh * HEAD:(hh + 1) * HEAD]


def _attn_specs(qs, ks, t):
    q_specs = [pl.BlockSpec((BQ, HPB * HEAD), lambda h, i: (i, h)) for _ in qs]
    per_head = pl.BlockSpec((t, HPB * HEAD), lambda h, i: (0, h))
    k_specs = [pl.BlockSpec((t, HEAD), lambda h, i: (0, 0)) if sh else per_head for _, sh in ks]
    return q_specs, k_specs, per_head


def _causal_sweep(i, pair, init, diagonal_first):
    order = (lambda s: i - 1 - s) if diagonal_first else (lambda s: s)
    two = lambda s, c: pair(order(2 * s + 1), pair(order(2 * s), c, False), False)
    one = lambda s, c: pair(order(s), c, False)
    carry = pair(i, init, True) if diagonal_first else init
    carry = lax.fori_loop(0, lax.shift_right_logical(i, 1), two, carry)
    carry = lax.fori_loop(i - (i & 1), i, one, carry)
    return carry if diagonal_first else pair(i, carry, True)


def _attn_fwd(name, mode, qs, ks, v, gather=None):
    t = qs[0].shape[0]
    nq, n = t // BQ, len(qs)
    q_specs, k_specs, per_head = _attn_specs(qs, ks, t)
    shared = [sh for _, sh in ks]

    def body(*refs):
        q_refs, k_refs, v_ref = refs[:n], refs[n:2 * n], refs[2 * n]
        n_in = 2 * n + 1 + int(gather is not None)
        o_ref, st_ref = refs[n_in], refs[n_in + 1]
        g, i = pl.program_id(0), pl.program_id(1)
        last_g = N_HEADS // HPB - 1
        if gather is not None:
            phases = _gather_phases(refs[n_in - 1], refs[n_in + 2], refs[-2], refs[-1])
            pl.when((g == 0) & (i == 0))(phases[0])
            pl.when((g == last_g) & (i == 0))(phases[1])
        row = lax.broadcasted_iota(jnp.int32, (BQ, BQ), 0)
        col = lax.broadcasted_iota(jnp.int32, (BQ, BQ), 1)
        after = _tri(BQ, "lt").astype(BF16)

        def head(hh):
            q = _cat([_head_view(r, hh)[...] for r in q_refs])
            k_h = [kr if sh else _head_view(kr, hh) for kr, sh in zip(k_refs, shared)]
            v_h = _head_view(v_ref, hh)

            def pair(j, carry, masked):
                off = pl.multiple_of(j * BQ, BQ)
                z = _dg(q, _cat([kr[pl.ds(off, BQ), :] for kr in k_h]), "nt")
                vj = v_h[pl.ds(off, BQ), :]
                if mode == "sb":
                    acc, run = carry
                    lsz = _logsig(z)
                    stay = lsz - z
                    if masked:
                        stay = jnp.where(col < row, stay, 0.0)
                    a = jnp.exp(lsz + (run + _xdot(stay, after, "nn", 1, SUM_TERMS)))
                    if masked:
                        a = jnp.where(col < row, a, 0.0)
                    return acc + _dg(a, vj, "nn"), run + jnp.sum(stay, axis=1, keepdims=True)
                m, l, acc = carry
                if masked:
                    z = jnp.where(col <= row, z, -1e30)
                m2 = jnp.maximum(m, jnp.max(z, axis=1, keepdims=True))
                p = jnp.exp(z - m2)
                alpha = jnp.exp(m - m2)
                return m2, alpha * l + jnp.sum(p, axis=1, keepdims=True), alpha * acc + _dg(p, vj, "nn")

            def finish(carry):
                if mode == "sb":
                    acc, run = carry
                    _head_view(o_ref, hh)[...] = acc
                    _head_view(st_ref, hh)[...] = jnp.broadcast_to(run, (BQ, HEAD))
                else:
                    m, l, acc = carry
                    _head_view(o_ref, hh)[...] = acc / l
                    _head_view(st_ref, hh)[...] = jnp.broadcast_to(m + jnp.log(l), (BQ, HEAD))

            zero = jnp.zeros((BQ, 1), F32)
            acc0 = jnp.zeros((BQ, HEAD), F32)
            init = (acc0, zero) if mode == "sb" else (jnp.full((BQ, 1), -1e30, F32), zero, acc0)
            return pair, init, finish

        heads = [head(hh) for hh in range(HPB)]
        both = lambda j, carry, masked: tuple(h[0](j, c, masked) for h, c in zip(heads, carry))
        final = _causal_sweep(i, both, tuple(h[1] for h in heads), diagonal_first=(mode == "sb"))
        for h, c in zip(heads, final):
            h[2](c)
        if gather is not None:
            pl.when((g == last_g) & (i == nq - 1))(phases[2])

    blk = pl.BlockSpec((BQ, HPB * HEAD), lambda h, i: (i, h))
    wide = jax.ShapeDtypeStruct((t, N_HEADS * HEAD), F32)
    comm_in, comm_out, comm_shape, comm_scratch = [], [], [], []
    if gather is not None:
        comm_in, comm_out = [gather], [HBM]
        comm_shape = [jax.ShapeDtypeStruct((N_CHIPS,) + gather.shape, gather.dtype)]
        comm_scratch = [pltpu.SemaphoreType.DMA((GATHER_SEMS,)), pltpu.SemaphoreType.DMA((GATHER_SEMS,))]
    return pl.pallas_call(
        body, grid=(N_HEADS // HPB, nq), in_specs=q_specs + k_specs + [per_head] + [HBM] * len(comm_in),
        out_specs=[blk, blk] + comm_out, out_shape=[wide, wide] + comm_shape, scratch_shapes=comm_scratch,
        name=name, compiler_params=_params(2),
    )(*qs, *[k for k, _ in ks], v, *comm_in)


def _attn_bwd(name, mode, qs, ks, v, o, stat, do, send=None):
    t = qs[0].shape[0]
    nq, n = t // BQ, len(qs)
    q_specs, k_specs, per_head = _attn_specs(qs, ks, t)
    shared = [sh for _, sh in ks]

    def body(*refs):
        q_refs, k_refs, v_ref = refs[:n], refs[n:2 * n], refs[2 * n]
        o_ref, st_ref, do_ref = refs[2 * n + 1:2 * n + 4]
        n_in = 2 * n + 4 + int(send is not None)
        dq_refs = refs[n_in:n_in + n]
        dk_refs = refs[n_in + n:n_in + 2 * n]
        dv_ref = refs[n_in + 2 * n]
        g, i = pl.program_id(0), pl.program_id(1)
        if send is not None:
            send_ref, got_ref, send_sems, recv_sems = refs[n_in - 1], refs[n_in + 2 * n + 1], refs[-2], refs[-1]

            def exchange():
                x, y, c = _me()
                return [pltpu.make_async_remote_copy(src_ref=send_ref.at[2 * px + py], dst_ref=got_ref.at[j],
                                                     send_sem=send_sems.at[j], recv_sem=recv_sems.at[j],
                                                     device_id=(px, py, c), device_id_type=MESH)
                        for j, (px, py) in enumerate(_other_chips(x, y))]

            @pl.when((g == 0) & (i == 0))
            def _():
                for cp in exchange():
                    cp.start()

        @pl.when(i == 0)
        def _():
            dv_ref[...] = jnp.zeros_like(dv_ref)
            for r, sh in zip(dk_refs, shared):
                if not sh:
                    r[...] = jnp.zeros_like(r)

        for r, sh in zip(dk_refs, shared):
            if sh:
                @pl.when((i == 0) & (g == 0))
                def _(r=r):
                    r[...] = jnp.zeros_like(r)

        row = lax.broadcasted_iota(jnp.int32, (BQ, BQ), 0)
        col = lax.broadcasted_iota(jnp.int32, (BQ, BQ), 1)
        upto = _tri(BQ, "ge").astype(BF16)
        before = _tri(BQ, "gt").astype(BF16)

        def head(hh):
            q = _cat([_head_view(r, hh)[...] for r in q_refs])
            k_h = [kr if sh else _head_view(kr, hh) for kr, sh in zip(k_refs, shared)]
            dk_h = [r if sh else _head_view(r, hh) for r, sh in zip(dk_refs, shared)]
            v_h, dv_h = _head_view(v_ref, hh), _head_view(dv_ref, hh)
            do_t = _head_view(do_ref, hh)[...]
            st = _head_view(st_ref, hh)[:, :1]
            if mode == "softmax":
                dsum = jnp.sum(do_t * _head_view(o_ref, hh)[...], axis=1, keepdims=True)

            def pair(j, carry, masked):
                off = pl.multiple_of(j * BQ, BQ)
                kj = _cat([kr[pl.ds(off, BQ), :] for kr in k_h])
                z = _dg(q, kj, "nt")
                da = _dg(do_t, v_h[pl.ds(off, BQ), :], "nt")
                if mode == "sb":
                    dq, pre, gpre = carry
                    lsz = _logsig(z)
                    stay = lsz - z
                    if masked:
                        stay = jnp.where(col < row, stay, 0.0)
                    a = jnp.exp(lsz + (st - (pre + _xdot(stay, upto, "nn", 1, SUM_TERMS))))
                    if masked:
                        a = jnp.where(col < row, a, 0.0)
                    gr = a * da
                    sig = jnp.exp(lsz)
                    dz = gr * (1.0 - sig) - sig * (gpre + _xdot(gr, before, "nn", 1, SUM_TERMS))
                    if masked:
                        dz = jnp.where(col < row, dz, 0.0)
                    tail = (pre + jnp.sum(stay, axis=1, keepdims=True), gpre + jnp.sum(gr, axis=1, keepdims=True))
                else:
                    dq = carry[0]
                    a = jnp.exp(z - st)
                    if masked:
                        a = jnp.where(col <= row, a, 0.0)
                    dz = a * (da - dsum)
                    tail = ()
                dk = _dg(dz, q, "tn")
                for p, r in enumerate(dk_h):
                    r[pl.ds(off, BQ), :] += dk[:, p * HEAD:(p + 1) * HEAD]
                dv_h[pl.ds(off, BQ), :] += _dg(a, do_t, "tn")
                return (dq + _dg(dz, kj, "nn"),) + tail

            def finish(carry):
                for p, r in enumerate(dq_refs):
                    _head_view(r, hh)[...] = carry[0][:, p * HEAD:(p + 1) * HEAD]

            zero = jnp.zeros((BQ, 1), F32)
            init = (jnp.zeros((BQ, n * HEAD), F32),) + ((zero, zero) if mode == "sb" else ())
            return pair, init, finish

        heads = [head(hh) for hh in range(HPB)]
        both = lambda j, carry, masked: tuple(h[0](j, c, masked) for h, c in zip(heads, carry))
        final = _causal_sweep(i, both, tuple(h[1] for h in heads), diagonal_first=False)
        for h, c in zip(heads, final):
            h[2](c)

        if send is not None:
            @pl.when((g == N_HEADS // HPB - 1) & (i == nq - 1))
            def _():
                for cp in exchange():
                    cp.wait()

    blk = pl.BlockSpec((BQ, HPB * HEAD), lambda h, i: (i, h))
    dk_specs = [pl.BlockSpec((t, HEAD), lambda h, i: (0, 0)) if sh else per_head for sh in shared]
    wide = jax.ShapeDtypeStruct((t, N_HEADS * HEAD), F32)
    comm_in, comm_out, comm_shape, comm_scratch = [], [], [], []
    if send is not None:
        comm_in, comm_out = [send], [HBM]
        comm_shape = [jax.ShapeDtypeStruct((3,) + send.shape[1:], send.dtype)]
        comm_scratch = [pltpu.SemaphoreType.DMA((3,)), pltpu.SemaphoreType.DMA((3,))]
    out = pl.pallas_call(
        body, grid=(N_HEADS // HPB, nq), in_specs=q_specs + k_specs + [per_head, blk, blk, blk] + [HBM] * len(comm_in),
        out_specs=[blk] * n + dk_specs + [per_head] + comm_out,
        out_shape=[wide] * n + [jax.ShapeDtypeStruct((t, HEAD), F32) if sh else wide for sh in shared] + [wide] + comm_shape,
        scratch_shapes=comm_scratch, name=name, compiler_params=_params(2),
    )(*qs, *[k for k, _ in ks], v, o, stat, do, *comm_in)
    return (out[:n], out[n:2 * n], out[2 * n]) + tuple(out[2 * n + 1:])


@jax.custom_vjp
def _given_inverse(lower, tinv):
    return tinv


def _given_inverse_fwd(lower, tinv):
    return tinv, tinv


def _given_inverse_bwd(tinv, ct):
    return -_dot3(_dot3(tinv, ct, "tn"), tinv, "nt"), jnp.zeros_like(tinv)


_given_inverse.defvjp(_given_inverse_fwd, _given_inverse_bwd)


def _dn_chunk(q, k, v, g, beta, state, tinv=None):
    c = q.shape[-2]
    gc = cumsum_rows(g)
    gcc = gc[..., :c]
    diff = gcc - jnp.swapaxes(gcc, -1, -2)
    causal, strict = _tri(c, "le"), _tri(c, "lt")
    decay = jnp.where(causal, jnp.exp(jnp.where(causal, diff, 0.0)), 0.0)
    kb = k * beta
    lower = jnp.where(strict, bdot(kb, k, "nt") * decay, 0.0)
    tinv = _unit_lower_inverse(lower) if tinv is None else _given_inverse(lower, tinv)
    eg = jnp.exp(gc)
    u = _hdot(tinv, v * beta)
    w = _hdot(tinv, kb * eg)
    attn = bdot(q, k, "nt") * decay
    glast = gc[..., c - 1:c, :]
    v_new = u - bdot(w, state, "nn")
    o = bdot(q * eg, state, "nn") + bdot(attn, v_new, "nn")
    new_state = state * jnp.exp(glast) + bdot(k * jnp.exp(glast - gc), v_new, "tn")
    return o, new_state, tinv


def _stack_heads(ref):
    return jnp.stack([ref[:, h * HEAD:(h + 1) * HEAD] for h in range(N_HEADS)])


def _store_heads(ref, val):
    for h in range(N_HEADS):
        ref[:, h * HEAD:(h + 1) * HEAD] = val[h]


def _dn_fwd(name, q, k, v, g, beta):
    t = q.shape[0]
    nc = t // DN_CHUNK
    wide = N_HEADS * HEAD
    blk = pl.BlockSpec((DN_CHUNK, wide), lambda n: (n, 0))
    st_spec = pl.BlockSpec((N_HEADS, None, HEAD, HEAD), lambda n: (0, n, 0, 0))
    inv_spec = pl.BlockSpec((N_HEADS, None, DN_CHUNK, DN_CHUNK), lambda n: (0, n, 0, 0))

    def body(q_ref, k_ref, v_ref, g_ref, b_ref, o_ref, st_ref, inv_ref, state):
        @pl.when(pl.program_id(0) == 0)
        def _():
            state[...] = jnp.zeros_like(state)

        s_in = state[...]
        st_ref[...] = s_in
        o, s_out, tinv = _dn_chunk(*[_stack_heads(r) for r in (q_ref, k_ref, v_ref, g_ref, b_ref)], s_in)
        _store_heads(o_ref, o)
        inv_ref[...] = tinv
        state[...] = s_out

    return pl.pallas_call(
        body, grid=(nc,), in_specs=[blk] * 5, out_specs=[blk, st_spec, inv_spec],
        out_shape=[jax.ShapeDtypeStruct((t, wide), F32), jax.ShapeDtypeStruct((N_HEADS, nc, HEAD, HEAD), F32),
                   jax.ShapeDtypeStruct((N_HEADS, nc, DN_CHUNK, DN_CHUNK), F32)],
        scratch_shapes=[pltpu.VMEM((N_HEADS, HEAD, HEAD), F32)], name=name, compiler_params=_params(1),
    )(q, k, v, g, beta)


def _dn_bwd(name, q, k, v, g, beta, states, inverses, do):
    t = q.shape[0]
    nc = t // DN_CHUNK
    wide = N_HEADS * HEAD
    blk = pl.BlockSpec((DN_CHUNK, wide), lambda n: (nc - 1 - n, 0))
    st_spec = pl.BlockSpec((N_HEADS, None, HEAD, HEAD), lambda n: (0, nc - 1 - n, 0, 0))
    inv_spec = pl.BlockSpec((N_HEADS, None, DN_CHUNK, DN_CHUNK), lambda n: (0, nc - 1 - n, 0, 0))

    def body(q_ref, k_ref, v_ref, g_ref, b_ref, st_ref, inv_ref, do_ref, dq_ref, dk_ref, dv_ref, dg_ref, db_ref, dstate):
        @pl.when(pl.program_id(0) == 0)
        def _():
            dstate[...] = jnp.zeros_like(dstate)

        tinv = inv_ref[...]
        chunk = lambda *args: _dn_chunk(*args, tinv=tinv)[:2]
        _, vjp = jax.vjp(chunk, *[_stack_heads(r) for r in (q_ref, k_ref, v_ref, g_ref, b_ref)], st_ref[...])
        cts = vjp((_stack_heads(do_ref), dstate[...]))
        for r, d in zip((dq_ref, dk_ref, dv_ref, dg_ref, db_ref), cts[:5]):
            _store_heads(r, d)
        dstate[...] = cts[5]

    shape = jax.ShapeDtypeStruct((t, wide), F32)
    return pl.pallas_call(
        body, grid=(nc,), in_specs=[blk] * 5 + [st_spec, inv_spec, blk], out_specs=[blk] * 5, out_shape=[shape] * 5,
        scratch_shapes=[pltpu.VMEM((N_HEADS, HEAD, HEAD), F32)], name=name, compiler_params=_params(1),
    )(q, k, v, g, beta, states, inverses, do)


CONV_W = 1024
HALO = 8


def _shift_down(cur, prev, s):
    sh = pltpu.roll(cur, s, 0)
    ph = pltpu.roll(prev, s, 0)
    r = lax.broadcasted_iota(jnp.int32, (HALO, cur.shape[1]), 0)
    return jnp.concatenate([jnp.where(r < s, ph, sh[:HALO]), sh[HALO:]], axis=0)


def _shift_up(cur, nxt, s):
    tm = cur.shape[0]
    sh = pltpu.roll(cur, tm - s, 0)
    nh = pltpu.roll(nxt, HALO - s, 0)
    r = lax.broadcasted_iota(jnp.int32, (HALO, cur.shape[1]), 0)
    return jnp.concatenate([sh[:tm - HALO], jnp.where(r >= HALO - s, nh, sh[tm - HALO:])], axis=0)


def _conv_fwd(name, proj, w, tm=256):
    t = proj.shape[0]
    tm = min(tm, t)
    width = w.shape[1]
    per = tm // HALO

    def body(cur_ref, prev_ref, w_ref, y_ref):
        cur = cur_ref[...]
        prev = jnp.where(pl.program_id(0) > 0, prev_ref[...], 0.0)
        y = cur * w_ref[3:4, :]
        for s in (1, 2, 3):
            y = y + _shift_down(cur, prev, s) * w_ref[3 - s:4 - s, :]
        y_ref[...] = y

    return pl.pallas_call(
        body, grid=(t // tm, width // CONV_W),
        in_specs=[pl.BlockSpec((tm, CONV_W), lambda i, c: (i, c)),
                  pl.BlockSpec((HALO, CONV_W), lambda i, c: (jnp.maximum(i * per - 1, 0), c)),
                  pl.BlockSpec((HALO, CONV_W), lambda i, c: (0, c))],
        out_specs=pl.BlockSpec((tm, CONV_W), lambda i, c: (i, c)),
        out_shape=jax.ShapeDtypeStruct((t, width), F32), name=name, compiler_params=_params(2),
    )(proj, proj, w)


def _conv_bwd(name, proj, w, dy, tm=256):
    t = proj.shape[0]
    tm = min(tm, t)
    width = w.shape[1]
    per, nt = tm // HALO, t // tm

    def body(cur_ref, prev_ref, w_ref, dy_ref, nxt_ref, du_ref, dw_ref):
        i = pl.program_id(1)
        cur, dy_t = cur_ref[...], dy_ref[...]
        prev = jnp.where(i > 0, prev_ref[...], 0.0)
        nxt = jnp.where(i < nt - 1, nxt_ref[...], 0.0)
        du = dy_t * w_ref[3:4, :]
        rows = [jnp.sum(dy_t * cur, axis=0, keepdims=True)]
        for s in (1, 2, 3):
            du = du + _shift_up(dy_t, nxt, s) * w_ref[3 - s:4 - s, :]
            rows.insert(0, jnp.sum(dy_t * _shift_down(cur, prev, s), axis=0, keepdims=True))
        du_ref[...] = du.astype(du_ref.dtype)

        @pl.when(i == 0)
        def _():
            dw_ref[...] = jnp.zeros_like(dw_ref)

        dw_ref[...] += jnp.concatenate(rows + [jnp.zeros((HALO - 4, CONV_W), F32)], axis=0)

    return pl.pallas_call(
        body, grid=(width // CONV_W, nt),
        in_specs=[pl.BlockSpec((tm, CONV_W), lambda c, i: (i, c)),
                  pl.BlockSpec((HALO, CONV_W), lambda c, i: (jnp.maximum(i * per - 1, 0), c)),
                  pl.BlockSpec((HALO, CONV_W), lambda c, i: (0, c)),
                  pl.BlockSpec((tm, CONV_W), lambda c, i: (i, c)),
                  pl.BlockSpec((HALO, CONV_W), lambda c, i: (jnp.minimum((i + 1) * per, t // HALO - 1), c))],
        out_specs=[pl.BlockSpec((tm, CONV_W), lambda c, i: (i, c)), pl.BlockSpec((HALO, CONV_W), lambda c, i: (0, c))],
        out_shape=[jax.ShapeDtypeStruct((t, width), BF16), jax.ShapeDtypeStruct((HALO, width), F32)],
        name=name, compiler_params=_params(2),
    )(proj, proj, w, dy, dy)


def _norm_fn(x, g):
    return _rms(x, g)


def _swiglu_fn(gu):
    return _silu(gu[:, :FFN_HIDDEN]) * gu[:, FFN_HIDDEN:]


def _heads(x):
    return [x[:, h * HEAD:(h + 1) * HEAD] for h in range(x.shape[1] // HEAD)]


def _dn_pre_fn(c, ab, a_log, dt_bias):
    w = N_HEADS * HEAD
    q = [_l2(_silu(x)) * (HEAD ** -0.5) for x in _heads(c[:, :w])]
    k = [_l2(_silu(x)) for x in _heads(c[:, w:2 * w])]
    v = _silu(c[:, 2 * w:])
    g, beta = [], []
    for h in range(N_HEADS):
        gh = -jnp.exp(a_log[:, h:h + 1]) * jax.nn.softplus(ab[:, h:h + 1] + dt_bias[:, h:h + 1])
        bh = jax.nn.sigmoid(ab[:, N_HEADS + h:N_HEADS + h + 1])
        g.append(jnp.broadcast_to(gh, (c.shape[0], HEAD)))
        beta.append(jnp.broadcast_to(bh, (c.shape[0], HEAD)))
    cat = lambda xs: jnp.concatenate(xs, axis=1)
    return cat(q), cat(k), v, cat(g), cat(beta)


def _dn_post_fn(o, z, out_norm):
    return jnp.concatenate([_rms(oh, out_norm) * _silu(zh) for oh, zh in zip(_heads(o), _heads(z))], axis=1)


def _sb_pre_fn(qkv, q_norm, k_norm):
    w = N_HEADS * HEAD
    q = [_rms(x, q_norm) * (HEAD ** -0.5) for x in _heads(qkv[:, :w])]
    k = [_rms(x, k_norm) for x in _heads(qkv[:, w:2 * w])]
    return jnp.concatenate(q, axis=1), jnp.concatenate(k, axis=1), qkv[:, 2 * w:]


def _mla_a_fn(down, cos, sin_lo, sin_hi, q_a_norm, kv_a_norm, k_rope_norm):
    cq = _rms(down[:, :256], q_a_norm)
    ckv = _rms(down[:, 256:384], kv_a_norm)
    kr = _rope(_rms(down[:, 384:], k_rope_norm, MLA_ROPE), cos, sin_lo, sin_hi)
    return cq, ckv, kr


def _mla_b_fn(qf, kvf, cos, sin_lo, sin_hi, q_nope_norm, q_rope_norm, k_nope_norm):
    scale = MLA_QK ** -0.5
    qn, qr, kn, v = [], [], [], []
    for h in range(N_HEADS):
        a = 2 * h * HEAD
        qn.append(_rms(qf[:, a:a + HEAD], q_nope_norm) * scale)
        qr.append(_rope(_rms(qf[:, a + HEAD:a + 2 * HEAD], q_rope_norm, MLA_ROPE), cos, sin_lo, sin_hi) * scale)
        kn.append(_rms(kvf[:, a:a + HEAD], k_nope_norm))
        v.append(kvf[:, a + HEAD:a + 2 * HEAD])
    cat = lambda xs: jnp.concatenate(xs, axis=1)
    return cat(qn), cat(qr), cat(kn), cat(v)


def _rope_tables(t):
    inv_freq = ROPE_THETA ** (-jnp.arange(0, MLA_ROPE, 2, dtype=F32) / MLA_ROPE)
    ang = jnp.arange(t, dtype=F32)[:, None] * inv_freq[None, :]
    cos, sin, zero = jnp.cos(ang), jnp.sin(ang), jnp.zeros((t, MLA_ROPE // 2), F32)
    cat = lambda xs: jnp.concatenate(xs, axis=1)
    return cat([cos, cos, zero, zero]), cat([-sin, zero, zero, zero]), cat([zero, sin, zero, zero])


def _row(v, width=None):
    width = v.shape[0] if width is None else width
    return jnp.pad(v.astype(F32), (0, width - v.shape[0])).reshape(1, width)


def _loss_kernel(y, target):
    t, d = y.shape
    tm = min(256, t)

    def body(y_ref, t_ref, part_ref, dy_ref):
        e = y_ref[...] - t_ref[...]
        dy_ref[...] = e * (1.0 / d)

        @pl.when(pl.program_id(0) == 0)
        def _():
            part_ref[...] = jnp.zeros_like(part_ref)

        part_ref[...] += jnp.sum(e * e, axis=0, keepdims=True)

    blk = pl.BlockSpec((tm, d), lambda i: (i, 0))
    one = pl.BlockSpec((1, d), lambda i: (0, 0))
    return pl.pallas_call(body, grid=(t // tm,), in_specs=[blk, blk], out_specs=[one, blk],
                          out_shape=[jax.ShapeDtypeStruct((1, d), F32), jax.ShapeDtypeStruct((t, d), F32)],
                          name="loss", compiler_params=_params(1))(y, target)


def _ffn_fwd(p, x, w, sm):
    h, = _ew(p + "ffn_norm", _norm_fn, [x], [sm["ffn_norm"]], [(D_MODEL, BF16)])
    gu = _mm(p + "ffn_gu", h, w["ffn_w_gate_up"], "nn")
    act, = _ew(p + "ffn_act", _swiglu_fn, [gu], [], [(FFN_HIDDEN, BF16)], tm=128)
    y = _mm(p + "ffn_down", act, w["ffn_w_down"], "nn", add=x)
    return y, (x, h, gu, act)


def _ffn_bwd(p, saved, dy, w, sm, grads):
    x, h, gu, act = saved
    dact = _mm(p + "ffn_down_dx", dy, w["ffn_w_down"], "nt")
    grads["ffn_w_down"] = _mm(p + "ffn_down_dw", act, dy, "tn")
    (dgu,), _ = _ew_bwd(p + "ffn_act_bwd", _swiglu_fn, [gu], [], [dact], tm=128, ct_dtypes=[BF16])
    dh = _mm(p + "ffn_gu_dx", dgu, w["ffn_w_gate_up"], "nt")
    grads["ffn_w_gate_up"] = _mm(p + "ffn_gu_dw", h, dgu, "tn")
    (dx,), (dg,) = _ew_bwd(p + "ffn_norm_bwd", _norm_fn, [x], [sm["ffn_norm"]], [dh], add=dy)
    grads["ffn_norm"] = dg
    return dx


def _dn_layer_fwd(p, x, w, sm):
    h, = _ew(p + "mix_norm", _norm_fn, [x], [sm["mix_norm"]], [(D_MODEL, BF16)])
    proj = _mm(p + "dn_in", h, w["dn_w_in"], "nn")
    conv = _conv_fwd(p + "dn_conv", proj, sm["dn_conv_w"])
    ab = _V(proj, LANES, 4 * N_HEADS)
    wide = N_HEADS * HEAD
    q, k, v, g, beta = _ew(p + "dn_pre", _dn_pre_fn, [conv, ab], [sm["dn_a_log"], sm["dn_dt_bias"]], [(wide, F32)] * 5)
    o, states, inverses = _dn_fwd(p + "dn_core", q, k, v, g, beta)
    z = _V(proj, wide, 3)
    on, = _ew(p + "dn_post", _dn_post_fn, [o, z], [sm["dn_out_norm"]], [(wide, BF16)])
    y = _mm(p + "dn_out", on, w["dn_w_out"], "nn", add=x)
    return y, (x, h, proj, conv, q, k, v, g, beta, o, states, inverses, on)


def _dn_layer_bwd(p, saved, dy, w, sm, grads):
    x, h, proj, conv, q, k, v, g, beta, o, states, inverses, on = saved
    wide = N_HEADS * HEAD
    don = _mm(p + "dn_out_dx", dy, w["dn_w_out"], "nt")
    grads["dn_w_out"] = _mm(p + "dn_out_dw", on, dy, "tn")
    (do, dz), (d_out_norm,) = _ew_bwd(p + "dn_post_bwd", _dn_post_fn, [o, _V(proj, wide, 3)], [sm["dn_out_norm"]], [don],
                                          ct_dtypes=[F32, BF16])
    grads["dn_out_norm"] = d_out_norm
    dq, dk, dv, dg, db = _dn_bwd(p + "dn_core_bwd", q, k, v, g, beta, states, inverses, do)
    (dconv, dab), (d_a_log, d_dt) = _ew_bwd(p + "dn_pre_bwd", _dn_pre_fn, [conv, _V(proj, LANES, 4 * N_HEADS)],
                                            [sm["dn_a_log"], sm["dn_dt_bias"]], [dq, dk, dv, dg, db], ct_dtypes=[F32, BF16])
    grads["dn_a_log"], grads["dn_dt_bias"] = d_a_log, d_dt
    dqkv, dconv_w = _conv_bwd(p + "dn_conv_bwd", proj, sm["dn_conv_w"], dconv)
    grads["dn_conv_w"] = dconv_w
    dproj = jnp.concatenate([dqkv, dz, dab], axis=1)
    dh = _mm(p + "dn_in_dx", dproj, w["dn_w_in"], "nt")
    grads["dn_w_in"] = _mm(p + "dn_in_dw", h, dproj, "tn")
    (dx,), (dgain,) = _ew_bwd(p + "mix_norm_bwd", _norm_fn, [x], [sm["mix_norm"]], [dh], add=dy)
    grads["mix_norm"] = dgain
    return dx


def _sb_layer_fwd(p, x, w, sm, exchange=None):
    h, = _ew(p + "mix_norm", _norm_fn, [x], [sm["mix_norm"]], [(D_MODEL, BF16)])
    qkv = _mm(p + "sb_qkv", h, w["sb_w_qkv"], "nn")
    wide = N_HEADS * HEAD
    q, k, v = _ew(p + "sb_pre", _sb_pre_fn, [qkv], [sm["sb_q_norm"], sm["sb_k_norm"]], [(wide, BF16)] * 3)
    gather = None if exchange is None else exchange.late_shards()
    o, stat, *gathered = _attn_fwd(p + "sb_core", "sb", [q], [(k, False)], v, gather=gather)
    if exchange is not None:
        exchange.deliver(gathered[0])
    y = _mm(p + "sb_out", o, w["sb_w_out"], "nn", add=x)
    return y, (x, h, qkv, q, k, v, o, stat)


def _sb_layer_bwd(p, saved, dy, w, sm, grads, exchange=None):
    x, h, qkv, q, k, v, o, stat = saved
    do = _mm(p + "sb_out_dx", dy, w["sb_w_out"], "nt")
    grads["sb_w_out"] = _mm(p + "sb_out_dw", o, dy, "tn")
    send = None if exchange is None else exchange.start(grads)
    (dq,), (dk,), dv, *arrived = _attn_bwd(p + "sb_core_bwd", "sb", [q], [(k, False)], v, o, stat, do, send=send)
    if exchange is not None:
        exchange.arrived = arrived[0]
    (dqkv,), (dqn, dkn) = _ew_bwd(p + "sb_pre_bwd", _sb_pre_fn, [qkv], [sm["sb_q_norm"], sm["sb_k_norm"]], [dq, dk, dv],
                                  ct_dtypes=[BF16])
    grads["sb_q_norm"], grads["sb_k_norm"] = dqn, dkn
    dh = _mm(p + "sb_qkv_dx", dqkv, w["sb_w_qkv"], "nt")
    grads["sb_w_qkv"] = _mm(p + "sb_qkv_dw", h, dqkv, "tn")
    (dx,), (dgain,) = _ew_bwd(p + "mix_norm_bwd", _norm_fn, [x], [sm["mix_norm"]], [dh], add=dy)
    grads["mix_norm"] = dgain
    return dx


def _mla_layer_fwd(p, x, w, sm):
    t = x.shape[0]
    tabs = [_V(a, diff=False) for a in _rope_tables(t)]
    h, = _ew(p + "mix_norm", _norm_fn, [x], [sm["mix_norm"]], [(D_MODEL, BF16)])
    down = _mm(p + "mla_down", h, w["mla_w_down"], "nn")
    sm_a = [sm["mla_q_a_norm"], sm["mla_kv_a_norm"], sm["mla_k_rope_norm"]]
    cq, ckv, kr = _ew(p + "mla_a", _mla_a_fn, [down] + tabs, sm_a, [(256, BF16), (128, BF16), (128, BF16)])
    qf = _mm(p + "mla_uq", cq, w["mla_w_uq"], "nn")
    kvf = _mm(p + "mla_ukv", ckv, w["mla_w_ukv"], "nn")
    sm_b = [sm["mla_q_nope_norm"], sm["mla_q_rope_norm"], sm["mla_k_nope_norm"]]
    wide = N_HEADS * HEAD
    qn, qr, kn, v = _ew(p + "mla_b", _mla_b_fn, [qf, kvf] + tabs, sm_b, [(wide, BF16)] * 4)
    o, stat = _attn_fwd(p + "mla_core", "softmax", [qn, qr], [(kn, False), (kr, True)], v)
    y = _mm(p + "mla_out", o, w["mla_w_out"], "nn", add=x)
    return y, (x, h, down, cq, ckv, kr, qf, kvf, qn, qr, kn, v, o, stat)


def _mla_layer_bwd(p, saved, dy, w, sm, grads):
    x, h, down, cq, ckv, kr, qf, kvf, qn, qr, kn, v, o, stat = saved
    tabs = [_V(a, diff=False) for a in _rope_tables(x.shape[0])]
    do = _mm(p + "mla_out_dx", dy, w["mla_w_out"], "nt")
    grads["mla_w_out"] = _mm(p + "mla_out_dw", o, dy, "tn")
    (dqn, dqr), (dkn, dkr), dv = _attn_bwd(p + "mla_core_bwd", "softmax", [qn, qr], [(kn, False), (kr, True)],
                                           v, o, stat, do)
    sm_b = [sm["mla_q_nope_norm"], sm["mla_q_rope_norm"], sm["mla_k_nope_norm"]]
    (dqf, dkvf), dsm_b = _ew_bwd(p + "mla_b_bwd", _mla_b_fn, [qf, kvf] + tabs, sm_b, [dqn, dqr, dkn, dv],
                                 ct_dtypes=[BF16, BF16])
    grads["mla_q_nope_norm"], grads["mla_q_rope_norm"], grads["mla_k_nope_norm"] = dsm_b
    dcq = _mm(p + "mla_uq_dx", dqf, w["mla_w_uq"], "nt")
    grads["mla_w_uq"] = _mm(p + "mla_uq_dw", cq, dqf, "tn")
    dckv = _mm(p + "mla_ukv_dx", dkvf, w["mla_w_ukv"], "nt")
    grads["mla_w_ukv"] = _mm(p + "mla_ukv_dw", ckv, dkvf, "tn")
    sm_a = [sm["mla_q_a_norm"], sm["mla_kv_a_norm"], sm["mla_k_rope_norm"]]
    (ddown,), dsm_a = _ew_bwd(p + "mla_a_bwd", _mla_a_fn, [down] + tabs, sm_a, [dcq, dckv, dkr], ct_dtypes=[BF16])
    grads["mla_q_a_norm"], grads["mla_kv_a_norm"], grads["mla_k_rope_norm"] = dsm_a
    dh = _mm(p + "mla_down_dx", ddown, w["mla_w_down"], "nt")
    grads["mla_w_down"] = _mm(p + "mla_down_dw", h, ddown, "tn")
    (dx,), (dgain,) = _ew_bwd(p + "mix_norm_bwd", _norm_fn, [x], [sm["mix_norm"]], [dh], add=dy)
    grads["mix_norm"] = dgain
    return dx


_MIX_FWD = (_dn_layer_fwd, _sb_layer_fwd, _mla_layer_fwd)
_MIX_BWD = (_dn_layer_bwd, _sb_layer_bwd, _mla_layer_bwd)


def _pad_cols(a, n):
    return jnp.pad(a, ((0, 0), (0, n - a.shape[1])))


def _prep_big(name, a):
    if name.endswith("dn_w_in"):
        return _pad_cols(a, 4 * N_HEADS * HEAD + LANES)
    if name.endswith("mla_w_down"):
        return _pad_cols(a, 512)
    if name.endswith("mla_w_uq"):
        a3 = a.reshape(a.shape[0], N_HEADS, MLA_QK)
        return jnp.pad(a3, ((0, 0), (0, 0), (0, 2 * HEAD - MLA_QK))).reshape(a.shape[0], N_HEADS * 2 * HEAD)
    return a


def _unprep_big(name, g):
    if name.endswith("dn_w_in"):
        return g[:, :4 * N_HEADS * HEAD + 2 * N_HEADS]
    if name.endswith("mla_w_down"):
        return g[:, :448]
    if name.endswith("mla_w_uq"):
        return g.reshape(g.shape[0], N_HEADS, 2 * HEAD)[:, :, :MLA_QK].reshape(g.shape[0], N_HEADS * MLA_QK)
    return g


def _prep_small(name, a):
    if name.endswith("dn_conv_w"):
        return jnp.pad(a.astype(F32), ((0, HALO - a.shape[0]), (0, 0)))
    if name.endswith(("dn_a_log", "dn_dt_bias", "mla_q_rope_norm", "mla_k_rope_norm")):
        return _row(a, LANES)
    return _row(a)


def _unprep_small(name, g, like):
    if name.endswith("dn_conv_w"):
        return g[:like.shape[0]]
    return g.reshape(-1)[:like.shape[0]]


EARLY_LAYER = 1


def local_step(x, target, big, small, exchange=None):
    layers, saved = [], []
    for i in range(DEPTH):
        p = "l%d_" % i
        names = _MIXERS[i % 3] + ("ffn_w_gate_up", "ffn_w_down", "mix_norm", "ffn_norm")
        sm = {n: _prep_small(n, small[p + n]) for n in names if p + n in small}
        mixer_w = {n: _prep_big(n, big[p + n]) for n in _MIXERS[i % 3] if p + n in big}
        if exchange is not None and i == EARLY_LAYER:
            x, s_mix = _MIX_FWD[i % 3](p, x, mixer_w, sm, exchange)
        else:
            x, s_mix = _MIX_FWD[i % 3](p, x, mixer_w, sm)
        w = dict(mixer_w, **{n: big[p + n] for n in ("ffn_w_gate_up", "ffn_w_down")})
        x, s_ffn = _ffn_fwd(p, x, w, sm)
        layers.append((p, w, sm))
        saved.append((s_mix, s_ffn))
    part, dx = _loss_kernel(x, target)
    grads = {}
    for i in reversed(range(DEPTH)):
        p, w, sm = layers[i]
        g = {}
        dx = _ffn_bwd(p, saved[i][1], dx, w, sm, g)
        if exchange is not None and i == EARLY_LAYER:
            exchange.grads = grads
            dx = _MIX_BWD[i % 3](p, saved[i][0], dx, w, sm, g, exchange)
        else:
            dx = _MIX_BWD[i % 3](p, saved[i][0], dx, w, sm, g)
        for n, val in g.items():
            grads[p + n] = _unprep_big(p + n, val) if p + n in big else _unprep_small(p + n, val, small[p + n])
    return part, dx, grads


ROW = 1024
BIG_ROWS = 1024


PACK_ALIGN = 16


def _packed_rows(shape):
    return -(-math.prod(shape) // (ROW * PACK_ALIGN)) * PACK_ALIGN


def _as_rows(a, lead=()):
    rows = _packed_rows(a.shape[len(lead):])
    flat = a.reshape(lead + (-1,))
    return jnp.pad(flat, ((0, 0),) * len(lead) + ((0, rows * ROW - flat.shape[-1]),)).reshape(lead + (rows, ROW))


def _pack(arrs, rows_multiple, lead=()):
    blocks = [_as_rows(a, lead) for a in arrs]
    used = sum(b.shape[-2] for b in blocks)
    fill = -(-used // rows_multiple) * rows_multiple - used
    if fill:
        blocks.append(jnp.zeros(lead + (fill, ROW), blocks[0].dtype))
    return jnp.concatenate(blocks, axis=len(lead))


def _unpack(buf, shapes, lead=()):
    out, r0 = [], 0
    for s in shapes:
        rows, n = _packed_rows(s), math.prod(s)
        block = lax.slice_in_dim(buf, r0, r0 + rows, axis=len(lead))
        out.append(block.reshape(lead + (-1,))[..., :n].reshape(lead + tuple(s)))
        r0 += rows
    return out


def _me():
    return lax.axis_index("x"), lax.axis_index("y"), lax.axis_index("c")


def _other_chips(x, y):
    return [(1 - x, y), (x, 1 - y), (1 - x, 1 - y)]


HBM = pl.BlockSpec(memory_space=pl.ANY)


OWN_STREAMS = 4


def _gather_phases(x_ref, out_ref, send_sems, recv_sems):
    rows = x_ref.shape[0]
    half = rows // 2
    x, y, c = _me()
    sibling, chips = (x, y, 1 - c), _other_chips(x, y)

    def part(px, py, pc):
        return out_ref.at[2 * px + py, pl.ds(pl.multiple_of(pc * half, 16), half), :]

    def copy(k, block, to, src=None):
        return pltpu.make_async_remote_copy(
            src_ref=part(*block) if src is None else src, dst_ref=part(*block),
            send_sem=send_sems.at[k], recv_sem=recv_sems.at[k], device_id=to, device_id_type=MESH)

    my_half = x_ref.at[pl.ds(pl.multiple_of(c * half, 16), half), :]
    first = [copy(j, (x, y, c), (*chip, c), src=my_half) for j, chip in enumerate(chips)]
    piece = rows // OWN_STREAMS
    for p in range(OWN_STREAMS):
        rows_p = pl.ds(p * piece, piece)
        first.append(pltpu.make_async_remote_copy(
            src_ref=x_ref.at[rows_p, :], dst_ref=out_ref.at[2 * x + y, rows_p, :], send_sem=send_sems.at[6 + p],
            recv_sem=recv_sems.at[6 + p], device_id=sibling, device_id_type=MESH))
    passed = [copy(3 + j, (*chip, c), sibling) for j, chip in enumerate(chips)]

    def start():
        for cp in first:
            cp.start()

    def relay():
        for j, chip in enumerate(chips):
            copy(j, (*chip, c), (x, y, c)).wait_recv()
            passed[j].start()

    def finish():
        for j, chip in enumerate(chips):
            copy(3 + j, (*chip, 1 - c), (x, y, c)).wait_recv()
        for cp in first[3:]:
            cp.wait_recv()
        for cp in first + passed:
            cp.wait_send()

    return start, relay, finish


GATHER_SEMS = 6 + OWN_STREAMS


def _gather_shards(packed):
    def body(x_ref, out_ref, send_sems, recv_sems):
        for phase in _gather_phases(x_ref, out_ref, send_sems, recv_sems):
            phase()

    return pl.pallas_call(
        body, out_shape=jax.ShapeDtypeStruct((N_CHIPS,) + packed.shape, packed.dtype), in_specs=[HBM], out_specs=HBM,
        scratch_shapes=[pltpu.SemaphoreType.DMA((GATHER_SEMS,)), pltpu.SemaphoreType.DMA((GATHER_SEMS,))],
        name="gather_weights",
    )(packed)


D2D_STREAMS = 16


def _swap_halves(g, tag):
    n, rows, _ = g.shape
    half = rows // 2
    per = D2D_STREAMS // n
    piece = half // per

    def body(g_ref, theirs_ref, send_sems, recv_sems):
        x, y, c = _me()
        give = (1 - c) * half
        copies = []
        for j in range(n):
            for p in range(per):
                k = j * per + p
                cp = pltpu.make_async_remote_copy(
                    src_ref=g_ref.at[j, pl.ds(pl.multiple_of(give + p * piece, 8), piece), :],
                    dst_ref=theirs_ref.at[j, pl.ds(p * piece, piece), :],
                    send_sem=send_sems.at[k], recv_sem=recv_sems.at[k], device_id=(x, y, 1 - c), device_id_type=MESH)
                cp.start()
                copies.append(cp)
        for cp in copies:
            cp.wait()

    return pl.pallas_call(
        body, out_shape=jax.ShapeDtypeStruct((n, half, ROW), g.dtype), in_specs=[HBM], out_specs=HBM,
        scratch_shapes=[pltpu.SemaphoreType.DMA((D2D_STREAMS,)), pltpu.SemaphoreType.DMA((D2D_STREAMS,))],
        name="grad_swap_halves" + tag,
    )(g)


def _pair_sum(g, theirs, c, tag):
    n, half, _ = theirs.shape
    tm = _tile(half)
    nb = half // tm

    def body(c_ref, g_ref, t_ref, s32_ref, s16_ref):
        s = g_ref[...] + t_ref[...]
        s32_ref[...] = s
        s16_ref[...] = s.astype(BF16)

    blk = pl.BlockSpec((None, tm, ROW), lambda j, i, c_ref: (j, i, 0))
    return pl.pallas_call(
        body,
        grid_spec=pltpu.PrefetchScalarGridSpec(
            num_scalar_prefetch=1, grid=(n, nb),
            in_specs=[pl.BlockSpec((None, tm, ROW), lambda j, i, c_ref: (j, c_ref[0] * nb + i, 0)), blk],
            out_specs=[blk, blk]),
        out_shape=[jax.ShapeDtypeStruct(theirs.shape, F32), jax.ShapeDtypeStruct(theirs.shape, BF16)],
        name="grad_pair_sum" + tag, compiler_params=_params(2),
    )(c.reshape(1).astype(jnp.int32), g, theirs)


def _scatter_chunks(s16):
    _, half, _ = s16.shape

    def body(s16_ref, got_ref, send_sems, recv_sems):
        x, y, c = _me()
        sends = []
        for j, (px, py) in enumerate(_other_chips(x, y)):
            cp = pltpu.make_async_remote_copy(src_ref=s16_ref.at[2 * px + py], dst_ref=got_ref.at[j],
                                              send_sem=send_sems.at[j], recv_sem=recv_sems.at[j],
                                              device_id=(px, py, c), device_id_type=MESH)
            cp.start()
            sends.append(cp)
        for cp in sends:
            cp.wait()

    return pl.pallas_call(
        body, out_shape=jax.ShapeDtypeStruct((3, half, ROW), BF16), in_specs=[HBM], out_specs=HBM,
        scratch_shapes=[pltpu.SemaphoreType.DMA((3,)), pltpu.SemaphoreType.DMA((3,))],
        name="grad_scatter",
    )(s16)


def _chip_sum(s32, got, chip, c, tag):
    _, half, _ = s32.shape
    tm = _tile(half)
    nb = half // tm

    def body(where_ref, own_ref, g0_ref, g1_ref, g2_ref, o_ref):
        o_ref[...] = ((own_ref[...] + g0_ref[...].astype(F32)) + g1_ref[...].astype(F32)) + g2_ref[...].astype(F32)

    got_spec = lambda k: pl.BlockSpec((None, tm, ROW), lambda i, where_ref, k=k: (k, i, 0))
    return pl.pallas_call(
        body,
        grid_spec=pltpu.PrefetchScalarGridSpec(
            num_scalar_prefetch=1, grid=(nb,),
            in_specs=[pl.BlockSpec((None, tm, ROW), lambda i, where_ref: (where_ref[0], i, 0)), got_spec(0), got_spec(1), got_spec(2)],
            out_specs=pl.BlockSpec((tm, ROW), lambda i, where_ref: (where_ref[1] * nb + i, 0))),
        out_shape=jax.ShapeDtypeStruct((2 * half, ROW), F32), name="grad_chip_sum" + tag, compiler_params=_params(1),
    )(jnp.stack([chip, c]).astype(jnp.int32), s32, got, got, got)


def _join_halves(f, tag):
    half = f.shape[0] // 2
    piece = half // D2D_STREAMS

    def body(f_ref, out_ref, send_sems, recv_sems):
        x, y, c = _me()
        copies = []
        for p in range(D2D_STREAMS):
            rows = out_ref.at[pl.ds(pl.multiple_of(c * half + p * piece, 8), piece), :]
            cp = pltpu.make_async_remote_copy(src_ref=rows, dst_ref=rows, send_sem=send_sems.at[p], recv_sem=recv_sems.at[p],
                                              device_id=(x, y, 1 - c), device_id_type=MESH)
            cp.start()
            copies.append(cp)
        for cp in copies:
            cp.wait()

    return pl.pallas_call(
        body, out_shape=jax.ShapeDtypeStruct(f.shape, F32), in_specs=[HBM], out_specs=HBM, input_output_aliases={0: 0},
        scratch_shapes=[pltpu.SemaphoreType.DMA((D2D_STREAMS,)), pltpu.SemaphoreType.DMA((D2D_STREAMS,))],
        name="grad_join_halves" + tag,
    )(f)


def _all_reduce_small(name, v):
    rows = v.shape[0]

    def body(v_ref, out_ref, slots, send_sems, recv_sems):
        x, y, c = _me()
        me = 4 * x + 2 * y + c
        slots[me] = v_ref[...]
        sends = []
        for r in range(1, 8):
            to = (x ^ (r >> 2), y ^ ((r >> 1) & 1), c ^ (r & 1))
            cp = pltpu.make_async_remote_copy(src_ref=v_ref, dst_ref=slots.at[me], send_sem=send_sems.at[r - 1],
                                              recv_sem=recv_sems.at[r - 1], device_id=to, device_id_type=MESH)
            cp.start()
            sends.append(cp)
        for cp in sends:
            cp.wait()
        total = slots[0]
        for d in range(1, 8):
            total = total + slots[d]
        out_ref[...] = total

    vmem = pl.BlockSpec(memory_space=pltpu.VMEM)
    return pl.pallas_call(
        body, out_shape=jax.ShapeDtypeStruct(v.shape, F32), in_specs=[vmem], out_specs=vmem,
        scratch_shapes=[pltpu.VMEM((8, rows, ROW), F32), pltpu.SemaphoreType.DMA((7,)), pltpu.SemaphoreType.DMA((7,))],
        name=name,
    )(v)


def _adam_fn(w, g, m, v):
    m2 = ADAM_B1 * m + (1.0 - ADAM_B1) * g
    v2 = ADAM_B2 * v + (1.0 - ADAM_B2) * (g * g)
    m_hat = m2 / (1.0 - ADAM_B1 ** ADAM_STEP)
    v_hat = v2 / (1.0 - ADAM_B2 ** ADAM_STEP)
    return -ADAM_LR * (m_hat / (jnp.sqrt(v_hat) + ADAM_EPS) + ADAM_WD * w), m2, v2


def _full_shape(name, shard_shape):
    ax = _shard_axis(name)
    return tuple(n * N_CHIPS if k == ax else n for k, n in enumerate(shard_shape))


def _chip_major(name, full):
    if _shard_axis(name) == 0:
        return full.reshape(N_CHIPS, -1, full.shape[1])
    n = full.shape[1] // N_CHIPS
    return jnp.stack([full[:, j * n:(j + 1) * n] for j in range(N_CHIPS)])


def _from_chip_major(name, shards):
    if _shard_axis(name) == 0:
        return shards.reshape(-1, shards.shape[2])
    return jnp.concatenate([shards[j] for j in range(N_CHIPS)], axis=1)


def _row_tile(rows, cap=512):
    return max(t for t in range(8, min(rows, cap) + 1, 8) if rows % t == 0)


def _step(a):
    x_i, y_i, c_i = _me()
    chip = 2 * x_i + y_i
    def packed_shards(names):
        return _pack([a[n].astype(BF16) for n in names], BIG_ROWS)

    def full_matrices(names, gathered):
        shards = _unpack(gathered, [a[n].shape for n in names], lead=(N_CHIPS,))
        return {n: _from_chip_major(n, sh) for n, sh in zip(names, shards)}

    first_w = [n for n in BIG if int(n[1]) <= EARLY_LAYER]
    later_w = [n for n in BIG if n not in first_w]
    big = full_matrices(first_w, _gather_shards(packed_shards(first_w)))

    small = {n: a[n] for n in SMALL}
    convs = [n for n in SMALL if n.endswith("dn_conv_w")]
    placed = []
    for n in convs:
        full = jnp.zeros(_full_shape_conv(a[n].shape), F32)
        placed.append(lax.dynamic_update_slice(full, a[n], (0, chip * a[n].shape[1])))
    conv_sum = _all_reduce_small("gather_conv", _pack(placed, 8))
    for n, full in zip(convs, _unpack(conv_sum, [p.shape for p in placed])):
        small[n] = full * 0.5

    def pair_sums(names, grads, tag):
        g_all = _pack([_chip_major(n, grads[n]) for n in names], BIG_ROWS, lead=(N_CHIPS,))
        return _pair_sum(g_all, _swap_halves(g_all, tag), c_i, tag)

    def reduced(names, s32, got, tag):
        whole = _join_halves(_chip_sum(s32, got, chip, c_i, tag), tag)
        return dict(zip(names, _unpack(whole, [a[n].shape for n in names])))

    early_prefix = "l%d_" % EARLY_LAYER
    early = [n for n in BIG if int(n[1]) > EARLY_LAYER or n in (early_prefix + "ffn_w_gate_up", early_prefix + "ffn_w_down",
                                                                early_prefix + "sb_w_out")]
    late = [n for n in BIG if n not in early]

    class Exchange:
        def late_shards(self):
            return packed_shards(later_w)

        def deliver(self, gathered):
            big.update(full_matrices(later_w, gathered))

        def start(self, layer_grads):
            have = dict(self.grads, **{early_prefix + k: val for k, val in layer_grads.items()})
            self.s32, s16 = pair_sums(early, have, "_early")
            return s16

    exchange = Exchange()
    part, dx, grads = local_step(a["x"][0], a["loss_target"][0], big, small, exchange)
    loss = lax.psum(0.5 * jnp.sum(part) / D_MODEL, ("x", "y", "c"))
    g_big = reduced(early, exchange.s32, exchange.arrived, "_early")
    s32, s16 = pair_sums(late, grads, "_late")
    g_big.update(reduced(late, s32, _scatter_chunks(s16), "_late"))

    g_small_full = _all_reduce_small("reduce_small", _pack([grads[n] for n in SMALL], 8))
    g_small = dict(zip(SMALL, _unpack(g_small_full, [grads[n].shape for n in SMALL])))
    for n in convs:
        g_small[n] = lax.dynamic_slice_in_dim(g_small[n], chip * a[n].shape[1], a[n].shape[1], axis=1)

    outs = {}
    for n in BIG:
        d, m2, v2 = _ew("adam_" + n, _adam_fn, [a[n], g_big[n], a["m_" + n], a["v_" + n]], [],
                        [(a[n].shape[1], F32)] * 3, tm=_row_tile(a[n].shape[0]))
        outs.update({"grad_" + n: g_big[n], "delta_" + n: d, "new_m_" + n: m2, "new_v_" + n: v2})
    pk = lambda prefix: _pack([a[prefix + n] for n in SMALL], 8)
    gs_packed = _pack([g_small[n] for n in SMALL], 8)
    small_bufs = (gs_packed,) + tuple(_ew("adam_small", _adam_fn, [pk(""), gs_packed, pk("m_"), pk("v_")], [], [(ROW, F32)] * 3,
                                          tm=gs_packed.shape[0]))
    small_shapes = [a[n].shape for n in SMALL]
    for key, buf in zip(("grad_", "delta_", "new_m_", "new_v_"), small_bufs):
        outs.update({key + n: val for n, val in zip(SMALL, _unpack(buf, small_shapes))})
    result = [loss, dx[None]]
    for key in ("grad_", "delta_", "new_m_", "new_v_"):
        result += [outs[key + n] for n in WEIGHTS]
    return tuple(result)


def _full_shape_conv(shard_shape):
    return (shard_shape[0], shard_shape[1] * N_CHIPS)


def kernel(x, l0_mix_norm, l0_dn_w_in, l0_dn_conv_w, l0_dn_a_log, l0_dn_dt_bias, l0_dn_out_norm, l0_dn_w_out, l0_ffn_norm, l0_ffn_w_gate_up, l0_ffn_w_down, l1_mix_norm, l1_sb_w_qkv, l1_sb_q_norm, l1_sb_k_norm, l1_sb_w_out, l1_ffn_norm, l1_ffn_w_gate_up, l1_ffn_w_down, l2_mix_norm, l2_mla_w_down, l2_mla_q_a_norm, l2_mla_kv_a_norm, l2_mla_w_uq, l2_mla_w_ukv, l2_mla_q_nope_norm, l2_mla_q_rope_norm, l2_mla_k_nope_norm, l2_mla_k_rope_norm, l2_mla_w_out, l2_ffn_norm, l2_ffn_w_gate_up, l2_ffn_w_down, l3_mix_norm, l3_dn_w_in, l3_dn_conv_w, l3_dn_a_log, l3_dn_dt_bias, l3_dn_out_norm, l3_dn_w_out, l3_ffn_norm, l3_ffn_w_gate_up, l3_ffn_w_down, loss_target, m_l0_mix_norm, m_l0_dn_w_in, m_l0_dn_conv_w, m_l0_dn_a_log, m_l0_dn_dt_bias, m_l0_dn_out_norm, m_l0_dn_w_out, m_l0_ffn_norm, m_l0_ffn_w_gate_up, m_l0_ffn_w_down, m_l1_mix_norm, m_l1_sb_w_qkv, m_l1_sb_q_norm, m_l1_sb_k_norm, m_l1_sb_w_out, m_l1_ffn_norm, m_l1_ffn_w_gate_up, m_l1_ffn_w_down, m_l2_mix_norm, m_l2_mla_w_down, m_l2_mla_q_a_norm, m_l2_mla_kv_a_norm, m_l2_mla_w_uq, m_l2_mla_w_ukv, m_l2_mla_q_nope_norm, m_l2_mla_q_rope_norm, m_l2_mla_k_nope_norm, m_l2_mla_k_rope_norm, m_l2_mla_w_out, m_l2_ffn_norm, m_l2_ffn_w_gate_up, m_l2_ffn_w_down, m_l3_mix_norm, m_l3_dn_w_in, m_l3_dn_conv_w, m_l3_dn_a_log, m_l3_dn_dt_bias, m_l3_dn_out_norm, m_l3_dn_w_out, m_l3_ffn_norm, m_l3_ffn_w_gate_up, m_l3_ffn_w_down, v_l0_mix_norm, v_l0_dn_w_in, v_l0_dn_conv_w, v_l0_dn_a_log, v_l0_dn_dt_bias, v_l0_dn_out_norm, v_l0_dn_w_out, v_l0_ffn_norm, v_l0_ffn_w_gate_up, v_l0_ffn_w_down, v_l1_mix_norm, v_l1_sb_w_qkv, v_l1_sb_q_norm, v_l1_sb_k_norm, v_l1_sb_w_out, v_l1_ffn_norm, v_l1_ffn_w_gate_up, v_l1_ffn_w_down, v_l2_mix_norm, v_l2_mla_w_down, v_l2_mla_q_a_norm, v_l2_mla_kv_a_norm, v_l2_mla_w_uq, v_l2_mla_w_ukv, v_l2_mla_q_nope_norm, v_l2_mla_q_rope_norm, v_l2_mla_k_nope_norm, v_l2_mla_k_rope_norm, v_l2_mla_w_out, v_l2_ffn_norm, v_l2_ffn_w_gate_up, v_l2_ffn_w_down, v_l3_mix_norm, v_l3_dn_w_in, v_l3_dn_conv_w, v_l3_dn_a_log, v_l3_dn_dt_bias, v_l3_dn_out_norm, v_l3_dn_w_out, v_l3_ffn_norm, v_l3_ffn_w_gate_up, v_l3_ffn_w_down):
    return _step(dict(locals()))
```

```python
import functools
import math

import jax
import jax.numpy as jnp
from jax import lax
from jax.experimental import pallas as pl
from jax.experimental.pallas import tpu as pltpu

F32, BF16 = jnp.float32, jnp.bfloat16
MESH = pl.DeviceIdType.MESH

D_MODEL = 1024
N_HEADS = 8
HEAD = 128
FFN_HIDDEN = 2816
DN_CHUNK = 64
NORM_EPS = 1e-6
MLA_ROPE = 64
MLA_QK = 192
ROPE_THETA = 10000.0
ADAM_LR, ADAM_B1, ADAM_B2, ADAM_EPS, ADAM_WD, ADAM_STEP = 0.001, 0.9, 0.999, 1e-08, 0.01, 10
N_CHIPS = 4
LANES = 128
VMEM_LIMIT = 56 * 2 ** 20


def _params(n_grid):
    return pltpu.CompilerParams(dimension_semantics=("arbitrary",) * n_grid, vmem_limit_bytes=VMEM_LIMIT)


_MIXERS = (
    ("dn_w_in", "dn_conv_w", "dn_a_log", "dn_dt_bias", "dn_out_norm", "dn_w_out"),
    ("sb_w_qkv", "sb_q_norm", "sb_k_norm", "sb_w_out"),
    ("mla_w_down", "mla_q_a_norm", "mla_kv_a_norm", "mla_w_uq", "mla_w_ukv", "mla_q_nope_norm",
     "mla_q_rope_norm", "mla_k_nope_norm", "mla_k_rope_norm", "mla_w_out"),
)
DEPTH = 4


def _layer_names(i):
    p = "l%d_" % i
    return [p + "mix_norm"] + [p + n for n in _MIXERS[i % 3]] + [p + "ffn_norm", p + "ffn_w_gate_up", p + "ffn_w_down"]


WEIGHTS = [n for i in range(DEPTH) for n in _layer_names(i)]
_ROW_SHARDED = ("w_out", "ffn_w_down", "mla_w_down")
_COL_SHARDED = ("dn_w_in", "sb_w_qkv", "mla_w_uq", "mla_w_ukv", "ffn_w_gate_up")


def _shard_axis(name):
    if name.endswith(_ROW_SHARDED):
        return 0
    if name.endswith(_COL_SHARDED):
        return 1
    return None


BIG = [n for n in WEIGHTS if _shard_axis(n) is not None]
SMALL = [n for n in WEIGHTS if _shard_axis(n) is None]


_DN = {"nn": (((1,), (0,)), ((), ())), "nt": (((1,), (1,)), ((), ())), "tn": (((0,), (0,)), ((), ()))}
_DN_BATCHED = {"nn": (((2,), (1,)), ((0,), (0,))), "nt": (((2,), (2,)), ((0,), (0,))), "tn": (((1,), (1,)), ((0,), (0,)))}


def _dims(a, kind):
    return _DN_BATCHED[kind] if a.ndim == 3 else _DN[kind]


def _dg(a, b, kind):
    return lax.dot_general(a.astype(BF16), b.astype(BF16), _dims(a, kind), preferred_element_type=F32)


@functools.partial(jax.custom_vjp, nondiff_argnums=(2,))
def bdot(a, b, kind):
    return _dg(a, b, kind)


def _bdot_fwd(a, b, kind):
    return _dg(a, b, kind), (a, b)


def _bdot_bwd(kind, res, ct):
    a, b = res
    if kind == "nn":
        return _dg(ct, b, "nt"), _dg(a, ct, "tn")
    if kind == "nt":
        return _dg(ct, b, "nn"), _dg(ct, a, "tn")
    return _dg(b, ct, "nt"), _dg(a, ct, "nn")


bdot.defvjp(_bdot_fwd, _bdot_bwd)


def _split(a, terms):
    out = []
    for _ in range(terms):
        hi = a.astype(BF16)
        out.append(hi)
        a = a - hi.astype(F32)
    return out


def _xdot(a, b, kind, exact, terms=3):
    if exact == 0:
        return sum(lax.dot_general(a, p, _dims(a, kind), preferred_element_type=F32) for p in _split(b, terms))
    return sum(lax.dot_general(p, b, _dims(a, kind), preferred_element_type=F32) for p in _split(a, terms))


def _tri(n, rel):
    r = lax.broadcasted_iota(jnp.int32, (n, n), 0)
    c = lax.broadcasted_iota(jnp.int32, (n, n), 1)
    return {"le": c <= r, "lt": c < r, "ge": c >= r, "gt": c > r}[rel]


def _running(g):
    n = g.shape[-2]
    return jnp.broadcast_to(_tri(n, "le").astype(BF16), g.shape[:-2] + (n, n))


@jax.custom_vjp
def cumsum_rows(g):
    return _xdot(_running(g), g, "nn", 0)


def _cumsum_fwd(g):
    return cumsum_rows(g), None


def _cumsum_bwd(_, ct):
    return (_xdot(_running(ct), ct, "tn", 0),)


cumsum_rows.defvjp(_cumsum_fwd, _cumsum_bwd)


def _dot3(a, b, kind):
    (ah, al), (bh, bl) = _split(a, 2), _split(b, 2)
    dot = lambda p, q: lax.dot_general(p, q, _dims(a, kind), preferred_element_type=F32)
    return dot(ah, bh) + (dot(ah, bl) + dot(al, bh))


@functools.partial(jax.custom_vjp, nondiff_argnums=(2,))
def _hdot3(a, b, kind):
    return _dot3(a, b, kind)


def _hdot3_fwd(a, b, kind):
    return _dot3(a, b, kind), (a, b)


def _hdot3_bwd(kind, res, ct):
    a, b = res
    if kind == "nn":
        return _dot3(ct, b, "nt"), _dot3(a, ct, "tn")
    if kind == "nt":
        return _dot3(ct, b, "nn"), _dot3(ct, a, "tn")
    return _dot3(b, ct, "nt"), _dot3(a, ct, "nn")


_hdot3.defvjp(_hdot3_fwd, _hdot3_bwd)


def _hdot(a, b):
    return _hdot3(a, b, "nn")


def _unit_lower_inverse(lower):
    n = lower.shape[-1]
    eye = (lax.broadcasted_iota(jnp.int32, (n, n), 0) == lax.broadcasted_iota(jnp.int32, (n, n), 1)).astype(F32)
    m = -lower
    p = eye + m
    for _ in range(int(math.log2(n)) - 1):
        m = _hdot(m, m)
        p = p + _hdot(p, m)
    return p


def _rms(x, g, n=None):
    n = x.shape[-1] if n is None else n
    return x * lax.rsqrt(jnp.sum(x * x, axis=-1, keepdims=True) * (1.0 / n) + NORM_EPS) * g


def _l2(x):
    return x * lax.rsqrt(jnp.sum(x * x, axis=-1, keepdims=True) + NORM_EPS)


def _silu(x):
    return x * jax.nn.sigmoid(x)


def _logsig(z):
    return jnp.minimum(z, 0.0) - jnp.log1p(jnp.exp(-jnp.abs(z)))


@jax.custom_vjp
def _rope(x, cos, sin_lo, sin_hi):
    return x * cos + pltpu.roll(x, 96, 1) * sin_lo + pltpu.roll(x, 32, 1) * sin_hi


def _rope_fwd(x, cos, sin_lo, sin_hi):
    return _rope(x, cos, sin_lo, sin_hi), (cos, sin_lo, sin_hi)


def _rope_bwd(res, ct):
    cos, sin_lo, sin_hi = res
    dx = ct * cos + pltpu.roll(ct * sin_lo, 32, 1) + pltpu.roll(ct * sin_hi, 96, 1)
    return dx, jnp.zeros_like(cos), jnp.zeros_like(sin_lo), jnp.zeros_like(sin_hi)


_rope.defvjp(_rope_fwd, _rope_bwd)


def _tile(n, prefs=(512, 384, 256, 128)):
    for t in prefs:
        if n % t == 0:
            return t
    return n


MM_OUT_TILES = (1024, 1408, 512, 384, 256, 128)
MM_K_TILES = (1024, 512, 384, 256, 128)


def _mm(name, a, b, kind, out_dtype=F32, add=None):
    if kind == "tn":
        (kdim, m), n = a.shape, b.shape[1]
    else:
        (m, kdim), n = a.shape, (b.shape[0] if kind == "nt" else b.shape[1])
    tm, tn, tk = _tile(m, MM_OUT_TILES), _tile(n, MM_OUT_TILES), _tile(kdim, MM_K_TILES)
    nk = kdim // tk
    a_spec = pl.BlockSpec((tk, tm), lambda i, j, k: (k, i)) if kind == "tn" else pl.BlockSpec((tm, tk), lambda i, j, k: (i, k))
    b_spec = pl.BlockSpec((tn, tk), lambda i, j, k: (j, k)) if kind == "nt" else pl.BlockSpec((tk, tn), lambda i, j, k: (k, j))
    o_spec = pl.BlockSpec((tm, tn), lambda i, j, k: (i, j))
    has_add = add is not None

    def body(*refs):
        a_ref, b_ref = refs[0], refs[1]
        o_ref, acc = refs[-2], refs[-1]
        k = pl.program_id(2)

        @pl.when(k == 0)
        def _():
            acc[...] = jnp.zeros_like(acc)

        acc[...] += _dg(a_ref[...], b_ref[...], kind)

        @pl.when(k == nk - 1)
        def _():
            r = acc[...]
            if has_add:
                r = r + refs[2][...]
            o_ref[...] = r.astype(o_ref.dtype)

    return pl.pallas_call(
        body, grid=(m // tm, n // tn, nk),
        in_specs=[a_spec, b_spec] + ([o_spec] if has_add else []),
        out_specs=o_spec, out_shape=jax.ShapeDtypeStruct((m, n), out_dtype),
        scratch_shapes=[pltpu.VMEM((tm, tn), F32)], name=name, compiler_params=_params(3),
    )(*([a, b] + ([add] if has_add else [])))


class _V:
    def __init__(self, arr, w=None, base=0, diff=True):
        self.arr, self.base, self.diff = arr, base, diff
        self.w = arr.shape[1] if w is None else w

    def spec(self, tm):
        return pl.BlockSpec((tm, self.w), lambda i, b=self.base: (i, b))


def _as_views(ins):
    return [v if isinstance(v, _V) else _V(v) for v in ins]


def _tup(r):
    return tuple(r) if isinstance(r, (tuple, list)) else (r,)


def _ew(name, fn, ins, smalls, outs, tm=256):
    ins = _as_views(ins)
    t = ins[0].arr.shape[0]
    tm = min(tm, t)
    n_in = len(ins) + len(smalls)

    def body(*refs):
        res = _tup(fn(*[r[...] for r in refs[:n_in]]))
        for r, o in zip(refs[n_in:], res):
            r[...] = o.astype(r.dtype)

    return pl.pallas_call(
        body, grid=(t // tm,),
        in_specs=[v.spec(tm) for v in ins] + [pl.BlockSpec(s.shape, lambda i: (0, 0)) for s in smalls],
        out_specs=[pl.BlockSpec((tm, w), lambda i: (i, 0)) for w, _ in outs],
        out_shape=[jax.ShapeDtypeStruct((t, w), dt) for w, dt in outs],
        name=name, compiler_params=_params(1),
    )(*[v.arr for v in ins], *smalls)


def _ew_bwd(name, fn, ins, smalls, cts, tm=256, add=None, ct_dtypes=None):
    ins = _as_views(ins)
    t = ins[0].arr.shape[0]
    tm = min(tm, t)
    n_in, n_sm = len(ins), len(smalls)
    diff = [k for k, v in enumerate(ins) if v.diff]
    ct_dtypes = [F32] * len(diff) if ct_dtypes is None else ct_dtypes
    ct_arrs = [c for c in cts if c is not None]
    has_add = add is not None

    def body(*refs):
        vals = [r[...] for r in refs[:n_in]]
        svals = [r[...] for r in refs[n_in:n_in + n_sm]]
        p = n_in + n_sm
        ct_refs = list(refs[p:p + len(ct_arrs)])
        p += len(ct_arrs)
        add_ref = refs[p] if has_add else None
        p += int(has_add)
        din_refs = refs[p:p + len(diff)]
        dsm_refs = refs[p + len(diff):]

        def f(dv, sv):
            full = list(vals)
            for k, d in zip(diff, dv):
                full[k] = d
            return _tup(fn(*full, *sv))

        res, vjp = jax.vjp(f, [vals[k] for k in diff], svals)
        ctv = tuple(jnp.zeros_like(o) if c is None else ct_refs.pop(0)[...].astype(o.dtype) for c, o in zip(cts, res))
        dv, dsv = vjp(ctv)
        for n, (r, d) in enumerate(zip(din_refs, dv)):
            if n == 0 and has_add:
                d = d + add_ref[...]
            r[...] = d.astype(r.dtype)

        @pl.when(pl.program_id(0) == 0)
        def _():
            for r in dsm_refs:
                r[...] = jnp.zeros_like(r)

        for r, d in zip(dsm_refs, dsv):
            r[...] += d

    row = lambda w: pl.BlockSpec((tm, w), lambda i: (i, 0))
    small_specs = [pl.BlockSpec(s.shape, lambda i: (0, 0)) for s in smalls]
    out = pl.pallas_call(
        body, grid=(t // tm,),
        in_specs=[v.spec(tm) for v in ins] + small_specs + [row(c.shape[1]) for c in ct_arrs]
        + ([row(add.shape[1])] if has_add else []),
        out_specs=[row(ins[k].w) for k in diff] + small_specs,
        out_shape=[jax.ShapeDtypeStruct((t, ins[k].w), dt) for k, dt in zip(diff, ct_dtypes)]
        + [jax.ShapeDtypeStruct(s.shape, F32) for s in smalls],
        name=name, compiler_params=_params(1),
    )(*[v.arr for v in ins], *smalls, *ct_arrs, *([add] if has_add else []))
    return out[:len(diff)], out[len(diff):]


BQ = 256
HPB = 2
SUM_TERMS = 2


def _cat(parts):
    return parts[0] if len(parts) == 1 else jnp.concatenate(parts, axis=1)


def _head_view(ref, hh):
    return ref.at[:, hh * HEAD:(hh + 1) * HEAD]


def _attn_specs(qs, ks, t):
    q_specs = [pl.BlockSpec((BQ, HPB * HEAD), lambda h, i: (i, h)) for _ in qs]
    per_head = pl.BlockSpec((t, HPB * HEAD), lambda h, i: (0, h))
    k_specs = [pl.BlockSpec((t, HEAD), lambda h, i: (0, 0)) if sh else per_head for _, sh in ks]
    return q_specs, k_specs, per_head


def _causal_sweep(i, pair, init, diagonal_first, log2_blocks):
    order = (lambda s: i - 1 - s) if diagonal_first else (lambda s: s)
    carry = pair(i, init, True) if diagonal_first else init
    done = 0
    for level in range(log2_blocks, -1, -1):
        per = 1 << level

        def group(s, c, per=per, done=done):
            for u in range(per):
                c = pair(order(done + per * s + u), c, False)
            return c

        left = i - done
        carry = lax.fori_loop(0, lax.shift_right_logical(left, level), group, carry)
        done = done + (left - (left & (per - 1)))
    return carry if diagonal_first else pair(i, carry, True)


def _attn_fwd(name, mode, qs, ks, v, gather=None):
    t = qs[0].shape[0]
    nq, n = t // BQ, len(qs)
    q_specs, k_specs, per_head = _attn_specs(qs, ks, t)
    shared = [sh for _, sh in ks]

    def body(*refs):
        q_refs, k_refs, v_ref = refs[:n], refs[n:2 * n], refs[2 * n]
        n_in = 2 * n + 1 + int(gather is not None)
        o_ref, st_ref = refs[n_in], refs[n_in + 1]
        g, i = pl.program_id(0), pl.program_id(1)
        last_g = N_HEADS // HPB - 1
        if gather is not None:
            phases = _gather_phases(refs[n_in - 1], refs[n_in + 2], refs[-2], refs[-1])
            pl.when((g == 0) & (i == 0))(phases[0])
            pl.when((g == last_g) & (i == 0))(phases[1])
        row = lax.broadcasted_iota(jnp.int32, (BQ, BQ), 0)
        col = lax.broadcasted_iota(jnp.int32, (BQ, BQ), 1)
        after = _tri(BQ, "lt").astype(BF16)

        def head(hh):
            q = _cat([_head_view(r, hh)[...] for r in q_refs])
            k_h = [kr if sh else _head_view(kr, hh) for kr, sh in zip(k_refs, shared)]
            v_h = _head_view(v_ref, hh)

            def pair(j, carry, masked):
                off = pl.multiple_of(j * BQ, BQ)
                z = _dg(q, _cat([kr[pl.ds(off, BQ), :] for kr in k_h]), "nt")
                vj = v_h[pl.ds(off, BQ), :]
                if mode == "sb":
                    acc, run = carry
                    lsz = _logsig(z)
                    stay = lsz - z
                    if masked:
                        stay = jnp.where(col < row, stay, 0.0)
                    a = jnp.exp(lsz + (run + _xdot(stay, after, "nn", 1, SUM_TERMS)))
                    if masked:
                        a = jnp.where(col < row, a, 0.0)
                    return acc + _dg(a, vj, "nn"), run + jnp.sum(stay, axis=1, keepdims=True)
                m, l, acc = carry
                if masked:
                    z = jnp.where(col <= row, z, -1e30)
                m2 = jnp.maximum(m, jnp.max(z, axis=1, keepdims=True))
                p = jnp.exp(z - m2)
                alpha = jnp.exp(m - m2)
                return m2, alpha * l + jnp.sum(p, axis=1, keepdims=True), alpha * acc + _dg(p, vj, "nn")

            def finish(carry):
                if mode == "sb":
                    acc, run = carry
                    _head_view(o_ref, hh)[...] = acc
                    _head_view(st_ref, hh)[...] = jnp.broadcast_to(run, (BQ, HEAD))
                else:
                    m, l, acc = carry
                    _head_view(o_ref, hh)[...] = acc / l
                    _head_view(st_ref, hh)[...] = jnp.broadcast_to(m + jnp.log(l), (BQ, HEAD))

            zero = jnp.zeros((BQ, 1), F32)
            acc0 = jnp.zeros((BQ, HEAD), F32)
            init = (acc0, zero) if mode == "sb" else (jnp.full((BQ, 1), -1e30, F32), zero, acc0)
            return pair, init, finish

        heads = [head(hh) for hh in range(HPB)]
        both = lambda j, carry, masked: tuple(h[0](j, c, masked) for h, c in zip(heads, carry))
        final = _causal_sweep(i, both, tuple(h[1] for h in heads), diagonal_first=(mode == "sb"), log2_blocks=2)
        for h, c in zip(heads, final):
            h[2](c)
        if gather is not None:
            pl.when((g == last_g) & (i == nq - 1))(phases[2])

    blk = pl.BlockSpec((BQ, HPB * HEAD), lambda h, i: (i, h))
    wide = jax.ShapeDtypeStruct((t, N_HEADS * HEAD), F32)
    comm_in, comm_out, comm_shape, comm_scratch = [], [], [], []
    if gather is not None:
        comm_in, comm_out = [gather], [HBM]
        comm_shape = [jax.ShapeDtypeStruct((N_CHIPS,) + gather.shape, gather.dtype)]
        comm_scratch = [pltpu.SemaphoreType.DMA((GATHER_SEMS,)), pltpu.SemaphoreType.DMA((GATHER_SEMS,))]
    return pl.pallas_call(
        body, grid=(N_HEADS // HPB, nq), in_specs=q_specs + k_specs + [per_head] + [HBM] * len(comm_in),
        out_specs=[blk, blk] + comm_out, out_shape=[wide, wide] + comm_shape, scratch_shapes=comm_scratch,
        name=name, compiler_params=_params(2),
    )(*qs, *[k for k, _ in ks], v, *comm_in)


def _attn_bwd(name, mode, qs, ks, v, o, stat, do, send=None):
    t = qs[0].shape[0]
    nq, n = t // BQ, len(qs)
    q_specs, k_specs, per_head = _attn_specs(qs, ks, t)
    shared = [sh for _, sh in ks]

    def body(*refs):
        q_refs, k_refs, v_ref = refs[:n], refs[n:2 * n], refs[2 * n]
        o_ref, st_ref, do_ref = refs[2 * n + 1:2 * n + 4]
        n_in = 2 * n + 4 + int(send is not None)
        dq_refs = refs[n_in:n_in + n]
        dk_refs = refs[n_in + n:n_in + 2 * n]
        dv_ref = refs[n_in + 2 * n]
        g, i = pl.program_id(0), pl.program_id(1)
        if send is not None:
            send_ref, got_ref, send_sems, recv_sems = refs[n_in - 1], refs[n_in + 2 * n + 1], refs[-2], refs[-1]

            def exchange():
                x, y, c = _me()
                return [pltpu.make_async_remote_copy(src_ref=send_ref.at[2 * px + py], dst_ref=got_ref.at[j],
                                                     send_sem=send_sems.at[j], recv_sem=recv_sems.at[j],
                                                     device_id=(px, py, c), device_id_type=MESH)
                        for j, (px, py) in enumerate(_other_chips(x, y))]

            @pl.when((g == 0) & (i == 0))
            def _():
                for cp in exchange():
                    cp.start()

        @pl.when(i == 0)
        def _():
            dv_ref[...] = jnp.zeros_like(dv_ref)
            for r, sh in zip(dk_refs, shared):
                if not sh:
                    r[...] = jnp.zeros_like(r)

        for r, sh in zip(dk_refs, shared):
            if sh:
                @pl.when((i == 0) & (g == 0))
                def _(r=r):
                    r[...] = jnp.zeros_like(r)

        row = lax.broadcasted_iota(jnp.int32, (BQ, BQ), 0)
        col = lax.broadcasted_iota(jnp.int32, (BQ, BQ), 1)
        upto = _tri(BQ, "ge").astype(BF16)
        before = _tri(BQ, "gt").astype(BF16)

        def head(hh):
            q = _cat([_head_view(r, hh)[...] for r in q_refs])
            k_h = [kr if sh else _head_view(kr, hh) for kr, sh in zip(k_refs, shared)]
            dk_h = [r if sh else _head_view(r, hh) for r, sh in zip(dk_refs, shared)]
            v_h, dv_h = _head_view(v_ref, hh), _head_view(dv_ref, hh)
            do_t = _head_view(do_ref, hh)[...]
            st = _head_view(st_ref, hh)[:, :1]
            if mode == "softmax":
                dsum = jnp.sum(do_t * _head_view(o_ref, hh)[...], axis=1, keepdims=True)

            def pair(j, carry, masked):
                off = pl.multiple_of(j * BQ, BQ)
                kj = _cat([kr[pl.ds(off, BQ), :] for kr in k_h])
                z = _dg(q, kj, "nt")
                da = _dg(do_t, v_h[pl.ds(off, BQ), :], "nt")
                if mode == "sb":
                    dq, pre, gpre = carry
                    lsz = _logsig(z)
                    stay = lsz - z
                    if masked:
                        stay = jnp.where(col < row, stay, 0.0)
                    a = jnp.exp(lsz + (st - (pre + _xdot(stay, upto, "nn", 1, SUM_TERMS))))
                    if masked:
                        a = jnp.where(col < row, a, 0.0)
                    gr = a * da
                    sig = jnp.exp(lsz)
                    dz = gr * (1.0 - sig) - sig * (gpre + _xdot(gr, before, "nn", 1, SUM_TERMS))
                    if masked:
                        dz = jnp.where(col < row, dz, 0.0)
                    tail = (pre + jnp.sum(stay, axis=1, keepdims=True), gpre + jnp.sum(gr, axis=1, keepdims=True))
                else:
                    dq = carry[0]
                    a = jnp.exp(z - st)
                    if masked:
                        a = jnp.where(col <= row, a, 0.0)
                    dz = a * (da - dsum)
                    tail = ()
                dk = _dg(dz, q, "tn")
                for p, r in enumerate(dk_h):
                    r[pl.ds(off, BQ), :] += dk[:, p * HEAD:(p + 1) * HEAD]
                dv_h[pl.ds(off, BQ), :] += _dg(a, do_t, "tn")
                return (dq + _dg(dz, kj, "nn"),) + tail

            def finish(carry):
                for p, r in enumerate(dq_refs):
                    _head_view(r, hh)[...] = carry[0][:, p * HEAD:(p + 1) * HEAD]

            zero = jnp.zeros((BQ, 1), F32)
            init = (jnp.zeros((BQ, n * HEAD), F32),) + ((zero, zero) if mode == "sb" else ())
            return pair, init, finish

        heads = [head(hh) for hh in range(HPB)]
        both = lambda j, carry, masked: tuple(h[0](j, c, masked) for h, c in zip(heads, carry))
        final = _causal_sweep(i, both, tuple(h[1] for h in heads), diagonal_first=False, log2_blocks=1)
        for h, c in zip(heads, final):
            h[2](c)

        if send is not None:
            @pl.when((g == N_HEADS // HPB - 1) & (i == nq - 1))
            def _():
                for cp in exchange():
                    cp.wait()

    blk = pl.BlockSpec((BQ, HPB * HEAD), lambda h, i: (i, h))
    dk_specs = [pl.BlockSpec((t, HEAD), lambda h, i: (0, 0)) if sh else per_head for sh in shared]
    wide = jax.ShapeDtypeStruct((t, N_HEADS * HEAD), F32)
    comm_in, comm_out, comm_shape, comm_scratch = [], [], [], []
    if send is not None:
        comm_in, comm_out = [send], [HBM]
        comm_shape = [jax.ShapeDtypeStruct((3,) + send.shape[1:], send.dtype)]
        comm_scratch = [pltpu.SemaphoreType.DMA((3,)), pltpu.SemaphoreType.DMA((3,))]
    out = pl.pallas_call(
        body, grid=(N_HEADS // HPB, nq), in_specs=q_specs + k_specs + [per_head, blk, blk, blk] + [HBM] * len(comm_in),
        out_specs=[blk] * n + dk_specs + [per_head] + comm_out,
        out_shape=[wide] * n + [jax.ShapeDtypeStruct((t, HEAD), F32) if sh else wide for sh in shared] + [wide] + comm_shape,
        scratch_shapes=comm_scratch, name=name, compiler_params=_params(2),
    )(*qs, *[k for k, _ in ks], v, o, stat, do, *comm_in)
    return (out[:n], out[n:2 * n], out[2 * n]) + tuple(out[2 * n + 1:])


@jax.custom_vjp
def _given_inverse(lower, tinv):
    return tinv


def _given_inverse_fwd(lower, tinv):
    return tinv, tinv


def _given_inverse_bwd(tinv, ct):
    return -_dot3(_dot3(tinv, ct, "tn"), tinv, "nt"), jnp.zeros_like(tinv)


_given_inverse.defvjp(_given_inverse_fwd, _given_inverse_bwd)


def _dn_chunk(q, k, v, g, beta, state, tinv=None):
    c = q.shape[-2]
    gc = cumsum_rows(g)
    gcc = gc[..., :c]
    diff = gcc - jnp.swapaxes(gcc, -1, -2)
    causal, strict = _tri(c, "le"), _tri(c, "lt")
    decay = jnp.where(causal, jnp.exp(jnp.where(causal, diff, 0.0)), 0.0)
    kb = k * beta
    lower = jnp.where(strict, bdot(kb, k, "nt") * decay, 0.0)
    tinv = _unit_lower_inverse(lower) if tinv is None else _given_inverse(lower, tinv)
    eg = jnp.exp(gc)
    u = _hdot(tinv, v * beta)
    w = _hdot(tinv, kb * eg)
    attn = bdot(q, k, "nt") * decay
    glast = gc[..., c - 1:c, :]
    v_new = u - bdot(w, state, "nn")
    o = bdot(q * eg, state, "nn") + bdot(attn, v_new, "nn")
    new_state = state * jnp.exp(glast) + bdot(k * jnp.exp(glast - gc), v_new, "tn")
    return o, new_state, tinv


def _stack_heads(ref):
    return jnp.stack([ref[:, h * HEAD:(h + 1) * HEAD] for h in range(N_HEADS)])


def _store_heads(ref, val):
    for h in range(N_HEADS):
        ref[:, h * HEAD:(h + 1) * HEAD] = val[h]


def _dn_fwd(name, q, k, v, g, beta):
    t = q.shape[0]
    nc = t // DN_CHUNK
    wide = N_HEADS * HEAD
    blk = pl.BlockSpec((DN_CHUNK, wide), lambda n: (n, 0))
    st_spec = pl.BlockSpec((N_HEADS, None, HEAD, HEAD), lambda n: (0, n, 0, 0))
    inv_spec = pl.BlockSpec((N_HEADS, None, DN_CHUNK, DN_CHUNK), lambda n: (0, n, 0, 0))

    def body(q_ref, k_ref, v_ref, g_ref, b_ref, o_ref, st_ref, inv_ref, state):
        @pl.when(pl.program_id(0) == 0)
        def _():
            state[...] = jnp.zeros_like(state)

        s_in = state[...]
        st_ref[...] = s_in
        o, s_out, tinv = _dn_chunk(*[_stack_heads(r) for r in (q_ref, k_ref, v_ref, g_ref, b_ref)], s_in)
        _store_heads(o_ref, o)
        inv_ref[...] = tinv
        state[...] = s_out

    return pl.pallas_call(
        body, grid=(nc,), in_specs=[blk] * 5, out_specs=[blk, st_spec, inv_spec],
        out_shape=[jax.ShapeDtypeStruct((t, wide), F32), jax.ShapeDtypeStruct((N_HEADS, nc, HEAD, HEAD), F32),
                   jax.ShapeDtypeStruct((N_HEADS, nc, DN_CHUNK, DN_CHUNK), F32)],
        scratch_shapes=[pltpu.VMEM((N_HEADS, HEAD, HEAD), F32)], name=name, compiler_params=_params(1),
    )(q, k, v, g, beta)


def _dn_bwd(name, q, k, v, g, beta, states, inverses, do):
    t = q.shape[0]
    nc = t // DN_CHUNK
    wide = N_HEADS * HEAD
    blk = pl.BlockSpec((DN_CHUNK, wide), lambda n: (nc - 1 - n, 0))
    st_spec = pl.BlockSpec((N_HEADS, None, HEAD, HEAD), lambda n: (0, nc - 1 - n, 0, 0))
    inv_spec = pl.BlockSpec((N_HEADS, None, DN_CHUNK, DN_CHUNK), lambda n: (0, nc - 1 - n, 0, 0))

    def body(q_ref, k_ref, v_ref, g_ref, b_ref, st_ref, inv_ref, do_ref, dq_ref, dk_ref, dv_ref, dg_ref, db_ref, dstate):
        @pl.when(pl.program_id(0) == 0)
        def _():
            dstate[...] = jnp.zeros_like(dstate)

        tinv = inv_ref[...]
        chunk = lambda *args: _dn_chunk(*args, tinv=tinv)[:2]
        _, vjp = jax.vjp(chunk, *[_stack_heads(r) for r in (q_ref, k_ref, v_ref, g_ref, b_ref)], st_ref[...])
        cts = vjp((_stack_heads(do_ref), dstate[...]))
        for r, d in zip((dq_ref, dk_ref, dv_ref, dg_ref, db_ref), cts[:5]):
            _store_heads(r, d)
        dstate[...] = cts[5]

    shape = jax.ShapeDtypeStruct((t, wide), F32)
    return pl.pallas_call(
        body, grid=(nc,), in_specs=[blk] * 5 + [st_spec, inv_spec, blk], out_specs=[blk] * 5, out_shape=[shape] * 5,
        scratch_shapes=[pltpu.VMEM((N_HEADS, HEAD, HEAD), F32)], name=name, compiler_params=_params(1),
    )(q, k, v, g, beta, states, inverses, do)


CONV_W = 1024
HALO = 8


def _shift_down(cur, prev, s):
    sh = pltpu.roll(cur, s, 0)
    ph = pltpu.roll(prev, s, 0)
    r = lax.broadcasted_iota(jnp.int32, (HALO, cur.shape[1]), 0)
    return jnp.concatenate([jnp.where(r < s, ph, sh[:HALO]), sh[HALO:]], axis=0)


def _shift_up(cur, nxt, s):
    tm = cur.shape[0]
    sh = pltpu.roll(cur, tm - s, 0)
    nh = pltpu.roll(nxt, HALO - s, 0)
    r = lax.broadcasted_iota(jnp.int32, (HALO, cur.shape[1]), 0)
    return jnp.concatenate([sh[:tm - HALO], jnp.where(r >= HALO - s, nh, sh[tm - HALO:])], axis=0)


def _conv_fwd(name, proj, w, tm=256):
    t = proj.shape[0]
    tm = min(tm, t)
    width = w.shape[1]
    per = tm // HALO

    def body(cur_ref, prev_ref, w_ref, y_ref):
        cur = cur_ref[...]
        prev = jnp.where(pl.program_id(0) > 0, prev_ref[...], 0.0)
        y = cur * w_ref[3:4, :]
        for s in (1, 2, 3):
            y = y + _shift_down(cur, prev, s) * w_ref[3 - s:4 - s, :]
        y_ref[...] = y

    return pl.pallas_call(
        body, grid=(t // tm, width // CONV_W),
        in_specs=[pl.BlockSpec((tm, CONV_W), lambda i, c: (i, c)),
                  pl.BlockSpec((HALO, CONV_W), lambda i, c: (jnp.maximum(i * per - 1, 0), c)),
                  pl.BlockSpec((HALO, CONV_W), lambda i, c: (0, c))],
        out_specs=pl.BlockSpec((tm, CONV_W), lambda i, c: (i, c)),
        out_shape=jax.ShapeDtypeStruct((t, width), F32), name=name, compiler_params=_params(2),
    )(proj, proj, w)


def _conv_bwd(name, proj, w, dy, tm=256):
    t = proj.shape[0]
    tm = min(tm, t)
    width = w.shape[1]
    per, nt = tm // HALO, t // tm

    def body(cur_ref, prev_ref, w_ref, dy_ref, nxt_ref, du_ref, dw_ref):
        i = pl.program_id(1)
        cur, dy_t = cur_ref[...], dy_ref[...]
        prev = jnp.where(i > 0, prev_ref[...], 0.0)
        nxt = jnp.where(i < nt - 1, nxt_ref[...], 0.0)
        du = dy_t * w_ref[3:4, :]
        rows = [jnp.sum(dy_t * cur, axis=0, keepdims=True)]
        for s in (1, 2, 3):
            du = du + _shift_up(dy_t, nxt, s) * w_ref[3 - s:4 - s, :]
            rows.insert(0, jnp.sum(dy_t * _shift_down(cur, prev, s), axis=0, keepdims=True))
        du_ref[...] = du.astype(du_ref.dtype)

        @pl.when(i == 0)
        def _():
            dw_ref[...] = jnp.zeros_like(dw_ref)

        dw_ref[...] += jnp.concatenate(rows + [jnp.zeros((HALO - 4, CONV_W), F32)], axis=0)

    return pl.pallas_call(
        body, grid=(width // CONV_W, nt),
        in_specs=[pl.BlockSpec((tm, CONV_W), lambda c, i: (i, c)),
                  pl.BlockSpec((HALO, CONV_W), lambda c, i: (jnp.maximum(i * per - 1, 0), c)),
                  pl.BlockSpec((HALO, CONV_W), lambda c, i: (0, c)),
                  pl.BlockSpec((tm, CONV_W), lambda c, i: (i, c)),
                  pl.BlockSpec((HALO, CONV_W), lambda c, i: (jnp.minimum((i + 1) * per, t // HALO - 1), c))],
        out_specs=[pl.BlockSpec((tm, CONV_W), lambda c, i: (i, c)), pl.BlockSpec((HALO, CONV_W), lambda c, i: (0, c))],
        out_shape=[jax.ShapeDtypeStruct((t, width), BF16), jax.ShapeDtypeStruct((HALO, width), F32)],
        name=name, compiler_params=_params(2),
    )(proj, proj, w, dy, dy)


def _norm_fn(x, g):
    return _rms(x, g)


def _swiglu_fn(gu):
    return _silu(gu[:, :FFN_HIDDEN]) * gu[:, FFN_HIDDEN:]


def _heads(x):
    return [x[:, h * HEAD:(h + 1) * HEAD] for h in range(x.shape[1] // HEAD)]


def _dn_pre_fn(c, ab, a_log, dt_bias):
    w = N_HEADS * HEAD
    q = [_l2(_silu(x)) * (HEAD ** -0.5) for x in _heads(c[:, :w])]
    k = [_l2(_silu(x)) for x in _heads(c[:, w:2 * w])]
    v = _silu(c[:, 2 * w:])
    g, beta = [], []
    for h in range(N_HEADS):
        gh = -jnp.exp(a_log[:, h:h + 1]) * jax.nn.softplus(ab[:, h:h + 1] + dt_bias[:, h:h + 1])
        bh = jax.nn.sigmoid(ab[:, N_HEADS + h:N_HEADS + h + 1])
        g.append(jnp.broadcast_to(gh, (c.shape[0], HEAD)))
        beta.append(jnp.broadcast_to(bh, (c.shape[0], HEAD)))
    cat = lambda xs: jnp.concatenate(xs, axis=1)
    return cat(q), cat(k), v, cat(g), cat(beta)


def _dn_post_fn(o, z, out_norm):
    return jnp.concatenate([_rms(oh, out_norm) * _silu(zh) for oh, zh in zip(_heads(o), _heads(z))], axis=1)


def _sb_pre_fn(qkv, q_norm, k_norm):
    w = N_HEADS * HEAD
    q = [_rms(x, q_norm) * (HEAD ** -0.5) for x in _heads(qkv[:, :w])]
    k = [_rms(x, k_norm) for x in _heads(qkv[:, w:2 * w])]
    return jnp.concatenate(q, axis=1), jnp.concatenate(k, axis=1), qkv[:, 2 * w:]


def _mla_a_fn(down, cos, sin_lo, sin_hi, q_a_norm, kv_a_norm, k_rope_norm):
    cq = _rms(down[:, :256], q_a_norm)
    ckv = _rms(down[:, 256:384], kv_a_norm)
    kr = _rope(_rms(down[:, 384:], k_rope_norm, MLA_ROPE), cos, sin_lo, sin_hi)
    return cq, ckv, kr


def _mla_b_fn(qf, kvf, cos, sin_lo, sin_hi, q_nope_norm, q_rope_norm, k_nope_norm):
    scale = MLA_QK ** -0.5
    qn, qr, kn, v = [], [], [], []
    for h in range(N_HEADS):
        a = 2 * h * HEAD
        qn.append(_rms(qf[:, a:a + HEAD], q_nope_norm) * scale)
        qr.append(_rope(_rms(qf[:, a + HEAD:a + 2 * HEAD], q_rope_norm, MLA_ROPE), cos, sin_lo, sin_hi) * scale)
        kn.append(_rms(kvf[:, a:a + HEAD], k_nope_norm))
        v.append(kvf[:, a + HEAD:a + 2 * HEAD])
    cat = lambda xs: jnp.concatenate(xs, axis=1)
    return cat(qn), cat(qr), cat(kn), cat(v)


def _rope_tables(t):
    inv_freq = ROPE_THETA ** (-jnp.arange(0, MLA_ROPE, 2, dtype=F32) / MLA_ROPE)
    ang = jnp.arange(t, dtype=F32)[:, None] * inv_freq[None, :]
    cos, sin, zero = jnp.cos(ang), jnp.sin(ang), jnp.zeros((t, MLA_ROPE // 2), F32)
    cat = lambda xs: jnp.concatenate(xs, axis=1)
    return cat([cos, cos, zero, zero]), cat([-sin, zero, zero, zero]), cat([zero, sin, zero, zero])


def _row(v, width=None):
    width = v.shape[0] if width is None else width
    return jnp.pad(v.astype(F32), (0, width - v.shape[0])).reshape(1, width)


def _loss_kernel(y, target):
    t, d = y.shape
    tm = min(256, t)

    def body(y_ref, t_ref, part_ref, dy_ref):
        e = y_ref[...] - t_ref[...]
        dy_ref[...] = e * (1.0 / d)

        @pl.when(pl.program_id(0) == 0)
        def _():
            part_ref[...] = jnp.zeros_like(part_ref)

        part_ref[...] += jnp.sum(e * e, axis=0, keepdims=True)

    blk = pl.BlockSpec((tm, d), lambda i: (i, 0))
    one = pl.BlockSpec((1, d), lambda i: (0, 0))
    return pl.pallas_call(body, grid=(t // tm,), in_specs=[blk, blk], out_specs=[one, blk],
                          out_shape=[jax.ShapeDtypeStruct((1, d), F32), jax.ShapeDtypeStruct((t, d), F32)],
                          name="loss", compiler_params=_params(1))(y, target)


def _ffn_fwd(p, x, w, sm):
    h, = _ew(p + "ffn_norm", _norm_fn, [x], [sm["ffn_norm"]], [(D_MODEL, BF16)])
    gu = _mm(p + "ffn_gu", h, w["ffn_w_gate_up"], "nn")
    act, = _ew(p + "ffn_act", _swiglu_fn, [gu], [], [(FFN_HIDDEN, BF16)], tm=128)
    y = _mm(p + "ffn_down", act, w["ffn_w_down"], "nn", add=x)
    return y, (x, h, gu, act)


def _ffn_bwd(p, saved, dy, w, sm, grads):
    x, h, gu, act = saved
    dact = _mm(p + "ffn_down_dx", dy, w["ffn_w_down"], "nt")
    grads["ffn_w_down"] = _mm(p + "ffn_down_dw", act, dy, "tn")
    (dgu,), _ = _ew_bwd(p + "ffn_act_bwd", _swiglu_fn, [gu], [], [dact], tm=128, ct_dtypes=[BF16])
    dh = _mm(p + "ffn_gu_dx", dgu, w["ffn_w_gate_up"], "nt")
    grads["ffn_w_gate_up"] = _mm(p + "ffn_gu_dw", h, dgu, "tn")
    (dx,), (dg,) = _ew_bwd(p + "ffn_norm_bwd", _norm_fn, [x], [sm["ffn_norm"]], [dh], add=dy)
    grads["ffn_norm"] = dg
    return dx


def _dn_layer_fwd(p, x, w, sm):
    h, = _ew(p + "mix_norm", _norm_fn, [x], [sm["mix_norm"]], [(D_MODEL, BF16)])
    proj = _mm(p + "dn_in", h, w["dn_w_in"], "nn")
    conv = _conv_fwd(p + "dn_conv", proj, sm["dn_conv_w"])
    ab = _V(proj, LANES, 4 * N_HEADS)
    wide = N_HEADS * HEAD
    q, k, v, g, beta = _ew(p + "dn_pre", _dn_pre_fn, [conv, ab], [sm["dn_a_log"], sm["dn_dt_bias"]], [(wide, F32)] * 5)
    o, states, inverses = _dn_fwd(p + "dn_core", q, k, v, g, beta)
    z = _V(proj, wide, 3)
    on, = _ew(p + "dn_post", _dn_post_fn, [o, z], [sm["dn_out_norm"]], [(wide, BF16)])
    y = _mm(p + "dn_out", on, w["dn_w_out"], "nn", add=x)
    return y, (x, h, proj, conv, q, k, v, g, beta, o, states, inverses, on)


def _dn_layer_bwd(p, saved, dy, w, sm, grads):
    x, h, proj, conv, q, k, v, g, beta, o, states, inverses, on = saved
    wide = N_HEADS * HEAD
    don = _mm(p + "dn_out_dx", dy, w["dn_w_out"], "nt")
    grads["dn_w_out"] = _mm(p + "dn_out_dw", on, dy, "tn")
    (do, dz), (d_out_norm,) = _ew_bwd(p + "dn_post_bwd", _dn_post_fn, [o, _V(proj, wide, 3)], [sm["dn_out_norm"]], [don],
                                          ct_dtypes=[F32, BF16])
    grads["dn_out_norm"] = d_out_norm
    dq, dk, dv, dg, db = _dn_bwd(p + "dn_core_bwd", q, k, v, g, beta, states, inverses, do)
    (dconv, dab), (d_a_log, d_dt) = _ew_bwd(p + "dn_pre_bwd", _dn_pre_fn, [conv, _V(proj, LANES, 4 * N_HEADS)],
                                            [sm["dn_a_log"], sm["dn_dt_bias"]], [dq, dk, dv, dg, db], ct_dtypes=[F32, BF16])
    grads["dn_a_log"], grads["dn_dt_bias"] = d_a_log, d_dt
    dqkv, dconv_w = _conv_bwd(p + "dn_conv_bwd", proj, sm["dn_conv_w"], dconv)
    grads["dn_conv_w"] = dconv_w
    dproj = jnp.concatenate([dqkv, dz, dab], axis=1)
    dh = _mm(p + "dn_in_dx", dproj, w["dn_w_in"], "nt")
    grads["dn_w_in"] = _mm(p + "dn_in_dw", h, dproj, "tn")
    (dx,), (dgain,) = _ew_bwd(p + "mix_norm_bwd", _norm_fn, [x], [sm["mix_norm"]], [dh], add=dy)
    grads["mix_norm"] = dgain
    return dx


def _sb_layer_fwd(p, x, w, sm, exchange=None):
    h, = _ew(p + "mix_norm", _norm_fn, [x], [sm["mix_norm"]], [(D_MODEL, BF16)])
    qkv = _mm(p + "sb_qkv", h, w["sb_w_qkv"], "nn")
    wide = N_HEADS * HEAD
    q, k, v = _ew(p + "sb_pre", _sb_pre_fn, [qkv], [sm["sb_q_norm"], sm["sb_k_norm"]], [(wide, BF16)] * 3)
    gather = None if exchange is None else exchange.late_shards()
    o, stat, *gathered = _attn_fwd(p + "sb_core", "sb", [q], [(k, False)], v, gather=gather)
    if exchange is not None:
        exchange.deliver(gathered[0])
    y = _mm(p + "sb_out", o, w["sb_w_out"], "nn", add=x)
    return y, (x, h, qkv, q, k, v, o, stat)


def _sb_layer_bwd(p, saved, dy, w, sm, grads, exchange=None):
    x, h, qkv, q, k, v, o, stat = saved
    do = _mm(p + "sb_out_dx", dy, w["sb_w_out"], "nt")
    grads["sb_w_out"] = _mm(p + "sb_out_dw", o, dy, "tn")
    send = None if exchange is None else exchange.start(grads)
    (dq,), (dk,), dv, *arrived = _attn_bwd(p + "sb_core_bwd", "sb", [q], [(k, False)], v, o, stat, do, send=send)
    if exchange is not None:
        exchange.arrived = arrived[0]
    (dqkv,), (dqn, dkn) = _ew_bwd(p + "sb_pre_bwd", _sb_pre_fn, [qkv], [sm["sb_q_norm"], sm["sb_k_norm"]], [dq, dk, dv],
                                  ct_dtypes=[BF16])
    grads["sb_q_norm"], grads["sb_k_norm"] = dqn, dkn
    dh = _mm(p + "sb_qkv_dx", dqkv, w["sb_w_qkv"], "nt")
    grads["sb_w_qkv"] = _mm(p + "sb_qkv_dw", h, dqkv, "tn")
    (dx,), (dgain,) = _ew_bwd(p + "mix_norm_bwd", _norm_fn, [x], [sm["mix_norm"]], [dh], add=dy)
    grads["mix_norm"] = dgain
    return dx


def _mla_layer_fwd(p, x, w, sm):
    t = x.shape[0]
    tabs = [_V(a, diff=False) for a in _rope_tables(t)]
    h, = _ew(p + "mix_norm", _norm_fn, [x], [sm["mix_norm"]], [(D_MODEL, BF16)])
    down = _mm(p + "mla_down", h, w["mla_w_down"], "nn")
    sm_a = [sm["mla_q_a_norm"], sm["mla_kv_a_norm"], sm["mla_k_rope_norm"]]
    cq, ckv, kr = _ew(p + "mla_a", _mla_a_fn, [down] + tabs, sm_a, [(256, BF16), (128, BF16), (128, BF16)])
    qf = _mm(p + "mla_uq", cq, w["mla_w_uq"], "nn")
    kvf = _mm(p + "mla_ukv", ckv, w["mla_w_ukv"], "nn")
    sm_b = [sm["mla_q_nope_norm"], sm["mla_q_rope_norm"], sm["mla_k_nope_norm"]]
    wide = N_HEADS * HEAD
    qn, qr, kn, v = _ew(p + "mla_b", _mla_b_fn, [qf, kvf] + tabs, sm_b, [(wide, BF16)] * 4)
    o, stat = _attn_fwd(p + "mla_core", "softmax", [qn, qr], [(kn, False), (kr, True)], v)
    y = _mm(p + "mla_out", o, w["mla_w_out"], "nn", add=x)
    return y, (x, h, down, cq, ckv, kr, qf, kvf, qn, qr, kn, v, o, stat)


def _mla_layer_bwd(p, saved, dy, w, sm, grads):
    x, h, down, cq, ckv, kr, qf, kvf, qn, qr, kn, v, o, stat = saved
    tabs = [_V(a, diff=False) for a in _rope_tables(x.shape[0])]
    do = _mm(p + "mla_out_dx", dy, w["mla_w_out"], "nt")
    grads["mla_w_out"] = _mm(p + "mla_out_dw", o, dy, "tn")
    (dqn, dqr), (dkn, dkr), dv = _attn_bwd(p + "mla_core_bwd", "softmax", [qn, qr], [(kn, False), (kr, True)],
                                           v, o, stat, do)
    sm_b = [sm["mla_q_nope_norm"], sm["mla_q_rope_norm"], sm["mla_k_nope_norm"]]
    (dqf, dkvf), dsm_b = _ew_bwd(p + "mla_b_bwd", _mla_b_fn, [qf, kvf] + tabs, sm_b, [dqn, dqr, dkn, dv],
                                 ct_dtypes=[BF16, BF16])
    grads["mla_q_nope_norm"], grads["mla_q_rope_norm"], grads["mla_k_nope_norm"] = dsm_b
    dcq = _mm(p + "mla_uq_dx", dqf, w["mla_w_uq"], "nt")
    grads["mla_w_uq"] = _mm(p + "mla_uq_dw", cq, dqf, "tn")
    dckv = _mm(p + "mla_ukv_dx", dkvf, w["mla_w_ukv"], "nt")
    grads["mla_w_ukv"] = _mm(p + "mla_ukv_dw", ckv, dkvf, "tn")
    sm_a = [sm["mla_q_a_norm"], sm["mla_kv_a_norm"], sm["mla_k_rope_norm"]]
    (ddown,), dsm_a = _ew_bwd(p + "mla_a_bwd", _mla_a_fn, [down] + tabs, sm_a, [dcq, dckv, dkr], ct_dtypes=[BF16])
    grads["mla_q_a_norm"], grads["mla_kv_a_norm"], grads["mla_k_rope_norm"] = dsm_a
    dh = _mm(p + "mla_down_dx", ddown, w["mla_w_down"], "nt")
    grads["mla_w_down"] = _mm(p + "mla_down_dw", h, ddown, "tn")
    (dx,), (dgain,) = _ew_bwd(p + "mix_norm_bwd", _norm_fn, [x], [sm["mix_norm"]], [dh], add=dy)
    grads["mix_norm"] = dgain
    return dx


_MIX_FWD = (_dn_layer_fwd, _sb_layer_fwd, _mla_layer_fwd)
_MIX_BWD = (_dn_layer_bwd, _sb_layer_bwd, _mla_layer_bwd)


def _pad_cols(a, n):
    return jnp.pad(a, ((0, 0), (0, n - a.shape[1])))


def _prep_big(name, a):
    if name.endswith("dn_w_in"):
        return _pad_cols(a, 4 * N_HEADS * HEAD + LANES)
    if name.endswith("mla_w_down"):
        return _pad_cols(a, 512)
    if name.endswith("mla_w_uq"):
        a3 = a.reshape(a.shape[0], N_HEADS, MLA_QK)
        return jnp.pad(a3, ((0, 0), (0, 0), (0, 2 * HEAD - MLA_QK))).reshape(a.shape[0], N_HEADS * 2 * HEAD)
    return a


def _unprep_big(name, g):
    if name.endswith("dn_w_in"):
        return g[:, :4 * N_HEADS * HEAD + 2 * N_HEADS]
    if name.endswith("mla_w_down"):
        return g[:, :448]
    if name.endswith("mla_w_uq"):
        return g.reshape(g.shape[0], N_HEADS, 2 * HEAD)[:, :, :MLA_QK].reshape(g.shape[0], N_HEADS * MLA_QK)
    return g


def _prep_small(name, a):
    if name.endswith("dn_conv_w"):
        return jnp.pad(a.astype(F32), ((0, HALO - a.shape[0]), (0, 0)))
    if name.endswith(("dn_a_log", "dn_dt_bias", "mla_q_rope_norm", "mla_k_rope_norm")):
        return _row(a, LANES)
    return _row(a)


def _unprep_small(name, g, like):
    if name.endswith("dn_conv_w"):
        return g[:like.shape[0]]
    return g.reshape(-1)[:like.shape[0]]


EARLY_LAYER = 1


def local_step(x, target, big, small, exchange=None):
    layers, saved = [], []
    for i in range(DEPTH):
        p = "l%d_" % i
        names = _MIXERS[i % 3] + ("ffn_w_gate_up", "ffn_w_down", "mix_norm", "ffn_norm")
        sm = {n: _prep_small(n, small[p + n]) for n in names if p + n in small}
        mixer_w = {n: _prep_big(n, big[p + n]) for n in _MIXERS[i % 3] if p + n in big}
        if exchange is not None and i == EARLY_LAYER:
            x, s_mix = _MIX_FWD[i % 3](p, x, mixer_w, sm, exchange)
        else:
            x, s_mix = _MIX_FWD[i % 3](p, x, mixer_w, sm)
        w = dict(mixer_w, **{n: big[p + n] for n in ("ffn_w_gate_up", "ffn_w_down")})
        x, s_ffn = _ffn_fwd(p, x, w, sm)
        layers.append((p, w, sm))
        saved.append((s_mix, s_ffn))
    part, dx = _loss_kernel(x, target)
    grads = {}
    for i in reversed(range(DEPTH)):
        p, w, sm = layers[i]
        g = {}
        dx = _ffn_bwd(p, saved[i][1], dx, w, sm, g)
        if exchange is not None and i == EARLY_LAYER:
            exchange.grads = grads
            dx = _MIX_BWD[i % 3](p, saved[i][0], dx, w, sm, g, exchange)
        else:
            dx = _MIX_BWD[i % 3](p, saved[i][0], dx, w, sm, g)
        for n, val in g.items():
            grads[p + n] = _unprep_big(p + n, val) if p + n in big else _unprep_small(p + n, val, small[p + n])
    return part, dx, grads


ROW = 1024
BIG_ROWS = 1024


PACK_ALIGN = 16


def _packed_rows(shape):
    return -(-math.prod(shape) // (ROW * PACK_ALIGN)) * PACK_ALIGN


def _as_rows(a, lead=()):
    rows = _packed_rows(a.shape[len(lead):])
    flat = a.reshape(lead + (-1,))
    return jnp.pad(flat, ((0, 0),) * len(lead) + ((0, rows * ROW - flat.shape[-1]),)).reshape(lead + (rows, ROW))


def _pack(arrs, rows_multiple, lead=()):
    blocks = [_as_rows(a, lead) for a in arrs]
    used = sum(b.shape[-2] for b in blocks)
    fill = -(-used // rows_multiple) * rows_multiple - used
    if fill:
        blocks.append(jnp.zeros(lead + (fill, ROW), blocks[0].dtype))
    return jnp.concatenate(blocks, axis=len(lead))


def _unpack(buf, shapes, lead=()):
    out, r0 = [], 0
    for s in shapes:
        rows, n = _packed_rows(s), math.prod(s)
        block = lax.slice_in_dim(buf, r0, r0 + rows, axis=len(lead))
        out.append(block.reshape(lead + (-1,))[..., :n].reshape(lead + tuple(s)))
        r0 += rows
    return out


def _me():
    return lax.axis_index("x"), lax.axis_index("y"), lax.axis_index("c")


def _other_chips(x, y):
    return [(1 - x, y), (x, 1 - y), (1 - x, 1 - y)]


HBM = pl.BlockSpec(memory_space=pl.ANY)


OWN_STREAMS = 4


def _gather_phases(x_ref, out_ref, send_sems, recv_sems):
    rows = x_ref.shape[0]
    half = rows // 2
    x, y, c = _me()
    sibling, chips = (x, y, 1 - c), _other_chips(x, y)

    def part(px, py, pc):
        return out_ref.at[2 * px + py, pl.ds(pl.multiple_of(pc * half, 16), half), :]

    def copy(k, block, to, src=None):
        return pltpu.make_async_remote_copy(
            src_ref=part(*block) if src is None else src, dst_ref=part(*block),
            send_sem=send_sems.at[k], recv_sem=recv_sems.at[k], device_id=to, device_id_type=MESH)

    my_half = x_ref.at[pl.ds(pl.multiple_of(c * half, 16), half), :]
    first = [copy(j, (x, y, c), (*chip, c), src=my_half) for j, chip in enumerate(chips)]
    piece = rows // OWN_STREAMS
    for p in range(OWN_STREAMS):
        rows_p = pl.ds(p * piece, piece)
        first.append(pltpu.make_async_remote_copy(
            src_ref=x_ref.at[rows_p, :], dst_ref=out_ref.at[2 * x + y, rows_p, :], send_sem=send_sems.at[6 + p],
            recv_sem=recv_sems.at[6 + p], device_id=sibling, device_id_type=MESH))
    passed = [copy(3 + j, (*chip, c), sibling) for j, chip in enumerate(chips)]

    def start():
        for cp in first:
            cp.start()

    def relay():
        for j, chip in enumerate(chips):
            copy(j, (*chip, c), (x, y, c)).wait_recv()
            passed[j].start()

    def finish():
        for j, chip in enumerate(chips):
            copy(3 + j, (*chip, 1 - c), (x, y, c)).wait_recv()
        for cp in first[3:]:
            cp.wait_recv()
        for cp in first + passed:
            cp.wait_send()

    return start, relay, finish


GATHER_SEMS = 6 + OWN_STREAMS


def _gather_shards(packed):
    def body(x_ref, out_ref, send_sems, recv_sems):
        for phase in _gather_phases(x_ref, out_ref, send_sems, recv_sems):
            phase()

    return pl.pallas_call(
        body, out_shape=jax.ShapeDtypeStruct((N_CHIPS,) + packed.shape, packed.dtype), in_specs=[HBM], out_specs=HBM,
        scratch_shapes=[pltpu.SemaphoreType.DMA((GATHER_SEMS,)), pltpu.SemaphoreType.DMA((GATHER_SEMS,))],
        name="gather_weights",
    )(packed)


D2D_STREAMS = 16


def _swap_halves(g, tag):
    n, rows, _ = g.shape
    half = rows // 2
    per = D2D_STREAMS // n
    piece = half // per

    def body(g_ref, theirs_ref, send_sems, recv_sems):
        x, y, c = _me()
        give = (1 - c) * half
        copies = []
        for j in range(n):
            for p in range(per):
                k = j * per + p
                cp = pltpu.make_async_remote_copy(
                    src_ref=g_ref.at[j, pl.ds(pl.multiple_of(give + p * piece, 8), piece), :],
                    dst_ref=theirs_ref.at[j, pl.ds(p * piece, piece), :],
                    send_sem=send_sems.at[k], recv_sem=recv_sems.at[k], device_id=(x, y, 1 - c), device_id_type=MESH)
                cp.start()
                copies.append(cp)
        for cp in copies:
            cp.wait()

    return pl.pallas_call(
        body, out_shape=jax.ShapeDtypeStruct((n, half, ROW), g.dtype), in_specs=[HBM], out_specs=HBM,
        scratch_shapes=[pltpu.SemaphoreType.DMA((D2D_STREAMS,)), pltpu.SemaphoreType.DMA((D2D_STREAMS,))],
        name="grad_swap_halves" + tag,
    )(g)


def _pair_sum(g, theirs, c, tag):
    n, half, _ = theirs.shape
    tm = _tile(half)
    nb = half // tm

    def body(c_ref, g_ref, t_ref, s32_ref, s16_ref):
        s = g_ref[...] + t_ref[...]
        s32_ref[...] = s
        s16_ref[...] = s.astype(BF16)

    blk = pl.BlockSpec((None, tm, ROW), lambda j, i, c_ref: (j, i, 0))
    return pl.pallas_call(
        body,
        grid_spec=pltpu.PrefetchScalarGridSpec(
            num_scalar_prefetch=1, grid=(n, nb),
            in_specs=[pl.BlockSpec((None, tm, ROW), lambda j, i, c_ref: (j, c_ref[0] * nb + i, 0)), blk],
            out_specs=[blk, blk]),
        out_shape=[jax.ShapeDtypeStruct(theirs.shape, F32), jax.ShapeDtypeStruct(theirs.shape, BF16)],
        name="grad_pair_sum" + tag, compiler_params=_params(2),
    )(c.reshape(1).astype(jnp.int32), g, theirs)


def _scatter_chunks(s16):
    _, half, _ = s16.shape

    def body(s16_ref, got_ref, send_sems, recv_sems):
        x, y, c = _me()
        sends = []
        for j, (px, py) in enumerate(_other_chips(x, y)):
            cp = pltpu.make_async_remote_copy(src_ref=s16_ref.at[2 * px + py], dst_ref=got_ref.at[j],
                                              send_sem=send_sems.at[j], recv_sem=recv_sems.at[j],
                                              device_id=(px, py, c), device_id_type=MESH)
            cp.start()
            sends.append(cp)
        for cp in sends:
            cp.wait()

    return pl.pallas_call(
        body, out_shape=jax.ShapeDtypeStruct((3, half, ROW), BF16), in_specs=[HBM], out_specs=HBM,
        scratch_shapes=[pltpu.SemaphoreType.DMA((3,)), pltpu.SemaphoreType.DMA((3,))],
        name="grad_scatter",
    )(s16)


def _chip_sum(s32, got, chip, c, tag):
    _, half, _ = s32.shape
    tm = _tile(half)
    nb = half // tm

    def body(where_ref, own_ref, g0_ref, g1_ref, g2_ref, o_ref):
        o_ref[...] = ((own_ref[...] + g0_ref[...].astype(F32)) + g1_ref[...].astype(F32)) + g2_ref[...].astype(F32)

    got_spec = lambda k: pl.BlockSpec((None, tm, ROW), lambda i, where_ref, k=k: (k, i, 0))
    return pl.pallas_call(
        body,
        grid_spec=pltpu.PrefetchScalarGridSpec(
            num_scalar_prefetch=1, grid=(nb,),
            in_specs=[pl.BlockSpec((None, tm, ROW), lambda i, where_ref: (where_ref[0], i, 0)), got_spec(0), got_spec(1), got_spec(2)],
            out_specs=pl.BlockSpec((tm, ROW), lambda i, where_ref: (where_ref[1] * nb + i, 0))),
        out_shape=jax.ShapeDtypeStruct((2 * half, ROW), F32), name="grad_chip_sum" + tag, compiler_params=_params(1),
    )(jnp.stack([chip, c]).astype(jnp.int32), s32, got, got, got)


def _join_halves(f, tag):
    half = f.shape[0] // 2
    piece = half // D2D_STREAMS

    def body(f_ref, out_ref, send_sems, recv_sems):
        x, y, c = _me()
        copies = []
        for p in range(D2D_STREAMS):
            rows = out_ref.at[pl.ds(pl.multiple_of(c * half + p * piece, 8), piece), :]
            cp = pltpu.make_async_remote_copy(src_ref=rows, dst_ref=rows, send_sem=send_sems.at[p], recv_sem=recv_sems.at[p],
                                              device_id=(x, y, 1 - c), device_id_type=MESH)
            cp.start()
            copies.append(cp)
        for cp in copies:
            cp.wait()

    return pl.pallas_call(
        body, out_shape=jax.ShapeDtypeStruct(f.shape, F32), in_specs=[HBM], out_specs=HBM, input_output_aliases={0: 0},
        scratch_shapes=[pltpu.SemaphoreType.DMA((D2D_STREAMS,)), pltpu.SemaphoreType.DMA((D2D_STREAMS,))],
        name="grad_join_halves" + tag,
    )(f)


def _all_reduce_small(name, v):
    rows = v.shape[0]

    def body(v_ref, out_ref, slots, send_sems, recv_sems):
        x, y, c = _me()
        me = 4 * x + 2 * y + c
        slots[me] = v_ref[...]
        sends = []
        for r in range(1, 8):
            to = (x ^ (r >> 2), y ^ ((r >> 1) & 1), c ^ (r & 1))
            cp = pltpu.make_async_remote_copy(src_ref=v_ref, dst_ref=slots.at[me], send_sem=send_sems.at[r - 1],
                                              recv_sem=recv_sems.at[r - 1], device_id=to, device_id_type=MESH)
            cp.start()
            sends.append(cp)
        for cp in sends:
            cp.wait()
        total = slots[0]
        for d in range(1, 8):
            total = total + slots[d]
        out_ref[...] = total

    vmem = pl.BlockSpec(memory_space=pltpu.VMEM)
    return pl.pallas_call(
        body, out_shape=jax.ShapeDtypeStruct(v.shape, F32), in_specs=[vmem], out_specs=vmem,
        scratch_shapes=[pltpu.VMEM((8, rows, ROW), F32), pltpu.SemaphoreType.DMA((7,)), pltpu.SemaphoreType.DMA((7,))],
        name=name,
    )(v)


def _adam_fn(w, g, m, v):
    m2 = ADAM_B1 * m + (1.0 - ADAM_B1) * g
    v2 = ADAM_B2 * v + (1.0 - ADAM_B2) * (g * g)
    m_hat = m2 / (1.0 - ADAM_B1 ** ADAM_STEP)
    v_hat = v2 / (1.0 - ADAM_B2 ** ADAM_STEP)
    return -ADAM_LR * (m_hat / (jnp.sqrt(v_hat) + ADAM_EPS) + ADAM_WD * w), m2, v2


def _full_shape(name, shard_shape):
    ax = _shard_axis(name)
    return tuple(n * N_CHIPS if k == ax else n for k, n in enumerate(shard_shape))


def _chip_major(name, full):
    if _shard_axis(name) == 0:
        return full.reshape(N_CHIPS, -1, full.shape[1])
    n = full.shape[1] // N_CHIPS
    return jnp.stack([full[:, j * n:(j + 1) * n] for j in range(N_CHIPS)])


def _from_chip_major(name, shards):
    if _shard_axis(name) == 0:
        return shards.reshape(-1, shards.shape[2])
    return jnp.concatenate([shards[j] for j in range(N_CHIPS)], axis=1)


def _row_tile(rows, cap=512):
    return max(t for t in range(8, min(rows, cap) + 1, 8) if rows % t == 0)


def _step(a):
    x_i, y_i, c_i = _me()
    chip = 2 * x_i + y_i
    def packed_shards(names):
        return _pack([a[n].astype(BF16) for n in names], BIG_ROWS)

    def full_matrices(names, gathered):
        shards = _unpack(gathered, [a[n].shape for n in names], lead=(N_CHIPS,))
        return {n: _from_chip_major(n, sh) for n, sh in zip(names, shards)}

    first_w = [n for n in BIG if int(n[1]) <= EARLY_LAYER]
    later_w = [n for n in BIG if n not in first_w]
    big = full_matrices(first_w, _gather_shards(packed_shards(first_w)))

    small = {n: a[n] for n in SMALL}
    convs = [n for n in SMALL if n.endswith("dn_conv_w")]
    placed = []
    for n in convs:
        full = jnp.zeros(_full_shape_conv(a[n].shape), F32)
        placed.append(lax.dynamic_update_slice(full, a[n], (0, chip * a[n].shape[1])))
    conv_sum = _all_reduce_small("gather_conv", _pack(placed, 8))
    for n, full in zip(convs, _unpack(conv_sum, [p.shape for p in placed])):
        small[n] = full * 0.5

    def pair_sums(names, grads, tag):
        g_all = _pack([_chip_major(n, grads[n]) for n in names], BIG_ROWS, lead=(N_CHIPS,))
        return _pair_sum(g_all, _swap_halves(g_all, tag), c_i, tag)

    def reduced(names, s32, got, tag):
        whole = _join_halves(_chip_sum(s32, got, chip, c_i, tag), tag)
        return dict(zip(names, _unpack(whole, [a[n].shape for n in names])))

    early_prefix = "l%d_" % EARLY_LAYER
    early = [n for n in BIG if int(n[1]) > EARLY_LAYER or n in (early_prefix + "ffn_w_gate_up", early_prefix + "ffn_w_down",
                                                                early_prefix + "sb_w_out")]
    late = [n for n in BIG if n not in early]

    class Exchange:
        def late_shards(self):
            return packed_shards(later_w)

        def deliver(self, gathered):
            big.update(full_matrices(later_w, gathered))

        def start(self, layer_grads):
            have = dict(self.grads, **{early_prefix + k: val for k, val in layer_grads.items()})
            self.s32, s16 = pair_sums(early, have, "_early")
            return s16

    exchange = Exchange()
    part, dx, grads = local_step(a["x"][0], a["loss_target"][0], big, small, exchange)
    loss = lax.psum(0.5 * jnp.sum(part) / D_MODEL, ("x", "y", "c"))
    g_big = reduced(early, exchange.s32, exchange.arrived, "_early")
    s32, s16 = pair_sums(late, grads, "_late")
    g_big.update(reduced(late, s32, _scatter_chunks(s16), "_late"))

    g_small_full = _all_reduce_small("reduce_small", _pack([grads[n] for n in SMALL], 8))
    g_small = dict(zip(SMALL, _unpack(g_small_full, [grads[n].shape for n in SMALL])))
    for n in convs:
        g_small[n] = lax.dynamic_slice_in_dim(g_small[n], chip * a[n].shape[1], a[n].shape[1], axis=1)

    outs = {}
    for n in BIG:
        d, m2, v2 = _ew("adam_" + n, _adam_fn, [a[n], g_big[n], a["m_" + n], a["v_" + n]], [],
                        [(a[n].shape[1], F32)] * 3, tm=_row_tile(a[n].shape[0]))
        outs.update({"grad_" + n: g_big[n], "delta_" + n: d, "new_m_" + n: m2, "new_v_" + n: v2})
    pk = lambda prefix: _pack([a[prefix + n] for n in SMALL], 8)
    gs_packed = _pack([g_small[n] for n in SMALL], 8)
    small_bufs = (gs_packed,) + tuple(_ew("adam_small", _adam_fn, [pk(""), gs_packed, pk("m_"), pk("v_")], [], [(ROW, F32)] * 3,
                                          tm=gs_packed.shape[0]))
    small_shapes = [a[n].shape for n in SMALL]
    for key, buf in zip(("grad_", "delta_", "new_m_", "new_v_"), small_bufs):
        outs.update({key + n: val for n, val in zip(SMALL, _unpack(buf, small_shapes))})
    result = [loss, dx[None]]
    for key in ("grad_", "delta_", "new_m_", "new_v_"):
        result += [outs[key + n] for n in WEIGHTS]
    return tuple(result)


def _full_shape_conv(shard_shape):
    return (shard_shape[0], shard_shape[1] * N_CHIPS)


def kernel(x, l0_mix_norm, l0_dn_w_in, l0_dn_conv_w, l0_dn_a_log, l0_dn_dt_bias, l0_dn_out_norm, l0_dn_w_out, l0_ffn_norm, l0_ffn_w_gate_up, l0_ffn_w_down, l1_mix_norm, l1_sb_w_qkv, l1_sb_q_norm, l1_sb_k_norm, l1_sb_w_out, l1_ffn_norm, l1_ffn_w_gate_up, l1_ffn_w_down, l2_mix_norm, l2_mla_w_down, l2_mla_q_a_norm, l2_mla_kv_a_norm, l2_mla_w_uq, l2_mla_w_ukv, l2_mla_q_nope_norm, l2_mla_q_rope_norm, l2_mla_k_nope_norm, l2_mla_k_rope_norm, l2_mla_w_out, l2_ffn_norm, l2_ffn_w_gate_up, l2_ffn_w_down, l3_mix_norm, l3_dn_w_in, l3_dn_conv_w, l3_dn_a_log, l3_dn_dt_bias, l3_dn_out_norm, l3_dn_w_out, l3_ffn_norm, l3_ffn_w_gate_up, l3_ffn_w_down, loss_target, m_l0_mix_norm, m_l0_dn_w_in, m_l0_dn_conv_w, m_l0_dn_a_log, m_l0_dn_dt_bias, m_l0_dn_out_norm, m_l0_dn_w_out, m_l0_ffn_norm, m_l0_ffn_w_gate_up, m_l0_ffn_w_down, m_l1_mix_norm, m_l1_sb_w_qkv, m_l1_sb_q_norm, m_l1_sb_k_norm, m_l1_sb_w_out, m_l1_ffn_norm, m_l1_ffn_w_gate_up, m_l1_ffn_w_down, m_l2_mix_norm, m_l2_mla_w_down, m_l2_mla_q_a_norm, m_l2_mla_kv_a_norm, m_l2_mla_w_uq, m_l2_mla_w_ukv, m_l2_mla_q_nope_norm, m_l2_mla_q_rope_norm, m_l2_mla_k_nope_norm, m_l2_mla_k_rope_norm, m_l2_mla_w_out, m_l2_ffn_norm, m_l2_ffn_w_gate_up, m_l2_ffn_w_down, m_l3_mix_norm, m_l3_dn_w_in, m_l3_dn_conv_w, m_l3_dn_a_log, m_l3_dn_dt_bias, m_l3_dn_out_norm, m_l3_dn_w_out, m_l3_ffn_norm, m_l3_ffn_w_gate_up, m_l3_ffn_w_down, v_l0_mix_norm, v_l0_dn_w_in, v_l0_dn_conv_w, v_l0_dn_a_log, v_l0_dn_dt_bias, v_l0_dn_out_norm, v_l0_dn_w_out, v_l0_ffn_norm, v_l0_ffn_w_gate_up, v_l0_ffn_w_down, v_l1_mix_norm, v_l1_sb_w_qkv, v_l1_sb_q_norm, v_l1_sb_k_norm, v_l1_sb_w_out, v_l1_ffn_norm, v_l1_ffn_w_gate_up, v_l1_ffn_w_down, v_l2_mix_norm, v_l2_mla_w_down, v_l2_mla_q_a_norm, v_l2_mla_kv_a_norm, v_l2_mla_w_uq, v_l2_mla_w_ukv, v_l2_mla_q_nope_norm, v_l2_mla_q_rope_norm, v_l2_mla_k_nope_norm, v_l2_mla_k_rope_norm, v_l2_mla_w_out, v_l2_ffn_norm, v_l2_ffn_w_gate_up, v_l2_ffn_w_down, v_l3_mix_norm, v_l3_dn_w_in, v_l3_dn_conv_w, v_l3_dn_a_log, v_l3_dn_dt_bias, v_l3_dn_out_norm, v_l3_dn_w_out, v_l3_ffn_norm, v_l3_ffn_w_gate_up, v_l3_ffn_w_down):
    return _step(dict(locals()))
```

```python
import functools
import math

import jax
import jax.numpy as jnp
from jax import lax
from jax.experimental import pallas as pl
from jax.experimental.pallas import tpu as pltpu

F32, BF16 = jnp.float32, jnp.bfloat16
MESH = pl.DeviceIdType.MESH

D_MODEL = 1024
N_HEADS = 8
HEAD = 128
FFN_HIDDEN = 2816
DN_CHUNK = 64
NORM_EPS = 1e-6
MLA_ROPE = 64
MLA_QK = 192
ROPE_THETA = 10000.0
ADAM_LR, ADAM_B1, ADAM_B2, ADAM_EPS, ADAM_WD, ADAM_STEP = 0.001, 0.9, 0.999, 1e-08, 0.01, 10
N_CHIPS = 4
LANES = 128
VMEM_LIMIT = 56 * 2 ** 20


def _params(n_grid):
    return pltpu.CompilerParams(dimension_semantics=("arbitrary",) * n_grid, vmem_limit_bytes=VMEM_LIMIT)


_MIXERS = (
    ("dn_w_in", "dn_conv_w", "dn_a_log", "dn_dt_bias", "dn_out_norm", "dn_w_out"),
    ("sb_w_qkv", "sb_q_norm", "sb_k_norm", "sb_w_out"),
    ("mla_w_down", "mla_q_a_norm", "mla_kv_a_norm", "mla_w_uq", "mla_w_ukv", "mla_q_nope_norm",
     "mla_q_rope_norm", "mla_k_nope_norm", "mla_k_rope_norm", "mla_w_out"),
)
DEPTH = 4


def _layer_names(i):
    p = "l%d_" % i
    return [p + "mix_norm"] + [p + n for n in _MIXERS[i % 3]] + [p + "ffn_norm", p + "ffn_w_gate_up", p + "ffn_w_down"]


WEIGHTS = [n for i in range(DEPTH) for n in _layer_names(i)]
_ROW_SHARDED = ("w_out", "ffn_w_down", "mla_w_down")
_COL_SHARDED = ("dn_w_in", "sb_w_qkv", "mla_w_uq", "mla_w_ukv", "ffn_w_gate_up")


def _shard_axis(name):
    if name.endswith(_ROW_SHARDED):
        return 0
    if name.endswith(_COL_SHARDED):
        return 1
    return None


BIG = [n for n in WEIGHTS if _shard_axis(n) is not None]
SMALL = [n for n in WEIGHTS if _shard_axis(n) is None]


_DN = {"nn": (((1,), (0,)), ((), ())), "nt": (((1,), (1,)), ((), ())), "tn": (((0,), (0,)), ((), ()))}
_DN_BATCHED = {"nn": (((2,), (1,)), ((0,), (0,))), "nt": (((2,), (2,)), ((0,), (0,))), "tn": (((1,), (1,)), ((0,), (0,)))}


def _dims(a, kind):
    return _DN_BATCHED[kind] if a.ndim == 3 else _DN[kind]


def _dg(a, b, kind):
    return lax.dot_general(a.astype(BF16), b.astype(BF16), _dims(a, kind), preferred_element_type=F32)


@functools.partial(jax.custom_vjp, nondiff_argnums=(2,))
def bdot(a, b, kind):
    return _dg(a, b, kind)


def _bdot_fwd(a, b, kind):
    return _dg(a, b, kind), (a, b)


def _bdot_bwd(kind, res, ct):
    a, b = res
    if kind == "nn":
        return _dg(ct, b, "nt"), _dg(a, ct, "tn")
    if kind == "nt":
        return _dg(ct, b, "nn"), _dg(ct, a, "tn")
    return _dg(b, ct, "nt"), _dg(a, ct, "nn")


bdot.defvjp(_bdot_fwd, _bdot_bwd)


def _split(a, terms):
    out = []
    for _ in range(terms):
        hi = a.astype(BF16)
        out.append(hi)
        a = a - hi.astype(F32)
    return out


def _xdot(a, b, kind, exact, terms=3):
    if exact == 0:
        return sum(lax.dot_general(a, p, _dims(a, kind), preferred_element_type=F32) for p in _split(b, terms))
    return sum(lax.dot_general(p, b, _dims(a, kind), preferred_element_type=F32) for p in _split(a, terms))


def _tri(n, rel):
    r = lax.broadcasted_iota(jnp.int32, (n, n), 0)
    c = lax.broadcasted_iota(jnp.int32, (n, n), 1)
    return {"le": c <= r, "lt": c < r, "ge": c >= r, "gt": c > r}[rel]


def _running(g):
    n = g.shape[-2]
    return jnp.broadcast_to(_tri(n, "le").astype(BF16), g.shape[:-2] + (n, n))


@jax.custom_vjp
def cumsum_rows(g):
    return _xdot(_running(g), g, "nn", 0)


def _cumsum_fwd(g):
    return cumsum_rows(g), None


def _cumsum_bwd(_, ct):
    return (_xdot(_running(ct), ct, "tn", 0),)


cumsum_rows.defvjp(_cumsum_fwd, _cumsum_bwd)


def _dot3(a, b, kind):
    (ah, al), (bh, bl) = _split(a, 2), _split(b, 2)
    dot = lambda p, q: lax.dot_general(p, q, _dims(a, kind), preferred_element_type=F32)
    return dot(ah, bh) + (dot(ah, bl) + dot(al, bh))


@functools.partial(jax.custom_vjp, nondiff_argnums=(2,))
def _hdot3(a, b, kind):
    return _dot3(a, b, kind)


def _hdot3_fwd(a, b, kind):
    return _dot3(a, b, kind), (a, b)


def _hdot3_bwd(kind, res, ct):
    a, b = res
    if kind == "nn":
        return _dot3(ct, b, "nt"), _dot3(a, ct, "tn")
    if kind == "nt":
        return _dot3(ct, b, "nn"), _dot3(ct, a, "tn")
    return _dot3(b, ct, "nt"), _dot3(a, ct, "nn")


_hdot3.defvjp(_hdot3_fwd, _hdot3_bwd)


def _hdot(a, b):
    return _hdot3(a, b, "nn")


def _unit_lower_inverse(lower):
    n = lower.shape[-1]
    eye = (lax.broadcasted_iota(jnp.int32, (n, n), 0) == lax.broadcasted_iota(jnp.int32, (n, n), 1)).astype(F32)
    m = -lower
    p = eye + m
    for _ in range(int(math.log2(n)) - 1):
        m = _hdot(m, m)
        p = p + _hdot(p, m)
    return p


def _rms(x, g, n=None):
    n = x.shape[-1] if n is None else n
    return x * lax.rsqrt(jnp.sum(x * x, axis=-1, keepdims=True) * (1.0 / n) + NORM_EPS) * g


def _l2(x):
    return x * lax.rsqrt(jnp.sum(x * x, axis=-1, keepdims=True) + NORM_EPS)


def _silu(x):
    return x * jax.nn.sigmoid(x)


def _logsig(z):
    return jnp.minimum(z, 0.0) - jnp.log1p(jnp.exp(-jnp.abs(z)))


@jax.custom_vjp
def _rope(x, cos, sin_lo, sin_hi):
    return x * cos + pltpu.roll(x, 96, 1) * sin_lo + pltpu.roll(x, 32, 1) * sin_hi


def _rope_fwd(x, cos, sin_lo, sin_hi):
    return _rope(x, cos, sin_lo, sin_hi), (cos, sin_lo, sin_hi)


def _rope_bwd(res, ct):
    cos, sin_lo, sin_hi = res
    dx = ct * cos + pltpu.roll(ct * sin_lo, 32, 1) + pltpu.roll(ct * sin_hi, 96, 1)
    return dx, jnp.zeros_like(cos), jnp.zeros_like(sin_lo), jnp.zeros_like(sin_hi)


_rope.defvjp(_rope_fwd, _rope_bwd)


def _tile(n, prefs=(512, 384, 256, 128)):
    for t in prefs:
        if n % t == 0:
            return t
    return n


MM_OUT_TILES = (1024, 1408, 512, 384, 256, 128)
MM_K_TILES = (1024, 512, 384, 256, 128)


def _mm(name, a, b, kind, out_dtype=F32, add=None):
    if kind == "tn":
        (kdim, m), n = a.shape, b.shape[1]
    else:
        (m, kdim), n = a.shape, (b.shape[0] if kind == "nt" else b.shape[1])
    tm, tn, tk = _tile(m, MM_OUT_TILES), _tile(n, MM_OUT_TILES), _tile(kdim, MM_K_TILES)
    nk = kdim // tk
    a_spec = pl.BlockSpec((tk, tm), lambda i, j, k: (k, i)) if kind == "tn" else pl.BlockSpec((tm, tk), lambda i, j, k: (i, k))
    b_spec = pl.BlockSpec((tn, tk), lambda i, j, k: (j, k)) if kind == "nt" else pl.BlockSpec((tk, tn), lambda i, j, k: (k, j))
    o_spec = pl.BlockSpec((tm, tn), lambda i, j, k: (i, j))
    has_add = add is not None

    def body(*refs):
        a_ref, b_ref = refs[0], refs[1]
        o_ref, acc = refs[-2], refs[-1]
        k = pl.program_id(2)

        @pl.when(k == 0)
        def _():
            acc[...] = jnp.zeros_like(acc)

        acc[...] += _dg(a_ref[...], b_ref[...], kind)

        @pl.when(k == nk - 1)
        def _():
            r = acc[...]
            if has_add:
                r = r + refs[2][...]
            o_ref[...] = r.astype(o_ref.dtype)

    return pl.pallas_call(
        body, grid=(m // tm, n // tn, nk),
        in_specs=[a_spec, b_spec] + ([o_spec] if has_add else []),
        out_specs=o_spec, out_shape=jax.ShapeDtypeStruct((m, n), out_dtype),
        scratch_shapes=[pltpu.VMEM((tm, tn), F32)], name=name, compiler_params=_params(3),
    )(*([a, b] + ([add] if has_add else [])))


class _V:
    def __init__(self, arr, w=None, base=0, diff=True):
        self.arr, self.base, self.diff = arr, base, diff
        self.w = arr.shape[1] if w is None else w

    def spec(self, tm):
        return pl.BlockSpec((tm, self.w), lambda i, b=self.base: (i, b))


def _as_views(ins):
    return [v if isinstance(v, _V) else _V(v) for v in ins]


def _tup(r):
    return tuple(r) if isinstance(r, (tuple, list)) else (r,)


def _ew(name, fn, ins, smalls, outs, tm=256):
    ins = _as_views(ins)
    t = ins[0].arr.shape[0]
    tm = min(tm, t)
    n_in = len(ins) + len(smalls)

    def body(*refs):
        res = _tup(fn(*[r[...] for r in refs[:n_in]]))
        for r, o in zip(refs[n_in:], res):
            r[...] = o.astype(r.dtype)

    return pl.pallas_call(
        body, grid=(t // tm,),
        in_specs=[v.spec(tm) for v in ins] + [pl.BlockSpec(s.shape, lambda i: (0, 0)) for s in smalls],
        out_specs=[pl.BlockSpec((tm, w), lambda i: (i, 0)) for w, _ in outs],
        out_shape=[jax.ShapeDtypeStruct((t, w), dt) for w, dt in outs],
        name=name, compiler_params=_params(1),
    )(*[v.arr for v in ins], *smalls)


def _ew_bwd(name, fn, ins, smalls, cts, tm=256, add=None, ct_dtypes=None):
    ins = _as_views(ins)
    t = ins[0].arr.shape[0]
    tm = min(tm, t)
    n_in, n_sm = len(ins), len(smalls)
    diff = [k for k, v in enumerate(ins) if v.diff]
    ct_dtypes = [F32] * len(diff) if ct_dtypes is None else ct_dtypes
    ct_arrs = [c for c in cts if c is not None]
    has_add = add is not None

    def body(*refs):
        vals = [r[...] for r in refs[:n_in]]
        svals = [r[...] for r in refs[n_in:n_in + n_sm]]
        p = n_in + n_sm
        ct_refs = list(refs[p:p + len(ct_arrs)])
        p += len(ct_arrs)
        add_ref = refs[p] if has_add else None
        p += int(has_add)
        din_refs = refs[p:p + len(diff)]
        dsm_refs = refs[p + len(diff):]

        def f(dv, sv):
            full = list(vals)
            for k, d in zip(diff, dv):
                full[k] = d
            return _tup(fn(*full, *sv))

        res, vjp = jax.vjp(f, [vals[k] for k in diff], svals)
        ctv = tuple(jnp.zeros_like(o) if c is None else ct_refs.pop(0)[...].astype(o.dtype) for c, o in zip(cts, res))
        dv, dsv = vjp(ctv)
        for n, (r, d) in enumerate(zip(din_refs, dv)):
            if n == 0 and has_add:
                d = d + add_ref[...]
            r[...] = d.astype(r.dtype)

        @pl.when(pl.program_id(0) == 0)
        def _():
            for r in dsm_refs:
                r[...] = jnp.zeros_like(r)

        for r, d in zip(dsm_refs, dsv):
            r[...] += d

    row = lambda w: pl.BlockSpec((tm, w), lambda i: (i, 0))
    small_specs = [pl.BlockSpec(s.shape, lambda i: (0, 0)) for s in smalls]
    out = pl.pallas_call(
        body, grid=(t // tm,),
        in_specs=[v.spec(tm) for v in ins] + small_specs + [row(c.shape[1]) for c in ct_arrs]
        + ([row(add.shape[1])] if has_add else []),
        out_specs=[row(ins[k].w) for k in diff] + small_specs,
        out_shape=[jax.ShapeDtypeStruct((t, ins[k].w), dt) for k, dt in zip(diff, ct_dtypes)]
        + [jax.ShapeDtypeStruct(s.shape, F32) for s in smalls],
        name=name, compiler_params=_params(1),
    )(*[v.arr for v in ins], *smalls, *ct_arrs, *([add] if has_add else []))
    return out[:len(diff)], out[len(diff):]


BQ = 256
HPB = 2
SUM_TERMS = 2


def _cat(parts):
    return parts[0] if len(parts) == 1 else jnp.concatenate(parts, axis=1)


def _head_view(ref, hh):
    return ref.at[:, hh * HEAD:(hh + 1) * HEAD]


def _attn_specs(qs, ks, t):
    q_specs = [pl.BlockSpec((BQ, HPB * HEAD), lambda h, i: (i, h)) for _ in qs]
    per_head = pl.BlockSpec((t, HPB * HEAD), lambda h, i: (0, h))
    k_specs = [pl.BlockSpec((t, HEAD), lambda h, i: (0, 0)) if sh else per_head for _, sh in ks]
    return q_specs, k_specs, per_head


def _causal_sweep(i, pair, init, diagonal_first, log2_blocks):
    order = (lambda s: i - 1 - s) if diagonal_first else (lambda s: s)
    carry = pair(i, init, True) if diagonal_first else init
    done = 0
    for level in range(log2_blocks, -1, -1):
        per = 1 << level

        def group(s, c, per=per, done=done):
            for u in range(per):
                c = pair(order(done + per * s + u), c, False)
            return c

        left = i - done
        carry = lax.fori_loop(0, lax.shift_right_logical(left, level), group, carry)
        done = done + (left - (left & (per - 1)))
    return carry if diagonal_first else pair(i, carry, True)


def _attn_fwd(name, mode, qs, ks, v, gather=None):
    t = qs[0].shape[0]
    nq, n = t // BQ, len(qs)
    q_specs, k_specs, per_head = _attn_specs(qs, ks, t)
    shared = [sh for _, sh in ks]

    def body(*refs):
        q_refs, k_refs, v_ref = refs[:n], refs[n:2 * n], refs[2 * n]
        n_in = 2 * n + 1 + int(gather is not None)
        o_ref, st_ref = refs[n_in], refs[n_in + 1]
        g, i = pl.program_id(0), pl.program_id(1)
        last_g = N_HEADS // HPB - 1
        if gather is not None:
            phases = _gather_phases(refs[n_in - 1], refs[n_in + 2], refs[-2], refs[-1])
            pl.when((g == 0) & (i == 0))(phases[0])
            pl.when((g == last_g) & (i == 0))(phases[1])
        row = lax.broadcasted_iota(jnp.int32, (BQ, BQ), 0)
        col = lax.broadcasted_iota(jnp.int32, (BQ, BQ), 1)
        after = _tri(BQ, "lt").astype(BF16)

        def head(hh):
            q = _cat([_head_view(r, hh)[...] for r in q_refs])
            k_h = [kr if sh else _head_view(kr, hh) for kr, sh in zip(k_refs, shared)]
            v_h = _head_view(v_ref, hh)

            def pair(j, carry, masked):
                off = pl.multiple_of(j * BQ, BQ)
                z = _dg(q, _cat([kr[pl.ds(off, BQ), :] for kr in k_h]), "nt")
                vj = v_h[pl.ds(off, BQ), :]
                if mode == "sb":
                    acc, run = carry
                    lsz = _logsig(z)
                    stay = lsz - z
                    if masked:
                        stay = jnp.where(col < row, stay, 0.0)
                    a = jnp.exp(lsz + (run + _xdot(stay, after, "nn", 1, SUM_TERMS)))
                    if masked:
                        a = jnp.where(col < row, a, 0.0)
                    return acc + _dg(a, vj, "nn"), run + jnp.sum(stay, axis=1, keepdims=True)
                m, l, acc = carry
                if masked:
                    z = jnp.where(col <= row, z, -1e30)
                m2 = jnp.maximum(m, jnp.max(z, axis=1, keepdims=True))
                p = jnp.exp(z - m2)
                alpha = jnp.exp(m - m2)
                return m2, alpha * l + jnp.sum(p, axis=1, keepdims=True), alpha * acc + _dg(p, vj, "nn")

            def finish(carry):
                if mode == "sb":
                    acc, run = carry
                    _head_view(o_ref, hh)[...] = acc
                    _head_view(st_ref, hh)[...] = jnp.broadcast_to(run, (BQ, HEAD))
                else:
                    m, l, acc = carry
                    _head_view(o_ref, hh)[...] = acc / l
                    _head_view(st_ref, hh)[...] = jnp.broadcast_to(m + jnp.log(l), (BQ, HEAD))

            zero = jnp.zeros((BQ, 1), F32)
            acc0 = jnp.zeros((BQ, HEAD), F32)
            init = (acc0, zero) if mode == "sb" else (jnp.full((BQ, 1), -1e30, F32), zero, acc0)
            return pair, init, finish

        heads = [head(hh) for hh in range(HPB)]
        both = lambda j, carry, masked: tuple(h[0](j, c, masked) for h, c in zip(heads, carry))
        final = _causal_sweep(i, both, tuple(h[1] for h in heads), diagonal_first=(mode == "sb"), log2_blocks=2)
        for h, c in zip(heads, final):
            h[2](c)
        if gather is not None:
            pl.when((g == last_g) & (i == nq - 1))(phases[2])

    blk = pl.BlockSpec((BQ, HPB * HEAD), lambda h, i: (i, h))
    wide = jax.ShapeDtypeStruct((t, N_HEADS * HEAD), F32)
    comm_in, comm_out, comm_shape, comm_scratch = [], [], [], []
    if gather is not None:
        comm_in, comm_out = [gather], [HBM]
        comm_shape = [jax.ShapeDtypeStruct((N_CHIPS,) + gather.shape, gather.dtype)]
        comm_scratch = [pltpu.SemaphoreType.DMA((GATHER_SEMS,)), pltpu.SemaphoreType.DMA((GATHER_SEMS,))]
    return pl.pallas_call(
        body, grid=(N_HEADS // HPB, nq), in_specs=q_specs + k_specs + [per_head] + [HBM] * len(comm_in),
        out_specs=[blk, blk] + comm_out, out_shape=[wide, wide] + comm_shape, scratch_shapes=comm_scratch,
        name=name, compiler_params=_params(2),
    )(*qs, *[k for k, _ in ks], v, *comm_in)


def _attn_bwd(name, mode, qs, ks, v, o, stat, do, send=None):
    t = qs[0].shape[0]
    nq, n = t // BQ, len(qs)
    q_specs, k_specs, per_head = _attn_specs(qs, ks, t)
    shared = [sh for _, sh in ks]

    def body(*refs):
        q_refs, k_refs, v_ref = refs[:n], refs[n:2 * n], refs[2 * n]
        o_ref, st_ref, do_ref = refs[2 * n + 1:2 * n + 4]
        n_in = 2 * n + 4 + int(send is not None)
        dq_refs = refs[n_in:n_in + n]
        dk_refs = refs[n_in + n:n_in + 2 * n]
        dv_ref = refs[n_in + 2 * n]
        g, i = pl.program_id(0), pl.program_id(1)
        if send is not None:
            send_ref, got_ref, send_sems, recv_sems = refs[n_in - 1], refs[n_in + 2 * n + 1], refs[-2], refs[-1]

            def exchange():
                x, y, c = _me()
                return [pltpu.make_async_remote_copy(src_ref=send_ref.at[2 * px + py], dst_ref=got_ref.at[j],
                                                     send_sem=send_sems.at[j], recv_sem=recv_sems.at[j],
                                                     device_id=(px, py, c), device_id_type=MESH)
                        for j, (px, py) in enumerate(_other_chips(x, y))]

            @pl.when((g == 0) & (i == 0))
            def _():
                for cp in exchange():
                    cp.start()

        @pl.when(i == 0)
        def _():
            dv_ref[...] = jnp.zeros_like(dv_ref)
            for r, sh in zip(dk_refs, shared):
                if not sh:
                    r[...] = jnp.zeros_like(r)

        for r, sh in zip(dk_refs, shared):
            if sh:
                @pl.when((i == 0) & (g == 0))
                def _(r=r):
                    r[...] = jnp.zeros_like(r)

        row = lax.broadcasted_iota(jnp.int32, (BQ, BQ), 0)
        col = lax.broadcasted_iota(jnp.int32, (BQ, BQ), 1)
        upto = _tri(BQ, "ge").astype(BF16)
        before = _tri(BQ, "gt").astype(BF16)

        def head(hh):
            q = _cat([_head_view(r, hh)[...] for r in q_refs])
            k_h = [kr if sh else _head_view(kr, hh) for kr, sh in zip(k_refs, shared)]
            dk_h = [r if sh else _head_view(r, hh) for r, sh in zip(dk_refs, shared)]
            v_h, dv_h = _head_view(v_ref, hh), _head_view(dv_ref, hh)
            do_t = _head_view(do_ref, hh)[...]
            st = _head_view(st_ref, hh)[:, :1]
            if mode == "softmax":
                dsum = jnp.sum(do_t * _head_view(o_ref, hh)[...], axis=1, keepdims=True)

            def pair(j, carry, masked):
                off = pl.multiple_of(j * BQ, BQ)
                kj = _cat([kr[pl.ds(off, BQ), :] for kr in k_h])
                z = _dg(q, kj, "nt")
                da = _dg(do_t, v_h[pl.ds(off, BQ), :], "nt")
                if mode == "sb":
                    dq, pre, gpre = carry
                    lsz = _logsig(z)
                    stay = lsz - z
                    if masked:
                        stay = jnp.where(col < row, stay, 0.0)
                    a = jnp.exp(lsz + (st - (pre + _xdot(stay, upto, "nn", 1, SUM_TERMS))))
                    if masked:
                        a = jnp.where(col < row, a, 0.0)
                    gr = a * da
                    sig = jnp.exp(lsz)
                    dz = gr * (1.0 - sig) - sig * (gpre + _xdot(gr, before, "nn", 1, SUM_TERMS))
                    if masked:
                        dz = jnp.where(col < row, dz, 0.0)
                    tail = (pre + jnp.sum(stay, axis=1, keepdims=True), gpre + jnp.sum(gr, axis=1, keepdims=True))
                else:
                    dq = carry[0]
                    a = jnp.exp(z - st)
                    if masked:
                        a = jnp.where(col <= row, a, 0.0)
                    dz = a * (da - dsum)
                    tail = ()
                dk = _dg(dz, q, "tn")
                for p, r in enumerate(dk_h):
                    r[pl.ds(off, BQ), :] += dk[:, p * HEAD:(p + 1) * HEAD]
                dv_h[pl.ds(off, BQ), :] += _dg(a, do_t, "tn")
                return (dq + _dg(dz, kj, "nn"),) + tail

            def finish(carry):
                for p, r in enumerate(dq_refs):
                    _head_view(r, hh)[...] = carry[0][:, p * HEAD:(p + 1) * HEAD]

            zero = jnp.zeros((BQ, 1), F32)
            init = (jnp.zeros((BQ, n * HEAD), F32),) + ((zero, zero) if mode == "sb" else ())
            return pair, init, finish

        heads = [head(hh) for hh in range(HPB)]
        both = lambda j, carry, masked: tuple(h[0](j, c, masked) for h, c in zip(heads, carry))
        final = _causal_sweep(i, both, tuple(h[1] for h in heads), diagonal_first=False, log2_blocks=2)
        for h, c in zip(heads, final):
            h[2](c)

        if send is not None:
            @pl.when((g == N_HEADS // HPB - 1) & (i == nq - 1))
            def _():
                for cp in exchange():
                    cp.wait()

    blk = pl.BlockSpec((BQ, HPB * HEAD), lambda h, i: (i, h))
    dk_specs = [pl.BlockSpec((t, HEAD), lambda h, i: (0, 0)) if sh else per_head for sh in shared]
    wide = jax.ShapeDtypeStruct((t, N_HEADS * HEAD), F32)
    comm_in, comm_out, comm_shape, comm_scratch = [], [], [], []
    if send is not None:
        comm_in, comm_out = [send], [HBM]
        comm_shape = [jax.ShapeDtypeStruct((3,) + send.shape[1:], send.dtype)]
        comm_scratch = [pltpu.SemaphoreType.DMA((3,)), pltpu.SemaphoreType.DMA((3,))]
    out = pl.pallas_call(
        body, grid=(N_HEADS // HPB, nq), in_specs=q_specs + k_specs + [per_head, blk, blk, blk] + [HBM] * len(comm_in),
        out_specs=[blk] * n + dk_specs + [per_head] + comm_out,
        out_shape=[wide] * n + [jax.ShapeDtypeStruct((t, HEAD), F32) if sh else wide for sh in shared] + [wide] + comm_shape,
        scratch_shapes=comm_scratch, name=name, compiler_params=_params(2),
    )(*qs, *[k for k, _ in ks], v, o, stat, do, *comm_in)
    return (out[:n], out[n:2 * n], out[2 * n]) + tuple(out[2 * n + 1:])


@jax.custom_vjp
def _given_inverse(lower, tinv):
    return tinv


def _given_inverse_fwd(lower, tinv):
    return tinv, tinv


def _given_inverse_bwd(tinv, ct):
    return -_dot3(_dot3(tinv, ct, "tn"), tinv, "nt"), jnp.zeros_like(tinv)


_given_inverse.defvjp(_given_inverse_fwd, _given_inverse_bwd)


def _dn_chunk(q, k, v, g, beta, state, tinv=None):
    c = q.shape[-2]
    gc = cumsum_rows(g)
    gcc = gc[..., :c]
    diff = gcc - jnp.swapaxes(gcc, -1, -2)
    causal, strict = _tri(c, "le"), _tri(c, "lt")
    decay = jnp.where(causal, jnp.exp(jnp.where(causal, diff, 0.0)), 0.0)
    kb = k * beta
    lower = jnp.where(strict, bdot(kb, k, "nt") * decay, 0.0)
    tinv = _unit_lower_inverse(lower) if tinv is None else _given_inverse(lower, tinv)
    eg = jnp.exp(gc)
    u = _hdot(tinv, v * beta)
    w = _hdot(tinv, kb * eg)
    attn = bdot(q, k, "nt") * decay
    glast = gc[..., c - 1:c, :]
    v_new = u - bdot(w, state, "nn")
    o = bdot(q * eg, state, "nn") + bdot(attn, v_new, "nn")
    new_state = state * jnp.exp(glast) + bdot(k * jnp.exp(glast - gc), v_new, "tn")
    return o, new_state, tinv


def _stack_heads(ref):
    return jnp.stack([ref[:, h * HEAD:(h + 1) * HEAD] for h in range(N_HEADS)])


def _store_heads(ref, val):
    for h in range(N_HEADS):
        ref[:, h * HEAD:(h + 1) * HEAD] = val[h]


def _dn_fwd(name, q, k, v, g, beta):
    t = q.shape[0]
    nc = t // DN_CHUNK
    wide = N_HEADS * HEAD
    blk = pl.BlockSpec((DN_CHUNK, wide), lambda n: (n, 0))
    st_spec = pl.BlockSpec((N_HEADS, None, HEAD, HEAD), lambda n: (0, n, 0, 0))
    inv_spec = pl.BlockSpec((N_HEADS, None, DN_CHUNK, DN_CHUNK), lambda n: (0, n, 0, 0))

    def body(q_ref, k_ref, v_ref, g_ref, b_ref, o_ref, st_ref, inv_ref, state):
        @pl.when(pl.program_id(0) == 0)
        def _():
            state[...] = jnp.zeros_like(state)

        s_in = state[...]
        st_ref[...] = s_in
        o, s_out, tinv = _dn_chunk(*[_stack_heads(r) for r in (q_ref, k_ref, v_ref, g_ref, b_ref)], s_in)
        _store_heads(o_ref, o)
        inv_ref[...] = tinv
        state[...] = s_out

    return pl.pallas_call(
        body, grid=(nc,), in_specs=[blk] * 5, out_specs=[blk, st_spec, inv_spec],
        out_shape=[jax.ShapeDtypeStruct((t, wide), F32), jax.ShapeDtypeStruct((N_HEADS, nc, HEAD, HEAD), F32),
                   jax.ShapeDtypeStruct((N_HEADS, nc, DN_CHUNK, DN_CHUNK), F32)],
        scratch_shapes=[pltpu.VMEM((N_HEADS, HEAD, HEAD), F32)], name=name, compiler_params=_params(1),
    )(q, k, v, g, beta)


def _dn_bwd(name, q, k, v, g, beta, states, inverses, do):
    t = q.shape[0]
    nc = t // DN_CHUNK
    wide = N_HEADS * HEAD
    blk = pl.BlockSpec((DN_CHUNK, wide), lambda n: (nc - 1 - n, 0))
    st_spec = pl.BlockSpec((N_HEADS, None, HEAD, HEAD), lambda n: (0, nc - 1 - n, 0, 0))
    inv_spec = pl.BlockSpec((N_HEADS, None, DN_CHUNK, DN_CHUNK), lambda n: (0, nc - 1 - n, 0, 0))

    def body(q_ref, k_ref, v_ref, g_ref, b_ref, st_ref, inv_ref, do_ref, dq_ref, dk_ref, dv_ref, dg_ref, db_ref, dstate):
        @pl.when(pl.program_id(0) == 0)
        def _():
            dstate[...] = jnp.zeros_like(dstate)

        tinv = inv_ref[...]
        chunk = lambda *args: _dn_chunk(*args, tinv=tinv)[:2]
        _, vjp = jax.vjp(chunk, *[_stack_heads(r) for r in (q_ref, k_ref, v_ref, g_ref, b_ref)], st_ref[...])
        cts = vjp((_stack_heads(do_ref), dstate[...]))
        for r, d in zip((dq_ref, dk_ref, dv_ref, dg_ref, db_ref), cts[:5]):
            _store_heads(r, d)
        dstate[...] = cts[5]

    shape = jax.ShapeDtypeStruct((t, wide), F32)
    return pl.pallas_call(
        body, grid=(nc,), in_specs=[blk] * 5 + [st_spec, inv_spec, blk], out_specs=[blk] * 5, out_shape=[shape] * 5,
        scratch_shapes=[pltpu.VMEM((N_HEADS, HEAD, HEAD), F32)], name=name, compiler_params=_params(1),
    )(q, k, v, g, beta, states, inverses, do)


CONV_W = 1024
HALO = 8


def _shift_down(cur, prev, s):
    sh = pltpu.roll(cur, s, 0)
    ph = pltpu.roll(prev, s, 0)
    r = lax.broadcasted_iota(jnp.int32, (HALO, cur.shape[1]), 0)
    return jnp.concatenate([jnp.where(r < s, ph, sh[:HALO]), sh[HALO:]], axis=0)


def _shift_up(cur, nxt, s):
    tm = cur.shape[0]
    sh = pltpu.roll(cur, tm - s, 0)
    nh = pltpu.roll(nxt, HALO - s, 0)
    r = lax.broadcasted_iota(jnp.int32, (HALO, cur.shape[1]), 0)
    return jnp.concatenate([sh[:tm - HALO], jnp.where(r >= HALO - s, nh, sh[tm - HALO:])], axis=0)


def _conv_fwd(name, proj, w, tm=256):
    t = proj.shape[0]
    tm = min(tm, t)
    width = w.shape[1]
    per = tm // HALO

    def body(cur_ref, prev_ref, w_ref, y_ref):
        cur = cur_ref[...]
        prev = jnp.where(pl.program_id(0) > 0, prev_ref[...], 0.0)
        y = cur * w_ref[3:4, :]
        for s in (1, 2, 3):
            y = y + _shift_down(cur, prev, s) * w_ref[3 - s:4 - s, :]
        y_ref[...] = y

    return pl.pallas_call(
        body, grid=(t // tm, width // CONV_W),
        in_specs=[pl.BlockSpec((tm, CONV_W), lambda i, c: (i, c)),
                  pl.BlockSpec((HALO, CONV_W), lambda i, c: (jnp.maximum(i * per - 1, 0), c)),
                  pl.BlockSpec((HALO, CONV_W), lambda i, c: (0, c))],
        out_specs=pl.BlockSpec((tm, CONV_W), lambda i, c: (i, c)),
        out_shape=jax.ShapeDtypeStruct((t, width), F32), name=name, compiler_params=_params(2),
    )(proj, proj, w)


def _conv_bwd(name, proj, w, dy, tm=256):
    t = proj.shape[0]
    tm = min(tm, t)
    width = w.shape[1]
    per, nt = tm // HALO, t // tm

    def body(cur_ref, prev_ref, w_ref, dy_ref, nxt_ref, du_ref, dw_ref):
        i = pl.program_id(1)
        cur, dy_t = cur_ref[...], dy_ref[...]
        prev = jnp.where(i > 0, prev_ref[...], 0.0)
        nxt = jnp.where(i < nt - 1, nxt_ref[...], 0.0)
        du = dy_t * w_ref[3:4, :]
        rows = [jnp.sum(dy_t * cur, axis=0, keepdims=True)]
        for s in (1, 2, 3):
            du = du + _shift_up(dy_t, nxt, s) * w_ref[3 - s:4 - s, :]
            rows.insert(0, jnp.sum(dy_t * _shift_down(cur, prev, s), axis=0, keepdims=True))
        du_ref[...] = du.astype(du_ref.dtype)

        @pl.when(i == 0)
        def _():
            dw_ref[...] = jnp.zeros_like(dw_ref)

        dw_ref[...] += jnp.concatenate(rows + [jnp.zeros((HALO - 4, CONV_W), F32)], axis=0)

    return pl.pallas_call(
        body, grid=(width // CONV_W, nt),
        in_specs=[pl.BlockSpec((tm, CONV_W), lambda c, i: (i, c)),
                  pl.BlockSpec((HALO, CONV_W), lambda c, i: (jnp.maximum(i * per - 1, 0), c)),
                  pl.BlockSpec((HALO, CONV_W), lambda c, i: (0, c)),
                  pl.BlockSpec((tm, CONV_W), lambda c, i: (i, c)),
                  pl.BlockSpec((HALO, CONV_W), lambda c, i: (jnp.minimum((i + 1) * per, t // HALO - 1), c))],
        out_specs=[pl.BlockSpec((tm, CONV_W), lambda c, i: (i, c)), pl.BlockSpec((HALO, CONV_W), lambda c, i: (0, c))],
        out_shape=[jax.ShapeDtypeStruct((t, width), BF16), jax.ShapeDtypeStruct((HALO, width), F32)],
        name=name, compiler_params=_params(2),
    )(proj, proj, w, dy, dy)


def _norm_fn(x, g):
    return _rms(x, g)


def _swiglu_fn(gu):
    return _silu(gu[:, :FFN_HIDDEN]) * gu[:, FFN_HIDDEN:]


def _heads(x):
    return [x[:, h * HEAD:(h + 1) * HEAD] for h in range(x.shape[1] // HEAD)]


def _dn_pre_fn(c, ab, a_log, dt_bias):
    w = N_HEADS * HEAD
    q = [_l2(_silu(x)) * (HEAD ** -0.5) for x in _heads(c[:, :w])]
    k = [_l2(_silu(x)) for x in _heads(c[:, w:2 * w])]
    v = _silu(c[:, 2 * w:])
    g, beta = [], []
    for h in range(N_HEADS):
        gh = -jnp.exp(a_log[:, h:h + 1]) * jax.nn.softplus(ab[:, h:h + 1] + dt_bias[:, h:h + 1])
        bh = jax.nn.sigmoid(ab[:, N_HEADS + h:N_HEADS + h + 1])
        g.append(jnp.broadcast_to(gh, (c.shape[0], HEAD)))
        beta.append(jnp.broadcast_to(bh, (c.shape[0], HEAD)))
    cat = lambda xs: jnp.concatenate(xs, axis=1)
    return cat(q), cat(k), v, cat(g), cat(beta)


def _dn_post_fn(o, z, out_norm):
    return jnp.concatenate([_rms(oh, out_norm) * _silu(zh) for oh, zh in zip(_heads(o), _heads(z))], axis=1)


def _sb_pre_fn(qkv, q_norm, k_norm):
    w = N_HEADS * HEAD
    q = [_rms(x, q_norm) * (HEAD ** -0.5) for x in _heads(qkv[:, :w])]
    k = [_rms(x, k_norm) for x in _heads(qkv[:, w:2 * w])]
    return jnp.concatenate(q, axis=1), jnp.concatenate(k, axis=1), qkv[:, 2 * w:]


def _mla_a_fn(down, cos, sin_lo, sin_hi, q_a_norm, kv_a_norm, k_rope_norm):
    cq = _rms(down[:, :256], q_a_norm)
    ckv = _rms(down[:, 256:384], kv_a_norm)
    kr = _rope(_rms(down[:, 384:], k_rope_norm, MLA_ROPE), cos, sin_lo, sin_hi)
    return cq, ckv, kr


def _mla_b_fn(qf, kvf, cos, sin_lo, sin_hi, q_nope_norm, q_rope_norm, k_nope_norm):
    scale = MLA_QK ** -0.5
    qn, qr, kn, v = [], [], [], []
    for h in range(N_HEADS):
        a = 2 * h * HEAD
        qn.append(_rms(qf[:, a:a + HEAD], q_nope_norm) * scale)
        qr.append(_rope(_rms(qf[:, a + HEAD:a + 2 * HEAD], q_rope_norm, MLA_ROPE), cos, sin_lo, sin_hi) * scale)
        kn.append(_rms(kvf[:, a:a + HEAD], k_nope_norm))
        v.append(kvf[:, a + HEAD:a + 2 * HEAD])
    cat = lambda xs: jnp.concatenate(xs, axis=1)
    return cat(qn), cat(qr), cat(kn), cat(v)


def _rope_tables(t):
    inv_freq = ROPE_THETA ** (-jnp.arange(0, MLA_ROPE, 2, dtype=F32) / MLA_ROPE)
    ang = jnp.arange(t, dtype=F32)[:, None] * inv_freq[None, :]
    cos, sin, zero = jnp.cos(ang), jnp.sin(ang), jnp.zeros((t, MLA_ROPE // 2), F32)
    cat = lambda xs: jnp.concatenate(xs, axis=1)
    return cat([cos, cos, zero, zero]), cat([-sin, zero, zero, zero]), cat([zero, sin, zero, zero])


def _row(v, width=None):
    width = v.shape[0] if width is None else width
    return jnp.pad(v.astype(F32), (0, width - v.shape[0])).reshape(1, width)


def _loss_kernel(y, target):
    t, d = y.shape
    tm = min(256, t)

    def body(y_ref, t_ref, part_ref, dy_ref):
        e = y_ref[...] - t_ref[...]
        dy_ref[...] = e * (1.0 / d)

        @pl.when(pl.program_id(0) == 0)
        def _():
            part_ref[...] = jnp.zeros_like(part_ref)

        part_ref[...] += jnp.sum(e * e, axis=0, keepdims=True)

    blk = pl.BlockSpec((tm, d), lambda i: (i, 0))
    one = pl.BlockSpec((1, d), lambda i: (0, 0))
    return pl.pallas_call(body, grid=(t // tm,), in_specs=[blk, blk], out_specs=[one, blk],
                          out_shape=[jax.ShapeDtypeStruct((1, d), F32), jax.ShapeDtypeStruct((t, d), F32)],
                          name="loss", compiler_params=_params(1))(y, target)


def _ffn_fwd(p, x, w, sm):
    h, = _ew(p + "ffn_norm", _norm_fn, [x], [sm["ffn_norm"]], [(D_MODEL, BF16)])
    gu = _mm(p + "ffn_gu", h, w["ffn_w_gate_up"], "nn")
    act, = _ew(p + "ffn_act", _swiglu_fn, [gu], [], [(FFN_HIDDEN, BF16)], tm=128)
    y = _mm(p + "ffn_down", act, w["ffn_w_down"], "nn", add=x)
    return y, (x, h, gu, act)


def _ffn_bwd(p, saved, dy, w, sm, grads):
    x, h, gu, act = saved
    dact = _mm(p + "ffn_down_dx", dy, w["ffn_w_down"], "nt")
    grads["ffn_w_down"] = _mm(p + "ffn_down_dw", act, dy, "tn")
    (dgu,), _ = _ew_bwd(p + "ffn_act_bwd", _swiglu_fn, [gu], [], [dact], tm=128, ct_dtypes=[BF16])
    dh = _mm(p + "ffn_gu_dx", dgu, w["ffn_w_gate_up"], "nt")
    grads["ffn_w_gate_up"] = _mm(p + "ffn_gu_dw", h, dgu, "tn")
    (dx,), (dg,) = _ew_bwd(p + "ffn_norm_bwd", _norm_fn, [x], [sm["ffn_norm"]], [dh], add=dy)
    grads["ffn_norm"] = dg
    return dx


def _dn_layer_fwd(p, x, w, sm):
    h, = _ew(p + "mix_norm", _norm_fn, [x], [sm["mix_norm"]], [(D_MODEL, BF16)])
    proj = _mm(p + "dn_in", h, w["dn_w_in"], "nn")
    conv = _conv_fwd(p + "dn_conv", proj, sm["dn_conv_w"])
    ab = _V(proj, LANES, 4 * N_HEADS)
    wide = N_HEADS * HEAD
    q, k, v, g, beta = _ew(p + "dn_pre", _dn_pre_fn, [conv, ab], [sm["dn_a_log"], sm["dn_dt_bias"]], [(wide, F32)] * 5)
    o, states, inverses = _dn_fwd(p + "dn_core", q, k, v, g, beta)
    z = _V(proj, wide, 3)
    on, = _ew(p + "dn_post", _dn_post_fn, [o, z], [sm["dn_out_norm"]], [(wide, BF16)])
    y = _mm(p + "dn_out", on, w["dn_w_out"], "nn", add=x)
    return y, (x, h, proj, conv, q, k, v, g, beta, o, states, inverses, on)


def _dn_layer_bwd(p, saved, dy, w, sm, grads):
    x, h, proj, conv, q, k, v, g, beta, o, states, inverses, on = saved
    wide = N_HEADS * HEAD
    don = _mm(p + "dn_out_dx", dy, w["dn_w_out"], "nt")
    grads["dn_w_out"] = _mm(p + "dn_out_dw", on, dy, "tn")
    (do, dz), (d_out_norm,) = _ew_bwd(p + "dn_post_bwd", _dn_post_fn, [o, _V(proj, wide, 3)], [sm["dn_out_norm"]], [don],
                                          ct_dtypes=[F32, BF16])
    grads["dn_out_norm"] = d_out_norm
    dq, dk, dv, dg, db = _dn_bwd(p + "dn_core_bwd", q, k, v, g, beta, states, inverses, do)
    (dconv, dab), (d_a_log, d_dt) = _ew_bwd(p + "dn_pre_bwd", _dn_pre_fn, [conv, _V(proj, LANES, 4 * N_HEADS)],
                                            [sm["dn_a_log"], sm["dn_dt_bias"]], [dq, dk, dv, dg, db], ct_dtypes=[F32, BF16])
    grads["dn_a_log"], grads["dn_dt_bias"] = d_a_log, d_dt
    dqkv, dconv_w = _conv_bwd(p + "dn_conv_bwd", proj, sm["dn_conv_w"], dconv)
    grads["dn_conv_w"] = dconv_w
    dproj = jnp.concatenate([dqkv, dz, dab], axis=1)
    dh = _mm(p + "dn_in_dx", dproj, w["dn_w_in"], "nt")
    grads["dn_w_in"] = _mm(p + "dn_in_dw", h, dproj, "tn")
    (dx,), (dgain,) = _ew_bwd(p + "mix_norm_bwd", _norm_fn, [x], [sm["mix_norm"]], [dh], add=dy)
    grads["mix_norm"] = dgain
    return dx


def _sb_layer_fwd(p, x, w, sm, exchange=None):
    h, = _ew(p + "mix_norm", _norm_fn, [x], [sm["mix_norm"]], [(D_MODEL, BF16)])
    qkv = _mm(p + "sb_qkv", h, w["sb_w_qkv"], "nn")
    wide = N_HEADS * HEAD
    q, k, v = _ew(p + "sb_pre", _sb_pre_fn, [qkv], [sm["sb_q_norm"], sm["sb_k_norm"]], [(wide, BF16)] * 3)
    gather = None if exchange is None else exchange.late_shards()
    o, stat, *gathered = _attn_fwd(p + "sb_core", "sb", [q], [(k, False)], v, gather=gather)
    if exchange is not None:
        exchange.deliver(gathered[0])
    y = _mm(p + "sb_out", o, w["sb_w_out"], "nn", add=x)
    return y, (x, h, qkv, q, k, v, o, stat)


def _sb_layer_bwd(p, saved, dy, w, sm, grads, exchange=None):
    x, h, qkv, q, k, v, o, stat = saved
    do = _mm(p + "sb_out_dx", dy, w["sb_w_out"], "nt")
    grads["sb_w_out"] = _mm(p + "sb_out_dw", o, dy, "tn")
    send = None if exchange is None else exchange.start(grads)
    (dq,), (dk,), dv, *arrived = _attn_bwd(p + "sb_core_bwd", "sb", [q], [(k, False)], v, o, stat, do, send=send)
    if exchange is not None:
        exchange.arrived = arrived[0]
    (dqkv,), (dqn, dkn) = _ew_bwd(p + "sb_pre_bwd", _sb_pre_fn, [qkv], [sm["sb_q_norm"], sm["sb_k_norm"]], [dq, dk, dv],
                                  ct_dtypes=[BF16])
    grads["sb_q_norm"], grads["sb_k_norm"] = dqn, dkn
    dh = _mm(p + "sb_qkv_dx", dqkv, w["sb_w_qkv"], "nt")
    grads["sb_w_qkv"] = _mm(p + "sb_qkv_dw", h, dqkv, "tn")
    (dx,), (dgain,) = _ew_bwd(p + "mix_norm_bwd", _norm_fn, [x], [sm["mix_norm"]], [dh], add=dy)
    grads["mix_norm"] = dgain
    return dx


def _mla_layer_fwd(p, x, w, sm):
    t = x.shape[0]
    tabs = [_V(a, diff=False) for a in _rope_tables(t)]
    h, = _ew(p + "mix_norm", _norm_fn, [x], [sm["mix_norm"]], [(D_MODEL, BF16)])
    down = _mm(p + "mla_down", h, w["mla_w_down"], "nn")
    sm_a = [sm["mla_q_a_norm"], sm["mla_kv_a_norm"], sm["mla_k_rope_norm"]]
    cq, ckv, kr = _ew(p + "mla_a", _mla_a_fn, [down] + tabs, sm_a, [(256, BF16), (128, BF16), (128, BF16)])
    qf = _mm(p + "mla_uq", cq, w["mla_w_uq"], "nn")
    kvf = _mm(p + "mla_ukv", ckv, w["mla_w_ukv"], "nn")
    sm_b = [sm["mla_q_nope_norm"], sm["mla_q_rope_norm"], sm["mla_k_nope_norm"]]
    wide = N_HEADS * HEAD
    qn, qr, kn, v = _ew(p + "mla_b", _mla_b_fn, [qf, kvf] + tabs, sm_b, [(wide, BF16)] * 4)
    o, stat = _attn_fwd(p + "mla_core", "softmax", [qn, qr], [(kn, False), (kr, True)], v)
    y = _mm(p + "mla_out", o, w["mla_w_out"], "nn", add=x)
    return y, (x, h, down, cq, ckv, kr, qf, kvf, qn, qr, kn, v, o, stat)


def _mla_layer_bwd(p, saved, dy, w, sm, grads):
    x, h, down, cq, ckv, kr, qf, kvf, qn, qr, kn, v, o, stat = saved
    tabs = [_V(a, diff=False) for a in _rope_tables(x.shape[0])]
    do = _mm(p + "mla_out_dx", dy, w["mla_w_out"], "nt")
    grads["mla_w_out"] = _mm(p + "mla_out_dw", o, dy, "tn")
    (dqn, dqr), (dkn, dkr), dv = _attn_bwd(p + "mla_core_bwd", "softmax", [qn, qr], [(kn, False), (kr, True)],
                                           v, o, stat, do)
    sm_b = [sm["mla_q_nope_norm"], sm["mla_q_rope_norm"], sm["mla_k_nope_norm"]]
    (dqf, dkvf), dsm_b = _ew_bwd(p + "mla_b_bwd", _mla_b_fn, [qf, kvf] + tabs, sm_b, [dqn, dqr, dkn, dv],
                                 ct_dtypes=[BF16, BF16])
    grads["mla_q_nope_norm"], grads["mla_q_rope_norm"], grads["mla_k_nope_norm"] = dsm_b
    dcq = _mm(p + "mla_uq_dx", dqf, w["mla_w_uq"], "nt")
    grads["mla_w_uq"] = _mm(p + "mla_uq_dw", cq, dqf, "tn")
    dckv = _mm(p + "mla_ukv_dx", dkvf, w["mla_w_ukv"], "nt")
    grads["mla_w_ukv"] = _mm(p + "mla_ukv_dw", ckv, dkvf, "tn")
    sm_a = [sm["mla_q_a_norm"], sm["mla_kv_a_norm"], sm["mla_k_rope_norm"]]
    (ddown,), dsm_a = _ew_bwd(p + "mla_a_bwd", _mla_a_fn, [down] + tabs, sm_a, [dcq, dckv, dkr], ct_dtypes=[BF16])
    grads["mla_q_a_norm"], grads["mla_kv_a_norm"], grads["mla_k_rope_norm"] = dsm_a
    dh = _mm(p + "mla_down_dx", ddown, w["mla_w_down"], "nt")
    grads["mla_w_down"] = _mm(p + "mla_down_dw", h, ddown, "tn")
    (dx,), (dgain,) = _ew_bwd(p + "mix_norm_bwd", _norm_fn, [x], [sm["mix_norm"]], [dh], add=dy)
    grads["mix_norm"] = dgain
    return dx


_MIX_FWD = (_dn_layer_fwd, _sb_layer_fwd, _mla_layer_fwd)
_MIX_BWD = (_dn_layer_bwd, _sb_layer_bwd, _mla_layer_bwd)


def _pad_cols(a, n):
    return jnp.pad(a, ((0, 0), (0, n - a.shape[1])))


def _prep_big(name, a):
    if name.endswith("dn_w_in"):
        return _pad_cols(a, 4 * N_HEADS * HEAD + LANES)
    if name.endswith("mla_w_down"):
        return _pad_cols(a, 512)
    if name.endswith("mla_w_uq"):
        a3 = a.reshape(a.shape[0], N_HEADS, MLA_QK)
        return jnp.pad(a3, ((0, 0), (0, 0), (0, 2 * HEAD - MLA_QK))).reshape(a.shape[0], N_HEADS * 2 * HEAD)
    return a


def _unprep_big(name, g):
    if name.endswith("dn_w_in"):
        return g[:, :4 * N_HEADS * HEAD + 2 * N_HEADS]
    if name.endswith("mla_w_down"):
        return g[:, :448]
    if name.endswith("mla_w_uq"):
        return g.reshape(g.shape[0], N_HEADS, 2 * HEAD)[:, :, :MLA_QK].reshape(g.shape[0], N_HEADS * MLA_QK)
    return g


def _prep_small(name, a):
    if name.endswith("dn_conv_w"):
        return jnp.pad(a.astype(F32), ((0, HALO - a.shape[0]), (0, 0)))
    if name.endswith(("dn_a_log", "dn_dt_bias", "mla_q_rope_norm", "mla_k_rope_norm")):
        return _row(a, LANES)
    return _row(a)


def _unprep_small(name, g, like):
    if name.endswith("dn_conv_w"):
        return g[:like.shape[0]]
    return g.reshape(-1)[:like.shape[0]]


EARLY_LAYER = 1


def local_step(x, target, big, small, exchange=None):
    layers, saved = [], []
    for i in range(DEPTH):
        p = "l%d_" % i
        names = _MIXERS[i % 3] + ("ffn_w_gate_up", "ffn_w_down", "mix_norm", "ffn_norm")
        sm = {n: _prep_small(n, small[p + n]) for n in names if p + n in small}
        mixer_w = {n: _prep_big(n, big[p + n]) for n in _MIXERS[i % 3] if p + n in big}
        if exchange is not None and i == EARLY_LAYER:
            x, s_mix = _MIX_FWD[i % 3](p, x, mixer_w, sm, exchange)
        else:
            x, s_mix = _MIX_FWD[i % 3](p, x, mixer_w, sm)
        w = dict(mixer_w, **{n: big[p + n] for n in ("ffn_w_gate_up", "ffn_w_down")})
        x, s_ffn = _ffn_fwd(p, x, w, sm)
        layers.append((p, w, sm))
        saved.append((s_mix, s_ffn))
    part, dx = _loss_kernel(x, target)
    grads = {}
    for i in reversed(range(DEPTH)):
        p, w, sm = layers[i]
        g = {}
        dx = _ffn_bwd(p, saved[i][1], dx, w, sm, g)
        if exchange is not None and i == EARLY_LAYER:
            exchange.grads = grads
            dx = _MIX_BWD[i % 3](p, saved[i][0], dx, w, sm, g, exchange)
        else:
            dx = _MIX_BWD[i % 3](p, saved[i][0], dx, w, sm, g)
        for n, val in g.items():
            grads[p + n] = _unprep_big(p + n, val) if p + n in big else _unprep_small(p + n, val, small[p + n])
    return part, dx, grads


ROW = 1024
BIG_ROWS = 1024


PACK_ALIGN = 16


def _packed_rows(shape):
    return -(-math.prod(shape) // (ROW * PACK_ALIGN)) * PACK_ALIGN


def _as_rows(a, lead=()):
    rows = _packed_rows(a.shape[len(lead):])
    flat = a.reshape(lead + (-1,))
    return jnp.pad(flat, ((0, 0),) * len(lead) + ((0, rows * ROW - flat.shape[-1]),)).reshape(lead + (rows, ROW))


def _pack(arrs, rows_multiple, lead=()):
    blocks = [_as_rows(a, lead) for a in arrs]
    used = sum(b.shape[-2] for b in blocks)
    fill = -(-used // rows_multiple) * rows_multiple - used
    if fill:
        blocks.append(jnp.zeros(lead + (fill, ROW), blocks[0].dtype))
    return jnp.concatenate(blocks, axis=len(lead))


def _unpack(buf, shapes, lead=()):
    out, r0 = [], 0
    for s in shapes:
        rows, n = _packed_rows(s), math.prod(s)
        block = lax.slice_in_dim(buf, r0, r0 + rows, axis=len(lead))
        out.append(block.reshape(lead + (-1,))[..., :n].reshape(lead + tuple(s)))
        r0 += rows
    return out


def _me():
    return lax.axis_index("x"), lax.axis_index("y"), lax.axis_index("c")


def _other_chips(x, y):
    return [(1 - x, y), (x, 1 - y), (1 - x, 1 - y)]


HBM = pl.BlockSpec(memory_space=pl.ANY)


OWN_STREAMS = 4


def _gather_phases(x_ref, out_ref, send_sems, recv_sems):
    rows = x_ref.shape[0]
    half = rows // 2
    x, y, c = _me()
    sibling, chips = (x, y, 1 - c), _other_chips(x, y)

    def part(px, py, pc):
        return out_ref.at[2 * px + py, pl.ds(pl.multiple_of(pc * half, 16), half), :]

    def copy(k, block, to, src=None):
        return pltpu.make_async_remote_copy(
            src_ref=part(*block) if src is None else src, dst_ref=part(*block),
            send_sem=send_sems.at[k], recv_sem=recv_sems.at[k], device_id=to, device_id_type=MESH)

    my_half = x_ref.at[pl.ds(pl.multiple_of(c * half, 16), half), :]
    first = [copy(j, (x, y, c), (*chip, c), src=my_half) for j, chip in enumerate(chips)]
    piece = rows // OWN_STREAMS
    for p in range(OWN_STREAMS):
        rows_p = pl.ds(p * piece, piece)
        first.append(pltpu.make_async_remote_copy(
            src_ref=x_ref.at[rows_p, :], dst_ref=out_ref.at[2 * x + y, rows_p, :], send_sem=send_sems.at[6 + p],
            recv_sem=recv_sems.at[6 + p], device_id=sibling, device_id_type=MESH))
    passed = [copy(3 + j, (*chip, c), sibling) for j, chip in enumerate(chips)]

    def start():
        for cp in first:
            cp.start()

    def relay():
        for j, chip in enumerate(chips):
            copy(j, (*chip, c), (x, y, c)).wait_recv()
            passed[j].start()

    def finish():
        for j, chip in enumerate(chips):
            copy(3 + j, (*chip, 1 - c), (x, y, c)).wait_recv()
        for cp in first[3:]:
            cp.wait_recv()
        for cp in first + passed:
            cp.wait_send()

    return start, relay, finish


GATHER_SEMS = 6 + OWN_STREAMS


def _gather_shards(packed):
    def body(x_ref, out_ref, send_sems, recv_sems):
        for phase in _gather_phases(x_ref, out_ref, send_sems, recv_sems):
            phase()

    return pl.pallas_call(
        body, out_shape=jax.ShapeDtypeStruct((N_CHIPS,) + packed.shape, packed.dtype), in_specs=[HBM], out_specs=HBM,
        scratch_shapes=[pltpu.SemaphoreType.DMA((GATHER_SEMS,)), pltpu.SemaphoreType.DMA((GATHER_SEMS,))],
        name="gather_weights",
    )(packed)


D2D_STREAMS = 16


def _swap_halves(g, tag):
    n, rows, _ = g.shape
    half = rows // 2
    per = D2D_STREAMS // n
    piece = half // per

    def body(g_ref, theirs_ref, send_sems, recv_sems):
        x, y, c = _me()
        give = (1 - c) * half
        copies = []
        for j in range(n):
            for p in range(per):
                k = j * per + p
                cp = pltpu.make_async_remote_copy(
                    src_ref=g_ref.at[j, pl.ds(pl.multiple_of(give + p * piece, 8), piece), :],
                    dst_ref=theirs_ref.at[j, pl.ds(p * piece, piece), :],
                    send_sem=send_sems.at[k], recv_sem=recv_sems.at[k], device_id=(x, y, 1 - c), device_id_type=MESH)
                cp.start()
                copies.append(cp)
        for cp in copies:
            cp.wait()

    return pl.pallas_call(
        body, out_shape=jax.ShapeDtypeStruct((n, half, ROW), g.dtype), in_specs=[HBM], out_specs=HBM,
        scratch_shapes=[pltpu.SemaphoreType.DMA((D2D_STREAMS,)), pltpu.SemaphoreType.DMA((D2D_STREAMS,))],
        name="grad_swap_halves" + tag,
    )(g)


def _pair_sum(g, theirs, c, tag):
    n, half, _ = theirs.shape
    tm = _tile(half)
    nb = half // tm

    def body(c_ref, g_ref, t_ref, s32_ref, s16_ref):
        s = g_ref[...] + t_ref[...]
        s32_ref[...] = s
        s16_ref[...] = s.astype(BF16)

    blk = pl.BlockSpec((None, tm, ROW), lambda j, i, c_ref: (j, i, 0))
    return pl.pallas_call(
        body,
        grid_spec=pltpu.PrefetchScalarGridSpec(
            num_scalar_prefetch=1, grid=(n, nb),
            in_specs=[pl.BlockSpec((None, tm, ROW), lambda j, i, c_ref: (j, c_ref[0] * nb + i, 0)), blk],
            out_specs=[blk, blk]),
        out_shape=[jax.ShapeDtypeStruct(theirs.shape, F32), jax.ShapeDtypeStruct(theirs.shape, BF16)],
        name="grad_pair_sum" + tag, compiler_params=_params(2),
    )(c.reshape(1).astype(jnp.int32), g, theirs)


def _scatter_chunks(s16):
    _, half, _ = s16.shape

    def body(s16_ref, got_ref, send_sems, recv_sems):
        x, y, c = _me()
        sends = []
        for j, (px, py) in enumerate(_other_chips(x, y)):
            cp = pltpu.make_async_remote_copy(src_ref=s16_ref.at[2 * px + py], dst_ref=got_ref.at[j],
                                              send_sem=send_sems.at[j], recv_sem=recv_sems.at[j],
                                              device_id=(px, py, c), device_id_type=MESH)
            cp.start()
            sends.append(cp)
        for cp in sends:
            cp.wait()

    return pl.pallas_call(
        body, out_shape=jax.ShapeDtypeStruct((3, half, ROW), BF16), in_specs=[HBM], out_specs=HBM,
        scratch_shapes=[pltpu.SemaphoreType.DMA((3,)), pltpu.SemaphoreType.DMA((3,))],
        name="grad_scatter",
    )(s16)


def _chip_sum(s32, got, chip, c, tag):
    _, half, _ = s32.shape
    tm = _tile(half)
    nb = half // tm

    def body(where_ref, own_ref, g0_ref, g1_ref, g2_ref, o_ref):
        o_ref[...] = ((own_ref[...] + g0_ref[...].astype(F32)) + g1_ref[...].astype(F32)) + g2_ref[...].astype(F32)

    got_spec = lambda k: pl.BlockSpec((None, tm, ROW), lambda i, where_ref, k=k: (k, i, 0))
    return pl.pallas_call(
        body,
        grid_spec=pltpu.PrefetchScalarGridSpec(
            num_scalar_prefetch=1, grid=(nb,),
            in_specs=[pl.BlockSpec((None, tm, ROW), lambda i, where_ref: (where_ref[0], i, 0)), got_spec(0), got_spec(1), got_spec(2)],
            out_specs=pl.BlockSpec((tm, ROW), lambda i, where_ref: (where_ref[1] * nb + i, 0))),
        out_shape=jax.ShapeDtypeStruct((2 * half, ROW), F32), name="grad_chip_sum" + tag, compiler_params=_params(1),
    )(jnp.stack([chip, c]).astype(jnp.int32), s32, got, got, got)


def _join_halves(f, tag):
    half = f.shape[0] // 2
    piece = half // D2D_STREAMS

    def body(f_ref, out_ref, send_sems, recv_sems):
        x, y, c = _me()
        copies = []
        for p in range(D2D_STREAMS):
            rows = out_ref.at[pl.ds(pl.multiple_of(c * half + p * piece, 8), piece), :]
            cp = pltpu.make_async_remote_copy(src_ref=rows, dst_ref=rows, send_sem=send_sems.at[p], recv_sem=recv_sems.at[p],
                                              device_id=(x, y, 1 - c), device_id_type=MESH)
            cp.start()
            copies.append(cp)
        for cp in copies:
            cp.wait()

    return pl.pallas_call(
        body, out_shape=jax.ShapeDtypeStruct(f.shape, F32), in_specs=[HBM], out_specs=HBM, input_output_aliases={0: 0},
        scratch_shapes=[pltpu.SemaphoreType.DMA((D2D_STREAMS,)), pltpu.SemaphoreType.DMA((D2D_STREAMS,))],
        name="grad_join_halves" + tag,
    )(f)


def _all_reduce_small(name, v):
    rows = v.shape[0]

    def body(v_ref, out_ref, slots, send_sems, recv_sems):
        x, y, c = _me()
        me = 4 * x + 2 * y + c
        slots[me] = v_ref[...]
        sends = []
        for r in range(1, 8):
            to = (x ^ (r >> 2), y ^ ((r >> 1) & 1), c ^ (r & 1))
            cp = pltpu.make_async_remote_copy(src_ref=v_ref, dst_ref=slots.at[me], send_sem=send_sems.at[r - 1],
                                              recv_sem=recv_sems.at[r - 1], device_id=to, device_id_type=MESH)
            cp.start()
            sends.append(cp)
        for cp in sends:
            cp.wait()
        total = slots[0]
        for d in range(1, 8):
            total = total + slots[d]
        out_ref[...] = total

    vmem = pl.BlockSpec(memory_space=pltpu.VMEM)
    return pl.pallas_call(
        body, out_shape=jax.ShapeDtypeStruct(v.shape, F32), in_specs=[vmem], out_specs=vmem,
        scratch_shapes=[pltpu.VMEM((8, rows, ROW), F32), pltpu.SemaphoreType.DMA((7,)), pltpu.SemaphoreType.DMA((7,))],
        name=name,
    )(v)


def _adam_fn(w, g, m, v):
    m2 = ADAM_B1 * m + (1.0 - ADAM_B1) * g
    v2 = ADAM_B2 * v + (1.0 - ADAM_B2) * (g * g)
    m_hat = m2 / (1.0 - ADAM_B1 ** ADAM_STEP)
    v_hat = v2 / (1.0 - ADAM_B2 ** ADAM_STEP)
    return -ADAM_LR * (m_hat / (jnp.sqrt(v_hat) + ADAM_EPS) + ADAM_WD * w), m2, v2


def _full_shape(name, shard_shape):
    ax = _shard_axis(name)
    return tuple(n * N_CHIPS if k == ax else n for k, n in enumerate(shard_shape))


def _chip_major(name, full):
    if _shard_axis(name) == 0:
        return full.reshape(N_CHIPS, -1, full.shape[1])
    n = full.shape[1] // N_CHIPS
    return jnp.stack([full[:, j * n:(j + 1) * n] for j in range(N_CHIPS)])


def _from_chip_major(name, shards):
    if _shard_axis(name) == 0:
        return shards.reshape(-1, shards.shape[2])
    return jnp.concatenate([shards[j] for j in range(N_CHIPS)], axis=1)


def _row_tile(rows, cap=512):
    return max(t for t in range(8, min(rows, cap) + 1, 8) if rows % t == 0)


def _step(a):
    x_i, y_i, c_i = _me()
    chip = 2 * x_i + y_i
    def packed_shards(names):
        return _pack([a[n].astype(BF16) for n in names], BIG_ROWS)

    def full_matrices(names, gathered):
        shards = _unpack(gathered, [a[n].shape for n in names], lead=(N_CHIPS,))
        return {n: _from_chip_major(n, sh) for n, sh in zip(names, shards)}

    first_w = [n for n in BIG if int(n[1]) <= EARLY_LAYER]
    later_w = [n for n in BIG if n not in first_w]
    big = full_matrices(first_w, _gather_shards(packed_shards(first_w)))

    small = {n: a[n] for n in SMALL}
    convs = [n for n in SMALL if n.endswith("dn_conv_w")]
    placed = []
    for n in convs:
        full = jnp.zeros(_full_shape_conv(a[n].shape), F32)
        placed.append(lax.dynamic_update_slice(full, a[n], (0, chip * a[n].shape[1])))
    conv_sum = _all_reduce_small("gather_conv", _pack(placed, 8))
    for n, full in zip(convs, _unpack(conv_sum, [p.shape for p in placed])):
        small[n] = full * 0.5

    def pair_sums(names, grads, tag):
        g_all = _pack([_chip_major(n, grads[n]) for n in names], BIG_ROWS, lead=(N_CHIPS,))
        return _pair_sum(g_all, _swap_halves(g_all, tag), c_i, tag)

    def reduced(names, s32, got, tag):
        whole = _join_halves(_chip_sum(s32, got, chip, c_i, tag), tag)
        return dict(zip(names, _unpack(whole, [a[n].shape for n in names])))

    early_prefix = "l%d_" % EARLY_LAYER
    early = [n for n in BIG if int(n[1]) > EARLY_LAYER or n in (early_prefix + "ffn_w_gate_up", early_prefix + "ffn_w_down",
                                                                early_prefix + "sb_w_out")]
    late = [n for n in BIG if n not in early]

    class Exchange:
        def late_shards(self):
            return packed_shards(later_w)

        def deliver(self, gathered):
            big.update(full_matrices(later_w, gathered))

        def start(self, layer_grads):
            have = dict(self.grads, **{early_prefix + k: val for k, val in layer_grads.items()})
            self.s32, s16 = pair_sums(early, have, "_early")
            return s16

    exchange = Exchange()
    part, dx, grads = local_step(a["x"][0], a["loss_target"][0], big, small, exchange)
    loss = lax.psum(0.5 * jnp.sum(part) / D_MODEL, ("x", "y", "c"))
    g_big = reduced(early, exchange.s32, exchange.arrived, "_early")
    s32, s16 = pair_sums(late, grads, "_late")
    g_big.update(reduced(late, s32, _scatter_chunks(s16), "_late"))

    g_small_full = _all_reduce_small("reduce_small", _pack([grads[n] for n in SMALL], 8))
    g_small = dict(zip(SMALL, _unpack(g_small_full, [grads[n].shape for n in SMALL])))
    for n in convs:
        g_small[n] = lax.dynamic_slice_in_dim(g_small[n], chip * a[n].shape[1], a[n].shape[1], axis=1)

    outs = {}
    for n in BIG:
        d, m2, v2 = _ew("adam_" + n, _adam_fn, [a[n], g_big[n], a["m_" + n], a["v_" + n]], [],
                        [(a[n].shape[1], F32)] * 3, tm=_row_tile(a[n].shape[0]))
        outs.update({"grad_" + n: g_big[n], "delta_" + n: d, "new_m_" + n: m2, "new_v_" + n: v2})
    pk = lambda prefix: _pack([a[prefix + n] for n in SMALL], 8)
    gs_packed = _pack([g_small[n] for n in SMALL], 8)
    small_bufs = (gs_packed,) + tuple(_ew("adam_small", _adam_fn, [pk(""), gs_packed, pk("m_"), pk("v_")], [], [(ROW, F32)] * 3,
                                          tm=gs_packed.shape[0]))
    small_shapes = [a[n].shape for n in SMALL]
    for key, buf in zip(("grad_", "delta_", "new_m_", "new_v_"), small_bufs):
        outs.update({key + n: val for n, val in zip(SMALL, _unpack(buf, small_shapes))})
    result = [loss, dx[None]]
    for key in ("grad_", "delta_", "new_m_", "new_v_"):
        result += [outs[key + n] for n in WEIGHTS]
    return tuple(result)


def _full_shape_conv(shard_shape):
    return (shard_shape[0], shard_shape[1] * N_CHIPS)


def kernel(x, l0_mix_norm, l0_dn_w_in, l0_dn_conv_w, l0_dn_a_log, l0_dn_dt_bias, l0_dn_out_norm, l0_dn_w_out, l0_ffn_norm, l0_ffn_w_gate_up, l0_ffn_w_down, l1_mix_norm, l1_sb_w_qkv, l1_sb_q_norm, l1_sb_k_norm, l1_sb_w_out, l1_ffn_norm, l1_ffn_w_gate_up, l1_ffn_w_down, l2_mix_norm, l2_mla_w_down, l2_mla_q_a_norm, l2_mla_kv_a_norm, l2_mla_w_uq, l2_mla_w_ukv, l2_mla_q_nope_norm, l2_mla_q_rope_norm, l2_mla_k_nope_norm, l2_mla_k_rope_norm, l2_mla_w_out, l2_ffn_norm, l2_ffn_w_gate_up, l2_ffn_w_down, l3_mix_norm, l3_dn_w_in, l3_dn_conv_w, l3_dn_a_log, l3_dn_dt_bias, l3_dn_out_norm, l3_dn_w_out, l3_ffn_norm, l3_ffn_w_gate_up, l3_ffn_w_down, loss_target, m_l0_mix_norm, m_l0_dn_w_in, m_l0_dn_conv_w, m_l0_dn_a_log, m_l0_dn_dt_bias, m_l0_dn_out_norm, m_l0_dn_w_out, m_l0_ffn_norm, m_l0_ffn_w_gate_up, m_l0_ffn_w_down, m_l1_mix_norm, m_l1_sb_w_qkv, m_l1_sb_q_norm, m_l1_sb_k_norm, m_l1_sb_w_out, m_l1_ffn_norm, m_l1_ffn_w_gate_up, m_l1_ffn_w_down, m_l2_mix_norm, m_l2_mla_w_down, m_l2_mla_q_a_norm, m_l2_mla_kv_a_norm, m_l2_mla_w_uq, m_l2_mla_w_ukv, m_l2_mla_q_nope_norm, m_l2_mla_q_rope_norm, m_l2_mla_k_nope_norm, m_l2_mla_k_rope_norm, m_l2_mla_w_out, m_l2_ffn_norm, m_l2_ffn_w_gate_up, m_l2_ffn_w_down, m_l3_mix_norm, m_l3_dn_w_in, m_l3_dn_conv_w, m_l3_dn_a_log, m_l3_dn_dt_bias, m_l3_dn_out_norm, m_l3_dn_w_out, m_l3_ffn_norm, m_l3_ffn_w_gate_up, m_l3_ffn_w_down, v_l0_mix_norm, v_l0_dn_w_in, v_l0_dn_conv_w, v_l0_dn_a_log, v_l0_dn_dt_bias, v_l0_dn_out_norm, v_l0_dn_w_out, v_l0_ffn_norm, v_l0_ffn_w_gate_up, v_l0_ffn_w_down, v_l1_mix_norm, v_l1_sb_w_qkv, v_l1_sb_q_norm, v_l1_sb_k_norm, v_l1_sb_w_out, v_l1_ffn_norm, v_l1_ffn_w_gate_up, v_l1_ffn_w_down, v_l2_mix_norm, v_l2_mla_w_down, v_l2_mla_q_a_norm, v_l2_mla_kv_a_norm, v_l2_mla_w_uq, v_l2_mla_w_ukv, v_l2_mla_q_nope_norm, v_l2_mla_q_rope_norm, v_l2_mla_k_nope_norm, v_l2_mla_k_rope_norm, v_l2_mla_w_out, v_l2_ffn_norm, v_l2_ffn_w_gate_up, v_l2_ffn_w_down, v_l3_mix_norm, v_l3_dn_w_in, v_l3_dn_conv_w, v_l3_dn_a_log, v_l3_dn_dt_bias, v_l3_dn_out_norm, v_l3_dn_w_out, v_l3_ffn_norm, v_l3_ffn_w_gate_up, v_l3_ffn_w_down):
    return _step(dict(locals()))
```

```python
import functools
import math

import jax
import jax.numpy as jnp
from jax import lax
from jax.experimental import pallas as pl
from jax.experimental.pallas import tpu as pltpu

F32, BF16 = jnp.float32, jnp.bfloat16
MESH = pl.DeviceIdType.MESH

D_MODEL = 1024
N_HEADS = 8
HEAD = 128
FFN_HIDDEN = 2816
DN_CHUNK = 64
NORM_EPS = 1e-6
MLA_ROPE = 64
MLA_QK = 192
ROPE_THETA = 10000.0
ADAM_LR, ADAM_B1, ADAM_B2, ADAM_EPS, ADAM_WD, ADAM_STEP = 0.001, 0.9, 0.999, 1e-08, 0.01, 10
N_CHIPS = 4
LANES = 128
VMEM_LIMIT = 56 * 2 ** 20


def _params(n_grid):
    return pltpu.CompilerParams(dimension_semantics=("arbitrary",) * n_grid, vmem_limit_bytes=VMEM_LIMIT)


_MIXERS = (
    ("dn_w_in", "dn_conv_w", "dn_a_log", "dn_dt_bias", "dn_out_norm", "dn_w_out"),
    ("sb_w_qkv", "sb_q_norm", "sb_k_norm", "sb_w_out"),
    ("mla_w_down", "mla_q_a_norm", "mla_kv_a_norm", "mla_w_uq", "mla_w_ukv", "mla_q_nope_norm",
     "mla_q_rope_norm", "mla_k_nope_norm", "mla_k_rope_norm", "mla_w_out"),
)
DEPTH = 4


def _layer_names(i):
    p = "l%d_" % i
    return [p + "mix_norm"] + [p + n for n in _MIXERS[i % 3]] + [p + "ffn_norm", p + "ffn_w_gate_up", p + "ffn_w_down"]


WEIGHTS = [n for i in range(DEPTH) for n in _layer_names(i)]
_ROW_SHARDED = ("w_out", "ffn_w_down", "mla_w_down")
_COL_SHARDED = ("dn_w_in", "sb_w_qkv", "mla_w_uq", "mla_w_ukv", "ffn_w_gate_up")


def _shard_axis(name):
    if name.endswith(_ROW_SHARDED):
        return 0
    if name.endswith(_COL_SHARDED):
        return 1
    return None


BIG = [n for n in WEIGHTS if _shard_axis(n) is not None]
SMALL = [n for n in WEIGHTS if _shard_axis(n) is None]


_DN = {"nn": (((1,), (0,)), ((), ())), "nt": (((1,), (1,)), ((), ())), "tn": (((0,), (0,)), ((), ()))}
_DN_BATCHED = {"nn": (((2,), (1,)), ((0,), (0,))), "nt": (((2,), (2,)), ((0,), (0,))), "tn": (((1,), (1,)), ((0,), (0,)))}


def _dims(a, kind):
    return _DN_BATCHED[kind] if a.ndim == 3 else _DN[kind]


def _dg(a, b, kind):
    return lax.dot_general(a.astype(BF16), b.astype(BF16), _dims(a, kind), preferred_element_type=F32)


@functools.partial(jax.custom_vjp, nondiff_argnums=(2,))
def bdot(a, b, kind):
    return _dg(a, b, kind)


def _bdot_fwd(a, b, kind):
    return _dg(a, b, kind), (a, b)


def _bdot_bwd(kind, res, ct):
    a, b = res
    if kind == "nn":
        return _dg(ct, b, "nt"), _dg(a, ct, "tn")
    if kind == "nt":
        return _dg(ct, b, "nn"), _dg(ct, a, "tn")
    return _dg(b, ct, "nt"), _dg(a, ct, "nn")


bdot.defvjp(_bdot_fwd, _bdot_bwd)


def _split(a, terms):
    out = []
    for _ in range(terms):
        hi = a.astype(BF16)
        out.append(hi)
        a = a - hi.astype(F32)
    return out


def _xdot(a, b, kind, exact, terms=3):
    if exact == 0:
        return sum(lax.dot_general(a, p, _dims(a, kind), preferred_element_type=F32) for p in _split(b, terms))
    return sum(lax.dot_general(p, b, _dims(a, kind), preferred_element_type=F32) for p in _split(a, terms))


def _tri(n, rel):
    r = lax.broadcasted_iota(jnp.int32, (n, n), 0)
    c = lax.broadcasted_iota(jnp.int32, (n, n), 1)
    return {"le": c <= r, "lt": c < r, "ge": c >= r, "gt": c > r}[rel]


def _running(g):
    n = g.shape[-2]
    return jnp.broadcast_to(_tri(n, "le").astype(BF16), g.shape[:-2] + (n, n))


@jax.custom_vjp
def cumsum_rows(g):
    return _xdot(_running(g), g, "nn", 0)


def _cumsum_fwd(g):
    return cumsum_rows(g), None


def _cumsum_bwd(_, ct):
    return (_xdot(_running(ct), ct, "tn", 0),)


cumsum_rows.defvjp(_cumsum_fwd, _cumsum_bwd)


def _dot3(a, b, kind):
    (ah, al), (bh, bl) = _split(a, 2), _split(b, 2)
    dot = lambda p, q: lax.dot_general(p, q, _dims(a, kind), preferred_element_type=F32)
    return dot(ah, bh) + (dot(ah, bl) + dot(al, bh))


@functools.partial(jax.custom_vjp, nondiff_argnums=(2,))
def _hdot3(a, b, kind):
    return _dot3(a, b, kind)


def _hdot3_fwd(a, b, kind):
    return _dot3(a, b, kind), (a, b)


def _hdot3_bwd(kind, res, ct):
    a, b = res
    if kind == "nn":
        return _dot3(ct, b, "nt"), _dot3(a, ct, "tn")
    if kind == "nt":
        return _dot3(ct, b, "nn"), _dot3(ct, a, "tn")
    return _dot3(b, ct, "nt"), _dot3(a, ct, "nn")


_hdot3.defvjp(_hdot3_fwd, _hdot3_bwd)


def _hdot(a, b):
    return _hdot3(a, b, "nn")


def _unit_lower_inverse(lower):
    n = lower.shape[-1]
    eye = (lax.broadcasted_iota(jnp.int32, (n, n), 0) == lax.broadcasted_iota(jnp.int32, (n, n), 1)).astype(F32)
    m = -lower
    p = eye + m
    for _ in range(int(math.log2(n)) - 1):
        m = _hdot(m, m)
        p = p + _hdot(p, m)
    return p


def _rms(x, g, n=None):
    n = x.shape[-1] if n is None else n
    return x * lax.rsqrt(jnp.sum(x * x, axis=-1, keepdims=True) * (1.0 / n) + NORM_EPS) * g


def _l2(x):
    return x * lax.rsqrt(jnp.sum(x * x, axis=-1, keepdims=True) + NORM_EPS)


def _silu(x):
    return x * jax.nn.sigmoid(x)


def _logsig(z):
    return jnp.minimum(z, 0.0) - jnp.log1p(jnp.exp(-jnp.abs(z)))


@jax.custom_vjp
def _rope(x, cos, sin_lo, sin_hi):
    return x * cos + pltpu.roll(x, 96, 1) * sin_lo + pltpu.roll(x, 32, 1) * sin_hi


def _rope_fwd(x, cos, sin_lo, sin_hi):
    return _rope(x, cos, sin_lo, sin_hi), (cos, sin_lo, sin_hi)


def _rope_bwd(res, ct):
    cos, sin_lo, sin_hi = res
    dx = ct * cos + pltpu.roll(ct * sin_lo, 32, 1) + pltpu.roll(ct * sin_hi, 96, 1)
    return dx, jnp.zeros_like(cos), jnp.zeros_like(sin_lo), jnp.zeros_like(sin_hi)


_rope.defvjp(_rope_fwd, _rope_bwd)


def _tile(n, prefs=(512, 384, 256, 128)):
    for t in prefs:
        if n % t == 0:
            return t
    return n


MM_OUT_TILES = (1024, 1408, 512, 384, 256, 128)
MM_K_TILES = (1024, 512, 384, 256, 128)


def _mm(name, a, b, kind, out_dtype=F32, add=None):
    if kind == "tn":
        (kdim, m), n = a.shape, b.shape[1]
    else:
        (m, kdim), n = a.shape, (b.shape[0] if kind == "nt" else b.shape[1])
    tm, tn, tk = _tile(m, MM_OUT_TILES), _tile(n, MM_OUT_TILES), _tile(kdim, MM_K_TILES)
    nk = kdim // tk
    a_spec = pl.BlockSpec((tk, tm), lambda i, j, k: (k, i)) if kind == "tn" else pl.BlockSpec((tm, tk), lambda i, j, k: (i, k))
    b_spec = pl.BlockSpec((tn, tk), lambda i, j, k: (j, k)) if kind == "nt" else pl.BlockSpec((tk, tn), lambda i, j, k: (k, j))
    o_spec = pl.BlockSpec((tm, tn), lambda i, j, k: (i, j))
    has_add = add is not None

    def body(*refs):
        a_ref, b_ref = refs[0], refs[1]
        o_ref, acc = refs[-2], refs[-1]
        k = pl.program_id(2)

        @pl.when(k == 0)
        def _():
            acc[...] = jnp.zeros_like(acc)

        acc[...] += _dg(a_ref[...], b_ref[...], kind)

        @pl.when(k == nk - 1)
        def _():
            r = acc[...]
            if has_add:
                r = r + refs[2][...]
            o_ref[...] = r.astype(o_ref.dtype)

    return pl.pallas_call(
        body, grid=(m // tm, n // tn, nk),
        in_specs=[a_spec, b_spec] + ([o_spec] if has_add else []),
        out_specs=o_spec, out_shape=jax.ShapeDtypeStruct((m, n), out_dtype),
        scratch_shapes=[pltpu.VMEM((tm, tn), F32)], name=name, compiler_params=_params(3),
    )(*([a, b] + ([add] if has_add else [])))


class _V:
    def __init__(self, arr, w=None, base=0, diff=True):
        self.arr, self.base, self.diff = arr, base, diff
        self.w = arr.shape[1] if w is None else w

    def spec(self, tm):
        return pl.BlockSpec((tm, self.w), lambda i, b=self.base: (i, b))


def _as_views(ins):
    return [v if isinstance(v, _V) else _V(v) for v in ins]


def _tup(r):
    return tuple(r) if isinstance(r, (tuple, list)) else (r,)


def _ew(name, fn, ins, smalls, outs, tm=256):
    ins = _as_views(ins)
    t = ins[0].arr.shape[0]
    tm = min(tm, t)
    n_in = len(ins) + len(smalls)

    def body(*refs):
        res = _tup(fn(*[r[...] for r in refs[:n_in]]))
        for r, o in zip(refs[n_in:], res):
            r[...] = o.astype(r.dtype)

    return pl.pallas_call(
        body, grid=(t // tm,),
        in_specs=[v.spec(tm) for v in ins] + [pl.BlockSpec(s.shape, lambda i: (0, 0)) for s in smalls],
        out_specs=[pl.BlockSpec((tm, w), lambda i: (i, 0)) for w, _ in outs],
        out_shape=[jax.ShapeDtypeStruct((t, w), dt) for w, dt in outs],
        name=name, compiler_params=_params(1),
    )(*[v.arr for v in ins], *smalls)


def _ew_bwd(name, fn, ins, smalls, cts, tm=256, add=None, ct_dtypes=None):
    ins = _as_views(ins)
    t = ins[0].arr.shape[0]
    tm = min(tm, t)
    n_in, n_sm = len(ins), len(smalls)
    diff = [k for k, v in enumerate(ins) if v.diff]
    ct_dtypes = [F32] * len(diff) if ct_dtypes is None else ct_dtypes
    ct_arrs = [c for c in cts if c is not None]
    has_add = add is not None

    def body(*refs):
        vals = [r[...] for r in refs[:n_in]]
        svals = [r[...] for r in refs[n_in:n_in + n_sm]]
        p = n_in + n_sm
        ct_refs = list(refs[p:p + len(ct_arrs)])
        p += len(ct_arrs)
        add_ref = refs[p] if has_add else None
        p += int(has_add)
        din_refs = refs[p:p + len(diff)]
        dsm_refs = refs[p + len(diff):]

        def f(dv, sv):
            full = list(vals)
            for k, d in zip(diff, dv):
                full[k] = d
            return _tup(fn(*full, *sv))

        res, vjp = jax.vjp(f, [vals[k] for k in diff], svals)
        ctv = tuple(jnp.zeros_like(o) if c is None else ct_refs.pop(0)[...].astype(o.dtype) for c, o in zip(cts, res))
        dv, dsv = vjp(ctv)
        for n, (r, d) in enumerate(zip(din_refs, dv)):
            if n == 0 and has_add:
                d = d + add_ref[...]
            r[...] = d.astype(r.dtype)

        @pl.when(pl.program_id(0) == 0)
        def _():
            for r in dsm_refs:
                r[...] = jnp.zeros_like(r)

        for r, d in zip(dsm_refs, dsv):
            r[...] += d

    row = lambda w: pl.BlockSpec((tm, w), lambda i: (i, 0))
    small_specs = [pl.BlockSpec(s.shape, lambda i: (0, 0)) for s in smalls]
    out = pl.pallas_call(
        body, grid=(t // tm,),
        in_specs=[v.spec(tm) for v in ins] + small_specs + [row(c.shape[1]) for c in ct_arrs]
        + ([row(add.shape[1])] if has_add else []),
        out_specs=[row(ins[k].w) for k in diff] + small_specs,
        out_shape=[jax.ShapeDtypeStruct((t, ins[k].w), dt) for k, dt in zip(diff, ct_dtypes)]
        + [jax.ShapeDtypeStruct(s.shape, F32) for s in smalls],
        name=name, compiler_params=_params(1),
    )(*[v.arr for v in ins], *smalls, *ct_arrs, *([add] if has_add else []))
    return out[:len(diff)], out[len(diff):]


BQ = 256
HPB = 2
SUM_TERMS = 2


def _cat(parts):
    return parts[0] if len(parts) == 1 else jnp.concatenate(parts, axis=1)


def _head_view(ref, hh):
    return ref.at[:, hh * HEAD:(hh + 1) * HEAD]


def _attn_specs(qs, ks, t):
    q_specs = [pl.BlockSpec((BQ, HPB * HEAD), lambda h, i: (i, h)) for _ in qs]
    per_head = pl.BlockSpec((t, HPB * HEAD), lambda h, i: (0, h))
    k_specs = [pl.BlockSpec((t, HEAD), lambda h, i: (0, 0)) if sh else per_head for _, sh in ks]
    return q_specs, k_specs, per_head


def _causal_sweep(i, pair, init, diagonal_first, log2_blocks):
    order = (lambda s: i - 1 - s) if diagonal_first else (lambda s: s)
    carry = pair(i, init, True) if diagonal_first else init
    done = 0
    for level in range(log2_blocks, -1, -1):
        per = 1 << level

        def group(s, c, per=per, done=done):
            for u in range(per):
                c = pair(order(done + per * s + u), c, False)
            return c

        left = i - done
        carry = lax.fori_loop(0, lax.shift_right_logical(left, level), group, carry)
        done = done + (left - (left & (per - 1)))
    return carry if diagonal_first else pair(i, carry, True)


def _attn_fwd(name, mode, qs, ks, v, gather=None):
    t = qs[0].shape[0]
    nq, n = t // BQ, len(qs)
    q_specs, k_specs, per_head = _attn_specs(qs, ks, t)
    shared = [sh for _, sh in ks]

    def body(*refs):
        q_refs, k_refs, v_ref = refs[:n], refs[n:2 * n], refs[2 * n]
        n_in = 2 * n + 1 + int(gather is not None)
        o_ref, st_ref = refs[n_in], refs[n_in + 1]
        g, i = pl.program_id(0), pl.program_id(1)
        last_g = N_HEADS // HPB - 1
        if gather is not None:
            phases = _gather_phases(refs[n_in - 1], refs[n_in + 2], refs[-2], refs[-1])
            pl.when((g == 0) & (i == 0))(phases[0])
            pl.when((g == last_g) & (i == 0))(phases[1])
        row = lax.broadcasted_iota(jnp.int32, (BQ, BQ), 0)
        col = lax.broadcasted_iota(jnp.int32, (BQ, BQ), 1)
        after = _tri(BQ, "lt").astype(BF16)

        def head(hh):
            q = _cat([_head_view(r, hh)[...] for r in q_refs])
            k_h = [kr if sh else _head_view(kr, hh) for kr, sh in zip(k_refs, shared)]
            v_h = _head_view(v_ref, hh)

            def pair(j, carry, masked):
                off = pl.multiple_of(j * BQ, BQ)
                z = _dg(q, _cat([kr[pl.ds(off, BQ), :] for kr in k_h]), "nt")
                vj = v_h[pl.ds(off, BQ), :]
                if mode == "sb":
                    acc, run = carry
                    lsz = _logsig(z)
                    stay = lsz - z
                    if masked:
                        stay = jnp.where(col < row, stay, 0.0)
                    a = jnp.exp(lsz + (run + _xdot(stay, after, "nn", 1, SUM_TERMS)))
                    if masked:
                        a = jnp.where(col < row, a, 0.0)
                    return acc + _dg(a, vj, "nn"), run + jnp.sum(stay, axis=1, keepdims=True)
                m, l, acc = carry
                if masked:
                    z = jnp.where(col <= row, z, -1e30)
                m2 = jnp.maximum(m, jnp.max(z, axis=1, keepdims=True))
                p = jnp.exp(z - m2)
                alpha = jnp.exp(m - m2)
                return m2, alpha * l + jnp.sum(p, axis=1, keepdims=True), alpha * acc + _dg(p, vj, "nn")

            def finish(carry):
                if mode == "sb":
                    acc, run = carry
                    _head_view(o_ref, hh)[...] = acc
                    _head_view(st_ref, hh)[...] = jnp.broadcast_to(run, (BQ, HEAD))
                else:
                    m, l, acc = carry
                    _head_view(o_ref, hh)[...] = acc / l
                    _head_view(st_ref, hh)[...] = jnp.broadcast_to(m + jnp.log(l), (BQ, HEAD))

            zero = jnp.zeros((BQ, 1), F32)
            acc0 = jnp.zeros((BQ, HEAD), F32)
            init = (acc0, zero) if mode == "sb" else (jnp.full((BQ, 1), -1e30, F32), zero, acc0)
            return pair, init, finish

        heads = [head(hh) for hh in range(HPB)]
        both = lambda j, carry, masked: tuple(h[0](j, c, masked) for h, c in zip(heads, carry))
        final = _causal_sweep(i, both, tuple(h[1] for h in heads), diagonal_first=(mode == "sb"), log2_blocks=2)
        for h, c in zip(heads, final):
            h[2](c)
        if gather is not None:
            pl.when((g == last_g) & (i == nq - 1))(phases[2])

    blk = pl.BlockSpec((BQ, HPB * HEAD), lambda h, i: (i, h))
    wide = jax.ShapeDtypeStruct((t, N_HEADS * HEAD), F32)
    comm_in, comm_out, comm_shape, comm_scratch = [], [], [], []
    if gather is not None:
        comm_in, comm_out = [gather], [HBM]
        comm_shape = [jax.ShapeDtypeStruct((N_CHIPS,) + gather.shape, gather.dtype)]
        comm_scratch = [pltpu.SemaphoreType.DMA((GATHER_SEMS,)), pltpu.SemaphoreType.DMA((GATHER_SEMS,))]
    return pl.pallas_call(
        body, grid=(N_HEADS // HPB, nq), in_specs=q_specs + k_specs + [per_head] + [HBM] * len(comm_in),
        out_specs=[blk, blk] + comm_out, out_shape=[wide, wide] + comm_shape, scratch_shapes=comm_scratch,
        name=name, compiler_params=_params(2),
    )(*qs, *[k for k, _ in ks], v, *comm_in)


def _attn_bwd(name, mode, qs, ks, v, o, stat, do, send=None):
    t = qs[0].shape[0]
    nq, n = t // BQ, len(qs)
    q_specs, k_specs, per_head = _attn_specs(qs, ks, t)
    shared = [sh for _, sh in ks]

    def body(*refs):
        q_refs, k_refs, v_ref = refs[:n], refs[n:2 * n], refs[2 * n]
        o_ref, st_ref, do_ref = refs[2 * n + 1:2 * n + 4]
        n_in = 2 * n + 4 + int(send is not None)
        dq_refs = refs[n_in:n_in + n]
        dk_refs = refs[n_in + n:n_in + 2 * n]
        dv_ref = refs[n_in + 2 * n]
        g, i = pl.program_id(0), pl.program_id(1)
        if send is not None:
            send_ref, got_ref, send_sems, recv_sems = refs[n_in - 1], refs[n_in + 2 * n + 1], refs[-2], refs[-1]

            def exchange():
                x, y, c = _me()
                return [pltpu.make_async_remote_copy(src_ref=send_ref.at[2 * px + py], dst_ref=got_ref.at[j],
                                                     send_sem=send_sems.at[j], recv_sem=recv_sems.at[j],
                                                     device_id=(px, py, c), device_id_type=MESH)
                        for j, (px, py) in enumerate(_other_chips(x, y))]

            @pl.when((g == 0) & (i == 0))
            def _():
                for cp in exchange():
                    cp.start()

        @pl.when(i == 0)
        def _():
            dv_ref[...] = jnp.zeros_like(dv_ref)
            for r, sh in zip(dk_refs, shared):
                if not sh:
                    r[...] = jnp.zeros_like(r)

        for r, sh in zip(dk_refs, shared):
            if sh:
                @pl.when((i == 0) & (g == 0))
                def _(r=r):
                    r[...] = jnp.zeros_like(r)

        row = lax.broadcasted_iota(jnp.int32, (BQ, BQ), 0)
        col = lax.broadcasted_iota(jnp.int32, (BQ, BQ), 1)
        upto = _tri(BQ, "ge").astype(BF16)
        before = _tri(BQ, "gt").astype(BF16)

        def head(hh):
            q = _cat([_head_view(r, hh)[...] for r in q_refs])
            k_h = [kr if sh else _head_view(kr, hh) for kr, sh in zip(k_refs, shared)]
            dk_h = [r if sh else _head_view(r, hh) for r, sh in zip(dk_refs, shared)]
            v_h, dv_h = _head_view(v_ref, hh), _head_view(dv_ref, hh)
            do_t = _head_view(do_ref, hh)[...]
            st = _head_view(st_ref, hh)[:, :1]
            if mode == "softmax":
                dsum = jnp.sum(do_t * _head_view(o_ref, hh)[...], axis=1, keepdims=True)

            def pair(j, carry, masked):
                off = pl.multiple_of(j * BQ, BQ)
                kj = _cat([kr[pl.ds(off, BQ), :] for kr in k_h])
                z = _dg(q, kj, "nt")
                da = _dg(do_t, v_h[pl.ds(off, BQ), :], "nt")
                if mode == "sb":
                    dq, pre, gpre = carry
                    lsz = _logsig(z)
                    stay = lsz - z
                    if masked:
                        stay = jnp.where(col < row, stay, 0.0)
                    a = jnp.exp(lsz + (st - (pre + _xdot(stay, upto, "nn", 1, SUM_TERMS))))
                    if masked:
                        a = jnp.where(col < row, a, 0.0)
                    gr = a * da
                    sig = jnp.exp(lsz)
                    dz = gr * (1.0 - sig) - sig * (gpre + _xdot(gr, before, "nn", 1, 1))
                    if masked:
                        dz = jnp.where(col < row, dz, 0.0)
                    tail = (pre + jnp.sum(stay, axis=1, keepdims=True), gpre + jnp.sum(gr, axis=1, keepdims=True))
                else:
                    dq = carry[0]
                    a = jnp.exp(z - st)
                    if masked:
                        a = jnp.where(col <= row, a, 0.0)
                    dz = a * (da - dsum)
                    tail = ()
                dk = _dg(dz, q, "tn")
                for p, r in enumerate(dk_h):
                    r[pl.ds(off, BQ), :] += dk[:, p * HEAD:(p + 1) * HEAD]
                dv_h[pl.ds(off, BQ), :] += _dg(a, do_t, "tn")
                return (dq + _dg(dz, kj, "nn"),) + tail

            def finish(carry):
                for p, r in enumerate(dq_refs):
                    _head_view(r, hh)[...] = carry[0][:, p * HEAD:(p + 1) * HEAD]

            zero = jnp.zeros((BQ, 1), F32)
            init = (jnp.zeros((BQ, n * HEAD), F32),) + ((zero, zero) if mode == "sb" else ())
            return pair, init, finish

        heads = [head(hh) for hh in range(HPB)]
        both = lambda j, carry, masked: tuple(h[0](j, c, masked) for h, c in zip(heads, carry))
        final = _causal_sweep(i, both, tuple(h[1] for h in heads), diagonal_first=False, log2_blocks=2)
        for h, c in zip(heads, final):
            h[2](c)

        if send is not None:
            @pl.when((g == N_HEADS // HPB - 1) & (i == nq - 1))
            def _():
                for cp in exchange():
                    cp.wait()

    blk = pl.BlockSpec((BQ, HPB * HEAD), lambda h, i: (i, h))
    dk_specs = [pl.BlockSpec((t, HEAD), lambda h, i: (0, 0)) if sh else per_head for sh in shared]
    wide = jax.ShapeDtypeStruct((t, N_HEADS * HEAD), F32)
    comm_in, comm_out, comm_shape, comm_scratch = [], [], [], []
    if send is not None:
        comm_in, comm_out = [send], [HBM]
        comm_shape = [jax.ShapeDtypeStruct((3,) + send.shape[1:], send.dtype)]
        comm_scratch = [pltpu.SemaphoreType.DMA((3,)), pltpu.SemaphoreType.DMA((3,))]
    out = pl.pallas_call(
        body, grid=(N_HEADS // HPB, nq), in_specs=q_specs + k_specs + [per_head, blk, blk, blk] + [HBM] * len(comm_in),
        out_specs=[blk] * n + dk_specs + [per_head] + comm_out,
        out_shape=[wide] * n + [jax.ShapeDtypeStruct((t, HEAD), F32) if sh else wide for sh in shared] + [wide] + comm_shape,
        scratch_shapes=comm_scratch, name=name, compiler_params=_params(2),
    )(*qs, *[k for k, _ in ks], v, o, stat, do, *comm_in)
    return (out[:n], out[n:2 * n], out[2 * n]) + tuple(out[2 * n + 1:])


@jax.custom_vjp
def _given_inverse(lower, tinv):
    return tinv


def _given_inverse_fwd(lower, tinv):
    return tinv, tinv


def _given_inverse_bwd(tinv, ct):
    return -_dot3(_dot3(tinv, ct, "tn"), tinv, "nt"), jnp.zeros_like(tinv)


_given_inverse.defvjp(_given_inverse_fwd, _given_inverse_bwd)


def _dn_chunk(q, k, v, g, beta, state, tinv=None):
    c = q.shape[-2]
    gc = cumsum_rows(g)
    gcc = gc[..., :c]
    diff = gcc - jnp.swapaxes(gcc, -1, -2)
    causal, strict = _tri(c, "le"), _tri(c, "lt")
    decay = jnp.where(causal, jnp.exp(jnp.where(causal, diff, 0.0)), 0.0)
    kb = k * beta
    lower = jnp.where(strict, bdot(kb, k, "nt") * decay, 0.0)
    tinv = _unit_lower_inverse(lower) if tinv is None else _given_inverse(lower, tinv)
    eg = jnp.exp(gc)
    u = _hdot(tinv, v * beta)
    w = _hdot(tinv, kb * eg)
    attn = bdot(q, k, "nt") * decay
    glast = gc[..., c - 1:c, :]
    v_new = u - bdot(w, state, "nn")
    o = bdot(q * eg, state, "nn") + bdot(attn, v_new, "nn")
    new_state = state * jnp.exp(glast) + bdot(k * jnp.exp(glast - gc), v_new, "tn")
    return o, new_state, tinv


def _stack_heads(ref):
    return jnp.stack([ref[:, h * HEAD:(h + 1) * HEAD] for h in range(N_HEADS)])


def _store_heads(ref, val):
    for h in range(N_HEADS):
        ref[:, h * HEAD:(h + 1) * HEAD] = val[h]


def _dn_fwd(name, q, k, v, g, beta):
    t = q.shape[0]
    nc = t // DN_CHUNK
    wide = N_HEADS * HEAD
    blk = pl.BlockSpec((DN_CHUNK, wide), lambda n: (n, 0))
    st_spec = pl.BlockSpec((N_HEADS, None, HEAD, HEAD), lambda n: (0, n, 0, 0))
    inv_spec = pl.BlockSpec((N_HEADS, None, DN_CHUNK, DN_CHUNK), lambda n: (0, n, 0, 0))

    def body(q_ref, k_ref, v_ref, g_ref, b_ref, o_ref, st_ref, inv_ref, state):
        @pl.when(pl.program_id(0) == 0)
        def _():
            state[...] = jnp.zeros_like(state)

        s_in = state[...]
        st_ref[...] = s_in
        o, s_out, tinv = _dn_chunk(*[_stack_heads(r) for r in (q_ref, k_ref, v_ref, g_ref, b_ref)], s_in)
        _store_heads(o_ref, o)
        inv_ref[...] = tinv
        state[...] = s_out

    return pl.pallas_call(
        body, grid=(nc,), in_specs=[blk] * 5, out_specs=[blk, st_spec, inv_spec],
        out_shape=[jax.ShapeDtypeStruct((t, wide), F32), jax.ShapeDtypeStruct((N_HEADS, nc, HEAD, HEAD), F32),
                   jax.ShapeDtypeStruct((N_HEADS, nc, DN_CHUNK, DN_CHUNK), F32)],
        scratch_shapes=[pltpu.VMEM((N_HEADS, HEAD, HEAD), F32)], name=name, compiler_params=_params(1),
    )(q, k, v, g, beta)


def _dn_bwd(name, q, k, v, g, beta, states, inverses, do):
    t = q.shape[0]
    nc = t // DN_CHUNK
    wide = N_HEADS * HEAD
    blk = pl.BlockSpec((DN_CHUNK, wide), lambda n: (nc - 1 - n, 0))
    st_spec = pl.BlockSpec((N_HEADS, None, HEAD, HEAD), lambda n: (0, nc - 1 - n, 0, 0))
    inv_spec = pl.BlockSpec((N_HEADS, None, DN_CHUNK, DN_CHUNK), lambda n: (0, nc - 1 - n, 0, 0))

    def body(q_ref, k_ref, v_ref, g_ref, b_ref, st_ref, inv_ref, do_ref, dq_ref, dk_ref, dv_ref, dg_ref, db_ref, dstate):
        @pl.when(pl.program_id(0) == 0)
        def _():
            dstate[...] = jnp.zeros_like(dstate)

        tinv = inv_ref[...]
        chunk = lambda *args: _dn_chunk(*args, tinv=tinv)[:2]
        _, vjp = jax.vjp(chunk, *[_stack_heads(r) for r in (q_ref, k_ref, v_ref, g_ref, b_ref)], st_ref[...])
        cts = vjp((_stack_heads(do_ref), dstate[...]))
        for r, d in zip((dq_ref, dk_ref, dv_ref, dg_ref, db_ref), cts[:5]):
            _store_heads(r, d)
        dstate[...] = cts[5]

    shape = jax.ShapeDtypeStruct((t, wide), F32)
    return pl.pallas_call(
        body, grid=(nc,), in_specs=[blk] * 5 + [st_spec, inv_spec, blk], out_specs=[blk] * 5, out_shape=[shape] * 5,
        scratch_shapes=[pltpu.VMEM((N_HEADS, HEAD, HEAD), F32)], name=name, compiler_params=_params(1),
    )(q, k, v, g, beta, states, inverses, do)


CONV_W = 1024
HALO = 8


def _shift_down(cur, prev, s):
    sh = pltpu.roll(cur, s, 0)
    ph = pltpu.roll(prev, s, 0)
    r = lax.broadcasted_iota(jnp.int32, (HALO, cur.shape[1]), 0)
    return jnp.concatenate([jnp.where(r < s, ph, sh[:HALO]), sh[HALO:]], axis=0)


def _shift_up(cur, nxt, s):
    tm = cur.shape[0]
    sh = pltpu.roll(cur, tm - s, 0)
    nh = pltpu.roll(nxt, HALO - s, 0)
    r = lax.broadcasted_iota(jnp.int32, (HALO, cur.shape[1]), 0)
    return jnp.concatenate([sh[:tm - HALO], jnp.where(r >= HALO - s, nh, sh[tm - HALO:])], axis=0)


def _conv_fwd(name, proj, w, tm=256):
    t = proj.shape[0]
    tm = min(tm, t)
    width = w.shape[1]
    per = tm // HALO

    def body(cur_ref, prev_ref, w_ref, y_ref):
        cur = cur_ref[...]
        prev = jnp.where(pl.program_id(0) > 0, prev_ref[...], 0.0)
        y = cur * w_ref[3:4, :]
        for s in (1, 2, 3):
            y = y + _shift_down(cur, prev, s) * w_ref[3 - s:4 - s, :]
        y_ref[...] = y

    return pl.pallas_call(
        body, grid=(t // tm, width // CONV_W),
        in_specs=[pl.BlockSpec((tm, CONV_W), lambda i, c: (i, c)),
                  pl.BlockSpec((HALO, CONV_W), lambda i, c: (jnp.maximum(i * per - 1, 0), c)),
                  pl.BlockSpec((HALO, CONV_W), lambda i, c: (0, c))],
        out_specs=pl.BlockSpec((tm, CONV_W), lambda i, c: (i, c)),
        out_shape=jax.ShapeDtypeStruct((t, width), F32), name=name, compiler_params=_params(2),
    )(proj, proj, w)


def _conv_bwd(name, proj, w, dy, tm=256):
    t = proj.shape[0]
    tm = min(tm, t)
    width = w.shape[1]
    per, nt = tm // HALO, t // tm

    def body(cur_ref, prev_ref, w_ref, dy_ref, nxt_ref, du_ref, dw_ref):
        i = pl.program_id(1)
        cur, dy_t = cur_ref[...], dy_ref[...]
        prev = jnp.where(i > 0, prev_ref[...], 0.0)
        nxt = jnp.where(i < nt - 1, nxt_ref[...], 0.0)
        du = dy_t * w_ref[3:4, :]
        rows = [jnp.sum(dy_t * cur, axis=0, keepdims=True)]
        for s in (1, 2, 3):
            du = du + _shift_up(dy_t, nxt, s) * w_ref[3 - s:4 - s, :]
            rows.insert(0, jnp.sum(dy_t * _shift_down(cur, prev, s), axis=0, keepdims=True))
        du_ref[...] = du.astype(du_ref.dtype)

        @pl.when(i == 0)
        def _():
            dw_ref[...] = jnp.zeros_like(dw_ref)

        dw_ref[...] += jnp.concatenate(rows + [jnp.zeros((HALO - 4, CONV_W), F32)], axis=0)

    return pl.pallas_call(
        body, grid=(width // CONV_W, nt),
        in_specs=[pl.BlockSpec((tm, CONV_W), lambda c, i: (i, c)),
                  pl.BlockSpec((HALO, CONV_W), lambda c, i: (jnp.maximum(i * per - 1, 0), c)),
                  pl.BlockSpec((HALO, CONV_W), lambda c, i: (0, c)),
                  pl.BlockSpec((tm, CONV_W), lambda c, i: (i, c)),
                  pl.BlockSpec((HALO, CONV_W), lambda c, i: (jnp.minimum((i + 1) * per, t // HALO - 1), c))],
        out_specs=[pl.BlockSpec((tm, CONV_W), lambda c, i: (i, c)), pl.BlockSpec((HALO, CONV_W), lambda c, i: (0, c))],
        out_shape=[jax.ShapeDtypeStruct((t, width), BF16), jax.ShapeDtypeStruct((HALO, width), F32)],
        name=name, compiler_params=_params(2),
    )(proj, proj, w, dy, dy)


def _norm_fn(x, g):
    return _rms(x, g)


def _swiglu_fn(gu):
    return _silu(gu[:, :FFN_HIDDEN]) * gu[:, FFN_HIDDEN:]


def _heads(x):
    return [x[:, h * HEAD:(h + 1) * HEAD] for h in range(x.shape[1] // HEAD)]


def _dn_pre_fn(c, ab, a_log, dt_bias):
    w = N_HEADS * HEAD
    q = [_l2(_silu(x)) * (HEAD ** -0.5) for x in _heads(c[:, :w])]
    k = [_l2(_silu(x)) for x in _heads(c[:, w:2 * w])]
    v = _silu(c[:, 2 * w:])
    g, beta = [], []
    for h in range(N_HEADS):
        gh = -jnp.exp(a_log[:, h:h + 1]) * jax.nn.softplus(ab[:, h:h + 1] + dt_bias[:, h:h + 1])
        bh = jax.nn.sigmoid(ab[:, N_HEADS + h:N_HEADS + h + 1])
        g.append(jnp.broadcast_to(gh, (c.shape[0], HEAD)))
        beta.append(jnp.broadcast_to(bh, (c.shape[0], HEAD)))
    cat = lambda xs: jnp.concatenate(xs, axis=1)
    return cat(q), cat(k), v, cat(g), cat(beta)


def _dn_post_fn(o, z, out_norm):
    return jnp.concatenate([_rms(oh, out_norm) * _silu(zh) for oh, zh in zip(_heads(o), _heads(z))], axis=1)


def _sb_pre_fn(qkv, q_norm, k_norm):
    w = N_HEADS * HEAD
    q = [_rms(x, q_norm) * (HEAD ** -0.5) for x in _heads(qkv[:, :w])]
    k = [_rms(x, k_norm) for x in _heads(qkv[:, w:2 * w])]
    return jnp.concatenate(q, axis=1), jnp.concatenate(k, axis=1), qkv[:, 2 * w:]


def _mla_a_fn(down, cos, sin_lo, sin_hi, q_a_norm, kv_a_norm, k_rope_norm):
    cq = _rms(down[:, :256], q_a_norm)
    ckv = _rms(down[:, 256:384], kv_a_norm)
    kr = _rope(_rms(down[:, 384:], k_rope_norm, MLA_ROPE), cos, sin_lo, sin_hi)
    return cq, ckv, kr


def _mla_b_fn(qf, kvf, cos, sin_lo, sin_hi, q_nope_norm, q_rope_norm, k_nope_norm):
    scale = MLA_QK ** -0.5
    qn, qr, kn, v = [], [], [], []
    for h in range(N_HEADS):
        a = 2 * h * HEAD
        qn.append(_rms(qf[:, a:a + HEAD], q_nope_norm) * scale)
        qr.append(_rope(_rms(qf[:, a + HEAD:a + 2 * HEAD], q_rope_norm, MLA_ROPE), cos, sin_lo, sin_hi) * scale)
        kn.append(_rms(kvf[:, a:a + HEAD], k_nope_norm))
        v.append(kvf[:, a + HEAD:a + 2 * HEAD])
    cat = lambda xs: jnp.concatenate(xs, axis=1)
    return cat(qn), cat(qr), cat(kn), cat(v)


def _rope_tables(t):
    inv_freq = ROPE_THETA ** (-jnp.arange(0, MLA_ROPE, 2, dtype=F32) / MLA_ROPE)
    ang = jnp.arange(t, dtype=F32)[:, None] * inv_freq[None, :]
    cos, sin, zero = jnp.cos(ang), jnp.sin(ang), jnp.zeros((t, MLA_ROPE // 2), F32)
    cat = lambda xs: jnp.concatenate(xs, axis=1)
    return cat([cos, cos, zero, zero]), cat([-sin, zero, zero, zero]), cat([zero, sin, zero, zero])


def _row(v, width=None):
    width = v.shape[0] if width is None else width
    return jnp.pad(v.astype(F32), (0, width - v.shape[0])).reshape(1, width)


def _loss_kernel(y, target):
    t, d = y.shape
    tm = min(256, t)

    def body(y_ref, t_ref, part_ref, dy_ref):
        e = y_ref[...] - t_ref[...]
        dy_ref[...] = e * (1.0 / d)

        @pl.when(pl.program_id(0) == 0)
        def _():
            part_ref[...] = jnp.zeros_like(part_ref)

        part_ref[...] += jnp.sum(e * e, axis=0, keepdims=True)

    blk = pl.BlockSpec((tm, d), lambda i: (i, 0))
    one = pl.BlockSpec((1, d), lambda i: (0, 0))
    return pl.pallas_call(body, grid=(t // tm,), in_specs=[blk, blk], out_specs=[one, blk],
                          out_shape=[jax.ShapeDtypeStruct((1, d), F32), jax.ShapeDtypeStruct((t, d), F32)],
                          name="loss", compiler_params=_params(1))(y, target)


def _ffn_fwd(p, x, w, sm):
    h, = _ew(p + "ffn_norm", _norm_fn, [x], [sm["ffn_norm"]], [(D_MODEL, BF16)])
    gu = _mm(p + "ffn_gu", h, w["ffn_w_gate_up"], "nn")
    act, = _ew(p + "ffn_act", _swiglu_fn, [gu], [], [(FFN_HIDDEN, BF16)], tm=128)
    y = _mm(p + "ffn_down", act, w["ffn_w_down"], "nn", add=x)
    return y, (x, h, gu, act)


def _ffn_bwd(p, saved, dy, w, sm, grads):
    x, h, gu, act = saved
    dact = _mm(p + "ffn_down_dx", dy, w["ffn_w_down"], "nt")
    grads["ffn_w_down"] = _mm(p + "ffn_down_dw", act, dy, "tn")
    (dgu,), _ = _ew_bwd(p + "ffn_act_bwd", _swiglu_fn, [gu], [], [dact], tm=128, ct_dtypes=[BF16])
    dh = _mm(p + "ffn_gu_dx", dgu, w["ffn_w_gate_up"], "nt")
    grads["ffn_w_gate_up"] = _mm(p + "ffn_gu_dw", h, dgu, "tn")
    (dx,), (dg,) = _ew_bwd(p + "ffn_norm_bwd", _norm_fn, [x], [sm["ffn_norm"]], [dh], add=dy)
    grads["ffn_norm"] = dg
    return dx


def _dn_layer_fwd(p, x, w, sm):
    h, = _ew(p + "mix_norm", _norm_fn, [x], [sm["mix_norm"]], [(D_MODEL, BF16)])
    proj = _mm(p + "dn_in", h, w["dn_w_in"], "nn")
    conv = _conv_fwd(p + "dn_conv", proj, sm["dn_conv_w"])
    ab = _V(proj, LANES, 4 * N_HEADS)
    wide = N_HEADS * HEAD
    q, k, v, g, beta = _ew(p + "dn_pre", _dn_pre_fn, [conv, ab], [sm["dn_a_log"], sm["dn_dt_bias"]], [(wide, F32)] * 5)
    o, states, inverses = _dn_fwd(p + "dn_core", q, k, v, g, beta)
    z = _V(proj, wide, 3)
    on, = _ew(p + "dn_post", _dn_post_fn, [o, z], [sm["dn_out_norm"]], [(wide, BF16)])
    y = _mm(p + "dn_out", on, w["dn_w_out"], "nn", add=x)
    return y, (x, h, proj, conv, q, k, v, g, beta, o, states, inverses, on)


def _dn_layer_bwd(p, saved, dy, w, sm, grads):
    x, h, proj, conv, q, k, v, g, beta, o, states, inverses, on = saved
    wide = N_HEADS * HEAD
    don = _mm(p + "dn_out_dx", dy, w["dn_w_out"], "nt")
    grads["dn_w_out"] = _mm(p + "dn_out_dw", on, dy, "tn")
    (do, dz), (d_out_norm,) = _ew_bwd(p + "dn_post_bwd", _dn_post_fn, [o, _V(proj, wide, 3)], [sm["dn_out_norm"]], [don],
                                          ct_dtypes=[F32, BF16])
    grads["dn_out_norm"] = d_out_norm
    dq, dk, dv, dg, db = _dn_bwd(p + "dn_core_bwd", q, k, v, g, beta, states, inverses, do)
    (dconv, dab), (d_a_log, d_dt) = _ew_bwd(p + "dn_pre_bwd", _dn_pre_fn, [conv, _V(proj, LANES, 4 * N_HEADS)],
                                            [sm["dn_a_log"], sm["dn_dt_bias"]], [dq, dk, dv, dg, db], ct_dtypes=[F32, BF16])
    grads["dn_a_log"], grads["dn_dt_bias"] = d_a_log, d_dt
    dqkv, dconv_w = _conv_bwd(p + "dn_conv_bwd", proj, sm["dn_conv_w"], dconv)
    grads["dn_conv_w"] = dconv_w
    dproj = jnp.concatenate([dqkv, dz, dab], axis=1)
    dh = _mm(p + "dn_in_dx", dproj, w["dn_w_in"], "nt")
    grads["dn_w_in"] = _mm(p + "dn_in_dw", h, dproj, "tn")
    (dx,), (dgain,) = _ew_bwd(p + "mix_norm_bwd", _norm_fn, [x], [sm["mix_norm"]], [dh], add=dy)
    grads["mix_norm"] = dgain
    return dx


def _sb_layer_fwd(p, x, w, sm, exchange=None):
    h, = _ew(p + "mix_norm", _norm_fn, [x], [sm["mix_norm"]], [(D_MODEL, BF16)])
    qkv = _mm(p + "sb_qkv", h, w["sb_w_qkv"], "nn")
    wide = N_HEADS * HEAD
    q, k, v = _ew(p + "sb_pre", _sb_pre_fn, [qkv], [sm["sb_q_norm"], sm["sb_k_norm"]], [(wide, BF16)] * 3)
    gather = None if exchange is None else exchange.late_shards()
    o, stat, *gathered = _attn_fwd(p + "sb_core", "sb", [q], [(k, False)], v, gather=gather)
    if exchange is not None:
        exchange.deliver(gathered[0])
    y = _mm(p + "sb_out", o, w["sb_w_out"], "nn", add=x)
    return y, (x, h, qkv, q, k, v, o, stat)


def _sb_layer_bwd(p, saved, dy, w, sm, grads, exchange=None):
    x, h, qkv, q, k, v, o, stat = saved
    do = _mm(p + "sb_out_dx", dy, w["sb_w_out"], "nt")
    grads["sb_w_out"] = _mm(p + "sb_out_dw", o, dy, "tn")
    send = None if exchange is None else exchange.start(grads)
    (dq,), (dk,), dv, *arrived = _attn_bwd(p + "sb_core_bwd", "sb", [q], [(k, False)], v, o, stat, do, send=send)
    if exchange is not None:
        exchange.arrived = arrived[0]
    (dqkv,), (dqn, dkn) = _ew_bwd(p + "sb_pre_bwd", _sb_pre_fn, [qkv], [sm["sb_q_norm"], sm["sb_k_norm"]], [dq, dk, dv],
                                  ct_dtypes=[BF16])
    grads["sb_q_norm"], grads["sb_k_norm"] = dqn, dkn
    dh = _mm(p + "sb_qkv_dx", dqkv, w["sb_w_qkv"], "nt")
    grads["sb_w_qkv"] = _mm(p + "sb_qkv_dw", h, dqkv, "tn")
    (dx,), (dgain,) = _ew_bwd(p + "mix_norm_bwd", _norm_fn, [x], [sm["mix_norm"]], [dh], add=dy)
    grads["mix_norm"] = dgain
    return dx


def _mla_layer_fwd(p, x, w, sm):
    t = x.shape[0]
    tabs = [_V(a, diff=False) for a in _rope_tables(t)]
    h, = _ew(p + "mix_norm", _norm_fn, [x], [sm["mix_norm"]], [(D_MODEL, BF16)])
    down = _mm(p + "mla_down", h, w["mla_w_down"], "nn")
    sm_a = [sm["mla_q_a_norm"], sm["mla_kv_a_norm"], sm["mla_k_rope_norm"]]
    cq, ckv, kr = _ew(p + "mla_a", _mla_a_fn, [down] + tabs, sm_a, [(256, BF16), (128, BF16), (128, BF16)])
    qf = _mm(p + "mla_uq", cq, w["mla_w_uq"], "nn")
    kvf = _mm(p + "mla_ukv", ckv, w["mla_w_ukv"], "nn")
    sm_b = [sm["mla_q_nope_norm"], sm["mla_q_rope_norm"], sm["mla_k_nope_norm"]]
    wide = N_HEADS * HEAD
    qn, qr, kn, v = _ew(p + "mla_b", _mla_b_fn, [qf, kvf] + tabs, sm_b, [(wide, BF16)] * 4)
    o, stat = _attn_fwd(p + "mla_core", "softmax", [qn, qr], [(kn, False), (kr, True)], v)
    y = _mm(p + "mla_out", o, w["mla_w_out"], "nn", add=x)
    return y, (x, h, down, cq, ckv, kr, qf, kvf, qn, qr, kn, v, o, stat)


def _mla_layer_bwd(p, saved, dy, w, sm, grads):
    x, h, down, cq, ckv, kr, qf, kvf, qn, qr, kn, v, o, stat = saved
    tabs = [_V(a, diff=False) for a in _rope_tables(x.shape[0])]
    do = _mm(p + "mla_out_dx", dy, w["mla_w_out"], "nt")
    grads["mla_w_out"] = _mm(p + "mla_out_dw", o, dy, "tn")
    (dqn, dqr), (dkn, dkr), dv = _attn_bwd(p + "mla_core_bwd", "softmax", [qn, qr], [(kn, False), (kr, True)],
                                           v, o, stat, do)
    sm_b = [sm["mla_q_nope_norm"], sm["mla_q_rope_norm"], sm["mla_k_nope_norm"]]
    (dqf, dkvf), dsm_b = _ew_bwd(p + "mla_b_bwd", _mla_b_fn, [qf, kvf] + tabs, sm_b, [dqn, dqr, dkn, dv],
                                 ct_dtypes=[BF16, BF16])
    grads["mla_q_nope_norm"], grads["mla_q_rope_norm"], grads["mla_k_nope_norm"] = dsm_b
    dcq = _mm(p + "mla_uq_dx", dqf, w["mla_w_uq"], "nt")
    grads["mla_w_uq"] = _mm(p + "mla_uq_dw", cq, dqf, "tn")
    dckv = _mm(p + "mla_ukv_dx", dkvf, w["mla_w_ukv"], "nt")
    grads["mla_w_ukv"] = _mm(p + "mla_ukv_dw", ckv, dkvf, "tn")
    sm_a = [sm["mla_q_a_norm"], sm["mla_kv_a_norm"], sm["mla_k_rope_norm"]]
    (ddown,), dsm_a = _ew_bwd(p + "mla_a_bwd", _mla_a_fn, [down] + tabs, sm_a, [dcq, dckv, dkr], ct_dtypes=[BF16])
    grads["mla_q_a_norm"], grads["mla_kv_a_norm"], grads["mla_k_rope_norm"] = dsm_a
    dh = _mm(p + "mla_down_dx", ddown, w["mla_w_down"], "nt")
    grads["mla_w_down"] = _mm(p + "mla_down_dw", h, ddown, "tn")
    (dx,), (dgain,) = _ew_bwd(p + "mix_norm_bwd", _norm_fn, [x], [sm["mix_norm"]], [dh], add=dy)
    grads["mix_norm"] = dgain
    return dx


_MIX_FWD = (_dn_layer_fwd, _sb_layer_fwd, _mla_layer_fwd)
_MIX_BWD = (_dn_layer_bwd, _sb_layer_bwd, _mla_layer_bwd)


def _pad_cols(a, n):
    return jnp.pad(a, ((0, 0), (0, n - a.shape[1])))


def _prep_big(name, a):
    if name.endswith("dn_w_in"):
        return _pad_cols(a, 4 * N_HEADS * HEAD + LANES)
    if name.endswith("mla_w_down"):
        return _pad_cols(a, 512)
    if name.endswith("mla_w_uq"):
        a3 = a.reshape(a.shape[0], N_HEADS, MLA_QK)
        return jnp.pad(a3, ((0, 0), (0, 0), (0, 2 * HEAD - MLA_QK))).reshape(a.shape[0], N_HEADS * 2 * HEAD)
    return a


def _unprep_big(name, g):
    if name.endswith("dn_w_in"):
        return g[:, :4 * N_HEADS * HEAD + 2 * N_HEADS]
    if name.endswith("mla_w_down"):
        return g[:, :448]
    if name.endswith("mla_w_uq"):
        return g.reshape(g.shape[0], N_HEADS, 2 * HEAD)[:, :, :MLA_QK].reshape(g.shape[0], N_HEADS * MLA_QK)
    return g


def _prep_small(name, a):
    if name.endswith("dn_conv_w"):
        return jnp.pad(a.astype(F32), ((0, HALO - a.shape[0]), (0, 0)))
    if name.endswith(("dn_a_log", "dn_dt_bias", "mla_q_rope_norm", "mla_k_rope_norm")):
        return _row(a, LANES)
    return _row(a)


def _unprep_small(name, g, like):
    if name.endswith("dn_conv_w"):
        return g[:like.shape[0]]
    return g.reshape(-1)[:like.shape[0]]


EARLY_LAYER = 1


def local_step(x, target, big, small, exchange=None):
    layers, saved = [], []
    for i in range(DEPTH):
        p = "l%d_" % i
        names = _MIXERS[i % 3] + ("ffn_w_gate_up", "ffn_w_down", "mix_norm", "ffn_norm")
        sm = {n: _prep_small(n, small[p + n]) for n in names if p + n in small}
        mixer_w = {n: _prep_big(n, big[p + n]) for n in _MIXERS[i % 3] if p + n in big}
        if exchange is not None and i == EARLY_LAYER:
            x, s_mix = _MIX_FWD[i % 3](p, x, mixer_w, sm, exchange)
        else:
            x, s_mix = _MIX_FWD[i % 3](p, x, mixer_w, sm)
        w = dict(mixer_w, **{n: big[p + n] for n in ("ffn_w_gate_up", "ffn_w_down")})
        x, s_ffn = _ffn_fwd(p, x, w, sm)
        layers.append((p, w, sm))
        saved.append((s_mix, s_ffn))
    part, dx = _loss_kernel(x, target)
    grads = {}
    for i in reversed(range(DEPTH)):
        p, w, sm = layers[i]
        g = {}
        dx = _ffn_bwd(p, saved[i][1], dx, w, sm, g)
        if exchange is not None and i == EARLY_LAYER:
            exchange.grads = grads
            dx = _MIX_BWD[i % 3](p, saved[i][0], dx, w, sm, g, exchange)
        else:
            dx = _MIX_BWD[i % 3](p, saved[i][0], dx, w, sm, g)
        for n, val in g.items():
            grads[p + n] = _unprep_big(p + n, val) if p + n in big else _unprep_small(p + n, val, small[p + n])
    return part, dx, grads


ROW = 1024
BIG_ROWS = 1024


PACK_ALIGN = 16


def _packed_rows(shape):
    return -(-math.prod(shape) // (ROW * PACK_ALIGN)) * PACK_ALIGN


def _as_rows(a, lead=()):
    rows = _packed_rows(a.shape[len(lead):])
    flat = a.reshape(lead + (-1,))
    return jnp.pad(flat, ((0, 0),) * len(lead) + ((0, rows * ROW - flat.shape[-1]),)).reshape(lead + (rows, ROW))


def _pack(arrs, rows_multiple, lead=()):
    blocks = [_as_rows(a, lead) for a in arrs]
    used = sum(b.shape[-2] for b in blocks)
    fill = -(-used // rows_multiple) * rows_multiple - used
    if fill:
        blocks.append(jnp.zeros(lead + (fill, ROW), blocks[0].dtype))
    return jnp.concatenate(blocks, axis=len(lead))


def _unpack(buf, shapes, lead=()):
    out, r0 = [], 0
    for s in shapes:
        rows, n = _packed_rows(s), math.prod(s)
        block = lax.slice_in_dim(buf, r0, r0 + rows, axis=len(lead))
        out.append(block.reshape(lead + (-1,))[..., :n].reshape(lead + tuple(s)))
        r0 += rows
    return out


def _me():
    return lax.axis_index("x"), lax.axis_index("y"), lax.axis_index("c")


def _other_chips(x, y):
    return [(1 - x, y), (x, 1 - y), (1 - x, 1 - y)]


HBM = pl.BlockSpec(memory_space=pl.ANY)


OWN_STREAMS = 4


def _gather_phases(x_ref, out_ref, send_sems, recv_sems):
    rows = x_ref.shape[0]
    half = rows // 2
    x, y, c = _me()
    sibling, chips = (x, y, 1 - c), _other_chips(x, y)

    def part(px, py, pc):
        return out_ref.at[2 * px + py, pl.ds(pl.multiple_of(pc * half, 16), half), :]

    def copy(k, block, to, src=None):
        return pltpu.make_async_remote_copy(
            src_ref=part(*block) if src is None else src, dst_ref=part(*block),
            send_sem=send_sems.at[k], recv_sem=recv_sems.at[k], device_id=to, device_id_type=MESH)

    my_half = x_ref.at[pl.ds(pl.multiple_of(c * half, 16), half), :]
    first = [copy(j, (x, y, c), (*chip, c), src=my_half) for j, chip in enumerate(chips)]
    piece = rows // OWN_STREAMS
    for p in range(OWN_STREAMS):
        rows_p = pl.ds(p * piece, piece)
        first.append(pltpu.make_async_remote_copy(
            src_ref=x_ref.at[rows_p, :], dst_ref=out_ref.at[2 * x + y, rows_p, :], send_sem=send_sems.at[6 + p],
            recv_sem=recv_sems.at[6 + p], device_id=sibling, device_id_type=MESH))
    passed = [copy(3 + j, (*chip, c), sibling) for j, chip in enumerate(chips)]

    def start():
        for cp in first:
            cp.start()

    def relay():
        for j, chip in enumerate(chips):
            copy(j, (*chip, c), (x, y, c)).wait_recv()
            passed[j].start()

    def finish():
        for j, chip in enumerate(chips):
            copy(3 + j, (*chip, 1 - c), (x, y, c)).wait_recv()
        for cp in first[3:]:
            cp.wait_recv()
        for cp in first + passed:
            cp.wait_send()

    return start, relay, finish


GATHER_SEMS = 6 + OWN_STREAMS


def _gather_shards(packed):
    def body(x_ref, out_ref, send_sems, recv_sems):
        for phase in _gather_phases(x_ref, out_ref, send_sems, recv_sems):
            phase()

    return pl.pallas_call(
        body, out_shape=jax.ShapeDtypeStruct((N_CHIPS,) + packed.shape, packed.dtype), in_specs=[HBM], out_specs=HBM,
        scratch_shapes=[pltpu.SemaphoreType.DMA((GATHER_SEMS,)), pltpu.SemaphoreType.DMA((GATHER_SEMS,))],
        name="gather_weights",
    )(packed)


D2D_STREAMS = 16


def _swap_halves(g, tag):
    n, rows, _ = g.shape
    half = rows // 2
    per = D2D_STREAMS // n
    piece = half // per

    def body(g_ref, theirs_ref, send_sems, recv_sems):
        x, y, c = _me()
        give = (1 - c) * half
        copies = []
        for j in range(n):
            for p in range(per):
                k = j * per + p
                cp = pltpu.make_async_remote_copy(
                    src_ref=g_ref.at[j, pl.ds(pl.multiple_of(give + p * piece, 8), piece), :],
                    dst_ref=theirs_ref.at[j, pl.ds(p * piece, piece), :],
                    send_sem=send_sems.at[k], recv_sem=recv_sems.at[k], device_id=(x, y, 1 - c), device_id_type=MESH)
                cp.start()
                copies.append(cp)
        for cp in copies:
            cp.wait()

    return pl.pallas_call(
        body, out_shape=jax.ShapeDtypeStruct((n, half, ROW), g.dtype), in_specs=[HBM], out_specs=HBM,
        scratch_shapes=[pltpu.SemaphoreType.DMA((D2D_STREAMS,)), pltpu.SemaphoreType.DMA((D2D_STREAMS,))],
        name="grad_swap_halves" + tag,
    )(g)


def _pair_sum(g, theirs, c, tag):
    n, half, _ = theirs.shape
    tm = _tile(half)
    nb = half // tm

    def body(c_ref, g_ref, t_ref, s32_ref, s16_ref):
        s = g_ref[...] + t_ref[...]
        s32_ref[...] = s
        s16_ref[...] = s.astype(BF16)

    blk = pl.BlockSpec((None, tm, ROW), lambda j, i, c_ref: (j, i, 0))
    return pl.pallas_call(
        body,
        grid_spec=pltpu.PrefetchScalarGridSpec(
            num_scalar_prefetch=1, grid=(n, nb),
            in_specs=[pl.BlockSpec((None, tm, ROW), lambda j, i, c_ref: (j, c_ref[0] * nb + i, 0)), blk],
            out_specs=[blk, blk]),
        out_shape=[jax.ShapeDtypeStruct(theirs.shape, F32), jax.ShapeDtypeStruct(theirs.shape, BF16)],
        name="grad_pair_sum" + tag, compiler_params=_params(2),
    )(c.reshape(1).astype(jnp.int32), g, theirs)


def _scatter_chunks(s16):
    _, half, _ = s16.shape

    def body(s16_ref, got_ref, send_sems, recv_sems):
        x, y, c = _me()
        sends = []
        for j, (px, py) in enumerate(_other_chips(x, y)):
            cp = pltpu.make_async_remote_copy(src_ref=s16_ref.at[2 * px + py], dst_ref=got_ref.at[j],
                                              send_sem=send_sems.at[j], recv_sem=recv_sems.at[j],
                                              device_id=(px, py, c), device_id_type=MESH)
            cp.start()
            sends.append(cp)
        for cp in sends:
            cp.wait()

    return pl.pallas_call(
        body, out_shape=jax.ShapeDtypeStruct((3, half, ROW), BF16), in_specs=[HBM], out_specs=HBM,
        scratch_shapes=[pltpu.SemaphoreType.DMA((3,)), pltpu.SemaphoreType.DMA((3,))],
        name="grad_scatter",
    )(s16)


def _chip_sum(s32, got, chip, c, tag):
    _, half, _ = s32.shape
    tm = _tile(half)
    nb = half // tm

    def body(where_ref, own_ref, g0_ref, g1_ref, g2_ref, o_ref):
        o_ref[...] = ((own_ref[...] + g0_ref[...].astype(F32)) + g1_ref[...].astype(F32)) + g2_ref[...].astype(F32)

    got_spec = lambda k: pl.BlockSpec((None, tm, ROW), lambda i, where_ref, k=k: (k, i, 0))
    return pl.pallas_call(
        body,
        grid_spec=pltpu.PrefetchScalarGridSpec(
            num_scalar_prefetch=1, grid=(nb,),
            in_specs=[pl.BlockSpec((None, tm, ROW), lambda i, where_ref: (where_ref[0], i, 0)), got_spec(0), got_spec(1), got_spec(2)],
            out_specs=pl.BlockSpec((tm, ROW), lambda i, where_ref: (where_ref[1] * nb + i, 0))),
        out_shape=jax.ShapeDtypeStruct((2 * half, ROW), F32), name="grad_chip_sum" + tag, compiler_params=_params(1),
    )(jnp.stack([chip, c]).astype(jnp.int32), s32, got, got, got)


def _join_halves(f, tag):
    half = f.shape[0] // 2
    piece = half // D2D_STREAMS

    def body(f_ref, out_ref, send_sems, recv_sems):
        x, y, c = _me()
        copies = []
        for p in range(D2D_STREAMS):
            rows = out_ref.at[pl.ds(pl.multiple_of(c * half + p * piece, 8), piece), :]
            cp = pltpu.make_async_remote_copy(src_ref=rows, dst_ref=rows, send_sem=send_sems.at[p], recv_sem=recv_sems.at[p],
                                              device_id=(x, y, 1 - c), device_id_type=MESH)
            cp.start()
            copies.append(cp)
        for cp in copies:
            cp.wait()

    return pl.pallas_call(
        body, out_shape=jax.ShapeDtypeStruct(f.shape, F32), in_specs=[HBM], out_specs=HBM, input_output_aliases={0: 0},
        scratch_shapes=[pltpu.SemaphoreType.DMA((D2D_STREAMS,)), pltpu.SemaphoreType.DMA((D2D_STREAMS,))],
        name="grad_join_halves" + tag,
    )(f)


def _all_reduce_small(name, v):
    rows = v.shape[0]

    def body(v_ref, out_ref, slots, send_sems, recv_sems):
        x, y, c = _me()
        me = 4 * x + 2 * y + c
        slots[me] = v_ref[...]
        sends = []
        for r in range(1, 8):
            to = (x ^ (r >> 2), y ^ ((r >> 1) & 1), c ^ (r & 1))
            cp = pltpu.make_async_remote_copy(src_ref=v_ref, dst_ref=slots.at[me], send_sem=send_sems.at[r - 1],
                                              recv_sem=recv_sems.at[r - 1], device_id=to, device_id_type=MESH)
            cp.start()
            sends.append(cp)
        for cp in sends:
            cp.wait()
        total = slots[0]
        for d in range(1, 8):
            total = total + slots[d]
        out_ref[...] = total

    vmem = pl.BlockSpec(memory_space=pltpu.VMEM)
    return pl.pallas_call(
        body, out_shape=jax.ShapeDtypeStruct(v.shape, F32), in_specs=[vmem], out_specs=vmem,
        scratch_shapes=[pltpu.VMEM((8, rows, ROW), F32), pltpu.SemaphoreType.DMA((7,)), pltpu.SemaphoreType.DMA((7,))],
        name=name,
    )(v)


def _adam_fn(w, g, m, v):
    m2 = ADAM_B1 * m + (1.0 - ADAM_B1) * g
    v2 = ADAM_B2 * v + (1.0 - ADAM_B2) * (g * g)
    m_hat = m2 / (1.0 - ADAM_B1 ** ADAM_STEP)
    v_hat = v2 / (1.0 - ADAM_B2 ** ADAM_STEP)
    return -ADAM_LR * (m_hat / (jnp.sqrt(v_hat) + ADAM_EPS) + ADAM_WD * w), m2, v2


def _full_shape(name, shard_shape):
    ax = _shard_axis(name)
    return tuple(n * N_CHIPS if k == ax else n for k, n in enumerate(shard_shape))


def _chip_major(name, full):
    if _shard_axis(name) == 0:
        return full.reshape(N_CHIPS, -1, full.shape[1])
    n = full.shape[1] // N_CHIPS
    return jnp.stack([full[:, j * n:(j + 1) * n] for j in range(N_CHIPS)])


def _from_chip_major(name, shards):
    if _shard_axis(name) == 0:
        return shards.reshape(-1, shards.shape[2])
    return jnp.concatenate([shards[j] for j in range(N_CHIPS)], axis=1)


def _row_tile(rows, cap=512):
    return max(t for t in range(8, min(rows, cap) + 1, 8) if rows % t == 0)


def _step(a):
    x_i, y_i, c_i = _me()
    chip = 2 * x_i + y_i
    def packed_shards(names):
        return _pack([a[n].astype(BF16) for n in names], BIG_ROWS)

    def full_matrices(names, gathered):
        shards = _unpack(gathered, [a[n].shape for n in names], lead=(N_CHIPS,))
        return {n: _from_chip_major(n, sh) for n, sh in zip(names, shards)}

    first_w = [n for n in BIG if int(n[1]) <= EARLY_LAYER]
    later_w = [n for n in BIG if n not in first_w]
    big = full_matrices(first_w, _gather_shards(packed_shards(first_w)))

    small = {n: a[n] for n in SMALL}
    convs = [n for n in SMALL if n.endswith("dn_conv_w")]
    placed = []
    for n in convs:
        full = jnp.zeros(_full_shape_conv(a[n].shape), F32)
        placed.append(lax.dynamic_update_slice(full, a[n], (0, chip * a[n].shape[1])))
    conv_sum = _all_reduce_small("gather_conv", _pack(placed, 8))
    for n, full in zip(convs, _unpack(conv_sum, [p.shape for p in placed])):
        small[n] = full * 0.5

    def pair_sums(names, grads, tag):
        g_all = _pack([_chip_major(n, grads[n]) for n in names], BIG_ROWS, lead=(N_CHIPS,))
        return _pair_sum(g_all, _swap_halves(g_all, tag), c_i, tag)

    def reduced(names, s32, got, tag):
        whole = _join_halves(_chip_sum(s32, got, chip, c_i, tag), tag)
        return dict(zip(names, _unpack(whole, [a[n].shape for n in names])))

    early_prefix = "l%d_" % EARLY_LAYER
    early = [n for n in BIG if int(n[1]) > EARLY_LAYER or n in (early_prefix + "ffn_w_gate_up", early_prefix + "ffn_w_down",
                                                                early_prefix + "sb_w_out")]
    late = [n for n in BIG if n not in early]

    class Exchange:
        def late_shards(self):
            return packed_shards(later_w)

        def deliver(self, gathered):
            big.update(full_matrices(later_w, gathered))

        def start(self, layer_grads):
            have = dict(self.grads, **{early_prefix + k: val for k, val in layer_grads.items()})
            self.s32, s16 = pair_sums(early, have, "_early")
            return s16

    exchange = Exchange()
    part, dx, grads = local_step(a["x"][0], a["loss_target"][0], big, small, exchange)
    loss = lax.psum(0.5 * jnp.sum(part) / D_MODEL, ("x", "y", "c"))
    g_big = reduced(early, exchange.s32, exchange.arrived, "_early")
    s32, s16 = pair_sums(late, grads, "_late")
    g_big.update(reduced(late, s32, _scatter_chunks(s16), "_late"))

    g_small_full = _all_reduce_small("reduce_small", _pack([grads[n] for n in SMALL], 8))
    g_small = dict(zip(SMALL, _unpack(g_small_full, [grads[n].shape for n in SMALL])))
    for n in convs:
        g_small[n] = lax.dynamic_slice_in_dim(g_small[n], chip * a[n].shape[1], a[n].shape[1], axis=1)

    outs = {}
    for n in BIG:
        d, m2, v2 = _ew("adam_" + n, _adam_fn, [a[n], g_big[n], a["m_" + n], a["v_" + n]], [],
                        [(a[n].shape[1], F32)] * 3, tm=_row_tile(a[n].shape[0]))
        outs.update({"grad_" + n: g_big[n], "delta_" + n: d, "new_m_" + n: m2, "new_v_" + n: v2})
    pk = lambda prefix: _pack([a[prefix + n] for n in SMALL], 8)
    gs_packed = _pack([g_small[n] for n in SMALL], 8)
    small_bufs = (gs_packed,) + tuple(_ew("adam_small", _adam_fn, [pk(""), gs_packed, pk("m_"), pk("v_")], [], [(ROW, F32)] * 3,
                                          tm=gs_packed.shape[0]))
    small_shapes = [a[n].shape for n in SMALL]
    for key, buf in zip(("grad_", "delta_", "new_m_", "new_v_"), small_bufs):
        outs.update({key + n: val for n, val in zip(SMALL, _unpack(buf, small_shapes))})
    result = [loss, dx[None]]
    for key in ("grad_", "delta_", "new_m_", "new_v_"):
        result += [outs[key + n] for n in WEIGHTS]
    return tuple(result)


def _full_shape_conv(shard_shape):
    return (shard_shape[0], shard_shape[1] * N_CHIPS)


def kernel(x, l0_mix_norm, l0_dn_w_in, l0_dn_conv_w, l0_dn_a_log, l0_dn_dt_bias, l0_dn_out_norm, l0_dn_w_out, l0_ffn_norm, l0_ffn_w_gate_up, l0_ffn_w_down, l1_mix_norm, l1_sb_w_qkv, l1_sb_q_norm, l1_sb_k_norm, l1_sb_w_out, l1_ffn_norm, l1_ffn_w_gate_up, l1_ffn_w_down, l2_mix_norm, l2_mla_w_down, l2_mla_q_a_norm, l2_mla_kv_a_norm, l2_mla_w_uq, l2_mla_w_ukv, l2_mla_q_nope_norm, l2_mla_q_rope_norm, l2_mla_k_nope_norm, l2_mla_k_rope_norm, l2_mla_w_out, l2_ffn_norm, l2_ffn_w_gate_up, l2_ffn_w_down, l3_mix_norm, l3_dn_w_in, l3_dn_conv_w, l3_dn_a_log, l3_dn_dt_bias, l3_dn_out_norm, l3_dn_w_out, l3_ffn_norm, l3_ffn_w_gate_up, l3_ffn_w_down, loss_target, m_l0_mix_norm, m_l0_dn_w_in, m_l0_dn_conv_w, m_l0_dn_a_log, m_l0_dn_dt_bias, m_l0_dn_out_norm, m_l0_dn_w_out, m_l0_ffn_norm, m_l0_ffn_w_gate_up, m_l0_ffn_w_down, m_l1_mix_norm, m_l1_sb_w_qkv, m_l1_sb_q_norm, m_l1_sb_k_norm, m_l1_sb_w_out, m_l1_ffn_norm, m_l1_ffn_w_gate_up, m_l1_ffn_w_down, m_l2_mix_norm, m_l2_mla_w_down, m_l2_mla_q_a_norm, m_l2_mla_kv_a_norm, m_l2_mla_w_uq, m_l2_mla_w_ukv, m_l2_mla_q_nope_norm, m_l2_mla_q_rope_norm, m_l2_mla_k_nope_norm, m_l2_mla_k_rope_norm, m_l2_mla_w_out, m_l2_ffn_norm, m_l2_ffn_w_gate_up, m_l2_ffn_w_down, m_l3_mix_norm, m_l3_dn_w_in, m_l3_dn_conv_w, m_l3_dn_a_log, m_l3_dn_dt_bias, m_l3_dn_out_norm, m_l3_dn_w_out, m_l3_ffn_norm, m_l3_ffn_w_gate_up, m_l3_ffn_w_down, v_l0_mix_norm, v_l0_dn_w_in, v_l0_dn_conv_w, v_l0_dn_a_log, v_l0_dn_dt_bias, v_l0_dn_out_norm, v_l0_dn_w_out, v_l0_ffn_norm, v_l0_ffn_w_gate_up, v_l0_ffn_w_down, v_l1_mix_norm, v_l1_sb_w_qkv, v_l1_sb_q_norm, v_l1_sb_k_norm, v_l1_sb_w_out, v_l1_ffn_norm, v_l1_ffn_w_gate_up, v_l1_ffn_w_down, v_l2_mix_norm, v_l2_mla_w_down, v_l2_mla_q_a_norm, v_l2_mla_kv_a_norm, v_l2_mla_w_uq, v_l2_mla_w_ukv, v_l2_mla_q_nope_norm, v_l2_mla_q_rope_norm, v_l2_mla_k_nope_norm, v_l2_mla_k_rope_norm, v_l2_mla_w_out, v_l2_ffn_norm, v_l2_ffn_w_gate_up, v_l2_ffn_w_down, v_l3_mix_norm, v_l3_dn_w_in, v_l3_dn_conv_w, v_l3_dn_a_log, v_l3_dn_dt_bias, v_l3_dn_out_norm, v_l3_dn_w_out, v_l3_ffn_norm, v_l3_ffn_w_gate_up, v_l3_ffn_w_down):
    return _step(dict(locals()))
```
